```python
import math
import jax, jax.numpy as jnp
from jax import lax
import numpy as np

D_MODEL = 2048
BATCH = 8
SEQ = 8192
DEPTH = 2

N_MIXERS = 2
N_SSM_LAYERS = (DEPTH + 1) // 2
N_ATTN_LAYERS = DEPTH // 2

SSM_EXPAND = 2
D_INNER = SSM_EXPAND * D_MODEL
SSM_HEAD_DIM = 64
SSM_HEADS = D_INNER // SSM_HEAD_DIM
SSM_GROUPS = 8
SSM_STATE = 128
CONV_WIDTH = 4
CHUNK = 128
CONV_DIM = D_INNER + 2 * SSM_GROUPS * SSM_STATE
IN_PROJ_DIM = D_INNER + CONV_DIM + SSM_HEADS

ATTN_GROUPS = ((128, 1), (512, 4), (2048, 16))
N_ATTN_GROUPS = len(ATTN_GROUPS)
HEADS_PER_GROUP = 16
ATTN_HEAD_DIM = 64
ATTN_WIDTH = HEADS_PER_GROUP * ATTN_HEAD_DIM
QKV_DIM = 3 * N_ATTN_GROUPS * ATTN_WIDTH

D_FF = 4 * D_MODEL
EPS = 1e-5

kernel_name = "hybrid_ssd_dilated_alibi_trunk"


def rms_norm(x, g):
    xf = x.astype(jnp.float32)
    y = xf * lax.rsqrt(jnp.mean(xf * xf, axis=-1, keepdims=True) + EPS)
    return (y * g.astype(jnp.float32)).astype(x.dtype)


def causal_depthwise_conv(u, w, b):
    out = lax.conv_general_dilated(
        u, w[:, None, :].astype(u.dtype), window_strides=(1,),
        padding=[(CONV_WIDTH - 1, 0)],
        dimension_numbers=("NWC", "WIO", "NWC"),
        feature_group_count=u.shape[-1])
    return out + b.astype(u.dtype)


def segsum_exp(a):
    T = a.shape[-1]
    cs = jnp.cumsum(a, axis=-1)
    diff = cs[..., :, None] - cs[..., None, :]
    mask = jnp.tril(jnp.ones((T, T), dtype=bool))
    return jnp.exp(jnp.where(mask, diff, -jnp.inf))


def ssd_chunked(x, a, bm, cm):
    Bsz, S, H, P = x.shape
    G, N = bm.shape[2], bm.shape[3]
    hg = H // G
    nc = S // CHUNK
    x = x.reshape(Bsz, nc, CHUNK, G, hg, P)
    a = a.reshape(Bsz, nc, CHUNK, G, hg).transpose(0, 3, 4, 1, 2)
    bm = bm.reshape(Bsz, nc, CHUNK, G, N)
    cm = cm.reshape(Bsz, nc, CHUNK, G, N)
    a_cs = jnp.cumsum(a, axis=-1)

    L = segsum_exp(a)
    cb = jnp.einsum("bclgn,bcsgn->bcgls", cm, bm)
    y_diag = jnp.einsum("bcgls,bgjcls,bcsgjp->bclgjp", cb, L, x)

    decay_states = jnp.exp(a_cs[..., -1:] - a_cs)
    states = jnp.einsum("bclgn,bgjcl,bclgjp->bcgjpn", bm, decay_states, x)
    chunk_decay = jnp.exp(a_cs[..., -1])

    def step(h, inp):
        st, dec = inp
        return h * dec[..., None, None] + st, h

    h0 = jnp.zeros((Bsz, G, hg, P, N), dtype=x.dtype)
    _, prev = lax.scan(step, h0, (states.transpose(1, 0, 2, 3, 4, 5),
                                  chunk_decay.transpose(3, 0, 1, 2)))
    prev = prev.transpose(1, 0, 2, 3, 4, 5)

    y_off = jnp.einsum("bclgn,bcgjpn,bgjcl->bclgjp", cm, prev, jnp.exp(a_cs))
    return (y_diag + y_off).reshape(Bsz, S, H, P)


def mamba2_mixer(u, w_in, conv_w, conv_b, dt_bias, a_log, d_skip, norm_w, w_out):
    Bsz, S, _ = u.shape
    zxbcdt = u @ w_in
    z = zxbcdt[..., :D_INNER]
    xbc = zxbcdt[..., D_INNER:D_INNER + CONV_DIM]
    dt = zxbcdt[..., D_INNER + CONV_DIM:]
    xbc = jax.nn.silu(causal_depthwise_conv(xbc, conv_w, conv_b))
    gn = SSM_GROUPS * SSM_STATE
    xs = xbc[..., :D_INNER].reshape(Bsz, S, SSM_HEADS, SSM_HEAD_DIM).astype(jnp.float32)
    bm = xbc[..., D_INNER:D_INNER + gn].reshape(Bsz, S, SSM_GROUPS, SSM_STATE).astype(jnp.float32)
    cm = xbc[..., D_INNER + gn:].reshape(Bsz, S, SSM_GROUPS, SSM_STATE).astype(jnp.float32)
    dt = jax.nn.softplus(dt.astype(jnp.float32) + dt_bias.astype(jnp.float32))
    A = -jnp.exp(a_log.astype(jnp.float32))
    y = ssd_chunked(xs * dt[..., None], dt * A, bm, cm)
    y = y + d_skip.astype(jnp.float32)[:, None] * xs
    y = y.reshape(Bsz, S, D_INNER) * jax.nn.silu(z.astype(jnp.float32))
    yg = y.reshape(Bsz, S, SSM_GROUPS, D_INNER // SSM_GROUPS)
    yg = yg * lax.rsqrt(jnp.mean(yg * yg, axis=-1, keepdims=True) + EPS)
    y = yg.reshape(Bsz, S, D_INNER) * norm_w.astype(jnp.float32)
    return y.astype(u.dtype) @ w_out


def alibi_slopes():
    n = N_ATTN_GROUPS * HEADS_PER_GROUP
    i = jnp.arange(1, n + 1, dtype=jnp.float32)
    return jnp.exp2(-8.0 * i / n).reshape(N_ATTN_GROUPS, HEADS_PER_GROUP)


def dilated_window_group(q, k, v, window, dilation, slopes):
    Bsz, S, H, dh = q.shape
    blk = window // dilation
    span = dilation * blk
    S_pad = -(-S // span) * span
    Lsub = S_pad // dilation
    nb = Lsub // blk

    def to_blocks(t):
        t = jnp.pad(t, ((0, 0), (0, S_pad - S), (0, 0), (0, 0)))
        t = t.reshape(Bsz, Lsub, dilation, H, dh).transpose(0, 2, 1, 3, 4)
        return t.reshape(Bsz, dilation, nb, blk, H, dh)

    def with_prev(t):
        prev = jnp.pad(t[:, :, :-1], ((0, 0), (0, 0), (1, 0), (0, 0), (0, 0), (0, 0)))
        return jnp.concatenate([prev, t], axis=3)

    qb = to_blocks(q)
    kk = with_prev(to_blocks(k))
    vv = with_prev(to_blocks(v))

    s = jnp.einsum("brnqhd,brnkhd->brnhqk", qb, kk).astype(jnp.float32) * (1.0 / math.sqrt(dh))
    qi = jnp.arange(blk)[:, None]
    kj = jnp.arange(2 * blk)[None, :]
    dist = qi + blk - kj
    valid = (dist >= 0) & (dist <= blk)
    valid = valid[None] & ((jnp.arange(nb)[:, None, None] > 0) | (kj[None] >= blk))
    bias = -slopes.astype(jnp.float32)[:, None, None] * (dist * dilation).astype(jnp.float32)[None]
    s = s + bias[None, None, None]
    s = jnp.where(valid[None, None, :, None], s, -jnp.inf)
    m = jnp.max(s, axis=-1, keepdims=True)
    p = jnp.exp(s - m)
    den = jnp.sum(p, axis=-1)
    o = jnp.einsum("brnhqk,brnkhd->brnqhd", p, vv.astype(jnp.float32))
    o = o / den.transpose(0, 1, 2, 4, 3)[..., None]
    lse = (m[..., 0] + jnp.log(den)).transpose(0, 1, 2, 4, 3)

    o = o.reshape(Bsz, dilation, Lsub, H, dh).transpose(0, 2, 1, 3, 4).reshape(Bsz, S_pad, H, dh)[:, :S]
    lse = lse.reshape(Bsz, dilation, Lsub, H).transpose(0, 2, 1, 3).reshape(Bsz, S_pad, H)[:, :S]
    return o, lse


def dilated_attention(u, w_qkv, w_o):
    Bsz, S, _ = u.shape
    qkv = (u @ w_qkv).reshape(Bsz, S, 3, N_ATTN_GROUPS, HEADS_PER_GROUP, ATTN_HEAD_DIM)
    slopes = alibi_slopes()
    outs, lses = [], []
    for g, (window, dilation) in enumerate(ATTN_GROUPS):
        o, l = dilated_window_group(qkv[:, :, 0, g], qkv[:, :, 1, g], qkv[:, :, 2, g],
                                    window, dilation, slopes[g])
        outs.append(o)
        lses.append(l)
    w = jax.nn.softmax(jnp.stack(lses, axis=0), axis=0)
    o = jnp.sum(w[..., None] * jnp.stack(outs, axis=0), axis=0)
    return o.reshape(Bsz, S, ATTN_WIDTH).astype(u.dtype) @ w_o


def sq_relu_mlp(h, w1, w2):
    a = jax.nn.relu(h @ w1)
    return (a * a) @ w2


def _fwd_setup_inputs(seed: int = 0) -> dict:
    key = jax.random.key(seed)
    ks = jax.random.split(key, 20)
    f32 = jnp.float32
    nS, nA, L = N_SSM_LAYERS, N_ATTN_LAYERS, DEPTH

    def normal(k, shape, scale):
        return jax.random.normal(k, shape, f32) * scale

    x = jax.random.normal(ks[0], (BATCH, SEQ, D_MODEL), f32)
    norm_mix = 1.0 + normal(ks[1], (L, D_MODEL), 0.02)
    norm_mlp = 1.0 + normal(ks[2], (L, D_MODEL), 0.02)

    ssm_w_in = normal(ks[3], (nS, D_MODEL, IN_PROJ_DIM), D_MODEL ** -0.5)
    ssm_conv_w = normal(ks[4], (nS, CONV_WIDTH, CONV_DIM), CONV_WIDTH ** -0.5)
    ssm_conv_b = normal(ks[5], (nS, CONV_DIM), 0.02)
    dt0 = jnp.exp(jax.random.uniform(ks[6], (nS, SSM_HEADS), f32,
                                     math.log(1e-3), math.log(1e-1)))
    ssm_dt_bias = dt0 + jnp.log(-jnp.expm1(-dt0))
    ssm_a_log = jnp.log(jax.random.uniform(ks[7], (nS, SSM_HEADS), f32, 1.0, 16.0))
    ssm_d = 1.0 + normal(ks[8], (nS, SSM_HEADS), 0.02)
    ssm_norm_w = 1.0 + normal(ks[9], (nS, D_INNER), 0.02)
    ssm_w_out = normal(ks[10], (nS, D_INNER, D_MODEL), D_INNER ** -0.5)

    attn_w_qkv = normal(ks[11], (nA, D_MODEL, QKV_DIM), D_MODEL ** -0.5)
    attn_w_o = normal(ks[12], (nA, ATTN_WIDTH, D_MODEL), ATTN_WIDTH ** -0.5)

    mlp_w1 = normal(ks[13], (L, D_MODEL, D_FF), D_MODEL ** -0.5)
    mlp_w2 = normal(ks[14], (L, D_FF, D_MODEL), D_FF ** -0.5)
    final_norm = 1.0 + normal(ks[15], (D_MODEL,), 0.02)
    return {"x": x, "norm_mix": norm_mix, "norm_mlp": norm_mlp,
            "ssm_w_in": ssm_w_in, "ssm_conv_w": ssm_conv_w, "ssm_conv_b": ssm_conv_b,
            "ssm_dt_bias": ssm_dt_bias, "ssm_a_log": ssm_a_log, "ssm_d": ssm_d,
            "ssm_norm_w": ssm_norm_w, "ssm_w_out": ssm_w_out,
            "attn_w_qkv": attn_w_qkv, "attn_w_o": attn_w_o,
            "mlp_w1": mlp_w1, "mlp_w2": mlp_w2, "final_norm": final_norm}


def _fwd_reference(x, norm_mix, norm_mlp, ssm_w_in, ssm_conv_w, ssm_conv_b, ssm_dt_bias,
              ssm_a_log, ssm_d, ssm_norm_w, ssm_w_out, attn_w_qkv, attn_w_o,
              mlp_w1, mlp_w2, final_norm):
    for i in range(DEPTH):
        h = rms_norm(x, norm_mix[i])
        j = i // N_MIXERS
        if i % N_MIXERS == 0:
            mix = mamba2_mixer(h, ssm_w_in[j], ssm_conv_w[j], ssm_conv_b[j], ssm_dt_bias[j],
                               ssm_a_log[j], ssm_d[j], ssm_norm_w[j], ssm_w_out[j])
        else:
            mix = dilated_attention(h, attn_w_qkv[j], attn_w_o[j])
        x = x + mix
        x = x + sq_relu_mlp(rms_norm(x, norm_mlp[i]), mlp_w1[i], mlp_w2[i])
    return rms_norm(x, final_norm)


import jax as _jax
import jax.numpy as _jnp

TWIN_FORMAT = 'train_step'
FWD_PARAMS = ['x', 'norm_mix', 'norm_mlp', 'ssm_w_in', 'ssm_conv_w', 'ssm_conv_b', 'ssm_dt_bias', 'ssm_a_log', 'ssm_d', 'ssm_norm_w', 'ssm_w_out', 'attn_w_qkv', 'attn_w_o', 'mlp_w1', 'mlp_w2', 'final_norm']
TWIN_WEIGHTS = ['norm_mix', 'norm_mlp', 'ssm_w_in', 'ssm_conv_w', 'ssm_conv_b', 'ssm_dt_bias', 'ssm_a_log', 'ssm_d', 'ssm_norm_w', 'ssm_w_out', 'attn_w_qkv', 'attn_w_o', 'mlp_w1', 'mlp_w2', 'final_norm']
TWIN_DIFF_INPUT = 'x'
TWIN_INPUTS = ['x', 'norm_mix', 'norm_mlp', 'ssm_w_in', 'ssm_conv_w', 'ssm_conv_b', 'ssm_dt_bias', 'ssm_a_log', 'ssm_d', 'ssm_norm_w', 'ssm_w_out', 'attn_w_qkv', 'attn_w_o', 'mlp_w1', 'mlp_w2', 'final_norm', 'loss_target', 'm_norm_mix', 'm_norm_mlp', 'm_ssm_w_in', 'm_ssm_conv_w', 'm_ssm_conv_b', 'm_ssm_dt_bias', 'm_ssm_a_log', 'm_ssm_d', 'm_ssm_norm_w', 'm_ssm_w_out', 'm_attn_w_qkv', 'm_attn_w_o', 'm_mlp_w1', 'm_mlp_w2', 'm_final_norm', 'v_norm_mix', 'v_norm_mlp', 'v_ssm_w_in', 'v_ssm_conv_w', 'v_ssm_conv_b', 'v_ssm_dt_bias', 'v_ssm_a_log', 'v_ssm_d', 'v_ssm_norm_w', 'v_ssm_w_out', 'v_attn_w_qkv', 'v_attn_w_o', 'v_mlp_w1', 'v_mlp_w2', 'v_final_norm']
TWIN_OUTPUTS = ['loss', 'grad_x', 'grad_norm_mix', 'grad_norm_mlp', 'grad_ssm_w_in', 'grad_ssm_conv_w', 'grad_ssm_conv_b', 'grad_ssm_dt_bias', 'grad_ssm_a_log', 'grad_ssm_d', 'grad_ssm_norm_w', 'grad_ssm_w_out', 'grad_attn_w_qkv', 'grad_attn_w_o', 'grad_mlp_w1', 'grad_mlp_w2', 'grad_final_norm', 'delta_norm_mix', 'delta_norm_mlp', 'delta_ssm_w_in', 'delta_ssm_conv_w', 'delta_ssm_conv_b', 'delta_ssm_dt_bias', 'delta_ssm_a_log', 'delta_ssm_d', 'delta_ssm_norm_w', 'delta_ssm_w_out', 'delta_attn_w_qkv', 'delta_attn_w_o', 'delta_mlp_w1', 'delta_mlp_w2', 'delta_final_norm', 'new_m_norm_mix', 'new_m_norm_mlp', 'new_m_ssm_w_in', 'new_m_ssm_conv_w', 'new_m_ssm_conv_b', 'new_m_ssm_dt_bias', 'new_m_ssm_a_log', 'new_m_ssm_d', 'new_m_ssm_norm_w', 'new_m_ssm_w_out', 'new_m_attn_w_qkv', 'new_m_attn_w_o', 'new_m_mlp_w1', 'new_m_mlp_w2', 'new_m_final_norm', 'new_v_norm_mix', 'new_v_norm_mlp', 'new_v_ssm_w_in', 'new_v_ssm_conv_w', 'new_v_ssm_conv_b', 'new_v_ssm_dt_bias', 'new_v_ssm_a_log', 'new_v_ssm_d', 'new_v_ssm_norm_w', 'new_v_ssm_w_out', 'new_v_attn_w_qkv', 'new_v_attn_w_o', 'new_v_mlp_w1', 'new_v_mlp_w2', 'new_v_final_norm']
TWIN_LEAF_KINDS = {'loss': 'loss', 'grad_x': 'grad_x', 'grad_norm_mix': 'grad_w', 'grad_norm_mlp': 'grad_w', 'grad_ssm_w_in': 'grad_w', 'grad_ssm_conv_w': 'grad_w', 'grad_ssm_conv_b': 'grad_w', 'grad_ssm_dt_bias': 'grad_w', 'grad_ssm_a_log': 'grad_w', 'grad_ssm_d': 'grad_w', 'grad_ssm_norm_w': 'grad_w', 'grad_ssm_w_out': 'grad_w', 'grad_attn_w_qkv': 'grad_w', 'grad_attn_w_o': 'grad_w', 'grad_mlp_w1': 'grad_w', 'grad_mlp_w2': 'grad_w', 'grad_final_norm': 'grad_w', 'delta_norm_mix': 'delta_w', 'delta_norm_mlp': 'delta_w', 'delta_ssm_w_in': 'delta_w', 'delta_ssm_conv_w': 'delta_w', 'delta_ssm_conv_b': 'delta_w', 'delta_ssm_dt_bias': 'delta_w', 'delta_ssm_a_log': 'delta_w', 'delta_ssm_d': 'delta_w', 'delta_ssm_norm_w': 'delta_w', 'delta_ssm_w_out': 'delta_w', 'delta_attn_w_qkv': 'delta_w', 'delta_attn_w_o': 'delta_w', 'delta_mlp_w1': 'delta_w', 'delta_mlp_w2': 'delta_w', 'delta_final_norm': 'delta_w', 'new_m_norm_mix': 'new_m', 'new_m_norm_mlp': 'new_m', 'new_m_ssm_w_in': 'new_m', 'new_m_ssm_conv_w': 'new_m', 'new_m_ssm_conv_b': 'new_m', 'new_m_ssm_dt_bias': 'new_m', 'new_m_ssm_a_log': 'new_m', 'new_m_ssm_d': 'new_m', 'new_m_ssm_norm_w': 'new_m', 'new_m_ssm_w_out': 'new_m', 'new_m_attn_w_qkv': 'new_m', 'new_m_attn_w_o': 'new_m', 'new_m_mlp_w1': 'new_m', 'new_m_mlp_w2': 'new_m', 'new_m_final_norm': 'new_m', 'new_v_norm_mix': 'new_v', 'new_v_norm_mlp': 'new_v', 'new_v_ssm_w_in': 'new_v', 'new_v_ssm_conv_w': 'new_v', 'new_v_ssm_conv_b': 'new_v', 'new_v_ssm_dt_bias': 'new_v', 'new_v_ssm_a_log': 'new_v', 'new_v_ssm_d': 'new_v', 'new_v_ssm_norm_w': 'new_v', 'new_v_ssm_w_out': 'new_v', 'new_v_attn_w_qkv': 'new_v', 'new_v_attn_w_o': 'new_v', 'new_v_mlp_w1': 'new_v', 'new_v_mlp_w2': 'new_v', 'new_v_final_norm': 'new_v'}


def _forward(args):
    return _fwd_reference(*[args[k] for k in FWD_PARAMS])


def _output_shape():
    def fwd():
        inp = _fwd_setup_inputs(0)
        return _fwd_reference(*[inp[k] for k in FWD_PARAMS])
    out = _jax.eval_shape(fwd)
    return out.shape, out.dtype

N_MICROBATCH = 1
ADAM_LR = 0.001
ADAM_B1 = 0.9
ADAM_B2 = 0.999
ADAM_EPS = 1e-08
ADAM_WD = 0.01
ADAM_STEP = 10
PER_EXAMPLE_BATCH_AXIS = {'x': 0, 'loss_target': 0}
SHARED_INPUTS = []
_WEIGHT_DTYPES = {'norm_mix': _jnp.float32, 'norm_mlp': _jnp.float32, 'ssm_w_in': _jnp.float32, 'ssm_conv_w': _jnp.float32, 'ssm_conv_b': _jnp.float32, 'ssm_dt_bias': _jnp.float32, 'ssm_a_log': _jnp.float32, 'ssm_d': _jnp.float32, 'ssm_norm_w': _jnp.float32, 'ssm_w_out': _jnp.float32, 'attn_w_qkv': _jnp.float32, 'attn_w_o': _jnp.float32, 'mlp_w1': _jnp.float32, 'mlp_w2': _jnp.float32, 'final_norm': _jnp.float32}
MOMENT_SCALE = {'norm_mix': 1.170777e-01, 'norm_mlp': 9.415723e-02, 'ssm_w_in': 6.803685e-02, 'ssm_conv_w': 6.206135e-02, 'ssm_conv_b': 9.236633e-02, 'ssm_dt_bias': 2.375260e-01, 'ssm_a_log': 2.464390e-01, 'ssm_d': 4.169633e-01, 'ssm_norm_w': 7.511787e-02, 'ssm_w_out': 1.039031e-01, 'attn_w_qkv': 1.793931e-02, 'attn_w_o': 2.902806e-02, 'mlp_w1': 4.640433e-02, 'mlp_w2': 1.000355e-01, 'final_norm': 3.249448e+01}


def _to_microbatches(a, axis):
    t = _jnp.moveaxis(a, axis, 0)
    t = t.reshape((N_MICROBATCH, t.shape[0] // N_MICROBATCH) + t.shape[1:])
    return _jnp.moveaxis(t, 1, axis + 1)


def setup_inputs(seed: int = 0) -> dict:
    inp = _fwd_setup_inputs(seed)
    key = _jax.random.fold_in(_jax.random.key(seed), 7919)
    shape, _ = _output_shape()
    out = dict(inp)
    out["loss_target"] = _jax.random.normal(_jax.random.fold_in(key, 0), shape, _jnp.float32)
    for i, name in enumerate(TWIN_WEIGHTS):
        w = inp[name].astype(_jnp.float32)
        if MOMENT_SCALE is None:
            s = _jnp.sqrt(_jnp.mean(_jnp.square(w)) + 1e-30)
        else:
            s = MOMENT_SCALE[name]
        km, kv = _jax.random.split(_jax.random.fold_in(key, i + 1))
        out[name] = w
        out["m_" + name] = s * _jax.random.normal(km, w.shape, _jnp.float32)
        out["v_" + name] = (s * s) * _jax.random.uniform(kv, w.shape, _jnp.float32, 0.5, 1.5)
    if N_MICROBATCH > 1:
        for name, axis in PER_EXAMPLE_BATCH_AXIS.items():
            out[name] = _to_microbatches(out[name], axis)
    return {'x': out['x'], 'norm_mix': out['norm_mix'], 'norm_mlp': out['norm_mlp'], 'ssm_w_in': out['ssm_w_in'], 'ssm_conv_w': out['ssm_conv_w'], 'ssm_conv_b': out['ssm_conv_b'], 'ssm_dt_bias': out['ssm_dt_bias'], 'ssm_a_log': out['ssm_a_log'], 'ssm_d': out['ssm_d'], 'ssm_norm_w': out['ssm_norm_w'], 'ssm_w_out': out['ssm_w_out'], 'attn_w_qkv': out['attn_w_qkv'], 'attn_w_o': out['attn_w_o'], 'mlp_w1': out['mlp_w1'], 'mlp_w2': out['mlp_w2'], 'final_norm': out['final_norm'], 'loss_target': out['loss_target'], 'm_norm_mix': out['m_norm_mix'], 'm_norm_mlp': out['m_norm_mlp'], 'm_ssm_w_in': out['m_ssm_w_in'], 'm_ssm_conv_w': out['m_ssm_conv_w'], 'm_ssm_conv_b': out['m_ssm_conv_b'], 'm_ssm_dt_bias': out['m_ssm_dt_bias'], 'm_ssm_a_log': out['m_ssm_a_log'], 'm_ssm_d': out['m_ssm_d'], 'm_ssm_norm_w': out['m_ssm_norm_w'], 'm_ssm_w_out': out['m_ssm_w_out'], 'm_attn_w_qkv': out['m_attn_w_qkv'], 'm_attn_w_o': out['m_attn_w_o'], 'm_mlp_w1': out['m_mlp_w1'], 'm_mlp_w2': out['m_mlp_w2'], 'm_final_norm': out['m_final_norm'], 'v_norm_mix': out['v_norm_mix'], 'v_norm_mlp': out['v_norm_mlp'], 'v_ssm_w_in': out['v_ssm_w_in'], 'v_ssm_conv_w': out['v_ssm_conv_w'], 'v_ssm_conv_b': out['v_ssm_conv_b'], 'v_ssm_dt_bias': out['v_ssm_dt_bias'], 'v_ssm_a_log': out['v_ssm_a_log'], 'v_ssm_d': out['v_ssm_d'], 'v_ssm_norm_w': out['v_ssm_norm_w'], 'v_ssm_w_out': out['v_ssm_w_out'], 'v_attn_w_qkv': out['v_attn_w_qkv'], 'v_attn_w_o': out['v_attn_w_o'], 'v_mlp_w1': out['v_mlp_w1'], 'v_mlp_w2': out['v_mlp_w2'], 'v_final_norm': out['v_final_norm']}


def _loss(weights, diff, rest, loss_target):
    with _jax.named_scope("forward"):
        args = {**rest, TWIN_DIFF_INPUT: diff, **{k: w.astype(_WEIGHT_DTYPES[k]) for k, w in weights.items()}}
        y = _forward(args)
    with _jax.named_scope("loss_head"):
        err = _jnp.square(y.astype(_jnp.float32) - loss_target)
        return 0.5 * _jnp.sum(_jnp.mean(err, axis=-1)) if err.ndim else 0.5 * err


def _adamw(w, g, m, v):
    m = ADAM_B1 * m + (1.0 - ADAM_B1) * g
    v = ADAM_B2 * v + (1.0 - ADAM_B2) * _jnp.square(g)
    m_hat = m / (1.0 - ADAM_B1 ** ADAM_STEP)
    v_hat = v / (1.0 - ADAM_B2 ** ADAM_STEP)
    delta = -ADAM_LR * (m_hat / (_jnp.sqrt(v_hat) + ADAM_EPS) + ADAM_WD * w)
    return delta, m, v


def reference(x, norm_mix, norm_mlp, ssm_w_in, ssm_conv_w, ssm_conv_b, ssm_dt_bias, ssm_a_log, ssm_d, ssm_norm_w, ssm_w_out, attn_w_qkv, attn_w_o, mlp_w1, mlp_w2, final_norm, loss_target, m_norm_mix, m_norm_mlp, m_ssm_w_in, m_ssm_conv_w, m_ssm_conv_b, m_ssm_dt_bias, m_ssm_a_log, m_ssm_d, m_ssm_norm_w, m_ssm_w_out, m_attn_w_qkv, m_attn_w_o, m_mlp_w1, m_mlp_w2, m_final_norm, v_norm_mix, v_norm_mlp, v_ssm_w_in, v_ssm_conv_w, v_ssm_conv_b, v_ssm_dt_bias, v_ssm_a_log, v_ssm_d, v_ssm_norm_w, v_ssm_w_out, v_attn_w_qkv, v_attn_w_o, v_mlp_w1, v_mlp_w2, v_final_norm):
    given = dict(x=x, norm_mix=norm_mix, norm_mlp=norm_mlp, ssm_w_in=ssm_w_in, ssm_conv_w=ssm_conv_w, ssm_conv_b=ssm_conv_b, ssm_dt_bias=ssm_dt_bias, ssm_a_log=ssm_a_log, ssm_d=ssm_d, ssm_norm_w=ssm_norm_w, ssm_w_out=ssm_w_out, attn_w_qkv=attn_w_qkv, attn_w_o=attn_w_o, mlp_w1=mlp_w1, mlp_w2=mlp_w2, final_norm=final_norm, loss_target=loss_target, m_norm_mix=m_norm_mix, m_norm_mlp=m_norm_mlp, m_ssm_w_in=m_ssm_w_in, m_ssm_conv_w=m_ssm_conv_w, m_ssm_conv_b=m_ssm_conv_b, m_ssm_dt_bias=m_ssm_dt_bias, m_ssm_a_log=m_ssm_a_log, m_ssm_d=m_ssm_d, m_ssm_norm_w=m_ssm_norm_w, m_ssm_w_out=m_ssm_w_out, m_attn_w_qkv=m_attn_w_qkv, m_attn_w_o=m_attn_w_o, m_mlp_w1=m_mlp_w1, m_mlp_w2=m_mlp_w2, m_final_norm=m_final_norm, v_norm_mix=v_norm_mix, v_norm_mlp=v_norm_mlp, v_ssm_w_in=v_ssm_w_in, v_ssm_conv_w=v_ssm_conv_w, v_ssm_conv_b=v_ssm_conv_b, v_ssm_dt_bias=v_ssm_dt_bias, v_ssm_a_log=v_ssm_a_log, v_ssm_d=v_ssm_d, v_ssm_norm_w=v_ssm_norm_w, v_ssm_w_out=v_ssm_w_out, v_attn_w_qkv=v_attn_w_qkv, v_attn_w_o=v_attn_w_o, v_mlp_w1=v_mlp_w1, v_mlp_w2=v_mlp_w2, v_final_norm=v_final_norm)
    weights = {n: given[n] for n in TWIN_WEIGHTS}
    shared = {n: given[n] for n in SHARED_INPUTS}
    per_example = {n: given[n] for n in ['x']}
    grad_fn = _jax.value_and_grad(_loss, argnums=(0, 1))

    def one_microbatch(ex, loss_target):
        ex = dict(ex)
        diff = ex.pop(TWIN_DIFF_INPUT)
        return grad_fn(weights, diff, {**shared, **ex}, loss_target)

    if N_MICROBATCH == 1:
        loss, (grad_w, grad_x) = one_microbatch(per_example, given["loss_target"])
    else:
        def body(carry, xs):
            loss_sum, grad_sum = carry
            l_k, (gw_k, gx_k) = one_microbatch(xs[0], xs[1])
            with _jax.named_scope("update"):
                return (loss_sum + l_k, _jax.tree.map(_jnp.add, grad_sum, gw_k)), gx_k

        init = (_jnp.zeros((), _jnp.float32), _jax.tree.map(_jnp.zeros_like, weights))
        (loss, grad_w), grad_x = _jax.lax.scan(body, init, (per_example, given["loss_target"]))
    with _jax.named_scope("update"):
        delta_w, new_m, new_v = {}, {}, {}
        for n in TWIN_WEIGHTS:
            delta_w[n], new_m[n], new_v[n] = _adamw(weights[n], grad_w[n], given["m_" + n], given["v_" + n])
    return (loss, grad_x, *[grad_w[n] for n in TWIN_WEIGHTS], *[delta_w[n] for n in TWIN_WEIGHTS],
            *[new_m[n] for n in TWIN_WEIGHTS], *[new_v[n] for n in TWIN_WEIGHTS])
```

```python
import math

import jax
import jax.numpy as jnp
from jax import lax
from jax.experimental import pallas as pl
from jax.experimental.pallas import tpu as pltpu

F32, BF16 = jnp.float32, jnp.bfloat16
SDS = jax.ShapeDtypeStruct
MESH = pl.DeviceIdType.MESH
HIGHEST = lax.Precision.HIGHEST

NDEV = 8
D_MODEL = 2048
D_INNER = 4096
SSM_HEADS = 64
SSM_HEAD_DIM = 64
SSM_GROUPS = 8
SSM_STATE = 128
CHUNK = 128
CONV_DIM = 6144
CONV_WIDTH = 4
GROUP_W = D_INNER // SSM_GROUPS
ATTN_GROUPS = ((128, 1), (512, 4), (2048, 16))
ATTN_W = 1024
QKV_DIM = 9216
D_FF = 8192
EPS = 1e-5
ADAM_LR, ADAM_B1, ADAM_B2, ADAM_EPS, ADAM_WD, ADAM_STEP = 0.001, 0.9, 0.999, 1e-08, 0.01, 10

VMEM_LIMIT = 48 * 1024 * 1024


def _params(sem):
    return pltpu.CompilerParams(dimension_semantics=sem, vmem_limit_bytes=VMEM_LIMIT)


def _sigmoid(v):
    return 1.0 / (1.0 + jnp.exp(-v))


def _softplus(v):
    return jnp.maximum(v, 0.0) + jnp.log1p(jnp.exp(-jnp.abs(v)))


def _wspec(layout, kw, nw, tr, tc, sel):
    if layout == "plain":
        return pl.BlockSpec((tr, tc), lambda *g: sel(*g))
    if layout == "cols":
        per = (nw // NDEV) // tc
        return pl.BlockSpec((None, tr, tc), lambda *g: (sel(*g)[1] // per, sel(*g)[0], sel(*g)[1] % per))
    per = (kw // NDEV) // tr
    return pl.BlockSpec((None, tr, tc), lambda *g: (sel(*g)[0] // per, sel(*g)[0] % per, sel(*g)[1]))


def _wshape(layout, kw, nw):
    if layout == "plain":
        return (kw, nw)
    if layout == "cols":
        return (NDEV, kw, nw // NDEV)
    return (NDEV, kw // NDEV, nw)


def _mm_call(name, grid, in_specs, out_specs, out_shape, dims, n_extra, epi, tm, tn):
    nk = grid[2]

    def body(*refs):
        a_ref, b_ref = refs[0], refs[1]
        extra = refs[2:2 + n_extra]
        outs = refs[2 + n_extra:-1]
        acc = refs[-1]
        k = pl.program_id(2)
        d = lax.dot_general(a_ref[...], b_ref[...], (dims, ((), ())), preferred_element_type=F32)

        def finish(total):
            vals = epi(total, *[e[...] for e in extra])
            for o, v in zip(outs, vals):
                o[...] = v.astype(o.dtype)

        if nk == 1:
            finish(d)
        else:
            @pl.when(k == 0)
            def _():
                acc[...] = d

            @pl.when(jnp.logical_and(k > 0, k < nk - 1))
            def _():
                acc[...] += d

            @pl.when(k == nk - 1)
            def _():
                finish(acc[...] + d)

    return pl.pallas_call(
        body, grid=grid, in_specs=in_specs, out_specs=out_specs, out_shape=out_shape,
        scratch_shapes=[pltpu.VMEM((tm, tn), F32)], name=name,
        compiler_params=_params(("parallel", "parallel", "arbitrary")))


def _mm_fwd(a, w, layout, kw, nw, *, epi, outs, extras=(), tm=1024, tn=1024, tk=1024, name):
    m = a.shape[0]
    tm, tn, tk = min(tm, m), min(tn, nw), min(tk, kw)
    grid = (m // tm, nw // tn, kw // tk)
    o_spec = pl.BlockSpec((tm, tn), lambda i, j, k: (i, j))
    in_specs = [pl.BlockSpec((tm, tk), lambda i, j, k: (i, k)), _wspec(layout, kw, nw, tk, tn, lambda i, j, k: (k, j))]
    in_specs += [o_spec] * len(extras)
    call = _mm_call(name, grid, in_specs, [o_spec] * len(outs), [SDS((m, nw), dt) for dt in outs], ((1,), (0,)),
                    len(extras), epi, tm, tn)
    return call(a, w, *extras)


def _mm_dx(g, w, layout, kw, nw, *, epi, outs, extras=(), tm=1024, tn=1024, tk=1024, name):
    m = g.shape[0]
    tm, tn, tk = min(tm, m), min(tn, kw), min(tk, nw)
    grid = (m // tm, kw // tn, nw // tk)
    o_spec = pl.BlockSpec((tm, tn), lambda i, j, k: (i, j))
    in_specs = [pl.BlockSpec((tm, tk), lambda i, j, k: (i, k)), _wspec(layout, kw, nw, tn, tk, lambda i, j, k: (j, k))]
    in_specs += [o_spec] * len(extras)
    call = _mm_call(name, grid, in_specs, [o_spec] * len(outs), [SDS((m, kw), dt) for dt in outs], ((1,), (1,)),
                    len(extras), epi, tm, tn)
    return call(g, w, *extras)


def _mm_dw(a, g, layout, *, tm=1024, tn=1024, tk=1024, name):
    m, kw = a.shape
    nw = g.shape[1]
    tm, tn, tk = min(tm, kw), min(tn, nw), min(tk, m)
    grid = (kw // tm, nw // tn, m // tk)
    in_specs = [pl.BlockSpec((tk, tm), lambda i, j, k: (k, i)), pl.BlockSpec((tk, tn), lambda i, j, k: (k, j))]
    o_spec = _wspec(layout, kw, nw, tm, tn, lambda i, j, k: (i, j))
    call = _mm_call(name, grid, in_specs, [o_spec], [SDS(_wshape(layout, kw, nw), BF16)], ((0,), (0,)),
                    0, lambda t: (t,), tm, tn)
    return call(a, g)[0]


ROW_TILE = 256


def _rmsnorm_fwd(x, g, name):
    t, d = x.shape

    def body(x_ref, g_ref, h_ref):
        xv = x_ref[...]
        r = lax.rsqrt(jnp.mean(xv * xv, axis=-1, keepdims=True) + EPS)
        h_ref[...] = (xv * r * g_ref[...]).astype(BF16)

    row = pl.BlockSpec((ROW_TILE, d), lambda i: (i, 0))
    vec = pl.BlockSpec((1, d), lambda i: (0, 0))
    return pl.pallas_call(body, grid=(t // ROW_TILE,), in_specs=[row, vec], out_specs=row, out_shape=SDS((t, d), BF16),
                          name=name, compiler_params=_params(("parallel",)))(x, g)


def _rmsnorm_bwd(x, g, dh, dres, name):
    t, d = x.shape

    def body(x_ref, g_ref, dh_ref, dres_ref, dx_ref, dxb_ref, dg_ref):
        xv = x_ref[...]
        r = lax.rsqrt(jnp.mean(xv * xv, axis=-1, keepdims=True) + EPS)
        xh = xv * r
        dhv = dh_ref[...]
        gd = dhv * g_ref[...]
        dx = dres_ref[...] + r * (gd - xh * jnp.mean(gd * xh, axis=-1, keepdims=True))
        dx_ref[...] = dx
        dxb_ref[...] = dx.astype(BF16)
        part = jnp.sum(dhv * xh, axis=0, keepdims=True)

        @pl.when(pl.program_id(0) == 0)
        def _():
            dg_ref[...] = part

        @pl.when(pl.program_id(0) > 0)
        def _():
            dg_ref[...] += part

    row = pl.BlockSpec((ROW_TILE, d), lambda i: (i, 0))
    vec = pl.BlockSpec((1, d), lambda i: (0, 0))
    return pl.pallas_call(body, grid=(t // ROW_TILE,), in_specs=[row, vec, row, row], out_specs=[row, row, vec],
                          out_shape=[SDS((t, d), F32), SDS((t, d), BF16), SDS((1, d), F32)], name=name,
                          compiler_params=_params(("arbitrary",)))(x, g, dh, dres)


def _loss_head(x, tgt, g, name):
    t, d = x.shape

    def body(x_ref, t_ref, g_ref, dx_ref, dxb_ref, loss_ref, dg_ref):
        xv = x_ref[...]
        r = lax.rsqrt(jnp.mean(xv * xv, axis=-1, keepdims=True) + EPS)
        xh = xv * r
        gv = g_ref[...]
        err = xh * gv - t_ref[...]
        part_loss = 0.5 * jnp.sum(jnp.mean(err * err, axis=-1, keepdims=True), axis=0, keepdims=True)
        dy = err * (1.0 / d)
        gd = dy * gv
        dx = r * (gd - xh * jnp.mean(gd * xh, axis=-1, keepdims=True))
        dx_ref[...] = dx
        dxb_ref[...] = dx.astype(BF16)
        part_g = jnp.sum(dy * xh, axis=0, keepdims=True)
        part_l = jnp.broadcast_to(part_loss, (8, 128))

        @pl.when(pl.program_id(0) == 0)
        def _():
            dg_ref[...] = part_g
            loss_ref[...] = part_l

        @pl.when(pl.program_id(0) > 0)
        def _():
            dg_ref[...] += part_g
            loss_ref[...] += part_l

    row = pl.BlockSpec((ROW_TILE, d), lambda i: (i, 0))
    vec = pl.BlockSpec((1, d), lambda i: (0, 0))
    sc = pl.BlockSpec((8, 128), lambda i: (0, 0))
    return pl.pallas_call(body, grid=(t // ROW_TILE,), in_specs=[row, row, vec], out_specs=[row, row, sc, vec],
                          out_shape=[SDS((t, d), F32), SDS((t, d), BF16), SDS((8, 128), F32), SDS((1, d), F32)], name=name,
                          compiler_params=_params(("arbitrary",)))(x, tgt, g)


CONV_ROWS = 512
CONV_COLS = 512


def _shift_down(cur, prev8, k):
    sh = pltpu.roll(cur, k, axis=0)
    ph = pltpu.roll(prev8, k, axis=0)
    rid = lax.broadcasted_iota(jnp.int32, ph.shape, 0)
    head = jnp.where(rid < k, ph, sh[0:8])
    return jnp.concatenate([head, sh[8:]], axis=0)


def _shift_up(cur, next8, k):
    n = cur.shape[0]
    sh = pltpu.roll(cur, n - k, axis=0)
    nh = pltpu.roll(next8, 8 - k, axis=0)
    rid = lax.broadcasted_iota(jnp.int32, nh.shape, 0)
    tail = jnp.where(rid >= 8 - k, nh, sh[n - 8:])
    return jnp.concatenate([sh[:n - 8], tail], axis=0)


def _conv_pre(cur, prev8, w, b):
    acc = w[3:4, :] * cur + b
    for k in range(1, CONV_WIDTH):
        acc = acc + w[3 - k:4 - k, :] * _shift_down(cur, prev8, k)
    return acc


def _conv_fwd(u, w, b, name):
    t, c = u.shape
    per = CONV_ROWS // 8

    def body(u_ref, p_ref, w_ref, b_ref, o_ref):
        prev8 = jnp.where(pl.program_id(1) == 0, 0.0, p_ref[...])
        pre = _conv_pre(u_ref[...], prev8, w_ref[...], b_ref[...])
        o_ref[...] = pre * _sigmoid(pre)

    cur = pl.BlockSpec((CONV_ROWS, CONV_COLS), lambda j, i: (i, j))
    prev = pl.BlockSpec((8, CONV_COLS), lambda j, i: (jnp.maximum(i * per - 1, 0), j))
    wsp = pl.BlockSpec((CONV_WIDTH, CONV_COLS), lambda j, i: (0, j))
    bsp = pl.BlockSpec((1, CONV_COLS), lambda j, i: (0, j))
    return pl.pallas_call(body, grid=(c // CONV_COLS, t // CONV_ROWS), in_specs=[cur, prev, wsp, bsp], out_specs=cur,
                          out_shape=SDS((t, c), F32), name=name, compiler_params=_params(("parallel", "parallel")))(u, u, w, b)


def _conv_bwd_pre(u, w, b, dout, name):
    t, c = u.shape
    per = CONV_ROWS // 8

    def body(u_ref, p_ref, w_ref, b_ref, d_ref, dp_ref, dw_ref):
        i = pl.program_id(1)
        cur = u_ref[...]
        prev8 = jnp.where(i == 0, 0.0, p_ref[...])
        pre = _conv_pre(cur, prev8, w_ref[...], b_ref[...])
        s = _sigmoid(pre)
        dpre = d_ref[...] * (s * (1.0 + pre * (1.0 - s)))
        dp_ref[...] = dpre
        rows = [jnp.sum(dpre * _shift_down(cur, prev8, 3 - k), axis=0, keepdims=True) for k in range(3)]
        rows.append(jnp.sum(dpre * cur, axis=0, keepdims=True))
        rows.append(jnp.sum(dpre, axis=0, keepdims=True))
        part = jnp.concatenate(rows + [jnp.zeros((3, cur.shape[1]), F32)], axis=0)

        @pl.when(i == 0)
        def _():
            dw_ref[...] = part

        @pl.when(i > 0)
        def _():
            dw_ref[...] += part

    cur = pl.BlockSpec((CONV_ROWS, CONV_COLS), lambda j, i: (i, j))
    prev = pl.BlockSpec((8, CONV_COLS), lambda j, i: (jnp.maximum(i * per - 1, 0), j))
    wsp = pl.BlockSpec((CONV_WIDTH, CONV_COLS), lambda j, i: (0, j))
    bsp = pl.BlockSpec((1, CONV_COLS), lambda j, i: (0, j))
    acc = pl.BlockSpec((8, CONV_COLS), lambda j, i: (0, j))
    return pl.pallas_call(body, grid=(c // CONV_COLS, t // CONV_ROWS), in_specs=[cur, prev, wsp, bsp, cur],
                          out_specs=[cur, acc], out_shape=[SDS((t, c), F32), SDS((8, c), F32)], name=name,
                          compiler_params=_params(("parallel", "arbitrary")))(u, u, w, b, dout)


def _conv_bwd_in(dpre, w, name):
    t, c = dpre.shape
    per = CONV_ROWS // 8
    last = t // CONV_ROWS - 1

    def body(d_ref, n_ref, w_ref, o_ref):
        cur = d_ref[...]
        next8 = jnp.where(pl.program_id(1) == last, 0.0, n_ref[...])
        wv = w_ref[...]
        acc = wv[3:4, :] * cur
        for k in range(1, CONV_WIDTH):
            acc = acc + wv[3 - k:4 - k, :] * _shift_up(cur, next8, k)
        o_ref[...] = acc.astype(BF16)

    cur = pl.BlockSpec((CONV_ROWS, CONV_COLS), lambda j, i: (i, j))
    nxt = pl.BlockSpec((8, CONV_COLS), lambda j, i: (jnp.minimum((i + 1) * per, t // 8 - 1), j))
    wsp = pl.BlockSpec((CONV_WIDTH, CONV_COLS), lambda j, i: (0, j))
    return pl.pallas_call(body, grid=(c // CONV_COLS, t // CONV_ROWS), in_specs=[cur, nxt, wsp], out_specs=cur,
                          out_shape=SDS((t, c), BF16), name=name, compiler_params=_params(("parallel", "parallel")))(dpre, dpre, w)


def _head_expand():
    r = lax.broadcasted_iota(jnp.int32, (128, GROUP_W), 0)
    c = lax.broadcasted_iota(jnp.int32, (128, GROUP_W), 1)
    return (c // SSM_HEAD_DIM == r).astype(F32)


def _head_sum():
    r = lax.broadcasted_iota(jnp.int32, (GROUP_W, 128), 0)
    c = lax.broadcasted_iota(jnp.int32, (GROUP_W, 128), 1)
    return (r // SSM_HEAD_DIM == c).astype(F32)


def _dot(a, b):
    return jnp.dot(a, b, preferred_element_type=F32)


def _dot_hi(a, b):
    return jnp.dot(a, b, precision=HIGHEST, preferred_element_type=F32)


def _dot_nt(a, b):
    return lax.dot_general(a, b, (((1,), (1,)), ((), ())), preferred_element_type=F32)


def _dot_tn(a, b):
    return lax.dot_general(a, b, (((0,), (0,)), ((), ())), preferred_element_type=F32)


def _ssd_common(dtp, dtpt, bias, biasr, alog, alogr):
    li = lax.broadcasted_iota(jnp.int32, (CHUNK, CHUNK), 0)
    si = lax.broadcasted_iota(jnp.int32, (CHUNK, CHUNK), 1)
    lower, upper = (li >= si), (li <= si)
    dt = _softplus(dtp + bias)
    a_neg = -jnp.exp(alog)
    cs = _dot_hi(lower.astype(F32), dt * a_neg)
    dtr = _softplus(dtpt + biasr)
    csr = _dot_hi(dtr * (-jnp.exp(alogr)), upper.astype(F32))
    return lower, upper, dt, a_neg, cs, csr


def _head_masked_rows(v):
    hl = lax.broadcasted_iota(jnp.int32, v.shape, 1) // SSM_HEAD_DIM
    return jnp.concatenate([jnp.where(hl == j, v, jnp.zeros_like(v)) for j in range(8)], axis=0)


def _ssd_fwd(xbc, z, dtp_g, dtp_t, bias_g, bias_r, alog_g, alog_r, d_exp, norm_w, name):
    t = xbc.shape[0]
    nc = t // CHUNK

    def body(xs_ref, b_ref, c_ref, z_ref, dtp_ref, dtpt_ref, bias_ref, biasr_ref, alog_ref, alogr_ref, dexp_ref, nw_ref,
             y_ref, yn_ref, st_ref, state):
        c = pl.program_id(1)

        @pl.when(c == 0)
        def _():
            state[...] = jnp.zeros_like(state)

        lower, _, dt, a_neg, cs, csr = _ssd_common(dtp_ref[...], dtpt_ref[...], bias_ref[...], biasr_ref[...],
                                                alog_ref[...], alogr_ref[...])
        expand = _head_expand()
        cs_e = _dot_hi(cs, expand)
        dt_e = _dot_hi(dt, expand)
        xs = xs_ref[...]
        xdt = xs * dt_e
        bm = b_ref[...]
        cm = c_ref[...]
        bmb, cmb = bm.astype(BF16), cm.astype(BF16)
        cb = _dot_nt(cmb, bmb)
        ms = []
        for j in range(8):
            dlt = cs_e[:, SSM_HEAD_DIM * j:SSM_HEAD_DIM * j + 1] - csr[j:j + 1, :]
            ms.append((cb * jnp.exp(jnp.where(lower, dlt, -jnp.inf))).astype(BF16))
        y = _dot(jnp.concatenate(ms, axis=1), _head_masked_rows(xdt.astype(BF16)))
        st_in = state[...]
        st_ref[...] = st_in
        y = y + jnp.exp(cs_e) * _dot(cmb, st_in.astype(BF16))
        cs_last = cs_e[CHUNK - 1:CHUNK, :]
        xdtd = (xdt * jnp.exp(cs_last - cs_e)).astype(BF16)
        state[...] = jnp.exp(cs_last) * st_in + _dot(bm.T.astype(BF16), xdtd)
        y_ref[...] = y
        zz = z_ref[...]
        y2 = (y + dexp_ref[...] * xs) * (zz * _sigmoid(zz))
        r = lax.rsqrt(jnp.mean(y2 * y2, axis=-1, keepdims=True) + EPS)
        yn_ref[...] = (y2 * r * nw_ref[...]).astype(BF16)

    gw = pl.BlockSpec((CHUNK, GROUP_W), lambda g, c: (c, g))
    in_specs = [
        gw,
        pl.BlockSpec((CHUNK, SSM_STATE), lambda g, c: (c, D_INNER // SSM_STATE + g)),
        pl.BlockSpec((CHUNK, SSM_STATE), lambda g, c: (c, D_INNER // SSM_STATE + SSM_GROUPS + g)),
        gw,
        pl.BlockSpec((None, CHUNK, 128), lambda g, c: (g, c, 0)),
        pl.BlockSpec((8, CHUNK), lambda g, c: (g, c)),
        pl.BlockSpec((None, 1, 128), lambda g, c: (g, 0, 0)),
        pl.BlockSpec((8, 128), lambda g, c: (g, 0)),
        pl.BlockSpec((None, 1, 128), lambda g, c: (g, 0, 0)),
        pl.BlockSpec((8, 128), lambda g, c: (g, 0)),
        pl.BlockSpec((1, GROUP_W), lambda g, c: (0, g)),
        pl.BlockSpec((1, GROUP_W), lambda g, c: (0, g)),
    ]
    out_specs = [gw, gw, pl.BlockSpec((None, SSM_STATE, GROUP_W), lambda g, c: (c, 0, g))]
    out_shape = [SDS((t, D_INNER), F32), SDS((t, D_INNER), BF16), SDS((nc, SSM_STATE, D_INNER), F32)]
    return pl.pallas_call(body, grid=(SSM_GROUPS, nc), in_specs=in_specs, out_specs=out_specs, out_shape=out_shape,
                          scratch_shapes=[pltpu.VMEM((SSM_STATE, GROUP_W), F32)], name=name,
                          compiler_params=_params(("parallel", "arbitrary")))(
        xbc, xbc, xbc, z, dtp_g, dtp_t, bias_g, bias_r, alog_g, alog_r, d_exp, norm_w)


def _ssd_bwd(dyn, y, xbc, z, states, dtp_g, dtp_t, bias_g, bias_r, alog_g, alog_r, d_exp, norm_w, name):
    t = xbc.shape[0]
    nc = t // CHUNK

    def body(dyn_ref, y_ref, xs_ref, b_ref, c_ref, z_ref, st_ref, dtp_ref, dtpt_ref, bias_ref, biasr_ref, alog_ref,
             alogr_ref, dexp_ref, nw_ref, dxs_ref, db_ref, dc_ref, dz_ref, ddt_ref, hsum_ref, csum_ref, dstate):
        step = pl.program_id(1)

        @pl.when(step == 0)
        def _():
            dstate[...] = jnp.zeros_like(dstate)

        dtp = dtp_ref[...]
        bias = bias_ref[...]
        lower, upper, dt, a_neg, cs, csr = _ssd_common(dtp, dtpt_ref[...], bias, biasr_ref[...], alog_ref[...],
                                                       alogr_ref[...])
        expand = _head_expand()
        hsum = _head_sum()
        cs_e = _dot_hi(cs, expand)
        dt_e = _dot_hi(dt, expand)
        xs = xs_ref[...]
        xdt = xs * dt_e
        bm = b_ref[...]
        cm = c_ref[...]
        bmb, cmb = bm.astype(BF16), cm.astype(BF16)
        y = y_ref[...]
        dexp = dexp_ref[...]
        nw = nw_ref[...]

        zz = z_ref[...]
        sg = _sigmoid(zz)
        gate = zz * sg
        ytot = y + dexp * xs
        y2 = ytot * gate
        r = lax.rsqrt(jnp.mean(y2 * y2, axis=-1, keepdims=True) + EPS)
        dynv = dyn_ref[...]
        xh = y2 * r
        gn = dynv * nw
        dy2 = r * (gn - xh * jnp.mean(gn * xh, axis=-1, keepdims=True))
        dy = dy2 * gate
        dz_ref[...] = (dy2 * ytot * (sg * (1.0 + zz * (1.0 - sg)))).astype(BF16)
        csum_part = jnp.sum(dynv * xh, axis=0, keepdims=True)
        d_skip = jnp.sum(_dot_hi(dy * xs, hsum), axis=0, keepdims=True)

        cb = _dot_nt(cmb, bmb)
        dyb = dy.astype(BF16)
        xdtb = xdt.astype(BF16)
        dym = _head_masked_rows(dyb)
        dm = _dot_nt(dym, xdtb)
        dmt = _dot_nt(_head_masked_rows(xdtb), dyb)
        lane = lax.broadcasted_iota(jnp.int32, (CHUNK, 128), 1)
        mts = []
        dcb = jnp.zeros((CHUNK, CHUNK), F32)
        dcs = jnp.zeros((CHUNK, 128), F32)
        for j in range(8):
            dlt = cs_e[:, SSM_HEAD_DIM * j:SSM_HEAD_DIM * j + 1] - csr[j:j + 1, :]
            lj = jnp.exp(jnp.where(lower, dlt, -jnp.inf))
            mj = cb * lj
            mjt = mj.T
            mts.append(mjt.astype(BF16))
            dmj = dm[CHUNK * j:CHUNK * (j + 1)]
            dcb = dcb + dmj * lj
            rows = jnp.sum(dmj * mj, axis=1, keepdims=True)
            cols = jnp.sum(dmt[CHUNK * j:CHUNK * (j + 1)] * mjt, axis=1, keepdims=True)
            dcs = dcs + jnp.where(lane == j, rows - cols, 0.0)
        dxdt = _dot(jnp.concatenate(mts, axis=1), dym)
        dst_out = dstate[...]
        dst_outb = dst_out.astype(BF16)
        st_in = st_ref[...]
        st_inb = st_in.astype(BF16)
        cs_last = cs_e[CHUNK - 1:CHUNK, :]
        decay = jnp.exp(cs_last - cs_e)
        e_last = jnp.exp(cs_last)
        gpart = decay * _dot(bmb, dst_outb)
        dxdt = dxdt + gpart
        dyw = (jnp.exp(cs_e) * dy).astype(BF16)
        dcbb = dcb.astype(BF16)
        dc_ref[...] = _dot_nt(dyw, st_inb) + _dot(dcbb, bmb)
        db_ref[...] = _dot_nt((xdt * decay).astype(BF16), dst_outb) + _dot(dcb.T.astype(BF16), cmb)
        dstate[...] = e_last * dst_out + _dot(cm.T.astype(BF16), dyw)
        y_off = jnp.exp(cs_e) * _dot(cmb, st_inb)
        dcs = dcs + _dot_hi(dy * y_off - xdt * gpart, hsum)
        tail = jnp.sum(_dot_hi(xdt * gpart, hsum), axis=0, keepdims=True) + \
            jnp.sum(_dot_hi(dst_out * e_last * st_in, hsum), axis=0, keepdims=True)
        ri = lax.broadcasted_iota(jnp.int32, (CHUNK, 128), 0)
        dcs = dcs + jnp.where(ri == CHUNK - 1, tail, 0.0)
        da = _dot_hi(upper.astype(F32), dcs)
        ddt = da * a_neg + _dot_hi(dxdt * xs, hsum)
        dxs_ref[...] = dxdt * dt_e + dy * dexp
        ddtp = ddt * _sigmoid(dtp + bias)
        ddt_ref[...] = ddtp
        d_alog = jnp.sum(da * dt, axis=0, keepdims=True) * a_neg
        hpart = jnp.concatenate([jnp.sum(ddtp, axis=0, keepdims=True), d_alog, d_skip, jnp.zeros((5, 128), F32)], axis=0)
        cpart = jnp.concatenate([csum_part, jnp.zeros((7, GROUP_W), F32)], axis=0)

        @pl.when(step == 0)
        def _():
            hsum_ref[...] = hpart
            csum_ref[...] = cpart

        @pl.when(step > 0)
        def _():
            hsum_ref[...] += hpart
            csum_ref[...] += cpart

    rc = lambda c: nc - 1 - c
    gw = pl.BlockSpec((CHUNK, GROUP_W), lambda g, c: (rc(c), g))
    bsp = pl.BlockSpec((CHUNK, SSM_STATE), lambda g, c: (rc(c), D_INNER // SSM_STATE + g))
    csp = pl.BlockSpec((CHUNK, SSM_STATE), lambda g, c: (rc(c), D_INNER // SSM_STATE + SSM_GROUPS + g))
    in_specs = [
        gw, gw, gw, bsp, csp, gw,
        pl.BlockSpec((None, SSM_STATE, GROUP_W), lambda g, c: (rc(c), 0, g)),
        pl.BlockSpec((None, CHUNK, 128), lambda g, c: (g, rc(c), 0)),
        pl.BlockSpec((8, CHUNK), lambda g, c: (g, rc(c))),
        pl.BlockSpec((None, 1, 128), lambda g, c: (g, 0, 0)),
        pl.BlockSpec((8, 128), lambda g, c: (g, 0)),
        pl.BlockSpec((None, 1, 128), lambda g, c: (g, 0, 0)),
        pl.BlockSpec((8, 128), lambda g, c: (g, 0)),
        pl.BlockSpec((1, GROUP_W), lambda g, c: (0, g)),
        pl.BlockSpec((1, GROUP_W), lambda g, c: (0, g)),
    ]
    nsp = pl.BlockSpec((CHUNK, SSM_STATE), lambda g, c: (rc(c), g))
    out_specs = [gw, nsp, nsp, gw,
                 pl.BlockSpec((None, CHUNK, 128), lambda g, c: (g, rc(c), 0)),
                 pl.BlockSpec((None, 8, 128), lambda g, c: (g, 0, 0)),
                 pl.BlockSpec((8, GROUP_W), lambda g, c: (0, g))]
    gn = SSM_GROUPS * SSM_STATE
    out_shape = [SDS((t, D_INNER), F32), SDS((t, gn), F32), SDS((t, gn), F32), SDS((t, D_INNER), BF16),
                 SDS((SSM_GROUPS, t, 128), F32), SDS((SSM_GROUPS, 8, 128), F32), SDS((8, D_INNER), F32)]
    return pl.pallas_call(body, grid=(SSM_GROUPS, nc), in_specs=in_specs, out_specs=out_specs, out_shape=out_shape,
                          scratch_shapes=[pltpu.VMEM((SSM_STATE, GROUP_W), F32)], name=name,
                          compiler_params=_params(("parallel", "arbitrary")))(
        dyn, y, xbc, xbc, xbc, z, states, dtp_g, dtp_t, bias_g, bias_r, alog_g, alog_r, d_exp, norm_w)


BLK = 128
HEAD_PAIRS = ATTN_W // 128
ATTN_SCALE = 0.125


def _slopes(group):
    n = len(ATTN_GROUPS) * 16
    return [2.0 ** (-8.0 * (16 * group + h + 1) / n) for h in range(16)]


def _lane_lo(rows):
    return lax.broadcasted_iota(jnp.int32, (rows, 128), 1) < 64


def _attn_fwd(qkv, group, name):
    _, dil = ATTN_GROUPS[group]
    t = qkv.shape[0]
    sub = t // dil
    nb = sub // BLK
    slopes = _slopes(group)

    def body(q_ref, kp_ref, kc_ref, vp_ref, vc_ref, o_ref, l_ref):
        n = pl.program_id(1)
        qi = lax.broadcasted_iota(jnp.int32, (BLK, 2 * BLK), 0)
        kj = lax.broadcasted_iota(jnp.int32, (BLK, 2 * BLK), 1)
        dist = qi + BLK - kj
        valid = (dist >= 0) & (dist <= BLK) & ((kj >= BLK) | (n > 0))
        distf = dist.astype(F32) * float(dil)
        lo_q, lo_k = _lane_lo(BLK), _lane_lo(2 * BLK)
        for hp in range(HEAD_PAIRS):
            sl = slice(128 * hp, 128 * hp + 128)
            q2 = q_ref[:, sl]
            k2 = jnp.concatenate([kp_ref[:, sl], kc_ref[:, sl]], axis=0)
            v2 = jnp.concatenate([vp_ref[:, sl], vc_ref[:, sl]], axis=0)
            o2 = jnp.zeros((BLK, 128), F32)
            l2 = jnp.zeros((BLK, 128), F32)
            for s01 in range(2):
                mq = lo_q if s01 == 0 else ~lo_q
                mk = lo_k if s01 == 0 else ~lo_k
                s = _dot_nt(jnp.where(mq, q2, jnp.zeros_like(q2)), k2) * ATTN_SCALE - slopes[2 * hp + s01] * distf
                s = jnp.where(valid, s, -jnp.inf)
                mx = jnp.max(s, axis=-1, keepdims=True)
                p = jnp.exp(s - mx)
                den = jnp.sum(p, axis=-1, keepdims=True)
                o2 = o2 + _dot(p.astype(BF16), jnp.where(mk, v2, jnp.zeros_like(v2))) / den
                l2 = jnp.where(mq, mx + jnp.log(den), l2)
            o_ref[:, sl] = o2
            l_ref[:, sl] = l2

    ncol = QKV_DIM // ATTN_W

    def col(which):
        return lambda r, n: (n, r * ncol + 3 * which + group)

    def colp(which):
        return lambda r, n: (jnp.maximum(n - 1, 0), r * ncol + 3 * which + group)

    blk = (BLK, ATTN_W)
    in_specs = [pl.BlockSpec(blk, col(0)), pl.BlockSpec(blk, colp(1)), pl.BlockSpec(blk, col(1)),
                pl.BlockSpec(blk, colp(2)), pl.BlockSpec(blk, col(2))]
    osp = pl.BlockSpec(blk, lambda r, n: (n, r))
    qv = qkv.reshape(sub, dil * QKV_DIM)
    o, l = pl.pallas_call(body, grid=(dil, nb), in_specs=in_specs, out_specs=[osp, osp],
                          out_shape=[SDS((sub, dil * ATTN_W), F32), SDS((sub, dil * ATTN_W), F32)], name=name,
                          compiler_params=_params(("parallel", "parallel")))(qv, qv, qv, qv, qv)
    return o.reshape(t, ATTN_W), l.reshape(t, ATTN_W)


def _attn_bwd_q(qkv, do, lse, dvec, group, name):
    _, dil = ATTN_GROUPS[group]
    t = qkv.shape[0]
    sub = t // dil
    nb = sub // BLK
    slopes = _slopes(group)

    def body(q_ref, kp_ref, kc_ref, vp_ref, vc_ref, do_ref, l_ref, dv_ref, dq_ref):
        n = pl.program_id(1)
        qi = lax.broadcasted_iota(jnp.int32, (BLK, 2 * BLK), 0)
        kj = lax.broadcasted_iota(jnp.int32, (BLK, 2 * BLK), 1)
        dist = qi + BLK - kj
        valid = (dist >= 0) & (dist <= BLK) & ((kj >= BLK) | (n > 0))
        distf = dist.astype(F32) * float(dil)
        lo_q, lo_k = _lane_lo(BLK), _lane_lo(2 * BLK)
        for hp in range(HEAD_PAIRS):
            sl = slice(128 * hp, 128 * hp + 128)
            q2 = q_ref[:, sl]
            k2 = jnp.concatenate([kp_ref[:, sl], kc_ref[:, sl]], axis=0)
            v2 = jnp.concatenate([vp_ref[:, sl], vc_ref[:, sl]], axis=0)
            do2 = do_ref[:, sl]
            l2 = l_ref[:, sl]
            d2 = dv_ref[:, sl]
            dq2 = jnp.zeros((BLK, 128), F32)
            for s01 in range(2):
                mq = lo_q if s01 == 0 else ~lo_q
                mk = lo_k if s01 == 0 else ~lo_k
                s = _dot_nt(jnp.where(mq, q2, jnp.zeros_like(q2)), k2) * ATTN_SCALE - slopes[2 * hp + s01] * distf
                p = jnp.exp(jnp.where(valid, s - l2[:, 64 * s01:64 * s01 + 1], -jnp.inf))
                dp = _dot_nt(jnp.where(mq, do2, jnp.zeros_like(do2)), v2)
                ds = p * (dp - d2[:, 64 * s01:64 * s01 + 1])
                dq2 = dq2 + _dot(ds.astype(BF16), jnp.where(mk, k2, jnp.zeros_like(k2))) * ATTN_SCALE
            dq_ref[:, sl] = dq2.astype(BF16)

    ncol = QKV_DIM // ATTN_W

    def col(which):
        return lambda r, n: (n, r * ncol + 3 * which + group)

    def colp(which):
        return lambda r, n: (jnp.maximum(n - 1, 0), r * ncol + 3 * which + group)

    blk = (BLK, ATTN_W)
    osp = pl.BlockSpec(blk, lambda r, n: (n, r))
    in_specs = [pl.BlockSpec(blk, col(0)), pl.BlockSpec(blk, colp(1)), pl.BlockSpec(blk, col(1)),
                pl.BlockSpec(blk, colp(2)), pl.BlockSpec(blk, col(2)), osp, osp, osp]
    qv = qkv.reshape(sub, dil * QKV_DIM)
    view = lambda a: a.reshape(sub, dil * ATTN_W)
    dq = pl.pallas_call(body, grid=(dil, nb), in_specs=in_specs, out_specs=osp,
                        out_shape=SDS((sub, dil * ATTN_W), BF16), name=name,
                        compiler_params=_params(("parallel", "parallel")))(qv, qv, qv, qv, qv, view(do), view(lse), view(dvec))
    return dq.reshape(t, ATTN_W)


def _attn_bwd_kv(qkv, do, lse, dvec, group, name):
    _, dil = ATTN_GROUPS[group]
    t = qkv.shape[0]
    sub = t // dil
    nb = sub // BLK
    slopes = _slopes(group)

    def body(kc_ref, vc_ref, qa_ref, qb_ref, doa_ref, dob_ref, la_ref, lb_ref, da_ref, db_ref, dk_ref, dv_ref):
        m = pl.program_id(1)
        qi = lax.broadcasted_iota(jnp.int32, (2 * BLK, BLK), 0)
        kj = lax.broadcasted_iota(jnp.int32, (2 * BLK, BLK), 1)
        dist = qi - kj
        valid = (dist >= 0) & (dist <= BLK) & ((qi < BLK) | (m < nb - 1))
        distf = dist.astype(F32) * float(dil)
        lo_q = _lane_lo(2 * BLK)
        for hp in range(HEAD_PAIRS):
            sl = slice(128 * hp, 128 * hp + 128)
            k2 = kc_ref[:, sl]
            v2 = vc_ref[:, sl]
            q2 = jnp.concatenate([qa_ref[:, sl], qb_ref[:, sl]], axis=0)
            do2 = jnp.concatenate([doa_ref[:, sl], dob_ref[:, sl]], axis=0)
            l2 = jnp.concatenate([la_ref[:, sl], lb_ref[:, sl]], axis=0)
            d2 = jnp.concatenate([da_ref[:, sl], db_ref[:, sl]], axis=0)
            dk2 = jnp.zeros((BLK, 128), F32)
            dv2 = jnp.zeros((BLK, 128), F32)
            for s01 in range(2):
                mq = lo_q if s01 == 0 else ~lo_q
                qm = jnp.where(mq, q2, jnp.zeros_like(q2))
                dom = jnp.where(mq, do2, jnp.zeros_like(do2))
                s = _dot_nt(qm, k2) * ATTN_SCALE - slopes[2 * hp + s01] * distf
                p = jnp.exp(jnp.where(valid, s - l2[:, 64 * s01:64 * s01 + 1], -jnp.inf))
                dv2 = dv2 + _dot_tn(p.astype(BF16), dom)
                dp = _dot_nt(dom, v2)
                ds = p * (dp - d2[:, 64 * s01:64 * s01 + 1])
                dk2 = dk2 + _dot_tn(ds.astype(BF16), qm) * ATTN_SCALE
            dk_ref[:, sl] = dk2.astype(BF16)
            dv_ref[:, sl] = dv2.astype(BF16)

    ncol = QKV_DIM // ATTN_W
    blk = (BLK, ATTN_W)

    def col(which):
        return lambda r, m: (m, r * ncol + 3 * which + group)

    cur = pl.BlockSpec(blk, lambda r, m: (m, r))
    nxt = pl.BlockSpec(blk, lambda r, m: (jnp.minimum(m + 1, nb - 1), r))
    qn = pl.BlockSpec(blk, lambda r, m: (jnp.minimum(m + 1, nb - 1), r * ncol + group))
    in_specs = [pl.BlockSpec(blk, col(1)), pl.BlockSpec(blk, col(2)), pl.BlockSpec(blk, col(0)), qn,
                cur, nxt, cur, nxt, cur, nxt]
    qv = qkv.reshape(sub, dil * QKV_DIM)
    view = lambda a: a.reshape(sub, dil * ATTN_W)
    dov, lv, dvv = view(do), view(lse), view(dvec)
    dk, dv = pl.pallas_call(body, grid=(dil, nb), in_specs=in_specs, out_specs=[cur, cur],
                            out_shape=[SDS((sub, dil * ATTN_W), BF16)] * 2, name=name,
                            compiler_params=_params(("parallel", "parallel")))(qv, qv, qv, qv, dov, dov, lv, lv, dvv, dvv)
    return dk.reshape(t, ATTN_W), dv.reshape(t, ATTN_W)


def _combine_weights(l0, l1, l2):
    mx = jnp.maximum(jnp.maximum(l0, l1), l2)
    e0, e1, e2 = jnp.exp(l0 - mx), jnp.exp(l1 - mx), jnp.exp(l2 - mx)
    den = e0 + e1 + e2
    return e0 / den, e1 / den, e2 / den


def _combine_fwd(os_, ls_, name):
    t = os_[0].shape[0]

    def body(o0, o1, o2, l0, l1, l2, out):
        w0, w1, w2 = _combine_weights(l0[...], l1[...], l2[...])
        out[...] = (w0 * o0[...] + w1 * o1[...] + w2 * o2[...]).astype(BF16)

    row = pl.BlockSpec((ROW_TILE, ATTN_W), lambda i: (i, 0))
    return pl.pallas_call(body, grid=(t // ROW_TILE,), in_specs=[row] * 6, out_specs=row, out_shape=SDS((t, ATTN_W), BF16),
                          name=name, compiler_params=_params(("parallel",)))(*os_, *ls_)


def _combine_bwd(do, os_, ls_, name):
    t = do.shape[0]

    def body(do_ref, o0, o1, o2, l0, l1, l2, g0, g1, g2, d0, d1, d2):
        w0, w1, w2 = _combine_weights(l0[...], l1[...], l2[...])
        dov = do_ref[...]
        prod = dov * (w0 * o0[...] + w1 * o1[...] + w2 * o2[...])
        r = lax.broadcasted_iota(jnp.int32, (128, 128), 0) // 64
        c = lax.broadcasted_iota(jnp.int32, (128, 128), 1) // 64
        same = (r == c).astype(F32)
        tbar = jnp.concatenate([_dot_hi(prod[:, 128 * k:128 * k + 128], same) for k in range(HEAD_PAIRS)], axis=1)
        for w, g, d in ((w0, g0, d0), (w1, g1, d1), (w2, g2, d2)):
            g[...] = (w * dov).astype(BF16)
            d[...] = w * tbar

    row = pl.BlockSpec((ROW_TILE, ATTN_W), lambda i: (i, 0))
    return pl.pallas_call(body, grid=(t // ROW_TILE,), in_specs=[row] * 7, out_specs=[row] * 6,
                          out_shape=[SDS((t, ATTN_W), BF16)] * 3 + [SDS((t, ATTN_W), F32)] * 3, name=name,
                          compiler_params=_params(("parallel",)))(do, *os_, *ls_)


def _peer(k):
    x, y, c = lax.axis_index("x"), lax.axis_index("y"), lax.axis_index("c")
    px = 1 - x if k & 4 else x
    py = 1 - y if k & 2 else y
    pc = 1 - c if k & 1 else c
    return (px, py, pc), 4 * px + 2 * py + pc


def _my_index():
    return 4 * lax.axis_index("x") + 2 * lax.axis_index("y") + lax.axis_index("c")


def _all_gather(shards, name):
    n = len(shards)

    def body(*refs):
        ins, outs = refs[:n], refs[n:2 * n]
        send, recv, local = refs[2 * n:]
        me = _my_index()
        own = [pltpu.make_async_copy(ins[i], outs[i].at[me], local.at[i]) for i in range(n)]
        for cp in own:
            cp.start()
        remote = []
        for i in range(n):
            for k in range(1, NDEV):
                peer, _ = _peer(k)
                s = i * (NDEV - 1) + k - 1
                cp = pltpu.make_async_remote_copy(src_ref=ins[i], dst_ref=outs[i].at[me], send_sem=send.at[s],
                                                  recv_sem=recv.at[s], device_id=peer, device_id_type=MESH)
                cp.start()
                remote.append((i, k, s, cp))
        for i, k, s, cp in remote:
            cp.wait_send()
        for i, k, s, cp in remote:
            peer, pidx = _peer(k)
            pltpu.make_async_remote_copy(src_ref=ins[i], dst_ref=outs[i].at[pidx], send_sem=send.at[s],
                                         recv_sem=recv.at[s], device_id=peer, device_id_type=MESH).wait_recv()
        for cp in own:
            cp.wait()

    hbm = pl.BlockSpec(memory_space=pl.ANY)
    return pl.pallas_call(
        body, in_specs=[hbm] * n, out_specs=[hbm] * n,
        out_shape=[SDS((NDEV,) + s.shape, s.dtype) for s in shards],
        scratch_shapes=[pltpu.SemaphoreType.DMA((n * (NDEV - 1),)), pltpu.SemaphoreType.DMA((n * (NDEV - 1),)),
                        pltpu.SemaphoreType.DMA((n,))],
        name=name, compiler_params=pltpu.CompilerParams(has_side_effects=True))(*shards)


def _all_to_all(parts, name):
    n = len(parts)

    def body(*refs):
        ins, outs = refs[:n], refs[n:2 * n]
        send, recv, local = refs[2 * n:]
        me = _my_index()
        own = [pltpu.make_async_copy(ins[i].at[me], outs[i].at[me], local.at[i]) for i in range(n)]
        for cp in own:
            cp.start()
        remote = []
        for i in range(n):
            for k in range(1, NDEV):
                peer, pidx = _peer(k)
                s = i * (NDEV - 1) + k - 1
                cp = pltpu.make_async_remote_copy(src_ref=ins[i].at[pidx], dst_ref=outs[i].at[me], send_sem=send.at[s],
                                                  recv_sem=recv.at[s], device_id=peer, device_id_type=MESH)
                cp.start()
                remote.append((i, k, s, cp))
        for i, k, s, cp in remote:
            cp.wait_send()
        for i, k, s, cp in remote:
            peer, pidx = _peer(k)
            pltpu.make_async_remote_copy(src_ref=ins[i].at[pidx], dst_ref=outs[i].at[pidx], send_sem=send.at[s],
                                         recv_sem=recv.at[s], device_id=peer, device_id_type=MESH).wait_recv()
        for cp in own:
            cp.wait()

    hbm = pl.BlockSpec(memory_space=pl.ANY)
    return pl.pallas_call(
        body, in_specs=[hbm] * n, out_specs=[hbm] * n, out_shape=[SDS(p.shape, p.dtype) for p in parts],
        scratch_shapes=[pltpu.SemaphoreType.DMA((n * (NDEV - 1),)), pltpu.SemaphoreType.DMA((n * (NDEV - 1),)),
                        pltpu.SemaphoreType.DMA((n,))],
        name=name, compiler_params=pltpu.CompilerParams(has_side_effects=True))(*parts)


def _all_reduce_small(v, name):
    rows = v.shape[0]

    def body(v_ref, out_ref, land, send, recv):
        me = _my_index()
        land[me] = v_ref[...]
        remote = []
        for k in range(1, NDEV):
            peer, _ = _peer(k)
            cp = pltpu.make_async_remote_copy(src_ref=v_ref, dst_ref=land.at[me], send_sem=send.at[k - 1],
                                              recv_sem=recv.at[k - 1], device_id=peer, device_id_type=MESH)
            cp.start()
            remote.append(cp)
        for cp in remote:
            cp.wait_send()
        for k in range(1, NDEV):
            peer, pidx = _peer(k)
            pltpu.make_async_remote_copy(src_ref=v_ref, dst_ref=land.at[pidx], send_sem=send.at[k - 1],
                                         recv_sem=recv.at[k - 1], device_id=peer, device_id_type=MESH).wait_recv()
        total = land[0]
        for d in range(1, NDEV):
            total = total + land[d]
        out_ref[...] = total

    vm = pl.BlockSpec(memory_space=pltpu.VMEM)
    return pl.pallas_call(
        body, in_specs=[vm], out_specs=vm, out_shape=SDS((rows, 128), F32),
        scratch_shapes=[pltpu.VMEM((NDEV, rows, 128), F32), pltpu.SemaphoreType.DMA((NDEV - 1,)),
                        pltpu.SemaphoreType.DMA((NDEV - 1,))],
        name=name)(v)


def _adamw_math(w, g, m, v):
    m = ADAM_B1 * m + (1.0 - ADAM_B1) * g
    v = ADAM_B2 * v + (1.0 - ADAM_B2) * (g * g)
    m_hat = m / (1.0 - ADAM_B1 ** ADAM_STEP)
    v_hat = v / (1.0 - ADAM_B2 ** ADAM_STEP)
    delta = -ADAM_LR * (m_hat / (jnp.sqrt(v_hat) + ADAM_EPS) + ADAM_WD * w)
    return delta, m, v


def _row_tile(rows, cols):
    tr = rows
    while tr * cols * 4 > (1 << 20) and tr % 16 == 0:
        tr //= 2
    return tr


def _adamw(g, w, m, v, name):
    rows, cols = w.shape
    tr = _row_tile(rows, cols)

    def body(g_ref, w_ref, m_ref, v_ref, d_out, m_out, v_out):
        d, mn, vn = _adamw_math(w_ref[...], g_ref[...], m_ref[...], v_ref[...])
        d_out[...] = d
        m_out[...] = mn
        v_out[...] = vn

    sp = pl.BlockSpec((tr, cols), lambda i: (i, 0))
    return pl.pallas_call(body, grid=(rows // tr,), in_specs=[sp] * 4, out_specs=[sp] * 3,
                          out_shape=[SDS((rows, cols), F32)] * 3, name=name, compiler_params=_params(("parallel",)))(g, w, m, v)


def _reduce_adamw(parts, w, m, v, name):
    rows, cols = w.shape
    tr = _row_tile(rows, cols)

    def body(p_ref, w_ref, m_ref, v_ref, g_out, d_out, m_out, v_out):
        g = p_ref[0].astype(F32)
        for d in range(1, NDEV):
            g = g + p_ref[d].astype(F32)
        g_out[...] = g
        dl, mn, vn = _adamw_math(w_ref[...], g, m_ref[...], v_ref[...])
        d_out[...] = dl
        m_out[...] = mn
        v_out[...] = vn

    sp = pl.BlockSpec((tr, cols), lambda i: (i, 0))
    psp = pl.BlockSpec((NDEV, tr, cols), lambda i: (0, i, 0))
    return pl.pallas_call(body, grid=(rows // tr,), in_specs=[psp, sp, sp, sp], out_specs=[sp] * 4,
                          out_shape=[SDS((rows, cols), F32)] * 4, name=name, compiler_params=_params(("parallel",)))(parts, w, m, v)


def _pack(items):
    rows = []
    for a in items:
        a = a.reshape(-1).astype(F32)
        pad = (-a.shape[0]) % 128
        rows.append(jnp.pad(a, (0, pad)).reshape(-1, 128))
    out = jnp.concatenate(rows, axis=0)
    return jnp.pad(out, ((0, (-out.shape[0]) % 8), (0, 0)))


def _unpack(packed, shapes):
    out, r = [], 0
    for shp in shapes:
        n = math.prod(shp)
        nr = -(-n // 128)
        out.append(packed[r:r + nr].reshape(-1)[:n].reshape(shp))
        r += nr
    return out


def _ident(t):
    return (t,)


def _add(t, res):
    return (t + res,)


def kernel(x, norm_mix, norm_mlp, ssm_w_in, ssm_conv_w, ssm_conv_b, ssm_dt_bias, ssm_a_log, ssm_d, ssm_norm_w, ssm_w_out, attn_w_qkv, attn_w_o, mlp_w1, mlp_w2, final_norm, loss_target, m_norm_mix, m_norm_mlp, m_ssm_w_in, m_ssm_conv_w, m_ssm_conv_b, m_ssm_dt_bias, m_ssm_a_log, m_ssm_d, m_ssm_norm_w, m_ssm_w_out, m_attn_w_qkv, m_attn_w_o, m_mlp_w1, m_mlp_w2, m_final_norm, v_norm_mix, v_norm_mlp, v_ssm_w_in, v_ssm_conv_w, v_ssm_conv_b, v_ssm_dt_bias, v_ssm_a_log, v_ssm_d, v_ssm_norm_w, v_ssm_w_out, v_attn_w_qkv, v_attn_w_o, v_mlp_w1, v_mlp_w2, v_final_norm):
    t = x.shape[1]
    x0 = x.reshape(t, D_MODEL)
    tgt = loss_target.reshape(t, D_MODEL)
    me = _my_index()
    in_dim = D_INNER + CONV_DIM + SSM_HEADS
    in_shard = in_dim // NDEV
    zx_dim = D_INNER + CONV_DIM

    g_in, g_out, g_qkv, g_o, g_w1a, g_w1b, g_w2a, g_w2b, g_cw = _all_gather(
        [ssm_w_in[0].astype(BF16), ssm_w_out[0].astype(BF16), attn_w_qkv[0].astype(BF16), attn_w_o[0].astype(BF16),
         mlp_w1[0].astype(BF16), mlp_w1[1].astype(BF16), mlp_w2[0].astype(BF16), mlp_w2[1].astype(BF16),
         ssm_conv_w[0]], "gather_weights")
    w_in = jnp.transpose(g_in, (1, 0, 2)).reshape(D_MODEL, in_dim)
    w_z, w_x = w_in[:, :D_INNER], w_in[:, D_INNER:zx_dim]
    w_dt = jnp.pad(w_in[:, zx_dim:], ((0, 0), (0, 128 - SSM_HEADS)))
    conv_w = jnp.transpose(g_cw, (1, 0, 2)).reshape(CONV_WIDTH, CONV_DIM)
    g_w1, g_w2 = (g_w1a, g_w1b), (g_w2a, g_w2b)

    def lanes(p):
        return jnp.pad(p.reshape(SSM_GROUPS, 1, 8), ((0, 0), (0, 0), (0, 120)))

    def rows(p):
        return jnp.broadcast_to(p.reshape(SSM_HEADS, 1), (SSM_HEADS, 128))

    bias_g, bias_r = lanes(ssm_dt_bias[0]), rows(ssm_dt_bias[0])
    alog_g, alog_r = lanes(ssm_a_log[0]), rows(ssm_a_log[0])
    d_exp = jnp.repeat(ssm_d[0], SSM_HEAD_DIM).reshape(1, D_INNER)
    norm_w = ssm_norm_w

    def relu2(tot):
        r = jnp.maximum(tot, 0.0)
        return r, r * r

    def mlp_fwd(xin, layer, tag):
        h = _rmsnorm_fwd(xin, norm_mlp[layer:layer + 1], f"norm_mlp{tag}")
        r, a = _mm_fwd(h, g_w1[layer], "cols", D_MODEL, D_FF, epi=relu2, outs=(BF16, BF16), name=f"mlp_up{tag}")
        xout, = _mm_fwd(a, g_w2[layer], "rows", D_FF, D_MODEL, epi=_add, outs=(F32,), extras=(xin,), name=f"mlp_down{tag}")
        return h, r, a, xout

    h0 = _rmsnorm_fwd(x0, norm_mix[0:1], "norm_mix0")
    z, = _mm_fwd(h0, w_z, "plain", D_MODEL, D_INNER, epi=_ident, outs=(F32,), name="ssm_in_z")
    xpre, = _mm_fwd(h0, w_x, "plain", D_MODEL, CONV_DIM, epi=_ident, outs=(F32,), name="ssm_in_x")
    dtp, = _mm_fwd(h0, w_dt, "plain", D_MODEL, 128, epi=_ident, outs=(F32,), name="ssm_in_dt")
    xbc = _conv_fwd(xpre, conv_w, ssm_conv_b, "conv_fwd")
    dtp64 = dtp[:, :SSM_HEADS]
    dtp_g = jnp.pad(jnp.transpose(dtp64.reshape(t, SSM_GROUPS, 8), (1, 0, 2)), ((0, 0), (0, 0), (0, 120)))
    dtp_t = jnp.transpose(dtp64)
    y_ssd, yn, states = _ssd_fwd(xbc, z, dtp_g, dtp_t, bias_g, bias_r, alog_g, alog_r, d_exp, norm_w, "ssd_fwd")
    x1, = _mm_fwd(yn, g_out, "rows", D_INNER, D_MODEL, epi=_add, outs=(F32,), extras=(x0,), tk=512, name="ssm_out")
    h1, r1, a1, x2 = mlp_fwd(x1, 0, "0")

    h2 = _rmsnorm_fwd(x2, norm_mix[1:2], "norm_mix1")
    qkv, = _mm_fwd(h2, g_qkv, "cols", D_MODEL, QKV_DIM, epi=_ident, outs=(BF16,), tn=QKV_DIM // NDEV, name="attn_qkv")
    att = [_attn_fwd(qkv, g, f"attn_fwd{g}") for g in range(3)]
    os_, ls_ = [a[0] for a in att], [a[1] for a in att]
    o_mix = _combine_fwd(os_, ls_, "attn_combine")
    x3, = _mm_fwd(o_mix, g_o, "cols", ATTN_W, D_MODEL, epi=_add, outs=(F32,), extras=(x2,), tn=D_MODEL // NDEV, name="attn_out")
    h3, r3, a3, x4 = mlp_fwd(x3, 1, "1")

    dx4, dx4b, loss_acc, d_final = _loss_head(x4, tgt, final_norm.reshape(1, D_MODEL), "loss_head")

    def mlp_bwd(xin, h, r, a, dxo, dxob, layer, tag):
        du, = _mm_dx(dxob, g_w2[layer], "rows", D_FF, D_MODEL, epi=lambda tot, rr: (tot * (2.0 * rr.astype(F32)),),
                     outs=(BF16,), extras=(r,), name=f"mlp_down_dx{tag}")
        dw2 = _mm_dw(a, dxob, "rows", name=f"mlp_down_dw{tag}")
        dw1 = _mm_dw(h, du, "cols", name=f"mlp_up_dw{tag}")
        dh, = _mm_dx(du, g_w1[layer], "cols", D_MODEL, D_FF, epi=_ident, outs=(F32,), name=f"mlp_up_dx{tag}")
        dxi, dxib, dg = _rmsnorm_bwd(xin, norm_mlp[layer:layer + 1], dh, dxo, f"norm_mlp_bwd{tag}")
        return dxi, dxib, dg, dw1, dw2

    dx3, dx3b, dg_mlp1, dw1_1, dw2_1 = mlp_bwd(x3, h3, r3, a3, dx4, dx4b, 1, "1")

    dw_o = _mm_dw(o_mix, dx3b, "cols", tn=D_MODEL // NDEV, name="attn_out_dw")
    do, = _mm_dx(dx3b, g_o, "cols", ATTN_W, D_MODEL, epi=_ident, outs=(F32,), tk=D_MODEL // NDEV, name="attn_out_dx")
    cb = _combine_bwd(do, os_, ls_, "attn_combine_bwd")
    dos, dvecs = cb[:3], cb[3:]
    dqs, dks, dvs = [], [], []
    for g in range(3):
        dqs.append(_attn_bwd_q(qkv, dos[g], ls_[g], dvecs[g], g, f"attn_bwd_q{g}"))
        dk, dv = _attn_bwd_kv(qkv, dos[g], ls_[g], dvecs[g], g, f"attn_bwd_kv{g}")
        dks.append(dk)
        dvs.append(dv)
    dqkv = jnp.concatenate(dqs + dks + dvs, axis=1)
    dw_qkv = _mm_dw(h2, dqkv, "cols", tn=QKV_DIM // NDEV, name="attn_qkv_dw")
    dh2, = _mm_dx(dqkv, g_qkv, "cols", D_MODEL, QKV_DIM, epi=_ident, outs=(F32,), tk=QKV_DIM // NDEV, name="attn_qkv_dx")
    dx2, dx2b, dg_mix1 = _rmsnorm_bwd(x2, norm_mix[1:2], dh2, dx3, "norm_mix_bwd1")

    dx1, dx1b, dg_mlp0, dw1_0, dw2_0 = mlp_bwd(x1, h1, r1, a1, dx2, dx2b, 0, "0")

    dw_out = _mm_dw(yn, dx1b, "rows", tm=512, name="ssm_out_dw")
    dyn, = _mm_dx(dx1b, g_out, "rows", D_INNER, D_MODEL, epi=_ident, outs=(F32,), tn=512, name="ssm_out_dx")
    dxs, d_b, d_c, dz, ddtp_g, hsums, csums = _ssd_bwd(dyn, y_ssd, xbc, z, states, dtp_g, dtp_t, bias_g, bias_r,
                                                       alog_g, alog_r, d_exp, norm_w, "ssd_bwd")
    dxbc = jnp.concatenate([dxs, d_b, d_c], axis=1)
    dpre, conv_sums = _conv_bwd_pre(xpre, conv_w, ssm_conv_b, dxbc, "conv_bwd_pre")
    du = _conv_bwd_in(dpre, conv_w, "conv_bwd_in")
    ddtp = jnp.transpose(ddtp_g[:, :, :8], (1, 0, 2)).reshape(t, SSM_HEADS)
    ddtp = jnp.pad(ddtp, ((0, 0), (0, 128 - SSM_HEADS))).astype(BF16)
    dw_z = _mm_dw(h0, dz, "plain", name="ssm_in_z_dw")
    dw_x = _mm_dw(h0, du, "plain", name="ssm_in_x_dw")
    dw_dt = _mm_dw(h0, ddtp, "plain", name="ssm_in_dt_dw")
    dh0, = _mm_dx(dz, w_z, "plain", D_MODEL, D_INNER, epi=_ident, outs=(F32,), name="ssm_in_z_dx")
    dh0, = _mm_dx(du, w_x, "plain", D_MODEL, CONV_DIM, epi=_add, outs=(F32,), extras=(dh0,), name="ssm_in_x_dx")
    dh0, = _mm_dx(ddtp, w_dt, "plain", D_MODEL, 128, epi=_add, outs=(F32,), extras=(dh0,), name="ssm_in_dt_dx")
    dx0, _, dg_mix0 = _rmsnorm_bwd(x0, norm_mix[0:1], dh0, dx1, "norm_mix_bwd0")

    dw_in = jnp.concatenate([dw_z, dw_x, dw_dt[:, :SSM_HEADS]], axis=1)
    dw_in = jnp.transpose(dw_in.reshape(D_MODEL, NDEV, in_shard), (1, 0, 2))
    parts = _all_to_all([dw_in, dw_out, dw_qkv, dw_o, dw1_0, dw1_1, dw2_0, dw2_1], "scatter_grads")

    def own(w, mm, vv):
        shp = w.shape
        f = lambda a: a.reshape(-1, shp[-1])
        return f(w), f(mm), f(vv), shp

    big = {}
    for key, part, (w, mm, vv) in (
            ("ssm_w_in", parts[0], (ssm_w_in, m_ssm_w_in, v_ssm_w_in)),
            ("ssm_w_out", parts[1], (ssm_w_out, m_ssm_w_out, v_ssm_w_out)),
            ("attn_w_qkv", parts[2], (attn_w_qkv, m_attn_w_qkv, v_attn_w_qkv)),
            ("attn_w_o", parts[3], (attn_w_o, m_attn_w_o, v_attn_w_o))):
        w2, m2, v2, shp = own(w, mm, vv)
        res = _reduce_adamw(part, w2, m2, v2, f"adamw_{key}")
        big[key] = [r.reshape(shp) for r in res]
    for key, pa, pb, (w, mm, vv) in (("mlp_w1", parts[4], parts[5], (mlp_w1, m_mlp_w1, v_mlp_w1)),
                                     ("mlp_w2", parts[6], parts[7], (mlp_w2, m_mlp_w2, v_mlp_w2))):
        res = [_reduce_adamw(p, w[l], mm[l], vv[l], f"adamw_{key}_{l}") for l, p in enumerate((pa, pb))]
        big[key] = [jnp.stack([res[0][i], res[1][i]], axis=0) for i in range(4)]

    d_norm_mix = jnp.concatenate([dg_mix0, dg_mix1], axis=0)
    d_norm_mlp = jnp.concatenate([dg_mlp0, dg_mlp1], axis=0)
    d_conv_b = conv_sums[4:5]
    d_conv_w = conv_sums[0:4]
    head = hsums[:, :, :8]
    d_dt_bias, d_a_log, d_d = (head[:, k, :].reshape(1, SSM_HEADS) for k in range(3))
    d_ssm_norm = csums[0:1]
    small = [d_norm_mix, d_norm_mlp, d_conv_b, d_dt_bias, d_a_log, d_d, d_ssm_norm, d_final, d_conv_w, loss_acc[0:1, 0:1]]
    shapes = [a.shape for a in small]
    summed = _unpack(_all_reduce_small(_pack(small), "reduce_small"), shapes)
    g_conv_w_full = summed[8]
    loss = summed[9].reshape(())
    g_conv_w = lax.dynamic_slice(g_conv_w_full, (0, me * (CONV_DIM // NDEV)), (CONV_WIDTH, CONV_DIM // NDEV))

    small_names = ["norm_mix", "norm_mlp", "ssm_conv_b", "ssm_dt_bias", "ssm_a_log", "ssm_d", "ssm_norm_w", "final_norm"]
    small_w = [norm_mix, norm_mlp, ssm_conv_b, ssm_dt_bias, ssm_a_log, ssm_d, ssm_norm_w, final_norm, ssm_conv_w]
    small_m = [m_norm_mix, m_norm_mlp, m_ssm_conv_b, m_ssm_dt_bias, m_ssm_a_log, m_ssm_d, m_ssm_norm_w, m_final_norm, m_ssm_conv_w]
    small_v = [v_norm_mix, v_norm_mlp, v_ssm_conv_b, v_ssm_dt_bias, v_ssm_a_log, v_ssm_d, v_ssm_norm_w, v_final_norm, v_ssm_conv_w]
    small_g = [summed[i].reshape(small_w[i].shape) for i in range(8)] + [g_conv_w.reshape(ssm_conv_w.shape)]
    wshapes = [a.shape for a in small_w]
    sd, sm, sv = _adamw(_pack(small_g), _pack(small_w), _pack(small_m), _pack(small_v), "adamw_small")
    sd, sm, sv = _unpack(sd, wshapes), _unpack(sm, wshapes), _unpack(sv, wshapes)
    res = {n: (small_g[i], sd[i], sm[i], sv[i]) for i, n in enumerate(small_names + ["ssm_conv_w"])}
    for n in big:
        res[n] = tuple(big[n])

    order = ["norm_mix", "norm_mlp", "ssm_w_in", "ssm_conv_w", "ssm_conv_b", "ssm_dt_bias", "ssm_a_log", "ssm_d",
             "ssm_norm_w", "ssm_w_out", "attn_w_qkv", "attn_w_o", "mlp_w1", "mlp_w2", "final_norm"]
    outs = [loss, dx0.reshape(x.shape)]
    for kind in range(4):
        outs += [res[n][kind] for n in order]
    return tuple(outs)
```

```python
import math

import jax
import jax.numpy as jnp
from jax import lax
from jax.experimental import pallas as pl
from jax.experimental.pallas import tpu as pltpu

F32, BF16 = jnp.float32, jnp.bfloat16
SDS = jax.ShapeDtypeStruct
MESH = pl.DeviceIdType.MESH
HIGHEST = lax.Precision.HIGHEST

NDEV = 8
D_MODEL = 2048
D_INNER = 4096
SSM_HEADS = 64
SSM_HEAD_DIM = 64
SSM_GROUPS = 8
SSM_STATE = 128
CHUNK = 128
CONV_DIM = 6144
CONV_WIDTH = 4
GROUP_W = D_INNER // SSM_GROUPS
ATTN_GROUPS = ((128, 1), (512, 4), (2048, 16))
ATTN_W = 1024
QKV_DIM = 9216
D_FF = 8192
EPS = 1e-5
ADAM_LR, ADAM_B1, ADAM_B2, ADAM_EPS, ADAM_WD, ADAM_STEP = 0.001, 0.9, 0.999, 1e-08, 0.01, 10

VMEM_LIMIT = 48 * 1024 * 1024


def _params(sem):
    return pltpu.CompilerParams(dimension_semantics=sem, vmem_limit_bytes=VMEM_LIMIT)


def _sigmoid(v):
    return 1.0 / (1.0 + jnp.exp(-v))


def _softplus(v):
    return jnp.maximum(v, 0.0) + jnp.log1p(jnp.exp(-jnp.abs(v)))


def _wspec(layout, kw, nw, tr, tc, sel):
    if layout == "plain":
        return pl.BlockSpec((tr, tc), lambda *g: sel(*g))
    per = (nw // NDEV) // tc
    return pl.BlockSpec((None, tr, tc), lambda *g: (sel(*g)[1] // per, sel(*g)[0], sel(*g)[1] % per))


def _wshape(layout, kw, nw):
    return (kw, nw) if layout == "plain" else (NDEV, kw, nw // NDEV)


def _mm_call(name, grid, in_specs, out_specs, out_shape, dims, n_extra, epi, tm, tn, carry):
    nk = grid[2]
    kind, moved = carry if carry else (None, ())
    nc = len(moved)
    n_out = len(out_shape)

    def body(*refs):
        a_ref, b_ref = refs[0], refs[1]
        extra = refs[2:2 + n_extra]
        c_in = refs[2 + n_extra:2 + n_extra + nc]
        outs = refs[2 + n_extra + nc:2 + n_extra + nc + n_out]
        c_out = refs[2 + n_extra + nc + n_out:2 + n_extra + 2 * nc + n_out]
        acc = refs[2 + n_extra + 2 * nc + n_out]
        sems = refs[3 + n_extra + 2 * nc + n_out:]
        i, j, k = pl.program_id(0), pl.program_id(1), pl.program_id(2)
        if nc:
            @pl.when((i == 0) & (j == 0) & (k == 0))
            def _():
                _exchange_start(kind, c_in, c_out, *sems)

        d = lax.dot_general(a_ref[...], b_ref[...], (dims, ((), ())), preferred_element_type=F32)

        def finish(total):
            vals = epi(total, *[e[...] for e in extra])
            for o, v in zip(outs, vals):
                o[...] = v.astype(o.dtype)

        if nk == 1:
            finish(d)
        else:
            @pl.when(k == 0)
            def _():
                acc[...] = d

            @pl.when(jnp.logical_and(k > 0, k < nk - 1))
            def _():
                acc[...] += d

            @pl.when(k == nk - 1)
            def _():
                finish(acc[...] + d)

        if nc:
            @pl.when((i == grid[0] - 1) & (j == grid[1] - 1) & (k == nk - 1))
            def _():
                _exchange_wait(kind, c_in, c_out, *sems)

    hbm = pl.BlockSpec(memory_space=pl.ANY)
    scratch = [pltpu.VMEM((tm, tn), F32)] + (_exchange_sems(nc) if nc else [])
    c_shape = [SDS(((NDEV,) + t.shape) if kind == "gather" else t.shape, t.dtype) for t in moved]
    sem = ("arbitrary",) * 3 if nc else ("parallel", "parallel", "arbitrary")
    return pl.pallas_call(
        body, grid=grid, in_specs=in_specs + [hbm] * nc, out_specs=out_specs + [hbm] * nc,
        out_shape=out_shape + c_shape, scratch_shapes=scratch, name=name, compiler_params=_params(sem))


def _mm_fwd(a, w, layout, kw, nw, *, epi, outs, extras=(), tm=1024, tn=1024, tk=2048, carry=None, name):
    m = a.shape[0]
    tm, tn, tk = min(tm, m), min(tn, nw), min(tk, kw)
    grid = (m // tm, nw // tn, kw // tk)
    o_spec = pl.BlockSpec((tm, tn), lambda i, j, k: (i, j))
    in_specs = [pl.BlockSpec((tm, tk), lambda i, j, k: (i, k)), _wspec(layout, kw, nw, tk, tn, lambda i, j, k: (k, j))]
    in_specs += [o_spec] * len(extras)
    call = _mm_call(name, grid, in_specs, [o_spec] * len(outs), [SDS((m, nw), dt) for dt in outs], ((1,), (0,)),
                    len(extras), epi, tm, tn, carry)
    return call(a, w, *extras, *(carry[1] if carry else ()))


def _mm_dx(g, w, layout, kw, nw, *, epi, outs, extras=(), tm=1024, tn=1024, tk=2048, carry=None, name):
    m = g.shape[0]
    tm, tn, tk = min(tm, m), min(tn, kw), min(tk, nw if layout == "plain" else nw // NDEV)
    grid = (m // tm, kw // tn, nw // tk)
    o_spec = pl.BlockSpec((tm, tn), lambda i, j, k: (i, j))
    in_specs = [pl.BlockSpec((tm, tk), lambda i, j, k: (i, k)), _wspec(layout, kw, nw, tn, tk, lambda i, j, k: (j, k))]
    in_specs += [o_spec] * len(extras)
    call = _mm_call(name, grid, in_specs, [o_spec] * len(outs), [SDS((m, kw), dt) for dt in outs], ((1,), (1,)),
                    len(extras), epi, tm, tn, carry)
    return call(g, w, *extras, *(carry[1] if carry else ()))


def _mm_dw(a, g, layout, *, tm=1024, tn=1024, tk=2048, carry=None, name):
    m, kw = a.shape
    nw = g.shape[1]
    tm, tn, tk = min(tm, kw), min(tn, nw if layout == "plain" else nw // NDEV), min(tk, m)
    grid = (kw // tm, nw // tn, m // tk)
    in_specs = [pl.BlockSpec((tk, tm), lambda i, j, k: (k, i)), pl.BlockSpec((tk, tn), lambda i, j, k: (k, j))]
    o_spec = _wspec(layout, kw, nw, tm, tn, lambda i, j, k: (i, j))
    call = _mm_call(name, grid, in_specs, [o_spec], [SDS(_wshape(layout, kw, nw), BF16)], ((0,), (0,)),
                    0, lambda t: (t,), tm, tn, carry)
    return call(a, g, *(carry[1] if carry else ()))


ROW_TILE = 256


def _rmsnorm_fwd(x, g, name):
    t, d = x.shape

    def body(x_ref, g_ref, h_ref):
        xv = x_ref[...]
        r = lax.rsqrt(jnp.mean(xv * xv, axis=-1, keepdims=True) + EPS)
        h_ref[...] = (xv * r * g_ref[...]).astype(BF16)

    row = pl.BlockSpec((ROW_TILE, d), lambda i: (i, 0))
    vec = pl.BlockSpec((1, d), lambda i: (0, 0))
    return pl.pallas_call(body, grid=(t // ROW_TILE,), in_specs=[row, vec], out_specs=row, out_shape=SDS((t, d), BF16),
                          name=name, compiler_params=_params(("parallel",)))(x, g)


def _rmsnorm_bwd(x, g, dh, dres, name):
    t, d = x.shape

    def body(x_ref, g_ref, dh_ref, dres_ref, dx_ref, dxb_ref, dg_ref):
        xv = x_ref[...]
        r = lax.rsqrt(jnp.mean(xv * xv, axis=-1, keepdims=True) + EPS)
        xh = xv * r
        dhv = dh_ref[...]
        gd = dhv * g_ref[...]
        dx = dres_ref[...] + r * (gd - xh * jnp.mean(gd * xh, axis=-1, keepdims=True))
        dx_ref[...] = dx
        dxb_ref[...] = dx.astype(BF16)
        part = jnp.sum(dhv * xh, axis=0, keepdims=True)

        @pl.when(pl.program_id(0) == 0)
        def _():
            dg_ref[...] = part

        @pl.when(pl.program_id(0) > 0)
        def _():
            dg_ref[...] += part

    row = pl.BlockSpec((ROW_TILE, d), lambda i: (i, 0))
    vec = pl.BlockSpec((1, d), lambda i: (0, 0))
    return pl.pallas_call(body, grid=(t // ROW_TILE,), in_specs=[row, vec, row, row], out_specs=[row, row, vec],
                          out_shape=[SDS((t, d), F32), SDS((t, d), BF16), SDS((1, d), F32)], name=name,
                          compiler_params=_params(("arbitrary",)))(x, g, dh, dres)


def _loss_head(x, tgt, g, name):
    t, d = x.shape

    def body(x_ref, t_ref, g_ref, dx_ref, dxb_ref, loss_ref, dg_ref):
        xv = x_ref[...]
        r = lax.rsqrt(jnp.mean(xv * xv, axis=-1, keepdims=True) + EPS)
        xh = xv * r
        gv = g_ref[...]
        err = xh * gv - t_ref[...]
        part_loss = 0.5 * jnp.sum(jnp.mean(err * err, axis=-1, keepdims=True), axis=0, keepdims=True)
        dy = err * (1.0 / d)
        gd = dy * gv
        dx = r * (gd - xh * jnp.mean(gd * xh, axis=-1, keepdims=True))
        dx_ref[...] = dx
        dxb_ref[...] = dx.astype(BF16)
        part_g = jnp.sum(dy * xh, axis=0, keepdims=True)
        part_l = jnp.broadcast_to(part_loss, (8, 128))

        @pl.when(pl.program_id(0) == 0)
        def _():
            dg_ref[...] = part_g
            loss_ref[...] = part_l

        @pl.when(pl.program_id(0) > 0)
        def _():
            dg_ref[...] += part_g
            loss_ref[...] += part_l

    row = pl.BlockSpec((ROW_TILE, d), lambda i: (i, 0))
    vec = pl.BlockSpec((1, d), lambda i: (0, 0))
    sc = pl.BlockSpec((8, 128), lambda i: (0, 0))
    return pl.pallas_call(body, grid=(t // ROW_TILE,), in_specs=[row, row, vec], out_specs=[row, row, sc, vec],
                          out_shape=[SDS((t, d), F32), SDS((t, d), BF16), SDS((8, 128), F32), SDS((1, d), F32)], name=name,
                          compiler_params=_params(("arbitrary",)))(x, tgt, g)


CONV_ROWS = 512
CONV_COLS = 512


def _shift_down(cur, prev8, k):
    sh = pltpu.roll(cur, k, axis=0)
    ph = pltpu.roll(prev8, k, axis=0)
    rid = lax.broadcasted_iota(jnp.int32, ph.shape, 0)
    head = jnp.where(rid < k, ph, sh[0:8])
    return jnp.concatenate([head, sh[8:]], axis=0)


def _shift_up(cur, next8, k):
    n = cur.shape[0]
    sh = pltpu.roll(cur, n - k, axis=0)
    nh = pltpu.roll(next8, 8 - k, axis=0)
    rid = lax.broadcasted_iota(jnp.int32, nh.shape, 0)
    tail = jnp.where(rid >= 8 - k, nh, sh[n - 8:])
    return jnp.concatenate([sh[:n - 8], tail], axis=0)


def _conv_pre(cur, prev8, w, b):
    acc = w[3:4, :] * cur + b
    for k in range(1, CONV_WIDTH):
        acc = acc + w[3 - k:4 - k, :] * _shift_down(cur, prev8, k)
    return acc


def _conv_fwd(u, w, b, name):
    t, c = u.shape
    per = CONV_ROWS // 8

    def body(u_ref, p_ref, w_ref, b_ref, o_ref):
        prev8 = jnp.where(pl.program_id(1) == 0, 0.0, p_ref[...])
        pre = _conv_pre(u_ref[...], prev8, w_ref[...], b_ref[...])
        o_ref[...] = pre * _sigmoid(pre)

    cur = pl.BlockSpec((CONV_ROWS, CONV_COLS), lambda j, i: (i, j))
    prev = pl.BlockSpec((8, CONV_COLS), lambda j, i: (jnp.maximum(i * per - 1, 0), j))
    wsp = pl.BlockSpec((CONV_WIDTH, CONV_COLS), lambda j, i: (0, j))
    bsp = pl.BlockSpec((1, CONV_COLS), lambda j, i: (0, j))
    return pl.pallas_call(body, grid=(c // CONV_COLS, t // CONV_ROWS), in_specs=[cur, prev, wsp, bsp], out_specs=cur,
                          out_shape=SDS((t, c), F32), name=name, compiler_params=_params(("parallel", "parallel")))(u, u, w, b)


def _conv_bwd_pre(u, w, b, dout, name):
    t, c = u.shape
    per = CONV_ROWS // 8

    def body(u_ref, p_ref, w_ref, b_ref, d_ref, dp_ref, dw_ref):
        i = pl.program_id(1)
        cur = u_ref[...]
        prev8 = jnp.where(i == 0, 0.0, p_ref[...])
        pre = _conv_pre(cur, prev8, w_ref[...], b_ref[...])
        s = _sigmoid(pre)
        dpre = d_ref[...] * (s * (1.0 + pre * (1.0 - s)))
        dp_ref[...] = dpre
        rows = [jnp.sum(dpre * _shift_down(cur, prev8, 3 - k), axis=0, keepdims=True) for k in range(3)]
        rows.append(jnp.sum(dpre * cur, axis=0, keepdims=True))
        rows.append(jnp.sum(dpre, axis=0, keepdims=True))
        part = jnp.concatenate(rows + [jnp.zeros((3, cur.shape[1]), F32)], axis=0)

        @pl.when(i == 0)
        def _():
            dw_ref[...] = part

        @pl.when(i > 0)
        def _():
            dw_ref[...] += part

    cur = pl.BlockSpec((CONV_ROWS, CONV_COLS), lambda j, i: (i, j))
    prev = pl.BlockSpec((8, CONV_COLS), lambda j, i: (jnp.maximum(i * per - 1, 0), j))
    wsp = pl.BlockSpec((CONV_WIDTH, CONV_COLS), lambda j, i: (0, j))
    bsp = pl.BlockSpec((1, CONV_COLS), lambda j, i: (0, j))
    acc = pl.BlockSpec((8, CONV_COLS), lambda j, i: (0, j))
    return pl.pallas_call(body, grid=(c // CONV_COLS, t // CONV_ROWS), in_specs=[cur, prev, wsp, bsp, cur],
                          out_specs=[cur, acc], out_shape=[SDS((t, c), F32), SDS((8, c), F32)], name=name,
                          compiler_params=_params(("parallel", "arbitrary")))(u, u, w, b, dout)


def _conv_bwd_in(dpre, w, name):
    t, c = dpre.shape
    per = CONV_ROWS // 8
    last = t // CONV_ROWS - 1

    def body(d_ref, n_ref, w_ref, o_ref):
        cur = d_ref[...]
        next8 = jnp.where(pl.program_id(1) == last, 0.0, n_ref[...])
        wv = w_ref[...]
        acc = wv[3:4, :] * cur
        for k in range(1, CONV_WIDTH):
            acc = acc + wv[3 - k:4 - k, :] * _shift_up(cur, next8, k)
        o_ref[...] = acc.astype(BF16)

    cur = pl.BlockSpec((CONV_ROWS, CONV_COLS), lambda j, i: (i, j))
    nxt = pl.BlockSpec((8, CONV_COLS), lambda j, i: (jnp.minimum((i + 1) * per, t // 8 - 1), j))
    wsp = pl.BlockSpec((CONV_WIDTH, CONV_COLS), lambda j, i: (0, j))
    return pl.pallas_call(body, grid=(c // CONV_COLS, t // CONV_ROWS), in_specs=[cur, nxt, wsp], out_specs=cur,
                          out_shape=SDS((t, c), BF16), name=name, compiler_params=_params(("parallel", "parallel")))(dpre, dpre, w)


def _head_expand():
    r = lax.broadcasted_iota(jnp.int32, (128, GROUP_W), 0)
    c = lax.broadcasted_iota(jnp.int32, (128, GROUP_W), 1)
    return (c // SSM_HEAD_DIM == r).astype(F32)


def _head_sum():
    r = lax.broadcasted_iota(jnp.int32, (GROUP_W, 128), 0)
    c = lax.broadcasted_iota(jnp.int32, (GROUP_W, 128), 1)
    return (r // SSM_HEAD_DIM == c).astype(F32)


def _dot(a, b):
    return jnp.dot(a, b, preferred_element_type=F32)


def _dot_hi(a, b):
    return jnp.dot(a, b, precision=HIGHEST, preferred_element_type=F32)


def _dot_nt(a, b):
    return lax.dot_general(a, b, (((1,), (1,)), ((), ())), preferred_element_type=F32)


def _dot_tn(a, b):
    return lax.dot_general(a, b, (((0,), (0,)), ((), ())), preferred_element_type=F32)


def _ssd_common(dtp, dtpt, bias, biasr, alog, alogr):
    li = lax.broadcasted_iota(jnp.int32, (CHUNK, CHUNK), 0)
    si = lax.broadcasted_iota(jnp.int32, (CHUNK, CHUNK), 1)
    lower, upper = (li >= si), (li <= si)
    dt = _softplus(dtp + bias)
    a_neg = -jnp.exp(alog)
    cs = _dot_hi(lower.astype(F32), dt * a_neg)
    dtr = _softplus(dtpt + biasr)
    csr = _dot_hi(dtr * (-jnp.exp(alogr)), upper.astype(F32))
    return lower, upper, dt, a_neg, cs, csr


def _head_masked_rows(v):
    hl = lax.broadcasted_iota(jnp.int32, v.shape, 1) // SSM_HEAD_DIM
    return jnp.concatenate([jnp.where(hl == j, v, jnp.zeros_like(v)) for j in range(8)], axis=0)


def _ssd_fwd(xbc, z, dtp_g, dtp_t, bias_g, bias_r, alog_g, alog_r, d_exp, norm_w, name):
    t = xbc.shape[0]
    nc = t // CHUNK

    def body(xs_ref, b_ref, c_ref, z_ref, dtp_ref, dtpt_ref, bias_ref, biasr_ref, alog_ref, alogr_ref, dexp_ref, nw_ref,
             y_ref, yn_ref, st_ref, state):
        c = pl.program_id(1)

        @pl.when(c == 0)
        def _():
            state[...] = jnp.zeros_like(state)

        lower, _, dt, a_neg, cs, csr = _ssd_common(dtp_ref[...], dtpt_ref[...], bias_ref[...], biasr_ref[...],
                                                alog_ref[...], alogr_ref[...])
        expand = _head_expand()
        cs_e = _dot_hi(cs, expand)
        dt_e = _dot_hi(dt, expand)
        xs = xs_ref[...]
        xdt = xs * dt_e
        bm = b_ref[...]
        cm = c_ref[...]
        bmb, cmb = bm.astype(BF16), cm.astype(BF16)
        cb = _dot_nt(cmb, bmb)
        ms = []
        for j in range(8):
            dlt = cs_e[:, SSM_HEAD_DIM * j:SSM_HEAD_DIM * j + 1] - csr[j:j + 1, :]
            ms.append((cb * jnp.exp(jnp.where(lower, dlt, -jnp.inf))).astype(BF16))
        y = _dot(jnp.concatenate(ms, axis=1), _head_masked_rows(xdt.astype(BF16)))
        st_in = state[...]
        st_ref[...] = st_in
        y = y + jnp.exp(cs_e) * _dot(cmb, st_in.astype(BF16))
        cs_last = cs_e[CHUNK - 1:CHUNK, :]
        xdtd = (xdt * jnp.exp(cs_last - cs_e)).astype(BF16)
        state[...] = jnp.exp(cs_last) * st_in + _dot(bm.T.astype(BF16), xdtd)
        y_ref[...] = y
        zz = z_ref[...]
        y2 = (y + dexp_ref[...] * xs) * (zz * _sigmoid(zz))
        r = lax.rsqrt(jnp.mean(y2 * y2, axis=-1, keepdims=True) + EPS)
        yn_ref[...] = (y2 * r * nw_ref[...]).astype(BF16)

    gw = pl.BlockSpec((CHUNK, GROUP_W), lambda g, c: (c, g))
    in_specs = [
        gw,
        pl.BlockSpec((CHUNK, SSM_STATE), lambda g, c: (c, D_INNER // SSM_STATE + g)),
        pl.BlockSpec((CHUNK, SSM_STATE), lambda g, c: (c, D_INNER // SSM_STATE + SSM_GROUPS + g)),
        gw,
        pl.BlockSpec((None, CHUNK, 128), lambda g, c: (g, c, 0)),
        pl.BlockSpec((8, CHUNK), lambda g, c: (g, c)),
        pl.BlockSpec((None, 1, 128), lambda g, c: (g, 0, 0)),
        pl.BlockSpec((8, 128), lambda g, c: (g, 0)),
        pl.BlockSpec((None, 1, 128), lambda g, c: (g, 0, 0)),
        pl.BlockSpec((8, 128), lambda g, c: (g, 0)),
        pl.BlockSpec((1, GROUP_W), lambda g, c: (0, g)),
        pl.BlockSpec((1, GROUP_W), lambda g, c: (0, g)),
    ]
    out_specs = [gw, gw, pl.BlockSpec((None, SSM_STATE, GROUP_W), lambda g, c: (c, 0, g))]
    out_shape = [SDS((t, D_INNER), F32), SDS((t, D_INNER), BF16), SDS((nc, SSM_STATE, D_INNER), F32)]
    return pl.pallas_call(body, grid=(SSM_GROUPS, nc), in_specs=in_specs, out_specs=out_specs, out_shape=out_shape,
                          scratch_shapes=[pltpu.VMEM((SSM_STATE, GROUP_W), F32)], name=name,
                          compiler_params=_params(("parallel", "arbitrary")))(
        xbc, xbc, xbc, z, dtp_g, dtp_t, bias_g, bias_r, alog_g, alog_r, d_exp, norm_w)


def _ssd_bwd(dyn, y, xbc, z, states, dtp_g, dtp_t, bias_g, bias_r, alog_g, alog_r, d_exp, norm_w, name):
    t = xbc.shape[0]
    nc = t // CHUNK

    def body(dyn_ref, y_ref, xs_ref, b_ref, c_ref, z_ref, st_ref, dtp_ref, dtpt_ref, bias_ref, biasr_ref, alog_ref,
             alogr_ref, dexp_ref, nw_ref, dxs_ref, db_ref, dc_ref, dz_ref, ddt_ref, hsum_ref, csum_ref, dstate):
        step = pl.program_id(1)

        @pl.when(step == 0)
        def _():
            dstate[...] = jnp.zeros_like(dstate)

        dtp = dtp_ref[...]
        bias = bias_ref[...]
        lower, upper, dt, a_neg, cs, csr = _ssd_common(dtp, dtpt_ref[...], bias, biasr_ref[...], alog_ref[...],
                                                       alogr_ref[...])
        expand = _head_expand()
        hsum = _head_sum()
        cs_e = _dot_hi(cs, expand)
        dt_e = _dot_hi(dt, expand)
        xs = xs_ref[...]
        xdt = xs * dt_e
        bm = b_ref[...]
        cm = c_ref[...]
        bmb, cmb = bm.astype(BF16), cm.astype(BF16)
        y = y_ref[...]
        dexp = dexp_ref[...]
        nw = nw_ref[...]

        zz = z_ref[...]
        sg = _sigmoid(zz)
        gate = zz * sg
        ytot = y + dexp * xs
        y2 = ytot * gate
        r = lax.rsqrt(jnp.mean(y2 * y2, axis=-1, keepdims=True) + EPS)
        dynv = dyn_ref[...]
        xh = y2 * r
        gn = dynv * nw
        dy2 = r * (gn - xh * jnp.mean(gn * xh, axis=-1, keepdims=True))
        dy = dy2 * gate
        dz_ref[...] = (dy2 * ytot * (sg * (1.0 + zz * (1.0 - sg)))).astype(BF16)
        csum_part = jnp.sum(dynv * xh, axis=0, keepdims=True)
        d_skip = jnp.sum(_dot_hi(dy * xs, hsum), axis=0, keepdims=True)

        cb = _dot_nt(cmb, bmb)
        dyb = dy.astype(BF16)
        xdtb = xdt.astype(BF16)
        dym = _head_masked_rows(dyb)
        dm = _dot_nt(dym, xdtb)
        dmt = _dot_nt(_head_masked_rows(xdtb), dyb)
        lane = lax.broadcasted_iota(jnp.int32, (CHUNK, 128), 1)
        mts = []
        dcb = jnp.zeros((CHUNK, CHUNK), F32)
        dcs = jnp.zeros((CHUNK, 128), F32)
        for j in range(8):
            dlt = cs_e[:, SSM_HEAD_DIM * j:SSM_HEAD_DIM * j + 1] - csr[j:j + 1, :]
            lj = jnp.exp(jnp.where(lower, dlt, -jnp.inf))
            mj = cb * lj
            mjt = mj.T
            mts.append(mjt.astype(BF16))
            dmj = dm[CHUNK * j:CHUNK * (j + 1)]
            dcb = dcb + dmj * lj
            rows = jnp.sum(dmj * mj, axis=1, keepdims=True)
            cols = jnp.sum(dmt[CHUNK * j:CHUNK * (j + 1)] * mjt, axis=1, keepdims=True)
            dcs = dcs + jnp.where(lane == j, rows - cols, 0.0)
        dxdt = _dot(jnp.concatenate(mts, axis=1), dym)
        dst_out = dstate[...]
        dst_outb = dst_out.astype(BF16)
        st_in = st_ref[...]
        st_inb = st_in.astype(BF16)
        cs_last = cs_e[CHUNK - 1:CHUNK, :]
        decay = jnp.exp(cs_last - cs_e)
        e_last = jnp.exp(cs_last)
        gpart = decay * _dot(bmb, dst_outb)
        dxdt = dxdt + gpart
        dyw = (jnp.exp(cs_e) * dy).astype(BF16)
        dcbb = dcb.astype(BF16)
        dc_ref[...] = _dot_nt(dyw, st_inb) + _dot(dcbb, bmb)
        db_ref[...] = _dot_nt((xdt * decay).astype(BF16), dst_outb) + _dot(dcb.T.astype(BF16), cmb)
        dstate[...] = e_last * dst_out + _dot(cm.T.astype(BF16), dyw)
        y_off = jnp.exp(cs_e) * _dot(cmb, st_inb)
        dcs = dcs + _dot_hi(dy * y_off - xdt * gpart, hsum)
        tail = jnp.sum(_dot_hi(xdt * gpart, hsum), axis=0, keepdims=True) + \
            jnp.sum(_dot_hi(dst_out * e_last * st_in, hsum), axis=0, keepdims=True)
        ri = lax.broadcasted_iota(jnp.int32, (CHUNK, 128), 0)
        dcs = dcs + jnp.where(ri == CHUNK - 1, tail, 0.0)
        da = _dot_hi(upper.astype(F32), dcs)
        ddt = da * a_neg + _dot_hi(dxdt * xs, hsum)
        dxs_ref[...] = dxdt * dt_e + dy * dexp
        ddtp = ddt * _sigmoid(dtp + bias)
        ddt_ref[...] = ddtp
        d_alog = jnp.sum(da * dt, axis=0, keepdims=True) * a_neg
        hpart = jnp.concatenate([jnp.sum(ddtp, axis=0, keepdims=True), d_alog, d_skip, jnp.zeros((5, 128), F32)], axis=0)
        cpart = jnp.concatenate([csum_part, jnp.zeros((7, GROUP_W), F32)], axis=0)

        @pl.when(step == 0)
        def _():
            hsum_ref[...] = hpart
            csum_ref[...] = cpart

        @pl.when(step > 0)
        def _():
            hsum_ref[...] += hpart
            csum_ref[...] += cpart

    rc = lambda c: nc - 1 - c
    gw = pl.BlockSpec((CHUNK, GROUP_W), lambda g, c: (rc(c), g))
    bsp = pl.BlockSpec((CHUNK, SSM_STATE), lambda g, c: (rc(c), D_INNER // SSM_STATE + g))
    csp = pl.BlockSpec((CHUNK, SSM_STATE), lambda g, c: (rc(c), D_INNER // SSM_STATE + SSM_GROUPS + g))
    in_specs = [
        gw, gw, gw, bsp, csp, gw,
        pl.BlockSpec((None, SSM_STATE, GROUP_W), lambda g, c: (rc(c), 0, g)),
        pl.BlockSpec((None, CHUNK, 128), lambda g, c: (g, rc(c), 0)),
        pl.BlockSpec((8, CHUNK), lambda g, c: (g, rc(c))),
        pl.BlockSpec((None, 1, 128), lambda g, c: (g, 0, 0)),
        pl.BlockSpec((8, 128), lambda g, c: (g, 0)),
        pl.BlockSpec((None, 1, 128), lambda g, c: (g, 0, 0)),
        pl.BlockSpec((8, 128), lambda g, c: (g, 0)),
        pl.BlockSpec((1, GROUP_W), lambda g, c: (0, g)),
        pl.BlockSpec((1, GROUP_W), lambda g, c: (0, g)),
    ]
    nsp = pl.BlockSpec((CHUNK, SSM_STATE), lambda g, c: (rc(c), g))
    out_specs = [gw, nsp, nsp, gw,
                 pl.BlockSpec((None, CHUNK, 128), lambda g, c: (g, rc(c), 0)),
                 pl.BlockSpec((None, 8, 128), lambda g, c: (g, 0, 0)),
                 pl.BlockSpec((8, GROUP_W), lambda g, c: (0, g))]
    gn = SSM_GROUPS * SSM_STATE
    out_shape = [SDS((t, D_INNER), F32), SDS((t, gn), F32), SDS((t, gn), F32), SDS((t, D_INNER), BF16),
                 SDS((SSM_GROUPS, t, 128), F32), SDS((SSM_GROUPS, 8, 128), F32), SDS((8, D_INNER), F32)]
    return pl.pallas_call(body, grid=(SSM_GROUPS, nc), in_specs=in_specs, out_specs=out_specs, out_shape=out_shape,
                          scratch_shapes=[pltpu.VMEM((SSM_STATE, GROUP_W), F32)], name=name,
                          compiler_params=_params(("parallel", "arbitrary")))(
        dyn, y, xbc, xbc, xbc, z, states, dtp_g, dtp_t, bias_g, bias_r, alog_g, alog_r, d_exp, norm_w)


BLK = 128
HEAD_PAIRS = ATTN_W // 128
ATTN_SCALE = 0.125


def _slopes(group):
    n = len(ATTN_GROUPS) * 16
    return [2.0 ** (-8.0 * (16 * group + h + 1) / n) for h in range(16)]


def _lane_lo(rows):
    return lax.broadcasted_iota(jnp.int32, (rows, 128), 1) < 64


def _attn_fwd(qkv, group, name):
    _, dil = ATTN_GROUPS[group]
    t = qkv.shape[0]
    sub = t // dil
    nb = sub // BLK
    slopes = _slopes(group)

    def body(q_ref, kp_ref, kc_ref, vp_ref, vc_ref, o_ref, l_ref):
        n = pl.program_id(1)
        qi = lax.broadcasted_iota(jnp.int32, (BLK, 2 * BLK), 0)
        kj = lax.broadcasted_iota(jnp.int32, (BLK, 2 * BLK), 1)
        dist = qi + BLK - kj
        valid = (dist >= 0) & (dist <= BLK) & ((kj >= BLK) | (n > 0))
        distf = dist.astype(F32) * float(dil)
        lo_q, lo_k = _lane_lo(BLK), _lane_lo(2 * BLK)
        for hp in range(HEAD_PAIRS):
            sl = slice(128 * hp, 128 * hp + 128)
            q2 = q_ref[:, sl]
            k2 = jnp.concatenate([kp_ref[:, sl], kc_ref[:, sl]], axis=0)
            v2 = jnp.concatenate([vp_ref[:, sl], vc_ref[:, sl]], axis=0)
            o2 = jnp.zeros((BLK, 128), F32)
            l2 = jnp.zeros((BLK, 128), F32)
            for s01 in range(2):
                mq = lo_q if s01 == 0 else ~lo_q
                mk = lo_k if s01 == 0 else ~lo_k
                s = _dot_nt(jnp.where(mq, q2, jnp.zeros_like(q2)), k2) * ATTN_SCALE - slopes[2 * hp + s01] * distf
                s = jnp.where(valid, s, -jnp.inf)
                mx = jnp.max(s, axis=-1, keepdims=True)
                p = jnp.exp(s - mx)
                den = jnp.sum(p, axis=-1, keepdims=True)
                o2 = o2 + _dot(p.astype(BF16), jnp.where(mk, v2, jnp.zeros_like(v2))) / den
                l2 = jnp.where(mq, mx + jnp.log(den), l2)
            o_ref[:, sl] = o2
            l_ref[:, sl] = l2

    ncol = QKV_DIM // ATTN_W

    def col(which):
        return lambda r, n: (n, r * ncol + 3 * which + group)

    def colp(which):
        return lambda r, n: (jnp.maximum(n - 1, 0), r * ncol + 3 * which + group)

    blk = (BLK, ATTN_W)
    in_specs = [pl.BlockSpec(blk, col(0)), pl.BlockSpec(blk, colp(1)), pl.BlockSpec(blk, col(1)),
                pl.BlockSpec(blk, colp(2)), pl.BlockSpec(blk, col(2))]
    osp = pl.BlockSpec(blk, lambda r, n: (n, r))
    qv = qkv.reshape(sub, dil * QKV_DIM)
    o, l = pl.pallas_call(body, grid=(dil, nb), in_specs=in_specs, out_specs=[osp, osp],
                          out_shape=[SDS((sub, dil * ATTN_W), F32), SDS((sub, dil * ATTN_W), F32)], name=name,
                          compiler_params=_params(("parallel", "parallel")))(qv, qv, qv, qv, qv)
    return o.reshape(t, ATTN_W), l.reshape(t, ATTN_W)


def _attn_bwd_q(qkv, do, lse, dvec, group, name):
    _, dil = ATTN_GROUPS[group]
    t = qkv.shape[0]
    sub = t // dil
    nb = sub // BLK
    slopes = _slopes(group)

    def body(q_ref, kp_ref, kc_ref, vp_ref, vc_ref, do_ref, l_ref, dv_ref, dq_ref):
        n = pl.program_id(1)
        qi = lax.broadcasted_iota(jnp.int32, (BLK, 2 * BLK), 0)
        kj = lax.broadcasted_iota(jnp.int32, (BLK, 2 * BLK), 1)
        dist = qi + BLK - kj
        valid = (dist >= 0) & (dist <= BLK) & ((kj >= BLK) | (n > 0))
        distf = dist.astype(F32) * float(dil)
        lo_q, lo_k = _lane_lo(BLK), _lane_lo(2 * BLK)
        for hp in range(HEAD_PAIRS):
            sl = slice(128 * hp, 128 * hp + 128)
            q2 = q_ref[:, sl]
            k2 = jnp.concatenate([kp_ref[:, sl], kc_ref[:, sl]], axis=0)
            v2 = jnp.concatenate([vp_ref[:, sl], vc_ref[:, sl]], axis=0)
            do2 = do_ref[:, sl]
            l2 = l_ref[:, sl]
            d2 = dv_ref[:, sl]
            dq2 = jnp.zeros((BLK, 128), F32)
            for s01 in range(2):
                mq = lo_q if s01 == 0 else ~lo_q
                mk = lo_k if s01 == 0 else ~lo_k
                s = _dot_nt(jnp.where(mq, q2, jnp.zeros_like(q2)), k2) * ATTN_SCALE - slopes[2 * hp + s01] * distf
                p = jnp.exp(jnp.where(valid, s - l2[:, 64 * s01:64 * s01 + 1], -jnp.inf))
                dp = _dot_nt(jnp.where(mq, do2, jnp.zeros_like(do2)), v2)
                ds = p * (dp - d2[:, 64 * s01:64 * s01 + 1])
                dq2 = dq2 + _dot(ds.astype(BF16), jnp.where(mk, k2, jnp.zeros_like(k2))) * ATTN_SCALE
            dq_ref[:, sl] = dq2.astype(BF16)

    ncol = QKV_DIM // ATTN_W

    def col(which):
        return lambda r, n: (n, r * ncol + 3 * which + group)

    def colp(which):
        return lambda r, n: (jnp.maximum(n - 1, 0), r * ncol + 3 * which + group)

    blk = (BLK, ATTN_W)
    osp = pl.BlockSpec(blk, lambda r, n: (n, r))
    in_specs = [pl.BlockSpec(blk, col(0)), pl.BlockSpec(blk, colp(1)), pl.BlockSpec(blk, col(1)),
                pl.BlockSpec(blk, colp(2)), pl.BlockSpec(blk, col(2)), osp, osp, osp]
    qv = qkv.reshape(sub, dil * QKV_DIM)
    view = lambda a: a.reshape(sub, dil * ATTN_W)
    dq = pl.pallas_call(body, grid=(dil, nb), in_specs=in_specs, out_specs=osp,
                        out_shape=SDS((sub, dil * ATTN_W), BF16), name=name,
                        compiler_params=_params(("parallel", "parallel")))(qv, qv, qv, qv, qv, view(do), view(lse), view(dvec))
    return dq.reshape(t, ATTN_W)


def _attn_bwd_kv(qkv, do, lse, dvec, group, name):
    _, dil = ATTN_GROUPS[group]
    t = qkv.shape[0]
    sub = t // dil
    nb = sub // BLK
    slopes = _slopes(group)

    def body(kc_ref, vc_ref, qa_ref, qb_ref, doa_ref, dob_ref, la_ref, lb_ref, da_ref, db_ref, dk_ref, dv_ref):
        m = pl.program_id(1)
        qi = lax.broadcasted_iota(jnp.int32, (2 * BLK, BLK), 0)
        kj = lax.broadcasted_iota(jnp.int32, (2 * BLK, BLK), 1)
        dist = qi - kj
        valid = (dist >= 0) & (dist <= BLK) & ((qi < BLK) | (m < nb - 1))
        distf = dist.astype(F32) * float(dil)
        lo_q = _lane_lo(2 * BLK)
        for hp in range(HEAD_PAIRS):
            sl = slice(128 * hp, 128 * hp + 128)
            k2 = kc_ref[:, sl]
            v2 = vc_ref[:, sl]
            q2 = jnp.concatenate([qa_ref[:, sl], qb_ref[:, sl]], axis=0)
            do2 = jnp.concatenate([doa_ref[:, sl], dob_ref[:, sl]], axis=0)
            l2 = jnp.concatenate([la_ref[:, sl], lb_ref[:, sl]], axis=0)
            d2 = jnp.concatenate([da_ref[:, sl], db_ref[:, sl]], axis=0)
            dk2 = jnp.zeros((BLK, 128), F32)
            dv2 = jnp.zeros((BLK, 128), F32)
            for s01 in range(2):
                mq = lo_q if s01 == 0 else ~lo_q
                qm = jnp.where(mq, q2, jnp.zeros_like(q2))
                dom = jnp.where(mq, do2, jnp.zeros_like(do2))
                s = _dot_nt(qm, k2) * ATTN_SCALE - slopes[2 * hp + s01] * distf
                p = jnp.exp(jnp.where(valid, s - l2[:, 64 * s01:64 * s01 + 1], -jnp.inf))
                dv2 = dv2 + _dot_tn(p.astype(BF16), dom)
                dp = _dot_nt(dom, v2)
                ds = p * (dp - d2[:, 64 * s01:64 * s01 + 1])
                dk2 = dk2 + _dot_tn(ds.astype(BF16), qm) * ATTN_SCALE
            dk_ref[:, sl] = dk2.astype(BF16)
            dv_ref[:, sl] = dv2.astype(BF16)

    ncol = QKV_DIM // ATTN_W
    blk = (BLK, ATTN_W)

    def col(which):
        return lambda r, m: (m, r * ncol + 3 * which + group)

    cur = pl.BlockSpec(blk, lambda r, m: (m, r))
    nxt = pl.BlockSpec(blk, lambda r, m: (jnp.minimum(m + 1, nb - 1), r))
    qn = pl.BlockSpec(blk, lambda r, m: (jnp.minimum(m + 1, nb - 1), r * ncol + group))
    in_specs = [pl.BlockSpec(blk, col(1)), pl.BlockSpec(blk, col(2)), pl.BlockSpec(blk, col(0)), qn,
                cur, nxt, cur, nxt, cur, nxt]
    qv = qkv.reshape(sub, dil * QKV_DIM)
    view = lambda a: a.reshape(sub, dil * ATTN_W)
    dov, lv, dvv = view(do), view(lse), view(dvec)
    dk, dv = pl.pallas_call(body, grid=(dil, nb), in_specs=in_specs, out_specs=[cur, cur],
                            out_shape=[SDS((sub, dil * ATTN_W), BF16)] * 2, name=name,
                            compiler_params=_params(("parallel", "parallel")))(qv, qv, qv, qv, dov, dov, lv, lv, dvv, dvv)
    return dk.reshape(t, ATTN_W), dv.reshape(t, ATTN_W)


def _combine_weights(l0, l1, l2):
    mx = jnp.maximum(jnp.maximum(l0, l1), l2)
    e0, e1, e2 = jnp.exp(l0 - mx), jnp.exp(l1 - mx), jnp.exp(l2 - mx)
    den = e0 + e1 + e2
    return e0 / den, e1 / den, e2 / den


def _combine_fwd(os_, ls_, name):
    t = os_[0].shape[0]

    def body(o0, o1, o2, l0, l1, l2, out):
        w0, w1, w2 = _combine_weights(l0[...], l1[...], l2[...])
        out[...] = (w0 * o0[...] + w1 * o1[...] + w2 * o2[...]).astype(BF16)

    row = pl.BlockSpec((ROW_TILE, ATTN_W), lambda i: (i, 0))
    return pl.pallas_call(body, grid=(t // ROW_TILE,), in_specs=[row] * 6, out_specs=row, out_shape=SDS((t, ATTN_W), BF16),
                          name=name, compiler_params=_params(("parallel",)))(*os_, *ls_)


def _combine_bwd(do, os_, ls_, name):
    t = do.shape[0]

    def body(do_ref, o0, o1, o2, l0, l1, l2, g0, g1, g2, d0, d1, d2):
        w0, w1, w2 = _combine_weights(l0[...], l1[...], l2[...])
        dov = do_ref[...]
        prod = dov * (w0 * o0[...] + w1 * o1[...] + w2 * o2[...])
        r = lax.broadcasted_iota(jnp.int32, (128, 128), 0) // 64
        c = lax.broadcasted_iota(jnp.int32, (128, 128), 1) // 64
        same = (r == c).astype(F32)
        tbar = jnp.concatenate([_dot_hi(prod[:, 128 * k:128 * k + 128], same) for k in range(HEAD_PAIRS)], axis=1)
        for w, g, d in ((w0, g0, d0), (w1, g1, d1), (w2, g2, d2)):
            g[...] = (w * dov).astype(BF16)
            d[...] = w * tbar

    row = pl.BlockSpec((ROW_TILE, ATTN_W), lambda i: (i, 0))
    return pl.pallas_call(body, grid=(t // ROW_TILE,), in_specs=[row] * 7, out_specs=[row] * 6,
                          out_shape=[SDS((t, ATTN_W), BF16)] * 3 + [SDS((t, ATTN_W), F32)] * 3, name=name,
                          compiler_params=_params(("parallel",)))(do, *os_, *ls_)


def _peer(k):
    x, y, c = lax.axis_index("x"), lax.axis_index("y"), lax.axis_index("c")
    px = 1 - x if k & 4 else x
    py = 1 - y if k & 2 else y
    pc = 1 - c if k & 1 else c
    return (px, py, pc), 4 * px + 2 * py + pc


def _my_index():
    return 4 * lax.axis_index("x") + 2 * lax.axis_index("y") + lax.axis_index("c")


def _exchange_sems(n):
    return [pltpu.SemaphoreType.DMA((n * (NDEV - 1),)), pltpu.SemaphoreType.DMA((n * (NDEV - 1),)),
            pltpu.SemaphoreType.DMA((n,))]


def _exchange_copies(kind, ins, outs, send, recv, local, arrivals):
    me = _my_index()
    own, sent, arriving = [], [], []
    for i in range(len(ins)):
        own.append(pltpu.make_async_copy(ins[i] if kind == "gather" else ins[i].at[me], outs[i].at[me], local.at[i]))
        for k in range(1, NDEV):
            peer, pidx = _peer(k)
            s = i * (NDEV - 1) + k - 1
            src = ins[i] if kind == "gather" else ins[i].at[pidx]
            for dst, into in ((outs[i].at[me], sent), (outs[i].at[pidx], arriving)):
                if into is sent or arrivals:
                    into.append(pltpu.make_async_remote_copy(src_ref=src, dst_ref=dst, send_sem=send.at[s],
                                                             recv_sem=recv.at[s], device_id=peer, device_id_type=MESH))
    return own, sent, arriving


def _exchange_start(kind, ins, outs, send, recv, local):
    own, sent, _ = _exchange_copies(kind, ins, outs, send, recv, local, arrivals=False)
    for cp in own + sent:
        cp.start()


def _exchange_wait(kind, ins, outs, send, recv, local):
    own, sent, arriving = _exchange_copies(kind, ins, outs, send, recv, local, arrivals=True)
    for cp in sent:
        cp.wait_send()
    for cp in arriving:
        cp.wait_recv()
    for cp in own:
        cp.wait()


def _exchange(kind, tensors, name):
    n = len(tensors)

    def body(*refs):
        _exchange_start(kind, refs[:n], refs[n:2 * n], *refs[2 * n:])
        _exchange_wait(kind, refs[:n], refs[n:2 * n], *refs[2 * n:])

    hbm = pl.BlockSpec(memory_space=pl.ANY)
    shapes = [SDS(((NDEV,) + t.shape) if kind == "gather" else t.shape, t.dtype) for t in tensors]
    return pl.pallas_call(body, in_specs=[hbm] * n, out_specs=[hbm] * n, out_shape=shapes,
                          scratch_shapes=_exchange_sems(n), name=name)(*tensors)


def _all_reduce_small(v, name):
    rows = v.shape[0]

    def body(v_ref, out_ref, land, send, recv):
        me = _my_index()
        land[me] = v_ref[...]
        remote = []
        for k in range(1, NDEV):
            peer, _ = _peer(k)
            cp = pltpu.make_async_remote_copy(src_ref=v_ref, dst_ref=land.at[me], send_sem=send.at[k - 1],
                                              recv_sem=recv.at[k - 1], device_id=peer, device_id_type=MESH)
            cp.start()
            remote.append(cp)
        for cp in remote:
            cp.wait_send()
        for k in range(1, NDEV):
            peer, pidx = _peer(k)
            pltpu.make_async_remote_copy(src_ref=v_ref, dst_ref=land.at[pidx], send_sem=send.at[k - 1],
                                         recv_sem=recv.at[k - 1], device_id=peer, device_id_type=MESH).wait_recv()
        total = land[0]
        for d in range(1, NDEV):
            total = total + land[d]
        out_ref[...] = total

    vm = pl.BlockSpec(memory_space=pltpu.VMEM)
    return pl.pallas_call(
        body, in_specs=[vm], out_specs=vm, out_shape=SDS((rows, 128), F32),
        scratch_shapes=[pltpu.VMEM((NDEV, rows, 128), F32), pltpu.SemaphoreType.DMA((NDEV - 1,)),
                        pltpu.SemaphoreType.DMA((NDEV - 1,))],
        name=name)(v)


def _adamw_math(w, g, m, v):
    m = ADAM_B1 * m + (1.0 - ADAM_B1) * g
    v = ADAM_B2 * v + (1.0 - ADAM_B2) * (g * g)
    m_hat = m / (1.0 - ADAM_B1 ** ADAM_STEP)
    v_hat = v / (1.0 - ADAM_B2 ** ADAM_STEP)
    delta = -ADAM_LR * (m_hat / (jnp.sqrt(v_hat) + ADAM_EPS) + ADAM_WD * w)
    return delta, m, v


def _row_tile(rows, cols):
    tr = rows
    while tr * cols * 4 > (1 << 20) and tr % 16 == 0:
        tr //= 2
    return tr


def _adamw(g, w, m, v, name):
    rows, cols = w.shape
    tr = _row_tile(rows, cols)

    def body(g_ref, w_ref, m_ref, v_ref, d_out, m_out, v_out):
        d, mn, vn = _adamw_math(w_ref[...], g_ref[...], m_ref[...], v_ref[...])
        d_out[...] = d
        m_out[...] = mn
        v_out[...] = vn

    sp = pl.BlockSpec((tr, cols), lambda i: (i, 0))
    return pl.pallas_call(body, grid=(rows // tr,), in_specs=[sp] * 4, out_specs=[sp] * 3,
                          out_shape=[SDS((rows, cols), F32)] * 3, name=name, compiler_params=_params(("parallel",)))(g, w, m, v)


def _reduce_adamw(parts, w, m, v, name):
    rows, cols = w.shape
    tr = _row_tile(rows, cols)

    def body(p_ref, w_ref, m_ref, v_ref, g_out, d_out, m_out, v_out):
        g = p_ref[0].astype(F32)
        for d in range(1, NDEV):
            g = g + p_ref[d].astype(F32)
        g_out[...] = g
        dl, mn, vn = _adamw_math(w_ref[...], g, m_ref[...], v_ref[...])
        d_out[...] = dl
        m_out[...] = mn
        v_out[...] = vn

    sp = pl.BlockSpec((tr, cols), lambda i: (i, 0))
    psp = pl.BlockSpec((NDEV, tr, cols), lambda i: (0, i, 0))
    return pl.pallas_call(body, grid=(rows // tr,), in_specs=[psp, sp, sp, sp], out_specs=[sp] * 4,
                          out_shape=[SDS((rows, cols), F32)] * 4, name=name, compiler_params=_params(("parallel",)))(parts, w, m, v)


def _pack(items):
    rows = []
    for a in items:
        a = a.reshape(-1).astype(F32)
        pad = (-a.shape[0]) % 128
        rows.append(jnp.pad(a, (0, pad)).reshape(-1, 128))
    out = jnp.concatenate(rows, axis=0)
    return jnp.pad(out, ((0, (-out.shape[0]) % 8), (0, 0)))


def _unpack(packed, shapes):
    out, r = [], 0
    for shp in shapes:
        n = math.prod(shp)
        nr = -(-n // 128)
        out.append(packed[r:r + nr].reshape(-1)[:n].reshape(shp))
        r += nr
    return out


def _ident(t):
    return (t,)


def _add(t, res):
    return (t + res,)


def kernel(x, norm_mix, norm_mlp, ssm_w_in, ssm_conv_w, ssm_conv_b, ssm_dt_bias, ssm_a_log, ssm_d, ssm_norm_w, ssm_w_out, attn_w_qkv, attn_w_o, mlp_w1, mlp_w2, final_norm, loss_target, m_norm_mix, m_norm_mlp, m_ssm_w_in, m_ssm_conv_w, m_ssm_conv_b, m_ssm_dt_bias, m_ssm_a_log, m_ssm_d, m_ssm_norm_w, m_ssm_w_out, m_attn_w_qkv, m_attn_w_o, m_mlp_w1, m_mlp_w2, m_final_norm, v_norm_mix, v_norm_mlp, v_ssm_w_in, v_ssm_conv_w, v_ssm_conv_b, v_ssm_dt_bias, v_ssm_a_log, v_ssm_d, v_ssm_norm_w, v_ssm_w_out, v_attn_w_qkv, v_attn_w_o, v_mlp_w1, v_mlp_w2, v_final_norm):
    t = x.shape[1]
    x0 = x.reshape(t, D_MODEL)
    tgt = loss_target.reshape(t, D_MODEL)
    me = _my_index()
    in_dim = D_INNER + CONV_DIM + SSM_HEADS
    in_shard = in_dim // NDEV
    zx_dim = D_INNER + CONV_DIM

    s_out, s_qkv, s_o = ssm_w_out[0].astype(BF16), attn_w_qkv[0].astype(BF16), attn_w_o[0].astype(BF16)
    s_w1, s_w2 = mlp_w1.astype(BF16), mlp_w2.astype(BF16)
    g_in, g_cw = _exchange("gather", [ssm_w_in[0].astype(BF16), ssm_conv_w[0]], "gather_in_proj")
    w_in = jnp.transpose(g_in, (1, 0, 2)).reshape(D_MODEL, in_dim)
    w_z, w_x = w_in[:, :D_INNER], w_in[:, D_INNER:zx_dim]
    w_dt = jnp.pad(w_in[:, zx_dim:], ((0, 0), (0, 128 - SSM_HEADS)))
    conv_w = jnp.transpose(g_cw, (1, 0, 2)).reshape(CONV_WIDTH, CONV_DIM)
    g_w1, g_w2 = [None, None], [None, None]

    def lanes(p):
        return jnp.pad(p.reshape(SSM_GROUPS, 1, 8), ((0, 0), (0, 0), (0, 120)))

    def rows(p):
        return jnp.broadcast_to(p.reshape(SSM_HEADS, 1), (SSM_HEADS, 128))

    bias_g, bias_r = lanes(ssm_dt_bias[0]), rows(ssm_dt_bias[0])
    alog_g, alog_r = lanes(ssm_a_log[0]), rows(ssm_a_log[0])
    d_exp = jnp.repeat(ssm_d[0], SSM_HEAD_DIM).reshape(1, D_INNER)
    norm_w = ssm_norm_w

    def relu2(tot):
        r = jnp.maximum(tot, 0.0)
        return r, r * r

    def mlp_fwd(xin, layer, tag, down_carry):
        h = _rmsnorm_fwd(xin, norm_mlp[layer:layer + 1], f"norm_mlp{tag}")
        r, a, w2 = _mm_fwd(h, g_w1[layer], "cols", D_MODEL, D_FF, epi=relu2, outs=(BF16, BF16),
                           carry=("gather", [s_w2[layer]]), name=f"mlp_up{tag}")
        g_w2[layer] = w2.reshape(D_FF, D_MODEL)
        xout, *got = _mm_fwd(a, g_w2[layer], "plain", D_FF, D_MODEL, epi=_add, outs=(F32,), extras=(xin,),
                             carry=("gather", down_carry) if down_carry else None, name=f"mlp_down{tag}")
        return h, r, a, xout, got

    h0 = _rmsnorm_fwd(x0, norm_mix[0:1], "norm_mix0")
    z, g_out = _mm_fwd(h0, w_z, "plain", D_MODEL, D_INNER, epi=_ident, outs=(F32,), carry=("gather", [s_out]), name="ssm_in_z")
    xpre, g_w1[0] = _mm_fwd(h0, w_x, "plain", D_MODEL, CONV_DIM, epi=_ident, outs=(F32,), carry=("gather", [s_w1[0]]),
                            name="ssm_in_x")
    dtp, = _mm_fwd(h0, w_dt, "plain", D_MODEL, 128, epi=_ident, outs=(F32,), name="ssm_in_dt")
    xbc = _conv_fwd(xpre, conv_w, ssm_conv_b, "conv_fwd")
    dtp64 = dtp[:, :SSM_HEADS]
    dtp_g = jnp.pad(jnp.transpose(dtp64.reshape(t, SSM_GROUPS, 8), (1, 0, 2)), ((0, 0), (0, 0), (0, 120)))
    dtp_t = jnp.transpose(dtp64)
    y_ssd, yn, states = _ssd_fwd(xbc, z, dtp_g, dtp_t, bias_g, bias_r, alog_g, alog_r, d_exp, norm_w, "ssd_fwd")
    g_out = g_out.reshape(D_INNER, D_MODEL)
    x1, g_o = _mm_fwd(yn, g_out, "plain", D_INNER, D_MODEL, epi=_add, outs=(F32,), extras=(x0,), carry=("gather", [s_o]),
                      name="ssm_out")
    h1, r1, a1, x2, (g_qkv,) = mlp_fwd(x1, 0, "0", [s_qkv])

    h2 = _rmsnorm_fwd(x2, norm_mix[1:2], "norm_mix1")
    qkv, g_w1[1] = _mm_fwd(h2, g_qkv, "cols", D_MODEL, QKV_DIM, epi=_ident, outs=(BF16,), tn=QKV_DIM // NDEV,
                           carry=("gather", [s_w1[1]]), name="attn_qkv")
    att = [_attn_fwd(qkv, g, f"attn_fwd{g}") for g in range(3)]
    os_, ls_ = [a[0] for a in att], [a[1] for a in att]
    o_mix = _combine_fwd(os_, ls_, "attn_combine")
    x3, = _mm_fwd(o_mix, g_o, "cols", ATTN_W, D_MODEL, epi=_add, outs=(F32,), extras=(x2,), tn=D_MODEL // NDEV, name="attn_out")
    h3, r3, a3, x4, _ = mlp_fwd(x3, 1, "1", None)

    dx4, dx4b, loss_acc, d_final = _loss_head(x4, tgt, final_norm.reshape(1, D_MODEL), "loss_head")

    def mlp_bwd(xin, h, r, a, dxo, dxob, layer, tag):
        du, = _mm_dx(dxob, g_w2[layer], "plain", D_FF, D_MODEL, epi=lambda tot, rr: (tot * (2.0 * rr.astype(F32)),),
                     outs=(BF16,), extras=(r,), name=f"mlp_down_dx{tag}")
        dw2, = _mm_dw(a, dxob, "plain", name=f"mlp_down_dw{tag}")
        dw2 = dw2.reshape(NDEV, D_FF // NDEV, D_MODEL)
        dw1, p_dw2 = _mm_dw(h, du, "cols", carry=("scatter", [dw2]), name=f"mlp_up_dw{tag}")
        dh, p_dw1 = _mm_dx(du, g_w1[layer], "cols", D_MODEL, D_FF, epi=_ident, outs=(F32,), carry=("scatter", [dw1]),
                           name=f"mlp_up_dx{tag}")
        dxi, dxib, dg = _rmsnorm_bwd(xin, norm_mlp[layer:layer + 1], dh, dxo, f"norm_mlp_bwd{tag}")
        return dxi, dxib, dg, p_dw1, p_dw2

    dx3, dx3b, dg_mlp1, p_w1_1, p_w2_1 = mlp_bwd(x3, h3, r3, a3, dx4, dx4b, 1, "1")

    dw_o, = _mm_dw(o_mix, dx3b, "cols", tn=D_MODEL // NDEV, name="attn_out_dw")
    do, p_o = _mm_dx(dx3b, g_o, "cols", ATTN_W, D_MODEL, epi=_ident, outs=(F32,), tk=D_MODEL // NDEV,
                     carry=("scatter", [dw_o]), name="attn_out_dx")
    cb = _combine_bwd(do, os_, ls_, "attn_combine_bwd")
    dos, dvecs = cb[:3], cb[3:]
    dqs, dks, dvs = [], [], []
    for g in range(3):
        dqs.append(_attn_bwd_q(qkv, dos[g], ls_[g], dvecs[g], g, f"attn_bwd_q{g}"))
        dk, dv = _attn_bwd_kv(qkv, dos[g], ls_[g], dvecs[g], g, f"attn_bwd_kv{g}")
        dks.append(dk)
        dvs.append(dv)
    dqkv = jnp.concatenate(dqs + dks + dvs, axis=1)
    dw_qkv, = _mm_dw(h2, dqkv, "cols", tn=QKV_DIM // NDEV, name="attn_qkv_dw")
    dh2, p_qkv = _mm_dx(dqkv, g_qkv, "cols", D_MODEL, QKV_DIM, epi=_ident, outs=(F32,), tk=QKV_DIM // NDEV,
                        carry=("scatter", [dw_qkv]), name="attn_qkv_dx")
    dx2, dx2b, dg_mix1 = _rmsnorm_bwd(x2, norm_mix[1:2], dh2, dx3, "norm_mix_bwd1")

    dx1, dx1b, dg_mlp0, p_w1_0, p_w2_0 = mlp_bwd(x1, h1, r1, a1, dx2, dx2b, 0, "0")

    dw_out, = _mm_dw(yn, dx1b, "plain", name="ssm_out_dw")
    dw_out = dw_out.reshape(NDEV, D_INNER // NDEV, D_MODEL)
    dyn, p_out = _mm_dx(dx1b, g_out, "plain", D_INNER, D_MODEL, epi=_ident, outs=(F32,), carry=("scatter", [dw_out]),
                        name="ssm_out_dx")
    dxs, d_b, d_c, dz, ddtp_g, hsums, csums = _ssd_bwd(dyn, y_ssd, xbc, z, states, dtp_g, dtp_t, bias_g, bias_r,
                                                       alog_g, alog_r, d_exp, norm_w, "ssd_bwd")
    dxbc = jnp.concatenate([dxs, d_b, d_c], axis=1)
    dpre, conv_sums = _conv_bwd_pre(xpre, conv_w, ssm_conv_b, dxbc, "conv_bwd_pre")
    du = _conv_bwd_in(dpre, conv_w, "conv_bwd_in")
    ddtp = jnp.transpose(ddtp_g[:, :, :8], (1, 0, 2)).reshape(t, SSM_HEADS)
    ddtp = jnp.pad(ddtp, ((0, 0), (0, 128 - SSM_HEADS))).astype(BF16)
    dw_z, = _mm_dw(h0, dz, "plain", name="ssm_in_z_dw")
    dw_x, = _mm_dw(h0, du, "plain", name="ssm_in_x_dw")
    dw_dt, = _mm_dw(h0, ddtp, "plain", name="ssm_in_dt_dw")
    dw_in = jnp.concatenate([dw_z, dw_x, dw_dt[:, :SSM_HEADS]], axis=1)
    dw_in = jnp.transpose(dw_in.reshape(D_MODEL, NDEV, in_shard), (1, 0, 2))
    dh0, = _mm_dx(ddtp, w_dt, "plain", D_MODEL, 128, epi=_ident, outs=(F32,), name="ssm_in_dt_dx")
    dh0, = _mm_dx(dz, w_z, "plain", D_MODEL, D_INNER, epi=_add, outs=(F32,), extras=(dh0,), name="ssm_in_z_dx")
    dh0, p_in = _mm_dx(du, w_x, "plain", D_MODEL, CONV_DIM, epi=_add, outs=(F32,), extras=(dh0,),
                       carry=("scatter", [dw_in]), name="ssm_in_x_dx")
    dx0, _, dg_mix0 = _rmsnorm_bwd(x0, norm_mix[0:1], dh0, dx1, "norm_mix_bwd0")

    parts = [p_in, p_out, p_qkv, p_o, p_w1_0, p_w1_1, p_w2_0, p_w2_1]

    def own(w, mm, vv):
        shp = w.shape
        f = lambda a: a.reshape(-1, shp[-1])
        return f(w), f(mm), f(vv), shp

    big = {}
    for key, part, (w, mm, vv) in (
            ("ssm_w_in", parts[0], (ssm_w_in, m_ssm_w_in, v_ssm_w_in)),
            ("ssm_w_out", parts[1], (ssm_w_out, m_ssm_w_out, v_ssm_w_out)),
            ("attn_w_qkv", parts[2], (attn_w_qkv, m_attn_w_qkv, v_attn_w_qkv)),
            ("attn_w_o", parts[3], (attn_w_o, m_attn_w_o, v_attn_w_o))):
        w2, m2, v2, shp = own(w, mm, vv)
        res = _reduce_adamw(part, w2, m2, v2, f"adamw_{key}")
        big[key] = [r.reshape(shp) for r in res]
    for key, pa, pb, (w, mm, vv) in (("mlp_w1", parts[4], parts[5], (mlp_w1, m_mlp_w1, v_mlp_w1)),
                                     ("mlp_w2", parts[6], parts[7], (mlp_w2, m_mlp_w2, v_mlp_w2))):
        res = [_reduce_adamw(p, w[l], mm[l], vv[l], f"adamw_{key}_{l}") for l, p in enumerate((pa, pb))]
        big[key] = [jnp.stack([res[0][i], res[1][i]], axis=0) for i in range(4)]

    d_norm_mix = jnp.concatenate([dg_mix0, dg_mix1], axis=0)
    d_norm_mlp = jnp.concatenate([dg_mlp0, dg_mlp1], axis=0)
    d_conv_b = conv_sums[4:5]
    d_conv_w = conv_sums[0:4]
    head = hsums[:, :, :8]
    d_dt_bias, d_a_log, d_d = (head[:, k, :].reshape(1, SSM_HEADS) for k in range(3))
    d_ssm_norm = csums[0:1]
    small = [d_norm_mix, d_norm_mlp, d_conv_b, d_dt_bias, d_a_log, d_d, d_ssm_norm, d_final, d_conv_w, loss_acc[0:1, 0:1]]
    shapes = [a.shape for a in small]
    summed = _unpack(_all_reduce_small(_pack(small), "reduce_small"), shapes)
    g_conv_w_full = summed[8]
    loss = summed[9].reshape(())
    g_conv_w = lax.dynamic_slice(g_conv_w_full, (0, me * (CONV_DIM // NDEV)), (CONV_WIDTH, CONV_DIM // NDEV))

    small_names = ["norm_mix", "norm_mlp", "ssm_conv_b", "ssm_dt_bias", "ssm_a_log", "ssm_d", "ssm_norm_w", "final_norm"]
    small_w = [norm_mix, norm_mlp, ssm_conv_b, ssm_dt_bias, ssm_a_log, ssm_d, ssm_norm_w, final_norm, ssm_conv_w]
    small_m = [m_norm_mix, m_norm_mlp, m_ssm_conv_b, m_ssm_dt_bias, m_ssm_a_log, m_ssm_d, m_ssm_norm_w, m_final_norm, m_ssm_conv_w]
    small_v = [v_norm_mix, v_norm_mlp, v_ssm_conv_b, v_ssm_dt_bias, v_ssm_a_log, v_ssm_d, v_ssm_norm_w, v_final_norm, v_ssm_conv_w]
    small_g = [summed[i].reshape(small_w[i].shape) for i in range(8)] + [g_conv_w.reshape(ssm_conv_w.shape)]
    wshapes = [a.shape for a in small_w]
    sd, sm, sv = _adamw(_pack(small_g), _pack(small_w), _pack(small_m), _pack(small_v), "adamw_small")
    sd, sm, sv = _unpack(sd, wshapes), _unpack(sm, wshapes), _unpack(sv, wshapes)
    res = {n: (small_g[i], sd[i], sm[i], sv[i]) for i, n in enumerate(small_names + ["ssm_conv_w"])}
    for n in big:
        res[n] = tuple(big[n])

    order = ["norm_mix", "norm_mlp", "ssm_w_in", "ssm_conv_w", "ssm_conv_b", "ssm_dt_bias", "ssm_a_log", "ssm_d",
             "ssm_norm_w", "ssm_w_out", "attn_w_qkv", "attn_w_o", "mlp_w1", "mlp_w2", "final_norm"]
    outs = [loss, dx0.reshape(x.shape)]
    for kind in range(4):
        outs += [res[n][kind] for n in order]
    return tuple(outs)
```

```python
import math

import jax
import jax.numpy as jnp
from jax import lax
from jax.experimental import pallas as pl
from jax.experimental.pallas import tpu as pltpu

F32, BF16 = jnp.float32, jnp.bfloat16
SDS = jax.ShapeDtypeStruct
MESH = pl.DeviceIdType.MESH
HIGHEST = lax.Precision.HIGHEST

NDEV = 8
D_MODEL = 2048
D_INNER = 4096
SSM_HEADS = 64
SSM_HEAD_DIM = 64
SSM_GROUPS = 8
SSM_STATE = 128
CHUNK = 128
CONV_DIM = 6144
CONV_WIDTH = 4
GROUP_W = D_INNER // SSM_GROUPS
ATTN_GROUPS = ((128, 1), (512, 4), (2048, 16))
ATTN_W = 1024
QKV_DIM = 9216
D_FF = 8192
EPS = 1e-5
ADAM_LR, ADAM_B1, ADAM_B2, ADAM_EPS, ADAM_WD, ADAM_STEP = 0.001, 0.9, 0.999, 1e-08, 0.01, 10

VMEM_LIMIT = 48 * 1024 * 1024


def _params(sem):
    return pltpu.CompilerParams(dimension_semantics=sem, vmem_limit_bytes=VMEM_LIMIT)


def _sigmoid(v):
    return 1.0 / (1.0 + jnp.exp(-v))


def _softplus(v):
    return jnp.maximum(v, 0.0) + jnp.log1p(jnp.exp(-jnp.abs(v)))


def _wspec(layout, kw, nw, tr, tc, sel):
    if layout == "plain":
        return pl.BlockSpec((tr, tc), lambda *g: sel(*g))
    per = (nw // NDEV) // tc
    return pl.BlockSpec((None, tr, tc), lambda *g: (sel(*g)[1] // per, sel(*g)[0], sel(*g)[1] % per))


def _wshape(layout, kw, nw):
    return (kw, nw) if layout == "plain" else (NDEV, kw, nw // NDEV)


def _mm_call(name, grid, in_specs, out_specs, out_shape, dims, n_extra, epi, tm, tn, carry):
    nk = grid[2]
    kind, moved = carry if carry else (None, ())
    nc = len(moved)
    n_out = len(out_shape)

    def body(*refs):
        a_ref, b_ref = refs[0], refs[1]
        extra = refs[2:2 + n_extra]
        c_in = refs[2 + n_extra:2 + n_extra + nc]
        outs = refs[2 + n_extra + nc:2 + n_extra + nc + n_out]
        c_out = refs[2 + n_extra + nc + n_out:2 + n_extra + 2 * nc + n_out]
        acc = refs[2 + n_extra + 2 * nc + n_out]
        sems = refs[3 + n_extra + 2 * nc + n_out:]
        i, j, k = pl.program_id(0), pl.program_id(1), pl.program_id(2)
        if nc:
            @pl.when((i == 0) & (j == 0) & (k == 0))
            def _():
                _exchange_start(kind, c_in, c_out, *sems)

        d = lax.dot_general(a_ref[...], b_ref[...], (dims, ((), ())), preferred_element_type=F32)

        def finish(total):
            vals = epi(total, *[e[...] for e in extra])
            for o, v in zip(outs, vals):
                o[...] = v.astype(o.dtype)

        if nk == 1:
            finish(d)
        else:
            @pl.when(k == 0)
            def _():
                acc[...] = d

            @pl.when(jnp.logical_and(k > 0, k < nk - 1))
            def _():
                acc[...] += d

            @pl.when(k == nk - 1)
            def _():
                finish(acc[...] + d)

        if nc:
            @pl.when((i == grid[0] - 1) & (j == grid[1] - 1) & (k == nk - 1))
            def _():
                _exchange_wait(kind, c_in, c_out, *sems)

    hbm = pl.BlockSpec(memory_space=pl.ANY)
    scratch = [pltpu.VMEM((tm, tn), F32)] + (_exchange_sems(nc) if nc else [])
    c_shape = [SDS(((NDEV,) + t.shape) if kind == "gather" else t.shape, t.dtype) for t in moved]
    sem = ("arbitrary",) * 3 if nc else ("parallel", "parallel", "arbitrary")
    return pl.pallas_call(
        body, grid=grid, in_specs=in_specs + [hbm] * nc, out_specs=out_specs + [hbm] * nc,
        out_shape=out_shape + c_shape, scratch_shapes=scratch, name=name, compiler_params=_params(sem))


def _mm_fwd(a, w, layout, kw, nw, *, epi, outs, extras=(), tm=1024, tn=1024, tk=2048, carry=None, name):
    m = a.shape[0]
    tm, tn, tk = min(tm, m), min(tn, nw), min(tk, kw)
    grid = (m // tm, nw // tn, kw // tk)
    o_spec = pl.BlockSpec((tm, tn), lambda i, j, k: (i, j))
    in_specs = [pl.BlockSpec((tm, tk), lambda i, j, k: (i, k)), _wspec(layout, kw, nw, tk, tn, lambda i, j, k: (k, j))]
    in_specs += [o_spec] * len(extras)
    call = _mm_call(name, grid, in_specs, [o_spec] * len(outs), [SDS((m, nw), dt) for dt in outs], ((1,), (0,)),
                    len(extras), epi, tm, tn, carry)
    return call(a, w, *extras, *(carry[1] if carry else ()))


def _mm_dx(g, w, layout, kw, nw, *, epi, outs, extras=(), tm=1024, tn=1024, tk=2048, carry=None, name):
    m = g.shape[0]
    tm, tn, tk = min(tm, m), min(tn, kw), min(tk, nw if layout == "plain" else nw // NDEV)
    grid = (m // tm, kw // tn, nw // tk)
    o_spec = pl.BlockSpec((tm, tn), lambda i, j, k: (i, j))
    in_specs = [pl.BlockSpec((tm, tk), lambda i, j, k: (i, k)), _wspec(layout, kw, nw, tn, tk, lambda i, j, k: (j, k))]
    in_specs += [o_spec] * len(extras)
    call = _mm_call(name, grid, in_specs, [o_spec] * len(outs), [SDS((m, kw), dt) for dt in outs], ((1,), (1,)),
                    len(extras), epi, tm, tn, carry)
    return call(g, w, *extras, *(carry[1] if carry else ()))


def _mm_dw(a, g, layout, *, tm=1024, tn=1024, tk=2048, carry=None, name):
    m, kw = a.shape
    nw = g.shape[1]
    tm, tn, tk = min(tm, kw), min(tn, nw if layout == "plain" else nw // NDEV), min(tk, m)
    grid = (kw // tm, nw // tn, m // tk)
    in_specs = [pl.BlockSpec((tk, tm), lambda i, j, k: (k, i)), pl.BlockSpec((tk, tn), lambda i, j, k: (k, j))]
    o_spec = _wspec(layout, kw, nw, tm, tn, lambda i, j, k: (i, j))
    call = _mm_call(name, grid, in_specs, [o_spec], [SDS(_wshape(layout, kw, nw), BF16)], ((0,), (0,)),
                    0, lambda t: (t,), tm, tn, carry)
    return call(a, g, *(carry[1] if carry else ()))


ROW_TILE = 256


def _rmsnorm_fwd(x, g, name):
    t, d = x.shape

    def body(x_ref, g_ref, h_ref):
        xv = x_ref[...]
        r = lax.rsqrt(jnp.mean(xv * xv, axis=-1, keepdims=True) + EPS)
        h_ref[...] = (xv * r * g_ref[...]).astype(BF16)

    row = pl.BlockSpec((ROW_TILE, d), lambda i: (i, 0))
    vec = pl.BlockSpec((1, d), lambda i: (0, 0))
    return pl.pallas_call(body, grid=(t // ROW_TILE,), in_specs=[row, vec], out_specs=row, out_shape=SDS((t, d), BF16),
                          name=name, compiler_params=_params(("parallel",)))(x, g)


def _rmsnorm_bwd(x, g, dh, dres, name):
    t, d = x.shape

    def body(x_ref, g_ref, dh_ref, dres_ref, dx_ref, dxb_ref, dg_ref):
        xv = x_ref[...]
        r = lax.rsqrt(jnp.mean(xv * xv, axis=-1, keepdims=True) + EPS)
        xh = xv * r
        dhv = dh_ref[...]
        gd = dhv * g_ref[...]
        dx = dres_ref[...] + r * (gd - xh * jnp.mean(gd * xh, axis=-1, keepdims=True))
        dx_ref[...] = dx
        dxb_ref[...] = dx.astype(BF16)
        part = jnp.sum(dhv * xh, axis=0, keepdims=True)

        @pl.when(pl.program_id(0) == 0)
        def _():
            dg_ref[...] = part

        @pl.when(pl.program_id(0) > 0)
        def _():
            dg_ref[...] += part

    row = pl.BlockSpec((ROW_TILE, d), lambda i: (i, 0))
    vec = pl.BlockSpec((1, d), lambda i: (0, 0))
    return pl.pallas_call(body, grid=(t // ROW_TILE,), in_specs=[row, vec, row, row], out_specs=[row, row, vec],
                          out_shape=[SDS((t, d), F32), SDS((t, d), BF16), SDS((1, d), F32)], name=name,
                          compiler_params=_params(("arbitrary",)))(x, g, dh, dres)


def _loss_head(x, tgt, g, name):
    t, d = x.shape

    def body(x_ref, t_ref, g_ref, dx_ref, dxb_ref, loss_ref, dg_ref):
        xv = x_ref[...]
        r = lax.rsqrt(jnp.mean(xv * xv, axis=-1, keepdims=True) + EPS)
        xh = xv * r
        gv = g_ref[...]
        err = xh * gv - t_ref[...]
        part_loss = 0.5 * jnp.sum(jnp.mean(err * err, axis=-1, keepdims=True), axis=0, keepdims=True)
        dy = err * (1.0 / d)
        gd = dy * gv
        dx = r * (gd - xh * jnp.mean(gd * xh, axis=-1, keepdims=True))
        dx_ref[...] = dx
        dxb_ref[...] = dx.astype(BF16)
        part_g = jnp.sum(dy * xh, axis=0, keepdims=True)
        part_l = jnp.broadcast_to(part_loss, (8, 128))

        @pl.when(pl.program_id(0) == 0)
        def _():
            dg_ref[...] = part_g
            loss_ref[...] = part_l

        @pl.when(pl.program_id(0) > 0)
        def _():
            dg_ref[...] += part_g
            loss_ref[...] += part_l

    row = pl.BlockSpec((ROW_TILE, d), lambda i: (i, 0))
    vec = pl.BlockSpec((1, d), lambda i: (0, 0))
    sc = pl.BlockSpec((8, 128), lambda i: (0, 0))
    return pl.pallas_call(body, grid=(t // ROW_TILE,), in_specs=[row, row, vec], out_specs=[row, row, sc, vec],
                          out_shape=[SDS((t, d), F32), SDS((t, d), BF16), SDS((8, 128), F32), SDS((1, d), F32)], name=name,
                          compiler_params=_params(("arbitrary",)))(x, tgt, g)


CONV_ROWS = 256
CONV_COLS = 2048


def _shift_down(cur, prev8, k):
    sh = pltpu.roll(cur, k, axis=0)
    ph = pltpu.roll(prev8, k, axis=0)
    rid = lax.broadcasted_iota(jnp.int32, ph.shape, 0)
    head = jnp.where(rid < k, ph, sh[0:8])
    return jnp.concatenate([head, sh[8:]], axis=0)


def _shift_up(cur, next8, k):
    n = cur.shape[0]
    sh = pltpu.roll(cur, n - k, axis=0)
    nh = pltpu.roll(next8, 8 - k, axis=0)
    rid = lax.broadcasted_iota(jnp.int32, nh.shape, 0)
    tail = jnp.where(rid >= 8 - k, nh, sh[n - 8:])
    return jnp.concatenate([sh[:n - 8], tail], axis=0)


def _conv_pre(cur, prev8, w, b):
    acc = w[3:4, :] * cur + b
    for k in range(1, CONV_WIDTH):
        acc = acc + w[3 - k:4 - k, :] * _shift_down(cur, prev8, k)
    return acc


def _conv_fwd(u, w, b, name):
    t, c = u.shape
    per = CONV_ROWS // 8

    def body(u_ref, p_ref, w_ref, b_ref, o_ref):
        prev8 = jnp.where(pl.program_id(1) == 0, 0.0, p_ref[...])
        pre = _conv_pre(u_ref[...], prev8, w_ref[...], b_ref[...])
        o_ref[...] = pre * _sigmoid(pre)

    cur = pl.BlockSpec((CONV_ROWS, CONV_COLS), lambda j, i: (i, j))
    prev = pl.BlockSpec((8, CONV_COLS), lambda j, i: (jnp.maximum(i * per - 1, 0), j))
    wsp = pl.BlockSpec((CONV_WIDTH, CONV_COLS), lambda j, i: (0, j))
    bsp = pl.BlockSpec((1, CONV_COLS), lambda j, i: (0, j))
    return pl.pallas_call(body, grid=(c // CONV_COLS, t // CONV_ROWS), in_specs=[cur, prev, wsp, bsp], out_specs=cur,
                          out_shape=SDS((t, c), F32), name=name, compiler_params=_params(("parallel", "parallel")))(u, u, w, b)


def _conv_bwd_pre(u, w, b, dout, name):
    t, c = u.shape
    per = CONV_ROWS // 8

    def body(u_ref, p_ref, w_ref, b_ref, d_ref, dp_ref, dw_ref):
        i = pl.program_id(1)
        cur = u_ref[...]
        prev8 = jnp.where(i == 0, 0.0, p_ref[...])
        pre = _conv_pre(cur, prev8, w_ref[...], b_ref[...])
        s = _sigmoid(pre)
        dpre = d_ref[...] * (s * (1.0 + pre * (1.0 - s)))
        dp_ref[...] = dpre
        rows = [jnp.sum(dpre * _shift_down(cur, prev8, 3 - k), axis=0, keepdims=True) for k in range(3)]
        rows.append(jnp.sum(dpre * cur, axis=0, keepdims=True))
        rows.append(jnp.sum(dpre, axis=0, keepdims=True))
        part = jnp.concatenate(rows + [jnp.zeros((3, cur.shape[1]), F32)], axis=0)

        @pl.when(i == 0)
        def _():
            dw_ref[...] = part

        @pl.when(i > 0)
        def _():
            dw_ref[...] += part

    cur = pl.BlockSpec((CONV_ROWS, CONV_COLS), lambda j, i: (i, j))
    prev = pl.BlockSpec((8, CONV_COLS), lambda j, i: (jnp.maximum(i * per - 1, 0), j))
    wsp = pl.BlockSpec((CONV_WIDTH, CONV_COLS), lambda j, i: (0, j))
    bsp = pl.BlockSpec((1, CONV_COLS), lambda j, i: (0, j))
    acc = pl.BlockSpec((8, CONV_COLS), lambda j, i: (0, j))
    return pl.pallas_call(body, grid=(c // CONV_COLS, t // CONV_ROWS), in_specs=[cur, prev, wsp, bsp, cur],
                          out_specs=[cur, acc], out_shape=[SDS((t, c), F32), SDS((8, c), F32)], name=name,
                          compiler_params=_params(("parallel", "arbitrary")))(u, u, w, b, dout)


def _conv_bwd_in(dpre, w, name):
    t, c = dpre.shape
    per = CONV_ROWS // 8
    last = t // CONV_ROWS - 1

    def body(d_ref, n_ref, w_ref, o_ref):
        cur = d_ref[...]
        next8 = jnp.where(pl.program_id(1) == last, 0.0, n_ref[...])
        wv = w_ref[...]
        acc = wv[3:4, :] * cur
        for k in range(1, CONV_WIDTH):
            acc = acc + wv[3 - k:4 - k, :] * _shift_up(cur, next8, k)
        o_ref[...] = acc.astype(BF16)

    cur = pl.BlockSpec((CONV_ROWS, CONV_COLS), lambda j, i: (i, j))
    nxt = pl.BlockSpec((8, CONV_COLS), lambda j, i: (jnp.minimum((i + 1) * per, t // 8 - 1), j))
    wsp = pl.BlockSpec((CONV_WIDTH, CONV_COLS), lambda j, i: (0, j))
    return pl.pallas_call(body, grid=(c // CONV_COLS, t // CONV_ROWS), in_specs=[cur, nxt, wsp], out_specs=cur,
                          out_shape=SDS((t, c), BF16), name=name, compiler_params=_params(("parallel", "parallel")))(dpre, dpre, w)


def _pieces(v, n):
    out, rest = [], v
    for _ in range(n):
        p = rest.astype(BF16)
        out.append(p)
        rest = rest - p.astype(F32)
    return out


def _head_expand(v):
    r = lax.broadcasted_iota(jnp.int32, (3 * 128, GROUP_W), 0) % 128
    c = lax.broadcasted_iota(jnp.int32, (3 * 128, GROUP_W), 1)
    return _dot(jnp.concatenate(_pieces(v, 3), axis=1), (c // SSM_HEAD_DIM == r).astype(BF16))


def _head_sum(vs):
    r = lax.broadcasted_iota(jnp.int32, (2 * GROUP_W, 128), 0) % GROUP_W
    c = lax.broadcasted_iota(jnp.int32, (2 * GROUP_W, 128), 1)
    stacked = jnp.concatenate([jnp.concatenate(_pieces(v, 2), axis=1) for v in vs], axis=0)
    out = _dot(stacked, (r // SSM_HEAD_DIM == c).astype(BF16))
    res, at = [], 0
    for v in vs:
        res.append(out[at:at + v.shape[0]])
        at += v.shape[0]
    return res


def _tri_left(tri, v):
    r = _dot(tri.astype(BF16), jnp.concatenate(_pieces(v, 3), axis=1))
    return r[:, 0:128] + r[:, 128:256] + r[:, 256:384]


def _dot(a, b):
    return jnp.dot(a, b, preferred_element_type=F32)


def _dot_nt(a, b):
    return lax.dot_general(a, b, (((1,), (1,)), ((), ())), preferred_element_type=F32)


def _dot_tn(a, b):
    return lax.dot_general(a, b, (((0,), (0,)), ((), ())), preferred_element_type=F32)


def _ssd_common(dtp, dtpt, bias, biasr, alog, alogr):
    li = lax.broadcasted_iota(jnp.int32, (CHUNK, CHUNK), 0)
    si = lax.broadcasted_iota(jnp.int32, (CHUNK, CHUNK), 1)
    lower, upper = (li >= si), (li <= si)
    dt = _softplus(dtp + bias)
    a_neg = -jnp.exp(alog)
    cs = _tri_left(lower, dt * a_neg)
    dtr = _softplus(dtpt + biasr)
    ar = jnp.concatenate([dtr * (-jnp.exp(alogr)), jnp.zeros((8, CHUNK), F32)], axis=0)
    r3 = lax.broadcasted_iota(jnp.int32, (3 * CHUNK, CHUNK), 0) % CHUNK
    c3 = lax.broadcasted_iota(jnp.int32, (3 * CHUNK, CHUNK), 1)
    csr = _dot(jnp.concatenate(_pieces(ar, 3), axis=1), (r3 <= c3).astype(BF16))[0:8]
    return lower, upper, dt, a_neg, cs, csr


def _head_masked_rows(v):
    hl = lax.broadcasted_iota(jnp.int32, v.shape, 1) // SSM_HEAD_DIM
    return jnp.concatenate([jnp.where(hl == j, v, jnp.zeros_like(v)) for j in range(8)], axis=0)


def _ssd_fwd(xbc, z, dtp_g, dtp_t, bias_g, bias_r, alog_g, alog_r, d_exp, norm_w, name):
    t = xbc.shape[0]
    nc = t // CHUNK

    def body(xs_ref, b_ref, c_ref, z_ref, dtp_ref, dtpt_ref, bias_ref, biasr_ref, alog_ref, alogr_ref, dexp_ref, nw_ref,
             y_ref, yn_ref, st_ref, state):
        c = pl.program_id(1)

        @pl.when(c == 0)
        def _():
            state[...] = jnp.zeros_like(state)

        lower, _, dt, a_neg, cs, csr = _ssd_common(dtp_ref[...], dtpt_ref[...], bias_ref[...], biasr_ref[...],
                                                alog_ref[...], alogr_ref[...])
        cs_e = _head_expand(cs)
        dt_e = _head_expand(dt)
        xs = xs_ref[...]
        xdt = xs * dt_e
        bm = b_ref[...]
        cm = c_ref[...]
        bmb, cmb = bm.astype(BF16), cm.astype(BF16)
        cb = _dot_nt(cmb, bmb)
        ms = []
        for j in range(8):
            dlt = cs_e[:, SSM_HEAD_DIM * j:SSM_HEAD_DIM * j + 1] - csr[j:j + 1, :]
            ms.append((cb * jnp.exp(jnp.where(lower, dlt, -jnp.inf))).astype(BF16))
        y = _dot(jnp.concatenate(ms, axis=1), _head_masked_rows(xdt.astype(BF16)))
        st_in = state[...]
        st_ref[...] = st_in
        y = y + jnp.exp(cs_e) * _dot(cmb, st_in.astype(BF16))
        cs_last = cs_e[CHUNK - 1:CHUNK, :]
        xdtd = (xdt * jnp.exp(cs_last - cs_e)).astype(BF16)
        state[...] = jnp.exp(cs_last) * st_in + _dot(bm.T.astype(BF16), xdtd)
        y_ref[...] = y
        zz = z_ref[...]
        y2 = (y + dexp_ref[...] * xs) * (zz * _sigmoid(zz))
        r = lax.rsqrt(jnp.mean(y2 * y2, axis=-1, keepdims=True) + EPS)
        yn_ref[...] = (y2 * r * nw_ref[...]).astype(BF16)

    gw = pl.BlockSpec((CHUNK, GROUP_W), lambda g, c: (c, g))
    in_specs = [
        gw,
        pl.BlockSpec((CHUNK, SSM_STATE), lambda g, c: (c, D_INNER // SSM_STATE + g)),
        pl.BlockSpec((CHUNK, SSM_STATE), lambda g, c: (c, D_INNER // SSM_STATE + SSM_GROUPS + g)),
        gw,
        pl.BlockSpec((None, CHUNK, 128), lambda g, c: (g, c, 0)),
        pl.BlockSpec((8, CHUNK), lambda g, c: (g, c)),
        pl.BlockSpec((None, 1, 128), lambda g, c: (g, 0, 0)),
        pl.BlockSpec((8, 128), lambda g, c: (g, 0)),
        pl.BlockSpec((None, 1, 128), lambda g, c: (g, 0, 0)),
        pl.BlockSpec((8, 128), lambda g, c: (g, 0)),
        pl.BlockSpec((1, GROUP_W), lambda g, c: (0, g)),
        pl.BlockSpec((1, GROUP_W), lambda g, c: (0, g)),
    ]
    out_specs = [gw, gw, pl.BlockSpec((None, SSM_STATE, GROUP_W), lambda g, c: (c, 0, g))]
    out_shape = [SDS((t, D_INNER), F32), SDS((t, D_INNER), BF16), SDS((nc, SSM_STATE, D_INNER), F32)]
    return pl.pallas_call(body, grid=(SSM_GROUPS, nc), in_specs=in_specs, out_specs=out_specs, out_shape=out_shape,
                          scratch_shapes=[pltpu.VMEM((SSM_STATE, GROUP_W), F32)], name=name,
                          compiler_params=_params(("parallel", "arbitrary")))(
        xbc, xbc, xbc, z, dtp_g, dtp_t, bias_g, bias_r, alog_g, alog_r, d_exp, norm_w)


def _ssd_bwd(dyn, y, xbc, z, states, dtp_g, dtp_t, bias_g, bias_r, alog_g, alog_r, d_exp, norm_w, name):
    t = xbc.shape[0]
    nc = t // CHUNK

    def body(dyn_ref, y_ref, xs_ref, b_ref, c_ref, z_ref, st_ref, dtp_ref, dtpt_ref, bias_ref, biasr_ref, alog_ref,
             alogr_ref, dexp_ref, nw_ref, dxs_ref, db_ref, dc_ref, dz_ref, ddt_ref, hsum_ref, csum_ref, dstate):
        step = pl.program_id(1)

        @pl.when(step == 0)
        def _():
            dstate[...] = jnp.zeros_like(dstate)

        dtp = dtp_ref[...]
        bias = bias_ref[...]
        lower, upper, dt, a_neg, cs, csr = _ssd_common(dtp, dtpt_ref[...], bias, biasr_ref[...], alog_ref[...],
                                                       alogr_ref[...])
        cs_e = _head_expand(cs)
        dt_e = _head_expand(dt)
        xs = xs_ref[...]
        xdt = xs * dt_e
        bm = b_ref[...]
        cm = c_ref[...]
        bmb, cmb = bm.astype(BF16), cm.astype(BF16)
        y = y_ref[...]
        dexp = dexp_ref[...]
        nw = nw_ref[...]

        zz = z_ref[...]
        sg = _sigmoid(zz)
        gate = zz * sg
        ytot = y + dexp * xs
        y2 = ytot * gate
        r = lax.rsqrt(jnp.mean(y2 * y2, axis=-1, keepdims=True) + EPS)
        dynv = dyn_ref[...]
        xh = y2 * r
        gn = dynv * nw
        dy2 = r * (gn - xh * jnp.mean(gn * xh, axis=-1, keepdims=True))
        dy = dy2 * gate
        dz_ref[...] = (dy2 * ytot * (sg * (1.0 + zz * (1.0 - sg)))).astype(BF16)
        csum_part = jnp.sum(dynv * xh, axis=0, keepdims=True)

        cb = _dot_nt(cmb, bmb)
        dyb = dy.astype(BF16)
        xdtb = xdt.astype(BF16)
        dym = _head_masked_rows(dyb)
        dm = _dot_nt(dym, xdtb)
        dmt = _dot_nt(_head_masked_rows(xdtb), dyb)
        lane = lax.broadcasted_iota(jnp.int32, (CHUNK, 128), 1)
        mts = []
        dcb = jnp.zeros((CHUNK, CHUNK), F32)
        dcs = jnp.zeros((CHUNK, 128), F32)
        for j in range(8):
            dlt = cs_e[:, SSM_HEAD_DIM * j:SSM_HEAD_DIM * j + 1] - csr[j:j + 1, :]
            lj = jnp.exp(jnp.where(lower, dlt, -jnp.inf))
            mj = cb * lj
            mjt = mj.T
            mts.append(mjt.astype(BF16))
            dmj = dm[CHUNK * j:CHUNK * (j + 1)]
            dcb = dcb + dmj * lj
            rows = jnp.sum(dmj * mj, axis=1, keepdims=True)
            cols = jnp.sum(dmt[CHUNK * j:CHUNK * (j + 1)] * mjt, axis=1, keepdims=True)
            dcs = dcs + jnp.where(lane == j, rows - cols, 0.0)
        dxdt = _dot(jnp.concatenate(mts, axis=1), dym)
        dst_out = dstate[...]
        dst_outb = dst_out.astype(BF16)
        st_in = st_ref[...]
        st_inb = st_in.astype(BF16)
        cs_last = cs_e[CHUNK - 1:CHUNK, :]
        decay = jnp.exp(cs_last - cs_e)
        e_last = jnp.exp(cs_last)
        gpart = decay * _dot(bmb, dst_outb)
        dxdt = dxdt + gpart
        dyw = (jnp.exp(cs_e) * dy).astype(BF16)
        dcbb = dcb.astype(BF16)
        dc_ref[...] = _dot_nt(dyw, st_inb) + _dot(dcbb, bmb)
        db_ref[...] = _dot_nt((xdt * decay).astype(BF16), dst_outb) + _dot(dcb.T.astype(BF16), cmb)
        dstate[...] = e_last * dst_out + _dot(cm.T.astype(BF16), dyw)
        y_off = jnp.exp(cs_e) * _dot(cmb, st_inb)
        xg = xdt * gpart
        vec = jnp.concatenate([jnp.sum(dy * xs, axis=0, keepdims=True),
                               jnp.sum(xg + dst_out * e_last * st_in, axis=0, keepdims=True),
                               jnp.zeros((14, GROUP_W), F32)], axis=0)
        s_cs, s_dt, s_vec = _head_sum([dy * y_off - xg, dxdt * xs, vec])
        d_skip = s_vec[0:1]
        ri = lax.broadcasted_iota(jnp.int32, (CHUNK, 128), 0)
        dcs = dcs + s_cs + jnp.where(ri == CHUNK - 1, s_vec[1:2], 0.0)
        da = _tri_left(upper, dcs)
        ddt = da * a_neg + s_dt
        dxs_ref[...] = dxdt * dt_e + dy * dexp
        ddtp = ddt * _sigmoid(dtp + bias)
        ddt_ref[...] = ddtp
        d_alog = jnp.sum(da * dt, axis=0, keepdims=True) * a_neg
        hpart = jnp.concatenate([jnp.sum(ddtp, axis=0, keepdims=True), d_alog, d_skip, jnp.zeros((5, 128), F32)], axis=0)
        cpart = jnp.concatenate([csum_part, jnp.zeros((7, GROUP_W), F32)], axis=0)

        @pl.when(step == 0)
        def _():
            hsum_ref[...] = hpart
            csum_ref[...] = cpart

        @pl.when(step > 0)
        def _():
            hsum_ref[...] += hpart
            csum_ref[...] += cpart

    rc = lambda c: nc - 1 - c
    gw = pl.BlockSpec((CHUNK, GROUP_W), lambda g, c: (rc(c), g))
    bsp = pl.BlockSpec((CHUNK, SSM_STATE), lambda g, c: (rc(c), D_INNER // SSM_STATE + g))
    csp = pl.BlockSpec((CHUNK, SSM_STATE), lambda g, c: (rc(c), D_INNER // SSM_STATE + SSM_GROUPS + g))
    in_specs = [
        gw, gw, gw, bsp, csp, gw,
        pl.BlockSpec((None, SSM_STATE, GROUP_W), lambda g, c: (rc(c), 0, g)),
        pl.BlockSpec((None, CHUNK, 128), lambda g, c: (g, rc(c), 0)),
        pl.BlockSpec((8, CHUNK), lambda g, c: (g, rc(c))),
        pl.BlockSpec((None, 1, 128), lambda g, c: (g, 0, 0)),
        pl.BlockSpec((8, 128), lambda g, c: (g, 0)),
        pl.BlockSpec((None, 1, 128), lambda g, c: (g, 0, 0)),
        pl.BlockSpec((8, 128), lambda g, c: (g, 0)),
        pl.BlockSpec((1, GROUP_W), lambda g, c: (0, g)),
        pl.BlockSpec((1, GROUP_W), lambda g, c: (0, g)),
    ]
    nsp = pl.BlockSpec((CHUNK, SSM_STATE), lambda g, c: (rc(c), g))
    out_specs = [gw, nsp, nsp, gw,
                 pl.BlockSpec((None, CHUNK, 128), lambda g, c: (g, rc(c), 0)),
                 pl.BlockSpec((None, 8, 128), lambda g, c: (g, 0, 0)),
                 pl.BlockSpec((8, GROUP_W), lambda g, c: (0, g))]
    gn = SSM_GROUPS * SSM_STATE
    out_shape = [SDS((t, D_INNER), F32), SDS((t, gn), F32), SDS((t, gn), F32), SDS((t, D_INNER), BF16),
                 SDS((SSM_GROUPS, t, 128), F32), SDS((SSM_GROUPS, 8, 128), F32), SDS((8, D_INNER), F32)]
    return pl.pallas_call(body, grid=(SSM_GROUPS, nc), in_specs=in_specs, out_specs=out_specs, out_shape=out_shape,
                          scratch_shapes=[pltpu.VMEM((SSM_STATE, GROUP_W), F32)], name=name,
                          compiler_params=_params(("parallel", "arbitrary")))(
        dyn, y, xbc, xbc, xbc, z, states, dtp_g, dtp_t, bias_g, bias_r, alog_g, alog_r, d_exp, norm_w)


BLK = 128
HEAD_PAIRS = ATTN_W // 128
ATTN_SCALE = 0.125


def _slopes(group):
    n = len(ATTN_GROUPS) * 16
    return [2.0 ** (-8.0 * (16 * group + h + 1) / n) for h in range(16)]


def _lane_lo(rows):
    return lax.broadcasted_iota(jnp.int32, (rows, 128), 1) < 64


def _attn_fwd(qkv, group, name):
    _, dil = ATTN_GROUPS[group]
    t = qkv.shape[0]
    sub = t // dil
    nb = sub // BLK
    slopes = _slopes(group)

    def body(q_ref, kp_ref, kc_ref, vp_ref, vc_ref, o_ref, l_ref):
        n = pl.program_id(1)
        qi = lax.broadcasted_iota(jnp.int32, (BLK, 2 * BLK), 0)
        kj = lax.broadcasted_iota(jnp.int32, (BLK, 2 * BLK), 1)
        dist = qi + BLK - kj
        valid = (dist >= 0) & (dist <= BLK) & ((kj >= BLK) | (n > 0))
        distf = dist.astype(F32) * float(dil)
        lo_q, lo_k = _lane_lo(BLK), _lane_lo(2 * BLK)
        for hp in range(HEAD_PAIRS):
            sl = slice(128 * hp, 128 * hp + 128)
            q2 = q_ref[:, sl]
            k2 = jnp.concatenate([kp_ref[:, sl], kc_ref[:, sl]], axis=0)
            v2 = jnp.concatenate([vp_ref[:, sl], vc_ref[:, sl]], axis=0)
            o2 = jnp.zeros((BLK, 128), F32)
            l2 = jnp.zeros((BLK, 128), F32)
            for s01 in range(2):
                mq = lo_q if s01 == 0 else ~lo_q
                mk = lo_k if s01 == 0 else ~lo_k
                s = _dot_nt(jnp.where(mq, q2, jnp.zeros_like(q2)), k2) * ATTN_SCALE - slopes[2 * hp + s01] * distf
                s = jnp.where(valid, s, -jnp.inf)
                mx = jnp.max(s, axis=-1, keepdims=True)
                p = jnp.exp(s - mx)
                den = jnp.sum(p, axis=-1, keepdims=True)
                o2 = o2 + _dot(p.astype(BF16), jnp.where(mk, v2, jnp.zeros_like(v2))) / den
                l2 = jnp.where(mq, mx + jnp.log(den), l2)
            o_ref[:, sl] = o2
            l_ref[:, sl] = l2

    ncol = QKV_DIM // ATTN_W

    def col(which):
        return lambda r, n: (n, r * ncol + 3 * which + group)

    def colp(which):
        return lambda r, n: (jnp.maximum(n - 1, 0), r * ncol + 3 * which + group)

    blk = (BLK, ATTN_W)
    in_specs = [pl.BlockSpec(blk, col(0)), pl.BlockSpec(blk, colp(1)), pl.BlockSpec(blk, col(1)),
                pl.BlockSpec(blk, colp(2)), pl.BlockSpec(blk, col(2))]
    osp = pl.BlockSpec(blk, lambda r, n: (n, r))
    qv = qkv.reshape(sub, dil * QKV_DIM)
    o, l = pl.pallas_call(body, grid=(dil, nb), in_specs=in_specs, out_specs=[osp, osp],
                          out_shape=[SDS((sub, dil * ATTN_W), F32), SDS((sub, dil * ATTN_W), F32)], name=name,
                          compiler_params=_params(("parallel", "parallel")))(qv, qv, qv, qv, qv)
    return o.reshape(t, ATTN_W), l.reshape(t, ATTN_W)


def _attn_bwd_q(qkv, do, lse, dvec, group, name):
    _, dil = ATTN_GROUPS[group]
    t = qkv.shape[0]
    sub = t // dil
    nb = sub // BLK
    slopes = _slopes(group)

    def body(q_ref, kp_ref, kc_ref, vp_ref, vc_ref, do_ref, l_ref, dv_ref, dq_ref):
        n = pl.program_id(1)
        qi = lax.broadcasted_iota(jnp.int32, (BLK, 2 * BLK), 0)
        kj = lax.broadcasted_iota(jnp.int32, (BLK, 2 * BLK), 1)
        dist = qi + BLK - kj
        valid = (dist >= 0) & (dist <= BLK) & ((kj >= BLK) | (n > 0))
        distf = dist.astype(F32) * float(dil)
        lo_q, lo_k = _lane_lo(BLK), _lane_lo(2 * BLK)
        for hp in range(HEAD_PAIRS):
            sl = slice(128 * hp, 128 * hp + 128)
            q2 = q_ref[:, sl]
            k2 = jnp.concatenate([kp_ref[:, sl], kc_ref[:, sl]], axis=0)
            v2 = jnp.concatenate([vp_ref[:, sl], vc_ref[:, sl]], axis=0)
            do2 = do_ref[:, sl]
            l2 = l_ref[:, sl]
            d2 = dv_ref[:, sl]
            dq2 = jnp.zeros((BLK, 128), F32)
            for s01 in range(2):
                mq = lo_q if s01 == 0 else ~lo_q
                mk = lo_k if s01 == 0 else ~lo_k
                s = _dot_nt(jnp.where(mq, q2, jnp.zeros_like(q2)), k2) * ATTN_SCALE - slopes[2 * hp + s01] * distf
                p = jnp.exp(jnp.where(valid, s - l2[:, 64 * s01:64 * s01 + 1], -jnp.inf))
                dp = _dot_nt(jnp.where(mq, do2, jnp.zeros_like(do2)), v2)
                ds = p * (dp - d2[:, 64 * s01:64 * s01 + 1])
                dq2 = dq2 + _dot(ds.astype(BF16), jnp.where(mk, k2, jnp.zeros_like(k2))) * ATTN_SCALE
            dq_ref[:, sl] = dq2.astype(BF16)

    ncol = QKV_DIM // ATTN_W

    def col(which):
        return lambda r, n: (n, r * ncol + 3 * which + group)

    def colp(which):
        return lambda r, n: (jnp.maximum(n - 1, 0), r * ncol + 3 * which + group)

    blk = (BLK, ATTN_W)
    osp = pl.BlockSpec(blk, lambda r, n: (n, r))
    in_specs = [pl.BlockSpec(blk, col(0)), pl.BlockSpec(blk, colp(1)), pl.BlockSpec(blk, col(1)),
                pl.BlockSpec(blk, colp(2)), pl.BlockSpec(blk, col(2)), osp, osp, osp]
    qv = qkv.reshape(sub, dil * QKV_DIM)
    view = lambda a: a.reshape(sub, dil * ATTN_W)
    dq = pl.pallas_call(body, grid=(dil, nb), in_specs=in_specs, out_specs=osp,
                        out_shape=SDS((sub, dil * ATTN_W), BF16), name=name,
                        compiler_params=_params(("parallel", "parallel")))(qv, qv, qv, qv, qv, view(do), view(lse), view(dvec))
    return dq.reshape(t, ATTN_W)


def _attn_bwd_kv(qkv, do, lse, dvec, group, name):
    _, dil = ATTN_GROUPS[group]
    t = qkv.shape[0]
    sub = t // dil
    nb = sub // BLK
    slopes = _slopes(group)

    def body(kc_ref, vc_ref, qa_ref, qb_ref, doa_ref, dob_ref, la_ref, lb_ref, da_ref, db_ref, dk_ref, dv_ref):
        m = pl.program_id(1)
        qi = lax.broadcasted_iota(jnp.int32, (2 * BLK, BLK), 0)
        kj = lax.broadcasted_iota(jnp.int32, (2 * BLK, BLK), 1)
        dist = qi - kj
        valid = (dist >= 0) & (dist <= BLK) & ((qi < BLK) | (m < nb - 1))
        distf = dist.astype(F32) * float(dil)
        lo_q = _lane_lo(2 * BLK)
        for hp in range(HEAD_PAIRS):
            sl = slice(128 * hp, 128 * hp + 128)
            k2 = kc_ref[:, sl]
            v2 = vc_ref[:, sl]
            q2 = jnp.concatenate([qa_ref[:, sl], qb_ref[:, sl]], axis=0)
            do2 = jnp.concatenate([doa_ref[:, sl], dob_ref[:, sl]], axis=0)
            l2 = jnp.concatenate([la_ref[:, sl], lb_ref[:, sl]], axis=0)
            d2 = jnp.concatenate([da_ref[:, sl], db_ref[:, sl]], axis=0)
            dk2 = jnp.zeros((BLK, 128), F32)
            dv2 = jnp.zeros((BLK, 128), F32)
            for s01 in range(2):
                mq = lo_q if s01 == 0 else ~lo_q
                qm = jnp.where(mq, q2, jnp.zeros_like(q2))
                dom = jnp.where(mq, do2, jnp.zeros_like(do2))
                s = _dot_nt(qm, k2) * ATTN_SCALE - slopes[2 * hp + s01] * distf
                p = jnp.exp(jnp.where(valid, s - l2[:, 64 * s01:64 * s01 + 1], -jnp.inf))
                dv2 = dv2 + _dot_tn(p.astype(BF16), dom)
                dp = _dot_nt(dom, v2)
                ds = p * (dp - d2[:, 64 * s01:64 * s01 + 1])
                dk2 = dk2 + _dot_tn(ds.astype(BF16), qm) * ATTN_SCALE
            dk_ref[:, sl] = dk2.astype(BF16)
            dv_ref[:, sl] = dv2.astype(BF16)

    ncol = QKV_DIM // ATTN_W
    blk = (BLK, ATTN_W)

    def col(which):
        return lambda r, m: (m, r * ncol + 3 * which + group)

    cur = pl.BlockSpec(blk, lambda r, m: (m, r))
    nxt = pl.BlockSpec(blk, lambda r, m: (jnp.minimum(m + 1, nb - 1), r))
    qn = pl.BlockSpec(blk, lambda r, m: (jnp.minimum(m + 1, nb - 1), r * ncol + group))
    in_specs = [pl.BlockSpec(blk, col(1)), pl.BlockSpec(blk, col(2)), pl.BlockSpec(blk, col(0)), qn,
                cur, nxt, cur, nxt, cur, nxt]
    qv = qkv.reshape(sub, dil * QKV_DIM)
    view = lambda a: a.reshape(sub, dil * ATTN_W)
    dov, lv, dvv = view(do), view(lse), view(dvec)
    dk, dv = pl.pallas_call(body, grid=(dil, nb), in_specs=in_specs, out_specs=[cur, cur],
                            out_shape=[SDS((sub, dil * ATTN_W), BF16)] * 2, name=name,
                            compiler_params=_params(("parallel", "parallel")))(qv, qv, qv, qv, dov, dov, lv, lv, dvv, dvv)
    return dk.reshape(t, ATTN_W), dv.reshape(t, ATTN_W)


def _combine_weights(l0, l1, l2):
    mx = jnp.maximum(jnp.maximum(l0, l1), l2)
    e0, e1, e2 = jnp.exp(l0 - mx), jnp.exp(l1 - mx), jnp.exp(l2 - mx)
    den = e0 + e1 + e2
    return e0 / den, e1 / den, e2 / den


def _combine_fwd(os_, ls_, name):
    t = os_[0].shape[0]

    def body(o0, o1, o2, l0, l1, l2, out):
        w0, w1, w2 = _combine_weights(l0[...], l1[...], l2[...])
        out[...] = (w0 * o0[...] + w1 * o1[...] + w2 * o2[...]).astype(BF16)

    row = pl.BlockSpec((ROW_TILE, ATTN_W), lambda i: (i, 0))
    return pl.pallas_call(body, grid=(t // ROW_TILE,), in_specs=[row] * 6, out_specs=row, out_shape=SDS((t, ATTN_W), BF16),
                          name=name, compiler_params=_params(("parallel",)))(*os_, *ls_)


def _combine_bwd(do, os_, ls_, name):
    t = do.shape[0]

    def body(do_ref, o0, o1, o2, l0, l1, l2, g0, g1, g2, d0, d1, d2):
        w0, w1, w2 = _combine_weights(l0[...], l1[...], l2[...])
        dov = do_ref[...]
        prod = dov * (w0 * o0[...] + w1 * o1[...] + w2 * o2[...])
        r = (lax.broadcasted_iota(jnp.int32, (3 * 128, 128), 0) % 128) // 64
        c = lax.broadcasted_iota(jnp.int32, (3 * 128, 128), 1) // 64
        same = (r == c).astype(BF16)
        tbar = jnp.concatenate([_dot(jnp.concatenate(_pieces(prod[:, 128 * k:128 * k + 128], 3), axis=1), same)
                                for k in range(HEAD_PAIRS)], axis=1)
        for w, g, d in ((w0, g0, d0), (w1, g1, d1), (w2, g2, d2)):
            g[...] = (w * dov).astype(BF16)
            d[...] = w * tbar

    row = pl.BlockSpec((ROW_TILE, ATTN_W), lambda i: (i, 0))
    return pl.pallas_call(body, grid=(t // ROW_TILE,), in_specs=[row] * 7, out_specs=[row] * 6,
                          out_shape=[SDS((t, ATTN_W), BF16)] * 3 + [SDS((t, ATTN_W), F32)] * 3, name=name,
                          compiler_params=_params(("parallel",)))(do, *os_, *ls_)


def _peer(k):
    x, y, c = lax.axis_index("x"), lax.axis_index("y"), lax.axis_index("c")
    px = 1 - x if k & 4 else x
    py = 1 - y if k & 2 else y
    pc = 1 - c if k & 1 else c
    return (px, py, pc), 4 * px + 2 * py + pc


def _my_index():
    return 4 * lax.axis_index("x") + 2 * lax.axis_index("y") + lax.axis_index("c")


def _exchange_sems(n):
    return [pltpu.SemaphoreType.DMA((n * (NDEV - 1),)), pltpu.SemaphoreType.DMA((n * (NDEV - 1),)),
            pltpu.SemaphoreType.DMA((n,))]


def _exchange_copies(kind, ins, outs, send, recv, local, arrivals):
    me = _my_index()
    own, sent, arriving = [], [], []
    for i in range(len(ins)):
        own.append(pltpu.make_async_copy(ins[i] if kind == "gather" else ins[i].at[me], outs[i].at[me], local.at[i]))
        for k in range(1, NDEV):
            peer, pidx = _peer(k)
            s = i * (NDEV - 1) + k - 1
            src = ins[i] if kind == "gather" else ins[i].at[pidx]
            for dst, into in ((outs[i].at[me], sent), (outs[i].at[pidx], arriving)):
                if into is sent or arrivals:
                    into.append(pltpu.make_async_remote_copy(src_ref=src, dst_ref=dst, send_sem=send.at[s],
                                                             recv_sem=recv.at[s], device_id=peer, device_id_type=MESH))
    return own, sent, arriving


def _exchange_start(kind, ins, outs, send, recv, local):
    own, sent, _ = _exchange_copies(kind, ins, outs, send, recv, local, arrivals=False)
    for cp in own + sent:
        cp.start()


def _exchange_wait(kind, ins, outs, send, recv, local):
    own, sent, arriving = _exchange_copies(kind, ins, outs, send, recv, local, arrivals=True)
    for cp in sent:
        cp.wait_send()
    for cp in arriving:
        cp.wait_recv()
    for cp in own:
        cp.wait()


def _exchange(kind, tensors, name):
    n = len(tensors)

    def body(*refs):
        _exchange_start(kind, refs[:n], refs[n:2 * n], *refs[2 * n:])
        _exchange_wait(kind, refs[:n], refs[n:2 * n], *refs[2 * n:])

    hbm = pl.BlockSpec(memory_space=pl.ANY)
    shapes = [SDS(((NDEV,) + t.shape) if kind == "gather" else t.shape, t.dtype) for t in tensors]
    return pl.pallas_call(body, in_specs=[hbm] * n, out_specs=[hbm] * n, out_shape=shapes,
                          scratch_shapes=_exchange_sems(n), name=name)(*tensors)


def _all_reduce_small(v, name):
    rows = v.shape[0]

    def body(v_ref, out_ref, land, send, recv):
        me = _my_index()
        land[me] = v_ref[...]
        remote = []
        for k in range(1, NDEV):
            peer, _ = _peer(k)
            cp = pltpu.make_async_remote_copy(src_ref=v_ref, dst_ref=land.at[me], send_sem=send.at[k - 1],
                                              recv_sem=recv.at[k - 1], device_id=peer, device_id_type=MESH)
            cp.start()
            remote.append(cp)
        for cp in remote:
            cp.wait_send()
        for k in range(1, NDEV):
            peer, pidx = _peer(k)
            pltpu.make_async_remote_copy(src_ref=v_ref, dst_ref=land.at[pidx], send_sem=send.at[k - 1],
                                         recv_sem=recv.at[k - 1], device_id=peer, device_id_type=MESH).wait_recv()
        total = land[0]
        for d in range(1, NDEV):
            total = total + land[d]
        out_ref[...] = total

    vm = pl.BlockSpec(memory_space=pltpu.VMEM)
    return pl.pallas_call(
        body, in_specs=[vm], out_specs=vm, out_shape=SDS((rows, 128), F32),
        scratch_shapes=[pltpu.VMEM((NDEV, rows, 128), F32), pltpu.SemaphoreType.DMA((NDEV - 1,)),
                        pltpu.SemaphoreType.DMA((NDEV - 1,))],
        name=name)(v)


def _adamw_math(w, g, m, v):
    m = ADAM_B1 * m + (1.0 - ADAM_B1) * g
    v = ADAM_B2 * v + (1.0 - ADAM_B2) * (g * g)
    m_hat = m / (1.0 - ADAM_B1 ** ADAM_STEP)
    v_hat = v / (1.0 - ADAM_B2 ** ADAM_STEP)
    delta = -ADAM_LR * (m_hat / (jnp.sqrt(v_hat) + ADAM_EPS) + ADAM_WD * w)
    return delta, m, v


def _row_tile(rows, cols):
    tr = rows
    while tr * cols * 4 > (1 << 20) and tr % 16 == 0:
        tr //= 2
    return tr


def _adamw(g, w, m, v, name):
    rows, cols = w.shape
    tr = _row_tile(rows, cols)

    def body(g_ref, w_ref, m_ref, v_ref, d_out, m_out, v_out):
        d, mn, vn = _adamw_math(w_ref[...], g_ref[...], m_ref[...], v_ref[...])
        d_out[...] = d
        m_out[...] = mn
        v_out[...] = vn

    sp = pl.BlockSpec((tr, cols), lambda i: (i, 0))
    return pl.pallas_call(body, grid=(rows // tr,), in_specs=[sp] * 4, out_specs=[sp] * 3,
                          out_shape=[SDS((rows, cols), F32)] * 3, name=name, compiler_params=_params(("parallel",)))(g, w, m, v)


def _reduce_adamw(parts, w, m, v, name):
    rows, cols = w.shape
    tr = _row_tile(rows, cols)

    def body(p_ref, w_ref, m_ref, v_ref, g_out, d_out, m_out, v_out):
        g = p_ref[0].astype(F32)
        for d in range(1, NDEV):
            g = g + p_ref[d].astype(F32)
        g_out[...] = g
        dl, mn, vn = _adamw_math(w_ref[...], g, m_ref[...], v_ref[...])
        d_out[...] = dl
        m_out[...] = mn
        v_out[...] = vn

    sp = pl.BlockSpec((tr, cols), lambda i: (i, 0))
    psp = pl.BlockSpec((NDEV, tr, cols), lambda i: (0, i, 0))
    return pl.pallas_call(body, grid=(rows // tr,), in_specs=[psp, sp, sp, sp], out_specs=[sp] * 4,
                          out_shape=[SDS((rows, cols), F32)] * 4, name=name, compiler_params=_params(("parallel",)))(parts, w, m, v)


def _pack(items):
    rows = []
    for a in items:
        a = a.reshape(-1).astype(F32)
        pad = (-a.shape[0]) % 128
        rows.append(jnp.pad(a, (0, pad)).reshape(-1, 128))
    out = jnp.concatenate(rows, axis=0)
    return jnp.pad(out, ((0, (-out.shape[0]) % 8), (0, 0)))


def _unpack(packed, shapes):
    out, r = [], 0
    for shp in shapes:
        n = math.prod(shp)
        nr = -(-n // 128)
        out.append(packed[r:r + nr].reshape(-1)[:n].reshape(shp))
        r += nr
    return out


def _ident(t):
    return (t,)


def _add(t, res):
    return (t + res,)


def kernel(x, norm_mix, norm_mlp, ssm_w_in, ssm_conv_w, ssm_conv_b, ssm_dt_bias, ssm_a_log, ssm_d, ssm_norm_w, ssm_w_out, attn_w_qkv, attn_w_o, mlp_w1, mlp_w2, final_norm, loss_target, m_norm_mix, m_norm_mlp, m_ssm_w_in, m_ssm_conv_w, m_ssm_conv_b, m_ssm_dt_bias, m_ssm_a_log, m_ssm_d, m_ssm_norm_w, m_ssm_w_out, m_attn_w_qkv, m_attn_w_o, m_mlp_w1, m_mlp_w2, m_final_norm, v_norm_mix, v_norm_mlp, v_ssm_w_in, v_ssm_conv_w, v_ssm_conv_b, v_ssm_dt_bias, v_ssm_a_log, v_ssm_d, v_ssm_norm_w, v_ssm_w_out, v_attn_w_qkv, v_attn_w_o, v_mlp_w1, v_mlp_w2, v_final_norm):
    t = x.shape[1]
    x0 = x.reshape(t, D_MODEL)
    tgt = loss_target.reshape(t, D_MODEL)
    me = _my_index()
    in_dim = D_INNER + CONV_DIM + SSM_HEADS
    in_shard = in_dim // NDEV
    zx_dim = D_INNER + CONV_DIM

    s_out, s_qkv, s_o = ssm_w_out[0].astype(BF16), attn_w_qkv[0].astype(BF16), attn_w_o[0].astype(BF16)
    s_w1, s_w2 = mlp_w1.astype(BF16), mlp_w2.astype(BF16)
    g_in, g_cw = _exchange("gather", [ssm_w_in[0].astype(BF16), ssm_conv_w[0]], "gather_in_proj")
    w_in = jnp.transpose(g_in, (1, 0, 2)).reshape(D_MODEL, in_dim)
    w_z, w_x = w_in[:, :D_INNER], w_in[:, D_INNER:zx_dim]
    w_dt = jnp.pad(w_in[:, zx_dim:], ((0, 0), (0, 128 - SSM_HEADS)))
    conv_w = jnp.transpose(g_cw, (1, 0, 2)).reshape(CONV_WIDTH, CONV_DIM)
    g_w1, g_w2 = [None, None], [None, None]

    def lanes(p):
        return jnp.pad(p.reshape(SSM_GROUPS, 1, 8), ((0, 0), (0, 0), (0, 120)))

    def rows(p):
        return jnp.broadcast_to(p.reshape(SSM_HEADS, 1), (SSM_HEADS, 128))

    bias_g, bias_r = lanes(ssm_dt_bias[0]), rows(ssm_dt_bias[0])
    alog_g, alog_r = lanes(ssm_a_log[0]), rows(ssm_a_log[0])
    d_exp = jnp.repeat(ssm_d[0], SSM_HEAD_DIM).reshape(1, D_INNER)
    norm_w = ssm_norm_w

    def relu2(tot):
        r = jnp.maximum(tot, 0.0)
        return r, r * r

    def mlp_fwd(xin, layer, tag, down_carry):
        h = _rmsnorm_fwd(xin, norm_mlp[layer:layer + 1], f"norm_mlp{tag}")
        r, a, w2 = _mm_fwd(h, g_w1[layer], "cols", D_MODEL, D_FF, epi=relu2, outs=(BF16, BF16),
                           carry=("gather", [s_w2[layer]]), name=f"mlp_up{tag}")
        g_w2[layer] = w2.reshape(D_FF, D_MODEL)
        xout, *got = _mm_fwd(a, g_w2[layer], "plain", D_FF, D_MODEL, epi=_add, outs=(F32,), extras=(xin,),
                             carry=("gather", down_carry) if down_carry else None, name=f"mlp_down{tag}")
        return h, r, a, xout, got

    h0 = _rmsnorm_fwd(x0, norm_mix[0:1], "norm_mix0")
    z, g_out = _mm_fwd(h0, w_z, "plain", D_MODEL, D_INNER, epi=_ident, outs=(F32,), carry=("gather", [s_out]), name="ssm_in_z")
    xpre, g_w1[0] = _mm_fwd(h0, w_x, "plain", D_MODEL, CONV_DIM, epi=_ident, outs=(F32,), carry=("gather", [s_w1[0]]),
                            name="ssm_in_x")
    dtp, = _mm_fwd(h0, w_dt, "plain", D_MODEL, 128, epi=_ident, outs=(F32,), name="ssm_in_dt")
    xbc = _conv_fwd(xpre, conv_w, ssm_conv_b, "conv_fwd")
    dtp64 = dtp[:, :SSM_HEADS]
    dtp_g = jnp.pad(jnp.transpose(dtp64.reshape(t, SSM_GROUPS, 8), (1, 0, 2)), ((0, 0), (0, 0), (0, 120)))
    dtp_t = jnp.transpose(dtp64)
    y_ssd, yn, states = _ssd_fwd(xbc, z, dtp_g, dtp_t, bias_g, bias_r, alog_g, alog_r, d_exp, norm_w, "ssd_fwd")
    g_out = g_out.reshape(D_INNER, D_MODEL)
    x1, g_o = _mm_fwd(yn, g_out, "plain", D_INNER, D_MODEL, epi=_add, outs=(F32,), extras=(x0,), carry=("gather", [s_o]),
                      name="ssm_out")
    h1, r1, a1, x2, (g_qkv,) = mlp_fwd(x1, 0, "0", [s_qkv])

    h2 = _rmsnorm_fwd(x2, norm_mix[1:2], "norm_mix1")
    qkv, g_w1[1] = _mm_fwd(h2, g_qkv, "cols", D_MODEL, QKV_DIM, epi=_ident, outs=(BF16,), tn=QKV_DIM // NDEV,
                           carry=("gather", [s_w1[1]]), name="attn_qkv")
    att = [_attn_fwd(qkv, g, f"attn_fwd{g}") for g in range(3)]
    os_, ls_ = [a[0] for a in att], [a[1] for a in att]
    o_mix = _combine_fwd(os_, ls_, "attn_combine")
    x3, = _mm_fwd(o_mix, g_o, "cols", ATTN_W, D_MODEL, epi=_add, outs=(F32,), extras=(x2,), tn=D_MODEL // NDEV, name="attn_out")
    h3, r3, a3, x4, _ = mlp_fwd(x3, 1, "1", None)

    dx4, dx4b, loss_acc, d_final = _loss_head(x4, tgt, final_norm.reshape(1, D_MODEL), "loss_head")

    def mlp_bwd(xin, h, r, a, dxo, dxob, layer, tag):
        du, = _mm_dx(dxob, g_w2[layer], "plain", D_FF, D_MODEL, epi=lambda tot, rr: (tot * (2.0 * rr.astype(F32)),),
                     outs=(BF16,), extras=(r,), name=f"mlp_down_dx{tag}")
        dw2, = _mm_dw(a, dxob, "plain", name=f"mlp_down_dw{tag}")
        dw2 = dw2.reshape(NDEV, D_FF // NDEV, D_MODEL)
        dw1, p_dw2 = _mm_dw(h, du, "cols", carry=("scatter", [dw2]), name=f"mlp_up_dw{tag}")
        dh, p_dw1 = _mm_dx(du, g_w1[layer], "cols", D_MODEL, D_FF, epi=_ident, outs=(F32,), carry=("scatter", [dw1]),
                           name=f"mlp_up_dx{tag}")
        dxi, dxib, dg = _rmsnorm_bwd(xin, norm_mlp[layer:layer + 1], dh, dxo, f"norm_mlp_bwd{tag}")
        return dxi, dxib, dg, p_dw1, p_dw2

    dx3, dx3b, dg_mlp1, p_w1_1, p_w2_1 = mlp_bwd(x3, h3, r3, a3, dx4, dx4b, 1, "1")

    dw_o, = _mm_dw(o_mix, dx3b, "cols", tn=D_MODEL // NDEV, name="attn_out_dw")
    do, p_o = _mm_dx(dx3b, g_o, "cols", ATTN_W, D_MODEL, epi=_ident, outs=(F32,), tk=D_MODEL // NDEV,
                     carry=("scatter", [dw_o]), name="attn_out_dx")
    cb = _combine_bwd(do, os_, ls_, "attn_combine_bwd")
    dos, dvecs = cb[:3], cb[3:]
    dqs, dks, dvs = [], [], []
    for g in range(3):
        dqs.append(_attn_bwd_q(qkv, dos[g], ls_[g], dvecs[g], g, f"attn_bwd_q{g}"))
        dk, dv = _attn_bwd_kv(qkv, dos[g], ls_[g], dvecs[g], g, f"attn_bwd_kv{g}")
        dks.append(dk)
        dvs.append(dv)
    dqkv = jnp.concatenate(dqs + dks + dvs, axis=1)
    dw_qkv, = _mm_dw(h2, dqkv, "cols", tn=QKV_DIM // NDEV, name="attn_qkv_dw")
    dh2, p_qkv = _mm_dx(dqkv, g_qkv, "cols", D_MODEL, QKV_DIM, epi=_ident, outs=(F32,), tk=QKV_DIM // NDEV,
                        carry=("scatter", [dw_qkv]), name="attn_qkv_dx")
    dx2, dx2b, dg_mix1 = _rmsnorm_bwd(x2, norm_mix[1:2], dh2, dx3, "norm_mix_bwd1")

    dx1, dx1b, dg_mlp0, p_w1_0, p_w2_0 = mlp_bwd(x1, h1, r1, a1, dx2, dx2b, 0, "0")

    dw_out, = _mm_dw(yn, dx1b, "plain", name="ssm_out_dw")
    dw_out = dw_out.reshape(NDEV, D_INNER // NDEV, D_MODEL)
    dyn, p_out = _mm_dx(dx1b, g_out, "plain", D_INNER, D_MODEL, epi=_ident, outs=(F32,), carry=("scatter", [dw_out]),
                        name="ssm_out_dx")
    dxs, d_b, d_c, dz, ddtp_g, hsums, csums = _ssd_bwd(dyn, y_ssd, xbc, z, states, dtp_g, dtp_t, bias_g, bias_r,
                                                       alog_g, alog_r, d_exp, norm_w, "ssd_bwd")
    dxbc = jnp.concatenate([dxs, d_b, d_c], axis=1)
    dpre, conv_sums = _conv_bwd_pre(xpre, conv_w, ssm_conv_b, dxbc, "conv_bwd_pre")
    du = _conv_bwd_in(dpre, conv_w, "conv_bwd_in")
    ddtp = jnp.transpose(ddtp_g[:, :, :8], (1, 0, 2)).reshape(t, SSM_HEADS)
    ddtp = jnp.pad(ddtp, ((0, 0), (0, 128 - SSM_HEADS))).astype(BF16)
    dw_z, = _mm_dw(h0, dz, "plain", name="ssm_in_z_dw")
    dw_x, = _mm_dw(h0, du, "plain", name="ssm_in_x_dw")
    dw_dt, = _mm_dw(h0, ddtp, "plain", name="ssm_in_dt_dw")
    dw_in = jnp.concatenate([dw_z, dw_x, dw_dt[:, :SSM_HEADS]], axis=1)
    dw_in = jnp.transpose(dw_in.reshape(D_MODEL, NDEV, in_shard), (1, 0, 2))
    dh0, = _mm_dx(ddtp, w_dt, "plain", D_MODEL, 128, epi=_ident, outs=(F32,), name="ssm_in_dt_dx")
    dh0, = _mm_dx(dz, w_z, "plain", D_MODEL, D_INNER, epi=_add, outs=(F32,), extras=(dh0,), name="ssm_in_z_dx")
    dh0, p_in = _mm_dx(du, w_x, "plain", D_MODEL, CONV_DIM, epi=_add, outs=(F32,), extras=(dh0,),
                       carry=("scatter", [dw_in]), name="ssm_in_x_dx")
    dx0, _, dg_mix0 = _rmsnorm_bwd(x0, norm_mix[0:1], dh0, dx1, "norm_mix_bwd0")

    parts = [p_in, p_out, p_qkv, p_o, p_w1_0, p_w1_1, p_w2_0, p_w2_1]

    def own(w, mm, vv):
        shp = w.shape
        f = lambda a: a.reshape(-1, shp[-1])
        return f(w), f(mm), f(vv), shp

    big = {}
    for key, part, (w, mm, vv) in (
            ("ssm_w_in", parts[0], (ssm_w_in, m_ssm_w_in, v_ssm_w_in)),
            ("ssm_w_out", parts[1], (ssm_w_out, m_ssm_w_out, v_ssm_w_out)),
            ("attn_w_qkv", parts[2], (attn_w_qkv, m_attn_w_qkv, v_attn_w_qkv)),
            ("attn_w_o", parts[3], (attn_w_o, m_attn_w_o, v_attn_w_o))):
        w2, m2, v2, shp = own(w, mm, vv)
        res = _reduce_adamw(part, w2, m2, v2, f"adamw_{key}")
        big[key] = [r.reshape(shp) for r in res]
    for key, pa, pb, (w, mm, vv) in (("mlp_w1", parts[4], parts[5], (mlp_w1, m_mlp_w1, v_mlp_w1)),
                                     ("mlp_w2", parts[6], parts[7], (mlp_w2, m_mlp_w2, v_mlp_w2))):
        res = [_reduce_adamw(p, w[l], mm[l], vv[l], f"adamw_{key}_{l}") for l, p in enumerate((pa, pb))]
        big[key] = [jnp.stack([res[0][i], res[1][i]], axis=0) for i in range(4)]

    d_norm_mix = jnp.concatenate([dg_mix0, dg_mix1], axis=0)
    d_norm_mlp = jnp.concatenate([dg_mlp0, dg_mlp1], axis=0)
    d_conv_b = conv_sums[4:5]
    d_conv_w = conv_sums[0:4]
    head = hsums[:, :, :8]
    d_dt_bias, d_a_log, d_d = (head[:, k, :].reshape(1, SSM_HEADS) for k in range(3))
    d_ssm_norm = csums[0:1]
    small = [d_norm_mix, d_norm_mlp, d_conv_b, d_dt_bias, d_a_log, d_d, d_ssm_norm, d_final, d_conv_w, loss_acc[0:1, 0:1]]
    shapes = [a.shape for a in small]
    summed = _unpack(_all_reduce_small(_pack(small), "reduce_small"), shapes)
    g_conv_w_full = summed[8]
    loss = summed[9].reshape(())
    g_conv_w = lax.dynamic_slice(g_conv_w_full, (0, me * (CONV_DIM // NDEV)), (CONV_WIDTH, CONV_DIM // NDEV))

    small_names = ["norm_mix", "norm_mlp", "ssm_conv_b", "ssm_dt_bias", "ssm_a_log", "ssm_d", "ssm_norm_w", "final_norm"]
    small_w = [norm_mix, norm_mlp, ssm_conv_b, ssm_dt_bias, ssm_a_log, ssm_d, ssm_norm_w, final_norm, ssm_conv_w]
    small_m = [m_norm_mix, m_norm_mlp, m_ssm_conv_b, m_ssm_dt_bias, m_ssm_a_log, m_ssm_d, m_ssm_norm_w, m_final_norm, m_ssm_conv_w]
    small_v = [v_norm_mix, v_norm_mlp, v_ssm_conv_b, v_ssm_dt_bias, v_ssm_a_log, v_ssm_d, v_ssm_norm_w, v_final_norm, v_ssm_conv_w]
    small_g = [summed[i].reshape(small_w[i].shape) for i in range(8)] + [g_conv_w.reshape(ssm_conv_w.shape)]
    wshapes = [a.shape for a in small_w]
    sd, sm, sv = _adamw(_pack(small_g), _pack(small_w), _pack(small_m), _pack(small_v), "adamw_small")
    sd, sm, sv = _unpack(sd, wshapes), _unpack(sm, wshapes), _unpack(sv, wshapes)
    res = {n: (small_g[i], sd[i], sm[i], sv[i]) for i, n in enumerate(small_names + ["ssm_conv_w"])}
    for n in big:
        res[n] = tuple(big[n])

    order = ["norm_mix", "norm_mlp", "ssm_w_in", "ssm_conv_w", "ssm_conv_b", "ssm_dt_bias", "ssm_a_log", "ssm_d",
             "ssm_norm_w", "ssm_w_out", "attn_w_qkv", "attn_w_o", "mlp_w1", "mlp_w2", "final_norm"]
    outs = [loss, dx0.reshape(x.shape)]
    for kind in range(4):
        outs += [res[n][kind] for n in order]
    return tuple(outs)
```

```python
import math

import jax
import jax.numpy as jnp
from jax import lax
from jax.experimental import pallas as pl
from jax.experimental.pallas import tpu as pltpu

F32, BF16 = jnp.float32, jnp.bfloat16
SDS = jax.ShapeDtypeStruct
MESH = pl.DeviceIdType.MESH
HIGHEST = lax.Precision.HIGHEST

NDEV = 8
D_MODEL = 2048
D_INNER = 4096
SSM_HEADS = 64
SSM_HEAD_DIM = 64
SSM_GROUPS = 8
SSM_STATE = 128
CHUNK = 128
CONV_DIM = 6144
CONV_WIDTH = 4
GROUP_W = D_INNER // SSM_GROUPS
ATTN_GROUPS = ((128, 1), (512, 4), (2048, 16))
ATTN_W = 1024
QKV_DIM = 9216
D_FF = 8192
EPS = 1e-5
ADAM_LR, ADAM_B1, ADAM_B2, ADAM_EPS, ADAM_WD, ADAM_STEP = 0.001, 0.9, 0.999, 1e-08, 0.01, 10

VMEM_LIMIT = 48 * 1024 * 1024


def _params(sem):
    return pltpu.CompilerParams(dimension_semantics=sem, vmem_limit_bytes=VMEM_LIMIT)


def _sigmoid(v):
    return 1.0 / (1.0 + jnp.exp(-v))


def _softplus(v):
    return jnp.maximum(v, 0.0) + jnp.log1p(jnp.exp(-jnp.abs(v)))


def _wspec(layout, kw, nw, tr, tc, sel):
    if layout == "plain":
        return pl.BlockSpec((tr, tc), lambda *g: sel(*g))
    per = (nw // NDEV) // tc
    return pl.BlockSpec((None, tr, tc), lambda *g: (sel(*g)[1] // per, sel(*g)[0], sel(*g)[1] % per))


def _wshape(layout, kw, nw):
    return (kw, nw) if layout == "plain" else (NDEV, kw, nw // NDEV)


def _mm_call(name, grid, in_specs, out_specs, out_shape, dims, n_extra, epi, tm, tn, carry):
    nk = grid[2]
    kind, moved = carry if carry else (None, ())
    nc = len(moved)
    n_out = len(out_shape)

    def body(*refs):
        a_ref, b_ref = refs[0], refs[1]
        extra = refs[2:2 + n_extra]
        c_in = refs[2 + n_extra:2 + n_extra + nc]
        outs = refs[2 + n_extra + nc:2 + n_extra + nc + n_out]
        c_out = refs[2 + n_extra + nc + n_out:2 + n_extra + 2 * nc + n_out]
        acc = refs[2 + n_extra + 2 * nc + n_out]
        sems = refs[3 + n_extra + 2 * nc + n_out:]
        i, j, k = pl.program_id(0), pl.program_id(1), pl.program_id(2)
        if nc:
            @pl.when((i == 0) & (j == 0) & (k == 0))
            def _():
                _exchange_start(kind, c_in, c_out, *sems)

        d = lax.dot_general(a_ref[...].astype(BF16), b_ref[...].astype(BF16), (dims, ((), ())), preferred_element_type=F32)

        def finish(total):
            vals = epi(total, *[e[...] for e in extra])
            for o, v in zip(outs, vals):
                o[...] = v.astype(o.dtype)

        if nk == 1:
            finish(d)
        else:
            @pl.when(k == 0)
            def _():
                acc[...] = d

            @pl.when(jnp.logical_and(k > 0, k < nk - 1))
            def _():
                acc[...] += d

            @pl.when(k == nk - 1)
            def _():
                finish(acc[...] + d)

        if nc:
            @pl.when((i == grid[0] - 1) & (j == grid[1] - 1) & (k == nk - 1))
            def _():
                _exchange_wait(kind, c_in, c_out, *sems)

    hbm = pl.BlockSpec(memory_space=pl.ANY)
    scratch = [pltpu.VMEM((tm, tn), F32)] + (_exchange_sems(nc) if nc else [])
    c_shape = [SDS(((NDEV,) + t.shape) if kind == "gather" else t.shape, t.dtype) for t in moved]
    sem = ("arbitrary",) * 3 if nc else ("parallel", "parallel", "arbitrary")
    return pl.pallas_call(
        body, grid=grid, in_specs=in_specs + [hbm] * nc, out_specs=out_specs + [hbm] * nc,
        out_shape=out_shape + c_shape, scratch_shapes=scratch, name=name, compiler_params=_params(sem))


def _mm_fwd(a, w, layout, kw, nw, *, epi, outs, extras=(), tm=1024, tn=1024, tk=2048, carry=None, name):
    m = a.shape[0]
    tm, tn, tk = min(tm, m), min(tn, nw), min(tk, kw)
    grid = (m // tm, nw // tn, kw // tk)
    o_spec = pl.BlockSpec((tm, tn), lambda i, j, k: (i, j))
    in_specs = [pl.BlockSpec((tm, tk), lambda i, j, k: (i, k)), _wspec(layout, kw, nw, tk, tn, lambda i, j, k: (k, j))]
    in_specs += [o_spec] * len(extras)
    call = _mm_call(name, grid, in_specs, [o_spec] * len(outs), [SDS((m, nw), dt) for dt in outs], ((1,), (0,)),
                    len(extras), epi, tm, tn, carry)
    return call(a, w, *extras, *(carry[1] if carry else ()))


def _mm_dx(g, w, layout, kw, nw, *, epi, outs, extras=(), tm=1024, tn=1024, tk=2048, carry=None, name):
    m = g.shape[0]
    tm, tn, tk = min(tm, m), min(tn, kw), min(tk, nw if layout == "plain" else nw // NDEV)
    grid = (m // tm, kw // tn, nw // tk)
    o_spec = pl.BlockSpec((tm, tn), lambda i, j, k: (i, j))
    in_specs = [pl.BlockSpec((tm, tk), lambda i, j, k: (i, k)), _wspec(layout, kw, nw, tn, tk, lambda i, j, k: (j, k))]
    in_specs += [o_spec] * len(extras)
    call = _mm_call(name, grid, in_specs, [o_spec] * len(outs), [SDS((m, kw), dt) for dt in outs], ((1,), (1,)),
                    len(extras), epi, tm, tn, carry)
    return call(g, w, *extras, *(carry[1] if carry else ()))


def _mm_dw(a, g, layout, *, tm=1024, tn=1024, tk=2048, carry=None, name):
    m, kw = a.shape
    nw = g.shape[1]
    tm, tn, tk = min(tm, kw), min(tn, nw if layout == "plain" else nw // NDEV), min(tk, m)
    grid = (kw // tm, nw // tn, m // tk)
    in_specs = [pl.BlockSpec((tk, tm), lambda i, j, k: (k, i)), pl.BlockSpec((tk, tn), lambda i, j, k: (k, j))]
    o_spec = _wspec(layout, kw, nw, tm, tn, lambda i, j, k: (i, j))
    call = _mm_call(name, grid, in_specs, [o_spec], [SDS(_wshape(layout, kw, nw), BF16)], ((0,), (0,)),
                    0, lambda t: (t,), tm, tn, carry)
    return call(a, g, *(carry[1] if carry else ()))


ROW_TILE = 256


def _rmsnorm_fwd(x, g, name):
    t, d = x.shape

    def body(x_ref, g_ref, h_ref):
        xv = x_ref[...]
        r = lax.rsqrt(jnp.mean(xv * xv, axis=-1, keepdims=True) + EPS)
        h_ref[...] = (xv * r * g_ref[...]).astype(BF16)

    row = pl.BlockSpec((ROW_TILE, d), lambda i: (i, 0))
    vec = pl.BlockSpec((1, d), lambda i: (0, 0))
    return pl.pallas_call(body, grid=(t // ROW_TILE,), in_specs=[row, vec], out_specs=row, out_shape=SDS((t, d), BF16),
                          name=name, compiler_params=_params(("parallel",)))(x, g)


def _rmsnorm_bwd(x, g, dh, dres, name):
    t, d = x.shape

    def body(x_ref, g_ref, dh_ref, dres_ref, dx_ref, dxb_ref, dg_ref):
        xv = x_ref[...]
        r = lax.rsqrt(jnp.mean(xv * xv, axis=-1, keepdims=True) + EPS)
        xh = xv * r
        dhv = dh_ref[...]
        gd = dhv * g_ref[...]
        dx = dres_ref[...] + r * (gd - xh * jnp.mean(gd * xh, axis=-1, keepdims=True))
        dx_ref[...] = dx
        dxb_ref[...] = dx.astype(BF16)
        part = jnp.sum(dhv * xh, axis=0, keepdims=True)

        @pl.when(pl.program_id(0) == 0)
        def _():
            dg_ref[...] = part

        @pl.when(pl.program_id(0) > 0)
        def _():
            dg_ref[...] += part

    row = pl.BlockSpec((ROW_TILE, d), lambda i: (i, 0))
    vec = pl.BlockSpec((1, d), lambda i: (0, 0))
    return pl.pallas_call(body, grid=(t // ROW_TILE,), in_specs=[row, vec, row, row], out_specs=[row, row, vec],
                          out_shape=[SDS((t, d), F32), SDS((t, d), BF16), SDS((1, d), F32)], name=name,
                          compiler_params=_params(("arbitrary",)))(x, g, dh, dres)


def _loss_head(x, tgt, g, name):
    t, d = x.shape

    def body(x_ref, t_ref, g_ref, dx_ref, dxb_ref, loss_ref, dg_ref):
        xv = x_ref[...]
        r = lax.rsqrt(jnp.mean(xv * xv, axis=-1, keepdims=True) + EPS)
        xh = xv * r
        gv = g_ref[...]
        err = xh * gv - t_ref[...]
        part_loss = 0.5 * jnp.sum(jnp.mean(err * err, axis=-1, keepdims=True), axis=0, keepdims=True)
        dy = err * (1.0 / d)
        gd = dy * gv
        dx = r * (gd - xh * jnp.mean(gd * xh, axis=-1, keepdims=True))
        dx_ref[...] = dx
        dxb_ref[...] = dx.astype(BF16)
        part_g = jnp.sum(dy * xh, axis=0, keepdims=True)
        part_l = jnp.broadcast_to(part_loss, (8, 128))

        @pl.when(pl.program_id(0) == 0)
        def _():
            dg_ref[...] = part_g
            loss_ref[...] = part_l

        @pl.when(pl.program_id(0) > 0)
        def _():
            dg_ref[...] += part_g
            loss_ref[...] += part_l

    row = pl.BlockSpec((ROW_TILE, d), lambda i: (i, 0))
    vec = pl.BlockSpec((1, d), lambda i: (0, 0))
    sc = pl.BlockSpec((8, 128), lambda i: (0, 0))
    return pl.pallas_call(body, grid=(t // ROW_TILE,), in_specs=[row, row, vec], out_specs=[row, row, sc, vec],
                          out_shape=[SDS((t, d), F32), SDS((t, d), BF16), SDS((8, 128), F32), SDS((1, d), F32)], name=name,
                          compiler_params=_params(("arbitrary",)))(x, tgt, g)


CONV_ROWS = 256
CONV_COLS = 2048


def _shift_down(cur, prev8, k):
    sh = pltpu.roll(cur, k, axis=0)
    ph = pltpu.roll(prev8, k, axis=0)
    rid = lax.broadcasted_iota(jnp.int32, ph.shape, 0)
    head = jnp.where(rid < k, ph, sh[0:8])
    return jnp.concatenate([head, sh[8:]], axis=0)


def _shift_up(cur, next8, k):
    n = cur.shape[0]
    sh = pltpu.roll(cur, n - k, axis=0)
    nh = pltpu.roll(next8, 8 - k, axis=0)
    rid = lax.broadcasted_iota(jnp.int32, nh.shape, 0)
    tail = jnp.where(rid >= 8 - k, nh, sh[n - 8:])
    return jnp.concatenate([sh[:n - 8], tail], axis=0)


def _conv_pre(cur, prev8, w, b):
    acc = w[3:4, :] * cur + b
    for k in range(1, CONV_WIDTH):
        acc = acc + w[3 - k:4 - k, :] * _shift_down(cur, prev8, k)
    return acc


def _conv_fwd(u, w, b, name):
    t, c = u.shape
    per = CONV_ROWS // 8

    def body(u_ref, p_ref, w_ref, b_ref, o_ref):
        prev8 = jnp.where(pl.program_id(1) == 0, 0.0, p_ref[...])
        pre = _conv_pre(u_ref[...], prev8, w_ref[...], b_ref[...])
        o_ref[...] = pre * _sigmoid(pre)

    cur = pl.BlockSpec((CONV_ROWS, CONV_COLS), lambda j, i: (i, j))
    prev = pl.BlockSpec((8, CONV_COLS), lambda j, i: (jnp.maximum(i * per - 1, 0), j))
    wsp = pl.BlockSpec((CONV_WIDTH, CONV_COLS), lambda j, i: (0, j))
    bsp = pl.BlockSpec((1, CONV_COLS), lambda j, i: (0, j))
    return pl.pallas_call(body, grid=(c // CONV_COLS, t // CONV_ROWS), in_specs=[cur, prev, wsp, bsp], out_specs=cur,
                          out_shape=SDS((t, c), F32), name=name, compiler_params=_params(("parallel", "parallel")))(u, u, w, b)


def _conv_bwd_pre(u, w, b, dout, name):
    t, c = u.shape
    per = CONV_ROWS // 8

    def body(u_ref, p_ref, w_ref, b_ref, d_ref, dp_ref, dw_ref):
        i = pl.program_id(1)
        cur = u_ref[...]
        prev8 = jnp.where(i == 0, 0.0, p_ref[...])
        pre = _conv_pre(cur, prev8, w_ref[...], b_ref[...])
        s = _sigmoid(pre)
        dpre = d_ref[...] * (s * (1.0 + pre * (1.0 - s)))
        dp_ref[...] = dpre
        rows = [jnp.sum(dpre * _shift_down(cur, prev8, 3 - k), axis=0, keepdims=True) for k in range(3)]
        rows.append(jnp.sum(dpre * cur, axis=0, keepdims=True))
        rows.append(jnp.sum(dpre, axis=0, keepdims=True))
        part = jnp.concatenate(rows + [jnp.zeros((3, cur.shape[1]), F32)], axis=0)

        @pl.when(i == 0)
        def _():
            dw_ref[...] = part

        @pl.when(i > 0)
        def _():
            dw_ref[...] += part

    cur = pl.BlockSpec((CONV_ROWS, CONV_COLS), lambda j, i: (i, j))
    prev = pl.BlockSpec((8, CONV_COLS), lambda j, i: (jnp.maximum(i * per - 1, 0), j))
    wsp = pl.BlockSpec((CONV_WIDTH, CONV_COLS), lambda j, i: (0, j))
    bsp = pl.BlockSpec((1, CONV_COLS), lambda j, i: (0, j))
    acc = pl.BlockSpec((8, CONV_COLS), lambda j, i: (0, j))
    return pl.pallas_call(body, grid=(c // CONV_COLS, t // CONV_ROWS), in_specs=[cur, prev, wsp, bsp, cur],
                          out_specs=[cur, acc], out_shape=[SDS((t, c), F32), SDS((8, c), F32)], name=name,
                          compiler_params=_params(("parallel", "arbitrary")))(u, u, w, b, dout)


def _conv_bwd_in(dpre, w, name):
    t, c = dpre.shape
    per = CONV_ROWS // 8
    last = t // CONV_ROWS - 1

    def body(d_ref, n_ref, w_ref, o_ref):
        cur = d_ref[...]
        next8 = jnp.where(pl.program_id(1) == last, 0.0, n_ref[...])
        wv = w_ref[...]
        acc = wv[3:4, :] * cur
        for k in range(1, CONV_WIDTH):
            acc = acc + wv[3 - k:4 - k, :] * _shift_up(cur, next8, k)
        o_ref[...] = acc.astype(BF16)

    cur = pl.BlockSpec((CONV_ROWS, CONV_COLS), lambda j, i: (i, j))
    nxt = pl.BlockSpec((8, CONV_COLS), lambda j, i: (jnp.minimum((i + 1) * per, t // 8 - 1), j))
    wsp = pl.BlockSpec((CONV_WIDTH, CONV_COLS), lambda j, i: (0, j))
    return pl.pallas_call(body, grid=(c // CONV_COLS, t // CONV_ROWS), in_specs=[cur, nxt, wsp], out_specs=cur,
                          out_shape=SDS((t, c), BF16), name=name, compiler_params=_params(("parallel", "parallel")))(dpre, dpre, w)


def _pieces(v, n):
    out, rest = [], v
    for _ in range(n):
        p = rest.astype(BF16)
        out.append(p)
        rest = rest - p.astype(F32)
    return out


def _head_expand(v):
    r = lax.broadcasted_iota(jnp.int32, (3 * 128, GROUP_W), 0) % 128
    c = lax.broadcasted_iota(jnp.int32, (3 * 128, GROUP_W), 1)
    return _dot(jnp.concatenate(_pieces(v, 3), axis=1), (c // SSM_HEAD_DIM == r).astype(BF16))


def _head_sum(vs):
    r = lax.broadcasted_iota(jnp.int32, (2 * GROUP_W, 128), 0) % GROUP_W
    c = lax.broadcasted_iota(jnp.int32, (2 * GROUP_W, 128), 1)
    stacked = jnp.concatenate([jnp.concatenate(_pieces(v, 2), axis=1) for v in vs], axis=0)
    out = _dot(stacked, (r // SSM_HEAD_DIM == c).astype(BF16))
    res, at = [], 0
    for v in vs:
        res.append(out[at:at + v.shape[0]])
        at += v.shape[0]
    return res


def _tri_left(tri, v):
    r = _dot(tri.astype(BF16), jnp.concatenate(_pieces(v, 3), axis=1))
    return r[:, 0:128] + r[:, 128:256] + r[:, 256:384]


def _dot(a, b):
    return jnp.dot(a, b, preferred_element_type=F32)


def _dot_nt(a, b):
    return lax.dot_general(a, b, (((1,), (1,)), ((), ())), preferred_element_type=F32)


def _dot_tn(a, b):
    return lax.dot_general(a, b, (((0,), (0,)), ((), ())), preferred_element_type=F32)


def _ssd_common(dtp, dtpt, bias, biasr, alog, alogr):
    li = lax.broadcasted_iota(jnp.int32, (CHUNK, CHUNK), 0)
    si = lax.broadcasted_iota(jnp.int32, (CHUNK, CHUNK), 1)
    lower, upper = (li >= si), (li <= si)
    dt = _softplus(dtp + bias)
    a_neg = -jnp.exp(alog)
    cs = _tri_left(lower, dt * a_neg)
    dtr = _softplus(dtpt + biasr)
    ar = jnp.concatenate([dtr * (-jnp.exp(alogr)), jnp.zeros((8, CHUNK), F32)], axis=0)
    r3 = lax.broadcasted_iota(jnp.int32, (3 * CHUNK, CHUNK), 0) % CHUNK
    c3 = lax.broadcasted_iota(jnp.int32, (3 * CHUNK, CHUNK), 1)
    csr = _dot(jnp.concatenate(_pieces(ar, 3), axis=1), (r3 <= c3).astype(BF16))[0:8]
    return lower, upper, dt, a_neg, cs, csr


def _head_masked_rows(v):
    hl = lax.broadcasted_iota(jnp.int32, v.shape, 1) // SSM_HEAD_DIM
    return jnp.concatenate([jnp.where(hl == j, v, jnp.zeros_like(v)) for j in range(8)], axis=0)


def _ssd_fwd(xbc, z, dtp_g, dtp_t, bias_g, bias_r, alog_g, alog_r, d_exp, norm_w, name):
    t = xbc.shape[0]
    nc = t // CHUNK

    def body(xs_ref, b_ref, c_ref, z_ref, dtp_ref, dtpt_ref, bias_ref, biasr_ref, alog_ref, alogr_ref, dexp_ref, nw_ref,
             y_ref, yn_ref, st_ref, state):
        c = pl.program_id(1)

        @pl.when(c == 0)
        def _():
            state[...] = jnp.zeros_like(state)

        lower, _, dt, a_neg, cs, csr = _ssd_common(dtp_ref[...], dtpt_ref[...], bias_ref[...], biasr_ref[...],
                                                alog_ref[...], alogr_ref[...])
        cs_e = _head_expand(cs)
        dt_e = _head_expand(dt)
        xs = xs_ref[...]
        xdt = xs * dt_e
        bm = b_ref[...]
        cm = c_ref[...]
        bmb, cmb = bm.astype(BF16), cm.astype(BF16)
        cb = _dot_nt(cmb, bmb)
        ms = []
        for j in range(8):
            dlt = cs_e[:, SSM_HEAD_DIM * j:SSM_HEAD_DIM * j + 1] - csr[j:j + 1, :]
            ms.append((cb * jnp.exp(jnp.where(lower, dlt, -jnp.inf))).astype(BF16))
        y = _dot(jnp.concatenate(ms, axis=1), _head_masked_rows(xdt.astype(BF16)))
        st_in = state[...]
        st_ref[...] = st_in
        y = y + jnp.exp(cs_e) * _dot(cmb, st_in.astype(BF16))
        cs_last = cs_e[CHUNK - 1:CHUNK, :]
        xdtd = (xdt * jnp.exp(cs_last - cs_e)).astype(BF16)
        state[...] = jnp.exp(cs_last) * st_in + _dot(bm.T.astype(BF16), xdtd)
        y_ref[...] = y
        zz = z_ref[...]
        y2 = (y + dexp_ref[...] * xs) * (zz * _sigmoid(zz))
        r = lax.rsqrt(jnp.mean(y2 * y2, axis=-1, keepdims=True) + EPS)
        yn_ref[...] = (y2 * r * nw_ref[...]).astype(BF16)

    gw = pl.BlockSpec((CHUNK, GROUP_W), lambda g, c: (c, g))
    in_specs = [
        gw,
        pl.BlockSpec((CHUNK, SSM_STATE), lambda g, c: (c, D_INNER // SSM_STATE + g)),
        pl.BlockSpec((CHUNK, SSM_STATE), lambda g, c: (c, D_INNER // SSM_STATE + SSM_GROUPS + g)),
        gw,
        pl.BlockSpec((None, CHUNK, 128), lambda g, c: (g, c, 0)),
        pl.BlockSpec((8, CHUNK), lambda g, c: (g, c)),
        pl.BlockSpec((None, 1, 128), lambda g, c: (g, 0, 0)),
        pl.BlockSpec((8, 128), lambda g, c: (g, 0)),
        pl.BlockSpec((None, 1, 128), lambda g, c: (g, 0, 0)),
        pl.BlockSpec((8, 128), lambda g, c: (g, 0)),
        pl.BlockSpec((1, GROUP_W), lambda g, c: (0, g)),
        pl.BlockSpec((1, GROUP_W), lambda g, c: (0, g)),
    ]
    out_specs = [gw, gw, pl.BlockSpec((None, SSM_STATE, GROUP_W), lambda g, c: (c, 0, g))]
    out_shape = [SDS((t, D_INNER), F32), SDS((t, D_INNER), BF16), SDS((nc, SSM_STATE, D_INNER), F32)]
    return pl.pallas_call(body, grid=(SSM_GROUPS, nc), in_specs=in_specs, out_specs=out_specs, out_shape=out_shape,
                          scratch_shapes=[pltpu.VMEM((SSM_STATE, GROUP_W), F32)], name=name,
                          compiler_params=_params(("parallel", "arbitrary")))(
        xbc, xbc, xbc, z, dtp_g, dtp_t, bias_g, bias_r, alog_g, alog_r, d_exp, norm_w)


def _ssd_bwd(dyn, y, xbc, z, states, dtp_g, dtp_t, bias_g, bias_r, alog_g, alog_r, d_exp, norm_w, name):
    t = xbc.shape[0]
    nc = t // CHUNK

    def body(dyn_ref, y_ref, xs_ref, b_ref, c_ref, z_ref, st_ref, dtp_ref, dtpt_ref, bias_ref, biasr_ref, alog_ref,
             alogr_ref, dexp_ref, nw_ref, dxs_ref, db_ref, dc_ref, dz_ref, ddt_ref, hsum_ref, csum_ref, dstate):
        step = pl.program_id(1)

        @pl.when(step == 0)
        def _():
            dstate[...] = jnp.zeros_like(dstate)

        dtp = dtp_ref[...]
        bias = bias_ref[...]
        lower, upper, dt, a_neg, cs, csr = _ssd_common(dtp, dtpt_ref[...], bias, biasr_ref[...], alog_ref[...],
                                                       alogr_ref[...])
        cs_e = _head_expand(cs)
        dt_e = _head_expand(dt)
        xs = xs_ref[...]
        xdt = xs * dt_e
        bm = b_ref[...]
        cm = c_ref[...]
        bmb, cmb = bm.astype(BF16), cm.astype(BF16)
        y = y_ref[...]
        dexp = dexp_ref[...]
        nw = nw_ref[...]

        zz = z_ref[...]
        sg = _sigmoid(zz)
        gate = zz * sg
        ytot = y + dexp * xs
        y2 = ytot * gate
        r = lax.rsqrt(jnp.mean(y2 * y2, axis=-1, keepdims=True) + EPS)
        dynv = dyn_ref[...]
        xh = y2 * r
        gn = dynv * nw
        dy2 = r * (gn - xh * jnp.mean(gn * xh, axis=-1, keepdims=True))
        dy = dy2 * gate
        dz_ref[...] = (dy2 * ytot * (sg * (1.0 + zz * (1.0 - sg)))).astype(BF16)
        csum_part = jnp.sum(dynv * xh, axis=0, keepdims=True)

        cb = _dot_nt(cmb, bmb)
        dyb = dy.astype(BF16)
        xdtb = xdt.astype(BF16)
        dym = _head_masked_rows(dyb)
        dm = _dot_nt(dym, xdtb)
        dmt = _dot_nt(_head_masked_rows(xdtb), dyb)
        lane = lax.broadcasted_iota(jnp.int32, (CHUNK, 128), 1)
        mts = []
        dcb = jnp.zeros((CHUNK, CHUNK), F32)
        dcs = jnp.zeros((CHUNK, 128), F32)
        for j in range(8):
            dlt = cs_e[:, SSM_HEAD_DIM * j:SSM_HEAD_DIM * j + 1] - csr[j:j + 1, :]
            lj = jnp.exp(jnp.where(lower, dlt, -jnp.inf))
            mj = cb * lj
            mjt = mj.T
            mts.append(mjt.astype(BF16))
            dmj = dm[CHUNK * j:CHUNK * (j + 1)]
            dcb = dcb + dmj * lj
            rows = jnp.sum(dmj * mj, axis=1, keepdims=True)
            cols = jnp.sum(dmt[CHUNK * j:CHUNK * (j + 1)] * mjt, axis=1, keepdims=True)
            dcs = dcs + jnp.where(lane == j, rows - cols, 0.0)
        dxdt = _dot(jnp.concatenate(mts, axis=1), dym)
        dst_out = dstate[...]
        dst_outb = dst_out.astype(BF16)
        st_in = st_ref[...]
        st_inb = st_in.astype(BF16)
        cs_last = cs_e[CHUNK - 1:CHUNK, :]
        decay = jnp.exp(cs_last - cs_e)
        e_last = jnp.exp(cs_last)
        gpart = decay * _dot(bmb, dst_outb)
        dxdt = dxdt + gpart
        dyw = (jnp.exp(cs_e) * dy).astype(BF16)
        dcbb = dcb.astype(BF16)
        dc_ref[...] = _dot_nt(dyw, st_inb) + _dot(dcbb, bmb)
        db_ref[...] = _dot_nt((xdt * decay).astype(BF16), dst_outb) + _dot(dcb.T.astype(BF16), cmb)
        dstate[...] = e_last * dst_out + _dot(cm.T.astype(BF16), dyw)
        y_off = jnp.exp(cs_e) * _dot(cmb, st_inb)
        xg = xdt * gpart
        vec = jnp.concatenate([jnp.sum(dy * xs, axis=0, keepdims=True),
                               jnp.sum(xg + dst_out * e_last * st_in, axis=0, keepdims=True),
                               jnp.zeros((14, GROUP_W), F32)], axis=0)
        s_cs, s_dt, s_vec = _head_sum([dy * y_off - xg, dxdt * xs, vec])
        d_skip = s_vec[0:1]
        ri = lax.broadcasted_iota(jnp.int32, (CHUNK, 128), 0)
        dcs = dcs + s_cs + jnp.where(ri == CHUNK - 1, s_vec[1:2], 0.0)
        da = _tri_left(upper, dcs)
        ddt = da * a_neg + s_dt
        dxs_ref[...] = dxdt * dt_e + dy * dexp
        ddtp = ddt * _sigmoid(dtp + bias)
        ddt_ref[...] = ddtp
        d_alog = jnp.sum(da * dt, axis=0, keepdims=True) * a_neg
        hpart = jnp.concatenate([jnp.sum(ddtp, axis=0, keepdims=True), d_alog, d_skip, jnp.zeros((5, 128), F32)], axis=0)
        cpart = jnp.concatenate([csum_part, jnp.zeros((7, GROUP_W), F32)], axis=0)

        @pl.when(step == 0)
        def _():
            hsum_ref[...] = hpart
            csum_ref[...] = cpart

        @pl.when(step > 0)
        def _():
            hsum_ref[...] += hpart
            csum_ref[...] += cpart

    rc = lambda c: nc - 1 - c
    gw = pl.BlockSpec((CHUNK, GROUP_W), lambda g, c: (rc(c), g))
    bsp = pl.BlockSpec((CHUNK, SSM_STATE), lambda g, c: (rc(c), D_INNER // SSM_STATE + g))
    csp = pl.BlockSpec((CHUNK, SSM_STATE), lambda g, c: (rc(c), D_INNER // SSM_STATE + SSM_GROUPS + g))
    in_specs = [
        gw, gw, gw, bsp, csp, gw,
        pl.BlockSpec((None, SSM_STATE, GROUP_W), lambda g, c: (rc(c), 0, g)),
        pl.BlockSpec((None, CHUNK, 128), lambda g, c: (g, rc(c), 0)),
        pl.BlockSpec((8, CHUNK), lambda g, c: (g, rc(c))),
        pl.BlockSpec((None, 1, 128), lambda g, c: (g, 0, 0)),
        pl.BlockSpec((8, 128), lambda g, c: (g, 0)),
        pl.BlockSpec((None, 1, 128), lambda g, c: (g, 0, 0)),
        pl.BlockSpec((8, 128), lambda g, c: (g, 0)),
        pl.BlockSpec((1, GROUP_W), lambda g, c: (0, g)),
        pl.BlockSpec((1, GROUP_W), lambda g, c: (0, g)),
    ]
    nsp = pl.BlockSpec((CHUNK, SSM_STATE), lambda g, c: (rc(c), g))
    out_specs = [gw, nsp, nsp, gw,
                 pl.BlockSpec((None, CHUNK, 128), lambda g, c: (g, rc(c), 0)),
                 pl.BlockSpec((None, 8, 128), lambda g, c: (g, 0, 0)),
                 pl.BlockSpec((8, GROUP_W), lambda g, c: (0, g))]
    gn = SSM_GROUPS * SSM_STATE
    out_shape = [SDS((t, D_INNER), F32), SDS((t, gn), F32), SDS((t, gn), F32), SDS((t, D_INNER), BF16),
                 SDS((SSM_GROUPS, t, 128), F32), SDS((SSM_GROUPS, 8, 128), F32), SDS((8, D_INNER), F32)]
    return pl.pallas_call(body, grid=(SSM_GROUPS, nc), in_specs=in_specs, out_specs=out_specs, out_shape=out_shape,
                          scratch_shapes=[pltpu.VMEM((SSM_STATE, GROUP_W), F32)], name=name,
                          compiler_params=_params(("parallel", "arbitrary")))(
        dyn, y, xbc, xbc, xbc, z, states, dtp_g, dtp_t, bias_g, bias_r, alog_g, alog_r, d_exp, norm_w)


BLK = 128
HEAD_PAIRS = ATTN_W // 128
ATTN_SCALE = 0.125
SPAN_BLOCKS = {1: 8, 4: 2, 16: 1}
QKV_LANE_BLOCKS = ATTN_W * len(ATTN_GROUPS) // 128


def _slope_table(group):
    n = len(ATTN_GROUPS) * 16
    tbl = [[2.0 ** (-8.0 * (16 * group + 2 * p + s + 1) / n) if s < 2 else 0.0 for s in range(128)] for p in range(HEAD_PAIRS)]
    return jnp.asarray(tbl, F32)


def _lane_lo(rows):
    return lax.broadcasted_iota(jnp.int32, (rows, 128), 1) < 64


def _rows(start, dil):
    return pl.ds(start, BLK, stride=dil) if dil > 1 else pl.ds(start, BLK)


def _each_residue(dil, fn):
    if dil == 1:
        fn(0)
    else:
        def step(r, carry):
            fn(r)
            return carry
        lax.fori_loop(0, dil, step, 0)


def _attn_specs(group, t):
    _, dil = ATTN_GROUPS[group]
    nblk = SPAN_BLOCKS[dil]
    span, edge = BLK * dil * nblk, BLK * dil
    per = span // edge

    def cur(which):
        off = 0 if which is None else which * QKV_LANE_BLOCKS + group * HEAD_PAIRS
        return pl.BlockSpec((span, 128), lambda s, p, *_: (s, off + p))

    def before(which):
        off = which * QKV_LANE_BLOCKS + group * HEAD_PAIRS
        return pl.BlockSpec((edge, 128), lambda s, p, *_: (jnp.maximum(s * per - 1, 0), off + p))

    def after(which):
        off = 0 if which is None else which * QKV_LANE_BLOCKS + group * HEAD_PAIRS
        return pl.BlockSpec((edge, 128), lambda s, p, *_: (jnp.minimum((s + 1) * per, t // edge - 1), off + p))

    slopes = pl.BlockSpec((HEAD_PAIRS, 128), lambda s, p, *_: (0, 0))
    return dil, nblk, span, cur, before, after, slopes


def _attn_fwd(qkv, group, name):
    t = qkv.shape[0]
    dil, nblk, span, cur, before, after, slopes = _attn_specs(group, t)

    def body(q_ref, k_ref, v_ref, kp_ref, vp_ref, sl_ref, o_ref, l_ref):
        first_span = pl.program_id(0) == 0
        sl = sl_ref[pl.ds(pl.program_id(1), 1), :]
        qi = lax.broadcasted_iota(jnp.int32, (BLK, 2 * BLK), 0)
        kj = lax.broadcasted_iota(jnp.int32, (BLK, 2 * BLK), 1)
        dist = qi + BLK - kj
        valid = (dist >= 0) & (dist <= BLK)
        valid_first = valid & ((kj >= BLK) | jnp.logical_not(first_span))
        distf = dist.astype(F32) * float(dil)
        lo_q, lo_k = _lane_lo(BLK), _lane_lo(2 * BLK)
        for b in range(nblk):
            def one(r, b=b):
                rows = _rows(b * BLK * dil + r, dil)
                q2 = q_ref[rows, :].astype(BF16)
                if b == 0:
                    prev = _rows(r, dil)
                    kp, vp, ok = kp_ref[prev, :], vp_ref[prev, :], valid_first
                else:
                    prev = _rows((b - 1) * BLK * dil + r, dil)
                    kp, vp, ok = k_ref[prev, :], v_ref[prev, :], valid
                k2 = jnp.concatenate([kp, k_ref[rows, :]], axis=0).astype(BF16)
                v2 = jnp.concatenate([vp, v_ref[rows, :]], axis=0).astype(BF16)
                o2 = jnp.zeros((BLK, 128), F32)
                l2 = jnp.zeros((BLK, 128), F32)
                for s01 in range(2):
                    mq = lo_q if s01 == 0 else ~lo_q
                    mk = lo_k if s01 == 0 else ~lo_k
                    s = _dot_nt(jnp.where(mq, q2, jnp.zeros_like(q2)), k2) * ATTN_SCALE - sl[:, s01:s01 + 1] * distf
                    s = jnp.where(ok, s, -jnp.inf)
                    mx = jnp.max(s, axis=-1, keepdims=True)
                    p = jnp.exp(s - mx)
                    den = jnp.sum(p, axis=-1, keepdims=True)
                    o2 = o2 + _dot(p.astype(BF16), jnp.where(mk, v2, jnp.zeros_like(v2))) / den
                    l2 = jnp.where(mq, mx + jnp.log(den), l2)
                o_ref[rows, :] = o2
                l_ref[rows, :] = l2
            _each_residue(dil, one)

    in_specs = [cur(0), cur(1), cur(2), before(1), before(2), slopes]
    return pl.pallas_call(body, grid=(t // span, HEAD_PAIRS), in_specs=in_specs, out_specs=[cur(None), cur(None)],
                          out_shape=[SDS((t, ATTN_W), F32), SDS((t, ATTN_W), F32)], name=name,
                          compiler_params=_params(("parallel", "parallel")))(qkv, qkv, qkv, qkv, qkv, _slope_table(group))


def _attn_bwd(qkv, do, lse, dvec, dqkv, group, name):
    t = qkv.shape[0]
    dil, nblk, span, cur, before, after, slopes = _attn_specs(group, t)
    nspan = t // span

    def body(q_ref, k_ref, v_ref, kp_ref, vp_ref, qn_ref, do_ref, l_ref, d_ref, don_ref, ln_ref, dn_ref, sl_ref, *rest):
        out_ref, dk_s, dv_s = rest[-3:]
        span_id, pair_id, which = pl.program_id(0), pl.program_id(1), pl.program_id(2)

        @pl.when(which == 0)
        def _():
            first_span = span_id == 0
            last_span = span_id == nspan - 1
            sl = sl_ref[pl.ds(pair_id, 1), :]
            qi = lax.broadcasted_iota(jnp.int32, (BLK, 2 * BLK), 0)
            kj = lax.broadcasted_iota(jnp.int32, (BLK, 2 * BLK), 1)
            dist = qi + BLK - kj
            valid = (dist >= 0) & (dist <= BLK)
            valid_first = valid & ((kj >= BLK) | jnp.logical_not(first_span))
            distf = dist.astype(F32) * float(dil)
            valid_next = (kj[:, :BLK] >= qi[:, :BLK]) & jnp.logical_not(last_span)
            distf_next = distf[:, :BLK]
            lo_q, lo_k = _lane_lo(BLK), _lane_lo(2 * BLK)
            for b in range(nblk):
                def one(r, b=b):
                    rows = _rows(b * BLK * dil + r, dil)
                    q2 = q_ref[rows, :].astype(BF16)
                    kc, vc = k_ref[rows, :].astype(BF16), v_ref[rows, :].astype(BF16)
                    if b == 0:
                        prev = _rows(r, dil)
                        kp, vp, ok = kp_ref[prev, :], vp_ref[prev, :], valid_first
                    else:
                        prev = _rows((b - 1) * BLK * dil + r, dil)
                        kp, vp, ok = k_ref[prev, :], v_ref[prev, :], valid
                    k2 = jnp.concatenate([kp.astype(BF16), kc], axis=0)
                    v2 = jnp.concatenate([vp.astype(BF16), vc], axis=0)
                    do2 = do_ref[rows, :].astype(BF16)
                    l2 = l_ref[rows, :]
                    d2 = d_ref[rows, :]
                    if b == nblk - 1:
                        nxt = _rows(r, dil)
                        qn, don, ln, dn = qn_ref[nxt, :].astype(BF16), don_ref[nxt, :].astype(BF16), ln_ref[nxt, :], dn_ref[nxt, :]
                    dq2 = jnp.zeros((BLK, 128), F32)
                    dk2 = jnp.zeros((2 * BLK, 128), F32)
                    dv2 = jnp.zeros((2 * BLK, 128), F32)
                    dkn = jnp.zeros((BLK, 128), F32)
                    dvn = jnp.zeros((BLK, 128), F32)
                    for s01 in range(2):
                        mq = lo_q if s01 == 0 else ~lo_q
                        mk = lo_k if s01 == 0 else ~lo_k
                        slope = sl[:, s01:s01 + 1]
                        col = slice(64 * s01, 64 * s01 + 1)
                        qm = jnp.where(mq, q2, jnp.zeros_like(q2))
                        dom = jnp.where(mq, do2, jnp.zeros_like(do2))
                        s = _dot_nt(qm, k2) * ATTN_SCALE - slope * distf
                        p = jnp.exp(jnp.where(ok, s - l2[:, col], -jnp.inf))
                        ds = (p * (_dot_nt(dom, v2) - d2[:, col])).astype(BF16)
                        dq2 = dq2 + _dot(ds, jnp.where(mk, k2, jnp.zeros_like(k2))) * ATTN_SCALE
                        dk2 = dk2 + _dot_tn(ds, qm) * ATTN_SCALE
                        dv2 = dv2 + _dot_tn(p.astype(BF16), dom)
                        if b == nblk - 1:
                            qnm = jnp.where(mq, qn, jnp.zeros_like(qn))
                            donm = jnp.where(mq, don, jnp.zeros_like(don))
                            sn = _dot_nt(qnm, kc) * ATTN_SCALE - slope * distf_next
                            pn = jnp.exp(jnp.where(valid_next, sn - ln[:, col], -jnp.inf))
                            dsn = (pn * (_dot_nt(donm, vc) - dn[:, col])).astype(BF16)
                            dkn = dkn + _dot_tn(dsn, qnm) * ATTN_SCALE
                            dvn = dvn + _dot_tn(pn.astype(BF16), donm)
                    out_ref[rows, :] = dq2
                    dk_s[rows, :] = dk2[BLK:] + dkn
                    dv_s[rows, :] = dv2[BLK:] + dvn
                    if b > 0:
                        dk_s[prev, :] = dk_s[prev, :] + dk2[:BLK]
                        dv_s[prev, :] = dv_s[prev, :] + dv2[:BLK]
                _each_residue(dil, one)

        @pl.when(which == 1)
        def _():
            out_ref[...] = dk_s[...]

        @pl.when(which == 2)
        def _():
            out_ref[...] = dv_s[...]

    off = group * HEAD_PAIRS
    out_spec = pl.BlockSpec((span, 128), lambda s, p, w: (s, w * QKV_LANE_BLOCKS + off + p))
    in_specs = [cur(0), cur(1), cur(2), before(1), before(2), after(0), cur(None), cur(None), cur(None),
                after(None), after(None), after(None), slopes]
    args = [qkv, qkv, qkv, qkv, qkv, qkv, do, lse, dvec, do, lse, dvec, _slope_table(group)]
    aliases = {}
    if dqkv is not None:
        in_specs.append(pl.BlockSpec(memory_space=pl.ANY))
        args.append(dqkv)
        aliases = {len(args) - 1: 0}
    return pl.pallas_call(body, grid=(nspan, HEAD_PAIRS, 3), in_specs=in_specs, out_specs=out_spec,
                          out_shape=SDS((t, QKV_DIM), F32), input_output_aliases=aliases,
                          scratch_shapes=[pltpu.VMEM((span, 128), F32), pltpu.VMEM((span, 128), F32)], name=name,
                          compiler_params=_params(("parallel", "parallel", "arbitrary")))(*args)


def _combine_weights(l0, l1, l2):
    mx = jnp.maximum(jnp.maximum(l0, l1), l2)
    e0, e1, e2 = jnp.exp(l0 - mx), jnp.exp(l1 - mx), jnp.exp(l2 - mx)
    den = e0 + e1 + e2
    return e0 / den, e1 / den, e2 / den


def _combine_fwd(os_, ls_, name):
    t = os_[0].shape[0]

    def body(o0, o1, o2, l0, l1, l2, out):
        w0, w1, w2 = _combine_weights(l0[...], l1[...], l2[...])
        out[...] = (w0 * o0[...] + w1 * o1[...] + w2 * o2[...]).astype(BF16)

    row = pl.BlockSpec((ROW_TILE, ATTN_W), lambda i: (i, 0))
    return pl.pallas_call(body, grid=(t // ROW_TILE,), in_specs=[row] * 6, out_specs=row, out_shape=SDS((t, ATTN_W), BF16),
                          name=name, compiler_params=_params(("parallel",)))(*os_, *ls_)


def _combine_bwd(do, os_, ls_, name):
    t = do.shape[0]

    def body(do_ref, o0, o1, o2, l0, l1, l2, g0, g1, g2, d0, d1, d2):
        w0, w1, w2 = _combine_weights(l0[...], l1[...], l2[...])
        dov = do_ref[...]
        prod = dov * (w0 * o0[...] + w1 * o1[...] + w2 * o2[...])
        r = (lax.broadcasted_iota(jnp.int32, (3 * 128, 128), 0) % 128) // 64
        c = lax.broadcasted_iota(jnp.int32, (3 * 128, 128), 1) // 64
        same = (r == c).astype(BF16)
        tbar = jnp.concatenate([_dot(jnp.concatenate(_pieces(prod[:, 128 * k:128 * k + 128], 3), axis=1), same)
                                for k in range(HEAD_PAIRS)], axis=1)
        for w, g, d in ((w0, g0, d0), (w1, g1, d1), (w2, g2, d2)):
            g[...] = w * dov
            d[...] = w * tbar

    row = pl.BlockSpec((ROW_TILE, ATTN_W), lambda i: (i, 0))
    return pl.pallas_call(body, grid=(t // ROW_TILE,), in_specs=[row] * 7, out_specs=[row] * 6,
                          out_shape=[SDS((t, ATTN_W), F32)] * 6, name=name,
                          compiler_params=_params(("parallel",)))(do, *os_, *ls_)


def _peer(k):
    x, y, c = lax.axis_index("x"), lax.axis_index("y"), lax.axis_index("c")
    px = 1 - x if k & 4 else x
    py = 1 - y if k & 2 else y
    pc = 1 - c if k & 1 else c
    return (px, py, pc), 4 * px + 2 * py + pc


def _my_index():
    return 4 * lax.axis_index("x") + 2 * lax.axis_index("y") + lax.axis_index("c")


def _exchange_sems(n):
    return [pltpu.SemaphoreType.DMA((n * (NDEV - 1),)), pltpu.SemaphoreType.DMA((n * (NDEV - 1),)),
            pltpu.SemaphoreType.DMA((n,))]


def _exchange_copies(kind, ins, outs, send, recv, local, arrivals):
    me = _my_index()
    own, sent, arriving = [], [], []
    for i in range(len(ins)):
        own.append(pltpu.make_async_copy(ins[i] if kind == "gather" else ins[i].at[me], outs[i].at[me], local.at[i]))
        for k in range(1, NDEV):
            peer, pidx = _peer(k)
            s = i * (NDEV - 1) + k - 1
            src = ins[i] if kind == "gather" else ins[i].at[pidx]
            for dst, into in ((outs[i].at[me], sent), (outs[i].at[pidx], arriving)):
                if into is sent or arrivals:
                    into.append(pltpu.make_async_remote_copy(src_ref=src, dst_ref=dst, send_sem=send.at[s],
                                                             recv_sem=recv.at[s], device_id=peer, device_id_type=MESH))
    return own, sent, arriving


def _exchange_start(kind, ins, outs, send, recv, local):
    own, sent, _ = _exchange_copies(kind, ins, outs, send, recv, local, arrivals=False)
    for cp in own + sent:
        cp.start()


def _exchange_wait(kind, ins, outs, send, recv, local):
    own, sent, arriving = _exchange_copies(kind, ins, outs, send, recv, local, arrivals=True)
    for cp in sent:
        cp.wait_send()
    for cp in arriving:
        cp.wait_recv()
    for cp in own:
        cp.wait()


def _exchange(kind, tensors, name):
    n = len(tensors)

    def body(*refs):
        _exchange_start(kind, refs[:n], refs[n:2 * n], *refs[2 * n:])
        _exchange_wait(kind, refs[:n], refs[n:2 * n], *refs[2 * n:])

    hbm = pl.BlockSpec(memory_space=pl.ANY)
    shapes = [SDS(((NDEV,) + t.shape) if kind == "gather" else t.shape, t.dtype) for t in tensors]
    return pl.pallas_call(body, in_specs=[hbm] * n, out_specs=[hbm] * n, out_shape=shapes,
                          scratch_shapes=_exchange_sems(n), name=name)(*tensors)


def _all_reduce_small(v, name):
    rows = v.shape[0]

    def body(v_ref, out_ref, land, send, recv):
        me = _my_index()
        land[me] = v_ref[...]
        remote = []
        for k in range(1, NDEV):
            peer, _ = _peer(k)
            cp = pltpu.make_async_remote_copy(src_ref=v_ref, dst_ref=land.at[me], send_sem=send.at[k - 1],
                                              recv_sem=recv.at[k - 1], device_id=peer, device_id_type=MESH)
            cp.start()
            remote.append(cp)
        for cp in remote:
            cp.wait_send()
        for k in range(1, NDEV):
            peer, pidx = _peer(k)
            pltpu.make_async_remote_copy(src_ref=v_ref, dst_ref=land.at[pidx], send_sem=send.at[k - 1],
                                         recv_sem=recv.at[k - 1], device_id=peer, device_id_type=MESH).wait_recv()
        total = land[0]
        for d in range(1, NDEV):
            total = total + land[d]
        out_ref[...] = total

    vm = pl.BlockSpec(memory_space=pltpu.VMEM)
    return pl.pallas_call(
        body, in_specs=[vm], out_specs=vm, out_shape=SDS((rows, 128), F32),
        scratch_shapes=[pltpu.VMEM((NDEV, rows, 128), F32), pltpu.SemaphoreType.DMA((NDEV - 1,)),
                        pltpu.SemaphoreType.DMA((NDEV - 1,))],
        name=name)(v)


def _adamw_math(w, g, m, v):
    m = ADAM_B1 * m + (1.0 - ADAM_B1) * g
    v = ADAM_B2 * v + (1.0 - ADAM_B2) * (g * g)
    m_hat = m / (1.0 - ADAM_B1 ** ADAM_STEP)
    v_hat = v / (1.0 - ADAM_B2 ** ADAM_STEP)
    delta = -ADAM_LR * (m_hat / (jnp.sqrt(v_hat) + ADAM_EPS) + ADAM_WD * w)
    return delta, m, v


def _row_tile(rows, cols):
    tr = rows
    while tr * cols * 4 > (1 << 20) and tr % 16 == 0:
        tr //= 2
    return tr


def _adamw(g, w, m, v, name):
    rows, cols = w.shape
    tr = _row_tile(rows, cols)

    def body(g_ref, w_ref, m_ref, v_ref, d_out, m_out, v_out):
        d, mn, vn = _adamw_math(w_ref[...], g_ref[...], m_ref[...], v_ref[...])
        d_out[...] = d
        m_out[...] = mn
        v_out[...] = vn

    sp = pl.BlockSpec((tr, cols), lambda i: (i, 0))
    return pl.pallas_call(body, grid=(rows // tr,), in_specs=[sp] * 4, out_specs=[sp] * 3,
                          out_shape=[SDS((rows, cols), F32)] * 3, name=name, compiler_params=_params(("parallel",)))(g, w, m, v)


def _reduce_adamw(parts, w, m, v, name):
    rows, cols = w.shape
    tr = _row_tile(rows, cols)

    def body(p_ref, w_ref, m_ref, v_ref, g_out, d_out, m_out, v_out):
        g = p_ref[0].astype(F32)
        for d in range(1, NDEV):
            g = g + p_ref[d].astype(F32)
        g_out[...] = g
        dl, mn, vn = _adamw_math(w_ref[...], g, m_ref[...], v_ref[...])
        d_out[...] = dl
        m_out[...] = mn
        v_out[...] = vn

    sp = pl.BlockSpec((tr, cols), lambda i: (i, 0))
    psp = pl.BlockSpec((NDEV, tr, cols), lambda i: (0, i, 0))
    return pl.pallas_call(body, grid=(rows // tr,), in_specs=[psp, sp, sp, sp], out_specs=[sp] * 4,
                          out_shape=[SDS((rows, cols), F32)] * 4, name=name, compiler_params=_params(("parallel",)))(parts, w, m, v)


def _pack(items):
    rows = []
    for a in items:
        a = a.reshape(-1).astype(F32)
        pad = (-a.shape[0]) % 128
        rows.append(jnp.pad(a, (0, pad)).reshape(-1, 128))
    out = jnp.concatenate(rows, axis=0)
    return jnp.pad(out, ((0, (-out.shape[0]) % 8), (0, 0)))


def _unpack(packed, shapes):
    out, r = [], 0
    for shp in shapes:
        n = math.prod(shp)
        nr = -(-n // 128)
        out.append(packed[r:r + nr].reshape(-1)[:n].reshape(shp))
        r += nr
    return out


def _ident(t):
    return (t,)


def _add(t, res):
    return (t + res,)


def kernel(x, norm_mix, norm_mlp, ssm_w_in, ssm_conv_w, ssm_conv_b, ssm_dt_bias, ssm_a_log, ssm_d, ssm_norm_w, ssm_w_out, attn_w_qkv, attn_w_o, mlp_w1, mlp_w2, final_norm, loss_target, m_norm_mix, m_norm_mlp, m_ssm_w_in, m_ssm_conv_w, m_ssm_conv_b, m_ssm_dt_bias, m_ssm_a_log, m_ssm_d, m_ssm_norm_w, m_ssm_w_out, m_attn_w_qkv, m_attn_w_o, m_mlp_w1, m_mlp_w2, m_final_norm, v_norm_mix, v_norm_mlp, v_ssm_w_in, v_ssm_conv_w, v_ssm_conv_b, v_ssm_dt_bias, v_ssm_a_log, v_ssm_d, v_ssm_norm_w, v_ssm_w_out, v_attn_w_qkv, v_attn_w_o, v_mlp_w1, v_mlp_w2, v_final_norm):
    t = x.shape[1]
    x0 = x.reshape(t, D_MODEL)
    tgt = loss_target.reshape(t, D_MODEL)
    me = _my_index()
    in_dim = D_INNER + CONV_DIM + SSM_HEADS
    in_shard = in_dim // NDEV
    zx_dim = D_INNER + CONV_DIM

    s_out, s_qkv, s_o = ssm_w_out[0].astype(BF16), attn_w_qkv[0].astype(BF16), attn_w_o[0].astype(BF16)
    s_w1, s_w2 = mlp_w1.astype(BF16), mlp_w2.astype(BF16)
    g_in, g_cw = _exchange("gather", [ssm_w_in[0].astype(BF16), ssm_conv_w[0]], "gather_in_proj")
    w_in = jnp.transpose(g_in, (1, 0, 2)).reshape(D_MODEL, in_dim)
    w_z, w_x = w_in[:, :D_INNER], w_in[:, D_INNER:zx_dim]
    w_dt = jnp.pad(w_in[:, zx_dim:], ((0, 0), (0, 128 - SSM_HEADS)))
    conv_w = jnp.transpose(g_cw, (1, 0, 2)).reshape(CONV_WIDTH, CONV_DIM)
    g_w1, g_w2 = [None, None], [None, None]

    def lanes(p):
        return jnp.pad(p.reshape(SSM_GROUPS, 1, 8), ((0, 0), (0, 0), (0, 120)))

    def rows(p):
        return jnp.broadcast_to(p.reshape(SSM_HEADS, 1), (SSM_HEADS, 128))

    bias_g, bias_r = lanes(ssm_dt_bias[0]), rows(ssm_dt_bias[0])
    alog_g, alog_r = lanes(ssm_a_log[0]), rows(ssm_a_log[0])
    d_exp = jnp.repeat(ssm_d[0], SSM_HEAD_DIM).reshape(1, D_INNER)
    norm_w = ssm_norm_w

    def relu2(tot):
        r = jnp.maximum(tot, 0.0)
        return r, r * r

    def mlp_fwd(xin, layer, tag, down_carry):
        h = _rmsnorm_fwd(xin, norm_mlp[layer:layer + 1], f"norm_mlp{tag}")
        r, a, w2 = _mm_fwd(h, g_w1[layer], "cols", D_MODEL, D_FF, epi=relu2, outs=(BF16, BF16),
                           carry=("gather", [s_w2[layer]]), name=f"mlp_up{tag}")
        g_w2[layer] = w2.reshape(D_FF, D_MODEL)
        xout, *got = _mm_fwd(a, g_w2[layer], "plain", D_FF, D_MODEL, epi=_add, outs=(F32,), extras=(xin,),
                             carry=("gather", down_carry) if down_carry else None, name=f"mlp_down{tag}")
        return h, r, a, xout, got

    h0 = _rmsnorm_fwd(x0, norm_mix[0:1], "norm_mix0")
    z, g_out = _mm_fwd(h0, w_z, "plain", D_MODEL, D_INNER, epi=_ident, outs=(F32,), carry=("gather", [s_out]), name="ssm_in_z")
    xpre, g_w1[0] = _mm_fwd(h0, w_x, "plain", D_MODEL, CONV_DIM, epi=_ident, outs=(F32,), carry=("gather", [s_w1[0]]),
                            name="ssm_in_x")
    dtp, = _mm_fwd(h0, w_dt, "plain", D_MODEL, 128, epi=_ident, outs=(F32,), name="ssm_in_dt")
    xbc = _conv_fwd(xpre, conv_w, ssm_conv_b, "conv_fwd")
    dtp64 = dtp[:, :SSM_HEADS]
    dtp_g = jnp.pad(jnp.transpose(dtp64.reshape(t, SSM_GROUPS, 8), (1, 0, 2)), ((0, 0), (0, 0), (0, 120)))
    dtp_t = jnp.transpose(dtp64)
    y_ssd, yn, states = _ssd_fwd(xbc, z, dtp_g, dtp_t, bias_g, bias_r, alog_g, alog_r, d_exp, norm_w, "ssd_fwd")
    g_out = g_out.reshape(D_INNER, D_MODEL)
    x1, g_o = _mm_fwd(yn, g_out, "plain", D_INNER, D_MODEL, epi=_add, outs=(F32,), extras=(x0,), carry=("gather", [s_o]),
                      name="ssm_out")
    h1, r1, a1, x2, (g_qkv,) = mlp_fwd(x1, 0, "0", [s_qkv])

    h2 = _rmsnorm_fwd(x2, norm_mix[1:2], "norm_mix1")
    qkv, g_w1[1] = _mm_fwd(h2, g_qkv, "cols", D_MODEL, QKV_DIM, epi=_ident, outs=(F32,), tn=QKV_DIM // NDEV,
                           carry=("gather", [s_w1[1]]), name="attn_qkv")
    att = [_attn_fwd(qkv, g, f"attn_fwd{g}") for g in range(3)]
    os_, ls_ = [a[0] for a in att], [a[1] for a in att]
    o_mix = _combine_fwd(os_, ls_, "attn_combine")
    x3, = _mm_fwd(o_mix, g_o, "cols", ATTN_W, D_MODEL, epi=_add, outs=(F32,), extras=(x2,), tn=D_MODEL // NDEV, name="attn_out")
    h3, r3, a3, x4, _ = mlp_fwd(x3, 1, "1", None)

    dx4, dx4b, loss_acc, d_final = _loss_head(x4, tgt, final_norm.reshape(1, D_MODEL), "loss_head")

    def mlp_bwd(xin, h, r, a, dxo, dxob, layer, tag):
        du, = _mm_dx(dxob, g_w2[layer], "plain", D_FF, D_MODEL, epi=lambda tot, rr: (tot * (2.0 * rr.astype(F32)),),
                     outs=(BF16,), extras=(r,), name=f"mlp_down_dx{tag}")
        dw2, = _mm_dw(a, dxob, "plain", name=f"mlp_down_dw{tag}")
        dw2 = dw2.reshape(NDEV, D_FF // NDEV, D_MODEL)
        dw1, p_dw2 = _mm_dw(h, du, "cols", carry=("scatter", [dw2]), name=f"mlp_up_dw{tag}")
        dh, p_dw1 = _mm_dx(du, g_w1[layer], "cols", D_MODEL, D_FF, epi=_ident, outs=(F32,), carry=("scatter", [dw1]),
                           name=f"mlp_up_dx{tag}")
        dxi, dxib, dg = _rmsnorm_bwd(xin, norm_mlp[layer:layer + 1], dh, dxo, f"norm_mlp_bwd{tag}")
        return dxi, dxib, dg, p_dw1, p_dw2

    dx3, dx3b, dg_mlp1, p_w1_1, p_w2_1 = mlp_bwd(x3, h3, r3, a3, dx4, dx4b, 1, "1")

    dw_o, = _mm_dw(o_mix, dx3b, "cols", tn=D_MODEL // NDEV, name="attn_out_dw")
    do, p_o = _mm_dx(dx3b, g_o, "cols", ATTN_W, D_MODEL, epi=_ident, outs=(F32,), tk=D_MODEL // NDEV,
                     carry=("scatter", [dw_o]), name="attn_out_dx")
    cb = _combine_bwd(do, os_, ls_, "attn_combine_bwd")
    dos, dvecs = cb[:3], cb[3:]
    dqkv = None
    for g in range(3):
        dqkv = _attn_bwd(qkv, dos[g], ls_[g], dvecs[g], dqkv, g, f"attn_bwd{g}")
    dw_qkv, = _mm_dw(h2, dqkv, "cols", tn=QKV_DIM // NDEV, tk=1024, name="attn_qkv_dw")
    dh2, p_qkv = _mm_dx(dqkv, g_qkv, "cols", D_MODEL, QKV_DIM, epi=_ident, outs=(F32,), tk=QKV_DIM // NDEV,
                        carry=("scatter", [dw_qkv]), name="attn_qkv_dx")
    dx2, dx2b, dg_mix1 = _rmsnorm_bwd(x2, norm_mix[1:2], dh2, dx3, "norm_mix_bwd1")

    dx1, dx1b, dg_mlp0, p_w1_0, p_w2_0 = mlp_bwd(x1, h1, r1, a1, dx2, dx2b, 0, "0")

    dw_out, = _mm_dw(yn, dx1b, "plain", name="ssm_out_dw")
    dw_out = dw_out.reshape(NDEV, D_INNER // NDEV, D_MODEL)
    dyn, p_out = _mm_dx(dx1b, g_out, "plain", D_INNER, D_MODEL, epi=_ident, outs=(F32,), carry=("scatter", [dw_out]),
                        name="ssm_out_dx")
    dxs, d_b, d_c, dz, ddtp_g, hsums, csums = _ssd_bwd(dyn, y_ssd, xbc, z, states, dtp_g, dtp_t, bias_g, bias_r,
                                                       alog_g, alog_r, d_exp, norm_w, "ssd_bwd")
    dxbc = jnp.concatenate([dxs, d_b, d_c], axis=1)
    dpre, conv_sums = _conv_bwd_pre(xpre, conv_w, ssm_conv_b, dxbc, "conv_bwd_pre")
    du = _conv_bwd_in(dpre, conv_w, "conv_bwd_in")
    ddtp = jnp.transpose(ddtp_g[:, :, :8], (1, 0, 2)).reshape(t, SSM_HEADS)
    ddtp = jnp.pad(ddtp, ((0, 0), (0, 128 - SSM_HEADS))).astype(BF16)
    dw_z, = _mm_dw(h0, dz, "plain", name="ssm_in_z_dw")
    dw_x, = _mm_dw(h0, du, "plain", name="ssm_in_x_dw")
    dw_dt, = _mm_dw(h0, ddtp, "plain", name="ssm_in_dt_dw")
    dw_in = jnp.concatenate([dw_z, dw_x, dw_dt[:, :SSM_HEADS]], axis=1)
    dw_in = jnp.transpose(dw_in.reshape(D_MODEL, NDEV, in_shard), (1, 0, 2))
    dh0, = _mm_dx(ddtp, w_dt, "plain", D_MODEL, 128, epi=_ident, outs=(F32,), name="ssm_in_dt_dx")
    dh0, = _mm_dx(dz, w_z, "plain", D_MODEL, D_INNER, epi=_add, outs=(F32,), extras=(dh0,), name="ssm_in_z_dx")
    dh0, p_in = _mm_dx(du, w_x, "plain", D_MODEL, CONV_DIM, epi=_add, outs=(F32,), extras=(dh0,),
                       carry=("scatter", [dw_in]), name="ssm_in_x_dx")
    dx0, _, dg_mix0 = _rmsnorm_bwd(x0, norm_mix[0:1], dh0, dx1, "norm_mix_bwd0")

    parts = [p_in, p_out, p_qkv, p_o, p_w1_0, p_w1_1, p_w2_0, p_w2_1]

    def own(w, mm, vv):
        shp = w.shape
        f = lambda a: a.reshape(-1, shp[-1])
        return f(w), f(mm), f(vv), shp

    big = {}
    for key, part, (w, mm, vv) in (
            ("ssm_w_in", parts[0], (ssm_w_in, m_ssm_w_in, v_ssm_w_in)),
            ("ssm_w_out", parts[1], (ssm_w_out, m_ssm_w_out, v_ssm_w_out)),
            ("attn_w_qkv", parts[2], (attn_w_qkv, m_attn_w_qkv, v_attn_w_qkv)),
            ("attn_w_o", parts[3], (attn_w_o, m_attn_w_o, v_attn_w_o))):
        w2, m2, v2, shp = own(w, mm, vv)
        res = _reduce_adamw(part, w2, m2, v2, f"adamw_{key}")
        big[key] = [r.reshape(shp) for r in res]
    for key, pa, pb, (w, mm, vv) in (("mlp_w1", parts[4], parts[5], (mlp_w1, m_mlp_w1, v_mlp_w1)),
                                     ("mlp_w2", parts[6], parts[7], (mlp_w2, m_mlp_w2, v_mlp_w2))):
        res = [_reduce_adamw(p, w[l], mm[l], vv[l], f"adamw_{key}_{l}") for l, p in enumerate((pa, pb))]
        big[key] = [jnp.stack([res[0][i], res[1][i]], axis=0) for i in range(4)]

    d_norm_mix = jnp.concatenate([dg_mix0, dg_mix1], axis=0)
    d_norm_mlp = jnp.concatenate([dg_mlp0, dg_mlp1], axis=0)
    d_conv_b = conv_sums[4:5]
    d_conv_w = conv_sums[0:4]
    head = hsums[:, :, :8]
    d_dt_bias, d_a_log, d_d = (head[:, k, :].reshape(1, SSM_HEADS) for k in range(3))
    d_ssm_norm = csums[0:1]
    small = [d_norm_mix, d_norm_mlp, d_conv_b, d_dt_bias, d_a_log, d_d, d_ssm_norm, d_final, d_conv_w, loss_acc[0:1, 0:1]]
    shapes = [a.shape for a in small]
    summed = _unpack(_all_reduce_small(_pack(small), "reduce_small"), shapes)
    g_conv_w_full = summed[8]
    loss = summed[9].reshape(())
    g_conv_w = lax.dynamic_slice(g_conv_w_full, (0, me * (CONV_DIM // NDEV)), (CONV_WIDTH, CONV_DIM // NDEV))

    small_names = ["norm_mix", "norm_mlp", "ssm_conv_b", "ssm_dt_bias", "ssm_a_log", "ssm_d", "ssm_norm_w", "final_norm"]
    small_w = [norm_mix, norm_mlp, ssm_conv_b, ssm_dt_bias, ssm_a_log, ssm_d, ssm_norm_w, final_norm, ssm_conv_w]
    small_m = [m_norm_mix, m_norm_mlp, m_ssm_conv_b, m_ssm_dt_bias, m_ssm_a_log, m_ssm_d, m_ssm_norm_w, m_final_norm, m_ssm_conv_w]
    small_v = [v_norm_mix, v_norm_mlp, v_ssm_conv_b, v_ssm_dt_bias, v_ssm_a_log, v_ssm_d, v_ssm_norm_w, v_final_norm, v_ssm_conv_w]
    small_g = [summed[i].reshape(small_w[i].shape) for i in range(8)] + [g_conv_w.reshape(ssm_conv_w.shape)]
    wshapes = [a.shape for a in small_w]
    sd, sm, sv = _adamw(_pack(small_g), _pack(small_w), _pack(small_m), _pack(small_v), "adamw_small")
    sd, sm, sv = _unpack(sd, wshapes), _unpack(sm, wshapes), _unpack(sv, wshapes)
    res = {n: (small_g[i], sd[i], sm[i], sv[i]) for i, n in enumerate(small_names + ["ssm_conv_w"])}
    for n in big:
        res[n] = tuple(big[n])

    order = ["norm_mix", "norm_mlp", "ssm_w_in", "ssm_conv_w", "ssm_conv_b", "ssm_dt_bias", "ssm_a_log", "ssm_d",
             "ssm_norm_w", "ssm_w_out", "attn_w_qkv", "attn_w_o", "mlp_w1", "mlp_w2", "final_norm"]
    outs = [loss, dx0.reshape(x.shape)]
    for kind in range(4):
        outs += [res[n][kind] for n in order]
    return tuple(outs)
```

```python
import math

import jax
import jax.numpy as jnp
from jax import lax
from jax.experimental import pallas as pl
from jax.experimental.pallas import tpu as pltpu

F32, BF16 = jnp.float32, jnp.bfloat16
SDS = jax.ShapeDtypeStruct
MESH = pl.DeviceIdType.MESH
HIGHEST = lax.Precision.HIGHEST

NDEV = 8
D_MODEL = 2048
D_INNER = 4096
SSM_HEADS = 64
SSM_HEAD_DIM = 64
SSM_GROUPS = 8
SSM_STATE = 128
CHUNK = 128
CONV_DIM = 6144
CONV_WIDTH = 4
GROUP_W = D_INNER // SSM_GROUPS
ATTN_GROUPS = ((128, 1), (512, 4), (2048, 16))
ATTN_W = 1024
QKV_DIM = 9216
D_FF = 8192
EPS = 1e-5
ADAM_LR, ADAM_B1, ADAM_B2, ADAM_EPS, ADAM_WD, ADAM_STEP = 0.001, 0.9, 0.999, 1e-08, 0.01, 10

VMEM_LIMIT = 48 * 1024 * 1024


def _params(sem):
    return pltpu.CompilerParams(dimension_semantics=sem, vmem_limit_bytes=VMEM_LIMIT)


def _sigmoid(v):
    return 1.0 / (1.0 + jnp.exp(-v))


def _softplus(v):
    return jnp.maximum(v, 0.0) + jnp.log1p(jnp.exp(-jnp.abs(v)))


def _wspec(layout, kw, nw, tr, tc, sel):
    if layout == "plain":
        return pl.BlockSpec((tr, tc), lambda *g: sel(*g))
    per = (nw // NDEV) // tc
    return pl.BlockSpec((None, tr, tc), lambda *g: (sel(*g)[1] // per, sel(*g)[0], sel(*g)[1] % per))


def _wshape(layout, kw, nw):
    return (kw, nw) if layout == "plain" else (NDEV, kw, nw // NDEV)


def _mm_call(name, grid, in_specs, out_specs, out_shape, dims, n_extra, epi, tm, tn, carry):
    nk = grid[2]
    kind, moved = carry if carry else (None, ())
    nc = len(moved)
    n_out = len(out_shape)

    def body(*refs):
        a_ref, b_ref = refs[0], refs[1]
        extra = refs[2:2 + n_extra]
        c_in = refs[2 + n_extra:2 + n_extra + nc]
        outs = refs[2 + n_extra + nc:2 + n_extra + nc + n_out]
        c_out = refs[2 + n_extra + nc + n_out:2 + n_extra + 2 * nc + n_out]
        acc = refs[2 + n_extra + 2 * nc + n_out]
        sems = refs[3 + n_extra + 2 * nc + n_out:]
        i, j, k = pl.program_id(0), pl.program_id(1), pl.program_id(2)
        if nc:
            @pl.when((i == 0) & (j == 0) & (k == 0))
            def _():
                _exchange_start(kind, c_in, c_out, *sems)

        d = lax.dot_general(a_ref[...].astype(BF16), b_ref[...].astype(BF16), (dims, ((), ())), preferred_element_type=F32)

        def finish(total):
            vals = epi(total, *[e[...] for e in extra])
            for o, v in zip(outs, vals):
                o[...] = v.astype(o.dtype)

        if nk == 1:
            finish(d)
        else:
            @pl.when(k == 0)
            def _():
                acc[...] = d

            @pl.when(jnp.logical_and(k > 0, k < nk - 1))
            def _():
                acc[...] += d

            @pl.when(k == nk - 1)
            def _():
                finish(acc[...] + d)

        if nc:
            @pl.when((i == grid[0] - 1) & (j == grid[1] - 1) & (k == nk - 1))
            def _():
                _exchange_wait(kind, c_in, c_out, *sems)

    hbm = pl.BlockSpec(memory_space=pl.ANY)
    scratch = [pltpu.VMEM((tm, tn), F32)] + (_exchange_sems(nc) if nc else [])
    c_shape = [SDS(((NDEV,) + t.shape) if kind == "gather" else t.shape, t.dtype) for t in moved]
    sem = ("arbitrary",) * 3 if nc else ("parallel", "parallel", "arbitrary")
    return pl.pallas_call(
        body, grid=grid, in_specs=in_specs + [hbm] * nc, out_specs=out_specs + [hbm] * nc,
        out_shape=out_shape + c_shape, scratch_shapes=scratch, name=name, compiler_params=_params(sem))


def _mm_fwd(a, w, layout, kw, nw, *, epi, outs, extras=(), tm=1024, tn=1024, tk=2048, carry=None, name):
    m = a.shape[0]
    tm, tn, tk = min(tm, m), min(tn, nw), min(tk, kw)
    grid = (m // tm, nw // tn, kw // tk)
    o_spec = pl.BlockSpec((tm, tn), lambda i, j, k: (i, j))
    in_specs = [pl.BlockSpec((tm, tk), lambda i, j, k: (i, k)), _wspec(layout, kw, nw, tk, tn, lambda i, j, k: (k, j))]
    in_specs += [o_spec] * len(extras)
    call = _mm_call(name, grid, in_specs, [o_spec] * len(outs), [SDS((m, nw), dt) for dt in outs], ((1,), (0,)),
                    len(extras), epi, tm, tn, carry)
    return call(a, w, *extras, *(carry[1] if carry else ()))


def _mm_dx(g, w, layout, kw, nw, *, epi, outs, extras=(), tm=1024, tn=1024, tk=2048, carry=None, name):
    m = g.shape[0]
    tm, tn, tk = min(tm, m), min(tn, kw), min(tk, nw if layout == "plain" else nw // NDEV)
    grid = (m // tm, kw // tn, nw // tk)
    o_spec = pl.BlockSpec((tm, tn), lambda i, j, k: (i, j))
    in_specs = [pl.BlockSpec((tm, tk), lambda i, j, k: (i, k)), _wspec(layout, kw, nw, tn, tk, lambda i, j, k: (j, k))]
    in_specs += [o_spec] * len(extras)
    call = _mm_call(name, grid, in_specs, [o_spec] * len(outs), [SDS((m, kw), dt) for dt in outs], ((1,), (1,)),
                    len(extras), epi, tm, tn, carry)
    return call(g, w, *extras, *(carry[1] if carry else ()))


def _mm_dw(a, g, layout, *, tm=1024, tn=1024, tk=2048, carry=None, name):
    m, kw = a.shape
    nw = g.shape[1]
    tm, tn, tk = min(tm, kw), min(tn, nw if layout == "plain" else nw // NDEV), min(tk, m)
    grid = (kw // tm, nw // tn, m // tk)
    in_specs = [pl.BlockSpec((tk, tm), lambda i, j, k: (k, i)), pl.BlockSpec((tk, tn), lambda i, j, k: (k, j))]
    o_spec = _wspec(layout, kw, nw, tm, tn, lambda i, j, k: (i, j))
    call = _mm_call(name, grid, in_specs, [o_spec], [SDS(_wshape(layout, kw, nw), BF16)], ((0,), (0,)),
                    0, lambda t: (t,), tm, tn, carry)
    return call(a, g, *(carry[1] if carry else ()))


ROW_TILE = 256


def _rmsnorm_fwd(x, g, name):
    t, d = x.shape

    def body(x_ref, g_ref, h_ref):
        xv = x_ref[...]
        r = lax.rsqrt(jnp.mean(xv * xv, axis=-1, keepdims=True) + EPS)
        h_ref[...] = (xv * r * g_ref[...]).astype(BF16)

    row = pl.BlockSpec((ROW_TILE, d), lambda i: (i, 0))
    vec = pl.BlockSpec((1, d), lambda i: (0, 0))
    return pl.pallas_call(body, grid=(t // ROW_TILE,), in_specs=[row, vec], out_specs=row, out_shape=SDS((t, d), BF16),
                          name=name, compiler_params=_params(("parallel",)))(x, g)


def _rmsnorm_bwd(x, g, dh, dres, name):
    t, d = x.shape

    def body(x_ref, g_ref, dh_ref, dres_ref, dx_ref, dxb_ref, dg_ref):
        xv = x_ref[...]
        r = lax.rsqrt(jnp.mean(xv * xv, axis=-1, keepdims=True) + EPS)
        xh = xv * r
        dhv = dh_ref[...]
        gd = dhv * g_ref[...]
        dx = dres_ref[...] + r * (gd - xh * jnp.mean(gd * xh, axis=-1, keepdims=True))
        dx_ref[...] = dx
        dxb_ref[...] = dx.astype(BF16)
        part = jnp.sum(dhv * xh, axis=0, keepdims=True)

        @pl.when(pl.program_id(0) == 0)
        def _():
            dg_ref[...] = part

        @pl.when(pl.program_id(0) > 0)
        def _():
            dg_ref[...] += part

    row = pl.BlockSpec((ROW_TILE, d), lambda i: (i, 0))
    vec = pl.BlockSpec((1, d), lambda i: (0, 0))
    return pl.pallas_call(body, grid=(t // ROW_TILE,), in_specs=[row, vec, row, row], out_specs=[row, row, vec],
                          out_shape=[SDS((t, d), F32), SDS((t, d), BF16), SDS((1, d), F32)], name=name,
                          compiler_params=_params(("arbitrary",)))(x, g, dh, dres)


def _loss_head(x, tgt, g, name):
    t, d = x.shape

    def body(x_ref, t_ref, g_ref, dx_ref, dxb_ref, loss_ref, dg_ref):
        xv = x_ref[...]
        r = lax.rsqrt(jnp.mean(xv * xv, axis=-1, keepdims=True) + EPS)
        xh = xv * r
        gv = g_ref[...]
        err = xh * gv - t_ref[...]
        part_loss = 0.5 * jnp.sum(jnp.mean(err * err, axis=-1, keepdims=True), axis=0, keepdims=True)
        dy = err * (1.0 / d)
        gd = dy * gv
        dx = r * (gd - xh * jnp.mean(gd * xh, axis=-1, keepdims=True))
        dx_ref[...] = dx
        dxb_ref[...] = dx.astype(BF16)
        part_g = jnp.sum(dy * xh, axis=0, keepdims=True)
        part_l = jnp.broadcast_to(part_loss, (8, 128))

        @pl.when(pl.program_id(0) == 0)
        def _():
            dg_ref[...] = part_g
            loss_ref[...] = part_l

        @pl.when(pl.program_id(0) > 0)
        def _():
            dg_ref[...] += part_g
            loss_ref[...] += part_l

    row = pl.BlockSpec((ROW_TILE, d), lambda i: (i, 0))
    vec = pl.BlockSpec((1, d), lambda i: (0, 0))
    sc = pl.BlockSpec((8, 128), lambda i: (0, 0))
    return pl.pallas_call(body, grid=(t // ROW_TILE,), in_specs=[row, row, vec], out_specs=[row, row, sc, vec],
                          out_shape=[SDS((t, d), F32), SDS((t, d), BF16), SDS((8, 128), F32), SDS((1, d), F32)], name=name,
                          compiler_params=_params(("arbitrary",)))(x, tgt, g)


CONV_ROWS = 256
CONV_COLS = 2048


def _shift_down(cur, prev8, k):
    sh = pltpu.roll(cur, k, axis=0)
    ph = pltpu.roll(prev8, k, axis=0)
    rid = lax.broadcasted_iota(jnp.int32, ph.shape, 0)
    head = jnp.where(rid < k, ph, sh[0:8])
    return jnp.concatenate([head, sh[8:]], axis=0)


def _shift_up(cur, next8, k):
    n = cur.shape[0]
    sh = pltpu.roll(cur, n - k, axis=0)
    nh = pltpu.roll(next8, 8 - k, axis=0)
    rid = lax.broadcasted_iota(jnp.int32, nh.shape, 0)
    tail = jnp.where(rid >= 8 - k, nh, sh[n - 8:])
    return jnp.concatenate([sh[:n - 8], tail], axis=0)


def _conv_pre(cur, prev8, w, b):
    acc = w[3:4, :] * cur + b
    for k in range(1, CONV_WIDTH):
        acc = acc + w[3 - k:4 - k, :] * _shift_down(cur, prev8, k)
    return acc


def _conv_fwd(u, w, b, name):
    t, c = u.shape
    per = CONV_ROWS // 8

    def body(u_ref, p_ref, w_ref, b_ref, o_ref):
        prev8 = jnp.where(pl.program_id(1) == 0, 0.0, p_ref[...])
        pre = _conv_pre(u_ref[...], prev8, w_ref[...], b_ref[...])
        o_ref[...] = pre * _sigmoid(pre)

    cur = pl.BlockSpec((CONV_ROWS, CONV_COLS), lambda j, i: (i, j))
    prev = pl.BlockSpec((8, CONV_COLS), lambda j, i: (jnp.maximum(i * per - 1, 0), j))
    wsp = pl.BlockSpec((CONV_WIDTH, CONV_COLS), lambda j, i: (0, j))
    bsp = pl.BlockSpec((1, CONV_COLS), lambda j, i: (0, j))
    return pl.pallas_call(body, grid=(c // CONV_COLS, t // CONV_ROWS), in_specs=[cur, prev, wsp, bsp], out_specs=cur,
                          out_shape=SDS((t, c), F32), name=name, compiler_params=_params(("parallel", "parallel")))(u, u, w, b)


def _conv_bwd_pre(u, w, b, dout, name):
    t, c = u.shape
    per = CONV_ROWS // 8

    def body(u_ref, p_ref, w_ref, b_ref, d_ref, dp_ref, dw_ref):
        i = pl.program_id(1)
        cur = u_ref[...]
        prev8 = jnp.where(i == 0, 0.0, p_ref[...])
        pre = _conv_pre(cur, prev8, w_ref[...], b_ref[...])
        s = _sigmoid(pre)
        dpre = d_ref[...] * (s * (1.0 + pre * (1.0 - s)))
        dp_ref[...] = dpre
        rows = [jnp.sum(dpre * _shift_down(cur, prev8, 3 - k), axis=0, keepdims=True) for k in range(3)]
        rows.append(jnp.sum(dpre * cur, axis=0, keepdims=True))
        rows.append(jnp.sum(dpre, axis=0, keepdims=True))
        part = jnp.concatenate(rows + [jnp.zeros((3, cur.shape[1]), F32)], axis=0)

        @pl.when(i == 0)
        def _():
            dw_ref[...] = part

        @pl.when(i > 0)
        def _():
            dw_ref[...] += part

    cur = pl.BlockSpec((CONV_ROWS, CONV_COLS), lambda j, i: (i, j))
    prev = pl.BlockSpec((8, CONV_COLS), lambda j, i: (jnp.maximum(i * per - 1, 0), j))
    wsp = pl.BlockSpec((CONV_WIDTH, CONV_COLS), lambda j, i: (0, j))
    bsp = pl.BlockSpec((1, CONV_COLS), lambda j, i: (0, j))
    acc = pl.BlockSpec((8, CONV_COLS), lambda j, i: (0, j))
    return pl.pallas_call(body, grid=(c // CONV_COLS, t // CONV_ROWS), in_specs=[cur, prev, wsp, bsp, cur],
                          out_specs=[cur, acc], out_shape=[SDS((t, c), F32), SDS((8, c), F32)], name=name,
                          compiler_params=_params(("parallel", "arbitrary")))(u, u, w, b, dout)


def _conv_bwd_in(dpre, w, name):
    t, c = dpre.shape
    per = CONV_ROWS // 8
    last = t // CONV_ROWS - 1

    def body(d_ref, n_ref, w_ref, o_ref):
        cur = d_ref[...]
        next8 = jnp.where(pl.program_id(1) == last, 0.0, n_ref[...])
        wv = w_ref[...]
        acc = wv[3:4, :] * cur
        for k in range(1, CONV_WIDTH):
            acc = acc + wv[3 - k:4 - k, :] * _shift_up(cur, next8, k)
        o_ref[...] = acc.astype(BF16)

    cur = pl.BlockSpec((CONV_ROWS, CONV_COLS), lambda j, i: (i, j))
    nxt = pl.BlockSpec((8, CONV_COLS), lambda j, i: (jnp.minimum((i + 1) * per, t // 8 - 1), j))
    wsp = pl.BlockSpec((CONV_WIDTH, CONV_COLS), lambda j, i: (0, j))
    return pl.pallas_call(body, grid=(c // CONV_COLS, t // CONV_ROWS), in_specs=[cur, nxt, wsp], out_specs=cur,
                          out_shape=SDS((t, c), BF16), name=name, compiler_params=_params(("parallel", "parallel")))(dpre, dpre, w)


def _pieces(v, n):
    out, rest = [], v
    for _ in range(n):
        p = rest.astype(BF16)
        out.append(p)
        rest = rest - p.astype(F32)
    return out


def _head_expand(v):
    r = lax.broadcasted_iota(jnp.int32, (3 * 128, GROUP_W), 0) % 128
    c = lax.broadcasted_iota(jnp.int32, (3 * 128, GROUP_W), 1)
    return _dot(jnp.concatenate(_pieces(v, 3), axis=1), (c // SSM_HEAD_DIM == r).astype(BF16))


def _head_sum(vs):
    r = lax.broadcasted_iota(jnp.int32, (2 * GROUP_W, 128), 0) % GROUP_W
    c = lax.broadcasted_iota(jnp.int32, (2 * GROUP_W, 128), 1)
    stacked = jnp.concatenate([jnp.concatenate(_pieces(v, 2), axis=1) for v in vs], axis=0)
    out = _dot(stacked, (r // SSM_HEAD_DIM == c).astype(BF16))
    res, at = [], 0
    for v in vs:
        res.append(out[at:at + v.shape[0]])
        at += v.shape[0]
    return res


def _tri_left(tri, v):
    r = _dot(tri.astype(BF16), jnp.concatenate(_pieces(v, 3), axis=1))
    return r[:, 0:128] + r[:, 128:256] + r[:, 256:384]


def _dot(a, b):
    return jnp.dot(a, b, preferred_element_type=F32)


def _dot_nt(a, b):
    return lax.dot_general(a, b, (((1,), (1,)), ((), ())), preferred_element_type=F32)


def _dot_tn(a, b):
    return lax.dot_general(a, b, (((0,), (0,)), ((), ())), preferred_element_type=F32)


def _ssd_common(dtp, dtpt, bias, biasr, alog, alogr):
    li = lax.broadcasted_iota(jnp.int32, (CHUNK, CHUNK), 0)
    si = lax.broadcasted_iota(jnp.int32, (CHUNK, CHUNK), 1)
    lower, upper = (li >= si), (li <= si)
    dt = _softplus(dtp + bias)
    a_neg = -jnp.exp(alog)
    cs = _tri_left(lower, dt * a_neg)
    dtr = _softplus(dtpt + biasr)
    ar = jnp.concatenate([dtr * (-jnp.exp(alogr)), jnp.zeros((8, CHUNK), F32)], axis=0)
    r3 = lax.broadcasted_iota(jnp.int32, (3 * CHUNK, CHUNK), 0) % CHUNK
    c3 = lax.broadcasted_iota(jnp.int32, (3 * CHUNK, CHUNK), 1)
    csr = _dot(jnp.concatenate(_pieces(ar, 3), axis=1), (r3 <= c3).astype(BF16))[0:8]
    return lower, upper, dt, a_neg, cs, csr


def _head_masked_rows(v):
    hl = lax.broadcasted_iota(jnp.int32, v.shape, 1) // SSM_HEAD_DIM
    return jnp.concatenate([jnp.where(hl == j, v, jnp.zeros_like(v)) for j in range(8)], axis=0)


def _ssd_fwd(xbc, z, dtp_g, dtp_t, bias_g, bias_r, alog_g, alog_r, d_exp, norm_w, name):
    t = xbc.shape[0]
    nc = t // CHUNK

    def body(xs_ref, b_ref, c_ref, z_ref, dtp_ref, dtpt_ref, bias_ref, biasr_ref, alog_ref, alogr_ref, dexp_ref, nw_ref,
             y_ref, yn_ref, st_ref, state):
        c = pl.program_id(1)

        @pl.when(c == 0)
        def _():
            state[...] = jnp.zeros_like(state)

        lower, _, dt, a_neg, cs, csr = _ssd_common(dtp_ref[...], dtpt_ref[...], bias_ref[...], biasr_ref[...],
                                                alog_ref[...], alogr_ref[...])
        cs_e = _head_expand(cs)
        dt_e = _head_expand(dt)
        xs = xs_ref[...]
        xdt = xs * dt_e
        bm = b_ref[...]
        cm = c_ref[...]
        bmb, cmb = bm.astype(BF16), cm.astype(BF16)
        cb = _dot_nt(cmb, bmb)
        ms = []
        for j in range(8):
            dlt = cs_e[:, SSM_HEAD_DIM * j:SSM_HEAD_DIM * j + 1] - csr[j:j + 1, :]
            ms.append((cb * jnp.exp(jnp.where(lower, dlt, -jnp.inf))).astype(BF16))
        y = _dot(jnp.concatenate(ms, axis=1), _head_masked_rows(xdt.astype(BF16)))
        st_in = state[...]
        st_ref[...] = st_in
        y = y + jnp.exp(cs_e) * _dot(cmb, st_in.astype(BF16))
        cs_last = cs_e[CHUNK - 1:CHUNK, :]
        xdtd = (xdt * jnp.exp(cs_last - cs_e)).astype(BF16)
        state[...] = jnp.exp(cs_last) * st_in + _dot(bm.T.astype(BF16), xdtd)
        y_ref[...] = y
        zz = z_ref[...]
        y2 = (y + dexp_ref[...] * xs) * (zz * _sigmoid(zz))
        r = lax.rsqrt(jnp.mean(y2 * y2, axis=-1, keepdims=True) + EPS)
        yn_ref[...] = (y2 * r * nw_ref[...]).astype(BF16)

    gw = pl.BlockSpec((CHUNK, GROUP_W), lambda g, c: (c, g))
    in_specs = [
        gw,
        pl.BlockSpec((CHUNK, SSM_STATE), lambda g, c: (c, D_INNER // SSM_STATE + g)),
        pl.BlockSpec((CHUNK, SSM_STATE), lambda g, c: (c, D_INNER // SSM_STATE + SSM_GROUPS + g)),
        gw,
        pl.BlockSpec((None, CHUNK, 128), lambda g, c: (g, c, 0)),
        pl.BlockSpec((8, CHUNK), lambda g, c: (g, c)),
        pl.BlockSpec((None, 1, 128), lambda g, c: (g, 0, 0)),
        pl.BlockSpec((8, 128), lambda g, c: (g, 0)),
        pl.BlockSpec((None, 1, 128), lambda g, c: (g, 0, 0)),
        pl.BlockSpec((8, 128), lambda g, c: (g, 0)),
        pl.BlockSpec((1, GROUP_W), lambda g, c: (0, g)),
        pl.BlockSpec((1, GROUP_W), lambda g, c: (0, g)),
    ]
    out_specs = [gw, gw, pl.BlockSpec((None, SSM_STATE, GROUP_W), lambda g, c: (c, 0, g))]
    out_shape = [SDS((t, D_INNER), F32), SDS((t, D_INNER), BF16), SDS((nc, SSM_STATE, D_INNER), F32)]
    return pl.pallas_call(body, grid=(SSM_GROUPS, nc), in_specs=in_specs, out_specs=out_specs, out_shape=out_shape,
                          scratch_shapes=[pltpu.VMEM((SSM_STATE, GROUP_W), F32)], name=name,
                          compiler_params=_params(("parallel", "arbitrary")))(
        xbc, xbc, xbc, z, dtp_g, dtp_t, bias_g, bias_r, alog_g, alog_r, d_exp, norm_w)


def _ssd_bwd(dyn, y, xbc, z, states, dtp_g, dtp_t, bias_g, bias_r, alog_g, alog_r, d_exp, norm_w, name):
    t = xbc.shape[0]
    nc = t // CHUNK

    def body(dyn_ref, y_ref, xs_ref, b_ref, c_ref, z_ref, st_ref, dtp_ref, dtpt_ref, bias_ref, biasr_ref, alog_ref,
             alogr_ref, dexp_ref, nw_ref, dxs_ref, db_ref, dc_ref, dz_ref, ddt_ref, hsum_ref, csum_ref, dstate):
        step = pl.program_id(1)

        @pl.when(step == 0)
        def _():
            dstate[...] = jnp.zeros_like(dstate)

        dtp = dtp_ref[...]
        bias = bias_ref[...]
        lower, upper, dt, a_neg, cs, csr = _ssd_common(dtp, dtpt_ref[...], bias, biasr_ref[...], alog_ref[...],
                                                       alogr_ref[...])
        cs_e = _head_expand(cs)
        dt_e = _head_expand(dt)
        xs = xs_ref[...]
        xdt = xs * dt_e
        bm = b_ref[...]
        cm = c_ref[...]
        bmb, cmb = bm.astype(BF16), cm.astype(BF16)
        y = y_ref[...]
        dexp = dexp_ref[...]
        nw = nw_ref[...]

        zz = z_ref[...]
        sg = _sigmoid(zz)
        gate = zz * sg
        ytot = y + dexp * xs
        y2 = ytot * gate
        r = lax.rsqrt(jnp.mean(y2 * y2, axis=-1, keepdims=True) + EPS)
        dynv = dyn_ref[...]
        xh = y2 * r
        gn = dynv * nw
        dy2 = r * (gn - xh * jnp.mean(gn * xh, axis=-1, keepdims=True))
        dy = dy2 * gate
        dz_ref[...] = (dy2 * ytot * (sg * (1.0 + zz * (1.0 - sg)))).astype(BF16)
        csum_part = jnp.sum(dynv * xh, axis=0, keepdims=True)

        cb = _dot_nt(cmb, bmb)
        dyb = dy.astype(BF16)
        xdtb = xdt.astype(BF16)
        dym = _head_masked_rows(dyb)
        dm = _dot_nt(dym, xdtb)
        dmt = _dot_nt(_head_masked_rows(xdtb), dyb)
        lane = lax.broadcasted_iota(jnp.int32, (CHUNK, 128), 1)
        mts = []
        dcb = jnp.zeros((CHUNK, CHUNK), F32)
        dcs = jnp.zeros((CHUNK, 128), F32)
        for j in range(8):
            dlt = cs_e[:, SSM_HEAD_DIM * j:SSM_HEAD_DIM * j + 1] - csr[j:j + 1, :]
            lj = jnp.exp(jnp.where(lower, dlt, -jnp.inf))
            mj = cb * lj
            mjt = mj.T
            mts.append(mjt.astype(BF16))
            dmj = dm[CHUNK * j:CHUNK * (j + 1)]
            dcb = dcb + dmj * lj
            rows = jnp.sum(dmj * mj, axis=1, keepdims=True)
            cols = jnp.sum(dmt[CHUNK * j:CHUNK * (j + 1)] * mjt, axis=1, keepdims=True)
            dcs = dcs + jnp.where(lane == j, rows - cols, 0.0)
        dxdt = _dot(jnp.concatenate(mts, axis=1), dym)
        dst_out = dstate[...]
        dst_outb = dst_out.astype(BF16)
        st_in = st_ref[...]
        st_inb = st_in.astype(BF16)
        cs_last = cs_e[CHUNK - 1:CHUNK, :]
        decay = jnp.exp(cs_last - cs_e)
        e_last = jnp.exp(cs_last)
        gpart = decay * _dot(bmb, dst_outb)
        dxdt = dxdt + gpart
        dyw = (jnp.exp(cs_e) * dy).astype(BF16)
        dcbb = dcb.astype(BF16)
        dc_ref[...] = _dot_nt(dyw, st_inb) + _dot(dcbb, bmb)
        db_ref[...] = _dot_nt((xdt * decay).astype(BF16), dst_outb) + _dot(dcb.T.astype(BF16), cmb)
        dstate[...] = e_last * dst_out + _dot(cm.T.astype(BF16), dyw)
        y_off = jnp.exp(cs_e) * _dot(cmb, st_inb)
        xg = xdt * gpart
        vec = jnp.concatenate([jnp.sum(dy * xs, axis=0, keepdims=True),
                               jnp.sum(xg + dst_out * e_last * st_in, axis=0, keepdims=True),
                               jnp.zeros((14, GROUP_W), F32)], axis=0)
        s_cs, s_dt, s_vec = _head_sum([dy * y_off - xg, dxdt * xs, vec])
        d_skip = s_vec[0:1]
        ri = lax.broadcasted_iota(jnp.int32, (CHUNK, 128), 0)
        dcs = dcs + s_cs + jnp.where(ri == CHUNK - 1, s_vec[1:2], 0.0)
        da = _tri_left(upper, dcs)
        ddt = da * a_neg + s_dt
        dxs_ref[...] = dxdt * dt_e + dy * dexp
        ddtp = ddt * _sigmoid(dtp + bias)
        ddt_ref[...] = ddtp
        d_alog = jnp.sum(da * dt, axis=0, keepdims=True) * a_neg
        hpart = jnp.concatenate([jnp.sum(ddtp, axis=0, keepdims=True), d_alog, d_skip, jnp.zeros((5, 128), F32)], axis=0)
        cpart = jnp.concatenate([csum_part, jnp.zeros((7, GROUP_W), F32)], axis=0)

        @pl.when(step == 0)
        def _():
            hsum_ref[...] = hpart
            csum_ref[...] = cpart

        @pl.when(step > 0)
        def _():
            hsum_ref[...] += hpart
            csum_ref[...] += cpart

    rc = lambda c: nc - 1 - c
    gw = pl.BlockSpec((CHUNK, GROUP_W), lambda g, c: (rc(c), g))
    bsp = pl.BlockSpec((CHUNK, SSM_STATE), lambda g, c: (rc(c), D_INNER // SSM_STATE + g))
    csp = pl.BlockSpec((CHUNK, SSM_STATE), lambda g, c: (rc(c), D_INNER // SSM_STATE + SSM_GROUPS + g))
    in_specs = [
        gw, gw, gw, bsp, csp, gw,
        pl.BlockSpec((None, SSM_STATE, GROUP_W), lambda g, c: (rc(c), 0, g)),
        pl.BlockSpec((None, CHUNK, 128), lambda g, c: (g, rc(c), 0)),
        pl.BlockSpec((8, CHUNK), lambda g, c: (g, rc(c))),
        pl.BlockSpec((None, 1, 128), lambda g, c: (g, 0, 0)),
        pl.BlockSpec((8, 128), lambda g, c: (g, 0)),
        pl.BlockSpec((None, 1, 128), lambda g, c: (g, 0, 0)),
        pl.BlockSpec((8, 128), lambda g, c: (g, 0)),
        pl.BlockSpec((1, GROUP_W), lambda g, c: (0, g)),
        pl.BlockSpec((1, GROUP_W), lambda g, c: (0, g)),
    ]
    nsp = pl.BlockSpec((CHUNK, SSM_STATE), lambda g, c: (rc(c), g))
    out_specs = [gw, nsp, nsp, gw,
                 pl.BlockSpec((None, CHUNK, 128), lambda g, c: (g, rc(c), 0)),
                 pl.BlockSpec((None, 8, 128), lambda g, c: (g, 0, 0)),
                 pl.BlockSpec((8, GROUP_W), lambda g, c: (0, g))]
    gn = SSM_GROUPS * SSM_STATE
    out_shape = [SDS((t, D_INNER), F32), SDS((t, gn), F32), SDS((t, gn), F32), SDS((t, D_INNER), BF16),
                 SDS((SSM_GROUPS, t, 128), F32), SDS((SSM_GROUPS, 8, 128), F32), SDS((8, D_INNER), F32)]
    return pl.pallas_call(body, grid=(SSM_GROUPS, nc), in_specs=in_specs, out_specs=out_specs, out_shape=out_shape,
                          scratch_shapes=[pltpu.VMEM((SSM_STATE, GROUP_W), F32)], name=name,
                          compiler_params=_params(("parallel", "arbitrary")))(
        dyn, y, xbc, xbc, xbc, z, states, dtp_g, dtp_t, bias_g, bias_r, alog_g, alog_r, d_exp, norm_w)


BLK = 128
HEAD_PAIRS = ATTN_W // 128
ATTN_SCALE = 0.125
SPAN_BLOCKS = {1: 8, 4: 2, 16: 1}
QKV_LANE_BLOCKS = ATTN_W * len(ATTN_GROUPS) // 128


def _slope_table(group):
    n = len(ATTN_GROUPS) * 16
    tbl = [[2.0 ** (-8.0 * (16 * group + 2 * p + s + 1) / n) if s < 2 else 0.0 for s in range(128)] for p in range(HEAD_PAIRS)]
    return jnp.asarray(tbl, F32)


def _lane_lo(rows):
    return lax.broadcasted_iota(jnp.int32, (rows, 128), 1) < 64


def _rows(start, dil):
    return pl.ds(start, BLK, stride=dil) if dil > 1 else pl.ds(start, BLK)


def _attn_specs(group, t):
    _, dil = ATTN_GROUPS[group]
    nblk = SPAN_BLOCKS[dil]
    span, edge = BLK * dil * nblk, BLK * dil
    per = span // edge

    def cur(which):
        off = 0 if which is None else which * QKV_LANE_BLOCKS + group * HEAD_PAIRS
        return pl.BlockSpec((span, 128), lambda s, p, *_: (s, off + p))

    def before(which):
        off = which * QKV_LANE_BLOCKS + group * HEAD_PAIRS
        return pl.BlockSpec((edge, 128), lambda s, p, *_: (jnp.maximum(s * per - 1, 0), off + p))

    def after(which):
        off = 0 if which is None else which * QKV_LANE_BLOCKS + group * HEAD_PAIRS
        return pl.BlockSpec((edge, 128), lambda s, p, *_: (jnp.minimum((s + 1) * per, t // edge - 1), off + p))

    slopes = pl.BlockSpec((HEAD_PAIRS, 128), lambda s, p, *_: (0, 0))
    return dil, nblk, span, cur, before, after, slopes


def _attn_fwd(qkv, group, name):
    t = qkv.shape[0]
    dil, nblk, span, cur, before, after, slopes = _attn_specs(group, t)

    def body(q_ref, k_ref, v_ref, kp_ref, vp_ref, sl_ref, o_ref, l_ref):
        first_span = pl.program_id(0) == 0
        sl = sl_ref[pl.ds(pl.program_id(1), 1), :]
        qi = lax.broadcasted_iota(jnp.int32, (BLK, 2 * BLK), 0)
        kj = lax.broadcasted_iota(jnp.int32, (BLK, 2 * BLK), 1)
        dist = qi + BLK - kj
        valid = (dist >= 0) & (dist <= BLK)
        valid_first = valid & ((kj >= BLK) | jnp.logical_not(first_span))
        distf = dist.astype(F32) * float(dil)
        lo_q, lo_k = _lane_lo(BLK), _lane_lo(2 * BLK)
        for r in range(dil):
            kp, vp = kp_ref[_rows(r, dil), :].astype(BF16), vp_ref[_rows(r, dil), :].astype(BF16)
            for b in range(nblk):
                rows = _rows(b * BLK * dil + r, dil)
                ok = valid_first if b == 0 else valid
                q2 = q_ref[rows, :].astype(BF16)
                kc, vc = k_ref[rows, :].astype(BF16), v_ref[rows, :].astype(BF16)
                k2 = jnp.concatenate([kp, kc], axis=0)
                v2 = jnp.concatenate([vp, vc], axis=0)
                kp, vp = kc, vc
                o2 = jnp.zeros((BLK, 128), F32)
                l2 = jnp.zeros((BLK, 128), F32)
                for s01 in range(2):
                    mq = lo_q if s01 == 0 else ~lo_q
                    mk = lo_k if s01 == 0 else ~lo_k
                    s = _dot_nt(jnp.where(mq, q2, jnp.zeros_like(q2)), k2) * ATTN_SCALE - sl[:, s01:s01 + 1] * distf
                    s = jnp.where(ok, s, -jnp.inf)
                    mx = jnp.max(s, axis=-1, keepdims=True)
                    p = jnp.exp(s - mx)
                    den = jnp.sum(p, axis=-1, keepdims=True)
                    o2 = o2 + _dot(p.astype(BF16), jnp.where(mk, v2, jnp.zeros_like(v2))) / den
                    l2 = jnp.where(mq, mx + jnp.log(den), l2)
                o_ref[rows, :] = o2
                l_ref[rows, :] = l2

    in_specs = [cur(0), cur(1), cur(2), before(1), before(2), slopes]
    return pl.pallas_call(body, grid=(t // span, HEAD_PAIRS), in_specs=in_specs, out_specs=[cur(None), cur(None)],
                          out_shape=[SDS((t, ATTN_W), F32), SDS((t, ATTN_W), F32)], name=name,
                          compiler_params=_params(("parallel", "parallel")))(qkv, qkv, qkv, qkv, qkv, _slope_table(group))


def _attn_bwd(qkv, do, lse, dvec, dqkv, group, name):
    t = qkv.shape[0]
    dil, nblk, span, cur, before, after, slopes = _attn_specs(group, t)
    nspan = t // span

    def body(q_ref, k_ref, v_ref, kp_ref, vp_ref, qn_ref, do_ref, l_ref, d_ref, don_ref, ln_ref, dn_ref, sl_ref, *rest):
        out_ref, dk_s, dv_s = rest[-3:]
        span_id, pair_id, which = pl.program_id(0), pl.program_id(1), pl.program_id(2)

        @pl.when(which == 0)
        def _():
            first_span = span_id == 0
            last_span = span_id == nspan - 1
            sl = sl_ref[pl.ds(pair_id, 1), :]
            qi = lax.broadcasted_iota(jnp.int32, (BLK, 2 * BLK), 0)
            kj = lax.broadcasted_iota(jnp.int32, (BLK, 2 * BLK), 1)
            dist = qi + BLK - kj
            valid = (dist >= 0) & (dist <= BLK)
            valid_first = valid & ((kj >= BLK) | jnp.logical_not(first_span))
            distf = dist.astype(F32) * float(dil)
            valid_next = (kj[:, :BLK] >= qi[:, :BLK]) & jnp.logical_not(last_span)
            distf_next = distf[:, :BLK]
            lo_q, lo_k = _lane_lo(BLK), _lane_lo(2 * BLK)
            for r in range(dil):
                kp, vp = kp_ref[_rows(r, dil), :].astype(BF16), vp_ref[_rows(r, dil), :].astype(BF16)
                held = None
                for b in range(nblk + 1):
                    last = b == nblk
                    rows = _rows(r if last else b * BLK * dil + r, dil)
                    qs, dos, ls, ds_ = (qn_ref, don_ref, ln_ref, dn_ref) if last else (q_ref, do_ref, l_ref, d_ref)
                    q2, do2 = qs[rows, :].astype(BF16), dos[rows, :].astype(BF16)
                    l2, d2 = ls[rows, :], ds_[rows, :]
                    if last:
                        k2, v2, ok, bias = kp, vp, valid_next, distf_next
                    else:
                        kc, vc = k_ref[rows, :].astype(BF16), v_ref[rows, :].astype(BF16)
                        k2, v2 = jnp.concatenate([kp, kc], axis=0), jnp.concatenate([vp, vc], axis=0)
                        kp, vp = kc, vc
                        ok, bias = (valid_first if b == 0 else valid), distf
                    dq2 = jnp.zeros((BLK, 128), F32)
                    dk2 = jnp.zeros((k2.shape[0], 128), F32)
                    dv2 = jnp.zeros((k2.shape[0], 128), F32)
                    for s01 in range(2):
                        mq = lo_q if s01 == 0 else ~lo_q
                        col = slice(64 * s01, 64 * s01 + 1)
                        qm = jnp.where(mq, q2, jnp.zeros_like(q2))
                        dom = jnp.where(mq, do2, jnp.zeros_like(do2))
                        s = _dot_nt(qm, k2) * ATTN_SCALE - sl[:, s01:s01 + 1] * bias
                        p = jnp.exp(jnp.where(ok, s - l2[:, col], -jnp.inf))
                        ds = (p * (_dot_nt(dom, v2) - d2[:, col])).astype(BF16)
                        if not last:
                            mk = lo_k if s01 == 0 else ~lo_k
                            dq2 = dq2 + _dot(ds, jnp.where(mk, k2, jnp.zeros_like(k2))) * ATTN_SCALE
                        dk2 = dk2 + _dot_tn(ds, qm) * ATTN_SCALE
                        dv2 = dv2 + _dot_tn(p.astype(BF16), dom)
                    if held is not None:
                        dk_s[held[0], :] = held[1] + dk2[:BLK]
                        dv_s[held[0], :] = held[2] + dv2[:BLK]
                    if not last:
                        out_ref[rows, :] = dq2
                        held = (rows, dk2[BLK:], dv2[BLK:])

        @pl.when(which == 1)
        def _():
            out_ref[...] = dk_s[...]

        @pl.when(which == 2)
        def _():
            out_ref[...] = dv_s[...]

    off = group * HEAD_PAIRS
    out_spec = pl.BlockSpec((span, 128), lambda s, p, w: (s, w * QKV_LANE_BLOCKS + off + p))
    in_specs = [cur(0), cur(1), cur(2), before(1), before(2), after(0), cur(None), cur(None), cur(None),
                after(None), after(None), after(None), slopes]
    args = [qkv, qkv, qkv, qkv, qkv, qkv, do, lse, dvec, do, lse, dvec, _slope_table(group)]
    aliases = {}
    if dqkv is not None:
        in_specs.append(pl.BlockSpec(memory_space=pl.ANY))
        args.append(dqkv)
        aliases = {len(args) - 1: 0}
    return pl.pallas_call(body, grid=(nspan, HEAD_PAIRS, 3), in_specs=in_specs, out_specs=out_spec,
                          out_shape=SDS((t, QKV_DIM), F32), input_output_aliases=aliases,
                          scratch_shapes=[pltpu.VMEM((span, 128), F32), pltpu.VMEM((span, 128), F32)], name=name,
                          compiler_params=_params(("parallel", "parallel", "arbitrary")))(*args)


def _combine_weights(l0, l1, l2):
    mx = jnp.maximum(jnp.maximum(l0, l1), l2)
    e0, e1, e2 = jnp.exp(l0 - mx), jnp.exp(l1 - mx), jnp.exp(l2 - mx)
    den = e0 + e1 + e2
    return e0 / den, e1 / den, e2 / den


def _combine_fwd(os_, ls_, name):
    t = os_[0].shape[0]

    def body(o0, o1, o2, l0, l1, l2, out):
        w0, w1, w2 = _combine_weights(l0[...], l1[...], l2[...])
        out[...] = (w0 * o0[...] + w1 * o1[...] + w2 * o2[...]).astype(BF16)

    row = pl.BlockSpec((ROW_TILE, ATTN_W), lambda i: (i, 0))
    return pl.pallas_call(body, grid=(t // ROW_TILE,), in_specs=[row] * 6, out_specs=row, out_shape=SDS((t, ATTN_W), BF16),
                          name=name, compiler_params=_params(("parallel",)))(*os_, *ls_)


def _combine_bwd(do, os_, ls_, name):
    t = do.shape[0]

    def body(do_ref, o0, o1, o2, l0, l1, l2, g0, g1, g2, d0, d1, d2):
        w0, w1, w2 = _combine_weights(l0[...], l1[...], l2[...])
        dov = do_ref[...]
        prod = dov * (w0 * o0[...] + w1 * o1[...] + w2 * o2[...])
        r = (lax.broadcasted_iota(jnp.int32, (3 * 128, 128), 0) % 128) // 64
        c = lax.broadcasted_iota(jnp.int32, (3 * 128, 128), 1) // 64
        same = (r == c).astype(BF16)
        tbar = jnp.concatenate([_dot(jnp.concatenate(_pieces(prod[:, 128 * k:128 * k + 128], 3), axis=1), same)
                                for k in range(HEAD_PAIRS)], axis=1)
        for w, g, d in ((w0, g0, d0), (w1, g1, d1), (w2, g2, d2)):
            g[...] = w * dov
            d[...] = w * tbar

    row = pl.BlockSpec((ROW_TILE, ATTN_W), lambda i: (i, 0))
    return pl.pallas_call(body, grid=(t // ROW_TILE,), in_specs=[row] * 7, out_specs=[row] * 6,
                          out_shape=[SDS((t, ATTN_W), F32)] * 6, name=name,
                          compiler_params=_params(("parallel",)))(do, *os_, *ls_)


def _peer(k):
    x, y, c = lax.axis_index("x"), lax.axis_index("y"), lax.axis_index("c")
    px = 1 - x if k & 4 else x
    py = 1 - y if k & 2 else y
    pc = 1 - c if k & 1 else c
    return (px, py, pc), 4 * px + 2 * py + pc


def _my_index():
    return 4 * lax.axis_index("x") + 2 * lax.axis_index("y") + lax.axis_index("c")


def _exchange_sems(n):
    return [pltpu.SemaphoreType.DMA((n * (NDEV - 1),)), pltpu.SemaphoreType.DMA((n * (NDEV - 1),)),
            pltpu.SemaphoreType.DMA((n,))]


def _exchange_copies(kind, ins, outs, send, recv, local, arrivals):
    me = _my_index()
    own, sent, arriving = [], [], []
    for i in range(len(ins)):
        own.append(pltpu.make_async_copy(ins[i] if kind == "gather" else ins[i].at[me], outs[i].at[me], local.at[i]))
        for k in range(1, NDEV):
            peer, pidx = _peer(k)
            s = i * (NDEV - 1) + k - 1
            src = ins[i] if kind == "gather" else ins[i].at[pidx]
            for dst, into in ((outs[i].at[me], sent), (outs[i].at[pidx], arriving)):
                if into is sent or arrivals:
                    into.append(pltpu.make_async_remote_copy(src_ref=src, dst_ref=dst, send_sem=send.at[s],
                                                             recv_sem=recv.at[s], device_id=peer, device_id_type=MESH))
    return own, sent, arriving


def _exchange_start(kind, ins, outs, send, recv, local):
    own, sent, _ = _exchange_copies(kind, ins, outs, send, recv, local, arrivals=False)
    for cp in own + sent:
        cp.start()


def _exchange_wait(kind, ins, outs, send, recv, local):
    own, sent, arriving = _exchange_copies(kind, ins, outs, send, recv, local, arrivals=True)
    for cp in sent:
        cp.wait_send()
    for cp in arriving:
        cp.wait_recv()
    for cp in own:
        cp.wait()


def _exchange(kind, tensors, name):
    n = len(tensors)

    def body(*refs):
        _exchange_start(kind, refs[:n], refs[n:2 * n], *refs[2 * n:])
        _exchange_wait(kind, refs[:n], refs[n:2 * n], *refs[2 * n:])

    hbm = pl.BlockSpec(memory_space=pl.ANY)
    shapes = [SDS(((NDEV,) + t.shape) if kind == "gather" else t.shape, t.dtype) for t in tensors]
    return pl.pallas_call(body, in_specs=[hbm] * n, out_specs=[hbm] * n, out_shape=shapes,
                          scratch_shapes=_exchange_sems(n), name=name)(*tensors)


def _all_reduce_small(v, name):
    rows = v.shape[0]

    def body(v_ref, out_ref, land, send, recv):
        me = _my_index()
        land[me] = v_ref[...]
        remote = []
        for k in range(1, NDEV):
            peer, _ = _peer(k)
            cp = pltpu.make_async_remote_copy(src_ref=v_ref, dst_ref=land.at[me], send_sem=send.at[k - 1],
                                              recv_sem=recv.at[k - 1], device_id=peer, device_id_type=MESH)
            cp.start()
            remote.append(cp)
        for cp in remote:
            cp.wait_send()
        for k in range(1, NDEV):
            peer, pidx = _peer(k)
            pltpu.make_async_remote_copy(src_ref=v_ref, dst_ref=land.at[pidx], send_sem=send.at[k - 1],
                                         recv_sem=recv.at[k - 1], device_id=peer, device_id_type=MESH).wait_recv()
        total = land[0]
        for d in range(1, NDEV):
            total = total + land[d]
        out_ref[...] = total

    vm = pl.BlockSpec(memory_space=pltpu.VMEM)
    return pl.pallas_call(
        body, in_specs=[vm], out_specs=vm, out_shape=SDS((rows, 128), F32),
        scratch_shapes=[pltpu.VMEM((NDEV, rows, 128), F32), pltpu.SemaphoreType.DMA((NDEV - 1,)),
                        pltpu.SemaphoreType.DMA((NDEV - 1,))],
        name=name)(v)


def _adamw_math(w, g, m, v):
    m = ADAM_B1 * m + (1.0 - ADAM_B1) * g
    v = ADAM_B2 * v + (1.0 - ADAM_B2) * (g * g)
    m_hat = m / (1.0 - ADAM_B1 ** ADAM_STEP)
    v_hat = v / (1.0 - ADAM_B2 ** ADAM_STEP)
    delta = -ADAM_LR * (m_hat / (jnp.sqrt(v_hat) + ADAM_EPS) + ADAM_WD * w)
    return delta, m, v


def _row_tile(rows, cols):
    tr = rows
    while tr * cols * 4 > (1 << 20) and tr % 16 == 0:
        tr //= 2
    return tr


def _adamw(g, w, m, v, name):
    rows, cols = w.shape
    tr = _row_tile(rows, cols)

    def body(g_ref, w_ref, m_ref, v_ref, d_out, m_out, v_out):
        d, mn, vn = _adamw_math(w_ref[...], g_ref[...], m_ref[...], v_ref[...])
        d_out[...] = d
        m_out[...] = mn
        v_out[...] = vn

    sp = pl.BlockSpec((tr, cols), lambda i: (i, 0))
    return pl.pallas_call(body, grid=(rows // tr,), in_specs=[sp] * 4, out_specs=[sp] * 3,
                          out_shape=[SDS((rows, cols), F32)] * 3, name=name, compiler_params=_params(("parallel",)))(g, w, m, v)


def _reduce_adamw(parts, w, m, v, name):
    rows, cols = w.shape
    tr = _row_tile(rows, cols)

    def body(p_ref, w_ref, m_ref, v_ref, g_out, d_out, m_out, v_out):
        g = p_ref[0].astype(F32)
        for d in range(1, NDEV):
            g = g + p_ref[d].astype(F32)
        g_out[...] = g
        dl, mn, vn = _adamw_math(w_ref[...], g, m_ref[...], v_ref[...])
        d_out[...] = dl
        m_out[...] = mn
        v_out[...] = vn

    sp = pl.BlockSpec((tr, cols), lambda i: (i, 0))
    psp = pl.BlockSpec((NDEV, tr, cols), lambda i: (0, i, 0))
    return pl.pallas_call(body, grid=(rows // tr,), in_specs=[psp, sp, sp, sp], out_specs=[sp] * 4,
                          out_shape=[SDS((rows, cols), F32)] * 4, name=name, compiler_params=_params(("parallel",)))(parts, w, m, v)


def _pack(items):
    rows = []
    for a in items:
        a = a.reshape(-1).astype(F32)
        pad = (-a.shape[0]) % 128
        rows.append(jnp.pad(a, (0, pad)).reshape(-1, 128))
    out = jnp.concatenate(rows, axis=0)
    return jnp.pad(out, ((0, (-out.shape[0]) % 8), (0, 0)))


def _unpack(packed, shapes):
    out, r = [], 0
    for shp in shapes:
        n = math.prod(shp)
        nr = -(-n // 128)
        out.append(packed[r:r + nr].reshape(-1)[:n].reshape(shp))
        r += nr
    return out


def _ident(t):
    return (t,)


def _add(t, res):
    return (t + res,)


def kernel(x, norm_mix, norm_mlp, ssm_w_in, ssm_conv_w, ssm_conv_b, ssm_dt_bias, ssm_a_log, ssm_d, ssm_norm_w, ssm_w_out, attn_w_qkv, attn_w_o, mlp_w1, mlp_w2, final_norm, loss_target, m_norm_mix, m_norm_mlp, m_ssm_w_in, m_ssm_conv_w, m_ssm_conv_b, m_ssm_dt_bias, m_ssm_a_log, m_ssm_d, m_ssm_norm_w, m_ssm_w_out, m_attn_w_qkv, m_attn_w_o, m_mlp_w1, m_mlp_w2, m_final_norm, v_norm_mix, v_norm_mlp, v_ssm_w_in, v_ssm_conv_w, v_ssm_conv_b, v_ssm_dt_bias, v_ssm_a_log, v_ssm_d, v_ssm_norm_w, v_ssm_w_out, v_attn_w_qkv, v_attn_w_o, v_mlp_w1, v_mlp_w2, v_final_norm):
    t = x.shape[1]
    x0 = x.reshape(t, D_MODEL)
    tgt = loss_target.reshape(t, D_MODEL)
    me = _my_index()
    in_dim = D_INNER + CONV_DIM + SSM_HEADS
    in_shard = in_dim // NDEV
    zx_dim = D_INNER + CONV_DIM

    s_out, s_qkv, s_o = ssm_w_out[0].astype(BF16), attn_w_qkv[0].astype(BF16), attn_w_o[0].astype(BF16)
    s_w1, s_w2 = mlp_w1.astype(BF16), mlp_w2.astype(BF16)
    g_in, g_cw = _exchange("gather", [ssm_w_in[0].astype(BF16), ssm_conv_w[0]], "gather_in_proj")
    w_in = jnp.transpose(g_in, (1, 0, 2)).reshape(D_MODEL, in_dim)
    w_z, w_x = w_in[:, :D_INNER], w_in[:, D_INNER:zx_dim]
    w_dt = jnp.pad(w_in[:, zx_dim:], ((0, 0), (0, 128 - SSM_HEADS)))
    conv_w = jnp.transpose(g_cw, (1, 0, 2)).reshape(CONV_WIDTH, CONV_DIM)
    g_w1, g_w2 = [None, None], [None, None]

    def lanes(p):
        return jnp.pad(p.reshape(SSM_GROUPS, 1, 8), ((0, 0), (0, 0), (0, 120)))

    def rows(p):
        return jnp.broadcast_to(p.reshape(SSM_HEADS, 1), (SSM_HEADS, 128))

    bias_g, bias_r = lanes(ssm_dt_bias[0]), rows(ssm_dt_bias[0])
    alog_g, alog_r = lanes(ssm_a_log[0]), rows(ssm_a_log[0])
    d_exp = jnp.repeat(ssm_d[0], SSM_HEAD_DIM).reshape(1, D_INNER)
    norm_w = ssm_norm_w

    def relu2(tot):
        r = jnp.maximum(tot, 0.0)
        return r, r * r

    def mlp_fwd(xin, layer, tag, down_carry):
        h = _rmsnorm_fwd(xin, norm_mlp[layer:layer + 1], f"norm_mlp{tag}")
        r, a, w2 = _mm_fwd(h, g_w1[layer], "cols", D_MODEL, D_FF, epi=relu2, outs=(BF16, BF16),
                           carry=("gather", [s_w2[layer]]), name=f"mlp_up{tag}")
        g_w2[layer] = w2.reshape(D_FF, D_MODEL)
        xout, *got = _mm_fwd(a, g_w2[layer], "plain", D_FF, D_MODEL, epi=_add, outs=(F32,), extras=(xin,),
                             carry=("gather", down_carry) if down_carry else None, name=f"mlp_down{tag}")
        return h, r, a, xout, got

    h0 = _rmsnorm_fwd(x0, norm_mix[0:1], "norm_mix0")
    z, g_out = _mm_fwd(h0, w_z, "plain", D_MODEL, D_INNER, epi=_ident, outs=(F32,), carry=("gather", [s_out]), name="ssm_in_z")
    xpre, g_w1[0] = _mm_fwd(h0, w_x, "plain", D_MODEL, CONV_DIM, epi=_ident, outs=(F32,), carry=("gather", [s_w1[0]]),
                            name="ssm_in_x")
    dtp, = _mm_fwd(h0, w_dt, "plain", D_MODEL, 128, epi=_ident, outs=(F32,), name="ssm_in_dt")
    xbc = _conv_fwd(xpre, conv_w, ssm_conv_b, "conv_fwd")
    dtp64 = dtp[:, :SSM_HEADS]
    dtp_g = jnp.pad(jnp.transpose(dtp64.reshape(t, SSM_GROUPS, 8), (1, 0, 2)), ((0, 0), (0, 0), (0, 120)))
    dtp_t = jnp.transpose(dtp64)
    y_ssd, yn, states = _ssd_fwd(xbc, z, dtp_g, dtp_t, bias_g, bias_r, alog_g, alog_r, d_exp, norm_w, "ssd_fwd")
    g_out = g_out.reshape(D_INNER, D_MODEL)
    x1, g_o = _mm_fwd(yn, g_out, "plain", D_INNER, D_MODEL, epi=_add, outs=(F32,), extras=(x0,), carry=("gather", [s_o]),
                      name="ssm_out")
    h1, r1, a1, x2, (g_qkv,) = mlp_fwd(x1, 0, "0", [s_qkv])

    h2 = _rmsnorm_fwd(x2, norm_mix[1:2], "norm_mix1")
    qkv, g_w1[1] = _mm_fwd(h2, g_qkv, "cols", D_MODEL, QKV_DIM, epi=_ident, outs=(F32,), tn=QKV_DIM // NDEV,
                           carry=("gather", [s_w1[1]]), name="attn_qkv")
    att = [_attn_fwd(qkv, g, f"attn_fwd{g}") for g in range(3)]
    os_, ls_ = [a[0] for a in att], [a[1] for a in att]
    o_mix = _combine_fwd(os_, ls_, "attn_combine")
    x3, = _mm_fwd(o_mix, g_o, "cols", ATTN_W, D_MODEL, epi=_add, outs=(F32,), extras=(x2,), tn=D_MODEL // NDEV, name="attn_out")
    h3, r3, a3, x4, _ = mlp_fwd(x3, 1, "1", None)

    dx4, dx4b, loss_acc, d_final = _loss_head(x4, tgt, final_norm.reshape(1, D_MODEL), "loss_head")

    def mlp_bwd(xin, h, r, a, dxo, dxob, layer, tag):
        du, = _mm_dx(dxob, g_w2[layer], "plain", D_FF, D_MODEL, epi=lambda tot, rr: (tot * (2.0 * rr.astype(F32)),),
                     outs=(BF16,), extras=(r,), name=f"mlp_down_dx{tag}")
        dw2, = _mm_dw(a, dxob, "plain", name=f"mlp_down_dw{tag}")
        dw2 = dw2.reshape(NDEV, D_FF // NDEV, D_MODEL)
        dw1, p_dw2 = _mm_dw(h, du, "cols", carry=("scatter", [dw2]), name=f"mlp_up_dw{tag}")
        dh, p_dw1 = _mm_dx(du, g_w1[layer], "cols", D_MODEL, D_FF, epi=_ident, outs=(F32,), carry=("scatter", [dw1]),
                           name=f"mlp_up_dx{tag}")
        dxi, dxib, dg = _rmsnorm_bwd(xin, norm_mlp[layer:layer + 1], dh, dxo, f"norm_mlp_bwd{tag}")
        return dxi, dxib, dg, p_dw1, p_dw2

    dx3, dx3b, dg_mlp1, p_w1_1, p_w2_1 = mlp_bwd(x3, h3, r3, a3, dx4, dx4b, 1, "1")

    dw_o, = _mm_dw(o_mix, dx3b, "cols", tn=D_MODEL // NDEV, name="attn_out_dw")
    do, p_o = _mm_dx(dx3b, g_o, "cols", ATTN_W, D_MODEL, epi=_ident, outs=(F32,), tk=D_MODEL // NDEV,
                     carry=("scatter", [dw_o]), name="attn_out_dx")
    cb = _combine_bwd(do, os_, ls_, "attn_combine_bwd")
    dos, dvecs = cb[:3], cb[3:]
    dqkv = None
    for g in range(3):
        dqkv = _attn_bwd(qkv, dos[g], ls_[g], dvecs[g], dqkv, g, f"attn_bwd{g}")
    dw_qkv, = _mm_dw(h2, dqkv, "cols", tn=QKV_DIM // NDEV, tk=1024, name="attn_qkv_dw")
    dh2, p_qkv = _mm_dx(dqkv, g_qkv, "cols", D_MODEL, QKV_DIM, epi=_ident, outs=(F32,), tk=QKV_DIM // NDEV,
                        carry=("scatter", [dw_qkv]), name="attn_qkv_dx")
    dx2, dx2b, dg_mix1 = _rmsnorm_bwd(x2, norm_mix[1:2], dh2, dx3, "norm_mix_bwd1")

    dx1, dx1b, dg_mlp0, p_w1_0, p_w2_0 = mlp_bwd(x1, h1, r1, a1, dx2, dx2b, 0, "0")

    dw_out, = _mm_dw(yn, dx1b, "plain", name="ssm_out_dw")
    dw_out = dw_out.reshape(NDEV, D_INNER // NDEV, D_MODEL)
    dyn, p_out = _mm_dx(dx1b, g_out, "plain", D_INNER, D_MODEL, epi=_ident, outs=(F32,), carry=("scatter", [dw_out]),
                        name="ssm_out_dx")
    dxs, d_b, d_c, dz, ddtp_g, hsums, csums = _ssd_bwd(dyn, y_ssd, xbc, z, states, dtp_g, dtp_t, bias_g, bias_r,
                                                       alog_g, alog_r, d_exp, norm_w, "ssd_bwd")
    dxbc = jnp.concatenate([dxs, d_b, d_c], axis=1)
    dpre, conv_sums = _conv_bwd_pre(xpre, conv_w, ssm_conv_b, dxbc, "conv_bwd_pre")
    du = _conv_bwd_in(dpre, conv_w, "conv_bwd_in")
    ddtp = jnp.transpose(ddtp_g[:, :, :8], (1, 0, 2)).reshape(t, SSM_HEADS)
    ddtp = jnp.pad(ddtp, ((0, 0), (0, 128 - SSM_HEADS))).astype(BF16)
    dw_z, = _mm_dw(h0, dz, "plain", name="ssm_in_z_dw")
    dw_x, = _mm_dw(h0, du, "plain", name="ssm_in_x_dw")
    dw_dt, = _mm_dw(h0, ddtp, "plain", name="ssm_in_dt_dw")
    dw_in = jnp.concatenate([dw_z, dw_x, dw_dt[:, :SSM_HEADS]], axis=1)
    dw_in = jnp.transpose(dw_in.reshape(D_MODEL, NDEV, in_shard), (1, 0, 2))
    dh0, = _mm_dx(ddtp, w_dt, "plain", D_MODEL, 128, epi=_ident, outs=(F32,), name="ssm_in_dt_dx")
    dh0, = _mm_dx(dz, w_z, "plain", D_MODEL, D_INNER, epi=_add, outs=(F32,), extras=(dh0,), name="ssm_in_z_dx")
    dh0, p_in = _mm_dx(du, w_x, "plain", D_MODEL, CONV_DIM, epi=_add, outs=(F32,), extras=(dh0,),
                       carry=("scatter", [dw_in]), name="ssm_in_x_dx")
    dx0, _, dg_mix0 = _rmsnorm_bwd(x0, norm_mix[0:1], dh0, dx1, "norm_mix_bwd0")

    parts = [p_in, p_out, p_qkv, p_o, p_w1_0, p_w1_1, p_w2_0, p_w2_1]

    def own(w, mm, vv):
        shp = w.shape
        f = lambda a: a.reshape(-1, shp[-1])
        return f(w), f(mm), f(vv), shp

    big = {}
    for key, part, (w, mm, vv) in (
            ("ssm_w_in", parts[0], (ssm_w_in, m_ssm_w_in, v_ssm_w_in)),
            ("ssm_w_out", parts[1], (ssm_w_out, m_ssm_w_out, v_ssm_w_out)),
            ("attn_w_qkv", parts[2], (attn_w_qkv, m_attn_w_qkv, v_attn_w_qkv)),
            ("attn_w_o", parts[3], (attn_w_o, m_attn_w_o, v_attn_w_o))):
        w2, m2, v2, shp = own(w, mm, vv)
        res = _reduce_adamw(part, w2, m2, v2, f"adamw_{key}")
        big[key] = [r.reshape(shp) for r in res]
    for key, pa, pb, (w, mm, vv) in (("mlp_w1", parts[4], parts[5], (mlp_w1, m_mlp_w1, v_mlp_w1)),
                                     ("mlp_w2", parts[6], parts[7], (mlp_w2, m_mlp_w2, v_mlp_w2))):
        res = [_reduce_adamw(p, w[l], mm[l], vv[l], f"adamw_{key}_{l}") for l, p in enumerate((pa, pb))]
        big[key] = [jnp.stack([res[0][i], res[1][i]], axis=0) for i in range(4)]

    d_norm_mix = jnp.concatenate([dg_mix0, dg_mix1], axis=0)
    d_norm_mlp = jnp.concatenate([dg_mlp0, dg_mlp1], axis=0)
    d_conv_b = conv_sums[4:5]
    d_conv_w = conv_sums[0:4]
    head = hsums[:, :, :8]
    d_dt_bias, d_a_log, d_d = (head[:, k, :].reshape(1, SSM_HEADS) for k in range(3))
    d_ssm_norm = csums[0:1]
    small = [d_norm_mix, d_norm_mlp, d_conv_b, d_dt_bias, d_a_log, d_d, d_ssm_norm, d_final, d_conv_w, loss_acc[0:1, 0:1]]
    shapes = [a.shape for a in small]
    summed = _unpack(_all_reduce_small(_pack(small), "reduce_small"), shapes)
    g_conv_w_full = summed[8]
    loss = summed[9].reshape(())
    g_conv_w = lax.dynamic_slice(g_conv_w_full, (0, me * (CONV_DIM // NDEV)), (CONV_WIDTH, CONV_DIM // NDEV))

    small_names = ["norm_mix", "norm_mlp", "ssm_conv_b", "ssm_dt_bias", "ssm_a_log", "ssm_d", "ssm_norm_w", "final_norm"]
    small_w = [norm_mix, norm_mlp, ssm_conv_b, ssm_dt_bias, ssm_a_log, ssm_d, ssm_norm_w, final_norm, ssm_conv_w]
    small_m = [m_norm_mix, m_norm_mlp, m_ssm_conv_b, m_ssm_dt_bias, m_ssm_a_log, m_ssm_d, m_ssm_norm_w, m_final_norm, m_ssm_conv_w]
    small_v = [v_norm_mix, v_norm_mlp, v_ssm_conv_b, v_ssm_dt_bias, v_ssm_a_log, v_ssm_d, v_ssm_norm_w, v_final_norm, v_ssm_conv_w]
    small_g = [summed[i].reshape(small_w[i].shape) for i in range(8)] + [g_conv_w.reshape(ssm_conv_w.shape)]
    wshapes = [a.shape for a in small_w]
    sd, sm, sv = _adamw(_pack(small_g), _pack(small_w), _pack(small_m), _pack(small_v), "adamw_small")
    sd, sm, sv = _unpack(sd, wshapes), _unpack(sm, wshapes), _unpack(sv, wshapes)
    res = {n: (small_g[i], sd[i], sm[i], sv[i]) for i, n in enumerate(small_names + ["ssm_conv_w"])}
    for n in big:
        res[n] = tuple(big[n])

    order = ["norm_mix", "norm_mlp", "ssm_w_in", "ssm_conv_w", "ssm_conv_b", "ssm_dt_bias", "ssm_a_log", "ssm_d",
             "ssm_norm_w", "ssm_w_out", "attn_w_qkv", "attn_w_o", "mlp_w1", "mlp_w2", "final_norm"]
    outs = [loss, dx0.reshape(x.shape)]
    for kind in range(4):
        outs += [res[n][kind] for n in order]
    return tuple(outs)
```

```python
import math

import jax
import jax.numpy as jnp
from jax import lax
from jax.experimental import pallas as pl
from jax.experimental.pallas import tpu as pltpu

F32, BF16 = jnp.float32, jnp.bfloat16
SDS = jax.ShapeDtypeStruct
MESH = pl.DeviceIdType.MESH
HIGHEST = lax.Precision.HIGHEST

NDEV = 8
D_MODEL = 2048
D_INNER = 4096
SSM_HEADS = 64
SSM_HEAD_DIM = 64
SSM_GROUPS = 8
SSM_STATE = 128
CHUNK = 128
CONV_DIM = 6144
CONV_WIDTH = 4
GROUP_W = D_INNER // SSM_GROUPS
ATTN_GROUPS = ((128, 1), (512, 4), (2048, 16))
ATTN_W = 1024
QKV_DIM = 9216
D_FF = 8192
EPS = 1e-5
ADAM_LR, ADAM_B1, ADAM_B2, ADAM_EPS, ADAM_WD, ADAM_STEP = 0.001, 0.9, 0.999, 1e-08, 0.01, 10

VMEM_LIMIT = 48 * 1024 * 1024


def _params(sem):
    return pltpu.CompilerParams(dimension_semantics=sem, vmem_limit_bytes=VMEM_LIMIT)


def _sigmoid(v):
    return 1.0 / (1.0 + jnp.exp(-v))


def _softplus(v):
    return jnp.maximum(v, 0.0) + jnp.log1p(jnp.exp(-jnp.abs(v)))


def _wspec(layout, kw, nw, tr, tc, sel):
    if layout == "plain":
        return pl.BlockSpec((tr, tc), lambda *g: sel(*g))
    per = (nw // NDEV) // tc
    return pl.BlockSpec((None, tr, tc), lambda *g: (sel(*g)[1] // per, sel(*g)[0], sel(*g)[1] % per))


def _wshape(layout, kw, nw):
    return (kw, nw) if layout == "plain" else (NDEV, kw, nw // NDEV)


def _mm_call(name, grid, in_specs, out_specs, out_shape, dims, n_extra, epi, tm, tn, carry):
    nk = grid[2]
    kind, moved = carry if carry else (None, ())
    nc = len(moved)
    n_out = len(out_shape)

    def body(*refs):
        a_ref, b_ref = refs[0], refs[1]
        extra = refs[2:2 + n_extra]
        c_in = refs[2 + n_extra:2 + n_extra + nc]
        outs = refs[2 + n_extra + nc:2 + n_extra + nc + n_out]
        c_out = refs[2 + n_extra + nc + n_out:2 + n_extra + 2 * nc + n_out]
        acc = refs[2 + n_extra + 2 * nc + n_out]
        sems = refs[3 + n_extra + 2 * nc + n_out:]
        i, j, k = pl.program_id(0), pl.program_id(1), pl.program_id(2)
        if nc:
            @pl.when((i == 0) & (j == 0) & (k == 0))
            def _():
                _exchange_start(kind, c_in, c_out, *sems)

        d = lax.dot_general(a_ref[...].astype(BF16), b_ref[...].astype(BF16), (dims, ((), ())), preferred_element_type=F32)

        def finish(total):
            vals = epi(total, *[e[...] for e in extra])
            for o, v in zip(outs, vals):
                o[...] = v.astype(o.dtype)

        if nk == 1:
            finish(d)
        else:
            @pl.when(k == 0)
            def _():
                acc[...] = d

            @pl.when(jnp.logical_and(k > 0, k < nk - 1))
            def _():
                acc[...] += d

            @pl.when(k == nk - 1)
            def _():
                finish(acc[...] + d)

        if nc:
            @pl.when((i == grid[0] - 1) & (j == grid[1] - 1) & (k == nk - 1))
            def _():
                _exchange_wait(kind, c_in, c_out, *sems)

    hbm = pl.BlockSpec(memory_space=pl.ANY)
    scratch = [pltpu.VMEM((tm, tn), F32)] + (_exchange_sems(nc) if nc else [])
    c_shape = [SDS(((NDEV,) + t.shape) if kind == "gather" else t.shape, t.dtype) for t in moved]
    sem = ("arbitrary",) * 3 if nc else ("parallel", "parallel", "arbitrary")
    return pl.pallas_call(
        body, grid=grid, in_specs=in_specs + [hbm] * nc, out_specs=out_specs + [hbm] * nc,
        out_shape=out_shape + c_shape, scratch_shapes=scratch, name=name, compiler_params=_params(sem))


def _mm_fwd(a, w, layout, kw, nw, *, epi, outs, extras=(), tm=1024, tn=1024, tk=2048, carry=None, name):
    m = a.shape[0]
    tm, tn, tk = min(tm, m), min(tn, nw), min(tk, kw)
    grid = (m // tm, nw // tn, kw // tk)
    o_spec = pl.BlockSpec((tm, tn), lambda i, j, k: (i, j))
    in_specs = [pl.BlockSpec((tm, tk), lambda i, j, k: (i, k)), _wspec(layout, kw, nw, tk, tn, lambda i, j, k: (k, j))]
    in_specs += [o_spec] * len(extras)
    call = _mm_call(name, grid, in_specs, [o_spec] * len(outs), [SDS((m, nw), dt) for dt in outs], ((1,), (0,)),
                    len(extras), epi, tm, tn, carry)
    return call(a, w, *extras, *(carry[1] if carry else ()))


def _mm_dx(g, w, layout, kw, nw, *, epi, outs, extras=(), tm=1024, tn=1024, tk=2048, carry=None, name):
    m = g.shape[0]
    tm, tn, tk = min(tm, m), min(tn, kw), min(tk, nw if layout == "plain" else nw // NDEV)
    grid = (m // tm, kw // tn, nw // tk)
    o_spec = pl.BlockSpec((tm, tn), lambda i, j, k: (i, j))
    in_specs = [pl.BlockSpec((tm, tk), lambda i, j, k: (i, k)), _wspec(layout, kw, nw, tn, tk, lambda i, j, k: (j, k))]
    in_specs += [o_spec] * len(extras)
    call = _mm_call(name, grid, in_specs, [o_spec] * len(outs), [SDS((m, kw), dt) for dt in outs], ((1,), (1,)),
                    len(extras), epi, tm, tn, carry)
    return call(g, w, *extras, *(carry[1] if carry else ()))


def _mm_dw(a, g, layout, *, tm=1024, tn=1024, tk=2048, carry=None, name):
    m, kw = a.shape
    nw = g.shape[1]
    tm, tn, tk = min(tm, kw), min(tn, nw if layout == "plain" else nw // NDEV), min(tk, m)
    grid = (kw // tm, nw // tn, m // tk)
    in_specs = [pl.BlockSpec((tk, tm), lambda i, j, k: (k, i)), pl.BlockSpec((tk, tn), lambda i, j, k: (k, j))]
    o_spec = _wspec(layout, kw, nw, tm, tn, lambda i, j, k: (i, j))
    call = _mm_call(name, grid, in_specs, [o_spec], [SDS(_wshape(layout, kw, nw), BF16)], ((0,), (0,)),
                    0, lambda t: (t,), tm, tn, carry)
    return call(a, g, *(carry[1] if carry else ()))


ROW_TILE = 256


def _rmsnorm_fwd(x, g, name):
    t, d = x.shape

    def body(x_ref, g_ref, h_ref):
        xv = x_ref[...]
        r = lax.rsqrt(jnp.mean(xv * xv, axis=-1, keepdims=True) + EPS)
        h_ref[...] = (xv * r * g_ref[...]).astype(BF16)

    row = pl.BlockSpec((ROW_TILE, d), lambda i: (i, 0))
    vec = pl.BlockSpec((1, d), lambda i: (0, 0))
    return pl.pallas_call(body, grid=(t // ROW_TILE,), in_specs=[row, vec], out_specs=row, out_shape=SDS((t, d), BF16),
                          name=name, compiler_params=_params(("parallel",)))(x, g)


def _rmsnorm_bwd(x, g, dh, dres, name):
    t, d = x.shape

    def body(x_ref, g_ref, dh_ref, dres_ref, dx_ref, dxb_ref, dg_ref):
        xv = x_ref[...]
        r = lax.rsqrt(jnp.mean(xv * xv, axis=-1, keepdims=True) + EPS)
        xh = xv * r
        dhv = dh_ref[...]
        gd = dhv * g_ref[...]
        dx = dres_ref[...] + r * (gd - xh * jnp.mean(gd * xh, axis=-1, keepdims=True))
        dx_ref[...] = dx
        dxb_ref[...] = dx.astype(BF16)
        part = jnp.sum(dhv * xh, axis=0, keepdims=True)

        @pl.when(pl.program_id(0) == 0)
        def _():
            dg_ref[...] = part

        @pl.when(pl.program_id(0) > 0)
        def _():
            dg_ref[...] += part

    row = pl.BlockSpec((ROW_TILE, d), lambda i: (i, 0))
    vec = pl.BlockSpec((1, d), lambda i: (0, 0))
    return pl.pallas_call(body, grid=(t // ROW_TILE,), in_specs=[row, vec, row, row], out_specs=[row, row, vec],
                          out_shape=[SDS((t, d), F32), SDS((t, d), BF16), SDS((1, d), F32)], name=name,
                          compiler_params=_params(("arbitrary",)))(x, g, dh, dres)


def _loss_head(x, tgt, g, name):
    t, d = x.shape

    def body(x_ref, t_ref, g_ref, dx_ref, dxb_ref, loss_ref, dg_ref):
        xv = x_ref[...]
        r = lax.rsqrt(jnp.mean(xv * xv, axis=-1, keepdims=True) + EPS)
        xh = xv * r
        gv = g_ref[...]
        err = xh * gv - t_ref[...]
        part_loss = 0.5 * jnp.sum(jnp.mean(err * err, axis=-1, keepdims=True), axis=0, keepdims=True)
        dy = err * (1.0 / d)
        gd = dy * gv
        dx = r * (gd - xh * jnp.mean(gd * xh, axis=-1, keepdims=True))
        dx_ref[...] = dx
        dxb_ref[...] = dx.astype(BF16)
        part_g = jnp.sum(dy * xh, axis=0, keepdims=True)
        part_l = jnp.broadcast_to(part_loss, (8, 128))

        @pl.when(pl.program_id(0) == 0)
        def _():
            dg_ref[...] = part_g
            loss_ref[...] = part_l

        @pl.when(pl.program_id(0) > 0)
        def _():
            dg_ref[...] += part_g
            loss_ref[...] += part_l

    row = pl.BlockSpec((ROW_TILE, d), lambda i: (i, 0))
    vec = pl.BlockSpec((1, d), lambda i: (0, 0))
    sc = pl.BlockSpec((8, 128), lambda i: (0, 0))
    return pl.pallas_call(body, grid=(t // ROW_TILE,), in_specs=[row, row, vec], out_specs=[row, row, sc, vec],
                          out_shape=[SDS((t, d), F32), SDS((t, d), BF16), SDS((8, 128), F32), SDS((1, d), F32)], name=name,
                          compiler_params=_params(("arbitrary",)))(x, tgt, g)


CONV_ROWS = 256
CONV_COLS = 2048


def _shift_down(cur, prev8, k):
    sh = pltpu.roll(cur, k, axis=0)
    ph = pltpu.roll(prev8, k, axis=0)
    rid = lax.broadcasted_iota(jnp.int32, ph.shape, 0)
    head = jnp.where(rid < k, ph, sh[0:8])
    return jnp.concatenate([head, sh[8:]], axis=0)


def _shift_up(cur, next8, k):
    n = cur.shape[0]
    sh = pltpu.roll(cur, n - k, axis=0)
    nh = pltpu.roll(next8, 8 - k, axis=0)
    rid = lax.broadcasted_iota(jnp.int32, nh.shape, 0)
    tail = jnp.where(rid >= 8 - k, nh, sh[n - 8:])
    return jnp.concatenate([sh[:n - 8], tail], axis=0)


def _conv_pre(cur, prev8, w, b):
    acc = w[3:4, :] * cur + b
    for k in range(1, CONV_WIDTH):
        acc = acc + w[3 - k:4 - k, :] * _shift_down(cur, prev8, k)
    return acc


def _conv_fwd(u, w, b, name):
    t, c = u.shape
    per = CONV_ROWS // 8

    def body(u_ref, p_ref, w_ref, b_ref, o_ref):
        prev8 = jnp.where(pl.program_id(1) == 0, 0.0, p_ref[...])
        pre = _conv_pre(u_ref[...], prev8, w_ref[...], b_ref[...])
        o_ref[...] = pre * _sigmoid(pre)

    cur = pl.BlockSpec((CONV_ROWS, CONV_COLS), lambda j, i: (i, j))
    prev = pl.BlockSpec((8, CONV_COLS), lambda j, i: (jnp.maximum(i * per - 1, 0), j))
    wsp = pl.BlockSpec((CONV_WIDTH, CONV_COLS), lambda j, i: (0, j))
    bsp = pl.BlockSpec((1, CONV_COLS), lambda j, i: (0, j))
    return pl.pallas_call(body, grid=(c // CONV_COLS, t // CONV_ROWS), in_specs=[cur, prev, wsp, bsp], out_specs=cur,
                          out_shape=SDS((t, c), F32), name=name, compiler_params=_params(("parallel", "parallel")))(u, u, w, b)


def _conv_bwd(u, w, b, dout, name):
    t, c = u.shape
    per = CONV_ROWS // 8
    last = t // CONV_ROWS - 1

    def dsilu(pre, d):
        s = _sigmoid(pre)
        return d * (s * (1.0 + pre * (1.0 - s)))

    def body(u_ref, p_ref, n_ref, w_ref, b_ref, d_ref, dn_ref, du_ref, dw_ref):
        i = pl.program_id(1)
        cur = u_ref[...]
        wv, bv = w_ref[...], b_ref[...]
        prev8 = jnp.where(i == 0, 0.0, p_ref[...])
        dpre = dsilu(_conv_pre(cur, prev8, wv, bv), d_ref[...])
        nxt, tail = n_ref[...], cur[CONV_ROWS - 8:]
        rid = lax.broadcasted_iota(jnp.int32, nxt.shape, 0)
        pre_n = wv[3:4, :] * nxt + bv
        for k in range(1, CONV_WIDTH):
            pre_n = pre_n + wv[3 - k:4 - k, :] * jnp.where(rid < k, pltpu.roll(tail, k, axis=0), pltpu.roll(nxt, k, axis=0))
        dpre_n = jnp.where(i == last, 0.0, dsilu(pre_n, dn_ref[...]))
        du = wv[3:4, :] * dpre
        for k in range(1, CONV_WIDTH):
            du = du + wv[3 - k:4 - k, :] * _shift_up(dpre, dpre_n, k)
        du_ref[...] = du.astype(BF16)
        rows = [jnp.sum(dpre * _shift_down(cur, prev8, 3 - k), axis=0, keepdims=True) for k in range(3)]
        rows.append(jnp.sum(dpre * cur, axis=0, keepdims=True))
        rows.append(jnp.sum(dpre, axis=0, keepdims=True))
        part = jnp.concatenate(rows + [jnp.zeros((3, cur.shape[1]), F32)], axis=0)

        @pl.when(i == 0)
        def _():
            dw_ref[...] = part

        @pl.when(i > 0)
        def _():
            dw_ref[...] += part

    cur = pl.BlockSpec((CONV_ROWS, CONV_COLS), lambda j, i: (i, j))
    prev = pl.BlockSpec((8, CONV_COLS), lambda j, i: (jnp.maximum(i * per - 1, 0), j))
    nxt = pl.BlockSpec((8, CONV_COLS), lambda j, i: (jnp.minimum((i + 1) * per, t // 8 - 1), j))
    wsp = pl.BlockSpec((CONV_WIDTH, CONV_COLS), lambda j, i: (0, j))
    bsp = pl.BlockSpec((1, CONV_COLS), lambda j, i: (0, j))
    acc = pl.BlockSpec((8, CONV_COLS), lambda j, i: (0, j))
    return pl.pallas_call(body, grid=(c // CONV_COLS, t // CONV_ROWS), in_specs=[cur, prev, nxt, wsp, bsp, cur, nxt],
                          out_specs=[cur, acc], out_shape=[SDS((t, c), BF16), SDS((8, c), F32)], name=name,
                          compiler_params=_params(("parallel", "arbitrary")))(u, u, u, w, b, dout, dout)


def _pieces(v, n):
    out, rest = [], v
    for _ in range(n):
        p = rest.astype(BF16)
        out.append(p)
        rest = rest - p.astype(F32)
    return out


def _head_expand(v):
    r = lax.broadcasted_iota(jnp.int32, (3 * 128, GROUP_W), 0) % 128
    c = lax.broadcasted_iota(jnp.int32, (3 * 128, GROUP_W), 1)
    return _dot(jnp.concatenate(_pieces(v, 3), axis=1), (c // SSM_HEAD_DIM == r).astype(BF16))


def _head_sum(vs):
    r = lax.broadcasted_iota(jnp.int32, (2 * GROUP_W, 128), 0) % GROUP_W
    c = lax.broadcasted_iota(jnp.int32, (2 * GROUP_W, 128), 1)
    stacked = jnp.concatenate([jnp.concatenate(_pieces(v, 2), axis=1) for v in vs], axis=0)
    out = _dot(stacked, (r // SSM_HEAD_DIM == c).astype(BF16))
    res, at = [], 0
    for v in vs:
        res.append(out[at:at + v.shape[0]])
        at += v.shape[0]
    return res


def _tri_left(tri, v):
    r = _dot(tri.astype(BF16), jnp.concatenate(_pieces(v, 3), axis=1))
    return r[:, 0:128] + r[:, 128:256] + r[:, 256:384]


def _dot(a, b):
    return jnp.dot(a, b, preferred_element_type=F32)


def _dot_nt(a, b):
    return lax.dot_general(a, b, (((1,), (1,)), ((), ())), preferred_element_type=F32)


def _dot_tn(a, b):
    return lax.dot_general(a, b, (((0,), (0,)), ((), ())), preferred_element_type=F32)


def _ssd_common(dtp, dtpt, bias, biasr, alog, alogr):
    li = lax.broadcasted_iota(jnp.int32, (CHUNK, CHUNK), 0)
    si = lax.broadcasted_iota(jnp.int32, (CHUNK, CHUNK), 1)
    lower, upper = (li >= si), (li <= si)
    dt = _softplus(dtp + bias)
    a_neg = -jnp.exp(alog)
    cs = _tri_left(lower, dt * a_neg)
    dtr = _softplus(dtpt + biasr)
    ar = jnp.concatenate([dtr * (-jnp.exp(alogr)), jnp.zeros((8, CHUNK), F32)], axis=0)
    r3 = lax.broadcasted_iota(jnp.int32, (3 * CHUNK, CHUNK), 0) % CHUNK
    c3 = lax.broadcasted_iota(jnp.int32, (3 * CHUNK, CHUNK), 1)
    csr = _dot(jnp.concatenate(_pieces(ar, 3), axis=1), (r3 <= c3).astype(BF16))[0:8]
    return lower, upper, dt, a_neg, cs, csr


def _head_masked_rows(v):
    hl = lax.broadcasted_iota(jnp.int32, v.shape, 1) // SSM_HEAD_DIM
    return jnp.concatenate([jnp.where(hl == j, v, jnp.zeros_like(v)) for j in range(8)], axis=0)


def _ssd_fwd(xbc, z, dtp_g, dtp_t, bias_g, bias_r, alog_g, alog_r, d_exp, norm_w, name):
    t = xbc.shape[0]
    nc = t // CHUNK

    def body(xs_ref, b_ref, c_ref, z_ref, dtp_ref, dtpt_ref, bias_ref, biasr_ref, alog_ref, alogr_ref, dexp_ref, nw_ref,
             y_ref, yn_ref, st_ref, state):
        c = pl.program_id(1)

        @pl.when(c == 0)
        def _():
            state[...] = jnp.zeros_like(state)

        lower, _, dt, a_neg, cs, csr = _ssd_common(dtp_ref[...], dtpt_ref[...], bias_ref[...], biasr_ref[...],
                                                alog_ref[...], alogr_ref[...])
        cs_e = _head_expand(cs)
        dt_e = _head_expand(dt)
        xs = xs_ref[...]
        xdt = xs * dt_e
        bm = b_ref[...]
        cm = c_ref[...]
        bmb, cmb = bm.astype(BF16), cm.astype(BF16)
        cb = _dot_nt(cmb, bmb)
        ms = []
        for j in range(8):
            dlt = cs_e[:, SSM_HEAD_DIM * j:SSM_HEAD_DIM * j + 1] - csr[j:j + 1, :]
            ms.append((cb * jnp.exp(jnp.where(lower, dlt, -jnp.inf))).astype(BF16))
        y = _dot(jnp.concatenate(ms, axis=1), _head_masked_rows(xdt.astype(BF16)))
        st_in = state[...]
        st_ref[...] = st_in
        y = y + jnp.exp(cs_e) * _dot(cmb, st_in.astype(BF16))
        cs_last = cs_e[CHUNK - 1:CHUNK, :]
        xdtd = (xdt * jnp.exp(cs_last - cs_e)).astype(BF16)
        state[...] = jnp.exp(cs_last) * st_in + _dot(bm.T.astype(BF16), xdtd)
        y_ref[...] = y
        zz = z_ref[...]
        y2 = (y + dexp_ref[...] * xs) * (zz * _sigmoid(zz))
        r = lax.rsqrt(jnp.mean(y2 * y2, axis=-1, keepdims=True) + EPS)
        yn_ref[...] = (y2 * r * nw_ref[...]).astype(BF16)

    gw = pl.BlockSpec((CHUNK, GROUP_W), lambda g, c: (c, g))
    in_specs = [
        gw,
        pl.BlockSpec((CHUNK, SSM_STATE), lambda g, c: (c, D_INNER // SSM_STATE + g)),
        pl.BlockSpec((CHUNK, SSM_STATE), lambda g, c: (c, D_INNER // SSM_STATE + SSM_GROUPS + g)),
        gw,
        pl.BlockSpec((None, CHUNK, 128), lambda g, c: (g, c, 0)),
        pl.BlockSpec((8, CHUNK), lambda g, c: (g, c)),
        pl.BlockSpec((None, 1, 128), lambda g, c: (g, 0, 0)),
        pl.BlockSpec((8, 128), lambda g, c: (g, 0)),
        pl.BlockSpec((None, 1, 128), lambda g, c: (g, 0, 0)),
        pl.BlockSpec((8, 128), lambda g, c: (g, 0)),
        pl.BlockSpec((1, GROUP_W), lambda g, c: (0, g)),
        pl.BlockSpec((1, GROUP_W), lambda g, c: (0, g)),
    ]
    out_specs = [gw, gw, pl.BlockSpec((None, SSM_STATE, GROUP_W), lambda g, c: (c, 0, g))]
    out_shape = [SDS((t, D_INNER), F32), SDS((t, D_INNER), BF16), SDS((nc, SSM_STATE, D_INNER), F32)]
    return pl.pallas_call(body, grid=(SSM_GROUPS, nc), in_specs=in_specs, out_specs=out_specs, out_shape=out_shape,
                          scratch_shapes=[pltpu.VMEM((SSM_STATE, GROUP_W), F32)], name=name,
                          compiler_params=_params(("parallel", "arbitrary")))(
        xbc, xbc, xbc, z, dtp_g, dtp_t, bias_g, bias_r, alog_g, alog_r, d_exp, norm_w)


def _ssd_bwd(dyn, y, xbc, z, states, dtp_g, dtp_t, bias_g, bias_r, alog_g, alog_r, d_exp, norm_w, name):
    t = xbc.shape[0]
    nc = t // CHUNK

    def body(dyn_ref, y_ref, xs_ref, b_ref, c_ref, z_ref, st_ref, dtp_ref, dtpt_ref, bias_ref, biasr_ref, alog_ref,
             alogr_ref, dexp_ref, nw_ref, dxs_ref, db_ref, dc_ref, dz_ref, ddt_ref, hsum_ref, csum_ref, dstate):
        step = pl.program_id(1)

        @pl.when(step == 0)
        def _():
            dstate[...] = jnp.zeros_like(dstate)

        dtp = dtp_ref[...]
        bias = bias_ref[...]
        lower, upper, dt, a_neg, cs, csr = _ssd_common(dtp, dtpt_ref[...], bias, biasr_ref[...], alog_ref[...],
                                                       alogr_ref[...])
        cs_e = _head_expand(cs)
        dt_e = _head_expand(dt)
        xs = xs_ref[...]
        xdt = xs * dt_e
        bm = b_ref[...]
        cm = c_ref[...]
        bmb, cmb = bm.astype(BF16), cm.astype(BF16)
        y = y_ref[...]
        dexp = dexp_ref[...]
        nw = nw_ref[...]

        zz = z_ref[...]
        sg = _sigmoid(zz)
        gate = zz * sg
        ytot = y + dexp * xs
        y2 = ytot * gate
        r = lax.rsqrt(jnp.mean(y2 * y2, axis=-1, keepdims=True) + EPS)
        dynv = dyn_ref[...]
        xh = y2 * r
        gn = dynv * nw
        dy2 = r * (gn - xh * jnp.mean(gn * xh, axis=-1, keepdims=True))
        dy = dy2 * gate
        dz_ref[...] = (dy2 * ytot * (sg * (1.0 + zz * (1.0 - sg)))).astype(BF16)
        csum_part = jnp.sum(dynv * xh, axis=0, keepdims=True)

        cb = _dot_nt(cmb, bmb)
        dyb = dy.astype(BF16)
        xdtb = xdt.astype(BF16)
        dym = _head_masked_rows(dyb)
        dm = _dot_nt(dym, xdtb)
        dmt = _dot_nt(_head_masked_rows(xdtb), dyb)
        lane = lax.broadcasted_iota(jnp.int32, (CHUNK, 128), 1)
        mts = []
        dcb = jnp.zeros((CHUNK, CHUNK), F32)
        dcs = jnp.zeros((CHUNK, 128), F32)
        for j in range(8):
            dlt = cs_e[:, SSM_HEAD_DIM * j:SSM_HEAD_DIM * j + 1] - csr[j:j + 1, :]
            lj = jnp.exp(jnp.where(lower, dlt, -jnp.inf))
            mj = cb * lj
            mjt = mj.T
            mts.append(mjt.astype(BF16))
            dmj = dm[CHUNK * j:CHUNK * (j + 1)]
            dcb = dcb + dmj * lj
            rows = jnp.sum(dmj * mj, axis=1, keepdims=True)
            cols = jnp.sum(dmt[CHUNK * j:CHUNK * (j + 1)] * mjt, axis=1, keepdims=True)
            dcs = dcs + jnp.where(lane == j, rows - cols, 0.0)
        dxdt = _dot(jnp.concatenate(mts, axis=1), dym)
        dst_out = dstate[...]
        dst_outb = dst_out.astype(BF16)
        st_in = st_ref[...]
        st_inb = st_in.astype(BF16)
        cs_last = cs_e[CHUNK - 1:CHUNK, :]
        decay = jnp.exp(cs_last - cs_e)
        e_last = jnp.exp(cs_last)
        gpart = decay * _dot(bmb, dst_outb)
        dxdt = dxdt + gpart
        dyw = (jnp.exp(cs_e) * dy).astype(BF16)
        dcbb = dcb.astype(BF16)
        dc_ref[...] = _dot_nt(dyw, st_inb) + _dot(dcbb, bmb)
        db_ref[...] = _dot_nt((xdt * decay).astype(BF16), dst_outb) + _dot(dcb.T.astype(BF16), cmb)
        dstate[...] = e_last * dst_out + _dot(cm.T.astype(BF16), dyw)
        y_off = jnp.exp(cs_e) * _dot(cmb, st_inb)
        xg = xdt * gpart
        vec = jnp.concatenate([jnp.sum(dy * xs, axis=0, keepdims=True),
                               jnp.sum(xg + dst_out * e_last * st_in, axis=0, keepdims=True),
                               jnp.zeros((14, GROUP_W), F32)], axis=0)
        s_cs, s_dt, s_vec = _head_sum([dy * y_off - xg, dxdt * xs, vec])
        d_skip = s_vec[0:1]
        ri = lax.broadcasted_iota(jnp.int32, (CHUNK, 128), 0)
        dcs = dcs + s_cs + jnp.where(ri == CHUNK - 1, s_vec[1:2], 0.0)
        da = _tri_left(upper, dcs)
        ddt = da * a_neg + s_dt
        dxs_ref[...] = dxdt * dt_e + dy * dexp
        ddtp = ddt * _sigmoid(dtp + bias)
        ddt_ref[...] = ddtp
        d_alog = jnp.sum(da * dt, axis=0, keepdims=True) * a_neg
        hpart = jnp.concatenate([jnp.sum(ddtp, axis=0, keepdims=True), d_alog, d_skip, jnp.zeros((5, 128), F32)], axis=0)
        cpart = jnp.concatenate([csum_part, jnp.zeros((7, GROUP_W), F32)], axis=0)

        @pl.when(step == 0)
        def _():
            hsum_ref[...] = hpart
            csum_ref[...] = cpart

        @pl.when(step > 0)
        def _():
            hsum_ref[...] += hpart
            csum_ref[...] += cpart

    rc = lambda c: nc - 1 - c
    gw = pl.BlockSpec((CHUNK, GROUP_W), lambda g, c: (rc(c), g))
    bsp = pl.BlockSpec((CHUNK, SSM_STATE), lambda g, c: (rc(c), D_INNER // SSM_STATE + g))
    csp = pl.BlockSpec((CHUNK, SSM_STATE), lambda g, c: (rc(c), D_INNER // SSM_STATE + SSM_GROUPS + g))
    in_specs = [
        gw, gw, gw, bsp, csp, gw,
        pl.BlockSpec((None, SSM_STATE, GROUP_W), lambda g, c: (rc(c), 0, g)),
        pl.BlockSpec((None, CHUNK, 128), lambda g, c: (g, rc(c), 0)),
        pl.BlockSpec((8, CHUNK), lambda g, c: (g, rc(c))),
        pl.BlockSpec((None, 1, 128), lambda g, c: (g, 0, 0)),
        pl.BlockSpec((8, 128), lambda g, c: (g, 0)),
        pl.BlockSpec((None, 1, 128), lambda g, c: (g, 0, 0)),
        pl.BlockSpec((8, 128), lambda g, c: (g, 0)),
        pl.BlockSpec((1, GROUP_W), lambda g, c: (0, g)),
        pl.BlockSpec((1, GROUP_W), lambda g, c: (0, g)),
    ]
    nsp = pl.BlockSpec((CHUNK, SSM_STATE), lambda g, c: (rc(c), g))
    out_specs = [gw, nsp, nsp, gw,
                 pl.BlockSpec((None, CHUNK, 128), lambda g, c: (g, rc(c), 0)),
                 pl.BlockSpec((None, 8, 128), lambda g, c: (g, 0, 0)),
                 pl.BlockSpec((8, GROUP_W), lambda g, c: (0, g))]
    gn = SSM_GROUPS * SSM_STATE
    out_shape = [SDS((t, D_INNER), F32), SDS((t, gn), F32), SDS((t, gn), F32), SDS((t, D_INNER), BF16),
                 SDS((SSM_GROUPS, t, 128), F32), SDS((SSM_GROUPS, 8, 128), F32), SDS((8, D_INNER), F32)]
    return pl.pallas_call(body, grid=(SSM_GROUPS, nc), in_specs=in_specs, out_specs=out_specs, out_shape=out_shape,
                          scratch_shapes=[pltpu.VMEM((SSM_STATE, GROUP_W), F32)], name=name,
                          compiler_params=_params(("parallel", "arbitrary")))(
        dyn, y, xbc, xbc, xbc, z, states, dtp_g, dtp_t, bias_g, bias_r, alog_g, alog_r, d_exp, norm_w)


BLK = 128
HEAD_PAIRS = ATTN_W // 128
ATTN_SCALE = 0.125
SPAN_BLOCKS = {1: 8, 4: 2, 16: 1}
QKV_LANE_BLOCKS = ATTN_W * len(ATTN_GROUPS) // 128


def _slope_table(group):
    n = len(ATTN_GROUPS) * 16
    tbl = [[2.0 ** (-8.0 * (16 * group + 2 * p + s + 1) / n) if s < 2 else 0.0 for s in range(128)] for p in range(HEAD_PAIRS)]
    return jnp.asarray(tbl, F32)


def _lane_lo(rows):
    return lax.broadcasted_iota(jnp.int32, (rows, 128), 1) < 64


def _rows(start, dil):
    return pl.ds(start, BLK, stride=dil) if dil > 1 else pl.ds(start, BLK)


def _attn_specs(group, t):
    _, dil = ATTN_GROUPS[group]
    nblk = SPAN_BLOCKS[dil]
    span, edge = BLK * dil * nblk, BLK * dil
    per = span // edge

    def cur(which):
        off = 0 if which is None else which * QKV_LANE_BLOCKS + group * HEAD_PAIRS
        return pl.BlockSpec((span, 128), lambda s, p, *_: (s, off + p))

    def before(which):
        off = which * QKV_LANE_BLOCKS + group * HEAD_PAIRS
        return pl.BlockSpec((edge, 128), lambda s, p, *_: (jnp.maximum(s * per - 1, 0), off + p))

    def after(which):
        off = 0 if which is None else which * QKV_LANE_BLOCKS + group * HEAD_PAIRS
        return pl.BlockSpec((edge, 128), lambda s, p, *_: (jnp.minimum((s + 1) * per, t // edge - 1), off + p))

    slopes = pl.BlockSpec((HEAD_PAIRS, 128), lambda s, p, *_: (0, 0))
    return dil, nblk, span, cur, before, after, slopes


def _attn_fwd(qkv, group, name):
    t = qkv.shape[0]
    dil, nblk, span, cur, before, after, slopes = _attn_specs(group, t)

    def body(q_ref, k_ref, v_ref, kp_ref, vp_ref, sl_ref, o_ref, l_ref):
        first_span = pl.program_id(0) == 0
        sl = sl_ref[pl.ds(pl.program_id(1), 1), :]
        qi = lax.broadcasted_iota(jnp.int32, (BLK, 2 * BLK), 0)
        kj = lax.broadcasted_iota(jnp.int32, (BLK, 2 * BLK), 1)
        dist = qi + BLK - kj
        valid = (dist >= 0) & (dist <= BLK)
        valid_first = valid & ((kj >= BLK) | jnp.logical_not(first_span))
        distf = dist.astype(F32) * float(dil)
        lo_q, lo_k = _lane_lo(BLK), _lane_lo(2 * BLK)
        for r in range(dil):
            kp, vp = kp_ref[_rows(r, dil), :].astype(BF16), vp_ref[_rows(r, dil), :].astype(BF16)
            for b in range(nblk):
                rows = _rows(b * BLK * dil + r, dil)
                ok = valid_first if b == 0 else valid
                q2 = q_ref[rows, :].astype(BF16)
                kc, vc = k_ref[rows, :].astype(BF16), v_ref[rows, :].astype(BF16)
                k2 = jnp.concatenate([kp, kc], axis=0)
                v2 = jnp.concatenate([vp, vc], axis=0)
                kp, vp = kc, vc
                o2 = jnp.zeros((BLK, 128), F32)
                l2 = jnp.zeros((BLK, 128), F32)
                for s01 in range(2):
                    mq = lo_q if s01 == 0 else ~lo_q
                    mk = lo_k if s01 == 0 else ~lo_k
                    s = _dot_nt(jnp.where(mq, q2, jnp.zeros_like(q2)), k2) * ATTN_SCALE - sl[:, s01:s01 + 1] * distf
                    s = jnp.where(ok, s, -jnp.inf)
                    mx = jnp.max(s, axis=-1, keepdims=True)
                    p = jnp.exp(s - mx)
                    den = jnp.sum(p, axis=-1, keepdims=True)
                    o2 = o2 + _dot(p.astype(BF16), jnp.where(mk, v2, jnp.zeros_like(v2))) / den
                    l2 = jnp.where(mq, mx + jnp.log(den), l2)
                o_ref[rows, :] = o2
                l_ref[rows, :] = l2

    in_specs = [cur(0), cur(1), cur(2), before(1), before(2), slopes]
    return pl.pallas_call(body, grid=(t // span, HEAD_PAIRS), in_specs=in_specs, out_specs=[cur(None), cur(None)],
                          out_shape=[SDS((t, ATTN_W), F32), SDS((t, ATTN_W), F32)], name=name,
                          compiler_params=_params(("parallel", "parallel")))(qkv, qkv, qkv, qkv, qkv, _slope_table(group))


def _attn_bwd(qkv, do, lse, dvec, dqkv, group, name):
    t = qkv.shape[0]
    dil, nblk, span, cur, before, after, slopes = _attn_specs(group, t)
    nspan = t // span

    def body(q_ref, k_ref, v_ref, kp_ref, vp_ref, qn_ref, do_ref, l_ref, d_ref, don_ref, ln_ref, dn_ref, sl_ref, *rest):
        out_ref, dk_s, dv_s = rest[-3:]
        span_id, pair_id, which = pl.program_id(0), pl.program_id(1), pl.program_id(2)

        @pl.when(which == 0)
        def _():
            first_span = span_id == 0
            last_span = span_id == nspan - 1
            sl = sl_ref[pl.ds(pair_id, 1), :]
            qi = lax.broadcasted_iota(jnp.int32, (BLK, 2 * BLK), 0)
            kj = lax.broadcasted_iota(jnp.int32, (BLK, 2 * BLK), 1)
            dist = qi + BLK - kj
            valid = (dist >= 0) & (dist <= BLK)
            valid_first = valid & ((kj >= BLK) | jnp.logical_not(first_span))
            distf = dist.astype(F32) * float(dil)
            valid_next = (kj[:, :BLK] >= qi[:, :BLK]) & jnp.logical_not(last_span)
            distf_next = distf[:, :BLK]
            lo_q, lo_k = _lane_lo(BLK), _lane_lo(2 * BLK)
            for r in range(dil):
                kp, vp = kp_ref[_rows(r, dil), :].astype(BF16), vp_ref[_rows(r, dil), :].astype(BF16)
                held = None
                for b in range(nblk + 1):
                    last = b == nblk
                    rows = _rows(r if last else b * BLK * dil + r, dil)
                    qs, dos, ls, ds_ = (qn_ref, don_ref, ln_ref, dn_ref) if last else (q_ref, do_ref, l_ref, d_ref)
                    q2, do2 = qs[rows, :].astype(BF16), dos[rows, :].astype(BF16)
                    l2, d2 = ls[rows, :], ds_[rows, :]
                    if last:
                        k2, v2, ok, bias = kp, vp, valid_next, distf_next
                    else:
                        kc, vc = k_ref[rows, :].astype(BF16), v_ref[rows, :].astype(BF16)
                        k2, v2 = jnp.concatenate([kp, kc], axis=0), jnp.concatenate([vp, vc], axis=0)
                        kp, vp = kc, vc
                        ok, bias = (valid_first if b == 0 else valid), distf
                    dq2 = jnp.zeros((BLK, 128), F32)
                    dk2 = jnp.zeros((k2.shape[0], 128), F32)
                    dv2 = jnp.zeros((k2.shape[0], 128), F32)
                    for s01 in range(2):
                        mq = lo_q if s01 == 0 else ~lo_q
                        col = slice(64 * s01, 64 * s01 + 1)
                        qm = jnp.where(mq, q2, jnp.zeros_like(q2))
                        dom = jnp.where(mq, do2, jnp.zeros_like(do2))
                        s = _dot_nt(qm, k2) * ATTN_SCALE - sl[:, s01:s01 + 1] * bias
                        p = jnp.exp(jnp.where(ok, s - l2[:, col], -jnp.inf))
                        ds = (p * (_dot_nt(dom, v2) - d2[:, col])).astype(BF16)
                        if not last:
                            mk = lo_k if s01 == 0 else ~lo_k
                            dq2 = dq2 + _dot(ds, jnp.where(mk, k2, jnp.zeros_like(k2))) * ATTN_SCALE
                        dk2 = dk2 + _dot_tn(ds, qm) * ATTN_SCALE
                        dv2 = dv2 + _dot_tn(p.astype(BF16), dom)
                    if held is not None:
                        dk_s[held[0], :] = held[1] + dk2[:BLK]
                        dv_s[held[0], :] = held[2] + dv2[:BLK]
                    if not last:
                        out_ref[rows, :] = dq2
                        held = (rows, dk2[BLK:], dv2[BLK:])

        @pl.when(which == 1)
        def _():
            out_ref[...] = dk_s[...]

        @pl.when(which == 2)
        def _():
            out_ref[...] = dv_s[...]

    off = group * HEAD_PAIRS
    out_spec = pl.BlockSpec((span, 128), lambda s, p, w: (s, w * QKV_LANE_BLOCKS + off + p))
    in_specs = [cur(0), cur(1), cur(2), before(1), before(2), after(0), cur(None), cur(None), cur(None),
                after(None), after(None), after(None), slopes]
    args = [qkv, qkv, qkv, qkv, qkv, qkv, do, lse, dvec, do, lse, dvec, _slope_table(group)]
    aliases = {}
    if dqkv is not None:
        in_specs.append(pl.BlockSpec(memory_space=pl.ANY))
        args.append(dqkv)
        aliases = {len(args) - 1: 0}
    return pl.pallas_call(body, grid=(nspan, HEAD_PAIRS, 3), in_specs=in_specs, out_specs=out_spec,
                          out_shape=SDS((t, QKV_DIM), F32), input_output_aliases=aliases,
                          scratch_shapes=[pltpu.VMEM((span, 128), F32), pltpu.VMEM((span, 128), F32)], name=name,
                          compiler_params=_params(("parallel", "parallel", "arbitrary")))(*args)


def _combine_weights(l0, l1, l2):
    mx = jnp.maximum(jnp.maximum(l0, l1), l2)
    e0, e1, e2 = jnp.exp(l0 - mx), jnp.exp(l1 - mx), jnp.exp(l2 - mx)
    den = e0 + e1 + e2
    return e0 / den, e1 / den, e2 / den


def _combine_fwd(os_, ls_, name):
    t = os_[0].shape[0]

    def body(o0, o1, o2, l0, l1, l2, out):
        w0, w1, w2 = _combine_weights(l0[...], l1[...], l2[...])
        out[...] = (w0 * o0[...] + w1 * o1[...] + w2 * o2[...]).astype(BF16)

    row = pl.BlockSpec((ROW_TILE, ATTN_W), lambda i: (i, 0))
    return pl.pallas_call(body, grid=(t // ROW_TILE,), in_specs=[row] * 6, out_specs=row, out_shape=SDS((t, ATTN_W), BF16),
                          name=name, compiler_params=_params(("parallel",)))(*os_, *ls_)


def _combine_bwd(do, os_, ls_, name):
    t = do.shape[0]

    def body(do_ref, o0, o1, o2, l0, l1, l2, g0, g1, g2, d0, d1, d2):
        w0, w1, w2 = _combine_weights(l0[...], l1[...], l2[...])
        dov = do_ref[...]
        prod = dov * (w0 * o0[...] + w1 * o1[...] + w2 * o2[...])
        r = (lax.broadcasted_iota(jnp.int32, (3 * 128, 128), 0) % 128) // 64
        c = lax.broadcasted_iota(jnp.int32, (3 * 128, 128), 1) // 64
        same = (r == c).astype(BF16)
        tbar = jnp.concatenate([_dot(jnp.concatenate(_pieces(prod[:, 128 * k:128 * k + 128], 3), axis=1), same)
                                for k in range(HEAD_PAIRS)], axis=1)
        for w, g, d in ((w0, g0, d0), (w1, g1, d1), (w2, g2, d2)):
            g[...] = w * dov
            d[...] = w * tbar

    row = pl.BlockSpec((ROW_TILE, ATTN_W), lambda i: (i, 0))
    return pl.pallas_call(body, grid=(t // ROW_TILE,), in_specs=[row] * 7, out_specs=[row] * 6,
                          out_shape=[SDS((t, ATTN_W), F32)] * 6, name=name,
                          compiler_params=_params(("parallel",)))(do, *os_, *ls_)


def _peer(k):
    x, y, c = lax.axis_index("x"), lax.axis_index("y"), lax.axis_index("c")
    px = 1 - x if k & 4 else x
    py = 1 - y if k & 2 else y
    pc = 1 - c if k & 1 else c
    return (px, py, pc), 4 * px + 2 * py + pc


def _my_index():
    return 4 * lax.axis_index("x") + 2 * lax.axis_index("y") + lax.axis_index("c")


def _exchange_sems(n):
    return [pltpu.SemaphoreType.DMA((n * (NDEV - 1),)), pltpu.SemaphoreType.DMA((n * (NDEV - 1),)),
            pltpu.SemaphoreType.DMA((n,))]


def _exchange_copies(kind, ins, outs, send, recv, local, arrivals):
    me = _my_index()
    own, sent, arriving = [], [], []
    for i in range(len(ins)):
        own.append(pltpu.make_async_copy(ins[i] if kind == "gather" else ins[i].at[me], outs[i].at[me], local.at[i]))
        for k in range(1, NDEV):
            peer, pidx = _peer(k)
            s = i * (NDEV - 1) + k - 1
            src = ins[i] if kind == "gather" else ins[i].at[pidx]
            for dst, into in ((outs[i].at[me], sent), (outs[i].at[pidx], arriving)):
                if into is sent or arrivals:
                    into.append(pltpu.make_async_remote_copy(src_ref=src, dst_ref=dst, send_sem=send.at[s],
                                                             recv_sem=recv.at[s], device_id=peer, device_id_type=MESH))
    return own, sent, arriving


def _exchange_start(kind, ins, outs, send, recv, local):
    own, sent, _ = _exchange_copies(kind, ins, outs, send, recv, local, arrivals=False)
    for cp in own + sent:
        cp.start()


def _exchange_wait(kind, ins, outs, send, recv, local):
    own, sent, arriving = _exchange_copies(kind, ins, outs, send, recv, local, arrivals=True)
    for cp in sent:
        cp.wait_send()
    for cp in arriving:
        cp.wait_recv()
    for cp in own:
        cp.wait()


def _gather_two_level(tensors, name):
    n = len(tensors)

    def body(*refs):
        ins, outs = refs[:n], refs[n:2 * n]
        send, recv, local = refs[2 * n:]
        x, y, c = lax.axis_index("x"), lax.axis_index("y"), lax.axis_index("c")
        sibling = (x, y, 1 - c)
        chips = [(1 - x, y), (x, 1 - y), (1 - x, 1 - y)]

        def slot(px, py, pc):
            return 4 * px + 2 * py + pc

        def copy(i, k, block, to, src=None):
            return pltpu.make_async_remote_copy(src_ref=outs[i].at[block] if src is None else src, dst_ref=outs[i].at[block],
                                                send_sem=send.at[7 * i + k], recv_sem=recv.at[7 * i + k],
                                                device_id=to, device_id_type=MESH)

        me = slot(x, y, c)
        own = [pltpu.make_async_copy(ins[i], outs[i].at[me], local.at[i]) for i in range(n)]
        first = [copy(i, 0, me, sibling, src=ins[i]) for i in range(n)]
        first += [copy(i, 1 + j, me, (*chip, c), src=ins[i]) for i in range(n) for j, chip in enumerate(chips)]
        for cp in own + first:
            cp.start()
        passed = []
        for i in range(n):
            for j, chip in enumerate(chips):
                copy(i, 1 + j, slot(*chip, c), (x, y, c)).wait_recv()
                fwd = copy(i, 4 + j, slot(*chip, c), sibling)
                fwd.start()
                passed.append(fwd)
        for i in range(n):
            copy(i, 0, slot(x, y, 1 - c), (x, y, c)).wait_recv()
            for j, chip in enumerate(chips):
                copy(i, 4 + j, slot(*chip, 1 - c), (x, y, c)).wait_recv()
        for cp in first + passed:
            cp.wait_send()
        for cp in own:
            cp.wait()

    hbm = pl.BlockSpec(memory_space=pl.ANY)
    return pl.pallas_call(body, in_specs=[hbm] * n, out_specs=[hbm] * n,
                          out_shape=[SDS((NDEV,) + t.shape, t.dtype) for t in tensors],
                          scratch_shapes=_exchange_sems(n), name=name)(*tensors)


def _all_reduce_small(v, name):
    rows = v.shape[0]

    def body(v_ref, out_ref, land, send, recv):
        me = _my_index()
        land[me] = v_ref[...]
        remote = []
        for k in range(1, NDEV):
            peer, _ = _peer(k)
            cp = pltpu.make_async_remote_copy(src_ref=v_ref, dst_ref=land.at[me], send_sem=send.at[k - 1],
                                              recv_sem=recv.at[k - 1], device_id=peer, device_id_type=MESH)
            cp.start()
            remote.append(cp)
        for cp in remote:
            cp.wait_send()
        for k in range(1, NDEV):
            peer, pidx = _peer(k)
            pltpu.make_async_remote_copy(src_ref=v_ref, dst_ref=land.at[pidx], send_sem=send.at[k - 1],
                                         recv_sem=recv.at[k - 1], device_id=peer, device_id_type=MESH).wait_recv()
        total = land[0]
        for d in range(1, NDEV):
            total = total + land[d]
        out_ref[...] = total

    vm = pl.BlockSpec(memory_space=pltpu.VMEM)
    return pl.pallas_call(
        body, in_specs=[vm], out_specs=vm, out_shape=SDS((rows, 128), F32),
        scratch_shapes=[pltpu.VMEM((NDEV, rows, 128), F32), pltpu.SemaphoreType.DMA((NDEV - 1,)),
                        pltpu.SemaphoreType.DMA((NDEV - 1,))],
        name=name)(v)


def _adamw_math(w, g, m, v):
    m = ADAM_B1 * m + (1.0 - ADAM_B1) * g
    v = ADAM_B2 * v + (1.0 - ADAM_B2) * (g * g)
    m_hat = m / (1.0 - ADAM_B1 ** ADAM_STEP)
    v_hat = v / (1.0 - ADAM_B2 ** ADAM_STEP)
    delta = -ADAM_LR * (m_hat / (jnp.sqrt(v_hat) + ADAM_EPS) + ADAM_WD * w)
    return delta, m, v


def _row_tile(rows, cols):
    tr = rows
    while tr * cols * 4 > (1 << 20) and tr % 16 == 0:
        tr //= 2
    return tr


def _adamw(g, w, m, v, name):
    rows, cols = w.shape
    tr = _row_tile(rows, cols)

    def body(g_ref, w_ref, m_ref, v_ref, d_out, m_out, v_out):
        d, mn, vn = _adamw_math(w_ref[...], g_ref[...], m_ref[...], v_ref[...])
        d_out[...] = d
        m_out[...] = mn
        v_out[...] = vn

    sp = pl.BlockSpec((tr, cols), lambda i: (i, 0))
    return pl.pallas_call(body, grid=(rows // tr,), in_specs=[sp] * 4, out_specs=[sp] * 3,
                          out_shape=[SDS((rows, cols), F32)] * 3, name=name, compiler_params=_params(("parallel",)))(g, w, m, v)


def _reduce_adamw(parts, w, m, v, name):
    rows, cols = w.shape
    tr = _row_tile(parts[0].shape[1], cols)
    tiles = [p.shape[1] // tr for p in parts]
    starts = [sum(tiles[:h]) for h in range(len(parts))]

    def body(*refs):
        p_refs = refs[:len(parts)]
        w_ref, m_ref, v_ref, g_out, d_out, m_out, v_out = refs[len(parts):]
        i = pl.program_id(0)
        for h, p_ref in enumerate(p_refs):
            @pl.when((i >= starts[h]) & (i < starts[h] + tiles[h]))
            def _(p_ref=p_ref):
                g = p_ref[0].astype(F32)
                for d in range(1, NDEV):
                    g = g + p_ref[d].astype(F32)
                g_out[...] = g
                dl, mn, vn = _adamw_math(w_ref[...], g, m_ref[...], v_ref[...])
                d_out[...] = dl
                m_out[...] = mn
                v_out[...] = vn

    sp = pl.BlockSpec((tr, cols), lambda i: (i, 0))
    psp = [pl.BlockSpec((NDEV, tr, cols), lambda i, h=h: (0, jnp.clip(i - starts[h], 0, tiles[h] - 1), 0))
           for h in range(len(parts))]
    return pl.pallas_call(body, grid=(rows // tr,), in_specs=psp + [sp, sp, sp], out_specs=[sp] * 4,
                          out_shape=[SDS((rows, cols), F32)] * 4, name=name,
                          compiler_params=_params(("parallel",)))(*parts, w, m, v)


def _pack(items):
    rows = []
    for a in items:
        a = a.reshape(-1).astype(F32)
        pad = (-a.shape[0]) % 128
        rows.append(jnp.pad(a, (0, pad)).reshape(-1, 128))
    out = jnp.concatenate(rows, axis=0)
    return jnp.pad(out, ((0, (-out.shape[0]) % 8), (0, 0)))


def _unpack(packed, shapes):
    out, r = [], 0
    for shp in shapes:
        n = math.prod(shp)
        nr = -(-n // 128)
        out.append(packed[r:r + nr].reshape(-1)[:n].reshape(shp))
        r += nr
    return out


def _ident(t):
    return (t,)


def _add(t, res):
    return (t + res,)


def kernel(x, norm_mix, norm_mlp, ssm_w_in, ssm_conv_w, ssm_conv_b, ssm_dt_bias, ssm_a_log, ssm_d, ssm_norm_w, ssm_w_out, attn_w_qkv, attn_w_o, mlp_w1, mlp_w2, final_norm, loss_target, m_norm_mix, m_norm_mlp, m_ssm_w_in, m_ssm_conv_w, m_ssm_conv_b, m_ssm_dt_bias, m_ssm_a_log, m_ssm_d, m_ssm_norm_w, m_ssm_w_out, m_attn_w_qkv, m_attn_w_o, m_mlp_w1, m_mlp_w2, m_final_norm, v_norm_mix, v_norm_mlp, v_ssm_w_in, v_ssm_conv_w, v_ssm_conv_b, v_ssm_dt_bias, v_ssm_a_log, v_ssm_d, v_ssm_norm_w, v_ssm_w_out, v_attn_w_qkv, v_attn_w_o, v_mlp_w1, v_mlp_w2, v_final_norm):
    t = x.shape[1]
    x0 = x.reshape(t, D_MODEL)
    tgt = loss_target.reshape(t, D_MODEL)
    me = _my_index()
    in_dim = D_INNER + CONV_DIM + SSM_HEADS
    in_shard = in_dim // NDEV
    zx_dim = D_INNER + CONV_DIM

    s_out, s_qkv, s_o = ssm_w_out[0].astype(BF16), attn_w_qkv[0].astype(BF16), attn_w_o[0].astype(BF16)
    s_w1, s_w2 = mlp_w1.astype(BF16), mlp_w2.astype(BF16)
    g_in, g_cw = _gather_two_level([ssm_w_in[0].astype(BF16), ssm_conv_w[0]], "gather_in_proj")
    w_in = jnp.transpose(g_in, (1, 0, 2)).reshape(D_MODEL, in_dim)
    w_z, w_x = w_in[:, :D_INNER], w_in[:, D_INNER:zx_dim]
    w_dt = jnp.pad(w_in[:, zx_dim:], ((0, 0), (0, 128 - SSM_HEADS)))
    conv_w = jnp.transpose(g_cw, (1, 0, 2)).reshape(CONV_WIDTH, CONV_DIM)
    g_w1, g_w2 = [None, None], [None, None]

    def lanes(p):
        return jnp.pad(p.reshape(SSM_GROUPS, 1, 8), ((0, 0), (0, 0), (0, 120)))

    def rows(p):
        return jnp.broadcast_to(p.reshape(SSM_HEADS, 1), (SSM_HEADS, 128))

    bias_g, bias_r = lanes(ssm_dt_bias[0]), rows(ssm_dt_bias[0])
    alog_g, alog_r = lanes(ssm_a_log[0]), rows(ssm_a_log[0])
    d_exp = jnp.repeat(ssm_d[0], SSM_HEAD_DIM).reshape(1, D_INNER)
    norm_w = ssm_norm_w

    def relu2(tot):
        r = jnp.maximum(tot, 0.0)
        return r, r * r

    def mlp_fwd(xin, layer, tag, down_carry):
        h = _rmsnorm_fwd(xin, norm_mlp[layer:layer + 1], f"norm_mlp{tag}")
        r, a, w2 = _mm_fwd(h, g_w1[layer], "cols", D_MODEL, D_FF, epi=relu2, outs=(BF16, BF16),
                           carry=("gather", [s_w2[layer]]), name=f"mlp_up{tag}")
        g_w2[layer] = w2.reshape(D_FF, D_MODEL)
        xout, *got = _mm_fwd(a, g_w2[layer], "plain", D_FF, D_MODEL, epi=_add, outs=(F32,), extras=(xin,),
                             carry=("gather", down_carry) if down_carry else None, name=f"mlp_down{tag}")
        return h, r, a, xout, got

    h0 = _rmsnorm_fwd(x0, norm_mix[0:1], "norm_mix0")
    z, g_out = _mm_fwd(h0, w_z, "plain", D_MODEL, D_INNER, epi=_ident, outs=(F32,), carry=("gather", [s_out]), name="ssm_in_z")
    xpre, g_w1[0] = _mm_fwd(h0, w_x, "plain", D_MODEL, CONV_DIM, epi=_ident, outs=(F32,), carry=("gather", [s_w1[0]]),
                            name="ssm_in_x")
    dtp, = _mm_fwd(h0, w_dt, "plain", D_MODEL, 128, epi=_ident, outs=(F32,), name="ssm_in_dt")
    xbc = _conv_fwd(xpre, conv_w, ssm_conv_b, "conv_fwd")
    dtp64 = dtp[:, :SSM_HEADS]
    dtp_g = jnp.pad(jnp.transpose(dtp64.reshape(t, SSM_GROUPS, 8), (1, 0, 2)), ((0, 0), (0, 0), (0, 120)))
    dtp_t = jnp.transpose(dtp64)
    y_ssd, yn, states = _ssd_fwd(xbc, z, dtp_g, dtp_t, bias_g, bias_r, alog_g, alog_r, d_exp, norm_w, "ssd_fwd")
    g_out = g_out.reshape(D_INNER, D_MODEL)
    x1, g_o = _mm_fwd(yn, g_out, "plain", D_INNER, D_MODEL, epi=_add, outs=(F32,), extras=(x0,), carry=("gather", [s_o]),
                      name="ssm_out")
    h1, r1, a1, x2, (g_qkv,) = mlp_fwd(x1, 0, "0", [s_qkv])

    h2 = _rmsnorm_fwd(x2, norm_mix[1:2], "norm_mix1")
    qkv, g_w1[1] = _mm_fwd(h2, g_qkv, "cols", D_MODEL, QKV_DIM, epi=_ident, outs=(F32,), tn=QKV_DIM // NDEV,
                           carry=("gather", [s_w1[1]]), name="attn_qkv")
    att = [_attn_fwd(qkv, g, f"attn_fwd{g}") for g in range(3)]
    os_, ls_ = [a[0] for a in att], [a[1] for a in att]
    o_mix = _combine_fwd(os_, ls_, "attn_combine")
    x3, = _mm_fwd(o_mix, g_o, "cols", ATTN_W, D_MODEL, epi=_add, outs=(F32,), extras=(x2,), tn=D_MODEL // NDEV, name="attn_out")
    h3, r3, a3, x4, _ = mlp_fwd(x3, 1, "1", None)

    dx4, dx4b, loss_acc, d_final = _loss_head(x4, tgt, final_norm.reshape(1, D_MODEL), "loss_head")

    def mlp_bwd(xin, h, r, a, dxo, dxob, layer, tag):
        du, = _mm_dx(dxob, g_w2[layer], "plain", D_FF, D_MODEL, epi=lambda tot, rr: (tot * (2.0 * rr.astype(F32)),),
                     outs=(BF16,), extras=(r,), name=f"mlp_down_dx{tag}")
        dw2, = _mm_dw(a, dxob, "plain", name=f"mlp_down_dw{tag}")
        dw2 = dw2.reshape(NDEV, D_FF // NDEV, D_MODEL)
        dw1, p_dw2 = _mm_dw(h, du, "cols", carry=("scatter", [dw2]), name=f"mlp_up_dw{tag}")
        dh, p_dw1 = _mm_dx(du, g_w1[layer], "cols", D_MODEL, D_FF, epi=_ident, outs=(F32,), carry=("scatter", [dw1]),
                           name=f"mlp_up_dx{tag}")
        dxi, dxib, dg = _rmsnorm_bwd(xin, norm_mlp[layer:layer + 1], dh, dxo, f"norm_mlp_bwd{tag}")
        return dxi, dxib, dg, p_dw1, p_dw2

    dx3, dx3b, dg_mlp1, p_w1_1, p_w2_1 = mlp_bwd(x3, h3, r3, a3, dx4, dx4b, 1, "1")

    dw_o, = _mm_dw(o_mix, dx3b, "cols", tn=D_MODEL // NDEV, name="attn_out_dw")
    do, p_o = _mm_dx(dx3b, g_o, "cols", ATTN_W, D_MODEL, epi=_ident, outs=(F32,), tk=D_MODEL // NDEV,
                     carry=("scatter", [dw_o]), name="attn_out_dx")
    cb = _combine_bwd(do, os_, ls_, "attn_combine_bwd")
    dos, dvecs = cb[:3], cb[3:]
    dqkv = None
    for g in range(3):
        dqkv = _attn_bwd(qkv, dos[g], ls_[g], dvecs[g], dqkv, g, f"attn_bwd{g}")
    dw_qkv, = _mm_dw(h2, dqkv, "cols", tn=QKV_DIM // NDEV, tk=1024, name="attn_qkv_dw")
    dh2, p_qkv = _mm_dx(dqkv, g_qkv, "cols", D_MODEL, QKV_DIM, epi=_ident, outs=(F32,), tk=QKV_DIM // NDEV,
                        carry=("scatter", [dw_qkv]), name="attn_qkv_dx")
    dx2, dx2b, dg_mix1 = _rmsnorm_bwd(x2, norm_mix[1:2], dh2, dx3, "norm_mix_bwd1")

    dx1, dx1b, dg_mlp0, p_w1_0, p_w2_0 = mlp_bwd(x1, h1, r1, a1, dx2, dx2b, 0, "0")

    dw_out, = _mm_dw(yn, dx1b, "plain", name="ssm_out_dw")
    dw_out = dw_out.reshape(NDEV, D_INNER // NDEV, D_MODEL)
    dyn, p_out = _mm_dx(dx1b, g_out, "plain", D_INNER, D_MODEL, epi=_ident, outs=(F32,), carry=("scatter", [dw_out]),
                        name="ssm_out_dx")
    dxs, d_b, d_c, dz, ddtp_g, hsums, csums = _ssd_bwd(dyn, y_ssd, xbc, z, states, dtp_g, dtp_t, bias_g, bias_r,
                                                       alog_g, alog_r, d_exp, norm_w, "ssd_bwd")
    dxbc = jnp.concatenate([dxs, d_b, d_c], axis=1)
    du, conv_sums = _conv_bwd(xpre, conv_w, ssm_conv_b, dxbc, "conv_bwd")
    ddtp = jnp.transpose(ddtp_g[:, :, :8], (1, 0, 2)).reshape(t, SSM_HEADS)
    ddtp = jnp.pad(ddtp, ((0, 0), (0, 128 - SSM_HEADS))).astype(BF16)
    dw_z, = _mm_dw(h0, dz, "plain", name="ssm_in_z_dw")
    dw_x, = _mm_dw(h0, du, "plain", name="ssm_in_x_dw")
    dw_dt, = _mm_dw(h0, ddtp, "plain", name="ssm_in_dt_dw")
    half = D_MODEL // 2

    def in_part(lo):
        rows = jnp.concatenate([dw_z[lo:lo + half], dw_x[lo:lo + half], dw_dt[lo:lo + half, :SSM_HEADS]], axis=1)
        return jnp.transpose(rows.reshape(half, NDEV, in_shard), (1, 0, 2))

    dh0, = _mm_dx(ddtp, w_dt, "plain", D_MODEL, 128, epi=_ident, outs=(F32,), name="ssm_in_dt_dx")
    dh0, p_in_a = _mm_dx(dz, w_z, "plain", D_MODEL, D_INNER, epi=_add, outs=(F32,), extras=(dh0,),
                         carry=("scatter", [in_part(0)]), name="ssm_in_z_dx")
    dh0, p_in_b = _mm_dx(du, w_x, "plain", D_MODEL, CONV_DIM, epi=_add, outs=(F32,), extras=(dh0,),
                         carry=("scatter", [in_part(half)]), name="ssm_in_x_dx")
    dx0, _, dg_mix0 = _rmsnorm_bwd(x0, norm_mix[0:1], dh0, dx1, "norm_mix_bwd0")

    parts = [[p_in_a, p_in_b], [p_out], [p_qkv], [p_o], [p_w1_0], [p_w1_1], [p_w2_0], [p_w2_1]]

    def own(w, mm, vv):
        shp = w.shape
        f = lambda a: a.reshape(-1, shp[-1])
        return f(w), f(mm), f(vv), shp

    big = {}
    for key, part, (w, mm, vv) in (
            ("ssm_w_in", parts[0], (ssm_w_in, m_ssm_w_in, v_ssm_w_in)),
            ("ssm_w_out", parts[1], (ssm_w_out, m_ssm_w_out, v_ssm_w_out)),
            ("attn_w_qkv", parts[2], (attn_w_qkv, m_attn_w_qkv, v_attn_w_qkv)),
            ("attn_w_o", parts[3], (attn_w_o, m_attn_w_o, v_attn_w_o))):
        w2, m2, v2, shp = own(w, mm, vv)
        res = _reduce_adamw(part, w2, m2, v2, f"adamw_{key}")
        big[key] = [r.reshape(shp) for r in res]
    for key, pa, pb, (w, mm, vv) in (("mlp_w1", parts[4], parts[5], (mlp_w1, m_mlp_w1, v_mlp_w1)),
                                     ("mlp_w2", parts[6], parts[7], (mlp_w2, m_mlp_w2, v_mlp_w2))):
        res = [_reduce_adamw(p, w[l], mm[l], vv[l], f"adamw_{key}_{l}") for l, p in enumerate((pa, pb))]
        big[key] = [jnp.stack([res[0][i], res[1][i]], axis=0) for i in range(4)]

    d_norm_mix = jnp.concatenate([dg_mix0, dg_mix1], axis=0)
    d_norm_mlp = jnp.concatenate([dg_mlp0, dg_mlp1], axis=0)
    d_conv_b = conv_sums[4:5]
    d_conv_w = conv_sums[0:4]
    head = hsums[:, :, :8]
    d_dt_bias, d_a_log, d_d = (head[:, k, :].reshape(1, SSM_HEADS) for k in range(3))
    d_ssm_norm = csums[0:1]
    small = [d_norm_mix, d_norm_mlp, d_conv_b, d_dt_bias, d_a_log, d_d, d_ssm_norm, d_final, d_conv_w, loss_acc[0:1, 0:1]]
    shapes = [a.shape for a in small]
    summed = _unpack(_all_reduce_small(_pack(small), "reduce_small"), shapes)
    g_conv_w_full = summed[8]
    loss = summed[9].reshape(())
    g_conv_w = lax.dynamic_slice(g_conv_w_full, (0, me * (CONV_DIM // NDEV)), (CONV_WIDTH, CONV_DIM // NDEV))

    small_names = ["norm_mix", "norm_mlp", "ssm_conv_b", "ssm_dt_bias", "ssm_a_log", "ssm_d", "ssm_norm_w", "final_norm"]
    small_w = [norm_mix, norm_mlp, ssm_conv_b, ssm_dt_bias, ssm_a_log, ssm_d, ssm_norm_w, final_norm, ssm_conv_w]
    small_m = [m_norm_mix, m_norm_mlp, m_ssm_conv_b, m_ssm_dt_bias, m_ssm_a_log, m_ssm_d, m_ssm_norm_w, m_final_norm, m_ssm_conv_w]
    small_v = [v_norm_mix, v_norm_mlp, v_ssm_conv_b, v_ssm_dt_bias, v_ssm_a_log, v_ssm_d, v_ssm_norm_w, v_final_norm, v_ssm_conv_w]
    small_g = [summed[i].reshape(small_w[i].shape) for i in range(8)] + [g_conv_w.reshape(ssm_conv_w.shape)]
    wshapes = [a.shape for a in small_w]
    sd, sm, sv = _adamw(_pack(small_g), _pack(small_w), _pack(small_m), _pack(small_v), "adamw_small")
    sd, sm, sv = _unpack(sd, wshapes), _unpack(sm, wshapes), _unpack(sv, wshapes)
    res = {n: (small_g[i], sd[i], sm[i], sv[i]) for i, n in enumerate(small_names + ["ssm_conv_w"])}
    for n in big:
        res[n] = tuple(big[n])

    order = ["norm_mix", "norm_mlp", "ssm_w_in", "ssm_conv_w", "ssm_conv_b", "ssm_dt_bias", "ssm_a_log", "ssm_d",
             "ssm_norm_w", "ssm_w_out", "attn_w_qkv", "attn_w_o", "mlp_w1", "mlp_w2", "final_norm"]
    outs = [loss, dx0.reshape(x.shape)]
    for kind in range(4):
        outs += [res[n][kind] for n in order]
    return tuple(outs)
```

```python
import math

import jax
import jax.numpy as jnp
from jax import lax
from jax.experimental import pallas as pl
from jax.experimental.pallas import tpu as pltpu

F32, BF16 = jnp.float32, jnp.bfloat16
SDS = jax.ShapeDtypeStruct
MESH = pl.DeviceIdType.MESH
HIGHEST = lax.Precision.HIGHEST

NDEV = 8
D_MODEL = 2048
D_INNER = 4096
SSM_HEADS = 64
SSM_HEAD_DIM = 64
SSM_GROUPS = 8
SSM_STATE = 128
CHUNK = 128
CONV_DIM = 6144
CONV_WIDTH = 4
GROUP_W = D_INNER // SSM_GROUPS
ATTN_GROUPS = ((128, 1), (512, 4), (2048, 16))
ATTN_W = 1024
QKV_DIM = 9216
D_FF = 8192
EPS = 1e-5
ADAM_LR, ADAM_B1, ADAM_B2, ADAM_EPS, ADAM_WD, ADAM_STEP = 0.001, 0.9, 0.999, 1e-08, 0.01, 10

VMEM_LIMIT = 48 * 1024 * 1024


def _params(sem):
    return pltpu.CompilerParams(dimension_semantics=sem, vmem_limit_bytes=VMEM_LIMIT)


def _sigmoid(v):
    return 1.0 / (1.0 + jnp.exp(-v))


def _softplus(v):
    return jnp.maximum(v, 0.0) + jnp.log1p(jnp.exp(-jnp.abs(v)))


def _wspec(layout, kw, nw, tr, tc, sel):
    if layout == "plain":
        return pl.BlockSpec((tr, tc), lambda *g: sel(*g))
    per = (nw // NDEV) // tc
    return pl.BlockSpec((None, tr, tc), lambda *g: (sel(*g)[1] // per, sel(*g)[0], sel(*g)[1] % per))


def _wshape(layout, kw, nw):
    return (kw, nw) if layout == "plain" else (NDEV, kw, nw // NDEV)


def _mm_call(name, grid, in_specs, out_specs, out_shape, dims, n_extra, epi, tm, tn, carry):
    nk = grid[2]
    kind, moved = carry if carry else (None, ())
    nc = len(moved)
    n_out = len(out_shape)

    def body(*refs):
        a_ref, b_ref = refs[0], refs[1]
        extra = refs[2:2 + n_extra]
        c_in = refs[2 + n_extra:2 + n_extra + nc]
        outs = refs[2 + n_extra + nc:2 + n_extra + nc + n_out]
        c_out = refs[2 + n_extra + nc + n_out:2 + n_extra + 2 * nc + n_out]
        acc = refs[2 + n_extra + 2 * nc + n_out]
        sems = refs[3 + n_extra + 2 * nc + n_out:]
        i, j, k = pl.program_id(0), pl.program_id(1), pl.program_id(2)
        if nc:
            @pl.when((i == 0) & (j == 0) & (k == 0))
            def _():
                _exchange_start(kind, c_in, c_out, *sems)

        d = lax.dot_general(a_ref[...].astype(BF16), b_ref[...].astype(BF16), (dims, ((), ())), preferred_element_type=F32)

        def finish(total):
            vals = epi(total, *[e[...] for e in extra])
            for o, v in zip(outs, vals):
                o[...] = v.astype(o.dtype)

        if nk == 1:
            finish(d)
        else:
            @pl.when(k == 0)
            def _():
                acc[...] = d

            @pl.when(jnp.logical_and(k > 0, k < nk - 1))
            def _():
                acc[...] += d

            @pl.when(k == nk - 1)
            def _():
                finish(acc[...] + d)

        if nc:
            @pl.when((i == grid[0] - 1) & (j == grid[1] - 1) & (k == nk - 1))
            def _():
                _exchange_wait(kind, c_in, c_out, *sems)

    hbm = pl.BlockSpec(memory_space=pl.ANY)
    scratch = [pltpu.VMEM((tm, tn), F32)] + (_exchange_sems(nc) if nc else [])
    c_shape = [SDS(((NDEV,) + t.shape) if kind == "gather" else t.shape, t.dtype) for t in moved]
    sem = ("arbitrary",) * 3 if nc else ("parallel", "parallel", "arbitrary")
    return pl.pallas_call(
        body, grid=grid, in_specs=in_specs + [hbm] * nc, out_specs=out_specs + [hbm] * nc,
        out_shape=out_shape + c_shape, scratch_shapes=scratch, name=name, compiler_params=_params(sem))


def _mm_fwd(a, w, layout, kw, nw, *, epi, outs, extras=(), tm=1024, tn=1024, tk=2048, carry=None, name):
    m = a.shape[0]
    tm, tn, tk = min(tm, m), min(tn, nw), min(tk, kw)
    grid = (m // tm, nw // tn, kw // tk)
    o_spec = pl.BlockSpec((tm, tn), lambda i, j, k: (i, j))
    in_specs = [pl.BlockSpec((tm, tk), lambda i, j, k: (i, k)), _wspec(layout, kw, nw, tk, tn, lambda i, j, k: (k, j))]
    in_specs += [o_spec] * len(extras)
    call = _mm_call(name, grid, in_specs, [o_spec] * len(outs), [SDS((m, nw), dt) for dt in outs], ((1,), (0,)),
                    len(extras), epi, tm, tn, carry)
    return call(a, w, *extras, *(carry[1] if carry else ()))


def _mm_dx(g, w, layout, kw, nw, *, epi, outs, extras=(), tm=1024, tn=1024, tk=2048, carry=None, name):
    m = g.shape[0]
    tm, tn, tk = min(tm, m), min(tn, kw), min(tk, nw if layout == "plain" else nw // NDEV)
    grid = (m // tm, kw // tn, nw // tk)
    o_spec = pl.BlockSpec((tm, tn), lambda i, j, k: (i, j))
    in_specs = [pl.BlockSpec((tm, tk), lambda i, j, k: (i, k)), _wspec(layout, kw, nw, tn, tk, lambda i, j, k: (j, k))]
    in_specs += [o_spec] * len(extras)
    call = _mm_call(name, grid, in_specs, [o_spec] * len(outs), [SDS((m, kw), dt) for dt in outs], ((1,), (1,)),
                    len(extras), epi, tm, tn, carry)
    return call(g, w, *extras, *(carry[1] if carry else ()))


def _mm_dw(a, g, layout, *, tm=1024, tn=1024, tk=2048, carry=None, name):
    m, kw = a.shape
    nw = g.shape[1]
    tm, tn, tk = min(tm, kw), min(tn, nw if layout == "plain" else nw // NDEV), min(tk, m)
    grid = (kw // tm, nw // tn, m // tk)
    in_specs = [pl.BlockSpec((tk, tm), lambda i, j, k: (k, i)), pl.BlockSpec((tk, tn), lambda i, j, k: (k, j))]
    o_spec = _wspec(layout, kw, nw, tm, tn, lambda i, j, k: (i, j))
    call = _mm_call(name, grid, in_specs, [o_spec], [SDS(_wshape(layout, kw, nw), BF16)], ((0,), (0,)),
                    0, lambda t: (t,), tm, tn, carry)
    return call(a, g, *(carry[1] if carry else ()))


ROW_TILE = 256


def _rmsnorm_fwd(x, g, name):
    t, d = x.shape

    def body(x_ref, g_ref, h_ref):
        xv = x_ref[...]
        r = lax.rsqrt(jnp.mean(xv * xv, axis=-1, keepdims=True) + EPS)
        h_ref[...] = (xv * r * g_ref[...]).astype(BF16)

    row = pl.BlockSpec((ROW_TILE, d), lambda i: (i, 0))
    vec = pl.BlockSpec((1, d), lambda i: (0, 0))
    return pl.pallas_call(body, grid=(t // ROW_TILE,), in_specs=[row, vec], out_specs=row, out_shape=SDS((t, d), BF16),
                          name=name, compiler_params=_params(("parallel",)))(x, g)


def _rmsnorm_bwd(x, g, dh, dres, name):
    t, d = x.shape

    def body(x_ref, g_ref, dh_ref, dres_ref, dx_ref, dxb_ref, dg_ref):
        xv = x_ref[...]
        r = lax.rsqrt(jnp.mean(xv * xv, axis=-1, keepdims=True) + EPS)
        xh = xv * r
        dhv = dh_ref[...]
        gd = dhv * g_ref[...]
        dx = dres_ref[...] + r * (gd - xh * jnp.mean(gd * xh, axis=-1, keepdims=True))
        dx_ref[...] = dx
        dxb_ref[...] = dx.astype(BF16)
        part = jnp.sum(dhv * xh, axis=0, keepdims=True)

        @pl.when(pl.program_id(0) == 0)
        def _():
            dg_ref[...] = part

        @pl.when(pl.program_id(0) > 0)
        def _():
            dg_ref[...] += part

    row = pl.BlockSpec((ROW_TILE, d), lambda i: (i, 0))
    vec = pl.BlockSpec((1, d), lambda i: (0, 0))
    return pl.pallas_call(body, grid=(t // ROW_TILE,), in_specs=[row, vec, row, row], out_specs=[row, row, vec],
                          out_shape=[SDS((t, d), F32), SDS((t, d), BF16), SDS((1, d), F32)], name=name,
                          compiler_params=_params(("arbitrary",)))(x, g, dh, dres)


def _loss_head(x, tgt, g, name):
    t, d = x.shape

    def body(x_ref, t_ref, g_ref, dx_ref, dxb_ref, loss_ref, dg_ref):
        xv = x_ref[...]
        r = lax.rsqrt(jnp.mean(xv * xv, axis=-1, keepdims=True) + EPS)
        xh = xv * r
        gv = g_ref[...]
        err = xh * gv - t_ref[...]
        part_loss = 0.5 * jnp.sum(jnp.mean(err * err, axis=-1, keepdims=True), axis=0, keepdims=True)
        dy = err * (1.0 / d)
        gd = dy * gv
        dx = r * (gd - xh * jnp.mean(gd * xh, axis=-1, keepdims=True))
        dx_ref[...] = dx
        dxb_ref[...] = dx.astype(BF16)
        part_g = jnp.sum(dy * xh, axis=0, keepdims=True)
        part_l = jnp.broadcast_to(part_loss, (8, 128))

        @pl.when(pl.program_id(0) == 0)
        def _():
            dg_ref[...] = part_g
            loss_ref[...] = part_l

        @pl.when(pl.program_id(0) > 0)
        def _():
            dg_ref[...] += part_g
            loss_ref[...] += part_l

    row = pl.BlockSpec((ROW_TILE, d), lambda i: (i, 0))
    vec = pl.BlockSpec((1, d), lambda i: (0, 0))
    sc = pl.BlockSpec((8, 128), lambda i: (0, 0))
    return pl.pallas_call(body, grid=(t // ROW_TILE,), in_specs=[row, row, vec], out_specs=[row, row, sc, vec],
                          out_shape=[SDS((t, d), F32), SDS((t, d), BF16), SDS((8, 128), F32), SDS((1, d), F32)], name=name,
                          compiler_params=_params(("arbitrary",)))(x, tgt, g)


CONV_ROWS = 256
CONV_COLS = 2048


def _shift_down(cur, prev8, k):
    sh = pltpu.roll(cur, k, axis=0)
    ph = pltpu.roll(prev8, k, axis=0)
    rid = lax.broadcasted_iota(jnp.int32, ph.shape, 0)
    head = jnp.where(rid < k, ph, sh[0:8])
    return jnp.concatenate([head, sh[8:]], axis=0)


def _shift_up(cur, next8, k):
    n = cur.shape[0]
    sh = pltpu.roll(cur, n - k, axis=0)
    nh = pltpu.roll(next8, 8 - k, axis=0)
    rid = lax.broadcasted_iota(jnp.int32, nh.shape, 0)
    tail = jnp.where(rid >= 8 - k, nh, sh[n - 8:])
    return jnp.concatenate([sh[:n - 8], tail], axis=0)


def _conv_pre(cur, prev8, w, b):
    acc = w[3:4, :] * cur + b
    for k in range(1, CONV_WIDTH):
        acc = acc + w[3 - k:4 - k, :] * _shift_down(cur, prev8, k)
    return acc


def _conv_fwd(u, w, b, name):
    t, c = u.shape
    per = CONV_ROWS // 8

    def body(u_ref, p_ref, w_ref, b_ref, o_ref):
        prev8 = jnp.where(pl.program_id(1) == 0, 0.0, p_ref[...])
        pre = _conv_pre(u_ref[...], prev8, w_ref[...], b_ref[...])
        o_ref[...] = pre * _sigmoid(pre)

    cur = pl.BlockSpec((CONV_ROWS, CONV_COLS), lambda j, i: (i, j))
    prev = pl.BlockSpec((8, CONV_COLS), lambda j, i: (jnp.maximum(i * per - 1, 0), j))
    wsp = pl.BlockSpec((CONV_WIDTH, CONV_COLS), lambda j, i: (0, j))
    bsp = pl.BlockSpec((1, CONV_COLS), lambda j, i: (0, j))
    return pl.pallas_call(body, grid=(c // CONV_COLS, t // CONV_ROWS), in_specs=[cur, prev, wsp, bsp], out_specs=cur,
                          out_shape=SDS((t, c), F32), name=name, compiler_params=_params(("parallel", "parallel")))(u, u, w, b)


def _conv_bwd(u, w, b, dout, name):
    t, c = u.shape
    per = CONV_ROWS // 8
    last = t // CONV_ROWS - 1

    def dsilu(pre, d):
        s = _sigmoid(pre)
        return d * (s * (1.0 + pre * (1.0 - s)))

    def body(u_ref, p_ref, n_ref, w_ref, b_ref, d_ref, dn_ref, du_ref, dw_ref):
        i = pl.program_id(1)
        cur = u_ref[...]
        wv, bv = w_ref[...], b_ref[...]
        prev8 = jnp.where(i == 0, 0.0, p_ref[...])
        dpre = dsilu(_conv_pre(cur, prev8, wv, bv), d_ref[...])
        nxt, tail = n_ref[...], cur[CONV_ROWS - 8:]
        rid = lax.broadcasted_iota(jnp.int32, nxt.shape, 0)
        pre_n = wv[3:4, :] * nxt + bv
        for k in range(1, CONV_WIDTH):
            pre_n = pre_n + wv[3 - k:4 - k, :] * jnp.where(rid < k, pltpu.roll(tail, k, axis=0), pltpu.roll(nxt, k, axis=0))
        dpre_n = jnp.where(i == last, 0.0, dsilu(pre_n, dn_ref[...]))
        du = wv[3:4, :] * dpre
        for k in range(1, CONV_WIDTH):
            du = du + wv[3 - k:4 - k, :] * _shift_up(dpre, dpre_n, k)
        du_ref[...] = du.astype(BF16)
        rows = [jnp.sum(dpre * _shift_down(cur, prev8, 3 - k), axis=0, keepdims=True) for k in range(3)]
        rows.append(jnp.sum(dpre * cur, axis=0, keepdims=True))
        rows.append(jnp.sum(dpre, axis=0, keepdims=True))
        part = jnp.concatenate(rows + [jnp.zeros((3, cur.shape[1]), F32)], axis=0)

        @pl.when(i == 0)
        def _():
            dw_ref[...] = part

        @pl.when(i > 0)
        def _():
            dw_ref[...] += part

    cur = pl.BlockSpec((CONV_ROWS, CONV_COLS), lambda j, i: (i, j))
    prev = pl.BlockSpec((8, CONV_COLS), lambda j, i: (jnp.maximum(i * per - 1, 0), j))
    nxt = pl.BlockSpec((8, CONV_COLS), lambda j, i: (jnp.minimum((i + 1) * per, t // 8 - 1), j))
    wsp = pl.BlockSpec((CONV_WIDTH, CONV_COLS), lambda j, i: (0, j))
    bsp = pl.BlockSpec((1, CONV_COLS), lambda j, i: (0, j))
    acc = pl.BlockSpec((8, CONV_COLS), lambda j, i: (0, j))
    return pl.pallas_call(body, grid=(c // CONV_COLS, t // CONV_ROWS), in_specs=[cur, prev, nxt, wsp, bsp, cur, nxt],
                          out_specs=[cur, acc], out_shape=[SDS((t, c), BF16), SDS((8, c), F32)], name=name,
                          compiler_params=_params(("parallel", "arbitrary")))(u, u, u, w, b, dout, dout)


def _pieces(v, n):
    out, rest = [], v
    for _ in range(n):
        p = rest.astype(BF16)
        out.append(p)
        rest = rest - p.astype(F32)
    return out


def _head_expand(v):
    r = lax.broadcasted_iota(jnp.int32, (3 * 128, GROUP_W), 0) % 128
    c = lax.broadcasted_iota(jnp.int32, (3 * 128, GROUP_W), 1)
    return _dot(jnp.concatenate(_pieces(v, 3), axis=1), (c // SSM_HEAD_DIM == r).astype(BF16))


def _head_sum(vs):
    r = lax.broadcasted_iota(jnp.int32, (2 * GROUP_W, 128), 0) % GROUP_W
    c = lax.broadcasted_iota(jnp.int32, (2 * GROUP_W, 128), 1)
    stacked = jnp.concatenate([jnp.concatenate(_pieces(v, 2), axis=1) for v in vs], axis=0)
    out = _dot(stacked, (r // SSM_HEAD_DIM == c).astype(BF16))
    res, at = [], 0
    for v in vs:
        res.append(out[at:at + v.shape[0]])
        at += v.shape[0]
    return res


def _tri_left(tri, v):
    r = _dot(tri.astype(BF16), jnp.concatenate(_pieces(v, 3), axis=1))
    return r[:, 0:128] + r[:, 128:256] + r[:, 256:384]


def _dot(a, b):
    return jnp.dot(a, b, preferred_element_type=F32)


def _dot_nt(a, b):
    return lax.dot_general(a, b, (((1,), (1,)), ((), ())), preferred_element_type=F32)


def _dot_tn(a, b):
    return lax.dot_general(a, b, (((0,), (0,)), ((), ())), preferred_element_type=F32)


def _ssd_common(dtp, dtpt, bias, biasr, alog, alogr):
    li = lax.broadcasted_iota(jnp.int32, (CHUNK, CHUNK), 0)
    si = lax.broadcasted_iota(jnp.int32, (CHUNK, CHUNK), 1)
    lower, upper = (li >= si), (li <= si)
    dt = _softplus(dtp + bias)
    a_neg = -jnp.exp(alog)
    cs = _tri_left(lower, dt * a_neg)
    dtr = _softplus(dtpt + biasr)
    ar = jnp.concatenate([dtr * (-jnp.exp(alogr)), jnp.zeros((8, CHUNK), F32)], axis=0)
    r3 = lax.broadcasted_iota(jnp.int32, (3 * CHUNK, CHUNK), 0) % CHUNK
    c3 = lax.broadcasted_iota(jnp.int32, (3 * CHUNK, CHUNK), 1)
    csr = _dot(jnp.concatenate(_pieces(ar, 3), axis=1), (r3 <= c3).astype(BF16))[0:8]
    return lower, upper, dt, a_neg, cs, csr


def _head_masked_rows(v):
    hl = lax.broadcasted_iota(jnp.int32, v.shape, 1) // SSM_HEAD_DIM
    return jnp.concatenate([jnp.where(hl == j, v, jnp.zeros_like(v)) for j in range(8)], axis=0)


def _ssd_fwd(xbc, z, dtp_g, dtp_t, bias_g, bias_r, alog_g, alog_r, d_exp, norm_w, name):
    t = xbc.shape[0]
    nc = t // CHUNK

    def body(xs_ref, b_ref, c_ref, z_ref, dtp_ref, dtpt_ref, bias_ref, biasr_ref, alog_ref, alogr_ref, dexp_ref, nw_ref,
             y_ref, yn_ref, st_ref, state):
        c = pl.program_id(1)

        @pl.when(c == 0)
        def _():
            state[...] = jnp.zeros_like(state)

        lower, _, dt, a_neg, cs, csr = _ssd_common(dtp_ref[...], dtpt_ref[...], bias_ref[...], biasr_ref[...],
                                                alog_ref[...], alogr_ref[...])
        cs_e = _head_expand(cs)
        dt_e = _head_expand(dt)
        xs = xs_ref[...]
        xdt = xs * dt_e
        bm = b_ref[...]
        cm = c_ref[...]
        bmb, cmb = bm.astype(BF16), cm.astype(BF16)
        cb = _dot_nt(cmb, bmb)
        ms = []
        for j in range(8):
            dlt = cs_e[:, SSM_HEAD_DIM * j:SSM_HEAD_DIM * j + 1] - csr[j:j + 1, :]
            ms.append((cb * jnp.exp(jnp.where(lower, dlt, -jnp.inf))).astype(BF16))
        y = _dot(jnp.concatenate(ms, axis=1), _head_masked_rows(xdt.astype(BF16)))
        st_in = state[...]
        st_ref[...] = st_in
        y = y + jnp.exp(cs_e) * _dot(cmb, st_in.astype(BF16))
        cs_last = cs_e[CHUNK - 1:CHUNK, :]
        xdtd = (xdt * jnp.exp(cs_last - cs_e)).astype(BF16)
        state[...] = jnp.exp(cs_last) * st_in + _dot(bm.T.astype(BF16), xdtd)
        y_ref[...] = y
        zz = z_ref[...]
        y2 = (y + dexp_ref[...] * xs) * (zz * _sigmoid(zz))
        r = lax.rsqrt(jnp.mean(y2 * y2, axis=-1, keepdims=True) + EPS)
        yn_ref[...] = (y2 * r * nw_ref[...]).astype(BF16)

    gw = pl.BlockSpec((CHUNK, GROUP_W), lambda g, c: (c, g))
    in_specs = [
        gw,
        pl.BlockSpec((CHUNK, SSM_STATE), lambda g, c: (c, D_INNER // SSM_STATE + g)),
        pl.BlockSpec((CHUNK, SSM_STATE), lambda g, c: (c, D_INNER // SSM_STATE + SSM_GROUPS + g)),
        gw,
        pl.BlockSpec((None, CHUNK, 128), lambda g, c: (g, c, 0)),
        pl.BlockSpec((8, CHUNK), lambda g, c: (g, c)),
        pl.BlockSpec((None, 1, 128), lambda g, c: (g, 0, 0)),
        pl.BlockSpec((8, 128), lambda g, c: (g, 0)),
        pl.BlockSpec((None, 1, 128), lambda g, c: (g, 0, 0)),
        pl.BlockSpec((8, 128), lambda g, c: (g, 0)),
        pl.BlockSpec((1, GROUP_W), lambda g, c: (0, g)),
        pl.BlockSpec((1, GROUP_W), lambda g, c: (0, g)),
    ]
    out_specs = [gw, gw, pl.BlockSpec((None, SSM_STATE, GROUP_W), lambda g, c: (c, 0, g))]
    out_shape = [SDS((t, D_INNER), F32), SDS((t, D_INNER), BF16), SDS((nc, SSM_STATE, D_INNER), F32)]
    return pl.pallas_call(body, grid=(SSM_GROUPS, nc), in_specs=in_specs, out_specs=out_specs, out_shape=out_shape,
                          scratch_shapes=[pltpu.VMEM((SSM_STATE, GROUP_W), F32)], name=name,
                          compiler_params=_params(("parallel", "arbitrary")))(
        xbc, xbc, xbc, z, dtp_g, dtp_t, bias_g, bias_r, alog_g, alog_r, d_exp, norm_w)


def _ssd_bwd(dyn, y, xbc, z, states, dtp_g, dtp_t, bias_g, bias_r, alog_g, alog_r, d_exp, norm_w, name):
    t = xbc.shape[0]
    nc = t // CHUNK

    def body(dyn_ref, y_ref, xs_ref, b_ref, c_ref, z_ref, st_ref, dtp_ref, dtpt_ref, bias_ref, biasr_ref, alog_ref,
             alogr_ref, dexp_ref, nw_ref, dxs_ref, db_ref, dc_ref, dz_ref, ddt_ref, hsum_ref, csum_ref, dstate):
        step = pl.program_id(1)

        @pl.when(step == 0)
        def _():
            dstate[...] = jnp.zeros_like(dstate)

        dtp = dtp_ref[...]
        bias = bias_ref[...]
        lower, upper, dt, a_neg, cs, csr = _ssd_common(dtp, dtpt_ref[...], bias, biasr_ref[...], alog_ref[...],
                                                       alogr_ref[...])
        cs_e = _head_expand(cs)
        dt_e = _head_expand(dt)
        xs = xs_ref[...]
        xdt = xs * dt_e
        bm = b_ref[...]
        cm = c_ref[...]
        bmb, cmb = bm.astype(BF16), cm.astype(BF16)
        y = y_ref[...]
        dexp = dexp_ref[...]
        nw = nw_ref[...]

        zz = z_ref[...]
        sg = _sigmoid(zz)
        gate = zz * sg
        ytot = y + dexp * xs
        y2 = ytot * gate
        r = lax.rsqrt(jnp.mean(y2 * y2, axis=-1, keepdims=True) + EPS)
        dynv = dyn_ref[...]
        xh = y2 * r
        gn = dynv * nw
        dy2 = r * (gn - xh * jnp.mean(gn * xh, axis=-1, keepdims=True))
        dy = dy2 * gate
        dz_ref[...] = (dy2 * ytot * (sg * (1.0 + zz * (1.0 - sg)))).astype(BF16)
        csum_part = jnp.sum(dynv * xh, axis=0, keepdims=True)

        cb = _dot_nt(cmb, bmb)
        dyb = dy.astype(BF16)
        xdtb = xdt.astype(BF16)
        dym = _head_masked_rows(dyb)
        dm = _dot_nt(dym, xdtb)
        dmt = _dot_nt(_head_masked_rows(xdtb), dyb)
        lane = lax.broadcasted_iota(jnp.int32, (CHUNK, 128), 1)
        mts = []
        dcb = jnp.zeros((CHUNK, CHUNK), F32)
        dcs = jnp.zeros((CHUNK, 128), F32)
        for j in range(8):
            dlt = cs_e[:, SSM_HEAD_DIM * j:SSM_HEAD_DIM * j + 1] - csr[j:j + 1, :]
            lj = jnp.exp(jnp.where(lower, dlt, -jnp.inf))
            mj = cb * lj
            mjt = mj.T
            mts.append(mjt.astype(BF16))
            dmj = dm[CHUNK * j:CHUNK * (j + 1)]
            dcb = dcb + dmj * lj
            rows = jnp.sum(dmj * mj, axis=1, keepdims=True)
            cols = jnp.sum(dmt[CHUNK * j:CHUNK * (j + 1)] * mjt, axis=1, keepdims=True)
            dcs = dcs + jnp.where(lane == j, rows - cols, 0.0)
        dxdt = _dot(jnp.concatenate(mts, axis=1), dym)
        dst_out = dstate[...]
        dst_outb = dst_out.astype(BF16)
        st_in = st_ref[...]
        st_inb = st_in.astype(BF16)
        cs_last = cs_e[CHUNK - 1:CHUNK, :]
        decay = jnp.exp(cs_last - cs_e)
        e_last = jnp.exp(cs_last)
        gpart = decay * _dot(bmb, dst_outb)
        dxdt = dxdt + gpart
        dyw = (jnp.exp(cs_e) * dy).astype(BF16)
        dcbb = dcb.astype(BF16)
        dc_ref[...] = _dot_nt(dyw, st_inb) + _dot(dcbb, bmb)
        db_ref[...] = _dot_nt((xdt * decay).astype(BF16), dst_outb) + _dot(dcb.T.astype(BF16), cmb)
        dstate[...] = e_last * dst_out + _dot(cm.T.astype(BF16), dyw)
        y_off = jnp.exp(cs_e) * _dot(cmb, st_inb)
        xg = xdt * gpart
        vec = jnp.concatenate([jnp.sum(dy * xs, axis=0, keepdims=True),
                               jnp.sum(xg + dst_out * e_last * st_in, axis=0, keepdims=True),
                               jnp.zeros((14, GROUP_W), F32)], axis=0)
        s_cs, s_dt, s_vec = _head_sum([dy * y_off - xg, dxdt * xs, vec])
        d_skip = s_vec[0:1]
        ri = lax.broadcasted_iota(jnp.int32, (CHUNK, 128), 0)
        dcs = dcs + s_cs + jnp.where(ri == CHUNK - 1, s_vec[1:2], 0.0)
        da = _tri_left(upper, dcs)
        ddt = da * a_neg + s_dt
        dxs_ref[...] = dxdt * dt_e + dy * dexp
        ddtp = ddt * _sigmoid(dtp + bias)
        ddt_ref[...] = ddtp
        d_alog = jnp.sum(da * dt, axis=0, keepdims=True) * a_neg
        hpart = jnp.concatenate([jnp.sum(ddtp, axis=0, keepdims=True), d_alog, d_skip, jnp.zeros((5, 128), F32)], axis=0)
        cpart = jnp.concatenate([csum_part, jnp.zeros((7, GROUP_W), F32)], axis=0)

        @pl.when(step == 0)
        def _():
            hsum_ref[...] = hpart
            csum_ref[...] = cpart

        @pl.when(step > 0)
        def _():
            hsum_ref[...] += hpart
            csum_ref[...] += cpart

    rc = lambda c: nc - 1 - c
    gw = pl.BlockSpec((CHUNK, GROUP_W), lambda g, c: (rc(c), g))
    bsp = pl.BlockSpec((CHUNK, SSM_STATE), lambda g, c: (rc(c), D_INNER // SSM_STATE + g))
    csp = pl.BlockSpec((CHUNK, SSM_STATE), lambda g, c: (rc(c), D_INNER // SSM_STATE + SSM_GROUPS + g))
    in_specs = [
        gw, gw, gw, bsp, csp, gw,
        pl.BlockSpec((None, SSM_STATE, GROUP_W), lambda g, c: (rc(c), 0, g)),
        pl.BlockSpec((None, CHUNK, 128), lambda g, c: (g, rc(c), 0)),
        pl.BlockSpec((8, CHUNK), lambda g, c: (g, rc(c))),
        pl.BlockSpec((None, 1, 128), lambda g, c: (g, 0, 0)),
        pl.BlockSpec((8, 128), lambda g, c: (g, 0)),
        pl.BlockSpec((None, 1, 128), lambda g, c: (g, 0, 0)),
        pl.BlockSpec((8, 128), lambda g, c: (g, 0)),
        pl.BlockSpec((1, GROUP_W), lambda g, c: (0, g)),
        pl.BlockSpec((1, GROUP_W), lambda g, c: (0, g)),
    ]
    nsp = pl.BlockSpec((CHUNK, SSM_STATE), lambda g, c: (rc(c), g))
    out_specs = [gw, nsp, nsp, gw,
                 pl.BlockSpec((None, CHUNK, 128), lambda g, c: (g, rc(c), 0)),
                 pl.BlockSpec((None, 8, 128), lambda g, c: (g, 0, 0)),
                 pl.BlockSpec((8, GROUP_W), lambda g, c: (0, g))]
    gn = SSM_GROUPS * SSM_STATE
    out_shape = [SDS((t, D_INNER), F32), SDS((t, gn), F32), SDS((t, gn), F32), SDS((t, D_INNER), BF16),
                 SDS((SSM_GROUPS, t, 128), F32), SDS((SSM_GROUPS, 8, 128), F32), SDS((8, D_INNER), F32)]
    return pl.pallas_call(body, grid=(SSM_GROUPS, nc), in_specs=in_specs, out_specs=out_specs, out_shape=out_shape,
                          scratch_shapes=[pltpu.VMEM((SSM_STATE, GROUP_W), F32)], name=name,
                          compiler_params=_params(("parallel", "arbitrary")))(
        dyn, y, xbc, xbc, xbc, z, states, dtp_g, dtp_t, bias_g, bias_r, alog_g, alog_r, d_exp, norm_w)


BLK = 128
HEAD_PAIRS = ATTN_W // 128
ATTN_SCALE = 0.125
SPAN_BLOCKS = {1: 8, 4: 2, 16: 1}
QKV_LANE_BLOCKS = ATTN_W * len(ATTN_GROUPS) // 128


def _slope_table(group):
    n = len(ATTN_GROUPS) * 16
    tbl = [[2.0 ** (-8.0 * (16 * group + 2 * p + s + 1) / n) if s < 2 else 0.0 for s in range(128)] for p in range(HEAD_PAIRS)]
    return jnp.asarray(tbl, F32)


def _lane_lo(rows):
    return lax.broadcasted_iota(jnp.int32, (rows, 128), 1) < 64


def _rows(start, dil):
    return pl.ds(start, BLK, stride=dil) if dil > 1 else pl.ds(start, BLK)


def _stack_heads(x):
    lo = _lane_lo(BLK)
    return jnp.concatenate([jnp.where(lo, x, jnp.zeros_like(x)), jnp.where(lo, jnp.zeros_like(x), x)], axis=0)


def _pair_cols(x):
    return jnp.concatenate([x[:, 0:1], x[:, 64:65]], axis=0)


def _attn_bias(sl, dil, first_span, last_span):
    qi = lax.broadcasted_iota(jnp.int32, (BLK, 2 * BLK), 0)
    kj = lax.broadcasted_iota(jnp.int32, (BLK, 2 * BLK), 1)
    dist = qi + BLK - kj
    valid = (dist >= 0) & (dist <= BLK)
    distf = dist.astype(F32) * float(dil)

    def stacked(ok, d):
        return jnp.concatenate([jnp.where(ok, -sl[:, h:h + 1] * d, -jnp.inf) for h in range(2)], axis=0)

    first = stacked(valid & ((kj >= BLK) | jnp.logical_not(first_span)), distf)
    after = None
    if last_span is not None:
        after = stacked((kj[:, :BLK] >= qi[:, :BLK]) & jnp.logical_not(last_span), distf[:, :BLK])
    return first, stacked(valid, distf), after


def _attn_specs(group, t):
    _, dil = ATTN_GROUPS[group]
    nblk = SPAN_BLOCKS[dil]
    span, edge = BLK * dil * nblk, BLK * dil
    per = span // edge

    def cur(which):
        off = 0 if which is None else which * QKV_LANE_BLOCKS + group * HEAD_PAIRS
        return pl.BlockSpec((span, 128), lambda s, p, *_: (s, off + p))

    def before(which):
        off = which * QKV_LANE_BLOCKS + group * HEAD_PAIRS
        return pl.BlockSpec((edge, 128), lambda s, p, *_: (jnp.maximum(s * per - 1, 0), off + p))

    def after(which):
        off = 0 if which is None else which * QKV_LANE_BLOCKS + group * HEAD_PAIRS
        return pl.BlockSpec((edge, 128), lambda s, p, *_: (jnp.minimum((s + 1) * per, t // edge - 1), off + p))

    slopes = pl.BlockSpec((HEAD_PAIRS, 128), lambda s, p, *_: (0, 0))
    return dil, nblk, span, cur, before, after, slopes


def _attn_fwd(qkv, group, name):
    t = qkv.shape[0]
    dil, nblk, span, cur, before, after, slopes = _attn_specs(group, t)

    def body(q_ref, k_ref, v_ref, kp_ref, vp_ref, sl_ref, o_ref, l_ref):
        first_span = pl.program_id(0) == 0
        bias_first, bias_mid, _ = _attn_bias(sl_ref[pl.ds(pl.program_id(1), 1), :], dil, first_span, None)
        lo_q, lo_k = _lane_lo(BLK), _lane_lo(2 * BLK)
        for r in range(dil):
            kp, vp = kp_ref[_rows(r, dil), :].astype(BF16), vp_ref[_rows(r, dil), :].astype(BF16)
            for b in range(nblk):
                rows = _rows(b * BLK * dil + r, dil)
                kc, vc = k_ref[rows, :].astype(BF16), v_ref[rows, :].astype(BF16)
                k2 = jnp.concatenate([kp, kc], axis=0)
                v2 = jnp.concatenate([vp, vc], axis=0)
                kp, vp = kc, vc
                s = _dot_nt(_stack_heads((q_ref[rows, :] * ATTN_SCALE).astype(BF16)), k2) + (bias_first if b == 0 else bias_mid)
                mx = jnp.max(s, axis=-1, keepdims=True)
                p = jnp.exp(s - mx)
                den = jnp.sum(p, axis=-1, keepdims=True)
                pb = p.astype(BF16)
                o_ref[rows, :] = _dot(pb[:BLK], jnp.where(lo_k, v2, jnp.zeros_like(v2))) / den[:BLK] + \
                    _dot(pb[BLK:], jnp.where(lo_k, jnp.zeros_like(v2), v2)) / den[BLK:]
                lse = mx + jnp.log(den)
                l_ref[rows, :] = jnp.where(lo_q, lse[:BLK], lse[BLK:])

    in_specs = [cur(0), cur(1), cur(2), before(1), before(2), slopes]
    return pl.pallas_call(body, grid=(t // span, HEAD_PAIRS), in_specs=in_specs, out_specs=[cur(None), cur(None)],
                          out_shape=[SDS((t, ATTN_W), F32), SDS((t, ATTN_W), F32)], name=name,
                          compiler_params=_params(("parallel", "parallel")))(qkv, qkv, qkv, qkv, qkv, _slope_table(group))


def _attn_bwd(qkv, do, lse, dvec, dqkv, group, name):
    t = qkv.shape[0]
    dil, nblk, span, cur, before, after, slopes = _attn_specs(group, t)
    nspan = t // span

    def body(q_ref, k_ref, v_ref, kp_ref, vp_ref, qn_ref, do_ref, l_ref, d_ref, don_ref, ln_ref, dn_ref, sl_ref, *rest):
        out_ref, dk_s, dv_s = rest[-3:]
        span_id, pair_id, which = pl.program_id(0), pl.program_id(1), pl.program_id(2)

        @pl.when(which == 0)
        def _():
            bias_first, bias_mid, bias_next = _attn_bias(sl_ref[pl.ds(pair_id, 1), :], dil, span_id == 0, span_id == nspan - 1)
            lo_k = _lane_lo(2 * BLK)
            for r in range(dil):
                kp, vp = kp_ref[_rows(r, dil), :].astype(BF16), vp_ref[_rows(r, dil), :].astype(BF16)
                held = None
                for b in range(nblk + 1):
                    last = b == nblk
                    rows = _rows(r if last else b * BLK * dil + r, dil)
                    qs, dos, ls, dvs = (qn_ref, don_ref, ln_ref, dn_ref) if last else (q_ref, do_ref, l_ref, d_ref)
                    qst = _stack_heads((qs[rows, :] * ATTN_SCALE).astype(BF16))
                    dost = _stack_heads(dos[rows, :].astype(BF16))
                    if last:
                        k2, v2, bias = kp, vp, bias_next
                    else:
                        kc, vc = k_ref[rows, :].astype(BF16), v_ref[rows, :].astype(BF16)
                        k2, v2 = jnp.concatenate([kp, kc], axis=0), jnp.concatenate([vp, vc], axis=0)
                        kp, vp = kc, vc
                        bias = bias_first if b == 0 else bias_mid
                    p = jnp.exp(_dot_nt(qst, k2) + bias - _pair_cols(ls[rows, :]))
                    ds = (p * (_dot_nt(dost, v2) - _pair_cols(dvs[rows, :]))).astype(BF16)
                    dk2 = _dot_tn(ds, qst)
                    dv2 = _dot_tn(p.astype(BF16), dost)
                    if held is not None:
                        dk_s[held[0], :] = held[1] + dk2[:BLK]
                        dv_s[held[0], :] = held[2] + dv2[:BLK]
                    if not last:
                        k_heads = jnp.concatenate([jnp.where(lo_k, k2, jnp.zeros_like(k2)),
                                                   jnp.where(lo_k, jnp.zeros_like(k2), k2)], axis=0)
                        out_ref[rows, :] = _dot(jnp.concatenate([ds[:BLK], ds[BLK:]], axis=1), k_heads) * ATTN_SCALE
                        held = (rows, dk2[BLK:], dv2[BLK:])

        @pl.when(which == 1)
        def _():
            out_ref[...] = dk_s[...]

        @pl.when(which == 2)
        def _():
            out_ref[...] = dv_s[...]

    off = group * HEAD_PAIRS
    out_spec = pl.BlockSpec((span, 128), lambda s, p, w: (s, w * QKV_LANE_BLOCKS + off + p))
    in_specs = [cur(0), cur(1), cur(2), before(1), before(2), after(0), cur(None), cur(None), cur(None),
                after(None), after(None), after(None), slopes]
    args = [qkv, qkv, qkv, qkv, qkv, qkv, do, lse, dvec, do, lse, dvec, _slope_table(group)]
    aliases = {}
    if dqkv is not None:
        in_specs.append(pl.BlockSpec(memory_space=pl.ANY))
        args.append(dqkv)
        aliases = {len(args) - 1: 0}
    return pl.pallas_call(body, grid=(nspan, HEAD_PAIRS, 3), in_specs=in_specs, out_specs=out_spec,
                          out_shape=SDS((t, QKV_DIM), F32), input_output_aliases=aliases,
                          scratch_shapes=[pltpu.VMEM((span, 128), F32), pltpu.VMEM((span, 128), F32)], name=name,
                          compiler_params=_params(("parallel", "parallel", "arbitrary")))(*args)


def _combine_weights(l0, l1, l2):
    mx = jnp.maximum(jnp.maximum(l0, l1), l2)
    e0, e1, e2 = jnp.exp(l0 - mx), jnp.exp(l1 - mx), jnp.exp(l2 - mx)
    den = e0 + e1 + e2
    return e0 / den, e1 / den, e2 / den


def _combine_fwd(os_, ls_, name):
    t = os_[0].shape[0]

    def body(o0, o1, o2, l0, l1, l2, out):
        w0, w1, w2 = _combine_weights(l0[...], l1[...], l2[...])
        out[...] = (w0 * o0[...] + w1 * o1[...] + w2 * o2[...]).astype(BF16)

    row = pl.BlockSpec((ROW_TILE, ATTN_W), lambda i: (i, 0))
    return pl.pallas_call(body, grid=(t // ROW_TILE,), in_specs=[row] * 6, out_specs=row, out_shape=SDS((t, ATTN_W), BF16),
                          name=name, compiler_params=_params(("parallel",)))(*os_, *ls_)


def _combine_bwd(do, os_, ls_, name):
    t = do.shape[0]

    def body(do_ref, o0, o1, o2, l0, l1, l2, g0, g1, g2, d0, d1, d2):
        w0, w1, w2 = _combine_weights(l0[...], l1[...], l2[...])
        dov = do_ref[...]
        prod = dov * (w0 * o0[...] + w1 * o1[...] + w2 * o2[...])
        r = (lax.broadcasted_iota(jnp.int32, (3 * 128, 128), 0) % 128) // 64
        c = lax.broadcasted_iota(jnp.int32, (3 * 128, 128), 1) // 64
        same = (r == c).astype(BF16)
        tbar = jnp.concatenate([_dot(jnp.concatenate(_pieces(prod[:, 128 * k:128 * k + 128], 3), axis=1), same)
                                for k in range(HEAD_PAIRS)], axis=1)
        for w, g, d in ((w0, g0, d0), (w1, g1, d1), (w2, g2, d2)):
            g[...] = w * dov
            d[...] = w * tbar

    row = pl.BlockSpec((ROW_TILE, ATTN_W), lambda i: (i, 0))
    return pl.pallas_call(body, grid=(t // ROW_TILE,), in_specs=[row] * 7, out_specs=[row] * 6,
                          out_shape=[SDS((t, ATTN_W), F32)] * 6, name=name,
                          compiler_params=_params(("parallel",)))(do, *os_, *ls_)


def _peer(k):
    x, y, c = lax.axis_index("x"), lax.axis_index("y"), lax.axis_index("c")
    px = 1 - x if k & 4 else x
    py = 1 - y if k & 2 else y
    pc = 1 - c if k & 1 else c
    return (px, py, pc), 4 * px + 2 * py + pc


def _my_index():
    return 4 * lax.axis_index("x") + 2 * lax.axis_index("y") + lax.axis_index("c")


def _exchange_sems(n):
    return [pltpu.SemaphoreType.DMA((n * (NDEV - 1),)), pltpu.SemaphoreType.DMA((n * (NDEV - 1),)),
            pltpu.SemaphoreType.DMA((n,))]


def _exchange_copies(kind, ins, outs, send, recv, local, arrivals):
    me = _my_index()
    own, sent, arriving = [], [], []
    for i in range(len(ins)):
        own.append(pltpu.make_async_copy(ins[i] if kind == "gather" else ins[i].at[me], outs[i].at[me], local.at[i]))
        for k in range(1, NDEV):
            peer, pidx = _peer(k)
            s = i * (NDEV - 1) + k - 1
            src = ins[i] if kind == "gather" else ins[i].at[pidx]
            for dst, into in ((outs[i].at[me], sent), (outs[i].at[pidx], arriving)):
                if into is sent or arrivals:
                    into.append(pltpu.make_async_remote_copy(src_ref=src, dst_ref=dst, send_sem=send.at[s],
                                                             recv_sem=recv.at[s], device_id=peer, device_id_type=MESH))
    return own, sent, arriving


def _exchange_start(kind, ins, outs, send, recv, local):
    own, sent, _ = _exchange_copies(kind, ins, outs, send, recv, local, arrivals=False)
    for cp in own + sent:
        cp.start()


def _exchange_wait(kind, ins, outs, send, recv, local):
    own, sent, arriving = _exchange_copies(kind, ins, outs, send, recv, local, arrivals=True)
    for cp in sent:
        cp.wait_send()
    for cp in arriving:
        cp.wait_recv()
    for cp in own:
        cp.wait()


def _gather_two_level(tensors, name):
    n = len(tensors)

    def body(*refs):
        ins, outs = refs[:n], refs[n:2 * n]
        send, recv, local = refs[2 * n:]
        x, y, c = lax.axis_index("x"), lax.axis_index("y"), lax.axis_index("c")
        sibling = (x, y, 1 - c)
        chips = [(1 - x, y), (x, 1 - y), (1 - x, 1 - y)]

        def slot(px, py, pc):
            return 4 * px + 2 * py + pc

        def copy(i, k, block, to, src=None):
            return pltpu.make_async_remote_copy(src_ref=outs[i].at[block] if src is None else src, dst_ref=outs[i].at[block],
                                                send_sem=send.at[7 * i + k], recv_sem=recv.at[7 * i + k],
                                                device_id=to, device_id_type=MESH)

        me = slot(x, y, c)
        own = [pltpu.make_async_copy(ins[i], outs[i].at[me], local.at[i]) for i in range(n)]
        first = [copy(i, 0, me, sibling, src=ins[i]) for i in range(n)]
        first += [copy(i, 1 + j, me, (*chip, c), src=ins[i]) for i in range(n) for j, chip in enumerate(chips)]
        for cp in own + first:
            cp.start()
        passed = []
        for i in range(n):
            for j, chip in enumerate(chips):
                copy(i, 1 + j, slot(*chip, c), (x, y, c)).wait_recv()
                fwd = copy(i, 4 + j, slot(*chip, c), sibling)
                fwd.start()
                passed.append(fwd)
        for i in range(n):
            copy(i, 0, slot(x, y, 1 - c), (x, y, c)).wait_recv()
            for j, chip in enumerate(chips):
                copy(i, 4 + j, slot(*chip, 1 - c), (x, y, c)).wait_recv()
        for cp in first + passed:
            cp.wait_send()
        for cp in own:
            cp.wait()

    hbm = pl.BlockSpec(memory_space=pl.ANY)
    return pl.pallas_call(body, in_specs=[hbm] * n, out_specs=[hbm] * n,
                          out_shape=[SDS((NDEV,) + t.shape, t.dtype) for t in tensors],
                          scratch_shapes=_exchange_sems(n), name=name)(*tensors)


def _all_reduce_small(v, name):
    rows = v.shape[0]

    def body(v_ref, out_ref, land, send, recv):
        me = _my_index()
        land[me] = v_ref[...]
        remote = []
        for k in range(1, NDEV):
            peer, _ = _peer(k)
            cp = pltpu.make_async_remote_copy(src_ref=v_ref, dst_ref=land.at[me], send_sem=send.at[k - 1],
                                              recv_sem=recv.at[k - 1], device_id=peer, device_id_type=MESH)
            cp.start()
            remote.append(cp)
        for cp in remote:
            cp.wait_send()
        for k in range(1, NDEV):
            peer, pidx = _peer(k)
            pltpu.make_async_remote_copy(src_ref=v_ref, dst_ref=land.at[pidx], send_sem=send.at[k - 1],
                                         recv_sem=recv.at[k - 1], device_id=peer, device_id_type=MESH).wait_recv()
        total = land[0]
        for d in range(1, NDEV):
            total = total + land[d]
        out_ref[...] = total

    vm = pl.BlockSpec(memory_space=pltpu.VMEM)
    return pl.pallas_call(
        body, in_specs=[vm], out_specs=vm, out_shape=SDS((rows, 128), F32),
        scratch_shapes=[pltpu.VMEM((NDEV, rows, 128), F32), pltpu.SemaphoreType.DMA((NDEV - 1,)),
                        pltpu.SemaphoreType.DMA((NDEV - 1,))],
        name=name)(v)


def _adamw_math(w, g, m, v):
    m = ADAM_B1 * m + (1.0 - ADAM_B1) * g
    v = ADAM_B2 * v + (1.0 - ADAM_B2) * (g * g)
    m_hat = m / (1.0 - ADAM_B1 ** ADAM_STEP)
    v_hat = v / (1.0 - ADAM_B2 ** ADAM_STEP)
    delta = -ADAM_LR * (m_hat / (jnp.sqrt(v_hat) + ADAM_EPS) + ADAM_WD * w)
    return delta, m, v


def _row_tile(rows, cols):
    tr = rows
    while tr * cols * 4 > (1 << 20) and tr % 16 == 0:
        tr //= 2
    return tr


def _adamw(g, w, m, v, name):
    rows, cols = w.shape
    tr = _row_tile(rows, cols)

    def body(g_ref, w_ref, m_ref, v_ref, d_out, m_out, v_out):
        d, mn, vn = _adamw_math(w_ref[...], g_ref[...], m_ref[...], v_ref[...])
        d_out[...] = d
        m_out[...] = mn
        v_out[...] = vn

    sp = pl.BlockSpec((tr, cols), lambda i: (i, 0))
    return pl.pallas_call(body, grid=(rows // tr,), in_specs=[sp] * 4, out_specs=[sp] * 3,
                          out_shape=[SDS((rows, cols), F32)] * 3, name=name, compiler_params=_params(("parallel",)))(g, w, m, v)


def _reduce_adamw(parts, w, m, v, name):
    rows, cols = w.shape
    tr = _row_tile(parts[0].shape[1], cols)
    tiles = [p.shape[1] // tr for p in parts]
    starts = [sum(tiles[:h]) for h in range(len(parts))]

    def body(*refs):
        p_refs = refs[:len(parts)]
        w_ref, m_ref, v_ref, g_out, d_out, m_out, v_out = refs[len(parts):]
        i = pl.program_id(0)
        for h, p_ref in enumerate(p_refs):
            @pl.when((i >= starts[h]) & (i < starts[h] + tiles[h]))
            def _(p_ref=p_ref):
                g = p_ref[0].astype(F32)
                for d in range(1, NDEV):
                    g = g + p_ref[d].astype(F32)
                g_out[...] = g
                dl, mn, vn = _adamw_math(w_ref[...], g, m_ref[...], v_ref[...])
                d_out[...] = dl
                m_out[...] = mn
                v_out[...] = vn

    sp = pl.BlockSpec((tr, cols), lambda i: (i, 0))
    psp = [pl.BlockSpec((NDEV, tr, cols), lambda i, h=h: (0, jnp.clip(i - starts[h], 0, tiles[h] - 1), 0))
           for h in range(len(parts))]
    return pl.pallas_call(body, grid=(rows // tr,), in_specs=psp + [sp, sp, sp], out_specs=[sp] * 4,
                          out_shape=[SDS((rows, cols), F32)] * 4, name=name,
                          compiler_params=_params(("parallel",)))(*parts, w, m, v)


def _pack(items):
    rows = []
    for a in items:
        a = a.reshape(-1).astype(F32)
        pad = (-a.shape[0]) % 128
        rows.append(jnp.pad(a, (0, pad)).reshape(-1, 128))
    out = jnp.concatenate(rows, axis=0)
    return jnp.pad(out, ((0, (-out.shape[0]) % 8), (0, 0)))


def _unpack(packed, shapes):
    out, r = [], 0
    for shp in shapes:
        n = math.prod(shp)
        nr = -(-n // 128)
        out.append(packed[r:r + nr].reshape(-1)[:n].reshape(shp))
        r += nr
    return out


def _ident(t):
    return (t,)


def _add(t, res):
    return (t + res,)


def kernel(x, norm_mix, norm_mlp, ssm_w_in, ssm_conv_w, ssm_conv_b, ssm_dt_bias, ssm_a_log, ssm_d, ssm_norm_w, ssm_w_out, attn_w_qkv, attn_w_o, mlp_w1, mlp_w2, final_norm, loss_target, m_norm_mix, m_norm_mlp, m_ssm_w_in, m_ssm_conv_w, m_ssm_conv_b, m_ssm_dt_bias, m_ssm_a_log, m_ssm_d, m_ssm_norm_w, m_ssm_w_out, m_attn_w_qkv, m_attn_w_o, m_mlp_w1, m_mlp_w2, m_final_norm, v_norm_mix, v_norm_mlp, v_ssm_w_in, v_ssm_conv_w, v_ssm_conv_b, v_ssm_dt_bias, v_ssm_a_log, v_ssm_d, v_ssm_norm_w, v_ssm_w_out, v_attn_w_qkv, v_attn_w_o, v_mlp_w1, v_mlp_w2, v_final_norm):
    t = x.shape[1]
    x0 = x.reshape(t, D_MODEL)
    tgt = loss_target.reshape(t, D_MODEL)
    me = _my_index()
    in_dim = D_INNER + CONV_DIM + SSM_HEADS
    in_shard = in_dim // NDEV
    zx_dim = D_INNER + CONV_DIM

    s_out, s_qkv, s_o = ssm_w_out[0].astype(BF16), attn_w_qkv[0].astype(BF16), attn_w_o[0].astype(BF16)
    s_w1, s_w2 = mlp_w1.astype(BF16), mlp_w2.astype(BF16)
    g_in, g_cw = _gather_two_level([ssm_w_in[0].astype(BF16), ssm_conv_w[0]], "gather_in_proj")
    w_in = jnp.transpose(g_in, (1, 0, 2)).reshape(D_MODEL, in_dim)
    w_z, w_x = w_in[:, :D_INNER], w_in[:, D_INNER:zx_dim]
    w_dt = jnp.pad(w_in[:, zx_dim:], ((0, 0), (0, 128 - SSM_HEADS)))
    conv_w = jnp.transpose(g_cw, (1, 0, 2)).reshape(CONV_WIDTH, CONV_DIM)
    g_w1, g_w2 = [None, None], [None, None]

    def lanes(p):
        return jnp.pad(p.reshape(SSM_GROUPS, 1, 8), ((0, 0), (0, 0), (0, 120)))

    def rows(p):
        return jnp.broadcast_to(p.reshape(SSM_HEADS, 1), (SSM_HEADS, 128))

    bias_g, bias_r = lanes(ssm_dt_bias[0]), rows(ssm_dt_bias[0])
    alog_g, alog_r = lanes(ssm_a_log[0]), rows(ssm_a_log[0])
    d_exp = jnp.repeat(ssm_d[0], SSM_HEAD_DIM).reshape(1, D_INNER)
    norm_w = ssm_norm_w

    def relu2(tot):
        r = jnp.maximum(tot, 0.0)
        return r, r * r

    def mlp_fwd(xin, layer, tag, down_carry):
        h = _rmsnorm_fwd(xin, norm_mlp[layer:layer + 1], f"norm_mlp{tag}")
        r, a, w2 = _mm_fwd(h, g_w1[layer], "cols", D_MODEL, D_FF, epi=relu2, outs=(BF16, BF16),
                           carry=("gather", [s_w2[layer]]), name=f"mlp_up{tag}")
        g_w2[layer] = w2.reshape(D_FF, D_MODEL)
        xout, *got = _mm_fwd(a, g_w2[layer], "plain", D_FF, D_MODEL, epi=_add, outs=(F32,), extras=(xin,),
                             carry=("gather", down_carry) if down_carry else None, name=f"mlp_down{tag}")
        return h, r, a, xout, got

    h0 = _rmsnorm_fwd(x0, norm_mix[0:1], "norm_mix0")
    z, g_out = _mm_fwd(h0, w_z, "plain", D_MODEL, D_INNER, epi=_ident, outs=(F32,), carry=("gather", [s_out]), name="ssm_in_z")
    xpre, g_w1[0] = _mm_fwd(h0, w_x, "plain", D_MODEL, CONV_DIM, epi=_ident, outs=(F32,), carry=("gather", [s_w1[0]]),
                            name="ssm_in_x")
    dtp, = _mm_fwd(h0, w_dt, "plain", D_MODEL, 128, epi=_ident, outs=(F32,), name="ssm_in_dt")
    xbc = _conv_fwd(xpre, conv_w, ssm_conv_b, "conv_fwd")
    dtp64 = dtp[:, :SSM_HEADS]
    dtp_g = jnp.pad(jnp.transpose(dtp64.reshape(t, SSM_GROUPS, 8), (1, 0, 2)), ((0, 0), (0, 0), (0, 120)))
    dtp_t = jnp.transpose(dtp64)
    y_ssd, yn, states = _ssd_fwd(xbc, z, dtp_g, dtp_t, bias_g, bias_r, alog_g, alog_r, d_exp, norm_w, "ssd_fwd")
    g_out = g_out.reshape(D_INNER, D_MODEL)
    x1, g_o = _mm_fwd(yn, g_out, "plain", D_INNER, D_MODEL, epi=_add, outs=(F32,), extras=(x0,), carry=("gather", [s_o]),
                      name="ssm_out")
    h1, r1, a1, x2, (g_qkv,) = mlp_fwd(x1, 0, "0", [s_qkv])

    h2 = _rmsnorm_fwd(x2, norm_mix[1:2], "norm_mix1")
    qkv, g_w1[1] = _mm_fwd(h2, g_qkv, "cols", D_MODEL, QKV_DIM, epi=_ident, outs=(F32,), tn=QKV_DIM // NDEV,
                           carry=("gather", [s_w1[1]]), name="attn_qkv")
    att = [_attn_fwd(qkv, g, f"attn_fwd{g}") for g in range(3)]
    os_, ls_ = [a[0] for a in att], [a[1] for a in att]
    o_mix = _combine_fwd(os_, ls_, "attn_combine")
    x3, = _mm_fwd(o_mix, g_o, "cols", ATTN_W, D_MODEL, epi=_add, outs=(F32,), extras=(x2,), tn=D_MODEL // NDEV, name="attn_out")
    h3, r3, a3, x4, _ = mlp_fwd(x3, 1, "1", None)

    dx4, dx4b, loss_acc, d_final = _loss_head(x4, tgt, final_norm.reshape(1, D_MODEL), "loss_head")

    def mlp_bwd(xin, h, r, a, dxo, dxob, layer, tag):
        du, = _mm_dx(dxob, g_w2[layer], "plain", D_FF, D_MODEL, epi=lambda tot, rr: (tot * (2.0 * rr.astype(F32)),),
                     outs=(BF16,), extras=(r,), name=f"mlp_down_dx{tag}")
        dw2, = _mm_dw(a, dxob, "plain", name=f"mlp_down_dw{tag}")
        dw2 = dw2.reshape(NDEV, D_FF // NDEV, D_MODEL)
        dw1, p_dw2 = _mm_dw(h, du, "cols", carry=("scatter", [dw2]), name=f"mlp_up_dw{tag}")
        dh, p_dw1 = _mm_dx(du, g_w1[layer], "cols", D_MODEL, D_FF, epi=_ident, outs=(F32,), carry=("scatter", [dw1]),
                           name=f"mlp_up_dx{tag}")
        dxi, dxib, dg = _rmsnorm_bwd(xin, norm_mlp[layer:layer + 1], dh, dxo, f"norm_mlp_bwd{tag}")
        return dxi, dxib, dg, p_dw1, p_dw2

    dx3, dx3b, dg_mlp1, p_w1_1, p_w2_1 = mlp_bwd(x3, h3, r3, a3, dx4, dx4b, 1, "1")

    dw_o, = _mm_dw(o_mix, dx3b, "cols", tn=D_MODEL // NDEV, name="attn_out_dw")
    do, p_o = _mm_dx(dx3b, g_o, "cols", ATTN_W, D_MODEL, epi=_ident, outs=(F32,), tk=D_MODEL // NDEV,
                     carry=("scatter", [dw_o]), name="attn_out_dx")
    cb = _combine_bwd(do, os_, ls_, "attn_combine_bwd")
    dos, dvecs = cb[:3], cb[3:]
    dqkv = None
    for g in range(3):
        dqkv = _attn_bwd(qkv, dos[g], ls_[g], dvecs[g], dqkv, g, f"attn_bwd{g}")
    dw_qkv, = _mm_dw(h2, dqkv, "cols", tn=QKV_DIM // NDEV, tk=1024, name="attn_qkv_dw")
    dh2, p_qkv = _mm_dx(dqkv, g_qkv, "cols", D_MODEL, QKV_DIM, epi=_ident, outs=(F32,), tk=QKV_DIM // NDEV,
                        carry=("scatter", [dw_qkv]), name="attn_qkv_dx")
    dx2, dx2b, dg_mix1 = _rmsnorm_bwd(x2, norm_mix[1:2], dh2, dx3, "norm_mix_bwd1")

    dx1, dx1b, dg_mlp0, p_w1_0, p_w2_0 = mlp_bwd(x1, h1, r1, a1, dx2, dx2b, 0, "0")

    dw_out, = _mm_dw(yn, dx1b, "plain", name="ssm_out_dw")
    dw_out = dw_out.reshape(NDEV, D_INNER // NDEV, D_MODEL)
    dyn, p_out = _mm_dx(dx1b, g_out, "plain", D_INNER, D_MODEL, epi=_ident, outs=(F32,), carry=("scatter", [dw_out]),
                        name="ssm_out_dx")
    dxs, d_b, d_c, dz, ddtp_g, hsums, csums = _ssd_bwd(dyn, y_ssd, xbc, z, states, dtp_g, dtp_t, bias_g, bias_r,
                                                       alog_g, alog_r, d_exp, norm_w, "ssd_bwd")
    dxbc = jnp.concatenate([dxs, d_b, d_c], axis=1)
    du, conv_sums = _conv_bwd(xpre, conv_w, ssm_conv_b, dxbc, "conv_bwd")
    ddtp = jnp.transpose(ddtp_g[:, :, :8], (1, 0, 2)).reshape(t, SSM_HEADS)
    ddtp = jnp.pad(ddtp, ((0, 0), (0, 128 - SSM_HEADS))).astype(BF16)
    dw_z, = _mm_dw(h0, dz, "plain", name="ssm_in_z_dw")
    dw_x, = _mm_dw(h0, du, "plain", name="ssm_in_x_dw")
    dw_dt, = _mm_dw(h0, ddtp, "plain", name="ssm_in_dt_dw")
    half = D_MODEL // 2

    def in_part(lo):
        rows = jnp.concatenate([dw_z[lo:lo + half], dw_x[lo:lo + half], dw_dt[lo:lo + half, :SSM_HEADS]], axis=1)
        return jnp.transpose(rows.reshape(half, NDEV, in_shard), (1, 0, 2))

    dh0, = _mm_dx(ddtp, w_dt, "plain", D_MODEL, 128, epi=_ident, outs=(F32,), name="ssm_in_dt_dx")
    dh0, p_in_a = _mm_dx(dz, w_z, "plain", D_MODEL, D_INNER, epi=_add, outs=(F32,), extras=(dh0,),
                         carry=("scatter", [in_part(0)]), name="ssm_in_z_dx")
    dh0, p_in_b = _mm_dx(du, w_x, "plain", D_MODEL, CONV_DIM, epi=_add, outs=(F32,), extras=(dh0,),
                         carry=("scatter", [in_part(half)]), name="ssm_in_x_dx")
    dx0, _, dg_mix0 = _rmsnorm_bwd(x0, norm_mix[0:1], dh0, dx1, "norm_mix_bwd0")

    parts = [[p_in_a, p_in_b], [p_out], [p_qkv], [p_o], [p_w1_0], [p_w1_1], [p_w2_0], [p_w2_1]]

    def own(w, mm, vv):
        shp = w.shape
        f = lambda a: a.reshape(-1, shp[-1])
        return f(w), f(mm), f(vv), shp

    big = {}
    for key, part, (w, mm, vv) in (
            ("ssm_w_in", parts[0], (ssm_w_in, m_ssm_w_in, v_ssm_w_in)),
            ("ssm_w_out", parts[1], (ssm_w_out, m_ssm_w_out, v_ssm_w_out)),
            ("attn_w_qkv", parts[2], (attn_w_qkv, m_attn_w_qkv, v_attn_w_qkv)),
            ("attn_w_o", parts[3], (attn_w_o, m_attn_w_o, v_attn_w_o))):
        w2, m2, v2, shp = own(w, mm, vv)
        res = _reduce_adamw(part, w2, m2, v2, f"adamw_{key}")
        big[key] = [r.reshape(shp) for r in res]
    for key, pa, pb, (w, mm, vv) in (("mlp_w1", parts[4], parts[5], (mlp_w1, m_mlp_w1, v_mlp_w1)),
                                     ("mlp_w2", parts[6], parts[7], (mlp_w2, m_mlp_w2, v_mlp_w2))):
        res = [_reduce_adamw(p, w[l], mm[l], vv[l], f"adamw_{key}_{l}") for l, p in enumerate((pa, pb))]
        big[key] = [jnp.stack([res[0][i], res[1][i]], axis=0) for i in range(4)]

    d_norm_mix = jnp.concatenate([dg_mix0, dg_mix1], axis=0)
    d_norm_mlp = jnp.concatenate([dg_mlp0, dg_mlp1], axis=0)
    d_conv_b = conv_sums[4:5]
    d_conv_w = conv_sums[0:4]
    head = hsums[:, :, :8]
    d_dt_bias, d_a_log, d_d = (head[:, k, :].reshape(1, SSM_HEADS) for k in range(3))
    d_ssm_norm = csums[0:1]
    small = [d_norm_mix, d_norm_mlp, d_conv_b, d_dt_bias, d_a_log, d_d, d_ssm_norm, d_final, d_conv_w, loss_acc[0:1, 0:1]]
    shapes = [a.shape for a in small]
    summed = _unpack(_all_reduce_small(_pack(small), "reduce_small"), shapes)
    g_conv_w_full = summed[8]
    loss = summed[9].reshape(())
    g_conv_w = lax.dynamic_slice(g_conv_w_full, (0, me * (CONV_DIM // NDEV)), (CONV_WIDTH, CONV_DIM // NDEV))

    small_names = ["norm_mix", "norm_mlp", "ssm_conv_b", "ssm_dt_bias", "ssm_a_log", "ssm_d", "ssm_norm_w", "final_norm"]
    small_w = [norm_mix, norm_mlp, ssm_conv_b, ssm_dt_bias, ssm_a_log, ssm_d, ssm_norm_w, final_norm, ssm_conv_w]
    small_m = [m_norm_mix, m_norm_mlp, m_ssm_conv_b, m_ssm_dt_bias, m_ssm_a_log, m_ssm_d, m_ssm_norm_w, m_final_norm, m_ssm_conv_w]
    small_v = [v_norm_mix, v_norm_mlp, v_ssm_conv_b, v_ssm_dt_bias, v_ssm_a_log, v_ssm_d, v_ssm_norm_w, v_final_norm, v_ssm_conv_w]
    small_g = [summed[i].reshape(small_w[i].shape) for i in range(8)] + [g_conv_w.reshape(ssm_conv_w.shape)]
    wshapes = [a.shape for a in small_w]
    sd, sm, sv = _adamw(_pack(small_g), _pack(small_w), _pack(small_m), _pack(small_v), "adamw_small")
    sd, sm, sv = _unpack(sd, wshapes), _unpack(sm, wshapes), _unpack(sv, wshapes)
    res = {n: (small_g[i], sd[i], sm[i], sv[i]) for i, n in enumerate(small_names + ["ssm_conv_w"])}
    for n in big:
        res[n] = tuple(big[n])

    order = ["norm_mix", "norm_mlp", "ssm_w_in", "ssm_conv_w", "ssm_conv_b", "ssm_dt_bias", "ssm_a_log", "ssm_d",
             "ssm_norm_w", "ssm_w_out", "attn_w_qkv", "attn_w_o", "mlp_w1", "mlp_w2", "final_norm"]
    outs = [loss, dx0.reshape(x.shape)]
    for kind in range(4):
        outs += [res[n][kind] for n in order]
    return tuple(outs)
```

```python
import math

import jax
import jax.numpy as jnp
from jax import lax
from jax.experimental import pallas as pl
from jax.experimental.pallas import tpu as pltpu

F32, BF16 = jnp.float32, jnp.bfloat16
SDS = jax.ShapeDtypeStruct
MESH = pl.DeviceIdType.MESH
HIGHEST = lax.Precision.HIGHEST

NDEV = 8
D_MODEL = 2048
D_INNER = 4096
SSM_HEADS = 64
SSM_HEAD_DIM = 64
SSM_GROUPS = 8
SSM_STATE = 128
CHUNK = 128
CONV_DIM = 6144
CONV_WIDTH = 4
GROUP_W = D_INNER // SSM_GROUPS
ATTN_GROUPS = ((128, 1), (512, 4), (2048, 16))
ATTN_W = 1024
QKV_DIM = 9216
D_FF = 8192
EPS = 1e-5
ADAM_LR, ADAM_B1, ADAM_B2, ADAM_EPS, ADAM_WD, ADAM_STEP = 0.001, 0.9, 0.999, 1e-08, 0.01, 10

VMEM_LIMIT = 48 * 1024 * 1024


def _params(sem):
    return pltpu.CompilerParams(dimension_semantics=sem, vmem_limit_bytes=VMEM_LIMIT)


def _sigmoid(v):
    return 1.0 / (1.0 + jnp.exp(-v))


def _softplus(v):
    return jnp.maximum(v, 0.0) + jnp.log1p(jnp.exp(-jnp.abs(v)))


def _wspec(layout, kw, nw, tr, tc, sel):
    if layout == "plain":
        return pl.BlockSpec((tr, tc), lambda *g: sel(*g))
    per = (nw // NDEV) // tc
    return pl.BlockSpec((None, tr, tc), lambda *g: (sel(*g)[1] // per, sel(*g)[0], sel(*g)[1] % per))


def _wshape(layout, kw, nw):
    return (kw, nw) if layout == "plain" else (NDEV, kw, nw // NDEV)


def _mm_call(name, grid, in_specs, out_specs, out_shape, dims, n_extra, epi, tm, tn, carry):
    nk = grid[2]
    kind, moved = carry if carry else (None, ())
    nc = len(moved)
    n_out = len(out_shape)

    def body(*refs):
        a_ref, b_ref = refs[0], refs[1]
        extra = refs[2:2 + n_extra]
        c_in = refs[2 + n_extra:2 + n_extra + nc]
        outs = refs[2 + n_extra + nc:2 + n_extra + nc + n_out]
        c_out = refs[2 + n_extra + nc + n_out:2 + n_extra + 2 * nc + n_out]
        acc = refs[2 + n_extra + 2 * nc + n_out]
        sems = refs[3 + n_extra + 2 * nc + n_out:]
        i, j, k = pl.program_id(0), pl.program_id(1), pl.program_id(2)
        if nc:
            @pl.when((i == 0) & (j == 0) & (k == 0))
            def _():
                _exchange_start(kind, c_in, c_out, *sems)

        d = lax.dot_general(a_ref[...].astype(BF16), b_ref[...].astype(BF16), (dims, ((), ())), preferred_element_type=F32)

        def finish(total):
            vals = epi(total, *[e[...] for e in extra])
            for o, v in zip(outs, vals):
                o[...] = v.astype(o.dtype)

        if nk == 1:
            finish(d)
        else:
            @pl.when(k == 0)
            def _():
                acc[...] = d

            @pl.when(jnp.logical_and(k > 0, k < nk - 1))
            def _():
                acc[...] += d

            @pl.when(k == nk - 1)
            def _():
                finish(acc[...] + d)

        if nc:
            @pl.when((i == grid[0] - 1) & (j == grid[1] - 1) & (k == nk - 1))
            def _():
                _exchange_wait(kind, c_in, c_out, *sems)

    hbm = pl.BlockSpec(memory_space=pl.ANY)
    scratch = [pltpu.VMEM((tm, tn), F32)] + (_exchange_sems(nc) if nc else [])
    c_shape = [SDS(((NDEV,) + t.shape) if kind == "gather" else t.shape, t.dtype) for t in moved]
    sem = ("arbitrary",) * 3 if nc else ("parallel", "parallel", "arbitrary")
    return pl.pallas_call(
        body, grid=grid, in_specs=in_specs + [hbm] * nc, out_specs=out_specs + [hbm] * nc,
        out_shape=out_shape + c_shape, scratch_shapes=scratch, name=name, compiler_params=_params(sem))


def _mm_fwd(a, w, layout, kw, nw, *, epi, outs, extras=(), tm=1024, tn=1024, tk=2048, carry=None, name):
    m = a.shape[0]
    tm, tn, tk = min(tm, m), min(tn, nw), min(tk, kw)
    grid = (m // tm, nw // tn, kw // tk)
    o_spec = pl.BlockSpec((tm, tn), lambda i, j, k: (i, j))
    in_specs = [pl.BlockSpec((tm, tk), lambda i, j, k: (i, k)), _wspec(layout, kw, nw, tk, tn, lambda i, j, k: (k, j))]
    in_specs += [o_spec] * len(extras)
    call = _mm_call(name, grid, in_specs, [o_spec] * len(outs), [SDS((m, nw), dt) for dt in outs], ((1,), (0,)),
                    len(extras), epi, tm, tn, carry)
    return call(a, w, *extras, *(carry[1] if carry else ()))


def _mm_dx(g, w, layout, kw, nw, *, epi, outs, extras=(), tm=1024, tn=1024, tk=2048, carry=None, name):
    m = g.shape[0]
    tm, tn, tk = min(tm, m), min(tn, kw), min(tk, nw if layout == "plain" else nw // NDEV)
    grid = (m // tm, kw // tn, nw // tk)
    o_spec = pl.BlockSpec((tm, tn), lambda i, j, k: (i, j))
    in_specs = [pl.BlockSpec((tm, tk), lambda i, j, k: (i, k)), _wspec(layout, kw, nw, tn, tk, lambda i, j, k: (j, k))]
    in_specs += [o_spec] * len(extras)
    call = _mm_call(name, grid, in_specs, [o_spec] * len(outs), [SDS((m, kw), dt) for dt in outs], ((1,), (1,)),
                    len(extras), epi, tm, tn, carry)
    return call(g, w, *extras, *(carry[1] if carry else ()))


def _mm_dw(a, g, layout, *, tm=1024, tn=1024, tk=2048, carry=None, name):
    m, kw = a.shape
    nw = g.shape[1]
    tm, tn, tk = min(tm, kw), min(tn, nw if layout == "plain" else nw // NDEV), min(tk, m)
    grid = (kw // tm, nw // tn, m // tk)
    in_specs = [pl.BlockSpec((tk, tm), lambda i, j, k: (k, i)), pl.BlockSpec((tk, tn), lambda i, j, k: (k, j))]
    o_spec = _wspec(layout, kw, nw, tm, tn, lambda i, j, k: (i, j))
    call = _mm_call(name, grid, in_specs, [o_spec], [SDS(_wshape(layout, kw, nw), BF16)], ((0,), (0,)),
                    0, lambda t: (t,), tm, tn, carry)
    return call(a, g, *(carry[1] if carry else ()))


ROW_TILE = 256


def _rmsnorm_fwd(x, g, name):
    t, d = x.shape

    def body(x_ref, g_ref, h_ref):
        xv = x_ref[...]
        r = lax.rsqrt(jnp.mean(xv * xv, axis=-1, keepdims=True) + EPS)
        h_ref[...] = (xv * r * g_ref[...]).astype(BF16)

    row = pl.BlockSpec((ROW_TILE, d), lambda i: (i, 0))
    vec = pl.BlockSpec((1, d), lambda i: (0, 0))
    return pl.pallas_call(body, grid=(t // ROW_TILE,), in_specs=[row, vec], out_specs=row, out_shape=SDS((t, d), BF16),
                          name=name, compiler_params=_params(("parallel",)))(x, g)


def _rmsnorm_bwd(x, g, dh, dres, name):
    t, d = x.shape

    def body(x_ref, g_ref, dh_ref, dres_ref, dx_ref, dxb_ref, dg_ref):
        xv = x_ref[...]
        r = lax.rsqrt(jnp.mean(xv * xv, axis=-1, keepdims=True) + EPS)
        xh = xv * r
        dhv = dh_ref[...]
        gd = dhv * g_ref[...]
        dx = dres_ref[...] + r * (gd - xh * jnp.mean(gd * xh, axis=-1, keepdims=True))
        dx_ref[...] = dx
        dxb_ref[...] = dx.astype(BF16)
        part = jnp.sum(dhv * xh, axis=0, keepdims=True)

        @pl.when(pl.program_id(0) == 0)
        def _():
            dg_ref[...] = part

        @pl.when(pl.program_id(0) > 0)
        def _():
            dg_ref[...] += part

    row = pl.BlockSpec((ROW_TILE, d), lambda i: (i, 0))
    vec = pl.BlockSpec((1, d), lambda i: (0, 0))
    return pl.pallas_call(body, grid=(t // ROW_TILE,), in_specs=[row, vec, row, row], out_specs=[row, row, vec],
                          out_shape=[SDS((t, d), F32), SDS((t, d), BF16), SDS((1, d), F32)], name=name,
                          compiler_params=_params(("arbitrary",)))(x, g, dh, dres)


def _loss_head(x, tgt, g, name):
    t, d = x.shape

    def body(x_ref, t_ref, g_ref, dx_ref, dxb_ref, loss_ref, dg_ref):
        xv = x_ref[...]
        r = lax.rsqrt(jnp.mean(xv * xv, axis=-1, keepdims=True) + EPS)
        xh = xv * r
        gv = g_ref[...]
        err = xh * gv - t_ref[...]
        part_loss = 0.5 * jnp.sum(jnp.mean(err * err, axis=-1, keepdims=True), axis=0, keepdims=True)
        dy = err * (1.0 / d)
        gd = dy * gv
        dx = r * (gd - xh * jnp.mean(gd * xh, axis=-1, keepdims=True))
        dx_ref[...] = dx
        dxb_ref[...] = dx.astype(BF16)
        part_g = jnp.sum(dy * xh, axis=0, keepdims=True)
        part_l = jnp.broadcast_to(part_loss, (8, 128))

        @pl.when(pl.program_id(0) == 0)
        def _():
            dg_ref[...] = part_g
            loss_ref[...] = part_l

        @pl.when(pl.program_id(0) > 0)
        def _():
            dg_ref[...] += part_g
            loss_ref[...] += part_l

    row = pl.BlockSpec((ROW_TILE, d), lambda i: (i, 0))
    vec = pl.BlockSpec((1, d), lambda i: (0, 0))
    sc = pl.BlockSpec((8, 128), lambda i: (0, 0))
    return pl.pallas_call(body, grid=(t // ROW_TILE,), in_specs=[row, row, vec], out_specs=[row, row, sc, vec],
                          out_shape=[SDS((t, d), F32), SDS((t, d), BF16), SDS((8, 128), F32), SDS((1, d), F32)], name=name,
                          compiler_params=_params(("arbitrary",)))(x, tgt, g)


CONV_ROWS = 256
CONV_COLS = 2048


def _shift_down(cur, prev8, k):
    sh = pltpu.roll(cur, k, axis=0)
    ph = pltpu.roll(prev8, k, axis=0)
    rid = lax.broadcasted_iota(jnp.int32, ph.shape, 0)
    head = jnp.where(rid < k, ph, sh[0:8])
    return jnp.concatenate([head, sh[8:]], axis=0)


def _shift_up(cur, next8, k):
    n = cur.shape[0]
    sh = pltpu.roll(cur, n - k, axis=0)
    nh = pltpu.roll(next8, 8 - k, axis=0)
    rid = lax.broadcasted_iota(jnp.int32, nh.shape, 0)
    tail = jnp.where(rid >= 8 - k, nh, sh[n - 8:])
    return jnp.concatenate([sh[:n - 8], tail], axis=0)


def _conv_pre(cur, prev8, w, b):
    acc = w[3:4, :] * cur + b
    for k in range(1, CONV_WIDTH):
        acc = acc + w[3 - k:4 - k, :] * _shift_down(cur, prev8, k)
    return acc


def _conv_fwd(u, w, b, name):
    t, c = u.shape
    per = CONV_ROWS // 8

    def body(u_ref, p_ref, w_ref, b_ref, o_ref):
        prev8 = jnp.where(pl.program_id(1) == 0, 0.0, p_ref[...])
        pre = _conv_pre(u_ref[...], prev8, w_ref[...], b_ref[...])
        o_ref[...] = pre * _sigmoid(pre)

    cur = pl.BlockSpec((CONV_ROWS, CONV_COLS), lambda j, i: (i, j))
    prev = pl.BlockSpec((8, CONV_COLS), lambda j, i: (jnp.maximum(i * per - 1, 0), j))
    wsp = pl.BlockSpec((CONV_WIDTH, CONV_COLS), lambda j, i: (0, j))
    bsp = pl.BlockSpec((1, CONV_COLS), lambda j, i: (0, j))
    return pl.pallas_call(body, grid=(c // CONV_COLS, t // CONV_ROWS), in_specs=[cur, prev, wsp, bsp], out_specs=cur,
                          out_shape=SDS((t, c), F32), name=name, compiler_params=_params(("parallel", "parallel")))(u, u, w, b)


def _conv_bwd(u, w, b, dout, name):
    t, c = u.shape
    per = CONV_ROWS // 8
    last = t // CONV_ROWS - 1

    def dsilu(pre, d):
        s = _sigmoid(pre)
        return d * (s * (1.0 + pre * (1.0 - s)))

    def body(u_ref, p_ref, n_ref, w_ref, b_ref, d_ref, dn_ref, du_ref, dw_ref):
        i = pl.program_id(1)
        cur = u_ref[...]
        wv, bv = w_ref[...], b_ref[...]
        prev8 = jnp.where(i == 0, 0.0, p_ref[...])
        dpre = dsilu(_conv_pre(cur, prev8, wv, bv), d_ref[...])
        nxt, tail = n_ref[...], cur[CONV_ROWS - 8:]
        rid = lax.broadcasted_iota(jnp.int32, nxt.shape, 0)
        pre_n = wv[3:4, :] * nxt + bv
        for k in range(1, CONV_WIDTH):
            pre_n = pre_n + wv[3 - k:4 - k, :] * jnp.where(rid < k, pltpu.roll(tail, k, axis=0), pltpu.roll(nxt, k, axis=0))
        dpre_n = jnp.where(i == last, 0.0, dsilu(pre_n, dn_ref[...]))
        du = wv[3:4, :] * dpre
        for k in range(1, CONV_WIDTH):
            du = du + wv[3 - k:4 - k, :] * _shift_up(dpre, dpre_n, k)
        du_ref[...] = du.astype(BF16)
        rows = [jnp.sum(dpre * _shift_down(cur, prev8, 3 - k), axis=0, keepdims=True) for k in range(3)]
        rows.append(jnp.sum(dpre * cur, axis=0, keepdims=True))
        rows.append(jnp.sum(dpre, axis=0, keepdims=True))
        part = jnp.concatenate(rows + [jnp.zeros((3, cur.shape[1]), F32)], axis=0)

        @pl.when(i == 0)
        def _():
            dw_ref[...] = part

        @pl.when(i > 0)
        def _():
            dw_ref[...] += part

    cur = pl.BlockSpec((CONV_ROWS, CONV_COLS), lambda j, i: (i, j))
    prev = pl.BlockSpec((8, CONV_COLS), lambda j, i: (jnp.maximum(i * per - 1, 0), j))
    nxt = pl.BlockSpec((8, CONV_COLS), lambda j, i: (jnp.minimum((i + 1) * per, t // 8 - 1), j))
    wsp = pl.BlockSpec((CONV_WIDTH, CONV_COLS), lambda j, i: (0, j))
    bsp = pl.BlockSpec((1, CONV_COLS), lambda j, i: (0, j))
    acc = pl.BlockSpec((8, CONV_COLS), lambda j, i: (0, j))
    return pl.pallas_call(body, grid=(c // CONV_COLS, t // CONV_ROWS), in_specs=[cur, prev, nxt, wsp, bsp, cur, nxt],
                          out_specs=[cur, acc], out_shape=[SDS((t, c), BF16), SDS((8, c), F32)], name=name,
                          compiler_params=_params(("parallel", "arbitrary")))(u, u, u, w, b, dout, dout)


def _pieces(v, n):
    out, rest = [], v
    for _ in range(n):
        p = rest.astype(BF16)
        out.append(p)
        rest = rest - p.astype(F32)
    return out


def _head_expand(v):
    r = lax.broadcasted_iota(jnp.int32, (3 * 128, GROUP_W), 0) % 128
    c = lax.broadcasted_iota(jnp.int32, (3 * 128, GROUP_W), 1)
    return _dot(jnp.concatenate(_pieces(v, 3), axis=1), (c // SSM_HEAD_DIM == r).astype(BF16))


def _head_sum(vs):
    r = lax.broadcasted_iota(jnp.int32, (2 * GROUP_W, 128), 0) % GROUP_W
    c = lax.broadcasted_iota(jnp.int32, (2 * GROUP_W, 128), 1)
    stacked = jnp.concatenate([jnp.concatenate(_pieces(v, 2), axis=1) for v in vs], axis=0)
    out = _dot(stacked, (r // SSM_HEAD_DIM == c).astype(BF16))
    res, at = [], 0
    for v in vs:
        res.append(out[at:at + v.shape[0]])
        at += v.shape[0]
    return res


def _tri_left(tri, v):
    r = _dot(tri.astype(BF16), jnp.concatenate(_pieces(v, 3), axis=1))
    return r[:, 0:128] + r[:, 128:256] + r[:, 256:384]


def _dot(a, b):
    return jnp.dot(a, b, preferred_element_type=F32)


def _dot_nt(a, b):
    return lax.dot_general(a, b, (((1,), (1,)), ((), ())), preferred_element_type=F32)


def _dot_tn(a, b):
    return lax.dot_general(a, b, (((0,), (0,)), ((), ())), preferred_element_type=F32)


def _ssd_common(dtp, dtpt, bias, biasr, alog, alogr):
    li = lax.broadcasted_iota(jnp.int32, (CHUNK, CHUNK), 0)
    si = lax.broadcasted_iota(jnp.int32, (CHUNK, CHUNK), 1)
    lower, upper = (li >= si), (li <= si)
    dt = _softplus(dtp + bias)
    a_neg = -jnp.exp(alog)
    cs = _tri_left(lower, dt * a_neg)
    dtr = _softplus(dtpt + biasr)
    ar = jnp.concatenate([dtr * (-jnp.exp(alogr)), jnp.zeros((8, CHUNK), F32)], axis=0)
    r3 = lax.broadcasted_iota(jnp.int32, (3 * CHUNK, CHUNK), 0) % CHUNK
    c3 = lax.broadcasted_iota(jnp.int32, (3 * CHUNK, CHUNK), 1)
    csr = _dot(jnp.concatenate(_pieces(ar, 3), axis=1), (r3 <= c3).astype(BF16))[0:8]
    return lower, upper, dt, a_neg, cs, csr


def _head_masked_rows(v):
    hl = lax.broadcasted_iota(jnp.int32, v.shape, 1) // SSM_HEAD_DIM
    return jnp.concatenate([jnp.where(hl == j, v, jnp.zeros_like(v)) for j in range(8)], axis=0)


def _ssd_fwd(xbc, z, dtp_g, dtp_t, bias_g, bias_r, alog_g, alog_r, d_exp, norm_w, name):
    t = xbc.shape[0]
    nc = t // CHUNK

    def body(xs_ref, b_ref, c_ref, z_ref, dtp_ref, dtpt_ref, bias_ref, biasr_ref, alog_ref, alogr_ref, dexp_ref, nw_ref,
             y_ref, yn_ref, st_ref, state):
        c = pl.program_id(1)

        @pl.when(c == 0)
        def _():
            state[...] = jnp.zeros_like(state)

        lower, _, dt, a_neg, cs, csr = _ssd_common(dtp_ref[...], dtpt_ref[...], bias_ref[...], biasr_ref[...],
                                                alog_ref[...], alogr_ref[...])
        cs_e = _head_expand(cs)
        dt_e = _head_expand(dt)
        xs = xs_ref[...]
        xdt = xs * dt_e
        bm = b_ref[...]
        cm = c_ref[...]
        bmb, cmb = bm.astype(BF16), cm.astype(BF16)
        cb = _dot_nt(cmb, bmb)
        ms = []
        for j in range(8):
            dlt = cs_e[:, SSM_HEAD_DIM * j:SSM_HEAD_DIM * j + 1] - csr[j:j + 1, :]
            ms.append((cb * jnp.exp(jnp.where(lower, dlt, -jnp.inf))).astype(BF16))
        y = _dot(jnp.concatenate(ms, axis=1), _head_masked_rows(xdt.astype(BF16)))
        st_in = state[...]
        st_ref[...] = st_in
        y = y + jnp.exp(cs_e) * _dot(cmb, st_in.astype(BF16))
        cs_last = cs_e[CHUNK - 1:CHUNK, :]
        xdtd = (xdt * jnp.exp(cs_last - cs_e)).astype(BF16)
        state[...] = jnp.exp(cs_last) * st_in + _dot(bm.T.astype(BF16), xdtd)
        y_ref[...] = y
        zz = z_ref[...]
        y2 = (y + dexp_ref[...] * xs) * (zz * _sigmoid(zz))
        r = lax.rsqrt(jnp.mean(y2 * y2, axis=-1, keepdims=True) + EPS)
        yn_ref[...] = (y2 * r * nw_ref[...]).astype(BF16)

    gw = pl.BlockSpec((CHUNK, GROUP_W), lambda g, c: (c, g))
    in_specs = [
        gw,
        pl.BlockSpec((CHUNK, SSM_STATE), lambda g, c: (c, D_INNER // SSM_STATE + g)),
        pl.BlockSpec((CHUNK, SSM_STATE), lambda g, c: (c, D_INNER // SSM_STATE + SSM_GROUPS + g)),
        gw,
        pl.BlockSpec((None, CHUNK, 128), lambda g, c: (g, c, 0)),
        pl.BlockSpec((8, CHUNK), lambda g, c: (g, c)),
        pl.BlockSpec((None, 1, 128), lambda g, c: (g, 0, 0)),
        pl.BlockSpec((8, 128), lambda g, c: (g, 0)),
        pl.BlockSpec((None, 1, 128), lambda g, c: (g, 0, 0)),
        pl.BlockSpec((8, 128), lambda g, c: (g, 0)),
        pl.BlockSpec((1, GROUP_W), lambda g, c: (0, g)),
        pl.BlockSpec((1, GROUP_W), lambda g, c: (0, g)),
    ]
    out_specs = [gw, gw, pl.BlockSpec((None, SSM_STATE, GROUP_W), lambda g, c: (c, 0, g))]
    out_shape = [SDS((t, D_INNER), F32), SDS((t, D_INNER), BF16), SDS((nc, SSM_STATE, D_INNER), F32)]
    return pl.pallas_call(body, grid=(SSM_GROUPS, nc), in_specs=in_specs, out_specs=out_specs, out_shape=out_shape,
                          scratch_shapes=[pltpu.VMEM((SSM_STATE, GROUP_W), F32)], name=name,
                          compiler_params=_params(("parallel", "arbitrary")))(
        xbc, xbc, xbc, z, dtp_g, dtp_t, bias_g, bias_r, alog_g, alog_r, d_exp, norm_w)


def _ssd_bwd(dyn, y, xbc, z, states, dtp_g, dtp_t, bias_g, bias_r, alog_g, alog_r, d_exp, norm_w, name):
    t = xbc.shape[0]
    nc = t // CHUNK

    def body(dyn_ref, y_ref, xs_ref, b_ref, c_ref, z_ref, st_ref, dtp_ref, dtpt_ref, bias_ref, biasr_ref, alog_ref,
             alogr_ref, dexp_ref, nw_ref, dxs_ref, db_ref, dc_ref, dz_ref, ddt_ref, hsum_ref, csum_ref, dstate):
        step = pl.program_id(1)

        @pl.when(step == 0)
        def _():
            dstate[...] = jnp.zeros_like(dstate)

        dtp = dtp_ref[...]
        bias = bias_ref[...]
        lower, upper, dt, a_neg, cs, csr = _ssd_common(dtp, dtpt_ref[...], bias, biasr_ref[...], alog_ref[...],
                                                       alogr_ref[...])
        cs_e = _head_expand(cs)
        dt_e = _head_expand(dt)
        xs = xs_ref[...]
        xdt = xs * dt_e
        bm = b_ref[...]
        cm = c_ref[...]
        bmb, cmb = bm.astype(BF16), cm.astype(BF16)
        y = y_ref[...]
        dexp = dexp_ref[...]
        nw = nw_ref[...]

        zz = z_ref[...]
        sg = _sigmoid(zz)
        gate = zz * sg
        ytot = y + dexp * xs
        y2 = ytot * gate
        r = lax.rsqrt(jnp.mean(y2 * y2, axis=-1, keepdims=True) + EPS)
        dynv = dyn_ref[...]
        xh = y2 * r
        gn = dynv * nw
        dy2 = r * (gn - xh * jnp.mean(gn * xh, axis=-1, keepdims=True))
        dy = dy2 * gate
        dz_ref[...] = (dy2 * ytot * (sg * (1.0 + zz * (1.0 - sg)))).astype(BF16)
        csum_part = jnp.sum(dynv * xh, axis=0, keepdims=True)

        cb = _dot_nt(cmb, bmb)
        dyb = dy.astype(BF16)
        xdtb = xdt.astype(BF16)
        dym = _head_masked_rows(dyb)
        dm = _dot_nt(dym, xdtb)
        dmt = _dot_nt(_head_masked_rows(xdtb), dyb)
        lane = lax.broadcasted_iota(jnp.int32, (CHUNK, 128), 1)
        mts = []
        dcb = jnp.zeros((CHUNK, CHUNK), F32)
        dcs = jnp.zeros((CHUNK, 128), F32)
        for j in range(8):
            dlt = cs_e[:, SSM_HEAD_DIM * j:SSM_HEAD_DIM * j + 1] - csr[j:j + 1, :]
            lj = jnp.exp(jnp.where(lower, dlt, -jnp.inf))
            mj = cb * lj
            mjt = mj.T
            mts.append(mjt.astype(BF16))
            dmj = dm[CHUNK * j:CHUNK * (j + 1)]
            dcb = dcb + dmj * lj
            rows = jnp.sum(dmj * mj, axis=1, keepdims=True)
            cols = jnp.sum(dmt[CHUNK * j:CHUNK * (j + 1)] * mjt, axis=1, keepdims=True)
            dcs = dcs + jnp.where(lane == j, rows - cols, 0.0)
        dxdt = _dot(jnp.concatenate(mts, axis=1), dym)
        dst_out = dstate[...]
        dst_outb = dst_out.astype(BF16)
        st_in = st_ref[...]
        st_inb = st_in.astype(BF16)
        cs_last = cs_e[CHUNK - 1:CHUNK, :]
        decay = jnp.exp(cs_last - cs_e)
        e_last = jnp.exp(cs_last)
        gpart = decay * _dot(bmb, dst_outb)
        dxdt = dxdt + gpart
        dyw = (jnp.exp(cs_e) * dy).astype(BF16)
        dcbb = dcb.astype(BF16)
        dc_ref[...] = _dot_nt(dyw, st_inb) + _dot(dcbb, bmb)
        db_ref[...] = _dot_nt((xdt * decay).astype(BF16), dst_outb) + _dot(dcb.T.astype(BF16), cmb)
        dstate[...] = e_last * dst_out + _dot(cm.T.astype(BF16), dyw)
        y_off = jnp.exp(cs_e) * _dot(cmb, st_inb)
        xg = xdt * gpart
        vec = jnp.concatenate([jnp.sum(dy * xs, axis=0, keepdims=True),
                               jnp.sum(xg + dst_out * e_last * st_in, axis=0, keepdims=True),
                               jnp.zeros((14, GROUP_W), F32)], axis=0)
        s_cs, s_dt, s_vec = _head_sum([dy * y_off - xg, dxdt * xs, vec])
        d_skip = s_vec[0:1]
        ri = lax.broadcasted_iota(jnp.int32, (CHUNK, 128), 0)
        dcs = dcs + s_cs + jnp.where(ri == CHUNK - 1, s_vec[1:2], 0.0)
        da = _tri_left(upper, dcs)
        ddt = da * a_neg + s_dt
        dxs_ref[...] = dxdt * dt_e + dy * dexp
        ddtp = ddt * _sigmoid(dtp + bias)
        ddt_ref[...] = ddtp
        d_alog = jnp.sum(da * dt, axis=0, keepdims=True) * a_neg
        hpart = jnp.concatenate([jnp.sum(ddtp, axis=0, keepdims=True), d_alog, d_skip, jnp.zeros((5, 128), F32)], axis=0)
        cpart = jnp.concatenate([csum_part, jnp.zeros((7, GROUP_W), F32)], axis=0)

        @pl.when(step == 0)
        def _():
            hsum_ref[...] = hpart
            csum_ref[...] = cpart

        @pl.when(step > 0)
        def _():
            hsum_ref[...] += hpart
            csum_ref[...] += cpart

    rc = lambda c: nc - 1 - c
    gw = pl.BlockSpec((CHUNK, GROUP_W), lambda g, c: (rc(c), g))
    bsp = pl.BlockSpec((CHUNK, SSM_STATE), lambda g, c: (rc(c), D_INNER // SSM_STATE + g))
    csp = pl.BlockSpec((CHUNK, SSM_STATE), lambda g, c: (rc(c), D_INNER // SSM_STATE + SSM_GROUPS + g))
    in_specs = [
        gw, gw, gw, bsp, csp, gw,
        pl.BlockSpec((None, SSM_STATE, GROUP_W), lambda g, c: (rc(c), 0, g)),
        pl.BlockSpec((None, CHUNK, 128), lambda g, c: (g, rc(c), 0)),
        pl.BlockSpec((8, CHUNK), lambda g, c: (g, rc(c))),
        pl.BlockSpec((None, 1, 128), lambda g, c: (g, 0, 0)),
        pl.BlockSpec((8, 128), lambda g, c: (g, 0)),
        pl.BlockSpec((None, 1, 128), lambda g, c: (g, 0, 0)),
        pl.BlockSpec((8, 128), lambda g, c: (g, 0)),
        pl.BlockSpec((1, GROUP_W), lambda g, c: (0, g)),
        pl.BlockSpec((1, GROUP_W), lambda g, c: (0, g)),
    ]
    nsp = pl.BlockSpec((CHUNK, SSM_STATE), lambda g, c: (rc(c), g))
    out_specs = [gw, nsp, nsp, gw,
                 pl.BlockSpec((None, CHUNK, 128), lambda g, c: (g, rc(c), 0)),
                 pl.BlockSpec((None, 8, 128), lambda g, c: (g, 0, 0)),
                 pl.BlockSpec((8, GROUP_W), lambda g, c: (0, g))]
    gn = SSM_GROUPS * SSM_STATE
    out_shape = [SDS((t, D_INNER), F32), SDS((t, gn), F32), SDS((t, gn), F32), SDS((t, D_INNER), BF16),
                 SDS((SSM_GROUPS, t, 128), F32), SDS((SSM_GROUPS, 8, 128), F32), SDS((8, D_INNER), F32)]
    return pl.pallas_call(body, grid=(SSM_GROUPS, nc), in_specs=in_specs, out_specs=out_specs, out_shape=out_shape,
                          scratch_shapes=[pltpu.VMEM((SSM_STATE, GROUP_W), F32)], name=name,
                          compiler_params=_params(("parallel", "arbitrary")))(
        dyn, y, xbc, xbc, xbc, z, states, dtp_g, dtp_t, bias_g, bias_r, alog_g, alog_r, d_exp, norm_w)


BLK = 128
HEAD_PAIRS = ATTN_W // 128
ATTN_SCALE = 0.125
SPAN_BLOCKS = {1: 8, 4: 2, 16: 1}


def _slope_table(group):
    n = len(ATTN_GROUPS) * 16
    tbl = [[2.0 ** (-8.0 * (16 * group + 2 * p + s + 1) / n) if s < 2 else 0.0 for s in range(128)] for p in range(HEAD_PAIRS)]
    return jnp.asarray(tbl, F32)


def _lane_lo(rows):
    return lax.broadcasted_iota(jnp.int32, (rows, 128), 1) < 64


def _rows(start, dil):
    return pl.ds(start, BLK, stride=dil) if dil > 1 else pl.ds(start, BLK)


def _stack_heads(x):
    lo = _lane_lo(BLK)
    return jnp.concatenate([jnp.where(lo, x, jnp.zeros_like(x)), jnp.where(lo, jnp.zeros_like(x), x)], axis=0)


def _pair_cols(x):
    return jnp.concatenate([x[:, 0:1], x[:, 64:65]], axis=0)


def _attn_bias(sl, dil, first_span, last_span):
    qi = lax.broadcasted_iota(jnp.int32, (BLK, 2 * BLK), 0)
    kj = lax.broadcasted_iota(jnp.int32, (BLK, 2 * BLK), 1)
    dist = qi + BLK - kj
    valid = (dist >= 0) & (dist <= BLK)
    distf = dist.astype(F32) * float(dil)

    def stacked(ok, d):
        return jnp.concatenate([jnp.where(ok, -sl[:, h:h + 1] * d, -jnp.inf) for h in range(2)], axis=0)

    first = stacked(valid & ((kj >= BLK) | jnp.logical_not(first_span)), distf)
    after = None
    if last_span is not None:
        after = stacked((kj[:, :BLK] >= qi[:, :BLK]) & jnp.logical_not(last_span), distf[:, :BLK])
    return first, stacked(valid, distf), after


def _attn_specs(group, t):
    _, dil = ATTN_GROUPS[group]
    nblk = SPAN_BLOCKS[dil]
    span, edge = BLK * dil * nblk, BLK * dil
    per = span // edge

    def lane_block(which):
        if which is None:
            return lambda p: p
        return lambda p: 3 * (group * HEAD_PAIRS + p) + which

    def cur(which):
        col = lane_block(which)
        return pl.BlockSpec((span, 128), lambda s, p: (s, col(p)))

    def before(which):
        col = lane_block(which)
        return pl.BlockSpec((edge, 128), lambda s, p: (jnp.maximum(s * per - 1, 0), col(p)))

    def after(which):
        col = lane_block(which)
        return pl.BlockSpec((edge, 128), lambda s, p: (jnp.minimum((s + 1) * per, t // edge - 1), col(p)))

    slopes = pl.BlockSpec((HEAD_PAIRS, 128), lambda s, p: (0, 0))
    return dil, nblk, span, cur, before, after, slopes


def _pair_major(w, inverse=False):
    k = w.shape[0]
    g = len(ATTN_GROUPS)
    if inverse:
        return jnp.transpose(w.reshape(k, g, HEAD_PAIRS, 3, 128), (0, 3, 1, 2, 4)).reshape(k, QKV_DIM)
    return jnp.transpose(w.reshape(k, 3, g, HEAD_PAIRS, 128), (0, 2, 3, 1, 4)).reshape(k, QKV_DIM)


def _attn_fwd(qkv, group, name):
    t = qkv.shape[0]
    dil, nblk, span, cur, before, after, slopes = _attn_specs(group, t)

    def body(q_ref, k_ref, v_ref, kp_ref, vp_ref, sl_ref, o_ref, l_ref):
        first_span = pl.program_id(0) == 0
        bias_first, bias_mid, _ = _attn_bias(sl_ref[pl.ds(pl.program_id(1), 1), :], dil, first_span, None)
        lo_q, lo_k = _lane_lo(BLK), _lane_lo(2 * BLK)
        for r in range(dil):
            kp, vp = kp_ref[_rows(r, dil), :].astype(BF16), vp_ref[_rows(r, dil), :].astype(BF16)
            for b in range(nblk):
                rows = _rows(b * BLK * dil + r, dil)
                kc, vc = k_ref[rows, :].astype(BF16), v_ref[rows, :].astype(BF16)
                k2 = jnp.concatenate([kp, kc], axis=0)
                v2 = jnp.concatenate([vp, vc], axis=0)
                kp, vp = kc, vc
                s = _dot_nt(_stack_heads((q_ref[rows, :] * ATTN_SCALE).astype(BF16)), k2) + (bias_first if b == 0 else bias_mid)
                mx = jnp.max(s, axis=-1, keepdims=True)
                p = jnp.exp(s - mx)
                den = jnp.sum(p, axis=-1, keepdims=True)
                pb = p.astype(BF16)
                o_ref[rows, :] = _dot(pb[:BLK], jnp.where(lo_k, v2, jnp.zeros_like(v2))) / den[:BLK] + \
                    _dot(pb[BLK:], jnp.where(lo_k, jnp.zeros_like(v2), v2)) / den[BLK:]
                lse = mx + jnp.log(den)
                l_ref[rows, :] = jnp.where(lo_q, lse[:BLK], lse[BLK:])

    in_specs = [cur(0), cur(1), cur(2), before(1), before(2), slopes]
    return pl.pallas_call(body, grid=(t // span, HEAD_PAIRS), in_specs=in_specs, out_specs=[cur(None), cur(None)],
                          out_shape=[SDS((t, ATTN_W), F32), SDS((t, ATTN_W), F32)], name=name,
                          compiler_params=_params(("parallel", "parallel")))(qkv, qkv, qkv, qkv, qkv, _slope_table(group))


def _attn_bwd(qkv, do, lse, dvec, dqkv, group, name):
    t = qkv.shape[0]
    dil, nblk, span, cur, before, after, slopes = _attn_specs(group, t)
    nspan = t // span

    def body(q_ref, k_ref, v_ref, kp_ref, vp_ref, qn_ref, do_ref, l_ref, d_ref, don_ref, ln_ref, dn_ref, sl_ref, *rest):
        out_ref, dq_s, dk_s, dv_s = rest[-4:]
        span_id, pair_id = pl.program_id(0), pl.program_id(1)
        bias_first, bias_mid, bias_next = _attn_bias(sl_ref[pl.ds(pair_id, 1), :], dil, span_id == 0, span_id == nspan - 1)
        lo_k = _lane_lo(2 * BLK)
        for r in range(dil):
            kp, vp = kp_ref[_rows(r, dil), :].astype(BF16), vp_ref[_rows(r, dil), :].astype(BF16)
            held = None
            for b in range(nblk + 1):
                last = b == nblk
                rows = _rows(r if last else b * BLK * dil + r, dil)
                qs, dos, ls, dvs = (qn_ref, don_ref, ln_ref, dn_ref) if last else (q_ref, do_ref, l_ref, d_ref)
                qst = _stack_heads((qs[rows, :] * ATTN_SCALE).astype(BF16))
                dost = _stack_heads(dos[rows, :].astype(BF16))
                if last:
                    k2, v2, bias = kp, vp, bias_next
                else:
                    kc, vc = k_ref[rows, :].astype(BF16), v_ref[rows, :].astype(BF16)
                    k2, v2 = jnp.concatenate([kp, kc], axis=0), jnp.concatenate([vp, vc], axis=0)
                    kp, vp = kc, vc
                    bias = bias_first if b == 0 else bias_mid
                p = jnp.exp(_dot_nt(qst, k2) + bias - _pair_cols(ls[rows, :]))
                ds = (p * (_dot_nt(dost, v2) - _pair_cols(dvs[rows, :]))).astype(BF16)
                dk2 = _dot_tn(ds, qst)
                dv2 = _dot_tn(p.astype(BF16), dost)
                if held is not None:
                    dk_s[held[0], :] = held[1] + dk2[:BLK]
                    dv_s[held[0], :] = held[2] + dv2[:BLK]
                if not last:
                    k_heads = jnp.concatenate([jnp.where(lo_k, k2, jnp.zeros_like(k2)),
                                               jnp.where(lo_k, jnp.zeros_like(k2), k2)], axis=0)
                    dq_s[rows, :] = _dot(jnp.concatenate([ds[:BLK], ds[BLK:]], axis=1), k_heads) * ATTN_SCALE
                    held = (rows, dk2[BLK:], dv2[BLK:])
        out_ref[:, 0:128] = dq_s[...]
        out_ref[:, 128:256] = dk_s[...]
        out_ref[:, 256:384] = dv_s[...]

    out_spec = pl.BlockSpec((span, 3 * 128), lambda s, p: (s, group * HEAD_PAIRS + p))
    in_specs = [cur(0), cur(1), cur(2), before(1), before(2), after(0), cur(None), cur(None), cur(None),
                after(None), after(None), after(None), slopes]
    args = [qkv, qkv, qkv, qkv, qkv, qkv, do, lse, dvec, do, lse, dvec, _slope_table(group)]
    aliases = {}
    if dqkv is not None:
        in_specs.append(pl.BlockSpec(memory_space=pl.ANY))
        args.append(dqkv)
        aliases = {len(args) - 1: 0}
    return pl.pallas_call(body, grid=(nspan, HEAD_PAIRS), in_specs=in_specs, out_specs=out_spec,
                          out_shape=SDS((t, QKV_DIM), F32), input_output_aliases=aliases,
                          scratch_shapes=[pltpu.VMEM((span, 128), F32)] * 3, name=name,
                          compiler_params=_params(("parallel", "parallel")))(*args)


def _combine_weights(l0, l1, l2):
    mx = jnp.maximum(jnp.maximum(l0, l1), l2)
    e0, e1, e2 = jnp.exp(l0 - mx), jnp.exp(l1 - mx), jnp.exp(l2 - mx)
    den = e0 + e1 + e2
    return e0 / den, e1 / den, e2 / den


def _combine_fwd(os_, ls_, name):
    t = os_[0].shape[0]

    def body(o0, o1, o2, l0, l1, l2, out):
        w0, w1, w2 = _combine_weights(l0[...], l1[...], l2[...])
        out[...] = (w0 * o0[...] + w1 * o1[...] + w2 * o2[...]).astype(BF16)

    row = pl.BlockSpec((ROW_TILE, ATTN_W), lambda i: (i, 0))
    return pl.pallas_call(body, grid=(t // ROW_TILE,), in_specs=[row] * 6, out_specs=row, out_shape=SDS((t, ATTN_W), BF16),
                          name=name, compiler_params=_params(("parallel",)))(*os_, *ls_)


def _combine_bwd(do, os_, ls_, name):
    t = do.shape[0]

    def body(do_ref, o0, o1, o2, l0, l1, l2, g0, g1, g2, d0, d1, d2):
        w0, w1, w2 = _combine_weights(l0[...], l1[...], l2[...])
        dov = do_ref[...]
        prod = dov * (w0 * o0[...] + w1 * o1[...] + w2 * o2[...])
        r = (lax.broadcasted_iota(jnp.int32, (3 * 128, 128), 0) % 128) // 64
        c = lax.broadcasted_iota(jnp.int32, (3 * 128, 128), 1) // 64
        same = (r == c).astype(BF16)
        tbar = jnp.concatenate([_dot(jnp.concatenate(_pieces(prod[:, 128 * k:128 * k + 128], 3), axis=1), same)
                                for k in range(HEAD_PAIRS)], axis=1)
        for w, g, d in ((w0, g0, d0), (w1, g1, d1), (w2, g2, d2)):
            g[...] = w * dov
            d[...] = w * tbar

    row = pl.BlockSpec((ROW_TILE, ATTN_W), lambda i: (i, 0))
    return pl.pallas_call(body, grid=(t // ROW_TILE,), in_specs=[row] * 7, out_specs=[row] * 6,
                          out_shape=[SDS((t, ATTN_W), F32)] * 6, name=name,
                          compiler_params=_params(("parallel",)))(do, *os_, *ls_)


def _peer(k):
    x, y, c = lax.axis_index("x"), lax.axis_index("y"), lax.axis_index("c")
    px = 1 - x if k & 4 else x
    py = 1 - y if k & 2 else y
    pc = 1 - c if k & 1 else c
    return (px, py, pc), 4 * px + 2 * py + pc


def _my_index():
    return 4 * lax.axis_index("x") + 2 * lax.axis_index("y") + lax.axis_index("c")


def _exchange_sems(n):
    return [pltpu.SemaphoreType.DMA((n * (NDEV - 1),)), pltpu.SemaphoreType.DMA((n * (NDEV - 1),)),
            pltpu.SemaphoreType.DMA((n,))]


def _exchange_copies(kind, ins, outs, send, recv, local, arrivals):
    me = _my_index()
    own, sent, arriving = [], [], []
    for i in range(len(ins)):
        own.append(pltpu.make_async_copy(ins[i] if kind == "gather" else ins[i].at[me], outs[i].at[me], local.at[i]))
        for k in range(1, NDEV):
            peer, pidx = _peer(k)
            s = i * (NDEV - 1) + k - 1
            src = ins[i] if kind == "gather" else ins[i].at[pidx]
            for dst, into in ((outs[i].at[me], sent), (outs[i].at[pidx], arriving)):
                if into is sent or arrivals:
                    into.append(pltpu.make_async_remote_copy(src_ref=src, dst_ref=dst, send_sem=send.at[s],
                                                             recv_sem=recv.at[s], device_id=peer, device_id_type=MESH))
    return own, sent, arriving


def _exchange_start(kind, ins, outs, send, recv, local):
    own, sent, _ = _exchange_copies(kind, ins, outs, send, recv, local, arrivals=False)
    for cp in own + sent:
        cp.start()


def _exchange_wait(kind, ins, outs, send, recv, local):
    own, sent, arriving = _exchange_copies(kind, ins, outs, send, recv, local, arrivals=True)
    for cp in sent:
        cp.wait_send()
    for cp in arriving:
        cp.wait_recv()
    for cp in own:
        cp.wait()


def _gather_two_level(tensors, name):
    n = len(tensors)

    def body(*refs):
        ins, outs = refs[:n], refs[n:2 * n]
        send, recv, local = refs[2 * n:]
        x, y, c = lax.axis_index("x"), lax.axis_index("y"), lax.axis_index("c")
        sibling = (x, y, 1 - c)
        chips = [(1 - x, y), (x, 1 - y), (1 - x, 1 - y)]

        def slot(px, py, pc):
            return 4 * px + 2 * py + pc

        def copy(i, k, block, to, src=None):
            return pltpu.make_async_remote_copy(src_ref=outs[i].at[block] if src is None else src, dst_ref=outs[i].at[block],
                                                send_sem=send.at[7 * i + k], recv_sem=recv.at[7 * i + k],
                                                device_id=to, device_id_type=MESH)

        me = slot(x, y, c)
        own = [pltpu.make_async_copy(ins[i], outs[i].at[me], local.at[i]) for i in range(n)]
        first = [copy(i, 0, me, sibling, src=ins[i]) for i in range(n)]
        first += [copy(i, 1 + j, me, (*chip, c), src=ins[i]) for i in range(n) for j, chip in enumerate(chips)]
        for cp in own + first:
            cp.start()
        passed = []
        for i in range(n):
            for j, chip in enumerate(chips):
                copy(i, 1 + j, slot(*chip, c), (x, y, c)).wait_recv()
                fwd = copy(i, 4 + j, slot(*chip, c), sibling)
                fwd.start()
                passed.append(fwd)
        for i in range(n):
            copy(i, 0, slot(x, y, 1 - c), (x, y, c)).wait_recv()
            for j, chip in enumerate(chips):
                copy(i, 4 + j, slot(*chip, 1 - c), (x, y, c)).wait_recv()
        for cp in first + passed:
            cp.wait_send()
        for cp in own:
            cp.wait()

    hbm = pl.BlockSpec(memory_space=pl.ANY)
    return pl.pallas_call(body, in_specs=[hbm] * n, out_specs=[hbm] * n,
                          out_shape=[SDS((NDEV,) + t.shape, t.dtype) for t in tensors],
                          scratch_shapes=_exchange_sems(n), name=name)(*tensors)


def _all_reduce_small(v, name):
    rows = v.shape[0]

    def body(v_ref, out_ref, land, send, recv):
        me = _my_index()
        land[me] = v_ref[...]
        remote = []
        for k in range(1, NDEV):
            peer, _ = _peer(k)
            cp = pltpu.make_async_remote_copy(src_ref=v_ref, dst_ref=land.at[me], send_sem=send.at[k - 1],
                                              recv_sem=recv.at[k - 1], device_id=peer, device_id_type=MESH)
            cp.start()
            remote.append(cp)
        for cp in remote:
            cp.wait_send()
        for k in range(1, NDEV):
            peer, pidx = _peer(k)
            pltpu.make_async_remote_copy(src_ref=v_ref, dst_ref=land.at[pidx], send_sem=send.at[k - 1],
                                         recv_sem=recv.at[k - 1], device_id=peer, device_id_type=MESH).wait_recv()
        total = land[0]
        for d in range(1, NDEV):
            total = total + land[d]
        out_ref[...] = total

    vm = pl.BlockSpec(memory_space=pltpu.VMEM)
    return pl.pallas_call(
        body, in_specs=[vm], out_specs=vm, out_shape=SDS((rows, 128), F32),
        scratch_shapes=[pltpu.VMEM((NDEV, rows, 128), F32), pltpu.SemaphoreType.DMA((NDEV - 1,)),
                        pltpu.SemaphoreType.DMA((NDEV - 1,))],
        name=name)(v)


def _adamw_math(w, g, m, v):
    m = ADAM_B1 * m + (1.0 - ADAM_B1) * g
    v = ADAM_B2 * v + (1.0 - ADAM_B2) * (g * g)
    m_hat = m / (1.0 - ADAM_B1 ** ADAM_STEP)
    v_hat = v / (1.0 - ADAM_B2 ** ADAM_STEP)
    delta = -ADAM_LR * (m_hat / (jnp.sqrt(v_hat) + ADAM_EPS) + ADAM_WD * w)
    return delta, m, v


def _row_tile(rows, cols):
    tr = rows
    while tr * cols * 4 > (1 << 20) and tr % 16 == 0:
        tr //= 2
    return tr


def _adamw(g, w, m, v, name):
    rows, cols = w.shape
    tr = _row_tile(rows, cols)

    def body(g_ref, w_ref, m_ref, v_ref, d_out, m_out, v_out):
        d, mn, vn = _adamw_math(w_ref[...], g_ref[...], m_ref[...], v_ref[...])
        d_out[...] = d
        m_out[...] = mn
        v_out[...] = vn

    sp = pl.BlockSpec((tr, cols), lambda i: (i, 0))
    return pl.pallas_call(body, grid=(rows // tr,), in_specs=[sp] * 4, out_specs=[sp] * 3,
                          out_shape=[SDS((rows, cols), F32)] * 3, name=name, compiler_params=_params(("parallel",)))(g, w, m, v)


def _reduce_adamw(parts, w, m, v, name):
    rows, cols = w.shape
    tr = _row_tile(parts[0].shape[1], cols)
    tiles = [p.shape[1] // tr for p in parts]
    starts = [sum(tiles[:h]) for h in range(len(parts))]

    def body(*refs):
        p_refs = refs[:len(parts)]
        w_ref, m_ref, v_ref, g_out, d_out, m_out, v_out = refs[len(parts):]
        i = pl.program_id(0)
        for h, p_ref in enumerate(p_refs):
            @pl.when((i >= starts[h]) & (i < starts[h] + tiles[h]))
            def _(p_ref=p_ref):
                g = p_ref[0].astype(F32)
                for d in range(1, NDEV):
                    g = g + p_ref[d].astype(F32)
                g_out[...] = g
                dl, mn, vn = _adamw_math(w_ref[...], g, m_ref[...], v_ref[...])
                d_out[...] = dl
                m_out[...] = mn
                v_out[...] = vn

    sp = pl.BlockSpec((tr, cols), lambda i: (i, 0))
    psp = [pl.BlockSpec((NDEV, tr, cols), lambda i, h=h: (0, jnp.clip(i - starts[h], 0, tiles[h] - 1), 0))
           for h in range(len(parts))]
    return pl.pallas_call(body, grid=(rows // tr,), in_specs=psp + [sp, sp, sp], out_specs=[sp] * 4,
                          out_shape=[SDS((rows, cols), F32)] * 4, name=name,
                          compiler_params=_params(("parallel",)))(*parts, w, m, v)


def _pack(items):
    rows = []
    for a in items:
        a = a.reshape(-1).astype(F32)
        pad = (-a.shape[0]) % 128
        rows.append(jnp.pad(a, (0, pad)).reshape(-1, 128))
    out = jnp.concatenate(rows, axis=0)
    return jnp.pad(out, ((0, (-out.shape[0]) % 8), (0, 0)))


def _unpack(packed, shapes):
    out, r = [], 0
    for shp in shapes:
        n = math.prod(shp)
        nr = -(-n // 128)
        out.append(packed[r:r + nr].reshape(-1)[:n].reshape(shp))
        r += nr
    return out


def _ident(t):
    return (t,)


def _add(t, res):
    return (t + res,)


def kernel(x, norm_mix, norm_mlp, ssm_w_in, ssm_conv_w, ssm_conv_b, ssm_dt_bias, ssm_a_log, ssm_d, ssm_norm_w, ssm_w_out, attn_w_qkv, attn_w_o, mlp_w1, mlp_w2, final_norm, loss_target, m_norm_mix, m_norm_mlp, m_ssm_w_in, m_ssm_conv_w, m_ssm_conv_b, m_ssm_dt_bias, m_ssm_a_log, m_ssm_d, m_ssm_norm_w, m_ssm_w_out, m_attn_w_qkv, m_attn_w_o, m_mlp_w1, m_mlp_w2, m_final_norm, v_norm_mix, v_norm_mlp, v_ssm_w_in, v_ssm_conv_w, v_ssm_conv_b, v_ssm_dt_bias, v_ssm_a_log, v_ssm_d, v_ssm_norm_w, v_ssm_w_out, v_attn_w_qkv, v_attn_w_o, v_mlp_w1, v_mlp_w2, v_final_norm):
    t = x.shape[1]
    x0 = x.reshape(t, D_MODEL)
    tgt = loss_target.reshape(t, D_MODEL)
    me = _my_index()
    in_dim = D_INNER + CONV_DIM + SSM_HEADS
    in_shard = in_dim // NDEV
    zx_dim = D_INNER + CONV_DIM

    s_out, s_qkv, s_o = ssm_w_out[0].astype(BF16), attn_w_qkv[0].astype(BF16), attn_w_o[0].astype(BF16)
    s_w1, s_w2 = mlp_w1.astype(BF16), mlp_w2.astype(BF16)
    g_in, g_cw = _gather_two_level([ssm_w_in[0].astype(BF16), ssm_conv_w[0]], "gather_in_proj")
    w_in = jnp.transpose(g_in, (1, 0, 2)).reshape(D_MODEL, in_dim)
    w_z, w_x = w_in[:, :D_INNER], w_in[:, D_INNER:zx_dim]
    w_dt = jnp.pad(w_in[:, zx_dim:], ((0, 0), (0, 128 - SSM_HEADS)))
    conv_w = jnp.transpose(g_cw, (1, 0, 2)).reshape(CONV_WIDTH, CONV_DIM)
    g_w1, g_w2 = [None, None], [None, None]

    def lanes(p):
        return jnp.pad(p.reshape(SSM_GROUPS, 1, 8), ((0, 0), (0, 0), (0, 120)))

    def rows(p):
        return jnp.broadcast_to(p.reshape(SSM_HEADS, 1), (SSM_HEADS, 128))

    bias_g, bias_r = lanes(ssm_dt_bias[0]), rows(ssm_dt_bias[0])
    alog_g, alog_r = lanes(ssm_a_log[0]), rows(ssm_a_log[0])
    d_exp = jnp.repeat(ssm_d[0], SSM_HEAD_DIM).reshape(1, D_INNER)
    norm_w = ssm_norm_w

    def relu2(tot):
        r = jnp.maximum(tot, 0.0)
        return r, r * r

    def mlp_fwd(xin, layer, tag, down_carry):
        h = _rmsnorm_fwd(xin, norm_mlp[layer:layer + 1], f"norm_mlp{tag}")
        r, a, w2 = _mm_fwd(h, g_w1[layer], "cols", D_MODEL, D_FF, epi=relu2, outs=(BF16, BF16),
                           carry=("gather", [s_w2[layer]]), name=f"mlp_up{tag}")
        g_w2[layer] = w2.reshape(D_FF, D_MODEL)
        xout, *got = _mm_fwd(a, g_w2[layer], "plain", D_FF, D_MODEL, epi=_add, outs=(F32,), extras=(xin,),
                             carry=("gather", down_carry) if down_carry else None, name=f"mlp_down{tag}")
        return h, r, a, xout, got

    h0 = _rmsnorm_fwd(x0, norm_mix[0:1], "norm_mix0")
    z, g_out = _mm_fwd(h0, w_z, "plain", D_MODEL, D_INNER, epi=_ident, outs=(F32,), carry=("gather", [s_out]), name="ssm_in_z")
    xpre, g_w1[0] = _mm_fwd(h0, w_x, "plain", D_MODEL, CONV_DIM, epi=_ident, outs=(F32,), carry=("gather", [s_w1[0]]),
                            name="ssm_in_x")
    dtp, = _mm_fwd(h0, w_dt, "plain", D_MODEL, 128, epi=_ident, outs=(F32,), name="ssm_in_dt")
    xbc = _conv_fwd(xpre, conv_w, ssm_conv_b, "conv_fwd")
    dtp64 = dtp[:, :SSM_HEADS]
    dtp_g = jnp.pad(jnp.transpose(dtp64.reshape(t, SSM_GROUPS, 8), (1, 0, 2)), ((0, 0), (0, 0), (0, 120)))
    dtp_t = jnp.transpose(dtp64)
    y_ssd, yn, states = _ssd_fwd(xbc, z, dtp_g, dtp_t, bias_g, bias_r, alog_g, alog_r, d_exp, norm_w, "ssd_fwd")
    g_out = g_out.reshape(D_INNER, D_MODEL)
    x1, g_o = _mm_fwd(yn, g_out, "plain", D_INNER, D_MODEL, epi=_add, outs=(F32,), extras=(x0,), carry=("gather", [s_o]),
                      name="ssm_out")
    h1, r1, a1, x2, (g_qkv,) = mlp_fwd(x1, 0, "0", [s_qkv])

    h2 = _rmsnorm_fwd(x2, norm_mix[1:2], "norm_mix1")
    w_qkv = _pair_major(jnp.transpose(g_qkv, (1, 0, 2)).reshape(D_MODEL, QKV_DIM))
    qkv, g_w1[1] = _mm_fwd(h2, w_qkv, "plain", D_MODEL, QKV_DIM, epi=_ident, outs=(F32,),
                           carry=("gather", [s_w1[1]]), name="attn_qkv")
    att = [_attn_fwd(qkv, g, f"attn_fwd{g}") for g in range(3)]
    os_, ls_ = [a[0] for a in att], [a[1] for a in att]
    o_mix = _combine_fwd(os_, ls_, "attn_combine")
    x3, = _mm_fwd(o_mix, g_o, "cols", ATTN_W, D_MODEL, epi=_add, outs=(F32,), extras=(x2,), tn=D_MODEL // NDEV, name="attn_out")
    h3, r3, a3, x4, _ = mlp_fwd(x3, 1, "1", None)

    dx4, dx4b, loss_acc, d_final = _loss_head(x4, tgt, final_norm.reshape(1, D_MODEL), "loss_head")

    def mlp_bwd(xin, h, r, a, dxo, dxob, layer, tag):
        du, = _mm_dx(dxob, g_w2[layer], "plain", D_FF, D_MODEL, epi=lambda tot, rr: (tot * (2.0 * rr.astype(F32)),),
                     outs=(BF16,), extras=(r,), name=f"mlp_down_dx{tag}")
        dw2, = _mm_dw(a, dxob, "plain", name=f"mlp_down_dw{tag}")
        dw2 = dw2.reshape(NDEV, D_FF // NDEV, D_MODEL)
        dw1, p_dw2 = _mm_dw(h, du, "cols", carry=("scatter", [dw2]), name=f"mlp_up_dw{tag}")
        dh, p_dw1 = _mm_dx(du, g_w1[layer], "cols", D_MODEL, D_FF, epi=_ident, outs=(F32,), carry=("scatter", [dw1]),
                           name=f"mlp_up_dx{tag}")
        dxi, dxib, dg = _rmsnorm_bwd(xin, norm_mlp[layer:layer + 1], dh, dxo, f"norm_mlp_bwd{tag}")
        return dxi, dxib, dg, p_dw1, p_dw2

    dx3, dx3b, dg_mlp1, p_w1_1, p_w2_1 = mlp_bwd(x3, h3, r3, a3, dx4, dx4b, 1, "1")

    dw_o, = _mm_dw(o_mix, dx3b, "cols", tn=D_MODEL // NDEV, name="attn_out_dw")
    do, p_o = _mm_dx(dx3b, g_o, "cols", ATTN_W, D_MODEL, epi=_ident, outs=(F32,), tk=D_MODEL // NDEV,
                     carry=("scatter", [dw_o]), name="attn_out_dx")
    cb = _combine_bwd(do, os_, ls_, "attn_combine_bwd")
    dos, dvecs = cb[:3], cb[3:]
    dqkv = None
    for g in range(3):
        dqkv = _attn_bwd(qkv, dos[g], ls_[g], dvecs[g], dqkv, g, f"attn_bwd{g}")
    dw_qkv, = _mm_dw(h2, dqkv, "plain", tk=1024, name="attn_qkv_dw")
    dw_qkv = jnp.transpose(_pair_major(dw_qkv, inverse=True).reshape(D_MODEL, NDEV, QKV_DIM // NDEV), (1, 0, 2))
    dh2, p_qkv = _mm_dx(dqkv, w_qkv, "plain", D_MODEL, QKV_DIM, epi=_ident, outs=(F32,), tk=1536,
                        carry=("scatter", [dw_qkv]), name="attn_qkv_dx")
    dx2, dx2b, dg_mix1 = _rmsnorm_bwd(x2, norm_mix[1:2], dh2, dx3, "norm_mix_bwd1")

    dx1, dx1b, dg_mlp0, p_w1_0, p_w2_0 = mlp_bwd(x1, h1, r1, a1, dx2, dx2b, 0, "0")

    dw_out, = _mm_dw(yn, dx1b, "plain", name="ssm_out_dw")
    dw_out = dw_out.reshape(NDEV, D_INNER // NDEV, D_MODEL)
    dyn, p_out = _mm_dx(dx1b, g_out, "plain", D_INNER, D_MODEL, epi=_ident, outs=(F32,), carry=("scatter", [dw_out]),
                        name="ssm_out_dx")
    dxs, d_b, d_c, dz, ddtp_g, hsums, csums = _ssd_bwd(dyn, y_ssd, xbc, z, states, dtp_g, dtp_t, bias_g, bias_r,
                                                       alog_g, alog_r, d_exp, norm_w, "ssd_bwd")
    dxbc = jnp.concatenate([dxs, d_b, d_c], axis=1)
    du, conv_sums = _conv_bwd(xpre, conv_w, ssm_conv_b, dxbc, "conv_bwd")
    ddtp = jnp.transpose(ddtp_g[:, :, :8], (1, 0, 2)).reshape(t, SSM_HEADS)
    ddtp = jnp.pad(ddtp, ((0, 0), (0, 128 - SSM_HEADS))).astype(BF16)
    dw_z, = _mm_dw(h0, dz, "plain", name="ssm_in_z_dw")
    dw_x, = _mm_dw(h0, du, "plain", name="ssm_in_x_dw")
    dw_dt, = _mm_dw(h0, ddtp, "plain", name="ssm_in_dt_dw")
    half = D_MODEL // 2

    def in_part(lo):
        rows = jnp.concatenate([dw_z[lo:lo + half], dw_x[lo:lo + half], dw_dt[lo:lo + half, :SSM_HEADS]], axis=1)
        return jnp.transpose(rows.reshape(half, NDEV, in_shard), (1, 0, 2))

    dh0, = _mm_dx(ddtp, w_dt, "plain", D_MODEL, 128, epi=_ident, outs=(F32,), name="ssm_in_dt_dx")
    dh0, p_in_a = _mm_dx(dz, w_z, "plain", D_MODEL, D_INNER, epi=_add, outs=(F32,), extras=(dh0,),
                         carry=("scatter", [in_part(0)]), name="ssm_in_z_dx")
    dh0, p_in_b = _mm_dx(du, w_x, "plain", D_MODEL, CONV_DIM, epi=_add, outs=(F32,), extras=(dh0,),
                         carry=("scatter", [in_part(half)]), name="ssm_in_x_dx")
    dx0, _, dg_mix0 = _rmsnorm_bwd(x0, norm_mix[0:1], dh0, dx1, "norm_mix_bwd0")

    parts = [[p_in_a, p_in_b], [p_out], [p_qkv], [p_o], [p_w1_0], [p_w1_1], [p_w2_0], [p_w2_1]]

    def own(w, mm, vv):
        shp = w.shape
        f = lambda a: a.reshape(-1, shp[-1])
        return f(w), f(mm), f(vv), shp

    big = {}
    for key, part, (w, mm, vv) in (
            ("ssm_w_in", parts[0], (ssm_w_in, m_ssm_w_in, v_ssm_w_in)),
            ("ssm_w_out", parts[1], (ssm_w_out, m_ssm_w_out, v_ssm_w_out)),
            ("attn_w_qkv", parts[2], (attn_w_qkv, m_attn_w_qkv, v_attn_w_qkv)),
            ("attn_w_o", parts[3], (attn_w_o, m_attn_w_o, v_attn_w_o))):
        w2, m2, v2, shp = own(w, mm, vv)
        res = _reduce_adamw(part, w2, m2, v2, f"adamw_{key}")
        big[key] = [r.reshape(shp) for r in res]
    for key, pa, pb, (w, mm, vv) in (("mlp_w1", parts[4], parts[5], (mlp_w1, m_mlp_w1, v_mlp_w1)),
                                     ("mlp_w2", parts[6], parts[7], (mlp_w2, m_mlp_w2, v_mlp_w2))):
        res = [_reduce_adamw(p, w[l], mm[l], vv[l], f"adamw_{key}_{l}") for l, p in enumerate((pa, pb))]
        big[key] = [jnp.stack([res[0][i], res[1][i]], axis=0) for i in range(4)]

    d_norm_mix = jnp.concatenate([dg_mix0, dg_mix1], axis=0)
    d_norm_mlp = jnp.concatenate([dg_mlp0, dg_mlp1], axis=0)
    d_conv_b = conv_sums[4:5]
    d_conv_w = conv_sums[0:4]
    head = hsums[:, :, :8]
    d_dt_bias, d_a_log, d_d = (head[:, k, :].reshape(1, SSM_HEADS) for k in range(3))
    d_ssm_norm = csums[0:1]
    small = [d_norm_mix, d_norm_mlp, d_conv_b, d_dt_bias, d_a_log, d_d, d_ssm_norm, d_final, d_conv_w, loss_acc[0:1, 0:1]]
    shapes = [a.shape for a in small]
    summed = _unpack(_all_reduce_small(_pack(small), "reduce_small"), shapes)
    g_conv_w_full = summed[8]
    loss = summed[9].reshape(())
    g_conv_w = lax.dynamic_slice(g_conv_w_full, (0, me * (CONV_DIM // NDEV)), (CONV_WIDTH, CONV_DIM // NDEV))

    small_names = ["norm_mix", "norm_mlp", "ssm_conv_b", "ssm_dt_bias", "ssm_a_log", "ssm_d", "ssm_norm_w", "final_norm"]
    small_w = [norm_mix, norm_mlp, ssm_conv_b, ssm_dt_bias, ssm_a_log, ssm_d, ssm_norm_w, final_norm, ssm_conv_w]
    small_m = [m_norm_mix, m_norm_mlp, m_ssm_conv_b, m_ssm_dt_bias, m_ssm_a_log, m_ssm_d, m_ssm_norm_w, m_final_norm, m_ssm_conv_w]
    small_v = [v_norm_mix, v_norm_mlp, v_ssm_conv_b, v_ssm_dt_bias, v_ssm_a_log, v_ssm_d, v_ssm_norm_w, v_final_norm, v_ssm_conv_w]
    small_g = [summed[i].reshape(small_w[i].shape) for i in range(8)] + [g_conv_w.reshape(ssm_conv_w.shape)]
    wshapes = [a.shape for a in small_w]
    sd, sm, sv = _adamw(_pack(small_g), _pack(small_w), _pack(small_m), _pack(small_v), "adamw_small")
    sd, sm, sv = _unpack(sd, wshapes), _unpack(sm, wshapes), _unpack(sv, wshapes)
    res = {n: (small_g[i], sd[i], sm[i], sv[i]) for i, n in enumerate(small_names + ["ssm_conv_w"])}
    for n in big:
        res[n] = tuple(big[n])

    order = ["norm_mix", "norm_mlp", "ssm_w_in", "ssm_conv_w", "ssm_conv_b", "ssm_dt_bias", "ssm_a_log", "ssm_d",
             "ssm_norm_w", "ssm_w_out", "attn_w_qkv", "attn_w_o", "mlp_w1", "mlp_w2", "final_norm"]
    outs = [loss, dx0.reshape(x.shape)]
    for kind in range(4):
        outs += [res[n][kind] for n in order]
    return tuple(outs)
```

```python
import math

import jax
import jax.numpy as jnp
from jax import lax
from jax.experimental import pallas as pl
from jax.experimental.pallas import tpu as pltpu

F32, BF16 = jnp.float32, jnp.bfloat16
SDS = jax.ShapeDtypeStruct
MESH = pl.DeviceIdType.MESH
HIGHEST = lax.Precision.HIGHEST

NDEV = 8
D_MODEL = 2048
D_INNER = 4096
SSM_HEADS = 64
SSM_HEAD_DIM = 64
SSM_GROUPS = 8
SSM_STATE = 128
CHUNK = 128
CONV_DIM = 6144
CONV_WIDTH = 4
GROUP_W = D_INNER // SSM_GROUPS
ATTN_GROUPS = ((128, 1), (512, 4), (2048, 16))
ATTN_W = 1024
QKV_DIM = 9216
D_FF = 8192
EPS = 1e-5
ADAM_LR, ADAM_B1, ADAM_B2, ADAM_EPS, ADAM_WD, ADAM_STEP = 0.001, 0.9, 0.999, 1e-08, 0.01, 10

VMEM_LIMIT = 48 * 1024 * 1024


def _params(sem):
    return pltpu.CompilerParams(dimension_semantics=sem, vmem_limit_bytes=VMEM_LIMIT)


def _sigmoid(v):
    return 1.0 / (1.0 + jnp.exp(-v))


def _softplus(v):
    return jnp.maximum(v, 0.0) + jnp.log1p(jnp.exp(-jnp.abs(v)))


def _wspec(layout, kw, nw, tr, tc, sel):
    if layout == "plain":
        return pl.BlockSpec((tr, tc), lambda *g: sel(*g))
    per = (nw // NDEV) // tc
    return pl.BlockSpec((None, tr, tc), lambda *g: (sel(*g)[1] // per, sel(*g)[0], sel(*g)[1] % per))


def _wshape(layout, kw, nw):
    return (kw, nw) if layout == "plain" else (NDEV, kw, nw // NDEV)


def _mm_call(name, grid, in_specs, out_specs, out_shape, dims, n_extra, epi, tm, tn, carry):
    nk = grid[2]
    steps = grid[0] * grid[1] * nk
    kind, moved = carry if carry else (None, ())
    nc = len(moved)
    n_out = len(out_shape)

    def body(*refs):
        a_ref, b_ref = refs[0], refs[1]
        extra = refs[2:2 + n_extra]
        c_in = refs[2 + n_extra:2 + n_extra + nc]
        outs = refs[2 + n_extra + nc:2 + n_extra + nc + n_out]
        c_out = refs[2 + n_extra + nc + n_out:2 + n_extra + 2 * nc + n_out]
        acc = refs[2 + n_extra + 2 * nc + n_out]
        sems = refs[3 + n_extra + 2 * nc + n_out:]
        i, j, k = pl.program_id(0), pl.program_id(1), pl.program_id(2)
        step = (i * grid[1] + j) * nk + k
        if nc:
            @pl.when(step == 0)
            def _():
                if kind == "gather":
                    _two_level("start", c_in, c_out, *sems)
                else:
                    _exchange_start(kind, c_in, c_out, *sems)

            if kind == "gather":
                @pl.when(step == (2 * steps) // 3)
                def _():
                    _two_level("pass", c_in, c_out, *sems)

        d = lax.dot_general(a_ref[...].astype(BF16), b_ref[...].astype(BF16), (dims, ((), ())), preferred_element_type=F32)

        def finish(total):
            vals = epi(total, *[e[...] for e in extra])
            for o, v in zip(outs, vals):
                o[...] = v.astype(o.dtype)

        if nk == 1:
            finish(d)
        else:
            @pl.when(k == 0)
            def _():
                acc[...] = d

            @pl.when(jnp.logical_and(k > 0, k < nk - 1))
            def _():
                acc[...] += d

            @pl.when(k == nk - 1)
            def _():
                finish(acc[...] + d)

        if nc:
            @pl.when(step == steps - 1)
            def _():
                if kind == "gather":
                    _two_level("wait", c_in, c_out, *sems)
                else:
                    _exchange_wait(kind, c_in, c_out, *sems)

    hbm = pl.BlockSpec(memory_space=pl.ANY)
    scratch = [pltpu.VMEM((tm, tn), F32)] + (_exchange_sems(nc) if nc else [])
    c_shape = [SDS(((NDEV,) + t.shape) if kind == "gather" else t.shape, t.dtype) for t in moved]
    sem = ("arbitrary",) * 3 if nc else ("parallel", "parallel", "arbitrary")
    return pl.pallas_call(
        body, grid=grid, in_specs=in_specs + [hbm] * nc, out_specs=out_specs + [hbm] * nc,
        out_shape=out_shape + c_shape, scratch_shapes=scratch, name=name, compiler_params=_params(sem))


def _mm_fwd(a, w, layout, kw, nw, *, epi, outs, extras=(), tm=1024, tn=1024, tk=2048, carry=None, name):
    m = a.shape[0]
    tm, tn, tk = min(tm, m), min(tn, nw), min(tk, kw)
    grid = (m // tm, nw // tn, kw // tk)
    o_spec = pl.BlockSpec((tm, tn), lambda i, j, k: (i, j))
    in_specs = [pl.BlockSpec((tm, tk), lambda i, j, k: (i, k)), _wspec(layout, kw, nw, tk, tn, lambda i, j, k: (k, j))]
    in_specs += [o_spec] * len(extras)
    call = _mm_call(name, grid, in_specs, [o_spec] * len(outs), [SDS((m, nw), dt) for dt in outs], ((1,), (0,)),
                    len(extras), epi, tm, tn, carry)
    return call(a, w, *extras, *(carry[1] if carry else ()))


def _mm_dx(g, w, layout, kw, nw, *, epi, outs, extras=(), tm=1024, tn=1024, tk=2048, carry=None, name):
    m = g.shape[0]
    tm, tn, tk = min(tm, m), min(tn, kw), min(tk, nw if layout == "plain" else nw // NDEV)
    grid = (m // tm, kw // tn, nw // tk)
    o_spec = pl.BlockSpec((tm, tn), lambda i, j, k: (i, j))
    in_specs = [pl.BlockSpec((tm, tk), lambda i, j, k: (i, k)), _wspec(layout, kw, nw, tn, tk, lambda i, j, k: (j, k))]
    in_specs += [o_spec] * len(extras)
    call = _mm_call(name, grid, in_specs, [o_spec] * len(outs), [SDS((m, kw), dt) for dt in outs], ((1,), (1,)),
                    len(extras), epi, tm, tn, carry)
    return call(g, w, *extras, *(carry[1] if carry else ()))


def _mm_dw(a, g, layout, *, tm=1024, tn=1024, tk=2048, carry=None, name):
    m, kw = a.shape
    nw = g.shape[1]
    tm, tn, tk = min(tm, kw), min(tn, nw if layout == "plain" else nw // NDEV), min(tk, m)
    grid = (kw // tm, nw // tn, m // tk)
    in_specs = [pl.BlockSpec((tk, tm), lambda i, j, k: (k, i)), pl.BlockSpec((tk, tn), lambda i, j, k: (k, j))]
    o_spec = _wspec(layout, kw, nw, tm, tn, lambda i, j, k: (i, j))
    call = _mm_call(name, grid, in_specs, [o_spec], [SDS(_wshape(layout, kw, nw), BF16)], ((0,), (0,)),
                    0, lambda t: (t,), tm, tn, carry)
    return call(a, g, *(carry[1] if carry else ()))


ROW_TILE = 256


def _rmsnorm_fwd(x, g, name):
    t, d = x.shape

    def body(x_ref, g_ref, h_ref):
        xv = x_ref[...]
        r = lax.rsqrt(jnp.mean(xv * xv, axis=-1, keepdims=True) + EPS)
        h_ref[...] = (xv * r * g_ref[...]).astype(BF16)

    row = pl.BlockSpec((ROW_TILE, d), lambda i: (i, 0))
    vec = pl.BlockSpec((1, d), lambda i: (0, 0))
    return pl.pallas_call(body, grid=(t // ROW_TILE,), in_specs=[row, vec], out_specs=row, out_shape=SDS((t, d), BF16),
                          name=name, compiler_params=_params(("parallel",)))(x, g)


def _rmsnorm_bwd(x, g, dh, dres, name):
    t, d = x.shape

    def body(x_ref, g_ref, dh_ref, dres_ref, dx_ref, dxb_ref, dg_ref):
        xv = x_ref[...]
        r = lax.rsqrt(jnp.mean(xv * xv, axis=-1, keepdims=True) + EPS)
        xh = xv * r
        dhv = dh_ref[...]
        gd = dhv * g_ref[...]
        dx = dres_ref[...] + r * (gd - xh * jnp.mean(gd * xh, axis=-1, keepdims=True))
        dx_ref[...] = dx
        dxb_ref[...] = dx.astype(BF16)
        part = jnp.sum(dhv * xh, axis=0, keepdims=True)

        @pl.when(pl.program_id(0) == 0)
        def _():
            dg_ref[...] = part

        @pl.when(pl.program_id(0) > 0)
        def _():
            dg_ref[...] += part

    row = pl.BlockSpec((ROW_TILE, d), lambda i: (i, 0))
    vec = pl.BlockSpec((1, d), lambda i: (0, 0))
    return pl.pallas_call(body, grid=(t // ROW_TILE,), in_specs=[row, vec, row, row], out_specs=[row, row, vec],
                          out_shape=[SDS((t, d), F32), SDS((t, d), BF16), SDS((1, d), F32)], name=name,
                          compiler_params=_params(("arbitrary",)))(x, g, dh, dres)


def _loss_head(x, tgt, g, name):
    t, d = x.shape

    def body(x_ref, t_ref, g_ref, dx_ref, dxb_ref, loss_ref, dg_ref):
        xv = x_ref[...]
        r = lax.rsqrt(jnp.mean(xv * xv, axis=-1, keepdims=True) + EPS)
        xh = xv * r
        gv = g_ref[...]
        err = xh * gv - t_ref[...]
        part_loss = 0.5 * jnp.sum(jnp.mean(err * err, axis=-1, keepdims=True), axis=0, keepdims=True)
        dy = err * (1.0 / d)
        gd = dy * gv
        dx = r * (gd - xh * jnp.mean(gd * xh, axis=-1, keepdims=True))
        dx_ref[...] = dx
        dxb_ref[...] = dx.astype(BF16)
        part_g = jnp.sum(dy * xh, axis=0, keepdims=True)
        part_l = jnp.broadcast_to(part_loss, (8, 128))

        @pl.when(pl.program_id(0) == 0)
        def _():
            dg_ref[...] = part_g
            loss_ref[...] = part_l

        @pl.when(pl.program_id(0) > 0)
        def _():
            dg_ref[...] += part_g
            loss_ref[...] += part_l

    row = pl.BlockSpec((ROW_TILE, d), lambda i: (i, 0))
    vec = pl.BlockSpec((1, d), lambda i: (0, 0))
    sc = pl.BlockSpec((8, 128), lambda i: (0, 0))
    return pl.pallas_call(body, grid=(t // ROW_TILE,), in_specs=[row, row, vec], out_specs=[row, row, sc, vec],
                          out_shape=[SDS((t, d), F32), SDS((t, d), BF16), SDS((8, 128), F32), SDS((1, d), F32)], name=name,
                          compiler_params=_params(("arbitrary",)))(x, tgt, g)


CONV_ROWS = 256
CONV_COLS = 2048


def _shift_down(cur, prev8, k):
    sh = pltpu.roll(cur, k, axis=0)
    ph = pltpu.roll(prev8, k, axis=0)
    rid = lax.broadcasted_iota(jnp.int32, ph.shape, 0)
    head = jnp.where(rid < k, ph, sh[0:8])
    return jnp.concatenate([head, sh[8:]], axis=0)


def _shift_up(cur, next8, k):
    n = cur.shape[0]
    sh = pltpu.roll(cur, n - k, axis=0)
    nh = pltpu.roll(next8, 8 - k, axis=0)
    rid = lax.broadcasted_iota(jnp.int32, nh.shape, 0)
    tail = jnp.where(rid >= 8 - k, nh, sh[n - 8:])
    return jnp.concatenate([sh[:n - 8], tail], axis=0)


def _conv_pre(cur, prev8, w, b):
    acc = w[3:4, :] * cur + b
    for k in range(1, CONV_WIDTH):
        acc = acc + w[3 - k:4 - k, :] * _shift_down(cur, prev8, k)
    return acc


def _conv_fwd(u, w, b, name):
    t, c = u.shape
    per = CONV_ROWS // 8

    def body(u_ref, p_ref, w_ref, b_ref, o_ref):
        prev8 = jnp.where(pl.program_id(1) == 0, 0.0, p_ref[...])
        pre = _conv_pre(u_ref[...], prev8, w_ref[...], b_ref[...])
        o_ref[...] = pre * _sigmoid(pre)

    cur = pl.BlockSpec((CONV_ROWS, CONV_COLS), lambda j, i: (i, j))
    prev = pl.BlockSpec((8, CONV_COLS), lambda j, i: (jnp.maximum(i * per - 1, 0), j))
    wsp = pl.BlockSpec((CONV_WIDTH, CONV_COLS), lambda j, i: (0, j))
    bsp = pl.BlockSpec((1, CONV_COLS), lambda j, i: (0, j))
    return pl.pallas_call(body, grid=(c // CONV_COLS, t // CONV_ROWS), in_specs=[cur, prev, wsp, bsp], out_specs=cur,
                          out_shape=SDS((t, c), F32), name=name, compiler_params=_params(("parallel", "parallel")))(u, u, w, b)


def _conv_bwd(u, w, b, dout, name):
    t, c = u.shape
    per = CONV_ROWS // 8
    last = t // CONV_ROWS - 1

    def dsilu(pre, d):
        s = _sigmoid(pre)
        return d * (s * (1.0 + pre * (1.0 - s)))

    def body(u_ref, p_ref, n_ref, w_ref, b_ref, d_ref, dn_ref, du_ref, dw_ref):
        i = pl.program_id(1)
        cur = u_ref[...]
        wv, bv = w_ref[...], b_ref[...]
        prev8 = jnp.where(i == 0, 0.0, p_ref[...])
        dpre = dsilu(_conv_pre(cur, prev8, wv, bv), d_ref[...])
        nxt, tail = n_ref[...], cur[CONV_ROWS - 8:]
        rid = lax.broadcasted_iota(jnp.int32, nxt.shape, 0)
        pre_n = wv[3:4, :] * nxt + bv
        for k in range(1, CONV_WIDTH):
            pre_n = pre_n + wv[3 - k:4 - k, :] * jnp.where(rid < k, pltpu.roll(tail, k, axis=0), pltpu.roll(nxt, k, axis=0))
        dpre_n = jnp.where(i == last, 0.0, dsilu(pre_n, dn_ref[...]))
        du = wv[3:4, :] * dpre
        for k in range(1, CONV_WIDTH):
            du = du + wv[3 - k:4 - k, :] * _shift_up(dpre, dpre_n, k)
        du_ref[...] = du.astype(BF16)
        rows = [jnp.sum(dpre * _shift_down(cur, prev8, 3 - k), axis=0, keepdims=True) for k in range(3)]
        rows.append(jnp.sum(dpre * cur, axis=0, keepdims=True))
        rows.append(jnp.sum(dpre, axis=0, keepdims=True))
        part = jnp.concatenate(rows + [jnp.zeros((3, cur.shape[1]), F32)], axis=0)

        @pl.when(i == 0)
        def _():
            dw_ref[...] = part

        @pl.when(i > 0)
        def _():
            dw_ref[...] += part

    cur = pl.BlockSpec((CONV_ROWS, CONV_COLS), lambda j, i: (i, j))
    prev = pl.BlockSpec((8, CONV_COLS), lambda j, i: (jnp.maximum(i * per - 1, 0), j))
    nxt = pl.BlockSpec((8, CONV_COLS), lambda j, i: (jnp.minimum((i + 1) * per, t // 8 - 1), j))
    wsp = pl.BlockSpec((CONV_WIDTH, CONV_COLS), lambda j, i: (0, j))
    bsp = pl.BlockSpec((1, CONV_COLS), lambda j, i: (0, j))
    acc = pl.BlockSpec((8, CONV_COLS), lambda j, i: (0, j))
    return pl.pallas_call(body, grid=(c // CONV_COLS, t // CONV_ROWS), in_specs=[cur, prev, nxt, wsp, bsp, cur, nxt],
                          out_specs=[cur, acc], out_shape=[SDS((t, c), BF16), SDS((8, c), F32)], name=name,
                          compiler_params=_params(("parallel", "arbitrary")))(u, u, u, w, b, dout, dout)


def _pieces(v, n):
    out, rest = [], v
    for _ in range(n):
        p = rest.astype(BF16)
        out.append(p)
        rest = rest - p.astype(F32)
    return out


def _head_expand(v):
    r = lax.broadcasted_iota(jnp.int32, (3 * 128, GROUP_W), 0) % 128
    c = lax.broadcasted_iota(jnp.int32, (3 * 128, GROUP_W), 1)
    return _dot(jnp.concatenate(_pieces(v, 3), axis=1), (c // SSM_HEAD_DIM == r).astype(BF16))


def _head_sum(vs):
    r = lax.broadcasted_iota(jnp.int32, (2 * GROUP_W, 128), 0) % GROUP_W
    c = lax.broadcasted_iota(jnp.int32, (2 * GROUP_W, 128), 1)
    stacked = jnp.concatenate([jnp.concatenate(_pieces(v, 2), axis=1) for v in vs], axis=0)
    out = _dot(stacked, (r // SSM_HEAD_DIM == c).astype(BF16))
    res, at = [], 0
    for v in vs:
        res.append(out[at:at + v.shape[0]])
        at += v.shape[0]
    return res


def _tri_left(tri, v):
    r = _dot(tri.astype(BF16), jnp.concatenate(_pieces(v, 3), axis=1))
    return r[:, 0:128] + r[:, 128:256] + r[:, 256:384]


def _dot(a, b):
    return jnp.dot(a, b, preferred_element_type=F32)


def _dot_nt(a, b):
    return lax.dot_general(a, b, (((1,), (1,)), ((), ())), preferred_element_type=F32)


def _dot_tn(a, b):
    return lax.dot_general(a, b, (((0,), (0,)), ((), ())), preferred_element_type=F32)


def _ssd_common(dtp, dtpt, bias, biasr, alog, alogr):
    li = lax.broadcasted_iota(jnp.int32, (CHUNK, CHUNK), 0)
    si = lax.broadcasted_iota(jnp.int32, (CHUNK, CHUNK), 1)
    lower, upper = (li >= si), (li <= si)
    dt = _softplus(dtp + bias)
    a_neg = -jnp.exp(alog)
    cs = _tri_left(lower, dt * a_neg)
    dtr = _softplus(dtpt + biasr)
    ar = jnp.concatenate([dtr * (-jnp.exp(alogr)), jnp.zeros((8, CHUNK), F32)], axis=0)
    r3 = lax.broadcasted_iota(jnp.int32, (3 * CHUNK, CHUNK), 0) % CHUNK
    c3 = lax.broadcasted_iota(jnp.int32, (3 * CHUNK, CHUNK), 1)
    csr = _dot(jnp.concatenate(_pieces(ar, 3), axis=1), (r3 <= c3).astype(BF16))[0:8]
    return lower, upper, dt, a_neg, cs, csr


def _head_masked_rows(v):
    hl = lax.broadcasted_iota(jnp.int32, v.shape, 1) // SSM_HEAD_DIM
    return jnp.concatenate([jnp.where(hl == j, v, jnp.zeros_like(v)) for j in range(8)], axis=0)


SSD_STEP_CHUNKS = 2


def _chunk_view(ref, kind, k):
    if kind == "rows":
        return ref.at[pl.ds(k * CHUNK, CHUNK), :]
    if kind == "lanes":
        return ref.at[:, pl.ds(k * CHUNK, CHUNK)]
    if kind == "lead":
        return ref.at[k]
    return ref


def _ssd_fwd(xbc, z, dtp_g, dtp_t, bias_g, bias_r, alog_g, alog_r, d_exp, norm_w, name):
    t = xbc.shape[0]
    nc = t // CHUNK

    def body(*refs):
        @pl.when(pl.program_id(1) == 0)
        def _():
            refs[-1][...] = jnp.zeros_like(refs[-1])

        for k in range(SSD_STEP_CHUNKS):
            chunk(*[_chunk_view(r, kind, k) for r, kind in zip(refs, kinds)])

    kinds = ["rows"] * 5 + ["lanes"] + [None] * 6 + ["rows", "rows", "lead", None]

    def chunk(xs_ref, b_ref, c_ref, z_ref, dtp_ref, dtpt_ref, bias_ref, biasr_ref, alog_ref, alogr_ref, dexp_ref, nw_ref,
              y_ref, yn_ref, st_ref, state):
        lower, _, dt, a_neg, cs, csr = _ssd_common(dtp_ref[...], dtpt_ref[...], bias_ref[...], biasr_ref[...],
                                                alog_ref[...], alogr_ref[...])
        cs_e = _head_expand(cs)
        dt_e = _head_expand(dt)
        xs = xs_ref[...]
        xdt = xs * dt_e
        bm = b_ref[...]
        cm = c_ref[...]
        bmb, cmb = bm.astype(BF16), cm.astype(BF16)
        cb = _dot_nt(cmb, bmb)
        ms = []
        for j in range(8):
            dlt = cs_e[:, SSM_HEAD_DIM * j:SSM_HEAD_DIM * j + 1] - csr[j:j + 1, :]
            ms.append((cb * jnp.exp(jnp.where(lower, dlt, -jnp.inf))).astype(BF16))
        y = _dot(jnp.concatenate(ms, axis=1), _head_masked_rows(xdt.astype(BF16)))
        st_in = state[...]
        st_ref[...] = st_in
        y = y + jnp.exp(cs_e) * _dot(cmb, st_in.astype(BF16))
        cs_last = cs_e[CHUNK - 1:CHUNK, :]
        xdtd = (xdt * jnp.exp(cs_last - cs_e)).astype(BF16)
        state[...] = jnp.exp(cs_last) * st_in + _dot(bm.T.astype(BF16), xdtd)
        y_ref[...] = y
        zz = z_ref[...]
        y2 = (y + dexp_ref[...] * xs) * (zz * _sigmoid(zz))
        r = lax.rsqrt(jnp.mean(y2 * y2, axis=-1, keepdims=True) + EPS)
        yn_ref[...] = (y2 * r * nw_ref[...]).astype(BF16)

    rows = SSD_STEP_CHUNKS * CHUNK
    gw = pl.BlockSpec((rows, GROUP_W), lambda g, c: (c, g))
    in_specs = [
        gw,
        pl.BlockSpec((rows, SSM_STATE), lambda g, c: (c, D_INNER // SSM_STATE + g)),
        pl.BlockSpec((rows, SSM_STATE), lambda g, c: (c, D_INNER // SSM_STATE + SSM_GROUPS + g)),
        gw,
        pl.BlockSpec((None, rows, 128), lambda g, c: (g, c, 0)),
        pl.BlockSpec((8, rows), lambda g, c: (g, c)),
        pl.BlockSpec((None, 1, 128), lambda g, c: (g, 0, 0)),
        pl.BlockSpec((8, 128), lambda g, c: (g, 0)),
        pl.BlockSpec((None, 1, 128), lambda g, c: (g, 0, 0)),
        pl.BlockSpec((8, 128), lambda g, c: (g, 0)),
        pl.BlockSpec((1, GROUP_W), lambda g, c: (0, g)),
        pl.BlockSpec((1, GROUP_W), lambda g, c: (0, g)),
    ]
    out_specs = [gw, gw, pl.BlockSpec((SSD_STEP_CHUNKS, SSM_STATE, GROUP_W), lambda g, c: (c, 0, g))]
    out_shape = [SDS((t, D_INNER), F32), SDS((t, D_INNER), BF16), SDS((nc, SSM_STATE, D_INNER), F32)]
    return pl.pallas_call(body, grid=(SSM_GROUPS, nc // SSD_STEP_CHUNKS), in_specs=in_specs, out_specs=out_specs,
                          out_shape=out_shape,
                          scratch_shapes=[pltpu.VMEM((SSM_STATE, GROUP_W), F32)], name=name,
                          compiler_params=_params(("parallel", "arbitrary")))(
        xbc, xbc, xbc, z, dtp_g, dtp_t, bias_g, bias_r, alog_g, alog_r, d_exp, norm_w)


def _ssd_bwd(dyn, y, xbc, z, states, dtp_g, dtp_t, bias_g, bias_r, alog_g, alog_r, d_exp, norm_w, name):
    t = xbc.shape[0]
    nc = t // CHUNK

    def body(*refs):
        step = pl.program_id(1)
        hsum_ref, csum_ref, dstate = refs[-3:]

        @pl.when(step == 0)
        def _():
            dstate[...] = jnp.zeros_like(dstate)

        parts = [chunk(*[_chunk_view(r, kind, k) for r, kind in zip(refs, kinds)])
                 for k in reversed(range(SSD_STEP_CHUNKS))]
        hpart, cpart = parts[0]
        for hp, cp in parts[1:]:
            hpart, cpart = hpart + hp, cpart + cp

        @pl.when(step == 0)
        def _():
            hsum_ref[...] = hpart
            csum_ref[...] = cpart

        @pl.when(step > 0)
        def _():
            hsum_ref[...] += hpart
            csum_ref[...] += cpart

    kinds = ["rows"] * 6 + ["lead", "rows", "lanes"] + [None] * 6 + ["rows"] * 5 + [None] * 3

    def chunk(dyn_ref, y_ref, xs_ref, b_ref, c_ref, z_ref, st_ref, dtp_ref, dtpt_ref, bias_ref, biasr_ref, alog_ref,
              alogr_ref, dexp_ref, nw_ref, dxs_ref, db_ref, dc_ref, dz_ref, ddt_ref, hsum_ref, csum_ref, dstate):
        dtp = dtp_ref[...]
        bias = bias_ref[...]
        lower, upper, dt, a_neg, cs, csr = _ssd_common(dtp, dtpt_ref[...], bias, biasr_ref[...], alog_ref[...],
                                                       alogr_ref[...])
        cs_e = _head_expand(cs)
        dt_e = _head_expand(dt)
        xs = xs_ref[...]
        xdt = xs * dt_e
        bm = b_ref[...]
        cm = c_ref[...]
        bmb, cmb = bm.astype(BF16), cm.astype(BF16)
        y = y_ref[...]
        dexp = dexp_ref[...]
        nw = nw_ref[...]

        zz = z_ref[...]
        sg = _sigmoid(zz)
        gate = zz * sg
        ytot = y + dexp * xs
        y2 = ytot * gate
        r = lax.rsqrt(jnp.mean(y2 * y2, axis=-1, keepdims=True) + EPS)
        dynv = dyn_ref[...]
        xh = y2 * r
        gn = dynv * nw
        dy2 = r * (gn - xh * jnp.mean(gn * xh, axis=-1, keepdims=True))
        dy = dy2 * gate
        dz_ref[...] = (dy2 * ytot * (sg * (1.0 + zz * (1.0 - sg)))).astype(BF16)
        csum_part = jnp.sum(dynv * xh, axis=0, keepdims=True)

        cb = _dot_nt(cmb, bmb)
        dyb = dy.astype(BF16)
        xdtb = xdt.astype(BF16)
        dym = _head_masked_rows(dyb)
        dm = _dot_nt(dym, xdtb)
        dmt = _dot_nt(_head_masked_rows(xdtb), dyb)
        lane = lax.broadcasted_iota(jnp.int32, (CHUNK, 128), 1)
        mts = []
        dcb = jnp.zeros((CHUNK, CHUNK), F32)
        dcs = jnp.zeros((CHUNK, 128), F32)
        for j in range(8):
            dlt = cs_e[:, SSM_HEAD_DIM * j:SSM_HEAD_DIM * j + 1] - csr[j:j + 1, :]
            lj = jnp.exp(jnp.where(lower, dlt, -jnp.inf))
            mj = cb * lj
            mjt = mj.T
            mts.append(mjt.astype(BF16))
            dmj = dm[CHUNK * j:CHUNK * (j + 1)]
            dcb = dcb + dmj * lj
            rows = jnp.sum(dmj * mj, axis=1, keepdims=True)
            cols = jnp.sum(dmt[CHUNK * j:CHUNK * (j + 1)] * mjt, axis=1, keepdims=True)
            dcs = dcs + jnp.where(lane == j, rows - cols, 0.0)
        dxdt = _dot(jnp.concatenate(mts, axis=1), dym)
        dst_out = dstate[...]
        dst_outb = dst_out.astype(BF16)
        st_in = st_ref[...]
        st_inb = st_in.astype(BF16)
        cs_last = cs_e[CHUNK - 1:CHUNK, :]
        decay = jnp.exp(cs_last - cs_e)
        e_last = jnp.exp(cs_last)
        gpart = decay * _dot(bmb, dst_outb)
        dxdt = dxdt + gpart
        dyw = (jnp.exp(cs_e) * dy).astype(BF16)
        dcbb = dcb.astype(BF16)
        dc_ref[...] = _dot_nt(dyw, st_inb) + _dot(dcbb, bmb)
        db_ref[...] = _dot_nt((xdt * decay).astype(BF16), dst_outb) + _dot(dcb.T.astype(BF16), cmb)
        dstate[...] = e_last * dst_out + _dot(cm.T.astype(BF16), dyw)
        y_off = jnp.exp(cs_e) * _dot(cmb, st_inb)
        xg = xdt * gpart
        vec = jnp.concatenate([jnp.sum(dy * xs, axis=0, keepdims=True),
                               jnp.sum(xg + dst_out * e_last * st_in, axis=0, keepdims=True),
                               jnp.zeros((14, GROUP_W), F32)], axis=0)
        s_cs, s_dt, s_vec = _head_sum([dy * y_off - xg, dxdt * xs, vec])
        d_skip = s_vec[0:1]
        ri = lax.broadcasted_iota(jnp.int32, (CHUNK, 128), 0)
        dcs = dcs + s_cs + jnp.where(ri == CHUNK - 1, s_vec[1:2], 0.0)
        da = _tri_left(upper, dcs)
        ddt = da * a_neg + s_dt
        dxs_ref[...] = dxdt * dt_e + dy * dexp
        ddtp = ddt * _sigmoid(dtp + bias)
        ddt_ref[...] = ddtp
        d_alog = jnp.sum(da * dt, axis=0, keepdims=True) * a_neg
        hpart = jnp.concatenate([jnp.sum(ddtp, axis=0, keepdims=True), d_alog, d_skip, jnp.zeros((5, 128), F32)], axis=0)
        cpart = jnp.concatenate([csum_part, jnp.zeros((7, GROUP_W), F32)], axis=0)
        return hpart, cpart

    steps = nc // SSD_STEP_CHUNKS
    rows = SSD_STEP_CHUNKS * CHUNK
    rc = lambda c: steps - 1 - c
    gw = pl.BlockSpec((rows, GROUP_W), lambda g, c: (rc(c), g))
    bsp = pl.BlockSpec((rows, SSM_STATE), lambda g, c: (rc(c), D_INNER // SSM_STATE + g))
    csp = pl.BlockSpec((rows, SSM_STATE), lambda g, c: (rc(c), D_INNER // SSM_STATE + SSM_GROUPS + g))
    in_specs = [
        gw, gw, gw, bsp, csp, gw,
        pl.BlockSpec((SSD_STEP_CHUNKS, SSM_STATE, GROUP_W), lambda g, c: (rc(c), 0, g)),
        pl.BlockSpec((None, rows, 128), lambda g, c: (g, rc(c), 0)),
        pl.BlockSpec((8, rows), lambda g, c: (g, rc(c))),
        pl.BlockSpec((None, 1, 128), lambda g, c: (g, 0, 0)),
        pl.BlockSpec((8, 128), lambda g, c: (g, 0)),
        pl.BlockSpec((None, 1, 128), lambda g, c: (g, 0, 0)),
        pl.BlockSpec((8, 128), lambda g, c: (g, 0)),
        pl.BlockSpec((1, GROUP_W), lambda g, c: (0, g)),
        pl.BlockSpec((1, GROUP_W), lambda g, c: (0, g)),
    ]
    nsp = pl.BlockSpec((rows, SSM_STATE), lambda g, c: (rc(c), g))
    out_specs = [gw, nsp, nsp, gw,
                 pl.BlockSpec((None, rows, 128), lambda g, c: (g, rc(c), 0)),
                 pl.BlockSpec((None, 8, 128), lambda g, c: (g, 0, 0)),
                 pl.BlockSpec((8, GROUP_W), lambda g, c: (0, g))]
    gn = SSM_GROUPS * SSM_STATE
    out_shape = [SDS((t, D_INNER), F32), SDS((t, gn), F32), SDS((t, gn), F32), SDS((t, D_INNER), BF16),
                 SDS((SSM_GROUPS, t, 128), F32), SDS((SSM_GROUPS, 8, 128), F32), SDS((8, D_INNER), F32)]
    return pl.pallas_call(body, grid=(SSM_GROUPS, steps), in_specs=in_specs, out_specs=out_specs, out_shape=out_shape,
                          scratch_shapes=[pltpu.VMEM((SSM_STATE, GROUP_W), F32)], name=name,
                          compiler_params=_params(("parallel", "arbitrary")))(
        dyn, y, xbc, xbc, xbc, z, states, dtp_g, dtp_t, bias_g, bias_r, alog_g, alog_r, d_exp, norm_w)


BLK = 128
HEAD_PAIRS = ATTN_W // 128
ATTN_SCALE = 0.125
SPAN_BLOCKS = {1: 8, 4: 2, 16: 1}


def _slope_table(group):
    n = len(ATTN_GROUPS) * 16
    tbl = [[2.0 ** (-8.0 * (16 * group + 2 * p + s + 1) / n) if s < 2 else 0.0 for s in range(128)] for p in range(HEAD_PAIRS)]
    return jnp.asarray(tbl, F32)


def _lane_lo(rows):
    return lax.broadcasted_iota(jnp.int32, (rows, 128), 1) < 64


def _rows(start, dil):
    return pl.ds(start, BLK, stride=dil) if dil > 1 else pl.ds(start, BLK)


def _stack_heads(x):
    lo = _lane_lo(BLK)
    return jnp.concatenate([jnp.where(lo, x, jnp.zeros_like(x)), jnp.where(lo, jnp.zeros_like(x), x)], axis=0)


def _pair_cols(x):
    return jnp.concatenate([x[:, 0:1], x[:, 64:65]], axis=0)


def _attn_bias(sl, dil, first_span, last_span):
    qi = lax.broadcasted_iota(jnp.int32, (BLK, 2 * BLK), 0)
    kj = lax.broadcasted_iota(jnp.int32, (BLK, 2 * BLK), 1)
    dist = qi + BLK - kj
    valid = (dist >= 0) & (dist <= BLK)
    distf = dist.astype(F32) * float(dil)

    def stacked(ok, d):
        return jnp.concatenate([jnp.where(ok, -sl[:, h:h + 1] * d, -jnp.inf) for h in range(2)], axis=0)

    first = stacked(valid & ((kj >= BLK) | jnp.logical_not(first_span)), distf)
    after = None
    if last_span is not None:
        after = stacked((kj[:, :BLK] >= qi[:, :BLK]) & jnp.logical_not(last_span), distf[:, :BLK])
    return first, stacked(valid, distf), after


def _attn_specs(group, t):
    _, dil = ATTN_GROUPS[group]
    nblk = SPAN_BLOCKS[dil]
    span, edge = BLK * dil * nblk, BLK * dil
    per = span // edge

    def lane_block(which):
        if which is None:
            return lambda p: p
        return lambda p: 3 * (group * HEAD_PAIRS + p) + which

    def cur(which):
        col = lane_block(which)
        return pl.BlockSpec((span, 128), lambda s, p: (s, col(p)))

    def before(which):
        col = lane_block(which)
        return pl.BlockSpec((edge, 128), lambda s, p: (jnp.maximum(s * per - 1, 0), col(p)))

    def after(which):
        col = lane_block(which)
        return pl.BlockSpec((edge, 128), lambda s, p: (jnp.minimum((s + 1) * per, t // edge - 1), col(p)))

    slopes = pl.BlockSpec((HEAD_PAIRS, 128), lambda s, p: (0, 0))
    return dil, nblk, span, cur, before, after, slopes


def _pair_major(w, inverse=False):
    k = w.shape[0]
    g = len(ATTN_GROUPS)
    if inverse:
        return jnp.transpose(w.reshape(k, g, HEAD_PAIRS, 3, 128), (0, 3, 1, 2, 4)).reshape(k, QKV_DIM)
    return jnp.transpose(w.reshape(k, 3, g, HEAD_PAIRS, 128), (0, 2, 3, 1, 4)).reshape(k, QKV_DIM)


def _attn_fwd(qkv, group, name):
    t = qkv.shape[0]
    dil, nblk, span, cur, before, after, slopes = _attn_specs(group, t)

    def body(q_ref, k_ref, v_ref, kp_ref, vp_ref, sl_ref, o_ref, l_ref):
        first_span = pl.program_id(0) == 0
        bias_first, bias_mid, _ = _attn_bias(sl_ref[pl.ds(pl.program_id(1), 1), :], dil, first_span, None)
        lo_q, lo_k = _lane_lo(BLK), _lane_lo(2 * BLK)
        for r in range(dil):
            kp, vp = kp_ref[_rows(r, dil), :].astype(BF16), vp_ref[_rows(r, dil), :].astype(BF16)
            for b in range(nblk):
                rows = _rows(b * BLK * dil + r, dil)
                kc, vc = k_ref[rows, :].astype(BF16), v_ref[rows, :].astype(BF16)
                k2 = jnp.concatenate([kp, kc], axis=0)
                v2 = jnp.concatenate([vp, vc], axis=0)
                kp, vp = kc, vc
                s = _dot_nt(_stack_heads((q_ref[rows, :] * ATTN_SCALE).astype(BF16)), k2) + (bias_first if b == 0 else bias_mid)
                mx = jnp.max(s, axis=-1, keepdims=True)
                p = jnp.exp(s - mx)
                den = jnp.sum(p, axis=-1, keepdims=True)
                pb = p.astype(BF16)
                o_ref[rows, :] = _dot(pb[:BLK], jnp.where(lo_k, v2, jnp.zeros_like(v2))) / den[:BLK] + \
                    _dot(pb[BLK:], jnp.where(lo_k, jnp.zeros_like(v2), v2)) / den[BLK:]
                lse = mx + jnp.log(den)
                l_ref[rows, :] = jnp.where(lo_q, lse[:BLK], lse[BLK:])

    in_specs = [cur(0), cur(1), cur(2), before(1), before(2), slopes]
    return pl.pallas_call(body, grid=(t // span, HEAD_PAIRS), in_specs=in_specs, out_specs=[cur(None), cur(None)],
                          out_shape=[SDS((t, ATTN_W), F32), SDS((t, ATTN_W), F32)], name=name,
                          compiler_params=_params(("parallel", "parallel")))(qkv, qkv, qkv, qkv, qkv, _slope_table(group))


def _attn_bwd(qkv, do, lse, dvec, dqkv, group, name):
    t = qkv.shape[0]
    dil, nblk, span, cur, before, after, slopes = _attn_specs(group, t)
    nspan = t // span

    def body(q_ref, k_ref, v_ref, kp_ref, vp_ref, qn_ref, do_ref, l_ref, d_ref, don_ref, ln_ref, dn_ref, sl_ref, *rest):
        out_ref, dq_s, dk_s, dv_s = rest[-4:]
        span_id, pair_id = pl.program_id(0), pl.program_id(1)
        bias_first, bias_mid, bias_next = _attn_bias(sl_ref[pl.ds(pair_id, 1), :], dil, span_id == 0, span_id == nspan - 1)
        lo_k = _lane_lo(2 * BLK)
        for r in range(dil):
            kp, vp = kp_ref[_rows(r, dil), :].astype(BF16), vp_ref[_rows(r, dil), :].astype(BF16)
            held = None
            for b in range(nblk + 1):
                last = b == nblk
                rows = _rows(r if last else b * BLK * dil + r, dil)
                qs, dos, ls, dvs = (qn_ref, don_ref, ln_ref, dn_ref) if last else (q_ref, do_ref, l_ref, d_ref)
                qst = _stack_heads((qs[rows, :] * ATTN_SCALE).astype(BF16))
                dost = _stack_heads(dos[rows, :].astype(BF16))
                if last:
                    k2, v2, bias = kp, vp, bias_next
                else:
                    kc, vc = k_ref[rows, :].astype(BF16), v_ref[rows, :].astype(BF16)
                    k2, v2 = jnp.concatenate([kp, kc], axis=0), jnp.concatenate([vp, vc], axis=0)
                    kp, vp = kc, vc
                    bias = bias_first if b == 0 else bias_mid
                p = jnp.exp(_dot_nt(qst, k2) + bias - _pair_cols(ls[rows, :]))
                ds = (p * (_dot_nt(dost, v2) - _pair_cols(dvs[rows, :]))).astype(BF16)
                dk2 = _dot_tn(ds, qst)
                dv2 = _dot_tn(p.astype(BF16), dost)
                if held is not None:
                    dk_s[held[0], :] = held[1] + dk2[:BLK]
                    dv_s[held[0], :] = held[2] + dv2[:BLK]
                if not last:
                    k_heads = jnp.concatenate([jnp.where(lo_k, k2, jnp.zeros_like(k2)),
                                               jnp.where(lo_k, jnp.zeros_like(k2), k2)], axis=0)
                    dq_s[rows, :] = _dot(jnp.concatenate([ds[:BLK], ds[BLK:]], axis=1), k_heads) * ATTN_SCALE
                    held = (rows, dk2[BLK:], dv2[BLK:])
        out_ref[:, 0:128] = dq_s[...]
        out_ref[:, 128:256] = dk_s[...]
        out_ref[:, 256:384] = dv_s[...]

    out_spec = pl.BlockSpec((span, 3 * 128), lambda s, p: (s, group * HEAD_PAIRS + p))
    in_specs = [cur(0), cur(1), cur(2), before(1), before(2), after(0), cur(None), cur(None), cur(None),
                after(None), after(None), after(None), slopes]
    args = [qkv, qkv, qkv, qkv, qkv, qkv, do, lse, dvec, do, lse, dvec, _slope_table(group)]
    aliases = {}
    if dqkv is not None:
        in_specs.append(pl.BlockSpec(memory_space=pl.ANY))
        args.append(dqkv)
        aliases = {len(args) - 1: 0}
    return pl.pallas_call(body, grid=(nspan, HEAD_PAIRS), in_specs=in_specs, out_specs=out_spec,
                          out_shape=SDS((t, QKV_DIM), F32), input_output_aliases=aliases,
                          scratch_shapes=[pltpu.VMEM((span, 128), F32)] * 3, name=name,
                          compiler_params=_params(("parallel", "parallel")))(*args)


def _combine_weights(l0, l1, l2):
    mx = jnp.maximum(jnp.maximum(l0, l1), l2)
    e0, e1, e2 = jnp.exp(l0 - mx), jnp.exp(l1 - mx), jnp.exp(l2 - mx)
    den = e0 + e1 + e2
    return e0 / den, e1 / den, e2 / den


def _combine_fwd(os_, ls_, name):
    t = os_[0].shape[0]

    def body(o0, o1, o2, l0, l1, l2, out):
        w0, w1, w2 = _combine_weights(l0[...], l1[...], l2[...])
        out[...] = (w0 * o0[...] + w1 * o1[...] + w2 * o2[...]).astype(BF16)

    row = pl.BlockSpec((ROW_TILE, ATTN_W), lambda i: (i, 0))
    return pl.pallas_call(body, grid=(t // ROW_TILE,), in_specs=[row] * 6, out_specs=row, out_shape=SDS((t, ATTN_W), BF16),
                          name=name, compiler_params=_params(("parallel",)))(*os_, *ls_)


def _combine_bwd(do, os_, ls_, name):
    t = do.shape[0]

    def body(do_ref, o0, o1, o2, l0, l1, l2, g0, g1, g2, d0, d1, d2):
        w0, w1, w2 = _combine_weights(l0[...], l1[...], l2[...])
        dov = do_ref[...]
        prod = dov * (w0 * o0[...] + w1 * o1[...] + w2 * o2[...])
        r = (lax.broadcasted_iota(jnp.int32, (3 * 128, 128), 0) % 128) // 64
        c = lax.broadcasted_iota(jnp.int32, (3 * 128, 128), 1) // 64
        same = (r == c).astype(BF16)
        tbar = jnp.concatenate([_dot(jnp.concatenate(_pieces(prod[:, 128 * k:128 * k + 128], 3), axis=1), same)
                                for k in range(HEAD_PAIRS)], axis=1)
        for w, g, d in ((w0, g0, d0), (w1, g1, d1), (w2, g2, d2)):
            g[...] = w * dov
            d[...] = w * tbar

    row = pl.BlockSpec((ROW_TILE, ATTN_W), lambda i: (i, 0))
    return pl.pallas_call(body, grid=(t // ROW_TILE,), in_specs=[row] * 7, out_specs=[row] * 6,
                          out_shape=[SDS((t, ATTN_W), F32)] * 6, name=name,
                          compiler_params=_params(("parallel",)))(do, *os_, *ls_)


def _peer(k):
    x, y, c = lax.axis_index("x"), lax.axis_index("y"), lax.axis_index("c")
    px = 1 - x if k & 4 else x
    py = 1 - y if k & 2 else y
    pc = 1 - c if k & 1 else c
    return (px, py, pc), 4 * px + 2 * py + pc


def _my_index():
    return 4 * lax.axis_index("x") + 2 * lax.axis_index("y") + lax.axis_index("c")


def _exchange_sems(n):
    return [pltpu.SemaphoreType.DMA((n * (NDEV - 1),)), pltpu.SemaphoreType.DMA((n * (NDEV - 1),)),
            pltpu.SemaphoreType.DMA((n,))]


def _exchange_copies(kind, ins, outs, send, recv, local, arrivals):
    me = _my_index()
    own, sent, arriving = [], [], []
    for i in range(len(ins)):
        own.append(pltpu.make_async_copy(ins[i] if kind == "gather" else ins[i].at[me], outs[i].at[me], local.at[i]))
        for k in range(1, NDEV):
            peer, pidx = _peer(k)
            s = i * (NDEV - 1) + k - 1
            src = ins[i] if kind == "gather" else ins[i].at[pidx]
            for dst, into in ((outs[i].at[me], sent), (outs[i].at[pidx], arriving)):
                if into is sent or arrivals:
                    into.append(pltpu.make_async_remote_copy(src_ref=src, dst_ref=dst, send_sem=send.at[s],
                                                             recv_sem=recv.at[s], device_id=peer, device_id_type=MESH))
    return own, sent, arriving


def _exchange_start(kind, ins, outs, send, recv, local):
    own, sent, _ = _exchange_copies(kind, ins, outs, send, recv, local, arrivals=False)
    for cp in own + sent:
        cp.start()


def _exchange_wait(kind, ins, outs, send, recv, local):
    own, sent, arriving = _exchange_copies(kind, ins, outs, send, recv, local, arrivals=True)
    for cp in sent:
        cp.wait_send()
    for cp in arriving:
        cp.wait_recv()
    for cp in own:
        cp.wait()


def _two_level(phase, ins, outs, send, recv, local):
    n = len(ins)
    x, y, c = lax.axis_index("x"), lax.axis_index("y"), lax.axis_index("c")
    here, sibling = (x, y, c), (x, y, 1 - c)
    chips = [(1 - x, y), (x, 1 - y), (1 - x, 1 - y)]

    def slot(px, py, pc):
        return 4 * px + 2 * py + pc

    def copy(i, k, block, to, src=None):
        return pltpu.make_async_remote_copy(src_ref=outs[i].at[block] if src is None else src, dst_ref=outs[i].at[block],
                                            send_sem=send.at[7 * i + k], recv_sem=recv.at[7 * i + k],
                                            device_id=to, device_id_type=MESH)

    me = slot(x, y, c)
    if phase == "pass":
        for i in range(n):
            for j, chip in enumerate(chips):
                copy(i, 1 + j, slot(*chip, c), here).wait_recv()
                copy(i, 4 + j, slot(*chip, c), sibling).start()
        return
    own = [pltpu.make_async_copy(ins[i], outs[i].at[me], local.at[i]) for i in range(n)]
    first = [copy(i, 0, me, sibling, src=ins[i]) for i in range(n)]
    first += [copy(i, 1 + j, me, (*chip, c), src=ins[i]) for i in range(n) for j, chip in enumerate(chips)]
    if phase == "start":
        for cp in own + first:
            cp.start()
        return
    for i in range(n):
        copy(i, 0, slot(x, y, 1 - c), here).wait_recv()
        for j, chip in enumerate(chips):
            copy(i, 4 + j, slot(*chip, 1 - c), here).wait_recv()
    passed = [copy(i, 4 + j, slot(*chip, c), sibling) for i in range(n) for j, chip in enumerate(chips)]
    for cp in first + passed:
        cp.wait_send()
    for cp in own:
        cp.wait()


def _gather_two_level(tensors, name):
    n = len(tensors)

    def body(*refs):
        for phase in ("start", "pass", "wait"):
            _two_level(phase, refs[:n], refs[n:2 * n], *refs[2 * n:])

    hbm = pl.BlockSpec(memory_space=pl.ANY)
    return pl.pallas_call(body, in_specs=[hbm] * n, out_specs=[hbm] * n,
                          out_shape=[SDS((NDEV,) + t.shape, t.dtype) for t in tensors],
                          scratch_shapes=_exchange_sems(n), name=name)(*tensors)


def _all_reduce_small(v, name):
    rows = v.shape[0]

    def body(v_ref, out_ref, land, send, recv):
        me = _my_index()
        land[me] = v_ref[...]
        remote = []
        for k in range(1, NDEV):
            peer, _ = _peer(k)
            cp = pltpu.make_async_remote_copy(src_ref=v_ref, dst_ref=land.at[me], send_sem=send.at[k - 1],
                                              recv_sem=recv.at[k - 1], device_id=peer, device_id_type=MESH)
            cp.start()
            remote.append(cp)
        for cp in remote:
            cp.wait_send()
        for k in range(1, NDEV):
            peer, pidx = _peer(k)
            pltpu.make_async_remote_copy(src_ref=v_ref, dst_ref=land.at[pidx], send_sem=send.at[k - 1],
                                         recv_sem=recv.at[k - 1], device_id=peer, device_id_type=MESH).wait_recv()
        total = land[0]
        for d in range(1, NDEV):
            total = total + land[d]
        out_ref[...] = total

    vm = pl.BlockSpec(memory_space=pltpu.VMEM)
    return pl.pallas_call(
        body, in_specs=[vm], out_specs=vm, out_shape=SDS((rows, 128), F32),
        scratch_shapes=[pltpu.VMEM((NDEV, rows, 128), F32), pltpu.SemaphoreType.DMA((NDEV - 1,)),
                        pltpu.SemaphoreType.DMA((NDEV - 1,))],
        name=name)(v)


def _adamw_math(w, g, m, v):
    m = ADAM_B1 * m + (1.0 - ADAM_B1) * g
    v = ADAM_B2 * v + (1.0 - ADAM_B2) * (g * g)
    m_hat = m / (1.0 - ADAM_B1 ** ADAM_STEP)
    v_hat = v / (1.0 - ADAM_B2 ** ADAM_STEP)
    delta = -ADAM_LR * (m_hat / (jnp.sqrt(v_hat) + ADAM_EPS) + ADAM_WD * w)
    return delta, m, v


def _row_tile(rows, cols):
    tr = rows
    while tr * cols * 4 > (1 << 20) and tr % 16 == 0:
        tr //= 2
    return tr


def _adamw(g, w, m, v, name):
    rows, cols = w.shape
    tr = _row_tile(rows, cols)

    def body(g_ref, w_ref, m_ref, v_ref, d_out, m_out, v_out):
        d, mn, vn = _adamw_math(w_ref[...], g_ref[...], m_ref[...], v_ref[...])
        d_out[...] = d
        m_out[...] = mn
        v_out[...] = vn

    sp = pl.BlockSpec((tr, cols), lambda i: (i, 0))
    return pl.pallas_call(body, grid=(rows // tr,), in_specs=[sp] * 4, out_specs=[sp] * 3,
                          out_shape=[SDS((rows, cols), F32)] * 3, name=name, compiler_params=_params(("parallel",)))(g, w, m, v)


def _reduce_adamw(parts, w, m, v, name):
    rows, cols = w.shape
    tr = _row_tile(parts[0].shape[1], cols)
    tiles = [p.shape[1] // tr for p in parts]
    starts = [sum(tiles[:h]) for h in range(len(parts))]

    def body(*refs):
        p_refs = refs[:len(parts)]
        w_ref, m_ref, v_ref, g_out, d_out, m_out, v_out = refs[len(parts):]
        i = pl.program_id(0)
        for h, p_ref in enumerate(p_refs):
            @pl.when((i >= starts[h]) & (i < starts[h] + tiles[h]))
            def _(p_ref=p_ref):
                g = p_ref[0].astype(F32)
                for d in range(1, NDEV):
                    g = g + p_ref[d].astype(F32)
                g_out[...] = g
                dl, mn, vn = _adamw_math(w_ref[...], g, m_ref[...], v_ref[...])
                d_out[...] = dl
                m_out[...] = mn
                v_out[...] = vn

    sp = pl.BlockSpec((tr, cols), lambda i: (i, 0))
    psp = [pl.BlockSpec((NDEV, tr, cols), lambda i, h=h: (0, jnp.clip(i - starts[h], 0, tiles[h] - 1), 0))
           for h in range(len(parts))]
    return pl.pallas_call(body, grid=(rows // tr,), in_specs=psp + [sp, sp, sp], out_specs=[sp] * 4,
                          out_shape=[SDS((rows, cols), F32)] * 4, name=name,
                          compiler_params=_params(("parallel",)))(*parts, w, m, v)


def _pack(items):
    rows = []
    for a in items:
        a = a.reshape(-1).astype(F32)
        pad = (-a.shape[0]) % 128
        rows.append(jnp.pad(a, (0, pad)).reshape(-1, 128))
    out = jnp.concatenate(rows, axis=0)
    return jnp.pad(out, ((0, (-out.shape[0]) % 8), (0, 0)))


def _unpack(packed, shapes):
    out, r = [], 0
    for shp in shapes:
        n = math.prod(shp)
        nr = -(-n // 128)
        out.append(packed[r:r + nr].reshape(-1)[:n].reshape(shp))
        r += nr
    return out


def _ident(t):
    return (t,)


def _add(t, res):
    return (t + res,)


def kernel(x, norm_mix, norm_mlp, ssm_w_in, ssm_conv_w, ssm_conv_b, ssm_dt_bias, ssm_a_log, ssm_d, ssm_norm_w, ssm_w_out, attn_w_qkv, attn_w_o, mlp_w1, mlp_w2, final_norm, loss_target, m_norm_mix, m_norm_mlp, m_ssm_w_in, m_ssm_conv_w, m_ssm_conv_b, m_ssm_dt_bias, m_ssm_a_log, m_ssm_d, m_ssm_norm_w, m_ssm_w_out, m_attn_w_qkv, m_attn_w_o, m_mlp_w1, m_mlp_w2, m_final_norm, v_norm_mix, v_norm_mlp, v_ssm_w_in, v_ssm_conv_w, v_ssm_conv_b, v_ssm_dt_bias, v_ssm_a_log, v_ssm_d, v_ssm_norm_w, v_ssm_w_out, v_attn_w_qkv, v_attn_w_o, v_mlp_w1, v_mlp_w2, v_final_norm):
    t = x.shape[1]
    x0 = x.reshape(t, D_MODEL)
    tgt = loss_target.reshape(t, D_MODEL)
    me = _my_index()
    in_dim = D_INNER + CONV_DIM + SSM_HEADS
    in_shard = in_dim // NDEV
    zx_dim = D_INNER + CONV_DIM

    s_out, s_qkv, s_o = ssm_w_out[0].astype(BF16), attn_w_qkv[0].astype(BF16), attn_w_o[0].astype(BF16)
    s_w1, s_w2 = mlp_w1.astype(BF16), mlp_w2.astype(BF16)
    g_in, g_cw = _gather_two_level([ssm_w_in[0].astype(BF16), ssm_conv_w[0]], "gather_in_proj")
    w_in = jnp.transpose(g_in, (1, 0, 2)).reshape(D_MODEL, in_dim)
    w_z, w_x = w_in[:, :D_INNER], w_in[:, D_INNER:zx_dim]
    w_dt = jnp.pad(w_in[:, zx_dim:], ((0, 0), (0, 128 - SSM_HEADS)))
    conv_w = jnp.transpose(g_cw, (1, 0, 2)).reshape(CONV_WIDTH, CONV_DIM)
    g_w1, g_w2 = [None, None], [None, None]

    def lanes(p):
        return jnp.pad(p.reshape(SSM_GROUPS, 1, 8), ((0, 0), (0, 0), (0, 120)))

    def rows(p):
        return jnp.broadcast_to(p.reshape(SSM_HEADS, 1), (SSM_HEADS, 128))

    bias_g, bias_r = lanes(ssm_dt_bias[0]), rows(ssm_dt_bias[0])
    alog_g, alog_r = lanes(ssm_a_log[0]), rows(ssm_a_log[0])
    d_exp = jnp.repeat(ssm_d[0], SSM_HEAD_DIM).reshape(1, D_INNER)
    norm_w = ssm_norm_w

    def relu2(tot):
        r = jnp.maximum(tot, 0.0)
        return r, r * r

    def mlp_fwd(xin, layer, tag, down_carry):
        h = _rmsnorm_fwd(xin, norm_mlp[layer:layer + 1], f"norm_mlp{tag}")
        r, a, w2 = _mm_fwd(h, g_w1[layer], "cols", D_MODEL, D_FF, epi=relu2, outs=(BF16, BF16),
                           carry=("gather", [s_w2[layer]]), name=f"mlp_up{tag}")
        g_w2[layer] = w2.reshape(D_FF, D_MODEL)
        xout, *got = _mm_fwd(a, g_w2[layer], "plain", D_FF, D_MODEL, epi=_add, outs=(F32,), extras=(xin,),
                             carry=("gather", down_carry) if down_carry else None, name=f"mlp_down{tag}")
        return h, r, a, xout, got

    h0 = _rmsnorm_fwd(x0, norm_mix[0:1], "norm_mix0")
    z, g_out = _mm_fwd(h0, w_z, "plain", D_MODEL, D_INNER, epi=_ident, outs=(F32,), carry=("gather", [s_out]), name="ssm_in_z")
    xpre, g_w1[0] = _mm_fwd(h0, w_x, "plain", D_MODEL, CONV_DIM, epi=_ident, outs=(F32,), carry=("gather", [s_w1[0]]),
                            name="ssm_in_x")
    dtp, = _mm_fwd(h0, w_dt, "plain", D_MODEL, 128, epi=_ident, outs=(F32,), name="ssm_in_dt")
    xbc = _conv_fwd(xpre, conv_w, ssm_conv_b, "conv_fwd")
    dtp64 = dtp[:, :SSM_HEADS]
    dtp_g = jnp.pad(jnp.transpose(dtp64.reshape(t, SSM_GROUPS, 8), (1, 0, 2)), ((0, 0), (0, 0), (0, 120)))
    dtp_t = jnp.transpose(dtp64)
    y_ssd, yn, states = _ssd_fwd(xbc, z, dtp_g, dtp_t, bias_g, bias_r, alog_g, alog_r, d_exp, norm_w, "ssd_fwd")
    g_out = g_out.reshape(D_INNER, D_MODEL)
    x1, g_o = _mm_fwd(yn, g_out, "plain", D_INNER, D_MODEL, epi=_add, outs=(F32,), extras=(x0,), carry=("gather", [s_o]),
                      name="ssm_out")
    h1, r1, a1, x2, (g_qkv,) = mlp_fwd(x1, 0, "0", [s_qkv])

    h2 = _rmsnorm_fwd(x2, norm_mix[1:2], "norm_mix1")
    w_qkv = _pair_major(jnp.transpose(g_qkv, (1, 0, 2)).reshape(D_MODEL, QKV_DIM))
    qkv, g_w1[1] = _mm_fwd(h2, w_qkv, "plain", D_MODEL, QKV_DIM, epi=_ident, outs=(F32,),
                           carry=("gather", [s_w1[1]]), name="attn_qkv")
    att = [_attn_fwd(qkv, g, f"attn_fwd{g}") for g in range(3)]
    os_, ls_ = [a[0] for a in att], [a[1] for a in att]
    o_mix = _combine_fwd(os_, ls_, "attn_combine")
    x3, = _mm_fwd(o_mix, g_o, "cols", ATTN_W, D_MODEL, epi=_add, outs=(F32,), extras=(x2,), tn=D_MODEL // NDEV, name="attn_out")
    h3, r3, a3, x4, _ = mlp_fwd(x3, 1, "1", None)

    dx4, dx4b, loss_acc, d_final = _loss_head(x4, tgt, final_norm.reshape(1, D_MODEL), "loss_head")

    def mlp_bwd(xin, h, r, a, dxo, dxob, layer, tag):
        du, = _mm_dx(dxob, g_w2[layer], "plain", D_FF, D_MODEL, epi=lambda tot, rr: (tot * (2.0 * rr.astype(F32)),),
                     outs=(BF16,), extras=(r,), name=f"mlp_down_dx{tag}")
        dw2, = _mm_dw(a, dxob, "plain", name=f"mlp_down_dw{tag}")
        dw2 = dw2.reshape(NDEV, D_FF // NDEV, D_MODEL)
        dw1, p_dw2 = _mm_dw(h, du, "cols", carry=("scatter", [dw2]), name=f"mlp_up_dw{tag}")
        dh, p_dw1 = _mm_dx(du, g_w1[layer], "cols", D_MODEL, D_FF, epi=_ident, outs=(F32,), carry=("scatter", [dw1]),
                           name=f"mlp_up_dx{tag}")
        dxi, dxib, dg = _rmsnorm_bwd(xin, norm_mlp[layer:layer + 1], dh, dxo, f"norm_mlp_bwd{tag}")
        return dxi, dxib, dg, p_dw1, p_dw2

    dx3, dx3b, dg_mlp1, p_w1_1, p_w2_1 = mlp_bwd(x3, h3, r3, a3, dx4, dx4b, 1, "1")

    dw_o, = _mm_dw(o_mix, dx3b, "cols", tn=D_MODEL // NDEV, name="attn_out_dw")
    do, p_o = _mm_dx(dx3b, g_o, "cols", ATTN_W, D_MODEL, epi=_ident, outs=(F32,), tk=D_MODEL // NDEV,
                     carry=("scatter", [dw_o]), name="attn_out_dx")
    cb = _combine_bwd(do, os_, ls_, "attn_combine_bwd")
    dos, dvecs = cb[:3], cb[3:]
    dqkv = None
    for g in range(3):
        dqkv = _attn_bwd(qkv, dos[g], ls_[g], dvecs[g], dqkv, g, f"attn_bwd{g}")
    dw_qkv, = _mm_dw(h2, dqkv, "plain", tk=1024, name="attn_qkv_dw")
    dw_qkv = jnp.transpose(_pair_major(dw_qkv, inverse=True).reshape(D_MODEL, NDEV, QKV_DIM // NDEV), (1, 0, 2))
    dh2, p_qkv = _mm_dx(dqkv, w_qkv, "plain", D_MODEL, QKV_DIM, epi=_ident, outs=(F32,), tk=1536,
                        carry=("scatter", [dw_qkv]), name="attn_qkv_dx")
    dx2, dx2b, dg_mix1 = _rmsnorm_bwd(x2, norm_mix[1:2], dh2, dx3, "norm_mix_bwd1")

    dx1, dx1b, dg_mlp0, p_w1_0, p_w2_0 = mlp_bwd(x1, h1, r1, a1, dx2, dx2b, 0, "0")

    dw_out, = _mm_dw(yn, dx1b, "plain", name="ssm_out_dw")
    dw_out = dw_out.reshape(NDEV, D_INNER // NDEV, D_MODEL)
    dyn, p_out = _mm_dx(dx1b, g_out, "plain", D_INNER, D_MODEL, epi=_ident, outs=(F32,), carry=("scatter", [dw_out]),
                        name="ssm_out_dx")
    dxs, d_b, d_c, dz, ddtp_g, hsums, csums = _ssd_bwd(dyn, y_ssd, xbc, z, states, dtp_g, dtp_t, bias_g, bias_r,
                                                       alog_g, alog_r, d_exp, norm_w, "ssd_bwd")
    dxbc = jnp.concatenate([dxs, d_b, d_c], axis=1)
    du, conv_sums = _conv_bwd(xpre, conv_w, ssm_conv_b, dxbc, "conv_bwd")
    ddtp = jnp.transpose(ddtp_g[:, :, :8], (1, 0, 2)).reshape(t, SSM_HEADS)
    ddtp = jnp.pad(ddtp, ((0, 0), (0, 128 - SSM_HEADS))).astype(BF16)
    dw_z, = _mm_dw(h0, dz, "plain", name="ssm_in_z_dw")
    dw_x, = _mm_dw(h0, du, "plain", name="ssm_in_x_dw")
    dw_dt, = _mm_dw(h0, ddtp, "plain", name="ssm_in_dt_dw")
    half = D_MODEL // 2

    def in_part(lo):
        rows = jnp.concatenate([dw_z[lo:lo + half], dw_x[lo:lo + half], dw_dt[lo:lo + half, :SSM_HEADS]], axis=1)
        return jnp.transpose(rows.reshape(half, NDEV, in_shard), (1, 0, 2))

    dh0, = _mm_dx(ddtp, w_dt, "plain", D_MODEL, 128, epi=_ident, outs=(F32,), name="ssm_in_dt_dx")
    dh0, p_in_a = _mm_dx(dz, w_z, "plain", D_MODEL, D_INNER, epi=_add, outs=(F32,), extras=(dh0,),
                         carry=("scatter", [in_part(0)]), name="ssm_in_z_dx")
    dh0, p_in_b = _mm_dx(du, w_x, "plain", D_MODEL, CONV_DIM, epi=_add, outs=(F32,), extras=(dh0,),
                         carry=("scatter", [in_part(half)]), name="ssm_in_x_dx")
    dx0, _, dg_mix0 = _rmsnorm_bwd(x0, norm_mix[0:1], dh0, dx1, "norm_mix_bwd0")

    parts = [[p_in_a, p_in_b], [p_out], [p_qkv], [p_o], [p_w1_0], [p_w1_1], [p_w2_0], [p_w2_1]]

    def own(w, mm, vv):
        shp = w.shape
        f = lambda a: a.reshape(-1, shp[-1])
        return f(w), f(mm), f(vv), shp

    big = {}
    for key, part, (w, mm, vv) in (
            ("ssm_w_in", parts[0], (ssm_w_in, m_ssm_w_in, v_ssm_w_in)),
            ("ssm_w_out", parts[1], (ssm_w_out, m_ssm_w_out, v_ssm_w_out)),
            ("attn_w_qkv", parts[2], (attn_w_qkv, m_attn_w_qkv, v_attn_w_qkv)),
            ("attn_w_o", parts[3], (attn_w_o, m_attn_w_o, v_attn_w_o))):
        w2, m2, v2, shp = own(w, mm, vv)
        res = _reduce_adamw(part, w2, m2, v2, f"adamw_{key}")
        big[key] = [r.reshape(shp) for r in res]
    for key, pa, pb, (w, mm, vv) in (("mlp_w1", parts[4], parts[5], (mlp_w1, m_mlp_w1, v_mlp_w1)),
                                     ("mlp_w2", parts[6], parts[7], (mlp_w2, m_mlp_w2, v_mlp_w2))):
        res = [_reduce_adamw(p, w[l], mm[l], vv[l], f"adamw_{key}_{l}") for l, p in enumerate((pa, pb))]
        big[key] = [jnp.stack([res[0][i], res[1][i]], axis=0) for i in range(4)]

    d_norm_mix = jnp.concatenate([dg_mix0, dg_mix1], axis=0)
    d_norm_mlp = jnp.concatenate([dg_mlp0, dg_mlp1], axis=0)
    d_conv_b = conv_sums[4:5]
    d_conv_w = conv_sums[0:4]
    head = hsums[:, :, :8]
    d_dt_bias, d_a_log, d_d = (head[:, k, :].reshape(1, SSM_HEADS) for k in range(3))
    d_ssm_norm = csums[0:1]
    small = [d_norm_mix, d_norm_mlp, d_conv_b, d_dt_bias, d_a_log, d_d, d_ssm_norm, d_final, d_conv_w, loss_acc[0:1, 0:1]]
    shapes = [a.shape for a in small]
    summed = _unpack(_all_reduce_small(_pack(small), "reduce_small"), shapes)
    g_conv_w_full = summed[8]
    loss = summed[9].reshape(())
    g_conv_w = lax.dynamic_slice(g_conv_w_full, (0, me * (CONV_DIM // NDEV)), (CONV_WIDTH, CONV_DIM // NDEV))

    small_names = ["norm_mix", "norm_mlp", "ssm_conv_b", "ssm_dt_bias", "ssm_a_log", "ssm_d", "ssm_norm_w", "final_norm"]
    small_w = [norm_mix, norm_mlp, ssm_conv_b, ssm_dt_bias, ssm_a_log, ssm_d, ssm_norm_w, final_norm, ssm_conv_w]
    small_m = [m_norm_mix, m_norm_mlp, m_ssm_conv_b, m_ssm_dt_bias, m_ssm_a_log, m_ssm_d, m_ssm_norm_w, m_final_norm, m_ssm_conv_w]
    small_v = [v_norm_mix, v_norm_mlp, v_ssm_conv_b, v_ssm_dt_bias, v_ssm_a_log, v_ssm_d, v_ssm_norm_w, v_final_norm, v_ssm_conv_w]
    small_g = [summed[i].reshape(small_w[i].shape) for i in range(8)] + [g_conv_w.reshape(ssm_conv_w.shape)]
    wshapes = [a.shape for a in small_w]
    sd, sm, sv = _adamw(_pack(small_g), _pack(small_w), _pack(small_m), _pack(small_v), "adamw_small")
    sd, sm, sv = _unpack(sd, wshapes), _unpack(sm, wshapes), _unpack(sv, wshapes)
    res = {n: (small_g[i], sd[i], sm[i], sv[i]) for i, n in enumerate(small_names + ["ssm_conv_w"])}
    for n in big:
        res[n] = tuple(big[n])

    order = ["norm_mix", "norm_mlp", "ssm_w_in", "ssm_conv_w", "ssm_conv_b", "ssm_dt_bias", "ssm_a_log", "ssm_d",
             "ssm_norm_w", "ssm_w_out", "attn_w_qkv", "attn_w_o", "mlp_w1", "mlp_w2", "final_norm"]
    outs = [loss, dx0.reshape(x.shape)]
    for kind in range(4):
        outs += [res[n][kind] for n in order]
    return tuple(outs)
```

```python
import math

import jax
import jax.numpy as jnp
from jax import lax
from jax.experimental import pallas as pl
from jax.experimental.pallas import tpu as pltpu

F32, BF16 = jnp.float32, jnp.bfloat16
SDS = jax.ShapeDtypeStruct
MESH = pl.DeviceIdType.MESH
HIGHEST = lax.Precision.HIGHEST

NDEV = 8
D_MODEL = 2048
D_INNER = 4096
SSM_HEADS = 64
SSM_HEAD_DIM = 64
SSM_GROUPS = 8
SSM_STATE = 128
CHUNK = 128
CONV_DIM = 6144
CONV_WIDTH = 4
GROUP_W = D_INNER // SSM_GROUPS
ATTN_GROUPS = ((128, 1), (512, 4), (2048, 16))
ATTN_W = 1024
QKV_DIM = 9216
D_FF = 8192
EPS = 1e-5
ADAM_LR, ADAM_B1, ADAM_B2, ADAM_EPS, ADAM_WD, ADAM_STEP = 0.001, 0.9, 0.999, 1e-08, 0.01, 10

VMEM_LIMIT = 48 * 1024 * 1024


def _params(sem):
    return pltpu.CompilerParams(dimension_semantics=sem, vmem_limit_bytes=VMEM_LIMIT)


def _sigmoid(v):
    return 1.0 / (1.0 + jnp.exp(-v))


def _softplus(v):
    return jnp.maximum(v, 0.0) + jnp.log1p(jnp.exp(-jnp.abs(v)))


def _wspec(layout, kw, nw, tr, tc, sel):
    if layout == "plain":
        return pl.BlockSpec((tr, tc), lambda *g: sel(*g))
    per = (nw // NDEV) // tc
    return pl.BlockSpec((None, tr, tc), lambda *g: (sel(*g)[1] // per, sel(*g)[0], sel(*g)[1] % per))


def _wshape(layout, kw, nw):
    return (kw, nw) if layout == "plain" else (NDEV, kw, nw // NDEV)


def _mm_call(name, grid, in_specs, out_specs, out_shape, dims, n_extra, epi, tm, tn, carry):
    nk = grid[2]
    steps = grid[0] * grid[1] * nk
    kind, moved = carry if carry else (None, ())
    nc = len(moved)
    n_out = len(out_shape)

    def body(*refs):
        a_ref, b_ref = refs[0], refs[1]
        extra = refs[2:2 + n_extra]
        c_in = refs[2 + n_extra:2 + n_extra + nc]
        outs = refs[2 + n_extra + nc:2 + n_extra + nc + n_out]
        c_out = refs[2 + n_extra + nc + n_out:2 + n_extra + 2 * nc + n_out]
        acc = refs[2 + n_extra + 2 * nc + n_out]
        sems = refs[3 + n_extra + 2 * nc + n_out:]
        i, j, k = pl.program_id(0), pl.program_id(1), pl.program_id(2)
        step = (i * grid[1] + j) * nk + k
        if nc:
            @pl.when(step == 0)
            def _():
                if kind == "gather":
                    _two_level("start", c_in, c_out, *sems)
                else:
                    _exchange_start(kind, c_in, c_out, *sems)

            if kind == "gather":
                @pl.when(step == (2 * steps) // 3)
                def _():
                    _two_level("pass", c_in, c_out, *sems)

        d = lax.dot_general(a_ref[...].astype(BF16), b_ref[...].astype(BF16), (dims, ((), ())), preferred_element_type=F32)

        def finish(total):
            vals = epi(total, *[e[...] for e in extra])
            for o, v in zip(outs, vals):
                o[...] = v.astype(o.dtype)

        if nk == 1:
            finish(d)
        else:
            @pl.when(k == 0)
            def _():
                acc[...] = d

            @pl.when(jnp.logical_and(k > 0, k < nk - 1))
            def _():
                acc[...] += d

            @pl.when(k == nk - 1)
            def _():
                finish(acc[...] + d)

        if nc:
            @pl.when(step == steps - 1)
            def _():
                if kind == "gather":
                    _two_level("wait", c_in, c_out, *sems)
                else:
                    _exchange_wait(kind, c_in, c_out, *sems)

    hbm = pl.BlockSpec(memory_space=pl.ANY)
    scratch = [pltpu.VMEM((tm, tn), F32)] + (_exchange_sems(nc) if nc else [])
    c_shape = [SDS(((NDEV,) + t.shape) if kind == "gather" else t.shape, t.dtype) for t in moved]
    sem = ("arbitrary",) * 3 if nc else ("parallel", "parallel", "arbitrary")
    return pl.pallas_call(
        body, grid=grid, in_specs=in_specs + [hbm] * nc, out_specs=out_specs + [hbm] * nc,
        out_shape=out_shape + c_shape, scratch_shapes=scratch, name=name, compiler_params=_params(sem))


def _mm_fwd(a, w, layout, kw, nw, *, epi, outs, extras=(), tm=1024, tn=1024, tk=2048, carry=None, name):
    m = a.shape[0]
    tm, tn, tk = min(tm, m), min(tn, nw), min(tk, kw)
    grid = (m // tm, nw // tn, kw // tk)
    o_spec = pl.BlockSpec((tm, tn), lambda i, j, k: (i, j))
    in_specs = [pl.BlockSpec((tm, tk), lambda i, j, k: (i, k)), _wspec(layout, kw, nw, tk, tn, lambda i, j, k: (k, j))]
    in_specs += [o_spec] * len(extras)
    call = _mm_call(name, grid, in_specs, [o_spec] * len(outs), [SDS((m, nw), dt) for dt in outs], ((1,), (0,)),
                    len(extras), epi, tm, tn, carry)
    return call(a, w, *extras, *(carry[1] if carry else ()))


def _mm_dx(g, w, layout, kw, nw, *, epi, outs, extras=(), tm=1024, tn=1024, tk=2048, carry=None, name):
    m = g.shape[0]
    tm, tn, tk = min(tm, m), min(tn, kw), min(tk, nw if layout == "plain" else nw // NDEV)
    grid = (m // tm, kw // tn, nw // tk)
    o_spec = pl.BlockSpec((tm, tn), lambda i, j, k: (i, j))
    in_specs = [pl.BlockSpec((tm, tk), lambda i, j, k: (i, k)), _wspec(layout, kw, nw, tn, tk, lambda i, j, k: (j, k))]
    in_specs += [o_spec] * len(extras)
    call = _mm_call(name, grid, in_specs, [o_spec] * len(outs), [SDS((m, kw), dt) for dt in outs], ((1,), (1,)),
                    len(extras), epi, tm, tn, carry)
    return call(g, w, *extras, *(carry[1] if carry else ()))


def _mm_dw(a, g, layout, *, tm=1024, tn=1024, tk=2048, carry=None, name):
    m, kw = a.shape
    nw = g.shape[1]
    tm, tn, tk = min(tm, kw), min(tn, nw if layout == "plain" else nw // NDEV), min(tk, m)
    grid = (kw // tm, nw // tn, m // tk)
    in_specs = [pl.BlockSpec((tk, tm), lambda i, j, k: (k, i)), pl.BlockSpec((tk, tn), lambda i, j, k: (k, j))]
    o_spec = _wspec(layout, kw, nw, tm, tn, lambda i, j, k: (i, j))
    call = _mm_call(name, grid, in_specs, [o_spec], [SDS(_wshape(layout, kw, nw), BF16)], ((0,), (0,)),
                    0, lambda t: (t,), tm, tn, carry)
    return call(a, g, *(carry[1] if carry else ()))


ROW_TILE = 256


def _rmsnorm_fwd(x, g, name):
    t, d = x.shape

    def body(x_ref, g_ref, h_ref):
        xv = x_ref[...]
        r = lax.rsqrt(jnp.mean(xv * xv, axis=-1, keepdims=True) + EPS)
        h_ref[...] = (xv * r * g_ref[...]).astype(BF16)

    row = pl.BlockSpec((ROW_TILE, d), lambda i: (i, 0))
    vec = pl.BlockSpec((1, d), lambda i: (0, 0))
    return pl.pallas_call(body, grid=(t // ROW_TILE,), in_specs=[row, vec], out_specs=row, out_shape=SDS((t, d), BF16),
                          name=name, compiler_params=_params(("parallel",)))(x, g)


def _rmsnorm_bwd(x, g, dh, dres, name):
    t, d = x.shape

    def body(x_ref, g_ref, dh_ref, dres_ref, dx_ref, dxb_ref, dg_ref):
        xv = x_ref[...]
        r = lax.rsqrt(jnp.mean(xv * xv, axis=-1, keepdims=True) + EPS)
        xh = xv * r
        dhv = dh_ref[...]
        gd = dhv * g_ref[...]
        dx = dres_ref[...] + r * (gd - xh * jnp.mean(gd * xh, axis=-1, keepdims=True))
        dx_ref[...] = dx
        dxb_ref[...] = dx.astype(BF16)
        part = jnp.sum(dhv * xh, axis=0, keepdims=True)

        @pl.when(pl.program_id(0) == 0)
        def _():
            dg_ref[...] = part

        @pl.when(pl.program_id(0) > 0)
        def _():
            dg_ref[...] += part

    row = pl.BlockSpec((ROW_TILE, d), lambda i: (i, 0))
    vec = pl.BlockSpec((1, d), lambda i: (0, 0))
    return pl.pallas_call(body, grid=(t // ROW_TILE,), in_specs=[row, vec, row, row], out_specs=[row, row, vec],
                          out_shape=[SDS((t, d), F32), SDS((t, d), BF16), SDS((1, d), F32)], name=name,
                          compiler_params=_params(("arbitrary",)))(x, g, dh, dres)


def _loss_head(x, tgt, g, name):
    t, d = x.shape

    def body(x_ref, t_ref, g_ref, dx_ref, dxb_ref, loss_ref, dg_ref):
        xv = x_ref[...]
        r = lax.rsqrt(jnp.mean(xv * xv, axis=-1, keepdims=True) + EPS)
        xh = xv * r
        gv = g_ref[...]
        err = xh * gv - t_ref[...]
        part_loss = 0.5 * jnp.sum(jnp.mean(err * err, axis=-1, keepdims=True), axis=0, keepdims=True)
        dy = err * (1.0 / d)
        gd = dy * gv
        dx = r * (gd - xh * jnp.mean(gd * xh, axis=-1, keepdims=True))
        dx_ref[...] = dx
        dxb_ref[...] = dx.astype(BF16)
        part_g = jnp.sum(dy * xh, axis=0, keepdims=True)
        part_l = jnp.broadcast_to(part_loss, (8, 128))

        @pl.when(pl.program_id(0) == 0)
        def _():
            dg_ref[...] = part_g
            loss_ref[...] = part_l

        @pl.when(pl.program_id(0) > 0)
        def _():
            dg_ref[...] += part_g
            loss_ref[...] += part_l

    row = pl.BlockSpec((ROW_TILE, d), lambda i: (i, 0))
    vec = pl.BlockSpec((1, d), lambda i: (0, 0))
    sc = pl.BlockSpec((8, 128), lambda i: (0, 0))
    return pl.pallas_call(body, grid=(t // ROW_TILE,), in_specs=[row, row, vec], out_specs=[row, row, sc, vec],
                          out_shape=[SDS((t, d), F32), SDS((t, d), BF16), SDS((8, 128), F32), SDS((1, d), F32)], name=name,
                          compiler_params=_params(("arbitrary",)))(x, tgt, g)


CONV_ROWS = 256
CONV_COLS = 2048


def _shift_down(cur, prev8, k):
    sh = pltpu.roll(cur, k, axis=0)
    ph = pltpu.roll(prev8, k, axis=0)
    rid = lax.broadcasted_iota(jnp.int32, ph.shape, 0)
    head = jnp.where(rid < k, ph, sh[0:8])
    return jnp.concatenate([head, sh[8:]], axis=0)


def _shift_up(cur, next8, k):
    n = cur.shape[0]
    sh = pltpu.roll(cur, n - k, axis=0)
    nh = pltpu.roll(next8, 8 - k, axis=0)
    rid = lax.broadcasted_iota(jnp.int32, nh.shape, 0)
    tail = jnp.where(rid >= 8 - k, nh, sh[n - 8:])
    return jnp.concatenate([sh[:n - 8], tail], axis=0)


def _conv_pre(cur, prev8, w, b):
    acc = w[3:4, :] * cur + b
    for k in range(1, CONV_WIDTH):
        acc = acc + w[3 - k:4 - k, :] * _shift_down(cur, prev8, k)
    return acc


def _conv_fwd(u, w, b, name):
    t, c = u.shape
    per = CONV_ROWS // 8

    def body(u_ref, p_ref, w_ref, b_ref, o_ref):
        prev8 = jnp.where(pl.program_id(1) == 0, 0.0, p_ref[...])
        pre = _conv_pre(u_ref[...], prev8, w_ref[...], b_ref[...])
        o_ref[...] = pre * _sigmoid(pre)

    cur = pl.BlockSpec((CONV_ROWS, CONV_COLS), lambda j, i: (i, j))
    prev = pl.BlockSpec((8, CONV_COLS), lambda j, i: (jnp.maximum(i * per - 1, 0), j))
    wsp = pl.BlockSpec((CONV_WIDTH, CONV_COLS), lambda j, i: (0, j))
    bsp = pl.BlockSpec((1, CONV_COLS), lambda j, i: (0, j))
    return pl.pallas_call(body, grid=(c // CONV_COLS, t // CONV_ROWS), in_specs=[cur, prev, wsp, bsp], out_specs=cur,
                          out_shape=SDS((t, c), F32), name=name, compiler_params=_params(("parallel", "parallel")))(u, u, w, b)


def _conv_bwd(u, w, b, dout, du_prev, col0, name):
    t, c = u.shape
    per = CONV_ROWS // 8
    last = t // CONV_ROWS - 1

    def dsilu(pre, d):
        s = _sigmoid(pre)
        return d * (s * (1.0 + pre * (1.0 - s)))

    def body(u_ref, p_ref, n_ref, w_ref, b_ref, d_ref, dn_ref, *rest):
        du_ref, dw_ref = rest[-2:]
        i = pl.program_id(1)
        cur = u_ref[...]
        wv, bv = w_ref[...], b_ref[...]
        prev8 = jnp.where(i == 0, 0.0, p_ref[...])
        dpre = dsilu(_conv_pre(cur, prev8, wv, bv), d_ref[...])
        nxt, tail = n_ref[...], cur[CONV_ROWS - 8:]
        rid = lax.broadcasted_iota(jnp.int32, nxt.shape, 0)
        pre_n = wv[3:4, :] * nxt + bv
        for k in range(1, CONV_WIDTH):
            pre_n = pre_n + wv[3 - k:4 - k, :] * jnp.where(rid < k, pltpu.roll(tail, k, axis=0), pltpu.roll(nxt, k, axis=0))
        dpre_n = jnp.where(i == last, 0.0, dsilu(pre_n, dn_ref[...]))
        du = wv[3:4, :] * dpre
        for k in range(1, CONV_WIDTH):
            du = du + wv[3 - k:4 - k, :] * _shift_up(dpre, dpre_n, k)
        du_ref[...] = du.astype(BF16)
        rows = [jnp.sum(dpre * _shift_down(cur, prev8, 3 - k), axis=0, keepdims=True) for k in range(3)]
        rows.append(jnp.sum(dpre * cur, axis=0, keepdims=True))
        rows.append(jnp.sum(dpre, axis=0, keepdims=True))
        part = jnp.concatenate(rows + [jnp.zeros((3, cur.shape[1]), F32)], axis=0)

        @pl.when(i == 0)
        def _():
            dw_ref[...] = part

        @pl.when(i > 0)
        def _():
            dw_ref[...] += part

    width = dout.shape[1]
    tc = min(CONV_COLS, width)
    ob = col0 // tc

    def at(rows, here):
        off = ob if here else 0
        if rows == CONV_ROWS:
            return pl.BlockSpec((rows, tc), lambda j, i: (i, off + j))
        if rows == -8:
            return pl.BlockSpec((8, tc), lambda j, i: (jnp.maximum(i * per - 1, 0), off + j))
        if rows == 8:
            return pl.BlockSpec((8, tc), lambda j, i: (jnp.minimum((i + 1) * per, t // 8 - 1), off + j))
        return pl.BlockSpec((rows, tc), lambda j, i: (0, off + j))

    in_specs = [at(CONV_ROWS, True), at(-8, True), at(8, True), at(CONV_WIDTH, True), at(1, True),
                at(CONV_ROWS, False), at(8, False)]
    args = [u, u, u, w, b, dout, dout]
    aliases = {}
    if du_prev is not None:
        in_specs.append(pl.BlockSpec(memory_space=pl.ANY))
        args.append(du_prev)
        aliases = {len(args) - 1: 0}
    acc = pl.BlockSpec((8, tc), lambda j, i: (0, j))
    return pl.pallas_call(body, grid=(width // tc, t // CONV_ROWS), in_specs=in_specs,
                          out_specs=[at(CONV_ROWS, True), acc], out_shape=[SDS((t, c), BF16), SDS((8, width), F32)],
                          input_output_aliases=aliases, name=name,
                          compiler_params=_params(("parallel", "arbitrary")))(*args)


def _pieces(v, n):
    out, rest = [], v
    for _ in range(n):
        p = rest.astype(BF16)
        out.append(p)
        rest = rest - p.astype(F32)
    return out


def _head_expand(v):
    r = lax.broadcasted_iota(jnp.int32, (3 * 128, GROUP_W), 0) % 128
    c = lax.broadcasted_iota(jnp.int32, (3 * 128, GROUP_W), 1)
    return _dot(jnp.concatenate(_pieces(v, 3), axis=1), (c // SSM_HEAD_DIM == r).astype(BF16))


def _head_sum(vs):
    r = lax.broadcasted_iota(jnp.int32, (2 * GROUP_W, 128), 0) % GROUP_W
    c = lax.broadcasted_iota(jnp.int32, (2 * GROUP_W, 128), 1)
    stacked = jnp.concatenate([jnp.concatenate(_pieces(v, 2), axis=1) for v in vs], axis=0)
    out = _dot(stacked, (r // SSM_HEAD_DIM == c).astype(BF16))
    res, at = [], 0
    for v in vs:
        res.append(out[at:at + v.shape[0]])
        at += v.shape[0]
    return res


def _tri_left(tri, v):
    r = _dot(tri.astype(BF16), jnp.concatenate(_pieces(v, 3), axis=1))
    return r[:, 0:128] + r[:, 128:256] + r[:, 256:384]


def _dot(a, b):
    return jnp.dot(a, b, preferred_element_type=F32)


def _dot_nt(a, b):
    return lax.dot_general(a, b, (((1,), (1,)), ((), ())), preferred_element_type=F32)


def _dot_tn(a, b):
    return lax.dot_general(a, b, (((0,), (0,)), ((), ())), preferred_element_type=F32)


def _ssd_common(dtp, dtpt, bias, biasr, alog, alogr):
    li = lax.broadcasted_iota(jnp.int32, (CHUNK, CHUNK), 0)
    si = lax.broadcasted_iota(jnp.int32, (CHUNK, CHUNK), 1)
    lower, upper = (li >= si), (li <= si)
    dt = _softplus(dtp + bias)
    a_neg = -jnp.exp(alog)
    cs = _tri_left(lower, dt * a_neg)
    dtr = _softplus(dtpt + biasr)
    ar = jnp.concatenate([dtr * (-jnp.exp(alogr)), jnp.zeros((8, CHUNK), F32)], axis=0)
    r3 = lax.broadcasted_iota(jnp.int32, (3 * CHUNK, CHUNK), 0) % CHUNK
    c3 = lax.broadcasted_iota(jnp.int32, (3 * CHUNK, CHUNK), 1)
    csr = _dot(jnp.concatenate(_pieces(ar, 3), axis=1), (r3 <= c3).astype(BF16))[0:8]
    return lower, upper, dt, a_neg, cs, csr


def _head_masked_rows(v):
    hl = lax.broadcasted_iota(jnp.int32, v.shape, 1) // SSM_HEAD_DIM
    return jnp.concatenate([jnp.where(hl == j, v, jnp.zeros_like(v)) for j in range(8)], axis=0)


SSD_STEP_CHUNKS = 4


def _chunk_view(ref, kind, k):
    if kind == "rows":
        return ref.at[pl.ds(k * CHUNK, CHUNK), :]
    if kind == "lanes":
        return ref.at[:, pl.ds(k * CHUNK, CHUNK)]
    if kind == "lead":
        return ref.at[k]
    return ref


def _ssd_fwd(xbc, z, dtp_g, dtp_t, bias_g, bias_r, alog_g, alog_r, d_exp, norm_w, name):
    t = xbc.shape[0]
    nc = t // CHUNK

    def body(*refs):
        @pl.when(pl.program_id(1) == 0)
        def _():
            refs[-1][...] = jnp.zeros_like(refs[-1])

        for k in range(SSD_STEP_CHUNKS):
            chunk(*[_chunk_view(r, kind, k) for r, kind in zip(refs, kinds)])

    kinds = ["rows"] * 5 + ["lanes"] + [None] * 6 + ["rows", "rows", "lead", None]

    def chunk(xs_ref, b_ref, c_ref, z_ref, dtp_ref, dtpt_ref, bias_ref, biasr_ref, alog_ref, alogr_ref, dexp_ref, nw_ref,
              y_ref, yn_ref, st_ref, state):
        lower, _, dt, a_neg, cs, csr = _ssd_common(dtp_ref[...], dtpt_ref[...], bias_ref[...], biasr_ref[...],
                                                alog_ref[...], alogr_ref[...])
        cs_e = _head_expand(cs)
        dt_e = _head_expand(dt)
        xs = xs_ref[...]
        xdt = xs * dt_e
        bm = b_ref[...]
        cm = c_ref[...]
        bmb, cmb = bm.astype(BF16), cm.astype(BF16)
        cb = _dot_nt(cmb, bmb)
        ms = []
        for j in range(8):
            dlt = cs_e[:, SSM_HEAD_DIM * j:SSM_HEAD_DIM * j + 1] - csr[j:j + 1, :]
            ms.append((cb * jnp.exp(jnp.where(lower, dlt, -jnp.inf))).astype(BF16))
        y = _dot(jnp.concatenate(ms, axis=1), _head_masked_rows(xdt.astype(BF16)))
        st_in = state[...]
        st_ref[...] = st_in
        y = y + jnp.exp(cs_e) * _dot(cmb, st_in.astype(BF16))
        cs_last = cs_e[CHUNK - 1:CHUNK, :]
        xdtd = (xdt * jnp.exp(cs_last - cs_e)).astype(BF16)
        state[...] = jnp.exp(cs_last) * st_in + _dot(bm.T.astype(BF16), xdtd)
        y_ref[...] = y
        zz = z_ref[...]
        y2 = (y + dexp_ref[...] * xs) * (zz * _sigmoid(zz))
        r = lax.rsqrt(jnp.mean(y2 * y2, axis=-1, keepdims=True) + EPS)
        yn_ref[...] = (y2 * r * nw_ref[...]).astype(BF16)

    rows = SSD_STEP_CHUNKS * CHUNK
    gw = pl.BlockSpec((rows, GROUP_W), lambda g, c: (c, g))
    in_specs = [
        gw,
        pl.BlockSpec((rows, SSM_STATE), lambda g, c: (c, D_INNER // SSM_STATE + g)),
        pl.BlockSpec((rows, SSM_STATE), lambda g, c: (c, D_INNER // SSM_STATE + SSM_GROUPS + g)),
        gw,
        pl.BlockSpec((None, rows, 128), lambda g, c: (g, c, 0)),
        pl.BlockSpec((8, rows), lambda g, c: (g, c)),
        pl.BlockSpec((None, 1, 128), lambda g, c: (g, 0, 0)),
        pl.BlockSpec((8, 128), lambda g, c: (g, 0)),
        pl.BlockSpec((None, 1, 128), lambda g, c: (g, 0, 0)),
        pl.BlockSpec((8, 128), lambda g, c: (g, 0)),
        pl.BlockSpec((1, GROUP_W), lambda g, c: (0, g)),
        pl.BlockSpec((1, GROUP_W), lambda g, c: (0, g)),
    ]
    out_specs = [gw, gw, pl.BlockSpec((SSD_STEP_CHUNKS, SSM_STATE, GROUP_W), lambda g, c: (c, 0, g))]
    out_shape = [SDS((t, D_INNER), F32), SDS((t, D_INNER), BF16), SDS((nc, SSM_STATE, D_INNER), F32)]
    return pl.pallas_call(body, grid=(SSM_GROUPS, nc // SSD_STEP_CHUNKS), in_specs=in_specs, out_specs=out_specs,
                          out_shape=out_shape,
                          scratch_shapes=[pltpu.VMEM((SSM_STATE, GROUP_W), F32)], name=name,
                          compiler_params=_params(("parallel", "arbitrary")))(
        xbc, xbc, xbc, z, dtp_g, dtp_t, bias_g, bias_r, alog_g, alog_r, d_exp, norm_w)


def _ssd_bwd(dyn, y, xbc, z, states, dtp_g, dtp_t, bias_g, bias_r, alog_g, alog_r, d_exp, norm_w, name):
    t = xbc.shape[0]
    nc = t // CHUNK

    def body(*refs):
        step = pl.program_id(1)
        hsum_ref, csum_ref, dstate = refs[-3:]

        @pl.when(step == 0)
        def _():
            dstate[...] = jnp.zeros_like(dstate)

        parts = [chunk(*[_chunk_view(r, kind, k) for r, kind in zip(refs, kinds)])
                 for k in reversed(range(SSD_STEP_CHUNKS))]
        hpart, cpart = parts[0]
        for hp, cp in parts[1:]:
            hpart, cpart = hpart + hp, cpart + cp

        @pl.when(step == 0)
        def _():
            hsum_ref[...] = hpart
            csum_ref[...] = cpart

        @pl.when(step > 0)
        def _():
            hsum_ref[...] += hpart
            csum_ref[...] += cpart

    kinds = ["rows"] * 6 + ["lead", "rows", "lanes"] + [None] * 6 + ["rows"] * 5 + [None] * 3

    def chunk(dyn_ref, y_ref, xs_ref, b_ref, c_ref, z_ref, st_ref, dtp_ref, dtpt_ref, bias_ref, biasr_ref, alog_ref,
              alogr_ref, dexp_ref, nw_ref, dxs_ref, db_ref, dc_ref, dz_ref, ddt_ref, hsum_ref, csum_ref, dstate):
        dtp = dtp_ref[...]
        bias = bias_ref[...]
        lower, upper, dt, a_neg, cs, csr = _ssd_common(dtp, dtpt_ref[...], bias, biasr_ref[...], alog_ref[...],
                                                       alogr_ref[...])
        cs_e = _head_expand(cs)
        dt_e = _head_expand(dt)
        xs = xs_ref[...]
        xdt = xs * dt_e
        bm = b_ref[...]
        cm = c_ref[...]
        bmb, cmb = bm.astype(BF16), cm.astype(BF16)
        y = y_ref[...]
        dexp = dexp_ref[...]
        nw = nw_ref[...]

        zz = z_ref[...]
        sg = _sigmoid(zz)
        gate = zz * sg
        ytot = y + dexp * xs
        y2 = ytot * gate
        r = lax.rsqrt(jnp.mean(y2 * y2, axis=-1, keepdims=True) + EPS)
        dynv = dyn_ref[...]
        xh = y2 * r
        gn = dynv * nw
        dy2 = r * (gn - xh * jnp.mean(gn * xh, axis=-1, keepdims=True))
        dy = dy2 * gate
        dz_ref[...] = (dy2 * ytot * (sg * (1.0 + zz * (1.0 - sg)))).astype(BF16)
        csum_part = jnp.sum(dynv * xh, axis=0, keepdims=True)

        cb = _dot_nt(cmb, bmb)
        dyb = dy.astype(BF16)
        xdtb = xdt.astype(BF16)
        dym = _head_masked_rows(dyb)
        dm = _dot_nt(dym, xdtb)
        dmt = _dot_nt(_head_masked_rows(xdtb), dyb)
        lane = lax.broadcasted_iota(jnp.int32, (CHUNK, 128), 1)
        mts = []
        dcb = jnp.zeros((CHUNK, CHUNK), F32)
        dcs = jnp.zeros((CHUNK, 128), F32)
        for j in range(8):
            dlt = cs_e[:, SSM_HEAD_DIM * j:SSM_HEAD_DIM * j + 1] - csr[j:j + 1, :]
            lj = jnp.exp(jnp.where(lower, dlt, -jnp.inf))
            mj = cb * lj
            mjt = mj.T
            mts.append(mjt.astype(BF16))
            dmj = dm[CHUNK * j:CHUNK * (j + 1)]
            dcb = dcb + dmj * lj
            rows = jnp.sum(dmj * mj, axis=1, keepdims=True)
            cols = jnp.sum(dmt[CHUNK * j:CHUNK * (j + 1)] * mjt, axis=1, keepdims=True)
            dcs = dcs + jnp.where(lane == j, rows - cols, 0.0)
        dxdt = _dot(jnp.concatenate(mts, axis=1), dym)
        dst_out = dstate[...]
        dst_outb = dst_out.astype(BF16)
        st_in = st_ref[...]
        st_inb = st_in.astype(BF16)
        cs_last = cs_e[CHUNK - 1:CHUNK, :]
        decay = jnp.exp(cs_last - cs_e)
        e_last = jnp.exp(cs_last)
        gpart = decay * _dot(bmb, dst_outb)
        dxdt = dxdt + gpart
        dyw = (jnp.exp(cs_e) * dy).astype(BF16)
        dcbb = dcb.astype(BF16)
        dc_ref[...] = _dot_nt(dyw, st_inb) + _dot(dcbb, bmb)
        db_ref[...] = _dot_nt((xdt * decay).astype(BF16), dst_outb) + _dot(dcb.T.astype(BF16), cmb)
        dstate[...] = e_last * dst_out + _dot(cm.T.astype(BF16), dyw)
        y_off = jnp.exp(cs_e) * _dot(cmb, st_inb)
        xg = xdt * gpart
        vec = jnp.concatenate([jnp.sum(dy * xs, axis=0, keepdims=True),
                               jnp.sum(xg + dst_out * e_last * st_in, axis=0, keepdims=True),
                               jnp.zeros((14, GROUP_W), F32)], axis=0)
        s_cs, s_dt, s_vec = _head_sum([dy * y_off - xg, dxdt * xs, vec])
        d_skip = s_vec[0:1]
        ri = lax.broadcasted_iota(jnp.int32, (CHUNK, 128), 0)
        dcs = dcs + s_cs + jnp.where(ri == CHUNK - 1, s_vec[1:2], 0.0)
        da = _tri_left(upper, dcs)
        ddt = da * a_neg + s_dt
        dxs_ref[...] = dxdt * dt_e + dy * dexp
        ddtp = ddt * _sigmoid(dtp + bias)
        ddt_ref[...] = ddtp
        d_alog = jnp.sum(da * dt, axis=0, keepdims=True) * a_neg
        hpart = jnp.concatenate([jnp.sum(ddtp, axis=0, keepdims=True), d_alog, d_skip, jnp.zeros((5, 128), F32)], axis=0)
        cpart = jnp.concatenate([csum_part, jnp.zeros((7, GROUP_W), F32)], axis=0)
        return hpart, cpart

    steps = nc // SSD_STEP_CHUNKS
    rows = SSD_STEP_CHUNKS * CHUNK
    rc = lambda c: steps - 1 - c
    gw = pl.BlockSpec((rows, GROUP_W), lambda g, c: (rc(c), g))
    bsp = pl.BlockSpec((rows, SSM_STATE), lambda g, c: (rc(c), D_INNER // SSM_STATE + g))
    csp = pl.BlockSpec((rows, SSM_STATE), lambda g, c: (rc(c), D_INNER // SSM_STATE + SSM_GROUPS + g))
    in_specs = [
        gw, gw, gw, bsp, csp, gw,
        pl.BlockSpec((SSD_STEP_CHUNKS, SSM_STATE, GROUP_W), lambda g, c: (rc(c), 0, g)),
        pl.BlockSpec((None, rows, 128), lambda g, c: (g, rc(c), 0)),
        pl.BlockSpec((8, rows), lambda g, c: (g, rc(c))),
        pl.BlockSpec((None, 1, 128), lambda g, c: (g, 0, 0)),
        pl.BlockSpec((8, 128), lambda g, c: (g, 0)),
        pl.BlockSpec((None, 1, 128), lambda g, c: (g, 0, 0)),
        pl.BlockSpec((8, 128), lambda g, c: (g, 0)),
        pl.BlockSpec((1, GROUP_W), lambda g, c: (0, g)),
        pl.BlockSpec((1, GROUP_W), lambda g, c: (0, g)),
    ]
    nsp = pl.BlockSpec((rows, SSM_STATE), lambda g, c: (rc(c), g))
    out_specs = [gw, nsp, nsp, gw,
                 pl.BlockSpec((None, rows, 128), lambda g, c: (g, rc(c), 0)),
                 pl.BlockSpec((None, 8, 128), lambda g, c: (g, 0, 0)),
                 pl.BlockSpec((8, GROUP_W), lambda g, c: (0, g))]
    gn = SSM_GROUPS * SSM_STATE
    out_shape = [SDS((t, D_INNER), F32), SDS((t, gn), F32), SDS((t, gn), F32), SDS((t, D_INNER), BF16),
                 SDS((SSM_GROUPS, t, 128), F32), SDS((SSM_GROUPS, 8, 128), F32), SDS((8, D_INNER), F32)]
    return pl.pallas_call(body, grid=(SSM_GROUPS, steps), in_specs=in_specs, out_specs=out_specs, out_shape=out_shape,
                          scratch_shapes=[pltpu.VMEM((SSM_STATE, GROUP_W), F32)], name=name,
                          compiler_params=_params(("parallel", "arbitrary")))(
        dyn, y, xbc, xbc, xbc, z, states, dtp_g, dtp_t, bias_g, bias_r, alog_g, alog_r, d_exp, norm_w)


BLK = 128
HEAD_PAIRS = ATTN_W // 128
ATTN_SCALE = 0.125
SPAN_BLOCKS = {1: 8, 4: 2, 16: 1}


def _slope_table(group):
    n = len(ATTN_GROUPS) * 16
    tbl = [[2.0 ** (-8.0 * (16 * group + 2 * p + s + 1) / n) if s < 2 else 0.0 for s in range(128)] for p in range(HEAD_PAIRS)]
    return jnp.asarray(tbl, F32)


def _lane_lo(rows):
    return lax.broadcasted_iota(jnp.int32, (rows, 128), 1) < 64


def _rows(start, dil):
    return pl.ds(start, BLK, stride=dil) if dil > 1 else pl.ds(start, BLK)


def _stack_heads(x):
    lo = _lane_lo(BLK)
    return jnp.concatenate([jnp.where(lo, x, jnp.zeros_like(x)), jnp.where(lo, jnp.zeros_like(x), x)], axis=0)


def _pair_cols(x):
    return jnp.concatenate([x[:, 0:1], x[:, 64:65]], axis=0)


def _attn_bias(sl, dil, first_span, last_span):
    qi = lax.broadcasted_iota(jnp.int32, (BLK, 2 * BLK), 0)
    kj = lax.broadcasted_iota(jnp.int32, (BLK, 2 * BLK), 1)
    dist = qi + BLK - kj
    valid = (dist >= 0) & (dist <= BLK)
    distf = dist.astype(F32) * float(dil)

    def stacked(ok, d):
        return jnp.concatenate([jnp.where(ok, -sl[:, h:h + 1] * d, -jnp.inf) for h in range(2)], axis=0)

    first = stacked(valid & ((kj >= BLK) | jnp.logical_not(first_span)), distf)
    after = None
    if last_span is not None:
        after = stacked((kj[:, :BLK] >= qi[:, :BLK]) & jnp.logical_not(last_span), distf[:, :BLK])
    return first, stacked(valid, distf), after


def _attn_specs(group, t):
    _, dil = ATTN_GROUPS[group]
    nblk = SPAN_BLOCKS[dil]
    span, edge = BLK * dil * nblk, BLK * dil
    per = span // edge

    def lane_block(which):
        if which is None:
            return lambda p: p
        return lambda p: 3 * (group * HEAD_PAIRS + p) + which

    def cur(which):
        col = lane_block(which)
        return pl.BlockSpec((span, 128), lambda s, p: (s, col(p)))

    def before(which):
        col = lane_block(which)
        return pl.BlockSpec((edge, 128), lambda s, p: (jnp.maximum(s * per - 1, 0), col(p)))

    def after(which):
        col = lane_block(which)
        return pl.BlockSpec((edge, 128), lambda s, p: (jnp.minimum((s + 1) * per, t // edge - 1), col(p)))

    slopes = pl.BlockSpec((HEAD_PAIRS, 128), lambda s, p: (0, 0))
    return dil, nblk, span, cur, before, after, slopes


def _pair_major(w, inverse=False):
    k = w.shape[0]
    g = len(ATTN_GROUPS)
    if inverse:
        return jnp.transpose(w.reshape(k, g, HEAD_PAIRS, 3, 128), (0, 3, 1, 2, 4)).reshape(k, QKV_DIM)
    return jnp.transpose(w.reshape(k, 3, g, HEAD_PAIRS, 128), (0, 2, 3, 1, 4)).reshape(k, QKV_DIM)


def _attn_fwd(qkv, group, name):
    t = qkv.shape[0]
    dil, nblk, span, cur, before, after, slopes = _attn_specs(group, t)

    def body(q_ref, k_ref, v_ref, kp_ref, vp_ref, sl_ref, o_ref, l_ref):
        first_span = pl.program_id(0) == 0
        bias_first, bias_mid, _ = _attn_bias(sl_ref[pl.ds(pl.program_id(1), 1), :], dil, first_span, None)
        lo_q, lo_k = _lane_lo(BLK), _lane_lo(2 * BLK)
        for r in range(dil):
            kp, vp = kp_ref[_rows(r, dil), :].astype(BF16), vp_ref[_rows(r, dil), :].astype(BF16)
            for b in range(nblk):
                rows = _rows(b * BLK * dil + r, dil)
                kc, vc = k_ref[rows, :].astype(BF16), v_ref[rows, :].astype(BF16)
                k2 = jnp.concatenate([kp, kc], axis=0)
                v2 = jnp.concatenate([vp, vc], axis=0)
                kp, vp = kc, vc
                s = _dot_nt(_stack_heads((q_ref[rows, :] * ATTN_SCALE).astype(BF16)), k2) + (bias_first if b == 0 else bias_mid)
                mx = jnp.max(s, axis=-1, keepdims=True)
                p = jnp.exp(s - mx)
                den = jnp.sum(p, axis=-1, keepdims=True)
                pb = p.astype(BF16)
                o_ref[rows, :] = _dot(pb[:BLK], jnp.where(lo_k, v2, jnp.zeros_like(v2))) / den[:BLK] + \
                    _dot(pb[BLK:], jnp.where(lo_k, jnp.zeros_like(v2), v2)) / den[BLK:]
                lse = mx + jnp.log(den)
                l_ref[rows, :] = jnp.where(lo_q, lse[:BLK], lse[BLK:])

    in_specs = [cur(0), cur(1), cur(2), before(1), before(2), slopes]
    return pl.pallas_call(body, grid=(t // span, HEAD_PAIRS), in_specs=in_specs, out_specs=[cur(None), cur(None)],
                          out_shape=[SDS((t, ATTN_W), F32), SDS((t, ATTN_W), F32)], name=name,
                          compiler_params=_params(("parallel", "parallel")))(qkv, qkv, qkv, qkv, qkv, _slope_table(group))


def _attn_bwd(qkv, do, lse, dvec, dqkv, group, name):
    t = qkv.shape[0]
    dil, nblk, span, cur, before, after, slopes = _attn_specs(group, t)
    nspan = t // span

    def body(q_ref, k_ref, v_ref, kp_ref, vp_ref, qn_ref, do_ref, l_ref, d_ref, don_ref, ln_ref, dn_ref, sl_ref, *rest):
        out_ref, dq_s, dk_s, dv_s = rest[-4:]
        span_id, pair_id = pl.program_id(0), pl.program_id(1)
        bias_first, bias_mid, bias_next = _attn_bias(sl_ref[pl.ds(pair_id, 1), :], dil, span_id == 0, span_id == nspan - 1)
        lo_k = _lane_lo(2 * BLK)
        for r in range(dil):
            kp, vp = kp_ref[_rows(r, dil), :].astype(BF16), vp_ref[_rows(r, dil), :].astype(BF16)
            held = None
            for b in range(nblk + 1):
                last = b == nblk
                rows = _rows(r if last else b * BLK * dil + r, dil)
                qs, dos, ls, dvs = (qn_ref, don_ref, ln_ref, dn_ref) if last else (q_ref, do_ref, l_ref, d_ref)
                qst = _stack_heads((qs[rows, :] * ATTN_SCALE).astype(BF16))
                dost = _stack_heads(dos[rows, :].astype(BF16))
                if last:
                    k2, v2, bias = kp, vp, bias_next
                else:
                    kc, vc = k_ref[rows, :].astype(BF16), v_ref[rows, :].astype(BF16)
                    k2, v2 = jnp.concatenate([kp, kc], axis=0), jnp.concatenate([vp, vc], axis=0)
                    kp, vp = kc, vc
                    bias = bias_first if b == 0 else bias_mid
                p = jnp.exp(_dot_nt(qst, k2) + bias - _pair_cols(ls[rows, :]))
                ds = (p * (_dot_nt(dost, v2) - _pair_cols(dvs[rows, :]))).astype(BF16)
                dk2 = _dot_tn(ds, qst)
                dv2 = _dot_tn(p.astype(BF16), dost)
                if held is not None:
                    dk_s[held[0], :] = held[1] + dk2[:BLK]
                    dv_s[held[0], :] = held[2] + dv2[:BLK]
                if not last:
                    k_heads = jnp.concatenate([jnp.where(lo_k, k2, jnp.zeros_like(k2)),
                                               jnp.where(lo_k, jnp.zeros_like(k2), k2)], axis=0)
                    dq_s[rows, :] = _dot(jnp.concatenate([ds[:BLK], ds[BLK:]], axis=1), k_heads) * ATTN_SCALE
                    held = (rows, dk2[BLK:], dv2[BLK:])
        out_ref[:, 0:128] = dq_s[...]
        out_ref[:, 128:256] = dk_s[...]
        out_ref[:, 256:384] = dv_s[...]

    out_spec = pl.BlockSpec((span, 3 * 128), lambda s, p: (s, group * HEAD_PAIRS + p))
    in_specs = [cur(0), cur(1), cur(2), before(1), before(2), after(0), cur(None), cur(None), cur(None),
                after(None), after(None), after(None), slopes]
    args = [qkv, qkv, qkv, qkv, qkv, qkv, do, lse, dvec, do, lse, dvec, _slope_table(group)]
    aliases = {}
    if dqkv is not None:
        in_specs.append(pl.BlockSpec(memory_space=pl.ANY))
        args.append(dqkv)
        aliases = {len(args) - 1: 0}
    return pl.pallas_call(body, grid=(nspan, HEAD_PAIRS), in_specs=in_specs, out_specs=out_spec,
                          out_shape=SDS((t, QKV_DIM), F32), input_output_aliases=aliases,
                          scratch_shapes=[pltpu.VMEM((span, 128), F32)] * 3, name=name,
                          compiler_params=_params(("parallel", "parallel")))(*args)


def _combine_weights(l0, l1, l2):
    mx = jnp.maximum(jnp.maximum(l0, l1), l2)
    e0, e1, e2 = jnp.exp(l0 - mx), jnp.exp(l1 - mx), jnp.exp(l2 - mx)
    den = e0 + e1 + e2
    return e0 / den, e1 / den, e2 / den


def _combine_fwd(os_, ls_, name):
    t = os_[0].shape[0]

    def body(o0, o1, o2, l0, l1, l2, out):
        w0, w1, w2 = _combine_weights(l0[...], l1[...], l2[...])
        out[...] = (w0 * o0[...] + w1 * o1[...] + w2 * o2[...]).astype(BF16)

    row = pl.BlockSpec((ROW_TILE, ATTN_W), lambda i: (i, 0))
    return pl.pallas_call(body, grid=(t // ROW_TILE,), in_specs=[row] * 6, out_specs=row, out_shape=SDS((t, ATTN_W), BF16),
                          name=name, compiler_params=_params(("parallel",)))(*os_, *ls_)


def _combine_bwd(do, os_, ls_, name):
    t = do.shape[0]

    def body(do_ref, o0, o1, o2, l0, l1, l2, g0, g1, g2, d0, d1, d2):
        w0, w1, w2 = _combine_weights(l0[...], l1[...], l2[...])
        dov = do_ref[...]
        prod = dov * (w0 * o0[...] + w1 * o1[...] + w2 * o2[...])
        r = (lax.broadcasted_iota(jnp.int32, (3 * 128, 128), 0) % 128) // 64
        c = lax.broadcasted_iota(jnp.int32, (3 * 128, 128), 1) // 64
        same = (r == c).astype(BF16)
        tbar = jnp.concatenate([_dot(jnp.concatenate(_pieces(prod[:, 128 * k:128 * k + 128], 3), axis=1), same)
                                for k in range(HEAD_PAIRS)], axis=1)
        for w, g, d in ((w0, g0, d0), (w1, g1, d1), (w2, g2, d2)):
            g[...] = w * dov
            d[...] = w * tbar

    row = pl.BlockSpec((ROW_TILE, ATTN_W), lambda i: (i, 0))
    return pl.pallas_call(body, grid=(t // ROW_TILE,), in_specs=[row] * 7, out_specs=[row] * 6,
                          out_shape=[SDS((t, ATTN_W), F32)] * 6, name=name,
                          compiler_params=_params(("parallel",)))(do, *os_, *ls_)


def _peer(k):
    x, y, c = lax.axis_index("x"), lax.axis_index("y"), lax.axis_index("c")
    px = 1 - x if k & 4 else x
    py = 1 - y if k & 2 else y
    pc = 1 - c if k & 1 else c
    return (px, py, pc), 4 * px + 2 * py + pc


def _my_index():
    return 4 * lax.axis_index("x") + 2 * lax.axis_index("y") + lax.axis_index("c")


def _exchange_sems(n):
    return [pltpu.SemaphoreType.DMA((n * (NDEV - 1),)), pltpu.SemaphoreType.DMA((n * (NDEV - 1),)),
            pltpu.SemaphoreType.DMA((n,))]


def _exchange_copies(kind, ins, outs, send, recv, local, arrivals):
    me = _my_index()
    own, sent, arriving = [], [], []
    for i in range(len(ins)):
        own.append(pltpu.make_async_copy(ins[i] if kind == "gather" else ins[i].at[me], outs[i].at[me], local.at[i]))
        for k in range(1, NDEV):
            peer, pidx = _peer(k)
            s = i * (NDEV - 1) + k - 1
            src = ins[i] if kind == "gather" else ins[i].at[pidx]
            for dst, into in ((outs[i].at[me], sent), (outs[i].at[pidx], arriving)):
                if into is sent or arrivals:
                    into.append(pltpu.make_async_remote_copy(src_ref=src, dst_ref=dst, send_sem=send.at[s],
                                                             recv_sem=recv.at[s], device_id=peer, device_id_type=MESH))
    return own, sent, arriving


def _exchange_start(kind, ins, outs, send, recv, local):
    own, sent, _ = _exchange_copies(kind, ins, outs, send, recv, local, arrivals=False)
    for cp in own + sent:
        cp.start()


def _exchange_wait(kind, ins, outs, send, recv, local):
    own, sent, arriving = _exchange_copies(kind, ins, outs, send, recv, local, arrivals=True)
    for cp in sent:
        cp.wait_send()
    for cp in arriving:
        cp.wait_recv()
    for cp in own:
        cp.wait()


def _two_level(phase, ins, outs, send, recv, local):
    n = len(ins)
    x, y, c = lax.axis_index("x"), lax.axis_index("y"), lax.axis_index("c")
    here, sibling = (x, y, c), (x, y, 1 - c)
    chips = [(1 - x, y), (x, 1 - y), (1 - x, 1 - y)]

    def slot(px, py, pc):
        return 4 * px + 2 * py + pc

    def copy(i, k, block, to, src=None):
        return pltpu.make_async_remote_copy(src_ref=outs[i].at[block] if src is None else src, dst_ref=outs[i].at[block],
                                            send_sem=send.at[7 * i + k], recv_sem=recv.at[7 * i + k],
                                            device_id=to, device_id_type=MESH)

    me = slot(x, y, c)
    if phase == "pass":
        for i in range(n):
            for j, chip in enumerate(chips):
                copy(i, 1 + j, slot(*chip, c), here).wait_recv()
                copy(i, 4 + j, slot(*chip, c), sibling).start()
        return
    own = [pltpu.make_async_copy(ins[i], outs[i].at[me], local.at[i]) for i in range(n)]
    first = [copy(i, 0, me, sibling, src=ins[i]) for i in range(n)]
    first += [copy(i, 1 + j, me, (*chip, c), src=ins[i]) for i in range(n) for j, chip in enumerate(chips)]
    if phase == "start":
        for cp in own + first:
            cp.start()
        return
    for i in range(n):
        copy(i, 0, slot(x, y, 1 - c), here).wait_recv()
        for j, chip in enumerate(chips):
            copy(i, 4 + j, slot(*chip, 1 - c), here).wait_recv()
    passed = [copy(i, 4 + j, slot(*chip, c), sibling) for i in range(n) for j, chip in enumerate(chips)]
    for cp in first + passed:
        cp.wait_send()
    for cp in own:
        cp.wait()


def _gather_two_level(tensors, name):
    n = len(tensors)

    def body(*refs):
        for phase in ("start", "pass", "wait"):
            _two_level(phase, refs[:n], refs[n:2 * n], *refs[2 * n:])

    hbm = pl.BlockSpec(memory_space=pl.ANY)
    return pl.pallas_call(body, in_specs=[hbm] * n, out_specs=[hbm] * n,
                          out_shape=[SDS((NDEV,) + t.shape, t.dtype) for t in tensors],
                          scratch_shapes=_exchange_sems(n), name=name)(*tensors)


def _all_reduce_small(v, name):
    rows = v.shape[0]

    def body(v_ref, out_ref, land, send, recv):
        me = _my_index()
        land[me] = v_ref[...]
        remote = []
        for k in range(1, NDEV):
            peer, _ = _peer(k)
            cp = pltpu.make_async_remote_copy(src_ref=v_ref, dst_ref=land.at[me], send_sem=send.at[k - 1],
                                              recv_sem=recv.at[k - 1], device_id=peer, device_id_type=MESH)
            cp.start()
            remote.append(cp)
        for cp in remote:
            cp.wait_send()
        for k in range(1, NDEV):
            peer, pidx = _peer(k)
            pltpu.make_async_remote_copy(src_ref=v_ref, dst_ref=land.at[pidx], send_sem=send.at[k - 1],
                                         recv_sem=recv.at[k - 1], device_id=peer, device_id_type=MESH).wait_recv()
        total = land[0]
        for d in range(1, NDEV):
            total = total + land[d]
        out_ref[...] = total

    vm = pl.BlockSpec(memory_space=pltpu.VMEM)
    return pl.pallas_call(
        body, in_specs=[vm], out_specs=vm, out_shape=SDS((rows, 128), F32),
        scratch_shapes=[pltpu.VMEM((NDEV, rows, 128), F32), pltpu.SemaphoreType.DMA((NDEV - 1,)),
                        pltpu.SemaphoreType.DMA((NDEV - 1,))],
        name=name)(v)


def _adamw_math(w, g, m, v):
    m = ADAM_B1 * m + (1.0 - ADAM_B1) * g
    v = ADAM_B2 * v + (1.0 - ADAM_B2) * (g * g)
    m_hat = m / (1.0 - ADAM_B1 ** ADAM_STEP)
    v_hat = v / (1.0 - ADAM_B2 ** ADAM_STEP)
    delta = -ADAM_LR * (m_hat / (jnp.sqrt(v_hat) + ADAM_EPS) + ADAM_WD * w)
    return delta, m, v


def _row_tile(rows, cols):
    tr = rows
    while tr * cols * 4 > (1 << 20) and tr % 16 == 0:
        tr //= 2
    return tr


def _adamw(g, w, m, v, name):
    rows, cols = w.shape
    tr = _row_tile(rows, cols)

    def body(g_ref, w_ref, m_ref, v_ref, d_out, m_out, v_out):
        d, mn, vn = _adamw_math(w_ref[...], g_ref[...], m_ref[...], v_ref[...])
        d_out[...] = d
        m_out[...] = mn
        v_out[...] = vn

    sp = pl.BlockSpec((tr, cols), lambda i: (i, 0))
    return pl.pallas_call(body, grid=(rows // tr,), in_specs=[sp] * 4, out_specs=[sp] * 3,
                          out_shape=[SDS((rows, cols), F32)] * 3, name=name, compiler_params=_params(("parallel",)))(g, w, m, v)


def _reduce_adamw(parts, w, m, v, name):
    rows, cols = w.shape
    tr = _row_tile(parts[0].shape[1], cols)
    tiles = [p.shape[1] // tr for p in parts]
    starts = [sum(tiles[:h]) for h in range(len(parts))]

    def body(*refs):
        p_refs = refs[:len(parts)]
        w_ref, m_ref, v_ref, g_out, d_out, m_out, v_out = refs[len(parts):]
        i = pl.program_id(0)
        for h, p_ref in enumerate(p_refs):
            @pl.when((i >= starts[h]) & (i < starts[h] + tiles[h]))
            def _(p_ref=p_ref):
                g = p_ref[0].astype(F32)
                for d in range(1, NDEV):
                    g = g + p_ref[d].astype(F32)
                g_out[...] = g
                dl, mn, vn = _adamw_math(w_ref[...], g, m_ref[...], v_ref[...])
                d_out[...] = dl
                m_out[...] = mn
                v_out[...] = vn

    sp = pl.BlockSpec((tr, cols), lambda i: (i, 0))
    psp = [pl.BlockSpec((NDEV, tr, cols), lambda i, h=h: (0, jnp.clip(i - starts[h], 0, tiles[h] - 1), 0))
           for h in range(len(parts))]
    return pl.pallas_call(body, grid=(rows // tr,), in_specs=psp + [sp, sp, sp], out_specs=[sp] * 4,
                          out_shape=[SDS((rows, cols), F32)] * 4, name=name,
                          compiler_params=_params(("parallel",)))(*parts, w, m, v)


def _pack(items):
    rows = []
    for a in items:
        a = a.reshape(-1).astype(F32)
        pad = (-a.shape[0]) % 128
        rows.append(jnp.pad(a, (0, pad)).reshape(-1, 128))
    out = jnp.concatenate(rows, axis=0)
    return jnp.pad(out, ((0, (-out.shape[0]) % 8), (0, 0)))


def _unpack(packed, shapes):
    out, r = [], 0
    for shp in shapes:
        n = math.prod(shp)
        nr = -(-n // 128)
        out.append(packed[r:r + nr].reshape(-1)[:n].reshape(shp))
        r += nr
    return out


def _ident(t):
    return (t,)


def _add(t, res):
    return (t + res,)


def kernel(x, norm_mix, norm_mlp, ssm_w_in, ssm_conv_w, ssm_conv_b, ssm_dt_bias, ssm_a_log, ssm_d, ssm_norm_w, ssm_w_out, attn_w_qkv, attn_w_o, mlp_w1, mlp_w2, final_norm, loss_target, m_norm_mix, m_norm_mlp, m_ssm_w_in, m_ssm_conv_w, m_ssm_conv_b, m_ssm_dt_bias, m_ssm_a_log, m_ssm_d, m_ssm_norm_w, m_ssm_w_out, m_attn_w_qkv, m_attn_w_o, m_mlp_w1, m_mlp_w2, m_final_norm, v_norm_mix, v_norm_mlp, v_ssm_w_in, v_ssm_conv_w, v_ssm_conv_b, v_ssm_dt_bias, v_ssm_a_log, v_ssm_d, v_ssm_norm_w, v_ssm_w_out, v_attn_w_qkv, v_attn_w_o, v_mlp_w1, v_mlp_w2, v_final_norm):
    t = x.shape[1]
    x0 = x.reshape(t, D_MODEL)
    tgt = loss_target.reshape(t, D_MODEL)
    me = _my_index()
    in_dim = D_INNER + CONV_DIM + SSM_HEADS
    in_shard = in_dim // NDEV
    zx_dim = D_INNER + CONV_DIM

    s_out, s_qkv, s_o = ssm_w_out[0].astype(BF16), attn_w_qkv[0].astype(BF16), attn_w_o[0].astype(BF16)
    s_w1, s_w2 = mlp_w1.astype(BF16), mlp_w2.astype(BF16)
    g_in, g_cw = _gather_two_level([ssm_w_in[0].astype(BF16), ssm_conv_w[0]], "gather_in_proj")
    w_in = jnp.transpose(g_in, (1, 0, 2)).reshape(D_MODEL, in_dim)
    w_z, w_x = w_in[:, :D_INNER], w_in[:, D_INNER:zx_dim]
    w_dt = jnp.pad(w_in[:, zx_dim:], ((0, 0), (0, 128 - SSM_HEADS)))
    conv_w = jnp.transpose(g_cw, (1, 0, 2)).reshape(CONV_WIDTH, CONV_DIM)
    g_w1, g_w2 = [None, None], [None, None]

    def lanes(p):
        return jnp.pad(p.reshape(SSM_GROUPS, 1, 8), ((0, 0), (0, 0), (0, 120)))

    def rows(p):
        return jnp.broadcast_to(p.reshape(SSM_HEADS, 1), (SSM_HEADS, 128))

    bias_g, bias_r = lanes(ssm_dt_bias[0]), rows(ssm_dt_bias[0])
    alog_g, alog_r = lanes(ssm_a_log[0]), rows(ssm_a_log[0])
    d_exp = jnp.repeat(ssm_d[0], SSM_HEAD_DIM).reshape(1, D_INNER)
    norm_w = ssm_norm_w

    def relu2(tot):
        r = jnp.maximum(tot, 0.0)
        return r, r * r

    def mlp_fwd(xin, layer, tag, down_carry):
        h = _rmsnorm_fwd(xin, norm_mlp[layer:layer + 1], f"norm_mlp{tag}")
        r, a, w2 = _mm_fwd(h, g_w1[layer], "cols", D_MODEL, D_FF, epi=relu2, outs=(BF16, BF16),
                           carry=("gather", [s_w2[layer]]), name=f"mlp_up{tag}")
        g_w2[layer] = w2.reshape(D_FF, D_MODEL)
        xout, *got = _mm_fwd(a, g_w2[layer], "plain", D_FF, D_MODEL, epi=_add, outs=(F32,), extras=(xin,),
                             carry=("gather", down_carry) if down_carry else None, name=f"mlp_down{tag}")
        return h, r, a, xout, got

    h0 = _rmsnorm_fwd(x0, norm_mix[0:1], "norm_mix0")
    z, g_out = _mm_fwd(h0, w_z, "plain", D_MODEL, D_INNER, epi=_ident, outs=(F32,), carry=("gather", [s_out]), name="ssm_in_z")
    xpre, g_w1[0] = _mm_fwd(h0, w_x, "plain", D_MODEL, CONV_DIM, epi=_ident, outs=(F32,), carry=("gather", [s_w1[0]]),
                            name="ssm_in_x")
    dtp, = _mm_fwd(h0, w_dt, "plain", D_MODEL, 128, epi=_ident, outs=(F32,), name="ssm_in_dt")
    xbc = _conv_fwd(xpre, conv_w, ssm_conv_b, "conv_fwd")
    dtp64 = dtp[:, :SSM_HEADS]
    dtp_g = jnp.pad(jnp.transpose(dtp64.reshape(t, SSM_GROUPS, 8), (1, 0, 2)), ((0, 0), (0, 0), (0, 120)))
    dtp_t = jnp.transpose(dtp64)
    y_ssd, yn, states = _ssd_fwd(xbc, z, dtp_g, dtp_t, bias_g, bias_r, alog_g, alog_r, d_exp, norm_w, "ssd_fwd")
    g_out = g_out.reshape(D_INNER, D_MODEL)
    x1, g_o = _mm_fwd(yn, g_out, "plain", D_INNER, D_MODEL, epi=_add, outs=(F32,), extras=(x0,), carry=("gather", [s_o]),
                      name="ssm_out")
    h1, r1, a1, x2, (g_qkv,) = mlp_fwd(x1, 0, "0", [s_qkv])

    h2 = _rmsnorm_fwd(x2, norm_mix[1:2], "norm_mix1")
    w_qkv = _pair_major(jnp.transpose(g_qkv, (1, 0, 2)).reshape(D_MODEL, QKV_DIM))
    qkv, g_w1[1] = _mm_fwd(h2, w_qkv, "plain", D_MODEL, QKV_DIM, epi=_ident, outs=(F32,),
                           carry=("gather", [s_w1[1]]), name="attn_qkv")
    att = [_attn_fwd(qkv, g, f"attn_fwd{g}") for g in range(3)]
    os_, ls_ = [a[0] for a in att], [a[1] for a in att]
    o_mix = _combine_fwd(os_, ls_, "attn_combine")
    x3, = _mm_fwd(o_mix, g_o, "cols", ATTN_W, D_MODEL, epi=_add, outs=(F32,), extras=(x2,), tn=D_MODEL // NDEV, name="attn_out")
    h3, r3, a3, x4, _ = mlp_fwd(x3, 1, "1", None)

    dx4, dx4b, loss_acc, d_final = _loss_head(x4, tgt, final_norm.reshape(1, D_MODEL), "loss_head")

    def mlp_bwd(xin, h, r, a, dxo, dxob, layer, tag):
        du, = _mm_dx(dxob, g_w2[layer], "plain", D_FF, D_MODEL, epi=lambda tot, rr: (tot * (2.0 * rr.astype(F32)),),
                     outs=(BF16,), extras=(r,), name=f"mlp_down_dx{tag}")
        dw2, = _mm_dw(a, dxob, "plain", name=f"mlp_down_dw{tag}")
        dw2 = dw2.reshape(NDEV, D_FF // NDEV, D_MODEL)
        dw1, p_dw2 = _mm_dw(h, du, "cols", carry=("scatter", [dw2]), name=f"mlp_up_dw{tag}")
        dh, p_dw1 = _mm_dx(du, g_w1[layer], "cols", D_MODEL, D_FF, epi=_ident, outs=(F32,), carry=("scatter", [dw1]),
                           name=f"mlp_up_dx{tag}")
        dxi, dxib, dg = _rmsnorm_bwd(xin, norm_mlp[layer:layer + 1], dh, dxo, f"norm_mlp_bwd{tag}")
        return dxi, dxib, dg, p_dw1, p_dw2

    dx3, dx3b, dg_mlp1, p_w1_1, p_w2_1 = mlp_bwd(x3, h3, r3, a3, dx4, dx4b, 1, "1")

    dw_o, = _mm_dw(o_mix, dx3b, "cols", tn=D_MODEL // NDEV, name="attn_out_dw")
    do, p_o = _mm_dx(dx3b, g_o, "cols", ATTN_W, D_MODEL, epi=_ident, outs=(F32,), tk=D_MODEL // NDEV,
                     carry=("scatter", [dw_o]), name="attn_out_dx")
    cb = _combine_bwd(do, os_, ls_, "attn_combine_bwd")
    dos, dvecs = cb[:3], cb[3:]
    dqkv = None
    for g in range(3):
        dqkv = _attn_bwd(qkv, dos[g], ls_[g], dvecs[g], dqkv, g, f"attn_bwd{g}")
    dw_qkv, = _mm_dw(h2, dqkv, "plain", tk=1024, name="attn_qkv_dw")
    dw_qkv = jnp.transpose(_pair_major(dw_qkv, inverse=True).reshape(D_MODEL, NDEV, QKV_DIM // NDEV), (1, 0, 2))
    dh2, p_qkv = _mm_dx(dqkv, w_qkv, "plain", D_MODEL, QKV_DIM, epi=_ident, outs=(F32,), tk=1536,
                        carry=("scatter", [dw_qkv]), name="attn_qkv_dx")
    dx2, dx2b, dg_mix1 = _rmsnorm_bwd(x2, norm_mix[1:2], dh2, dx3, "norm_mix_bwd1")

    dx1, dx1b, dg_mlp0, p_w1_0, p_w2_0 = mlp_bwd(x1, h1, r1, a1, dx2, dx2b, 0, "0")

    dw_out, = _mm_dw(yn, dx1b, "plain", name="ssm_out_dw")
    dw_out = dw_out.reshape(NDEV, D_INNER // NDEV, D_MODEL)
    dyn, p_out = _mm_dx(dx1b, g_out, "plain", D_INNER, D_MODEL, epi=_ident, outs=(F32,), carry=("scatter", [dw_out]),
                        name="ssm_out_dx")
    dxs, d_b, d_c, dz, ddtp_g, hsums, csums = _ssd_bwd(dyn, y_ssd, xbc, z, states, dtp_g, dtp_t, bias_g, bias_r,
                                                       alog_g, alog_r, d_exp, norm_w, "ssd_bwd")
    du, sums = None, []
    for part, col0, tag in ((dxs, 0, "x"), (d_b, D_INNER, "b"), (d_c, D_INNER + SSM_GROUPS * SSM_STATE, "c")):
        du, s = _conv_bwd(xpre, conv_w, ssm_conv_b, part, du, col0, f"conv_bwd_{tag}")
        sums.append(s)
    conv_sums = jnp.concatenate(sums, axis=1)
    ddtp = jnp.transpose(ddtp_g[:, :, :8], (1, 0, 2)).reshape(t, SSM_HEADS)
    ddtp = jnp.pad(ddtp, ((0, 0), (0, 128 - SSM_HEADS))).astype(BF16)
    dw_z, = _mm_dw(h0, dz, "plain", name="ssm_in_z_dw")
    dw_x, = _mm_dw(h0, du, "plain", name="ssm_in_x_dw")
    dw_dt, = _mm_dw(h0, ddtp, "plain", name="ssm_in_dt_dw")
    half = D_MODEL // 2

    def in_part(lo):
        rows = jnp.concatenate([dw_z[lo:lo + half], dw_x[lo:lo + half], dw_dt[lo:lo + half, :SSM_HEADS]], axis=1)
        return jnp.transpose(rows.reshape(half, NDEV, in_shard), (1, 0, 2))

    dh0, = _mm_dx(ddtp, w_dt, "plain", D_MODEL, 128, epi=_ident, outs=(F32,), name="ssm_in_dt_dx")
    dh0, p_in_a = _mm_dx(dz, w_z, "plain", D_MODEL, D_INNER, epi=_add, outs=(F32,), extras=(dh0,),
                         carry=("scatter", [in_part(0)]), name="ssm_in_z_dx")
    dh0, p_in_b = _mm_dx(du, w_x, "plain", D_MODEL, CONV_DIM, epi=_add, outs=(F32,), extras=(dh0,),
                         carry=("scatter", [in_part(half)]), name="ssm_in_x_dx")
    dx0, _, dg_mix0 = _rmsnorm_bwd(x0, norm_mix[0:1], dh0, dx1, "norm_mix_bwd0")

    parts = [[p_in_a, p_in_b], [p_out], [p_qkv], [p_o], [p_w1_0], [p_w1_1], [p_w2_0], [p_w2_1]]

    def own(w, mm, vv):
        shp = w.shape
        f = lambda a: a.reshape(-1, shp[-1])
        return f(w), f(mm), f(vv), shp

    big = {}
    for key, part, (w, mm, vv) in (
            ("ssm_w_in", parts[0], (ssm_w_in, m_ssm_w_in, v_ssm_w_in)),
            ("ssm_w_out", parts[1], (ssm_w_out, m_ssm_w_out, v_ssm_w_out)),
            ("attn_w_qkv", parts[2], (attn_w_qkv, m_attn_w_qkv, v_attn_w_qkv)),
            ("attn_w_o", parts[3], (attn_w_o, m_attn_w_o, v_attn_w_o))):
        w2, m2, v2, shp = own(w, mm, vv)
        res = _reduce_adamw(part, w2, m2, v2, f"adamw_{key}")
        big[key] = [r.reshape(shp) for r in res]
    for key, pa, pb, (w, mm, vv) in (("mlp_w1", parts[4], parts[5], (mlp_w1, m_mlp_w1, v_mlp_w1)),
                                     ("mlp_w2", parts[6], parts[7], (mlp_w2, m_mlp_w2, v_mlp_w2))):
        res = [_reduce_adamw(p, w[l], mm[l], vv[l], f"adamw_{key}_{l}") for l, p in enumerate((pa, pb))]
        big[key] = [jnp.stack([res[0][i], res[1][i]], axis=0) for i in range(4)]

    d_norm_mix = jnp.concatenate([dg_mix0, dg_mix1], axis=0)
    d_norm_mlp = jnp.concatenate([dg_mlp0, dg_mlp1], axis=0)
    d_conv_b = conv_sums[4:5]
    d_conv_w = conv_sums[0:4]
    head = hsums[:, :, :8]
    d_dt_bias, d_a_log, d_d = (head[:, k, :].reshape(1, SSM_HEADS) for k in range(3))
    d_ssm_norm = csums[0:1]
    small = [d_norm_mix, d_norm_mlp, d_conv_b, d_dt_bias, d_a_log, d_d, d_ssm_norm, d_final, d_conv_w, loss_acc[0:1, 0:1]]
    shapes = [a.shape for a in small]
    summed = _unpack(_all_reduce_small(_pack(small), "reduce_small"), shapes)
    g_conv_w_full = summed[8]
    loss = summed[9].reshape(())
    g_conv_w = lax.dynamic_slice(g_conv_w_full, (0, me * (CONV_DIM // NDEV)), (CONV_WIDTH, CONV_DIM // NDEV))

    small_names = ["norm_mix", "norm_mlp", "ssm_conv_b", "ssm_dt_bias", "ssm_a_log", "ssm_d", "ssm_norm_w", "final_norm"]
    small_w = [norm_mix, norm_mlp, ssm_conv_b, ssm_dt_bias, ssm_a_log, ssm_d, ssm_norm_w, final_norm, ssm_conv_w]
    small_m = [m_norm_mix, m_norm_mlp, m_ssm_conv_b, m_ssm_dt_bias, m_ssm_a_log, m_ssm_d, m_ssm_norm_w, m_final_norm, m_ssm_conv_w]
    small_v = [v_norm_mix, v_norm_mlp, v_ssm_conv_b, v_ssm_dt_bias, v_ssm_a_log, v_ssm_d, v_ssm_norm_w, v_final_norm, v_ssm_conv_w]
    small_g = [summed[i].reshape(small_w[i].shape) for i in range(8)] + [g_conv_w.reshape(ssm_conv_w.shape)]
    wshapes = [a.shape for a in small_w]
    sd, sm, sv = _adamw(_pack(small_g), _pack(small_w), _pack(small_m), _pack(small_v), "adamw_small")
    sd, sm, sv = _unpack(sd, wshapes), _unpack(sm, wshapes), _unpack(sv, wshapes)
    res = {n: (small_g[i], sd[i], sm[i], sv[i]) for i, n in enumerate(small_names + ["ssm_conv_w"])}
    for n in big:
        res[n] = tuple(big[n])

    order = ["norm_mix", "norm_mlp", "ssm_w_in", "ssm_conv_w", "ssm_conv_b", "ssm_dt_bias", "ssm_a_log", "ssm_d",
             "ssm_norm_w", "ssm_w_out", "attn_w_qkv", "attn_w_o", "mlp_w1", "mlp_w2", "final_norm"]
    outs = [loss, dx0.reshape(x.shape)]
    for kind in range(4):
        outs += [res[n][kind] for n in order]
    return tuple(outs)
```

```python
import math

import jax
import jax.numpy as jnp
from jax import lax
from jax.experimental import pallas as pl
from jax.experimental.pallas import tpu as pltpu

F32, BF16 = jnp.float32, jnp.bfloat16
SDS = jax.ShapeDtypeStruct
MESH = pl.DeviceIdType.MESH
HIGHEST = lax.Precision.HIGHEST

NDEV = 8
D_MODEL = 2048
D_INNER = 4096
SSM_HEADS = 64
SSM_HEAD_DIM = 64
SSM_GROUPS = 8
SSM_STATE = 128
CHUNK = 128
CONV_DIM = 6144
CONV_WIDTH = 4
GROUP_W = D_INNER // SSM_GROUPS
ATTN_GROUPS = ((128, 1), (512, 4), (2048, 16))
ATTN_W = 1024
QKV_DIM = 9216
D_FF = 8192
EPS = 1e-5
ADAM_LR, ADAM_B1, ADAM_B2, ADAM_EPS, ADAM_WD, ADAM_STEP = 0.001, 0.9, 0.999, 1e-08, 0.01, 10

VMEM_LIMIT = 48 * 1024 * 1024


def _params(sem):
    return pltpu.CompilerParams(dimension_semantics=sem, vmem_limit_bytes=VMEM_LIMIT)


def _sigmoid(v):
    return 1.0 / (1.0 + jnp.exp(-v))


def _softplus(v):
    return jnp.maximum(v, 0.0) + jnp.log1p(jnp.exp(-jnp.abs(v)))


def _wspec(layout, kw, nw, tr, tc, sel):
    if layout == "plain":
        return pl.BlockSpec((tr, tc), lambda *g: sel(*g))
    per = (nw // NDEV) // tc
    return pl.BlockSpec((None, tr, tc), lambda *g: (sel(*g)[1] // per, sel(*g)[0], sel(*g)[1] % per))


def _wshape(layout, kw, nw):
    return (kw, nw) if layout == "plain" else (NDEV, kw, nw // NDEV)


def _mm_call(name, grid, in_specs, out_specs, out_shape, dims, n_extra, epi, tm, tn, carry):
    nk = grid[2]
    steps = grid[0] * grid[1] * nk
    kind, moved = carry[:2] if carry else (None, ())
    spans = carry[2] if carry and len(carry) > 2 else [None] * len(moved)
    nc = len(moved)
    n_out = len(out_shape)

    def body(*refs):
        a_ref, b_ref = refs[0], refs[1]
        extra = refs[2:2 + n_extra]
        c_in = refs[2 + n_extra:2 + n_extra + nc]
        outs = refs[2 + n_extra + nc:2 + n_extra + nc + n_out]
        c_out = refs[2 + n_extra + nc + n_out:2 + n_extra + 2 * nc + n_out]
        acc = refs[2 + n_extra + 2 * nc + n_out]
        sems = refs[3 + n_extra + 2 * nc + n_out:]
        i, j, k = pl.program_id(0), pl.program_id(1), pl.program_id(2)
        step = (i * grid[1] + j) * nk + k
        if nc:
            @pl.when(step == 0)
            def _():
                if kind == "gather":
                    _two_level("start", c_in, c_out, *sems)
                else:
                    _exchange_start(spans, c_in, c_out, *sems)

            if kind == "gather":
                @pl.when(step == (2 * steps) // 3)
                def _():
                    _two_level("pass", c_in, c_out, *sems)

        d = lax.dot_general(a_ref[...].astype(BF16), b_ref[...].astype(BF16), (dims, ((), ())), preferred_element_type=F32)

        def finish(total):
            vals = epi(total, *[e[...] for e in extra])
            for o, v in zip(outs, vals):
                o[...] = v.astype(o.dtype)

        if nk == 1:
            finish(d)
        else:
            @pl.when(k == 0)
            def _():
                acc[...] = d

            @pl.when(jnp.logical_and(k > 0, k < nk - 1))
            def _():
                acc[...] += d

            @pl.when(k == nk - 1)
            def _():
                finish(acc[...] + d)

        if nc:
            @pl.when(step == steps - 1)
            def _():
                if kind == "gather":
                    _two_level("wait", c_in, c_out, *sems)
                else:
                    _exchange_wait(spans, c_in, c_out, *sems)

    hbm = pl.BlockSpec(memory_space=pl.ANY)
    scratch = [pltpu.VMEM((tm, tn), F32)] + (_exchange_sems(nc) if nc else [])
    c_shape = [SDS(((NDEV,) + t.shape) if kind == "gather" else t.shape, t.dtype) for t in moved]
    sem = ("arbitrary",) * 3 if nc else ("parallel", "parallel", "arbitrary")
    return pl.pallas_call(
        body, grid=grid, in_specs=in_specs + [hbm] * nc, out_specs=out_specs + [hbm] * nc,
        out_shape=out_shape + c_shape, scratch_shapes=scratch, name=name, compiler_params=_params(sem))


def _mm_fwd(a, w, layout, kw, nw, *, epi, outs, extras=(), tm=1024, tn=1024, tk=2048, carry=None, name):
    m = a.shape[0]
    tm, tn, tk = min(tm, m), min(tn, nw), min(tk, kw)
    grid = (m // tm, nw // tn, kw // tk)
    o_spec = pl.BlockSpec((tm, tn), lambda i, j, k: (i, j))
    in_specs = [pl.BlockSpec((tm, tk), lambda i, j, k: (i, k)), _wspec(layout, kw, nw, tk, tn, lambda i, j, k: (k, j))]
    in_specs += [o_spec] * len(extras)
    call = _mm_call(name, grid, in_specs, [o_spec] * len(outs), [SDS((m, nw), dt) for dt in outs], ((1,), (0,)),
                    len(extras), epi, tm, tn, carry)
    return call(a, w, *extras, *(carry[1] if carry else ()))


def _mm_dx(g, w, layout, kw, nw, *, epi, outs, extras=(), tm=1024, tn=1024, tk=2048, carry=None, name):
    m = g.shape[0]
    tm, tn, tk = min(tm, m), min(tn, kw), min(tk, nw if layout == "plain" else nw // NDEV)
    grid = (m // tm, kw // tn, nw // tk)
    o_spec = pl.BlockSpec((tm, tn), lambda i, j, k: (i, j))
    in_specs = [pl.BlockSpec((tm, tk), lambda i, j, k: (i, k)), _wspec(layout, kw, nw, tn, tk, lambda i, j, k: (j, k))]
    in_specs += [o_spec] * len(extras)
    call = _mm_call(name, grid, in_specs, [o_spec] * len(outs), [SDS((m, kw), dt) for dt in outs], ((1,), (1,)),
                    len(extras), epi, tm, tn, carry)
    return call(g, w, *extras, *(carry[1] if carry else ()))


def _mm_dw(a, g, layout, *, tm=1024, tn=1024, tk=2048, carry=None, name):
    m, kw = a.shape
    nw = g.shape[1]
    tm, tn, tk = min(tm, kw), min(tn, nw if layout == "plain" else nw // NDEV), min(tk, m)
    grid = (kw // tm, nw // tn, m // tk)
    in_specs = [pl.BlockSpec((tk, tm), lambda i, j, k: (k, i)), pl.BlockSpec((tk, tn), lambda i, j, k: (k, j))]
    o_spec = _wspec(layout, kw, nw, tm, tn, lambda i, j, k: (i, j))
    call = _mm_call(name, grid, in_specs, [o_spec], [SDS(_wshape(layout, kw, nw), BF16)], ((0,), (0,)),
                    0, lambda t: (t,), tm, tn, carry)
    return call(a, g, *(carry[1] if carry else ()))


ROW_TILE = 256


def _rmsnorm_fwd(x, g, name):
    t, d = x.shape

    def body(x_ref, g_ref, h_ref):
        xv = x_ref[...]
        r = lax.rsqrt(jnp.mean(xv * xv, axis=-1, keepdims=True) + EPS)
        h_ref[...] = (xv * r * g_ref[...]).astype(BF16)

    row = pl.BlockSpec((ROW_TILE, d), lambda i: (i, 0))
    vec = pl.BlockSpec((1, d), lambda i: (0, 0))
    return pl.pallas_call(body, grid=(t // ROW_TILE,), in_specs=[row, vec], out_specs=row, out_shape=SDS((t, d), BF16),
                          name=name, compiler_params=_params(("parallel",)))(x, g)


def _rmsnorm_bwd(x, g, dh, dres, name):
    t, d = x.shape

    def body(x_ref, g_ref, dh_ref, dres_ref, dx_ref, dxb_ref, dg_ref):
        xv = x_ref[...]
        r = lax.rsqrt(jnp.mean(xv * xv, axis=-1, keepdims=True) + EPS)
        xh = xv * r
        dhv = dh_ref[...]
        gd = dhv * g_ref[...]
        dx = dres_ref[...] + r * (gd - xh * jnp.mean(gd * xh, axis=-1, keepdims=True))
        dx_ref[...] = dx
        dxb_ref[...] = dx.astype(BF16)
        part = jnp.sum(dhv * xh, axis=0, keepdims=True)

        @pl.when(pl.program_id(0) == 0)
        def _():
            dg_ref[...] = part

        @pl.when(pl.program_id(0) > 0)
        def _():
            dg_ref[...] += part

    row = pl.BlockSpec((ROW_TILE, d), lambda i: (i, 0))
    vec = pl.BlockSpec((1, d), lambda i: (0, 0))
    return pl.pallas_call(body, grid=(t // ROW_TILE,), in_specs=[row, vec, row, row], out_specs=[row, row, vec],
                          out_shape=[SDS((t, d), F32), SDS((t, d), BF16), SDS((1, d), F32)], name=name,
                          compiler_params=_params(("arbitrary",)))(x, g, dh, dres)


def _loss_head(x, tgt, g, name):
    t, d = x.shape

    def body(x_ref, t_ref, g_ref, dx_ref, dxb_ref, loss_ref, dg_ref):
        xv = x_ref[...]
        r = lax.rsqrt(jnp.mean(xv * xv, axis=-1, keepdims=True) + EPS)
        xh = xv * r
        gv = g_ref[...]
        err = xh * gv - t_ref[...]
        part_loss = 0.5 * jnp.sum(jnp.mean(err * err, axis=-1, keepdims=True), axis=0, keepdims=True)
        dy = err * (1.0 / d)
        gd = dy * gv
        dx = r * (gd - xh * jnp.mean(gd * xh, axis=-1, keepdims=True))
        dx_ref[...] = dx
        dxb_ref[...] = dx.astype(BF16)
        part_g = jnp.sum(dy * xh, axis=0, keepdims=True)
        part_l = jnp.broadcast_to(part_loss, (8, 128))

        @pl.when(pl.program_id(0) == 0)
        def _():
            dg_ref[...] = part_g
            loss_ref[...] = part_l

        @pl.when(pl.program_id(0) > 0)
        def _():
            dg_ref[...] += part_g
            loss_ref[...] += part_l

    row = pl.BlockSpec((ROW_TILE, d), lambda i: (i, 0))
    vec = pl.BlockSpec((1, d), lambda i: (0, 0))
    sc = pl.BlockSpec((8, 128), lambda i: (0, 0))
    return pl.pallas_call(body, grid=(t // ROW_TILE,), in_specs=[row, row, vec], out_specs=[row, row, sc, vec],
                          out_shape=[SDS((t, d), F32), SDS((t, d), BF16), SDS((8, 128), F32), SDS((1, d), F32)], name=name,
                          compiler_params=_params(("arbitrary",)))(x, tgt, g)


CONV_ROWS = 256
CONV_COLS = 2048


def _shift_down(cur, prev8, k):
    sh = pltpu.roll(cur, k, axis=0)
    ph = pltpu.roll(prev8, k, axis=0)
    rid = lax.broadcasted_iota(jnp.int32, ph.shape, 0)
    head = jnp.where(rid < k, ph, sh[0:8])
    return jnp.concatenate([head, sh[8:]], axis=0)


def _shift_up(cur, next8, k):
    n = cur.shape[0]
    sh = pltpu.roll(cur, n - k, axis=0)
    nh = pltpu.roll(next8, 8 - k, axis=0)
    rid = lax.broadcasted_iota(jnp.int32, nh.shape, 0)
    tail = jnp.where(rid >= 8 - k, nh, sh[n - 8:])
    return jnp.concatenate([sh[:n - 8], tail], axis=0)


def _conv_pre(cur, prev8, w, b):
    acc = w[3:4, :] * cur + b
    for k in range(1, CONV_WIDTH):
        acc = acc + w[3 - k:4 - k, :] * _shift_down(cur, prev8, k)
    return acc


def _conv_fwd(u, w, b, name):
    t, c = u.shape
    per = CONV_ROWS // 8

    def body(u_ref, p_ref, w_ref, b_ref, o_ref):
        prev8 = jnp.where(pl.program_id(1) == 0, 0.0, p_ref[...])
        pre = _conv_pre(u_ref[...], prev8, w_ref[...], b_ref[...])
        o_ref[...] = pre * _sigmoid(pre)

    cur = pl.BlockSpec((CONV_ROWS, CONV_COLS), lambda j, i: (i, j))
    prev = pl.BlockSpec((8, CONV_COLS), lambda j, i: (jnp.maximum(i * per - 1, 0), j))
    wsp = pl.BlockSpec((CONV_WIDTH, CONV_COLS), lambda j, i: (0, j))
    bsp = pl.BlockSpec((1, CONV_COLS), lambda j, i: (0, j))
    return pl.pallas_call(body, grid=(c // CONV_COLS, t // CONV_ROWS), in_specs=[cur, prev, wsp, bsp], out_specs=cur,
                          out_shape=SDS((t, c), F32), name=name, compiler_params=_params(("parallel", "parallel")))(u, u, w, b)


def _conv_bwd(u, w, b, dout, du_prev, col0, name):
    t, c = u.shape
    per = CONV_ROWS // 8
    last = t // CONV_ROWS - 1

    def dsilu(pre, d):
        s = _sigmoid(pre)
        return d * (s * (1.0 + pre * (1.0 - s)))

    def body(u_ref, p_ref, n_ref, w_ref, b_ref, d_ref, dn_ref, *rest):
        du_ref, dw_ref = rest[-2:]
        i = pl.program_id(1)
        cur = u_ref[...]
        wv, bv = w_ref[...], b_ref[...]
        prev8 = jnp.where(i == 0, 0.0, p_ref[...])
        dpre = dsilu(_conv_pre(cur, prev8, wv, bv), d_ref[...])
        nxt, tail = n_ref[...], cur[CONV_ROWS - 8:]
        rid = lax.broadcasted_iota(jnp.int32, nxt.shape, 0)
        pre_n = wv[3:4, :] * nxt + bv
        for k in range(1, CONV_WIDTH):
            pre_n = pre_n + wv[3 - k:4 - k, :] * jnp.where(rid < k, pltpu.roll(tail, k, axis=0), pltpu.roll(nxt, k, axis=0))
        dpre_n = jnp.where(i == last, 0.0, dsilu(pre_n, dn_ref[...]))
        du = wv[3:4, :] * dpre
        for k in range(1, CONV_WIDTH):
            du = du + wv[3 - k:4 - k, :] * _shift_up(dpre, dpre_n, k)
        du_ref[...] = du.astype(BF16)
        rows = [jnp.sum(dpre * _shift_down(cur, prev8, 3 - k), axis=0, keepdims=True) for k in range(3)]
        rows.append(jnp.sum(dpre * cur, axis=0, keepdims=True))
        rows.append(jnp.sum(dpre, axis=0, keepdims=True))
        part = jnp.concatenate(rows + [jnp.zeros((3, cur.shape[1]), F32)], axis=0)

        @pl.when(i == 0)
        def _():
            dw_ref[...] = part

        @pl.when(i > 0)
        def _():
            dw_ref[...] += part

    width = dout.shape[1]
    tc = min(CONV_COLS, width)
    ob = col0 // tc

    def at(rows, here):
        off = ob if here else 0
        if rows == CONV_ROWS:
            return pl.BlockSpec((rows, tc), lambda j, i: (i, off + j))
        if rows == -8:
            return pl.BlockSpec((8, tc), lambda j, i: (jnp.maximum(i * per - 1, 0), off + j))
        if rows == 8:
            return pl.BlockSpec((8, tc), lambda j, i: (jnp.minimum((i + 1) * per, t // 8 - 1), off + j))
        return pl.BlockSpec((rows, tc), lambda j, i: (0, off + j))

    in_specs = [at(CONV_ROWS, True), at(-8, True), at(8, True), at(CONV_WIDTH, True), at(1, True),
                at(CONV_ROWS, False), at(8, False)]
    args = [u, u, u, w, b, dout, dout]
    aliases = {}
    if du_prev is not None:
        in_specs.append(pl.BlockSpec(memory_space=pl.ANY))
        args.append(du_prev)
        aliases = {len(args) - 1: 0}
    acc = pl.BlockSpec((8, tc), lambda j, i: (0, j))
    return pl.pallas_call(body, grid=(width // tc, t // CONV_ROWS), in_specs=in_specs,
                          out_specs=[at(CONV_ROWS, True), acc], out_shape=[SDS((t, c), BF16), SDS((8, width), F32)],
                          input_output_aliases=aliases, name=name,
                          compiler_params=_params(("parallel", "arbitrary")))(*args)


def _pieces(v, n):
    out, rest = [], v
    for _ in range(n):
        p = rest.astype(BF16)
        out.append(p)
        rest = rest - p.astype(F32)
    return out


def _head_expand(v):
    r = lax.broadcasted_iota(jnp.int32, (3 * 128, GROUP_W), 0) % 128
    c = lax.broadcasted_iota(jnp.int32, (3 * 128, GROUP_W), 1)
    return _dot(jnp.concatenate(_pieces(v, 3), axis=1), (c // SSM_HEAD_DIM == r).astype(BF16))


def _head_sum(vs):
    r = lax.broadcasted_iota(jnp.int32, (2 * GROUP_W, 128), 0) % GROUP_W
    c = lax.broadcasted_iota(jnp.int32, (2 * GROUP_W, 128), 1)
    stacked = jnp.concatenate([jnp.concatenate(_pieces(v, 2), axis=1) for v in vs], axis=0)
    out = _dot(stacked, (r // SSM_HEAD_DIM == c).astype(BF16))
    res, at = [], 0
    for v in vs:
        res.append(out[at:at + v.shape[0]])
        at += v.shape[0]
    return res


def _tri_left(tri, v):
    r = _dot(tri.astype(BF16), jnp.concatenate(_pieces(v, 3), axis=1))
    return r[:, 0:128] + r[:, 128:256] + r[:, 256:384]


def _dot(a, b):
    return jnp.dot(a, b, preferred_element_type=F32)


def _dot_nt(a, b):
    return lax.dot_general(a, b, (((1,), (1,)), ((), ())), preferred_element_type=F32)


def _dot_tn(a, b):
    return lax.dot_general(a, b, (((0,), (0,)), ((), ())), preferred_element_type=F32)


def _ssd_common(dtp, dtpt, bias, biasr, alog, alogr):
    li = lax.broadcasted_iota(jnp.int32, (CHUNK, CHUNK), 0)
    si = lax.broadcasted_iota(jnp.int32, (CHUNK, CHUNK), 1)
    lower, upper = (li >= si), (li <= si)
    dt = _softplus(dtp + bias)
    a_neg = -jnp.exp(alog)
    cs = _tri_left(lower, dt * a_neg)
    dtr = _softplus(dtpt + biasr)
    ar = jnp.concatenate([dtr * (-jnp.exp(alogr)), jnp.zeros((8, CHUNK), F32)], axis=0)
    r3 = lax.broadcasted_iota(jnp.int32, (3 * CHUNK, CHUNK), 0) % CHUNK
    c3 = lax.broadcasted_iota(jnp.int32, (3 * CHUNK, CHUNK), 1)
    csr = _dot(jnp.concatenate(_pieces(ar, 3), axis=1), (r3 <= c3).astype(BF16))[0:8]
    return lower, upper, dt, a_neg, cs, csr


def _head_masked_rows(v):
    hl = lax.broadcasted_iota(jnp.int32, v.shape, 1) // SSM_HEAD_DIM
    return jnp.concatenate([jnp.where(hl == j, v, jnp.zeros_like(v)) for j in range(8)], axis=0)


SSD_STEP_CHUNKS = 4


def _chunk_view(ref, kind, k):
    if kind == "rows":
        return ref.at[pl.ds(k * CHUNK, CHUNK), :]
    if kind == "lanes":
        return ref.at[:, pl.ds(k * CHUNK, CHUNK)]
    if kind == "lead":
        return ref.at[k]
    return ref


def _ssd_fwd(xbc, z, dtp_g, dtp_t, bias_g, bias_r, alog_g, alog_r, d_exp, norm_w, name):
    t = xbc.shape[0]
    nc = t // CHUNK

    def body(*refs):
        @pl.when(pl.program_id(1) == 0)
        def _():
            refs[-1][...] = jnp.zeros_like(refs[-1])

        for k in range(SSD_STEP_CHUNKS):
            chunk(*[_chunk_view(r, kind, k) for r, kind in zip(refs, kinds)])

    kinds = ["rows"] * 5 + ["lanes"] + [None] * 6 + ["rows", "rows", "lead", None]

    def chunk(xs_ref, b_ref, c_ref, z_ref, dtp_ref, dtpt_ref, bias_ref, biasr_ref, alog_ref, alogr_ref, dexp_ref, nw_ref,
              y_ref, yn_ref, st_ref, state):
        lower, _, dt, a_neg, cs, csr = _ssd_common(dtp_ref[...], dtpt_ref[...], bias_ref[...], biasr_ref[...],
                                                alog_ref[...], alogr_ref[...])
        cs_e = _head_expand(cs)
        dt_e = _head_expand(dt)
        xs = xs_ref[...]
        xdt = xs * dt_e
        bm = b_ref[...]
        cm = c_ref[...]
        bmb, cmb = bm.astype(BF16), cm.astype(BF16)
        cb = _dot_nt(cmb, bmb)
        ms = []
        for j in range(8):
            dlt = cs_e[:, SSM_HEAD_DIM * j:SSM_HEAD_DIM * j + 1] - csr[j:j + 1, :]
            ms.append((cb * jnp.exp(jnp.where(lower, dlt, -jnp.inf))).astype(BF16))
        y = _dot(jnp.concatenate(ms, axis=1), _head_masked_rows(xdt.astype(BF16)))
        st_in = state[...]
        st_ref[...] = st_in
        y = y + jnp.exp(cs_e) * _dot(cmb, st_in.astype(BF16))
        cs_last = cs_e[CHUNK - 1:CHUNK, :]
        xdtd = (xdt * jnp.exp(cs_last - cs_e)).astype(BF16)
        state[...] = jnp.exp(cs_last) * st_in + _dot(bm.T.astype(BF16), xdtd)
        y_ref[...] = y
        zz = z_ref[...]
        y2 = (y + dexp_ref[...] * xs) * (zz * _sigmoid(zz))
        r = lax.rsqrt(jnp.mean(y2 * y2, axis=-1, keepdims=True) + EPS)
        yn_ref[...] = (y2 * r * nw_ref[...]).astype(BF16)

    rows = SSD_STEP_CHUNKS * CHUNK
    gw = pl.BlockSpec((rows, GROUP_W), lambda g, c: (c, g))
    in_specs = [
        gw,
        pl.BlockSpec((rows, SSM_STATE), lambda g, c: (c, D_INNER // SSM_STATE + g)),
        pl.BlockSpec((rows, SSM_STATE), lambda g, c: (c, D_INNER // SSM_STATE + SSM_GROUPS + g)),
        gw,
        pl.BlockSpec((None, rows, 128), lambda g, c: (g, c, 0)),
        pl.BlockSpec((8, rows), lambda g, c: (g, c)),
        pl.BlockSpec((None, 1, 128), lambda g, c: (g, 0, 0)),
        pl.BlockSpec((8, 128), lambda g, c: (g, 0)),
        pl.BlockSpec((None, 1, 128), lambda g, c: (g, 0, 0)),
        pl.BlockSpec((8, 128), lambda g, c: (g, 0)),
        pl.BlockSpec((1, GROUP_W), lambda g, c: (0, g)),
        pl.BlockSpec((1, GROUP_W), lambda g, c: (0, g)),
    ]
    out_specs = [gw, gw, pl.BlockSpec((SSD_STEP_CHUNKS, SSM_STATE, GROUP_W), lambda g, c: (c, 0, g))]
    out_shape = [SDS((t, D_INNER), F32), SDS((t, D_INNER), BF16), SDS((nc, SSM_STATE, D_INNER), F32)]
    return pl.pallas_call(body, grid=(SSM_GROUPS, nc // SSD_STEP_CHUNKS), in_specs=in_specs, out_specs=out_specs,
                          out_shape=out_shape,
                          scratch_shapes=[pltpu.VMEM((SSM_STATE, GROUP_W), F32)], name=name,
                          compiler_params=_params(("parallel", "arbitrary")))(
        xbc, xbc, xbc, z, dtp_g, dtp_t, bias_g, bias_r, alog_g, alog_r, d_exp, norm_w)


def _ssd_bwd(dyn, y, xbc, z, states, dtp_g, dtp_t, bias_g, bias_r, alog_g, alog_r, d_exp, norm_w, name):
    t = xbc.shape[0]
    nc = t // CHUNK

    def body(*refs):
        step = pl.program_id(1)
        hsum_ref, csum_ref, dstate = refs[-3:]

        @pl.when(step == 0)
        def _():
            dstate[...] = jnp.zeros_like(dstate)

        parts = [chunk(*[_chunk_view(r, kind, k) for r, kind in zip(refs, kinds)])
                 for k in reversed(range(SSD_STEP_CHUNKS))]
        hpart, cpart = parts[0]
        for hp, cp in parts[1:]:
            hpart, cpart = hpart + hp, cpart + cp

        @pl.when(step == 0)
        def _():
            hsum_ref[...] = hpart
            csum_ref[...] = cpart

        @pl.when(step > 0)
        def _():
            hsum_ref[...] += hpart
            csum_ref[...] += cpart

    kinds = ["rows"] * 6 + ["lead", "rows", "lanes"] + [None] * 6 + ["rows"] * 5 + [None] * 3

    def chunk(dyn_ref, y_ref, xs_ref, b_ref, c_ref, z_ref, st_ref, dtp_ref, dtpt_ref, bias_ref, biasr_ref, alog_ref,
              alogr_ref, dexp_ref, nw_ref, dxs_ref, db_ref, dc_ref, dz_ref, ddt_ref, hsum_ref, csum_ref, dstate):
        dtp = dtp_ref[...]
        bias = bias_ref[...]
        lower, upper, dt, a_neg, cs, csr = _ssd_common(dtp, dtpt_ref[...], bias, biasr_ref[...], alog_ref[...],
                                                       alogr_ref[...])
        cs_e = _head_expand(cs)
        dt_e = _head_expand(dt)
        xs = xs_ref[...]
        xdt = xs * dt_e
        bm = b_ref[...]
        cm = c_ref[...]
        bmb, cmb = bm.astype(BF16), cm.astype(BF16)
        y = y_ref[...]
        dexp = dexp_ref[...]
        nw = nw_ref[...]

        zz = z_ref[...]
        sg = _sigmoid(zz)
        gate = zz * sg
        ytot = y + dexp * xs
        y2 = ytot * gate
        r = lax.rsqrt(jnp.mean(y2 * y2, axis=-1, keepdims=True) + EPS)
        dynv = dyn_ref[...]
        xh = y2 * r
        gn = dynv * nw
        dy2 = r * (gn - xh * jnp.mean(gn * xh, axis=-1, keepdims=True))
        dy = dy2 * gate
        dz_ref[...] = (dy2 * ytot * (sg * (1.0 + zz * (1.0 - sg)))).astype(BF16)
        csum_part = jnp.sum(dynv * xh, axis=0, keepdims=True)

        cb = _dot_nt(cmb, bmb)
        dyb = dy.astype(BF16)
        xdtb = xdt.astype(BF16)
        dym = _head_masked_rows(dyb)
        dm = _dot_nt(dym, xdtb)
        dmt = _dot_nt(_head_masked_rows(xdtb), dyb)
        lane = lax.broadcasted_iota(jnp.int32, (CHUNK, 128), 1)
        mts = []
        dcb = jnp.zeros((CHUNK, CHUNK), F32)
        dcs = jnp.zeros((CHUNK, 128), F32)
        for j in range(8):
            dlt = cs_e[:, SSM_HEAD_DIM * j:SSM_HEAD_DIM * j + 1] - csr[j:j + 1, :]
            lj = jnp.exp(jnp.where(lower, dlt, -jnp.inf))
            mj = cb * lj
            mjt = mj.T
            mts.append(mjt.astype(BF16))
            dmj = dm[CHUNK * j:CHUNK * (j + 1)]
            dcb = dcb + dmj * lj
            rows = jnp.sum(dmj * mj, axis=1, keepdims=True)
            cols = jnp.sum(dmt[CHUNK * j:CHUNK * (j + 1)] * mjt, axis=1, keepdims=True)
            dcs = dcs + jnp.where(lane == j, rows - cols, 0.0)
        dxdt = _dot(jnp.concatenate(mts, axis=1), dym)
        dst_out = dstate[...]
        dst_outb = dst_out.astype(BF16)
        st_in = st_ref[...]
        st_inb = st_in.astype(BF16)
        cs_last = cs_e[CHUNK - 1:CHUNK, :]
        decay = jnp.exp(cs_last - cs_e)
        e_last = jnp.exp(cs_last)
        gpart = decay * _dot(bmb, dst_outb)
        dxdt = dxdt + gpart
        dyw = (jnp.exp(cs_e) * dy).astype(BF16)
        dcbb = dcb.astype(BF16)
        dc_ref[...] = _dot_nt(dyw, st_inb) + _dot(dcbb, bmb)
        db_ref[...] = _dot_nt((xdt * decay).astype(BF16), dst_outb) + _dot(dcb.T.astype(BF16), cmb)
        dstate[...] = e_last * dst_out + _dot(cm.T.astype(BF16), dyw)
        y_off = jnp.exp(cs_e) * _dot(cmb, st_inb)
        xg = xdt * gpart
        vec = jnp.concatenate([jnp.sum(dy * xs, axis=0, keepdims=True),
                               jnp.sum(xg + dst_out * e_last * st_in, axis=0, keepdims=True),
                               jnp.zeros((14, GROUP_W), F32)], axis=0)
        s_cs, s_dt, s_vec = _head_sum([dy * y_off - xg, dxdt * xs, vec])
        d_skip = s_vec[0:1]
        ri = lax.broadcasted_iota(jnp.int32, (CHUNK, 128), 0)
        dcs = dcs + s_cs + jnp.where(ri == CHUNK - 1, s_vec[1:2], 0.0)
        da = _tri_left(upper, dcs)
        ddt = da * a_neg + s_dt
        dxs_ref[...] = dxdt * dt_e + dy * dexp
        ddtp = ddt * _sigmoid(dtp + bias)
        ddt_ref[...] = ddtp
        d_alog = jnp.sum(da * dt, axis=0, keepdims=True) * a_neg
        hpart = jnp.concatenate([jnp.sum(ddtp, axis=0, keepdims=True), d_alog, d_skip, jnp.zeros((5, 128), F32)], axis=0)
        cpart = jnp.concatenate([csum_part, jnp.zeros((7, GROUP_W), F32)], axis=0)
        return hpart, cpart

    steps = nc // SSD_STEP_CHUNKS
    rows = SSD_STEP_CHUNKS * CHUNK
    rc = lambda c: steps - 1 - c
    gw = pl.BlockSpec((rows, GROUP_W), lambda g, c: (rc(c), g))
    bsp = pl.BlockSpec((rows, SSM_STATE), lambda g, c: (rc(c), D_INNER // SSM_STATE + g))
    csp = pl.BlockSpec((rows, SSM_STATE), lambda g, c: (rc(c), D_INNER // SSM_STATE + SSM_GROUPS + g))
    in_specs = [
        gw, gw, gw, bsp, csp, gw,
        pl.BlockSpec((SSD_STEP_CHUNKS, SSM_STATE, GROUP_W), lambda g, c: (rc(c), 0, g)),
        pl.BlockSpec((None, rows, 128), lambda g, c: (g, rc(c), 0)),
        pl.BlockSpec((8, rows), lambda g, c: (g, rc(c))),
        pl.BlockSpec((None, 1, 128), lambda g, c: (g, 0, 0)),
        pl.BlockSpec((8, 128), lambda g, c: (g, 0)),
        pl.BlockSpec((None, 1, 128), lambda g, c: (g, 0, 0)),
        pl.BlockSpec((8, 128), lambda g, c: (g, 0)),
        pl.BlockSpec((1, GROUP_W), lambda g, c: (0, g)),
        pl.BlockSpec((1, GROUP_W), lambda g, c: (0, g)),
    ]
    nsp = pl.BlockSpec((rows, SSM_STATE), lambda g, c: (rc(c), g))
    out_specs = [gw, nsp, nsp, gw,
                 pl.BlockSpec((None, rows, 128), lambda g, c: (g, rc(c), 0)),
                 pl.BlockSpec((None, 8, 128), lambda g, c: (g, 0, 0)),
                 pl.BlockSpec((8, GROUP_W), lambda g, c: (0, g))]
    gn = SSM_GROUPS * SSM_STATE
    out_shape = [SDS((t, D_INNER), F32), SDS((t, gn), F32), SDS((t, gn), F32), SDS((t, D_INNER), BF16),
                 SDS((SSM_GROUPS, t, 128), F32), SDS((SSM_GROUPS, 8, 128), F32), SDS((8, D_INNER), F32)]
    return pl.pallas_call(body, grid=(SSM_GROUPS, steps), in_specs=in_specs, out_specs=out_specs, out_shape=out_shape,
                          scratch_shapes=[pltpu.VMEM((SSM_STATE, GROUP_W), F32)], name=name,
                          compiler_params=_params(("parallel", "arbitrary")))(
        dyn, y, xbc, xbc, xbc, z, states, dtp_g, dtp_t, bias_g, bias_r, alog_g, alog_r, d_exp, norm_w)


BLK = 128
HEAD_PAIRS = ATTN_W // 128
ATTN_SCALE = 0.125
SPAN_BLOCKS = {1: 8, 4: 2, 16: 1}


def _slope_table(group):
    n = len(ATTN_GROUPS) * 16
    tbl = [[2.0 ** (-8.0 * (16 * group + 2 * p + s + 1) / n) if s < 2 else 0.0 for s in range(128)] for p in range(HEAD_PAIRS)]
    return jnp.asarray(tbl, F32)


def _lane_lo(rows):
    return lax.broadcasted_iota(jnp.int32, (rows, 128), 1) < 64


def _rows(start, dil):
    return pl.ds(start, BLK, stride=dil) if dil > 1 else pl.ds(start, BLK)


def _stack_heads(x):
    lo = _lane_lo(BLK)
    return jnp.concatenate([jnp.where(lo, x, jnp.zeros_like(x)), jnp.where(lo, jnp.zeros_like(x), x)], axis=0)


def _pair_cols(x):
    return jnp.concatenate([x[:, 0:1], x[:, 64:65]], axis=0)


def _attn_bias(sl, dil, first_span, last_span):
    qi = lax.broadcasted_iota(jnp.int32, (BLK, 2 * BLK), 0)
    kj = lax.broadcasted_iota(jnp.int32, (BLK, 2 * BLK), 1)
    dist = qi + BLK - kj
    valid = (dist >= 0) & (dist <= BLK)
    distf = dist.astype(F32) * float(dil)

    def stacked(ok, d):
        return jnp.concatenate([jnp.where(ok, -sl[:, h:h + 1] * d, -jnp.inf) for h in range(2)], axis=0)

    first = stacked(valid & ((kj >= BLK) | jnp.logical_not(first_span)), distf)
    after = None
    if last_span is not None:
        after = stacked((kj[:, :BLK] >= qi[:, :BLK]) & jnp.logical_not(last_span), distf[:, :BLK])
    return first, stacked(valid, distf), after


def _attn_specs(group, t):
    _, dil = ATTN_GROUPS[group]
    nblk = SPAN_BLOCKS[dil]
    span, edge = BLK * dil * nblk, BLK * dil
    per = span // edge

    def lane_block(which):
        if which is None:
            return lambda p: p
        return lambda p: 3 * (group * HEAD_PAIRS + p) + which

    def cur(which):
        col = lane_block(which)
        return pl.BlockSpec((span, 128), lambda s, p: (s, col(p)))

    def before(which):
        col = lane_block(which)
        return pl.BlockSpec((edge, 128), lambda s, p: (jnp.maximum(s * per - 1, 0), col(p)))

    def after(which):
        col = lane_block(which)
        return pl.BlockSpec((edge, 128), lambda s, p: (jnp.minimum((s + 1) * per, t // edge - 1), col(p)))

    slopes = pl.BlockSpec((HEAD_PAIRS, 128), lambda s, p: (0, 0))
    return dil, nblk, span, cur, before, after, slopes


def _pair_major(w, inverse=False):
    k = w.shape[0]
    g = len(ATTN_GROUPS)
    if inverse:
        return jnp.transpose(w.reshape(k, g, HEAD_PAIRS, 3, 128), (0, 3, 1, 2, 4)).reshape(k, QKV_DIM)
    return jnp.transpose(w.reshape(k, 3, g, HEAD_PAIRS, 128), (0, 2, 3, 1, 4)).reshape(k, QKV_DIM)


def _attn_fwd(qkv, group, name):
    t = qkv.shape[0]
    dil, nblk, span, cur, before, after, slopes = _attn_specs(group, t)

    def body(q_ref, k_ref, v_ref, kp_ref, vp_ref, sl_ref, o_ref, l_ref):
        first_span = pl.program_id(0) == 0
        bias_first, bias_mid, _ = _attn_bias(sl_ref[pl.ds(pl.program_id(1), 1), :], dil, first_span, None)
        lo_q, lo_k = _lane_lo(BLK), _lane_lo(2 * BLK)
        for r in range(dil):
            kp, vp = kp_ref[_rows(r, dil), :].astype(BF16), vp_ref[_rows(r, dil), :].astype(BF16)
            for b in range(nblk):
                rows = _rows(b * BLK * dil + r, dil)
                kc, vc = k_ref[rows, :].astype(BF16), v_ref[rows, :].astype(BF16)
                k2 = jnp.concatenate([kp, kc], axis=0)
                v2 = jnp.concatenate([vp, vc], axis=0)
                kp, vp = kc, vc
                s = _dot_nt(_stack_heads((q_ref[rows, :] * ATTN_SCALE).astype(BF16)), k2) + (bias_first if b == 0 else bias_mid)
                mx = jnp.max(s, axis=-1, keepdims=True)
                p = jnp.exp(s - mx)
                den = jnp.sum(p, axis=-1, keepdims=True)
                pb = p.astype(BF16)
                o_ref[rows, :] = _dot(pb[:BLK], jnp.where(lo_k, v2, jnp.zeros_like(v2))) / den[:BLK] + \
                    _dot(pb[BLK:], jnp.where(lo_k, jnp.zeros_like(v2), v2)) / den[BLK:]
                lse = mx + jnp.log(den)
                l_ref[rows, :] = jnp.where(lo_q, lse[:BLK], lse[BLK:])

    in_specs = [cur(0), cur(1), cur(2), before(1), before(2), slopes]
    return pl.pallas_call(body, grid=(t // span, HEAD_PAIRS), in_specs=in_specs, out_specs=[cur(None), cur(None)],
                          out_shape=[SDS((t, ATTN_W), F32), SDS((t, ATTN_W), F32)], name=name,
                          compiler_params=_params(("parallel", "parallel")))(qkv, qkv, qkv, qkv, qkv, _slope_table(group))


def _attn_bwd(qkv, do, lse, dvec, dqkv, group, name):
    t = qkv.shape[0]
    dil, nblk, span, cur, before, after, slopes = _attn_specs(group, t)
    nspan = t // span

    def body(q_ref, k_ref, v_ref, kp_ref, vp_ref, qn_ref, do_ref, l_ref, d_ref, don_ref, ln_ref, dn_ref, sl_ref, *rest):
        out_ref, dq_s, dk_s, dv_s = rest[-4:]
        span_id, pair_id = pl.program_id(0), pl.program_id(1)
        bias_first, bias_mid, bias_next = _attn_bias(sl_ref[pl.ds(pair_id, 1), :], dil, span_id == 0, span_id == nspan - 1)
        lo_k = _lane_lo(2 * BLK)
        for r in range(dil):
            kp, vp = kp_ref[_rows(r, dil), :].astype(BF16), vp_ref[_rows(r, dil), :].astype(BF16)
            held = None
            for b in range(nblk + 1):
                last = b == nblk
                rows = _rows(r if last else b * BLK * dil + r, dil)
                qs, dos, ls, dvs = (qn_ref, don_ref, ln_ref, dn_ref) if last else (q_ref, do_ref, l_ref, d_ref)
                qst = _stack_heads((qs[rows, :] * ATTN_SCALE).astype(BF16))
                dost = _stack_heads(dos[rows, :].astype(BF16))
                if last:
                    k2, v2, bias = kp, vp, bias_next
                else:
                    kc, vc = k_ref[rows, :].astype(BF16), v_ref[rows, :].astype(BF16)
                    k2, v2 = jnp.concatenate([kp, kc], axis=0), jnp.concatenate([vp, vc], axis=0)
                    kp, vp = kc, vc
                    bias = bias_first if b == 0 else bias_mid
                p = jnp.exp(_dot_nt(qst, k2) + bias - _pair_cols(ls[rows, :]))
                ds = (p * (_dot_nt(dost, v2) - _pair_cols(dvs[rows, :]))).astype(BF16)
                dk2 = _dot_tn(ds, qst)
                dv2 = _dot_tn(p.astype(BF16), dost)
                if held is not None:
                    dk_s[held[0], :] = held[1] + dk2[:BLK]
                    dv_s[held[0], :] = held[2] + dv2[:BLK]
                if not last:
                    k_heads = jnp.concatenate([jnp.where(lo_k, k2, jnp.zeros_like(k2)),
                                               jnp.where(lo_k, jnp.zeros_like(k2), k2)], axis=0)
                    dq_s[rows, :] = _dot(jnp.concatenate([ds[:BLK], ds[BLK:]], axis=1), k_heads) * ATTN_SCALE
                    held = (rows, dk2[BLK:], dv2[BLK:])
        out_ref[:, 0:128] = dq_s[...]
        out_ref[:, 128:256] = dk_s[...]
        out_ref[:, 256:384] = dv_s[...]

    out_spec = pl.BlockSpec((span, 3 * 128), lambda s, p: (s, group * HEAD_PAIRS + p))
    in_specs = [cur(0), cur(1), cur(2), before(1), before(2), after(0), cur(None), cur(None), cur(None),
                after(None), after(None), after(None), slopes]
    args = [qkv, qkv, qkv, qkv, qkv, qkv, do, lse, dvec, do, lse, dvec, _slope_table(group)]
    aliases = {}
    if dqkv is not None:
        in_specs.append(pl.BlockSpec(memory_space=pl.ANY))
        args.append(dqkv)
        aliases = {len(args) - 1: 0}
    return pl.pallas_call(body, grid=(nspan, HEAD_PAIRS), in_specs=in_specs, out_specs=out_spec,
                          out_shape=SDS((t, QKV_DIM), F32), input_output_aliases=aliases,
                          scratch_shapes=[pltpu.VMEM((span, 128), F32)] * 3, name=name,
                          compiler_params=_params(("parallel", "parallel")))(*args)


def _combine_weights(l0, l1, l2):
    mx = jnp.maximum(jnp.maximum(l0, l1), l2)
    e0, e1, e2 = jnp.exp(l0 - mx), jnp.exp(l1 - mx), jnp.exp(l2 - mx)
    den = e0 + e1 + e2
    return e0 / den, e1 / den, e2 / den


def _combine_fwd(os_, ls_, name):
    t = os_[0].shape[0]

    def body(o0, o1, o2, l0, l1, l2, out):
        w0, w1, w2 = _combine_weights(l0[...], l1[...], l2[...])
        out[...] = (w0 * o0[...] + w1 * o1[...] + w2 * o2[...]).astype(BF16)

    row = pl.BlockSpec((ROW_TILE, ATTN_W), lambda i: (i, 0))
    return pl.pallas_call(body, grid=(t // ROW_TILE,), in_specs=[row] * 6, out_specs=row, out_shape=SDS((t, ATTN_W), BF16),
                          name=name, compiler_params=_params(("parallel",)))(*os_, *ls_)


def _combine_bwd(do, os_, ls_, name):
    t = do.shape[0]

    def body(do_ref, o0, o1, o2, l0, l1, l2, g0, g1, g2, d0, d1, d2):
        w0, w1, w2 = _combine_weights(l0[...], l1[...], l2[...])
        dov = do_ref[...]
        prod = dov * (w0 * o0[...] + w1 * o1[...] + w2 * o2[...])
        r = (lax.broadcasted_iota(jnp.int32, (3 * 128, 128), 0) % 128) // 64
        c = lax.broadcasted_iota(jnp.int32, (3 * 128, 128), 1) // 64
        same = (r == c).astype(BF16)
        tbar = jnp.concatenate([_dot(jnp.concatenate(_pieces(prod[:, 128 * k:128 * k + 128], 3), axis=1), same)
                                for k in range(HEAD_PAIRS)], axis=1)
        for w, g, d in ((w0, g0, d0), (w1, g1, d1), (w2, g2, d2)):
            g[...] = w * dov
            d[...] = w * tbar

    row = pl.BlockSpec((ROW_TILE, ATTN_W), lambda i: (i, 0))
    return pl.pallas_call(body, grid=(t // ROW_TILE,), in_specs=[row] * 7, out_specs=[row] * 6,
                          out_shape=[SDS((t, ATTN_W), F32)] * 6, name=name,
                          compiler_params=_params(("parallel",)))(do, *os_, *ls_)


def _peer(k):
    x, y, c = lax.axis_index("x"), lax.axis_index("y"), lax.axis_index("c")
    px = 1 - x if k & 4 else x
    py = 1 - y if k & 2 else y
    pc = 1 - c if k & 1 else c
    return (px, py, pc), 4 * px + 2 * py + pc


def _my_index():
    return 4 * lax.axis_index("x") + 2 * lax.axis_index("y") + lax.axis_index("c")


def _exchange_sems(n):
    return [pltpu.SemaphoreType.DMA((n * (NDEV - 1),)), pltpu.SemaphoreType.DMA((n * (NDEV - 1),)),
            pltpu.SemaphoreType.DMA((n,))]


def _scatter_copies(spans, ins, outs, send, recv, local, arrivals):
    me = _my_index()
    own, sent, arriving = [], [], []
    for i in range(len(ins)):
        def part(ref, slot, i=i):
            return ref.at[slot] if spans[i] is None else ref.at[slot, pl.ds(spans[i][0], spans[i][1])]
        own.append(pltpu.make_async_copy(part(ins[i], me), part(outs[i], me), local.at[i]))
        for k in range(1, NDEV):
            peer, pidx = _peer(k)
            s = i * (NDEV - 1) + k - 1
            for dst, into in ((part(outs[i], me), sent), (part(outs[i], pidx), arriving)):
                if into is sent or arrivals:
                    into.append(pltpu.make_async_remote_copy(src_ref=part(ins[i], pidx), dst_ref=dst, send_sem=send.at[s],
                                                             recv_sem=recv.at[s], device_id=peer, device_id_type=MESH))
    return own, sent, arriving


def _exchange_start(spans, ins, outs, send, recv, local):
    own, sent, _ = _scatter_copies(spans, ins, outs, send, recv, local, arrivals=False)
    for cp in own + sent:
        cp.start()


def _exchange_wait(spans, ins, outs, send, recv, local):
    own, sent, arriving = _scatter_copies(spans, ins, outs, send, recv, local, arrivals=True)
    for cp in sent:
        cp.wait_send()
    for cp in arriving:
        cp.wait_recv()
    for cp in own:
        cp.wait()


def _two_level(phase, ins, outs, send, recv, local):
    n = len(ins)
    x, y, c = lax.axis_index("x"), lax.axis_index("y"), lax.axis_index("c")
    here, sibling = (x, y, c), (x, y, 1 - c)
    chips = [(1 - x, y), (x, 1 - y), (1 - x, 1 - y)]

    def slot(px, py, pc):
        return 4 * px + 2 * py + pc

    def copy(i, k, block, to, src=None):
        return pltpu.make_async_remote_copy(src_ref=outs[i].at[block] if src is None else src, dst_ref=outs[i].at[block],
                                            send_sem=send.at[7 * i + k], recv_sem=recv.at[7 * i + k],
                                            device_id=to, device_id_type=MESH)

    me = slot(x, y, c)
    if phase == "pass":
        for i in range(n):
            for j, chip in enumerate(chips):
                copy(i, 1 + j, slot(*chip, c), here).wait_recv()
                copy(i, 4 + j, slot(*chip, c), sibling).start()
        return
    own = [pltpu.make_async_copy(ins[i], outs[i].at[me], local.at[i]) for i in range(n)]
    first = [copy(i, 0, me, sibling, src=ins[i]) for i in range(n)]
    first += [copy(i, 1 + j, me, (*chip, c), src=ins[i]) for i in range(n) for j, chip in enumerate(chips)]
    if phase == "start":
        for cp in own + first:
            cp.start()
        return
    for i in range(n):
        copy(i, 0, slot(x, y, 1 - c), here).wait_recv()
        for j, chip in enumerate(chips):
            copy(i, 4 + j, slot(*chip, 1 - c), here).wait_recv()
    passed = [copy(i, 4 + j, slot(*chip, c), sibling) for i in range(n) for j, chip in enumerate(chips)]
    for cp in first + passed:
        cp.wait_send()
    for cp in own:
        cp.wait()


def _gather_two_level(tensors, name):
    n = len(tensors)

    def body(*refs):
        for phase in ("start", "pass", "wait"):
            _two_level(phase, refs[:n], refs[n:2 * n], *refs[2 * n:])

    hbm = pl.BlockSpec(memory_space=pl.ANY)
    return pl.pallas_call(body, in_specs=[hbm] * n, out_specs=[hbm] * n,
                          out_shape=[SDS((NDEV,) + t.shape, t.dtype) for t in tensors],
                          scratch_shapes=_exchange_sems(n), name=name)(*tensors)


def _all_reduce_small(v, name):
    rows = v.shape[0]

    def body(v_ref, out_ref, land, send, recv):
        me = _my_index()
        land[me] = v_ref[...]
        remote = []
        for k in range(1, NDEV):
            peer, _ = _peer(k)
            cp = pltpu.make_async_remote_copy(src_ref=v_ref, dst_ref=land.at[me], send_sem=send.at[k - 1],
                                              recv_sem=recv.at[k - 1], device_id=peer, device_id_type=MESH)
            cp.start()
            remote.append(cp)
        for cp in remote:
            cp.wait_send()
        for k in range(1, NDEV):
            peer, pidx = _peer(k)
            pltpu.make_async_remote_copy(src_ref=v_ref, dst_ref=land.at[pidx], send_sem=send.at[k - 1],
                                         recv_sem=recv.at[k - 1], device_id=peer, device_id_type=MESH).wait_recv()
        total = land[0]
        for d in range(1, NDEV):
            total = total + land[d]
        out_ref[...] = total

    vm = pl.BlockSpec(memory_space=pltpu.VMEM)
    return pl.pallas_call(
        body, in_specs=[vm], out_specs=vm, out_shape=SDS((rows, 128), F32),
        scratch_shapes=[pltpu.VMEM((NDEV, rows, 128), F32), pltpu.SemaphoreType.DMA((NDEV - 1,)),
                        pltpu.SemaphoreType.DMA((NDEV - 1,))],
        name=name)(v)


def _adamw_math(w, g, m, v):
    m = ADAM_B1 * m + (1.0 - ADAM_B1) * g
    v = ADAM_B2 * v + (1.0 - ADAM_B2) * (g * g)
    m_hat = m / (1.0 - ADAM_B1 ** ADAM_STEP)
    v_hat = v / (1.0 - ADAM_B2 ** ADAM_STEP)
    delta = -ADAM_LR * (m_hat / (jnp.sqrt(v_hat) + ADAM_EPS) + ADAM_WD * w)
    return delta, m, v


def _row_tile(rows, cols):
    tr = rows
    while tr * cols * 4 > (1 << 20) and tr % 16 == 0:
        tr //= 2
    return tr


def _adamw(g, w, m, v, name):
    rows, cols = w.shape
    tr = _row_tile(rows, cols)

    def body(g_ref, w_ref, m_ref, v_ref, d_out, m_out, v_out):
        d, mn, vn = _adamw_math(w_ref[...], g_ref[...], m_ref[...], v_ref[...])
        d_out[...] = d
        m_out[...] = mn
        v_out[...] = vn

    sp = pl.BlockSpec((tr, cols), lambda i: (i, 0))
    return pl.pallas_call(body, grid=(rows // tr,), in_specs=[sp] * 4, out_specs=[sp] * 3,
                          out_shape=[SDS((rows, cols), F32)] * 3, name=name, compiler_params=_params(("parallel",)))(g, w, m, v)


def _reduce_adamw(parts, w, m, v, name):
    rows, cols = w.shape
    tr = _row_tile(math.gcd(*[n for _, _, n in parts]), cols)
    tiles = [n // tr for _, _, n in parts]
    starts = [r0 // tr for _, r0, _ in parts]
    assert all(r0 % tr == 0 for _, r0, _ in parts) and sum(tiles) == rows // tr

    def body(*refs):
        p_refs = refs[:len(parts)]
        w_ref, m_ref, v_ref, g_out, d_out, m_out, v_out = refs[len(parts):]
        i = pl.program_id(0)
        for h, p_ref in enumerate(p_refs):
            @pl.when((i >= starts[h]) & (i < starts[h] + tiles[h]))
            def _(p_ref=p_ref):
                g = p_ref[0].astype(F32)
                for d in range(1, NDEV):
                    g = g + p_ref[d].astype(F32)
                g_out[...] = g
                dl, mn, vn = _adamw_math(w_ref[...], g, m_ref[...], v_ref[...])
                d_out[...] = dl
                m_out[...] = mn
                v_out[...] = vn

    sp = pl.BlockSpec((tr, cols), lambda i: (i, 0))
    psp = [pl.BlockSpec((NDEV, tr, cols), lambda i, h=h: (0, jnp.clip(i, starts[h], starts[h] + tiles[h] - 1), 0))
           for h in range(len(parts))]
    return pl.pallas_call(body, grid=(rows // tr,), in_specs=psp + [sp, sp, sp], out_specs=[sp] * 4,
                          out_shape=[SDS((rows, cols), F32)] * 4, name=name,
                          compiler_params=_params(("parallel",)))(*[p for p, _, _ in parts], w, m, v)


def _pack(items):
    rows = []
    for a in items:
        a = a.reshape(-1).astype(F32)
        pad = (-a.shape[0]) % 128
        rows.append(jnp.pad(a, (0, pad)).reshape(-1, 128))
    out = jnp.concatenate(rows, axis=0)
    return jnp.pad(out, ((0, (-out.shape[0]) % 8), (0, 0)))


def _unpack(packed, shapes):
    out, r = [], 0
    for shp in shapes:
        n = math.prod(shp)
        nr = -(-n // 128)
        out.append(packed[r:r + nr].reshape(-1)[:n].reshape(shp))
        r += nr
    return out


def _ident(t):
    return (t,)


def _add(t, res):
    return (t + res,)


def kernel(x, norm_mix, norm_mlp, ssm_w_in, ssm_conv_w, ssm_conv_b, ssm_dt_bias, ssm_a_log, ssm_d, ssm_norm_w, ssm_w_out, attn_w_qkv, attn_w_o, mlp_w1, mlp_w2, final_norm, loss_target, m_norm_mix, m_norm_mlp, m_ssm_w_in, m_ssm_conv_w, m_ssm_conv_b, m_ssm_dt_bias, m_ssm_a_log, m_ssm_d, m_ssm_norm_w, m_ssm_w_out, m_attn_w_qkv, m_attn_w_o, m_mlp_w1, m_mlp_w2, m_final_norm, v_norm_mix, v_norm_mlp, v_ssm_w_in, v_ssm_conv_w, v_ssm_conv_b, v_ssm_dt_bias, v_ssm_a_log, v_ssm_d, v_ssm_norm_w, v_ssm_w_out, v_attn_w_qkv, v_attn_w_o, v_mlp_w1, v_mlp_w2, v_final_norm):
    t = x.shape[1]
    x0 = x.reshape(t, D_MODEL)
    tgt = loss_target.reshape(t, D_MODEL)
    me = _my_index()
    in_dim = D_INNER + CONV_DIM + SSM_HEADS
    in_shard = in_dim // NDEV
    zx_dim = D_INNER + CONV_DIM

    s_out, s_qkv, s_o = ssm_w_out[0].astype(BF16), attn_w_qkv[0].astype(BF16), attn_w_o[0].astype(BF16)
    s_w1, s_w2 = mlp_w1.astype(BF16), mlp_w2.astype(BF16)
    g_in, g_cw = _gather_two_level([ssm_w_in[0].astype(BF16), ssm_conv_w[0]], "gather_in_proj")
    w_in = jnp.transpose(g_in, (1, 0, 2)).reshape(D_MODEL, in_dim)
    w_z, w_x = w_in[:, :D_INNER], w_in[:, D_INNER:zx_dim]
    w_dt = jnp.pad(w_in[:, zx_dim:], ((0, 0), (0, 128 - SSM_HEADS)))
    conv_w = jnp.transpose(g_cw, (1, 0, 2)).reshape(CONV_WIDTH, CONV_DIM)
    g_w1, g_w2 = [None, None], [None, None]

    def lanes(p):
        return jnp.pad(p.reshape(SSM_GROUPS, 1, 8), ((0, 0), (0, 0), (0, 120)))

    def rows(p):
        return jnp.broadcast_to(p.reshape(SSM_HEADS, 1), (SSM_HEADS, 128))

    bias_g, bias_r = lanes(ssm_dt_bias[0]), rows(ssm_dt_bias[0])
    alog_g, alog_r = lanes(ssm_a_log[0]), rows(ssm_a_log[0])
    d_exp = jnp.repeat(ssm_d[0], SSM_HEAD_DIM).reshape(1, D_INNER)
    norm_w = ssm_norm_w

    def relu2(tot):
        r = jnp.maximum(tot, 0.0)
        return r, r * r

    def mlp_fwd(xin, layer, tag, down_carry):
        h = _rmsnorm_fwd(xin, norm_mlp[layer:layer + 1], f"norm_mlp{tag}")
        r, a, w2 = _mm_fwd(h, g_w1[layer], "cols", D_MODEL, D_FF, epi=relu2, outs=(BF16, BF16),
                           carry=("gather", [s_w2[layer]]), name=f"mlp_up{tag}")
        g_w2[layer] = w2.reshape(D_FF, D_MODEL)
        xout, *got = _mm_fwd(a, g_w2[layer], "plain", D_FF, D_MODEL, epi=_add, outs=(F32,), extras=(xin,),
                             carry=("gather", down_carry) if down_carry else None, name=f"mlp_down{tag}")
        return h, r, a, xout, got

    h0 = _rmsnorm_fwd(x0, norm_mix[0:1], "norm_mix0")
    z, g_out = _mm_fwd(h0, w_z, "plain", D_MODEL, D_INNER, epi=_ident, outs=(F32,), carry=("gather", [s_out]), name="ssm_in_z")
    xpre, g_w1[0] = _mm_fwd(h0, w_x, "plain", D_MODEL, CONV_DIM, epi=_ident, outs=(F32,), carry=("gather", [s_w1[0]]),
                            name="ssm_in_x")
    dtp, = _mm_fwd(h0, w_dt, "plain", D_MODEL, 128, epi=_ident, outs=(F32,), name="ssm_in_dt")
    xbc = _conv_fwd(xpre, conv_w, ssm_conv_b, "conv_fwd")
    dtp64 = dtp[:, :SSM_HEADS]
    dtp_g = jnp.pad(jnp.transpose(dtp64.reshape(t, SSM_GROUPS, 8), (1, 0, 2)), ((0, 0), (0, 0), (0, 120)))
    dtp_t = jnp.transpose(dtp64)
    y_ssd, yn, states = _ssd_fwd(xbc, z, dtp_g, dtp_t, bias_g, bias_r, alog_g, alog_r, d_exp, norm_w, "ssd_fwd")
    g_out = g_out.reshape(D_INNER, D_MODEL)
    x1, g_o = _mm_fwd(yn, g_out, "plain", D_INNER, D_MODEL, epi=_add, outs=(F32,), extras=(x0,), carry=("gather", [s_o]),
                      name="ssm_out")
    h1, r1, a1, x2, (g_qkv,) = mlp_fwd(x1, 0, "0", [s_qkv])

    h2 = _rmsnorm_fwd(x2, norm_mix[1:2], "norm_mix1")
    w_qkv = _pair_major(jnp.transpose(g_qkv, (1, 0, 2)).reshape(D_MODEL, QKV_DIM))
    qkv, g_w1[1] = _mm_fwd(h2, w_qkv, "plain", D_MODEL, QKV_DIM, epi=_ident, outs=(F32,),
                           carry=("gather", [s_w1[1]]), name="attn_qkv")
    att = [_attn_fwd(qkv, g, f"attn_fwd{g}") for g in range(3)]
    os_, ls_ = [a[0] for a in att], [a[1] for a in att]
    o_mix = _combine_fwd(os_, ls_, "attn_combine")
    x3, = _mm_fwd(o_mix, g_o, "cols", ATTN_W, D_MODEL, epi=_add, outs=(F32,), extras=(x2,), tn=D_MODEL // NDEV, name="attn_out")
    h3, r3, a3, x4, _ = mlp_fwd(x3, 1, "1", None)

    dx4, dx4b, loss_acc, d_final = _loss_head(x4, tgt, final_norm.reshape(1, D_MODEL), "loss_head")

    def mlp_bwd(xin, h, r, a, dxo, dxob, layer, tag):
        du, = _mm_dx(dxob, g_w2[layer], "plain", D_FF, D_MODEL, epi=lambda tot, rr: (tot * (2.0 * rr.astype(F32)),),
                     outs=(BF16,), extras=(r,), name=f"mlp_down_dx{tag}")
        dw2, = _mm_dw(a, dxob, "plain", name=f"mlp_down_dw{tag}")
        dw2 = dw2.reshape(NDEV, D_FF // NDEV, D_MODEL)
        dw1, p_dw2 = _mm_dw(h, du, "cols", carry=("scatter", [dw2], [W2_SPANS[0]]), name=f"mlp_up_dw{tag}")
        dh, p_dw1 = _mm_dx(du, g_w1[layer], "cols", D_MODEL, D_FF, epi=_ident, outs=(F32,),
                           carry=("scatter", [dw1], [W1_SPANS[0]]), name=f"mlp_up_dx{tag}")
        dxi, dxib, dg = _rmsnorm_bwd(xin, norm_mlp[layer:layer + 1], dh, dxo, f"norm_mlp_bwd{tag}")
        return dxi, dxib, dg, (dw1, dw2), (p_dw1, p_dw2)

    W1_SPANS = [(0, 7 * D_MODEL // 8), (7 * D_MODEL // 8, D_MODEL // 8)]
    W2_SPANS = [(0, 7 * D_FF // NDEV // 8), (7 * D_FF // NDEV // 8, D_FF // NDEV // 8)]

    def mlp_parts(first, rest):
        return [[(first[0],) + W1_SPANS[0], (rest[0],) + W1_SPANS[1]], [(first[1],) + W2_SPANS[0], (rest[1],) + W2_SPANS[1]]]

    dx3, dx3b, dg_mlp1, left1, got1 = mlp_bwd(x3, h3, r3, a3, dx4, dx4b, 1, "1")

    dw_o, *rest1 = _mm_dw(o_mix, dx3b, "cols", tn=D_MODEL // NDEV, carry=("scatter", list(left1), [W1_SPANS[1], W2_SPANS[1]]),
                          name="attn_out_dw")
    p_w1_1, p_w2_1 = mlp_parts(got1, rest1)
    do, p_o = _mm_dx(dx3b, g_o, "cols", ATTN_W, D_MODEL, epi=_ident, outs=(F32,), tk=D_MODEL // NDEV,
                     carry=("scatter", [dw_o]), name="attn_out_dx")
    cb = _combine_bwd(do, os_, ls_, "attn_combine_bwd")
    dos, dvecs = cb[:3], cb[3:]
    dqkv = None
    for g in range(3):
        dqkv = _attn_bwd(qkv, dos[g], ls_[g], dvecs[g], dqkv, g, f"attn_bwd{g}")
    dw_qkv, = _mm_dw(h2, dqkv, "plain", tk=1024, name="attn_qkv_dw")
    dw_qkv = jnp.transpose(_pair_major(dw_qkv, inverse=True).reshape(D_MODEL, NDEV, QKV_DIM // NDEV), (1, 0, 2))
    dh2, p_qkv = _mm_dx(dqkv, w_qkv, "plain", D_MODEL, QKV_DIM, epi=_ident, outs=(F32,), tk=1536,
                        carry=("scatter", [dw_qkv]), name="attn_qkv_dx")
    dx2, dx2b, dg_mix1 = _rmsnorm_bwd(x2, norm_mix[1:2], dh2, dx3, "norm_mix_bwd1")

    dx1, dx1b, dg_mlp0, left0, got0 = mlp_bwd(x1, h1, r1, a1, dx2, dx2b, 0, "0")

    dw_out, *rest0 = _mm_dw(yn, dx1b, "plain", carry=("scatter", list(left0), [W1_SPANS[1], W2_SPANS[1]]), name="ssm_out_dw")
    p_w1_0, p_w2_0 = mlp_parts(got0, rest0)
    dw_out = dw_out.reshape(NDEV, D_INNER // NDEV, D_MODEL)
    dyn, p_out = _mm_dx(dx1b, g_out, "plain", D_INNER, D_MODEL, epi=_ident, outs=(F32,), carry=("scatter", [dw_out]),
                        name="ssm_out_dx")
    dxs, d_b, d_c, dz, ddtp_g, hsums, csums = _ssd_bwd(dyn, y_ssd, xbc, z, states, dtp_g, dtp_t, bias_g, bias_r,
                                                       alog_g, alog_r, d_exp, norm_w, "ssd_bwd")
    du, sums = None, []
    for part, col0, tag in ((dxs, 0, "x"), (d_b, D_INNER, "b"), (d_c, D_INNER + SSM_GROUPS * SSM_STATE, "c")):
        du, s = _conv_bwd(xpre, conv_w, ssm_conv_b, part, du, col0, f"conv_bwd_{tag}")
        sums.append(s)
    conv_sums = jnp.concatenate(sums, axis=1)
    ddtp = jnp.transpose(ddtp_g[:, :, :8], (1, 0, 2)).reshape(t, SSM_HEADS)
    ddtp = jnp.pad(ddtp, ((0, 0), (0, 128 - SSM_HEADS))).astype(BF16)
    dw_z, = _mm_dw(h0, dz, "plain", name="ssm_in_z_dw")
    dw_x, = _mm_dw(h0, du, "plain", name="ssm_in_x_dw")
    dw_dt, = _mm_dw(h0, ddtp, "plain", name="ssm_in_dt_dw")
    half = D_MODEL // 2
    dw_in = jnp.concatenate([dw_z, dw_x, dw_dt[:, :SSM_HEADS]], axis=1)
    dw_in = jnp.transpose(dw_in.reshape(D_MODEL, NDEV, in_shard), (1, 0, 2))
    dh0, = _mm_dx(ddtp, w_dt, "plain", D_MODEL, 128, epi=_ident, outs=(F32,), name="ssm_in_dt_dx")
    dh0, p_in_a = _mm_dx(dz, w_z, "plain", D_MODEL, D_INNER, epi=_add, outs=(F32,), extras=(dh0,),
                         carry=("scatter", [dw_in], [(0, half)]), name="ssm_in_z_dx")
    dh0, p_in_b = _mm_dx(du, w_x, "plain", D_MODEL, CONV_DIM, epi=_add, outs=(F32,), extras=(dh0,),
                         carry=("scatter", [dw_in], [(half, half)]), name="ssm_in_x_dx")
    dx0, _, dg_mix0 = _rmsnorm_bwd(x0, norm_mix[0:1], dh0, dx1, "norm_mix_bwd0")

    def whole(p):
        return [(p, 0, p.shape[1])]

    parts = [[(p_in_a, 0, half), (p_in_b, half, half)], whole(p_out), whole(p_qkv), whole(p_o),
             p_w1_0, p_w1_1, p_w2_0, p_w2_1]

    def own(w, mm, vv):
        shp = w.shape
        f = lambda a: a.reshape(-1, shp[-1])
        return f(w), f(mm), f(vv), shp

    big = {}
    for key, part, (w, mm, vv) in (
            ("ssm_w_in", parts[0], (ssm_w_in, m_ssm_w_in, v_ssm_w_in)),
            ("ssm_w_out", parts[1], (ssm_w_out, m_ssm_w_out, v_ssm_w_out)),
            ("attn_w_qkv", parts[2], (attn_w_qkv, m_attn_w_qkv, v_attn_w_qkv)),
            ("attn_w_o", parts[3], (attn_w_o, m_attn_w_o, v_attn_w_o))):
        w2, m2, v2, shp = own(w, mm, vv)
        res = _reduce_adamw(part, w2, m2, v2, f"adamw_{key}")
        big[key] = [r.reshape(shp) for r in res]
    for key, pa, pb, (w, mm, vv) in (("mlp_w1", parts[4], parts[5], (mlp_w1, m_mlp_w1, v_mlp_w1)),
                                     ("mlp_w2", parts[6], parts[7], (mlp_w2, m_mlp_w2, v_mlp_w2))):
        res = [_reduce_adamw(p, w[l], mm[l], vv[l], f"adamw_{key}_{l}") for l, p in enumerate((pa, pb))]
        big[key] = [jnp.stack([res[0][i], res[1][i]], axis=0) for i in range(4)]

    d_norm_mix = jnp.concatenate([dg_mix0, dg_mix1], axis=0)
    d_norm_mlp = jnp.concatenate([dg_mlp0, dg_mlp1], axis=0)
    d_conv_b = conv_sums[4:5]
    d_conv_w = conv_sums[0:4]
    head = hsums[:, :, :8]
    d_dt_bias, d_a_log, d_d = (head[:, k, :].reshape(1, SSM_HEADS) for k in range(3))
    d_ssm_norm = csums[0:1]
    small = [d_norm_mix, d_norm_mlp, d_conv_b, d_dt_bias, d_a_log, d_d, d_ssm_norm, d_final, d_conv_w, loss_acc[0:1, 0:1]]
    shapes = [a.shape for a in small]
    summed = _unpack(_all_reduce_small(_pack(small), "reduce_small"), shapes)
    g_conv_w_full = summed[8]
    loss = summed[9].reshape(())
    g_conv_w = lax.dynamic_slice(g_conv_w_full, (0, me * (CONV_DIM // NDEV)), (CONV_WIDTH, CONV_DIM // NDEV))

    small_names = ["norm_mix", "norm_mlp", "ssm_conv_b", "ssm_dt_bias", "ssm_a_log", "ssm_d", "ssm_norm_w", "final_norm"]
    small_w = [norm_mix, norm_mlp, ssm_conv_b, ssm_dt_bias, ssm_a_log, ssm_d, ssm_norm_w, final_norm, ssm_conv_w]
    small_m = [m_norm_mix, m_norm_mlp, m_ssm_conv_b, m_ssm_dt_bias, m_ssm_a_log, m_ssm_d, m_ssm_norm_w, m_final_norm, m_ssm_conv_w]
    small_v = [v_norm_mix, v_norm_mlp, v_ssm_conv_b, v_ssm_dt_bias, v_ssm_a_log, v_ssm_d, v_ssm_norm_w, v_final_norm, v_ssm_conv_w]
    small_g = [summed[i].reshape(small_w[i].shape) for i in range(8)] + [g_conv_w.reshape(ssm_conv_w.shape)]
    wshapes = [a.shape for a in small_w]
    sd, sm, sv = _adamw(_pack(small_g), _pack(small_w), _pack(small_m), _pack(small_v), "adamw_small")
    sd, sm, sv = _unpack(sd, wshapes), _unpack(sm, wshapes), _unpack(sv, wshapes)
    res = {n: (small_g[i], sd[i], sm[i], sv[i]) for i, n in enumerate(small_names + ["ssm_conv_w"])}
    for n in big:
        res[n] = tuple(big[n])

    order = ["norm_mix", "norm_mlp", "ssm_w_in", "ssm_conv_w", "ssm_conv_b", "ssm_dt_bias", "ssm_a_log", "ssm_d",
             "ssm_norm_w", "ssm_w_out", "attn_w_qkv", "attn_w_o", "mlp_w1", "mlp_w2", "final_norm"]
    outs = [loss, dx0.reshape(x.shape)]
    for kind in range(4):
        outs += [res[n][kind] for n in order]
    return tuple(outs)
```

```python
import math

import jax
import jax.numpy as jnp
from jax import lax
from jax.experimental import pallas as pl
from jax.experimental.pallas import tpu as pltpu

F32, BF16 = jnp.float32, jnp.bfloat16
SDS = jax.ShapeDtypeStruct
MESH = pl.DeviceIdType.MESH

NDEV = 8
D_MODEL = 2048
D_INNER = 4096
SSM_HEADS = 64
SSM_HEAD_DIM = 64
SSM_GROUPS = 8
SSM_STATE = 128
CHUNK = 128
CONV_DIM = 6144
CONV_WIDTH = 4
GROUP_W = D_INNER // SSM_GROUPS
ATTN_GROUPS = ((128, 1), (512, 4), (2048, 16))
ATTN_W = 1024
QKV_DIM = 9216
D_FF = 8192
EPS = 1e-5
ADAM_LR, ADAM_B1, ADAM_B2, ADAM_EPS, ADAM_WD, ADAM_STEP = 0.001, 0.9, 0.999, 1e-08, 0.01, 10

VMEM_LIMIT = 48 * 1024 * 1024


def _params(sem):
    return pltpu.CompilerParams(dimension_semantics=sem, vmem_limit_bytes=VMEM_LIMIT)


def _sigmoid(v):
    return 1.0 / (1.0 + jnp.exp(-v))


def _softplus(v):
    return jnp.maximum(v, 0.0) + jnp.log1p(jnp.exp(-jnp.abs(v)))


def _wspec(layout, kw, nw, tr, tc, sel):
    if layout == "plain":
        return pl.BlockSpec((tr, tc), lambda *g: sel(*g))
    per = (nw // NDEV) // tc
    return pl.BlockSpec((None, tr, tc), lambda *g: (sel(*g)[1] // per, sel(*g)[0], sel(*g)[1] % per))


def _wshape(layout, kw, nw):
    return (kw, nw) if layout == "plain" else (NDEV, kw, nw // NDEV)


def _mm_call(name, grid, in_specs, out_specs, out_shape, dims, n_extra, epi, tm, tn, carry):
    nk = grid[2]
    steps = grid[0] * grid[1] * nk
    kind, moved = carry[:2] if carry else (None, ())
    spans = carry[2] if carry and len(carry) > 2 else [None] * len(moved)
    nc = len(moved)
    n_out = len(out_shape)

    def body(*refs):
        a_ref, b_ref = refs[0], refs[1]
        extra = refs[2:2 + n_extra]
        c_in = refs[2 + n_extra:2 + n_extra + nc]
        outs = refs[2 + n_extra + nc:2 + n_extra + nc + n_out]
        c_out = refs[2 + n_extra + nc + n_out:2 + n_extra + 2 * nc + n_out]
        acc = refs[2 + n_extra + 2 * nc + n_out]
        sems = refs[3 + n_extra + 2 * nc + n_out:]
        i, j, k = pl.program_id(0), pl.program_id(1), pl.program_id(2)
        step = (i * grid[1] + j) * nk + k
        if nc:
            @pl.when(step == 0)
            def _():
                if kind == "gather":
                    _two_level("start", c_in, c_out, *sems)
                else:
                    _exchange_start(spans, c_in, c_out, *sems)

            if kind == "gather":
                @pl.when(step == (2 * steps) // 3)
                def _():
                    _two_level("pass", c_in, c_out, *sems)

        d = lax.dot_general(a_ref[...].astype(BF16), b_ref[...].astype(BF16), (dims, ((), ())), preferred_element_type=F32)

        def finish(total):
            vals = epi(total, *[e[...] for e in extra])
            for o, v in zip(outs, vals):
                o[...] = v.astype(o.dtype)

        if nk == 1:
            finish(d)
        else:
            @pl.when(k == 0)
            def _():
                acc[...] = d

            @pl.when(jnp.logical_and(k > 0, k < nk - 1))
            def _():
                acc[...] += d

            @pl.when(k == nk - 1)
            def _():
                finish(acc[...] + d)

        if nc:
            @pl.when(step == steps - 1)
            def _():
                if kind == "gather":
                    _two_level("wait", c_in, c_out, *sems)
                else:
                    _exchange_wait(spans, c_in, c_out, *sems)

    hbm = pl.BlockSpec(memory_space=pl.ANY)
    scratch = [pltpu.VMEM((tm, tn), F32)] + (_exchange_sems(nc) if nc else [])
    c_shape = [SDS(((NDEV,) + t.shape) if kind == "gather" else t.shape, t.dtype) for t in moved]
    sem = ("arbitrary",) * 3 if nc else ("parallel", "parallel", "arbitrary")
    return pl.pallas_call(
        body, grid=grid, in_specs=in_specs + [hbm] * nc, out_specs=out_specs + [hbm] * nc,
        out_shape=out_shape + c_shape, scratch_shapes=scratch, name=name, compiler_params=_params(sem))


def _mm_fwd(a, w, layout, kw, nw, *, epi, outs, extras=(), tm=1024, tn=1024, tk=2048, carry=None, name):
    m = a.shape[0]
    tm, tn, tk = min(tm, m), min(tn, nw), min(tk, kw)
    grid = (m // tm, nw // tn, kw // tk)
    o_spec = pl.BlockSpec((tm, tn), lambda i, j, k: (i, j))
    in_specs = [pl.BlockSpec((tm, tk), lambda i, j, k: (i, k)), _wspec(layout, kw, nw, tk, tn, lambda i, j, k: (k, j))]
    in_specs += [o_spec] * len(extras)
    call = _mm_call(name, grid, in_specs, [o_spec] * len(outs), [SDS((m, nw), dt) for dt in outs], ((1,), (0,)),
                    len(extras), epi, tm, tn, carry)
    return call(a, w, *extras, *(carry[1] if carry else ()))


def _mm_dx(g, w, layout, kw, nw, *, epi, outs, extras=(), tm=1024, tn=1024, tk=2048, carry=None, name):
    m = g.shape[0]
    tm, tn, tk = min(tm, m), min(tn, kw), min(tk, nw if layout == "plain" else nw // NDEV)
    grid = (m // tm, kw // tn, nw // tk)
    o_spec = pl.BlockSpec((tm, tn), lambda i, j, k: (i, j))
    in_specs = [pl.BlockSpec((tm, tk), lambda i, j, k: (i, k)), _wspec(layout, kw, nw, tn, tk, lambda i, j, k: (j, k))]
    in_specs += [o_spec] * len(extras)
    call = _mm_call(name, grid, in_specs, [o_spec] * len(outs), [SDS((m, kw), dt) for dt in outs], ((1,), (1,)),
                    len(extras), epi, tm, tn, carry)
    return call(g, w, *extras, *(carry[1] if carry else ()))


def _mm_dw(a, g, layout, *, tm=1024, tn=1024, tk=2048, carry=None, name):
    m, kw = a.shape
    nw = g.shape[1]
    tm, tn, tk = min(tm, kw), min(tn, nw if layout == "plain" else nw // NDEV), min(tk, m)
    grid = (kw // tm, nw // tn, m // tk)
    in_specs = [pl.BlockSpec((tk, tm), lambda i, j, k: (k, i)), pl.BlockSpec((tk, tn), lambda i, j, k: (k, j))]
    o_spec = _wspec(layout, kw, nw, tm, tn, lambda i, j, k: (i, j))
    call = _mm_call(name, grid, in_specs, [o_spec], [SDS(_wshape(layout, kw, nw), BF16)], ((0,), (0,)),
                    0, lambda t: (t,), tm, tn, carry)
    return call(a, g, *(carry[1] if carry else ()))


ROW_TILE = 256


def _rmsnorm_fwd(x, g, name):
    t, d = x.shape

    def body(x_ref, g_ref, h_ref):
        xv = x_ref[...]
        r = lax.rsqrt(jnp.mean(xv * xv, axis=-1, keepdims=True) + EPS)
        h_ref[...] = (xv * r * g_ref[...]).astype(BF16)

    row = pl.BlockSpec((ROW_TILE, d), lambda i: (i, 0))
    vec = pl.BlockSpec((1, d), lambda i: (0, 0))
    return pl.pallas_call(body, grid=(t // ROW_TILE,), in_specs=[row, vec], out_specs=row, out_shape=SDS((t, d), BF16),
                          name=name, compiler_params=_params(("parallel",)))(x, g)


def _rmsnorm_bwd(x, g, dh, dres, name):
    t, d = x.shape

    def body(x_ref, g_ref, dh_ref, dres_ref, dx_ref, dxb_ref, dg_ref):
        xv = x_ref[...]
        r = lax.rsqrt(jnp.mean(xv * xv, axis=-1, keepdims=True) + EPS)
        xh = xv * r
        dhv = dh_ref[...]
        gd = dhv * g_ref[...]
        dx = dres_ref[...] + r * (gd - xh * jnp.mean(gd * xh, axis=-1, keepdims=True))
        dx_ref[...] = dx
        dxb_ref[...] = dx.astype(BF16)
        part = jnp.sum(dhv * xh, axis=0, keepdims=True)

        @pl.when(pl.program_id(0) == 0)
        def _():
            dg_ref[...] = part

        @pl.when(pl.program_id(0) > 0)
        def _():
            dg_ref[...] += part

    row = pl.BlockSpec((ROW_TILE, d), lambda i: (i, 0))
    vec = pl.BlockSpec((1, d), lambda i: (0, 0))
    return pl.pallas_call(body, grid=(t // ROW_TILE,), in_specs=[row, vec, row, row], out_specs=[row, row, vec],
                          out_shape=[SDS((t, d), F32), SDS((t, d), BF16), SDS((1, d), F32)], name=name,
                          compiler_params=_params(("arbitrary",)))(x, g, dh, dres)


def _loss_head(x, tgt, g, name):
    t, d = x.shape

    def body(x_ref, t_ref, g_ref, dx_ref, dxb_ref, loss_ref, dg_ref):
        xv = x_ref[...]
        r = lax.rsqrt(jnp.mean(xv * xv, axis=-1, keepdims=True) + EPS)
        xh = xv * r
        gv = g_ref[...]
        err = xh * gv - t_ref[...]
        part_loss = 0.5 * jnp.sum(jnp.mean(err * err, axis=-1, keepdims=True), axis=0, keepdims=True)
        dy = err * (1.0 / d)
        gd = dy * gv
        dx = r * (gd - xh * jnp.mean(gd * xh, axis=-1, keepdims=True))
        dx_ref[...] = dx
        dxb_ref[...] = dx.astype(BF16)
        part_g = jnp.sum(dy * xh, axis=0, keepdims=True)
        part_l = jnp.broadcast_to(part_loss, (8, 128))

        @pl.when(pl.program_id(0) == 0)
        def _():
            dg_ref[...] = part_g
            loss_ref[...] = part_l

        @pl.when(pl.program_id(0) > 0)
        def _():
            dg_ref[...] += part_g
            loss_ref[...] += part_l

    row = pl.BlockSpec((ROW_TILE, d), lambda i: (i, 0))
    vec = pl.BlockSpec((1, d), lambda i: (0, 0))
    sc = pl.BlockSpec((8, 128), lambda i: (0, 0))
    return pl.pallas_call(body, grid=(t // ROW_TILE,), in_specs=[row, row, vec], out_specs=[row, row, sc, vec],
                          out_shape=[SDS((t, d), F32), SDS((t, d), BF16), SDS((8, 128), F32), SDS((1, d), F32)], name=name,
                          compiler_params=_params(("arbitrary",)))(x, tgt, g)


CONV_ROWS = 256
CONV_COLS = 2048


def _shift_down(cur, prev8, k):
    sh = pltpu.roll(cur, k, axis=0)
    ph = pltpu.roll(prev8, k, axis=0)
    rid = lax.broadcasted_iota(jnp.int32, ph.shape, 0)
    head = jnp.where(rid < k, ph, sh[0:8])
    return jnp.concatenate([head, sh[8:]], axis=0)


def _shift_up(cur, next8, k):
    n = cur.shape[0]
    sh = pltpu.roll(cur, n - k, axis=0)
    nh = pltpu.roll(next8, 8 - k, axis=0)
    rid = lax.broadcasted_iota(jnp.int32, nh.shape, 0)
    tail = jnp.where(rid >= 8 - k, nh, sh[n - 8:])
    return jnp.concatenate([sh[:n - 8], tail], axis=0)


def _conv_pre(cur, prev8, w, b):
    acc = w[3:4, :] * cur + b
    for k in range(1, CONV_WIDTH):
        acc = acc + w[3 - k:4 - k, :] * _shift_down(cur, prev8, k)
    return acc


def _conv_fwd(u, w, b, name):
    t, c = u.shape
    per = CONV_ROWS // 8

    def body(u_ref, p_ref, w_ref, b_ref, o_ref):
        prev8 = jnp.where(pl.program_id(1) == 0, 0.0, p_ref[...])
        pre = _conv_pre(u_ref[...], prev8, w_ref[...], b_ref[...])
        o_ref[...] = pre * _sigmoid(pre)

    cur = pl.BlockSpec((CONV_ROWS, CONV_COLS), lambda j, i: (i, j))
    prev = pl.BlockSpec((8, CONV_COLS), lambda j, i: (jnp.maximum(i * per - 1, 0), j))
    wsp = pl.BlockSpec((CONV_WIDTH, CONV_COLS), lambda j, i: (0, j))
    bsp = pl.BlockSpec((1, CONV_COLS), lambda j, i: (0, j))
    return pl.pallas_call(body, grid=(c // CONV_COLS, t // CONV_ROWS), in_specs=[cur, prev, wsp, bsp], out_specs=cur,
                          out_shape=SDS((t, c), F32), name=name, compiler_params=_params(("parallel", "parallel")))(u, u, w, b)


def _conv_bwd(u, w, b, dout, du_prev, col0, name):
    t, c = u.shape
    per = CONV_ROWS // 8
    last = t // CONV_ROWS - 1

    def dsilu(pre, d):
        s = _sigmoid(pre)
        return d * (s * (1.0 + pre * (1.0 - s)))

    def body(u_ref, p_ref, n_ref, w_ref, b_ref, d_ref, dn_ref, *rest):
        du_ref, dw_ref = rest[-2:]
        i = pl.program_id(1)
        cur = u_ref[...]
        wv, bv = w_ref[...], b_ref[...]
        prev8 = jnp.where(i == 0, 0.0, p_ref[...])
        dpre = dsilu(_conv_pre(cur, prev8, wv, bv), d_ref[...])
        nxt, tail = n_ref[...], cur[CONV_ROWS - 8:]
        rid = lax.broadcasted_iota(jnp.int32, nxt.shape, 0)
        pre_n = wv[3:4, :] * nxt + bv
        for k in range(1, CONV_WIDTH):
            pre_n = pre_n + wv[3 - k:4 - k, :] * jnp.where(rid < k, pltpu.roll(tail, k, axis=0), pltpu.roll(nxt, k, axis=0))
        dpre_n = jnp.where(i == last, 0.0, dsilu(pre_n, dn_ref[...]))
        du = wv[3:4, :] * dpre
        for k in range(1, CONV_WIDTH):
            du = du + wv[3 - k:4 - k, :] * _shift_up(dpre, dpre_n, k)
        du_ref[...] = du.astype(BF16)
        rows = [jnp.sum(dpre * _shift_down(cur, prev8, 3 - k), axis=0, keepdims=True) for k in range(3)]
        rows.append(jnp.sum(dpre * cur, axis=0, keepdims=True))
        rows.append(jnp.sum(dpre, axis=0, keepdims=True))
        part = jnp.concatenate(rows + [jnp.zeros((3, cur.shape[1]), F32)], axis=0)

        @pl.when(i == 0)
        def _():
            dw_ref[...] = part

        @pl.when(i > 0)
        def _():
            dw_ref[...] += part

    width = dout.shape[1]
    tc = min(CONV_COLS, width)
    ob = col0 // tc

    def at(rows, here):
        off = ob if here else 0
        if rows == CONV_ROWS:
            return pl.BlockSpec((rows, tc), lambda j, i: (i, off + j))
        if rows == -8:
            return pl.BlockSpec((8, tc), lambda j, i: (jnp.maximum(i * per - 1, 0), off + j))
        if rows == 8:
            return pl.BlockSpec((8, tc), lambda j, i: (jnp.minimum((i + 1) * per, t // 8 - 1), off + j))
        return pl.BlockSpec((rows, tc), lambda j, i: (0, off + j))

    in_specs = [at(CONV_ROWS, True), at(-8, True), at(8, True), at(CONV_WIDTH, True), at(1, True),
                at(CONV_ROWS, False), at(8, False)]
    args = [u, u, u, w, b, dout, dout]
    aliases = {}
    if du_prev is not None:
        in_specs.append(pl.BlockSpec(memory_space=pl.ANY))
        args.append(du_prev)
        aliases = {len(args) - 1: 0}
    acc = pl.BlockSpec((8, tc), lambda j, i: (0, j))
    return pl.pallas_call(body, grid=(width // tc, t // CONV_ROWS), in_specs=in_specs,
                          out_specs=[at(CONV_ROWS, True), acc], out_shape=[SDS((t, c), BF16), SDS((8, width), F32)],
                          input_output_aliases=aliases, name=name,
                          compiler_params=_params(("parallel", "arbitrary")))(*args)


def _pieces(v, n):
    out, rest = [], v
    for _ in range(n):
        p = rest.astype(BF16)
        out.append(p)
        rest = rest - p.astype(F32)
    return out


def _head_expand(v):
    r = lax.broadcasted_iota(jnp.int32, (3 * 128, GROUP_W), 0) % 128
    c = lax.broadcasted_iota(jnp.int32, (3 * 128, GROUP_W), 1)
    return _dot(jnp.concatenate(_pieces(v, 3), axis=1), (c // SSM_HEAD_DIM == r).astype(BF16))


def _head_sum(vs):
    r = lax.broadcasted_iota(jnp.int32, (2 * GROUP_W, 128), 0) % GROUP_W
    c = lax.broadcasted_iota(jnp.int32, (2 * GROUP_W, 128), 1)
    stacked = jnp.concatenate([jnp.concatenate(_pieces(v, 2), axis=1) for v in vs], axis=0)
    out = _dot(stacked, (r // SSM_HEAD_DIM == c).astype(BF16))
    res, at = [], 0
    for v in vs:
        res.append(out[at:at + v.shape[0]])
        at += v.shape[0]
    return res


def _tri_left(tri, v):
    r = _dot(tri.astype(BF16), jnp.concatenate(_pieces(v, 3), axis=1))
    return r[:, 0:128] + r[:, 128:256] + r[:, 256:384]


def _dot(a, b):
    return jnp.dot(a, b, preferred_element_type=F32)


def _dot_nt(a, b):
    return lax.dot_general(a, b, (((1,), (1,)), ((), ())), preferred_element_type=F32)


def _dot_tn(a, b):
    return lax.dot_general(a, b, (((0,), (0,)), ((), ())), preferred_element_type=F32)


def _ssd_common(dtp, dtpt, bias, biasr, alog, alogr):
    li = lax.broadcasted_iota(jnp.int32, (CHUNK, CHUNK), 0)
    si = lax.broadcasted_iota(jnp.int32, (CHUNK, CHUNK), 1)
    lower, upper = (li >= si), (li <= si)
    dt = _softplus(dtp + bias)
    a_neg = -jnp.exp(alog)
    cs = _tri_left(lower, dt * a_neg)
    dtr = _softplus(dtpt + biasr)
    ar = jnp.concatenate([dtr * (-jnp.exp(alogr)), jnp.zeros((8, CHUNK), F32)], axis=0)
    r3 = lax.broadcasted_iota(jnp.int32, (3 * CHUNK, CHUNK), 0) % CHUNK
    c3 = lax.broadcasted_iota(jnp.int32, (3 * CHUNK, CHUNK), 1)
    csr = _dot(jnp.concatenate(_pieces(ar, 3), axis=1), (r3 <= c3).astype(BF16))[0:8]
    return lower, upper, dt, a_neg, cs, csr


def _head_masked_rows(v):
    hl = lax.broadcasted_iota(jnp.int32, v.shape, 1) // SSM_HEAD_DIM
    return jnp.concatenate([jnp.where(hl == j, v, jnp.zeros_like(v)) for j in range(8)], axis=0)


SSD_STEP_CHUNKS = 8


def _chunk_view(ref, kind, k):
    if kind == "rows":
        return ref.at[pl.ds(k * CHUNK, CHUNK), :]
    if kind == "lanes":
        return ref.at[:, pl.ds(k * CHUNK, CHUNK)]
    if kind == "lead":
        return ref.at[k]
    return ref


def _ssd_fwd(xbc, z, dtp_g, dtp_t, bias_g, bias_r, alog_g, alog_r, d_exp, norm_w, name):
    t = xbc.shape[0]
    nc = t // CHUNK

    def body(*refs):
        @pl.when(pl.program_id(1) == 0)
        def _():
            refs[-1][...] = jnp.zeros_like(refs[-1])

        for k in range(SSD_STEP_CHUNKS):
            chunk(*[_chunk_view(r, kind, k) for r, kind in zip(refs, kinds)])

    kinds = ["rows"] * 5 + ["lanes"] + [None] * 6 + ["rows", "rows", "lead", None]

    def chunk(xs_ref, b_ref, c_ref, z_ref, dtp_ref, dtpt_ref, bias_ref, biasr_ref, alog_ref, alogr_ref, dexp_ref, nw_ref,
              y_ref, yn_ref, st_ref, state):
        lower, _, dt, a_neg, cs, csr = _ssd_common(dtp_ref[...], dtpt_ref[...], bias_ref[...], biasr_ref[...],
                                                alog_ref[...], alogr_ref[...])
        cs_e = _head_expand(cs)
        dt_e = _head_expand(dt)
        xs = xs_ref[...]
        xdt = xs * dt_e
        bm = b_ref[...]
        cm = c_ref[...]
        bmb, cmb = bm.astype(BF16), cm.astype(BF16)
        cb = _dot_nt(cmb, bmb)
        ms = []
        for j in range(8):
            dlt = cs_e[:, SSM_HEAD_DIM * j:SSM_HEAD_DIM * j + 1] - csr[j:j + 1, :]
            ms.append((cb * jnp.exp(jnp.where(lower, dlt, -jnp.inf))).astype(BF16))
        y = _dot(jnp.concatenate(ms, axis=1), _head_masked_rows(xdt.astype(BF16)))
        st_in = state[...]
        st_ref[...] = st_in
        y = y + jnp.exp(cs_e) * _dot(cmb, st_in.astype(BF16))
        cs_last = cs_e[CHUNK - 1:CHUNK, :]
        xdtd = (xdt * jnp.exp(cs_last - cs_e)).astype(BF16)
        state[...] = jnp.exp(cs_last) * st_in + _dot(bm.T.astype(BF16), xdtd)
        y_ref[...] = y
        zz = z_ref[...]
        y2 = (y + dexp_ref[...] * xs) * (zz * _sigmoid(zz))
        r = lax.rsqrt(jnp.mean(y2 * y2, axis=-1, keepdims=True) + EPS)
        yn_ref[...] = (y2 * r * nw_ref[...]).astype(BF16)

    rows = SSD_STEP_CHUNKS * CHUNK
    gw = pl.BlockSpec((rows, GROUP_W), lambda g, c: (c, g))
    in_specs = [
        gw,
        pl.BlockSpec((rows, SSM_STATE), lambda g, c: (c, D_INNER // SSM_STATE + g)),
        pl.BlockSpec((rows, SSM_STATE), lambda g, c: (c, D_INNER // SSM_STATE + SSM_GROUPS + g)),
        gw,
        pl.BlockSpec((None, rows, 128), lambda g, c: (g, c, 0)),
        pl.BlockSpec((8, rows), lambda g, c: (g, c)),
        pl.BlockSpec((None, 1, 128), lambda g, c: (g, 0, 0)),
        pl.BlockSpec((8, 128), lambda g, c: (g, 0)),
        pl.BlockSpec((None, 1, 128), lambda g, c: (g, 0, 0)),
        pl.BlockSpec((8, 128), lambda g, c: (g, 0)),
        pl.BlockSpec((1, GROUP_W), lambda g, c: (0, g)),
        pl.BlockSpec((1, GROUP_W), lambda g, c: (0, g)),
    ]
    out_specs = [gw, gw, pl.BlockSpec((SSD_STEP_CHUNKS, SSM_STATE, GROUP_W), lambda g, c: (c, 0, g))]
    out_shape = [SDS((t, D_INNER), F32), SDS((t, D_INNER), BF16), SDS((nc, SSM_STATE, D_INNER), F32)]
    return pl.pallas_call(body, grid=(SSM_GROUPS, nc // SSD_STEP_CHUNKS), in_specs=in_specs, out_specs=out_specs,
                          out_shape=out_shape,
                          scratch_shapes=[pltpu.VMEM((SSM_STATE, GROUP_W), F32)], name=name,
                          compiler_params=_params(("parallel", "arbitrary")))(
        xbc, xbc, xbc, z, dtp_g, dtp_t, bias_g, bias_r, alog_g, alog_r, d_exp, norm_w)


def _ssd_bwd(dyn, y, xbc, z, states, dtp_g, dtp_t, bias_g, bias_r, alog_g, alog_r, d_exp, norm_w, name):
    t = xbc.shape[0]
    nc = t // CHUNK

    def body(*refs):
        step = pl.program_id(1)
        hsum_ref, csum_ref, dstate = refs[-3:]

        @pl.when(step == 0)
        def _():
            dstate[...] = jnp.zeros_like(dstate)

        parts = [chunk(*[_chunk_view(r, kind, k) for r, kind in zip(refs, kinds)])
                 for k in reversed(range(SSD_STEP_CHUNKS))]
        hpart, cpart = parts[0]
        for hp, cp in parts[1:]:
            hpart, cpart = hpart + hp, cpart + cp

        @pl.when(step == 0)
        def _():
            hsum_ref[...] = hpart
            csum_ref[...] = cpart

        @pl.when(step > 0)
        def _():
            hsum_ref[...] += hpart
            csum_ref[...] += cpart

    kinds = ["rows"] * 6 + ["lead", "rows", "lanes"] + [None] * 6 + ["rows"] * 5 + [None] * 3

    def chunk(dyn_ref, y_ref, xs_ref, b_ref, c_ref, z_ref, st_ref, dtp_ref, dtpt_ref, bias_ref, biasr_ref, alog_ref,
              alogr_ref, dexp_ref, nw_ref, dxs_ref, db_ref, dc_ref, dz_ref, ddt_ref, hsum_ref, csum_ref, dstate):
        dtp = dtp_ref[...]
        bias = bias_ref[...]
        lower, upper, dt, a_neg, cs, csr = _ssd_common(dtp, dtpt_ref[...], bias, biasr_ref[...], alog_ref[...],
                                                       alogr_ref[...])
        cs_e = _head_expand(cs)
        dt_e = _head_expand(dt)
        xs = xs_ref[...]
        xdt = xs * dt_e
        bm = b_ref[...]
        cm = c_ref[...]
        bmb, cmb = bm.astype(BF16), cm.astype(BF16)
        y = y_ref[...]
        dexp = dexp_ref[...]
        nw = nw_ref[...]

        zz = z_ref[...]
        sg = _sigmoid(zz)
        gate = zz * sg
        ytot = y + dexp * xs
        y2 = ytot * gate
        r = lax.rsqrt(jnp.mean(y2 * y2, axis=-1, keepdims=True) + EPS)
        dynv = dyn_ref[...]
        xh = y2 * r
        gn = dynv * nw
        dy2 = r * (gn - xh * jnp.mean(gn * xh, axis=-1, keepdims=True))
        dy = dy2 * gate
        dz_ref[...] = (dy2 * ytot * (sg * (1.0 + zz * (1.0 - sg)))).astype(BF16)
        csum_part = jnp.sum(dynv * xh, axis=0, keepdims=True)

        cb = _dot_nt(cmb, bmb)
        dyb = dy.astype(BF16)
        xdtb = xdt.astype(BF16)
        dym = _head_masked_rows(dyb)
        dm = _dot_nt(dym, xdtb)
        dmt = _dot_nt(_head_masked_rows(xdtb), dyb)
        lane = lax.broadcasted_iota(jnp.int32, (CHUNK, 128), 1)
        mts = []
        dcb = jnp.zeros((CHUNK, CHUNK), F32)
        dcs = jnp.zeros((CHUNK, 128), F32)
        for j in range(8):
            dlt = cs_e[:, SSM_HEAD_DIM * j:SSM_HEAD_DIM * j + 1] - csr[j:j + 1, :]
            lj = jnp.exp(jnp.where(lower, dlt, -jnp.inf))
            mj = cb * lj
            mjt = mj.T
            mts.append(mjt.astype(BF16))
            dmj = dm[CHUNK * j:CHUNK * (j + 1)]
            dcb = dcb + dmj * lj
            rows = jnp.sum(dmj * mj, axis=1, keepdims=True)
            cols = jnp.sum(dmt[CHUNK * j:CHUNK * (j + 1)] * mjt, axis=1, keepdims=True)
            dcs = dcs + jnp.where(lane == j, rows - cols, 0.0)
        dxdt = _dot(jnp.concatenate(mts, axis=1), dym)
        dst_out = dstate[...]
        dst_outb = dst_out.astype(BF16)
        st_in = st_ref[...]
        st_inb = st_in.astype(BF16)
        cs_last = cs_e[CHUNK - 1:CHUNK, :]
        decay = jnp.exp(cs_last - cs_e)
        e_last = jnp.exp(cs_last)
        gpart = decay * _dot(bmb, dst_outb)
        dxdt = dxdt + gpart
        dyw = (jnp.exp(cs_e) * dy).astype(BF16)
        dcbb = dcb.astype(BF16)
        dc_ref[...] = _dot_nt(dyw, st_inb) + _dot(dcbb, bmb)
        db_ref[...] = _dot_nt((xdt * decay).astype(BF16), dst_outb) + _dot(dcb.T.astype(BF16), cmb)
        dstate[...] = e_last * dst_out + _dot(cm.T.astype(BF16), dyw)
        y_off = jnp.exp(cs_e) * _dot(cmb, st_inb)
        xg = xdt * gpart
        vec = jnp.concatenate([jnp.sum(dy * xs, axis=0, keepdims=True),
                               jnp.sum(xg + dst_out * e_last * st_in, axis=0, keepdims=True),
                               jnp.zeros((14, GROUP_W), F32)], axis=0)
        s_cs, s_dt, s_vec = _head_sum([dy * y_off - xg, dxdt * xs, vec])
        d_skip = s_vec[0:1]
        ri = lax.broadcasted_iota(jnp.int32, (CHUNK, 128), 0)
        dcs = dcs + s_cs + jnp.where(ri == CHUNK - 1, s_vec[1:2], 0.0)
        da = _tri_left(upper, dcs)
        ddt = da * a_neg + s_dt
        dxs_ref[...] = dxdt * dt_e + dy * dexp
        ddtp = ddt * _sigmoid(dtp + bias)
        ddt_ref[...] = ddtp
        d_alog = jnp.sum(da * dt, axis=0, keepdims=True) * a_neg
        hpart = jnp.concatenate([jnp.sum(ddtp, axis=0, keepdims=True), d_alog, d_skip, jnp.zeros((5, 128), F32)], axis=0)
        cpart = jnp.concatenate([csum_part, jnp.zeros((7, GROUP_W), F32)], axis=0)
        return hpart, cpart

    steps = nc // SSD_STEP_CHUNKS
    rows = SSD_STEP_CHUNKS * CHUNK
    rc = lambda c: steps - 1 - c
    gw = pl.BlockSpec((rows, GROUP_W), lambda g, c: (rc(c), g))
    bsp = pl.BlockSpec((rows, SSM_STATE), lambda g, c: (rc(c), D_INNER // SSM_STATE + g))
    csp = pl.BlockSpec((rows, SSM_STATE), lambda g, c: (rc(c), D_INNER // SSM_STATE + SSM_GROUPS + g))
    in_specs = [
        gw, gw, gw, bsp, csp, gw,
        pl.BlockSpec((SSD_STEP_CHUNKS, SSM_STATE, GROUP_W), lambda g, c: (rc(c), 0, g)),
        pl.BlockSpec((None, rows, 128), lambda g, c: (g, rc(c), 0)),
        pl.BlockSpec((8, rows), lambda g, c: (g, rc(c))),
        pl.BlockSpec((None, 1, 128), lambda g, c: (g, 0, 0)),
        pl.BlockSpec((8, 128), lambda g, c: (g, 0)),
        pl.BlockSpec((None, 1, 128), lambda g, c: (g, 0, 0)),
        pl.BlockSpec((8, 128), lambda g, c: (g, 0)),
        pl.BlockSpec((1, GROUP_W), lambda g, c: (0, g)),
        pl.BlockSpec((1, GROUP_W), lambda g, c: (0, g)),
    ]
    nsp = pl.BlockSpec((rows, SSM_STATE), lambda g, c: (rc(c), g))
    out_specs = [gw, nsp, nsp, gw,
                 pl.BlockSpec((None, rows, 128), lambda g, c: (g, rc(c), 0)),
                 pl.BlockSpec((None, 8, 128), lambda g, c: (g, 0, 0)),
                 pl.BlockSpec((8, GROUP_W), lambda g, c: (0, g))]
    gn = SSM_GROUPS * SSM_STATE
    out_shape = [SDS((t, D_INNER), F32), SDS((t, gn), F32), SDS((t, gn), F32), SDS((t, D_INNER), BF16),
                 SDS((SSM_GROUPS, t, 128), F32), SDS((SSM_GROUPS, 8, 128), F32), SDS((8, D_INNER), F32)]
    return pl.pallas_call(body, grid=(SSM_GROUPS, steps), in_specs=in_specs, out_specs=out_specs, out_shape=out_shape,
                          scratch_shapes=[pltpu.VMEM((SSM_STATE, GROUP_W), F32)], name=name,
                          compiler_params=_params(("parallel", "arbitrary")))(
        dyn, y, xbc, xbc, xbc, z, states, dtp_g, dtp_t, bias_g, bias_r, alog_g, alog_r, d_exp, norm_w)


BLK = 128
HEAD_PAIRS = ATTN_W // 128
ATTN_SCALE = 0.125
SPAN_BLOCKS = {1: 8, 4: 2, 16: 1}


def _slope_table(group):
    n = len(ATTN_GROUPS) * 16
    tbl = [[2.0 ** (-8.0 * (16 * group + 2 * p + s + 1) / n) if s < 2 else 0.0 for s in range(128)] for p in range(HEAD_PAIRS)]
    return jnp.asarray(tbl, F32)


def _lane_lo(rows):
    return lax.broadcasted_iota(jnp.int32, (rows, 128), 1) < 64


def _rows(start, dil):
    return pl.ds(start, BLK, stride=dil) if dil > 1 else pl.ds(start, BLK)


def _stack_heads(x):
    lo = _lane_lo(BLK)
    return jnp.concatenate([jnp.where(lo, x, jnp.zeros_like(x)), jnp.where(lo, jnp.zeros_like(x), x)], axis=0)


def _pair_cols(x):
    return jnp.concatenate([x[:, 0:1], x[:, 64:65]], axis=0)


def _attn_bias(sl, dil, first_span, last_span):
    qi = lax.broadcasted_iota(jnp.int32, (BLK, 2 * BLK), 0)
    kj = lax.broadcasted_iota(jnp.int32, (BLK, 2 * BLK), 1)
    dist = qi + BLK - kj
    valid = (dist >= 0) & (dist <= BLK)
    distf = dist.astype(F32) * float(dil)

    def stacked(ok, d):
        return jnp.concatenate([jnp.where(ok, -sl[:, h:h + 1] * d, -jnp.inf) for h in range(2)], axis=0)

    first = stacked(valid & ((kj >= BLK) | jnp.logical_not(first_span)), distf)
    after = None
    if last_span is not None:
        after = stacked((kj[:, :BLK] >= qi[:, :BLK]) & jnp.logical_not(last_span), distf[:, :BLK])
    return first, stacked(valid, distf), after


def _attn_specs(group, t):
    _, dil = ATTN_GROUPS[group]
    nblk = SPAN_BLOCKS[dil]
    span, edge = BLK * dil * nblk, BLK * dil
    per = span // edge

    def lane_block(which):
        if which is None:
            return lambda p: p
        return lambda p: 3 * (group * HEAD_PAIRS + p) + which

    def cur(which):
        col = lane_block(which)
        return pl.BlockSpec((span, 128), lambda s, p: (s, col(p)))

    def before(which):
        col = lane_block(which)
        return pl.BlockSpec((edge, 128), lambda s, p: (jnp.maximum(s * per - 1, 0), col(p)))

    def after(which):
        col = lane_block(which)
        return pl.BlockSpec((edge, 128), lambda s, p: (jnp.minimum((s + 1) * per, t // edge - 1), col(p)))

    slopes = pl.BlockSpec((HEAD_PAIRS, 128), lambda s, p: (0, 0))
    return dil, nblk, span, cur, before, after, slopes


def _pair_major(w, inverse=False):
    k = w.shape[0]
    g = len(ATTN_GROUPS)
    if inverse:
        return jnp.transpose(w.reshape(k, g, HEAD_PAIRS, 3, 128), (0, 3, 1, 2, 4)).reshape(k, QKV_DIM)
    return jnp.transpose(w.reshape(k, 3, g, HEAD_PAIRS, 128), (0, 2, 3, 1, 4)).reshape(k, QKV_DIM)


def _attn_fwd(qkv, group, name):
    t = qkv.shape[0]
    dil, nblk, span, cur, before, after, slopes = _attn_specs(group, t)

    def body(q_ref, k_ref, v_ref, kp_ref, vp_ref, sl_ref, o_ref, l_ref):
        first_span = pl.program_id(0) == 0
        bias_first, bias_mid, _ = _attn_bias(sl_ref[pl.ds(pl.program_id(1), 1), :], dil, first_span, None)
        lo_q, lo_k = _lane_lo(BLK), _lane_lo(2 * BLK)
        for r in range(dil):
            kp, vp = kp_ref[_rows(r, dil), :].astype(BF16), vp_ref[_rows(r, dil), :].astype(BF16)
            for b in range(nblk):
                rows = _rows(b * BLK * dil + r, dil)
                kc, vc = k_ref[rows, :].astype(BF16), v_ref[rows, :].astype(BF16)
                k2 = jnp.concatenate([kp, kc], axis=0)
                v2 = jnp.concatenate([vp, vc], axis=0)
                kp, vp = kc, vc
                s = _dot_nt(_stack_heads((q_ref[rows, :] * ATTN_SCALE).astype(BF16)), k2) + (bias_first if b == 0 else bias_mid)
                mx = jnp.max(s, axis=-1, keepdims=True)
                p = jnp.exp(s - mx)
                den = jnp.sum(p, axis=-1, keepdims=True)
                pb = p.astype(BF16)
                o_ref[rows, :] = _dot(pb[:BLK], jnp.where(lo_k, v2, jnp.zeros_like(v2))) / den[:BLK] + \
                    _dot(pb[BLK:], jnp.where(lo_k, jnp.zeros_like(v2), v2)) / den[BLK:]
                lse = mx + jnp.log(den)
                l_ref[rows, :] = jnp.where(lo_q, lse[:BLK], lse[BLK:])

    in_specs = [cur(0), cur(1), cur(2), before(1), before(2), slopes]
    return pl.pallas_call(body, grid=(t // span, HEAD_PAIRS), in_specs=in_specs, out_specs=[cur(None), cur(None)],
                          out_shape=[SDS((t, ATTN_W), F32), SDS((t, ATTN_W), F32)], name=name,
                          compiler_params=_params(("parallel", "parallel")))(qkv, qkv, qkv, qkv, qkv, _slope_table(group))


def _attn_bwd(qkv, do, lse, dvec, dqkv, group, name):
    t = qkv.shape[0]
    dil, nblk, span, cur, before, after, slopes = _attn_specs(group, t)
    nspan = t // span

    def body(q_ref, k_ref, v_ref, kp_ref, vp_ref, qn_ref, do_ref, l_ref, d_ref, don_ref, ln_ref, dn_ref, sl_ref, *rest):
        out_ref, dq_s, dk_s, dv_s = rest[-4:]
        span_id, pair_id = pl.program_id(0), pl.program_id(1)
        bias_first, bias_mid, bias_next = _attn_bias(sl_ref[pl.ds(pair_id, 1), :], dil, span_id == 0, span_id == nspan - 1)
        lo_k = _lane_lo(2 * BLK)
        for r in range(dil):
            kp, vp = kp_ref[_rows(r, dil), :].astype(BF16), vp_ref[_rows(r, dil), :].astype(BF16)
            held = None
            for b in range(nblk + 1):
                last = b == nblk
                rows = _rows(r if last else b * BLK * dil + r, dil)
                qs, dos, ls, dvs = (qn_ref, don_ref, ln_ref, dn_ref) if last else (q_ref, do_ref, l_ref, d_ref)
                qst = _stack_heads((qs[rows, :] * ATTN_SCALE).astype(BF16))
                dost = _stack_heads(dos[rows, :].astype(BF16))
                if last:
                    k2, v2, bias = kp, vp, bias_next
                else:
                    kc, vc = k_ref[rows, :].astype(BF16), v_ref[rows, :].astype(BF16)
                    k2, v2 = jnp.concatenate([kp, kc], axis=0), jnp.concatenate([vp, vc], axis=0)
                    kp, vp = kc, vc
                    bias = bias_first if b == 0 else bias_mid
                p = jnp.exp(_dot_nt(qst, k2) + bias - _pair_cols(ls[rows, :]))
                ds = (p * (_dot_nt(dost, v2) - _pair_cols(dvs[rows, :]))).astype(BF16)
                dk2 = _dot_tn(ds, qst)
                dv2 = _dot_tn(p.astype(BF16), dost)
                if held is not None:
                    dk_s[held[0], :] = held[1] + dk2[:BLK]
                    dv_s[held[0], :] = held[2] + dv2[:BLK]
                if not last:
                    k_heads = jnp.concatenate([jnp.where(lo_k, k2, jnp.zeros_like(k2)),
                                               jnp.where(lo_k, jnp.zeros_like(k2), k2)], axis=0)
                    dq_s[rows, :] = _dot(jnp.concatenate([ds[:BLK], ds[BLK:]], axis=1), k_heads) * ATTN_SCALE
                    held = (rows, dk2[BLK:], dv2[BLK:])
        out_ref[:, 0:128] = dq_s[...]
        out_ref[:, 128:256] = dk_s[...]
        out_ref[:, 256:384] = dv_s[...]

    out_spec = pl.BlockSpec((span, 3 * 128), lambda s, p: (s, group * HEAD_PAIRS + p))
    in_specs = [cur(0), cur(1), cur(2), before(1), before(2), after(0), cur(None), cur(None), cur(None),
                after(None), after(None), after(None), slopes]
    args = [qkv, qkv, qkv, qkv, qkv, qkv, do, lse, dvec, do, lse, dvec, _slope_table(group)]
    aliases = {}
    if dqkv is not None:
        in_specs.append(pl.BlockSpec(memory_space=pl.ANY))
        args.append(dqkv)
        aliases = {len(args) - 1: 0}
    return pl.pallas_call(body, grid=(nspan, HEAD_PAIRS), in_specs=in_specs, out_specs=out_spec,
                          out_shape=SDS((t, QKV_DIM), F32), input_output_aliases=aliases,
                          scratch_shapes=[pltpu.VMEM((span, 128), F32)] * 3, name=name,
                          compiler_params=_params(("parallel", "parallel")))(*args)


def _combine_weights(l0, l1, l2):
    mx = jnp.maximum(jnp.maximum(l0, l1), l2)
    e0, e1, e2 = jnp.exp(l0 - mx), jnp.exp(l1 - mx), jnp.exp(l2 - mx)
    den = e0 + e1 + e2
    return e0 / den, e1 / den, e2 / den


def _combine_fwd(os_, ls_, name):
    t = os_[0].shape[0]

    def body(o0, o1, o2, l0, l1, l2, out):
        w0, w1, w2 = _combine_weights(l0[...], l1[...], l2[...])
        out[...] = (w0 * o0[...] + w1 * o1[...] + w2 * o2[...]).astype(BF16)

    row = pl.BlockSpec((ROW_TILE, ATTN_W), lambda i: (i, 0))
    return pl.pallas_call(body, grid=(t // ROW_TILE,), in_specs=[row] * 6, out_specs=row, out_shape=SDS((t, ATTN_W), BF16),
                          name=name, compiler_params=_params(("parallel",)))(*os_, *ls_)


def _combine_bwd(do, os_, ls_, name):
    t = do.shape[0]

    def body(do_ref, o0, o1, o2, l0, l1, l2, g0, g1, g2, d0, d1, d2):
        w0, w1, w2 = _combine_weights(l0[...], l1[...], l2[...])
        dov = do_ref[...]
        prod = dov * (w0 * o0[...] + w1 * o1[...] + w2 * o2[...])
        r = (lax.broadcasted_iota(jnp.int32, (3 * 128, 128), 0) % 128) // 64
        c = lax.broadcasted_iota(jnp.int32, (3 * 128, 128), 1) // 64
        same = (r == c).astype(BF16)
        tbar = jnp.concatenate([_dot(jnp.concatenate(_pieces(prod[:, 128 * k:128 * k + 128], 3), axis=1), same)
                                for k in range(HEAD_PAIRS)], axis=1)
        for w, g, d in ((w0, g0, d0), (w1, g1, d1), (w2, g2, d2)):
            g[...] = w * dov
            d[...] = w * tbar

    row = pl.BlockSpec((ROW_TILE, ATTN_W), lambda i: (i, 0))
    return pl.pallas_call(body, grid=(t // ROW_TILE,), in_specs=[row] * 7, out_specs=[row] * 6,
                          out_shape=[SDS((t, ATTN_W), F32)] * 6, name=name,
                          compiler_params=_params(("parallel",)))(do, *os_, *ls_)


def _peer(k):
    x, y, c = lax.axis_index("x"), lax.axis_index("y"), lax.axis_index("c")
    px = 1 - x if k & 4 else x
    py = 1 - y if k & 2 else y
    pc = 1 - c if k & 1 else c
    return (px, py, pc), 4 * px + 2 * py + pc


def _my_index():
    return 4 * lax.axis_index("x") + 2 * lax.axis_index("y") + lax.axis_index("c")


def _exchange_sems(n):
    return [pltpu.SemaphoreType.DMA((n * (NDEV - 1),)), pltpu.SemaphoreType.DMA((n * (NDEV - 1),)),
            pltpu.SemaphoreType.DMA((n,))]


def _scatter_copies(spans, ins, outs, send, recv, local, arrivals):
    me = _my_index()
    own, sent, arriving = [], [], []
    for i in range(len(ins)):
        def part(ref, slot, i=i):
            return ref.at[slot] if spans[i] is None else ref.at[slot, pl.ds(spans[i][0], spans[i][1])]
        own.append(pltpu.make_async_copy(part(ins[i], me), part(outs[i], me), local.at[i]))
        for k in range(1, NDEV):
            peer, pidx = _peer(k)
            s = i * (NDEV - 1) + k - 1
            for dst, into in ((part(outs[i], me), sent), (part(outs[i], pidx), arriving)):
                if into is sent or arrivals:
                    into.append(pltpu.make_async_remote_copy(src_ref=part(ins[i], pidx), dst_ref=dst, send_sem=send.at[s],
                                                             recv_sem=recv.at[s], device_id=peer, device_id_type=MESH))
    return own, sent, arriving


def _exchange_start(spans, ins, outs, send, recv, local):
    own, sent, _ = _scatter_copies(spans, ins, outs, send, recv, local, arrivals=False)
    for cp in own + sent:
        cp.start()


def _exchange_wait(spans, ins, outs, send, recv, local):
    own, sent, arriving = _scatter_copies(spans, ins, outs, send, recv, local, arrivals=True)
    for cp in sent:
        cp.wait_send()
    for cp in arriving:
        cp.wait_recv()
    for cp in own:
        cp.wait()


def _two_level(phase, ins, outs, send, recv, local):
    n = len(ins)
    x, y, c = lax.axis_index("x"), lax.axis_index("y"), lax.axis_index("c")
    here, sibling = (x, y, c), (x, y, 1 - c)
    chips = [(1 - x, y), (x, 1 - y), (1 - x, 1 - y)]

    def slot(px, py, pc):
        return 4 * px + 2 * py + pc

    def copy(i, k, block, to, src=None):
        return pltpu.make_async_remote_copy(src_ref=outs[i].at[block] if src is None else src, dst_ref=outs[i].at[block],
                                            send_sem=send.at[7 * i + k], recv_sem=recv.at[7 * i + k],
                                            device_id=to, device_id_type=MESH)

    me = slot(x, y, c)
    if phase == "pass":
        for i in range(n):
            for j, chip in enumerate(chips):
                copy(i, 1 + j, slot(*chip, c), here).wait_recv()
                copy(i, 4 + j, slot(*chip, c), sibling).start()
        return
    own = [pltpu.make_async_copy(ins[i], outs[i].at[me], local.at[i]) for i in range(n)]
    first = [copy(i, 0, me, sibling, src=ins[i]) for i in range(n)]
    first += [copy(i, 1 + j, me, (*chip, c), src=ins[i]) for i in range(n) for j, chip in enumerate(chips)]
    if phase == "start":
        for cp in own + first:
            cp.start()
        return
    for i in range(n):
        copy(i, 0, slot(x, y, 1 - c), here).wait_recv()
        for j, chip in enumerate(chips):
            copy(i, 4 + j, slot(*chip, 1 - c), here).wait_recv()
    passed = [copy(i, 4 + j, slot(*chip, c), sibling) for i in range(n) for j, chip in enumerate(chips)]
    for cp in first + passed:
        cp.wait_send()
    for cp in own:
        cp.wait()


def _gather_two_level(tensors, name):
    n = len(tensors)

    def body(*refs):
        for phase in ("start", "pass", "wait"):
            _two_level(phase, refs[:n], refs[n:2 * n], *refs[2 * n:])

    hbm = pl.BlockSpec(memory_space=pl.ANY)
    return pl.pallas_call(body, in_specs=[hbm] * n, out_specs=[hbm] * n,
                          out_shape=[SDS((NDEV,) + t.shape, t.dtype) for t in tensors],
                          scratch_shapes=_exchange_sems(n), name=name)(*tensors)


def _all_reduce_small(v, name):
    rows = v.shape[0]

    def body(v_ref, out_ref, land, send, recv):
        me = _my_index()
        land[me] = v_ref[...]
        remote = []
        for k in range(1, NDEV):
            peer, _ = _peer(k)
            cp = pltpu.make_async_remote_copy(src_ref=v_ref, dst_ref=land.at[me], send_sem=send.at[k - 1],
                                              recv_sem=recv.at[k - 1], device_id=peer, device_id_type=MESH)
            cp.start()
            remote.append(cp)
        for cp in remote:
            cp.wait_send()
        for k in range(1, NDEV):
            peer, pidx = _peer(k)
            pltpu.make_async_remote_copy(src_ref=v_ref, dst_ref=land.at[pidx], send_sem=send.at[k - 1],
                                         recv_sem=recv.at[k - 1], device_id=peer, device_id_type=MESH).wait_recv()
        total = land[0]
        for d in range(1, NDEV):
            total = total + land[d]
        out_ref[...] = total

    vm = pl.BlockSpec(memory_space=pltpu.VMEM)
    return pl.pallas_call(
        body, in_specs=[vm], out_specs=vm, out_shape=SDS((rows, 128), F32),
        scratch_shapes=[pltpu.VMEM((NDEV, rows, 128), F32), pltpu.SemaphoreType.DMA((NDEV - 1,)),
                        pltpu.SemaphoreType.DMA((NDEV - 1,))],
        name=name)(v)


def _adamw_math(w, g, m, v):
    m = ADAM_B1 * m + (1.0 - ADAM_B1) * g
    v = ADAM_B2 * v + (1.0 - ADAM_B2) * (g * g)
    m_hat = m / (1.0 - ADAM_B1 ** ADAM_STEP)
    v_hat = v / (1.0 - ADAM_B2 ** ADAM_STEP)
    delta = -ADAM_LR * (m_hat / (jnp.sqrt(v_hat) + ADAM_EPS) + ADAM_WD * w)
    return delta, m, v


def _row_tile(rows, cols):
    tr = rows
    while tr * cols * 4 > (1 << 20) and tr % 16 == 0:
        tr //= 2
    return tr


def _adamw(g, w, m, v, name):
    rows, cols = w.shape
    tr = _row_tile(rows, cols)

    def body(g_ref, w_ref, m_ref, v_ref, d_out, m_out, v_out):
        d, mn, vn = _adamw_math(w_ref[...], g_ref[...], m_ref[...], v_ref[...])
        d_out[...] = d
        m_out[...] = mn
        v_out[...] = vn

    sp = pl.BlockSpec((tr, cols), lambda i: (i, 0))
    return pl.pallas_call(body, grid=(rows // tr,), in_specs=[sp] * 4, out_specs=[sp] * 3,
                          out_shape=[SDS((rows, cols), F32)] * 3, name=name, compiler_params=_params(("parallel",)))(g, w, m, v)


def _reduce_adamw(parts, w, m, v, name):
    rows, cols = w.shape
    tr = _row_tile(math.gcd(*[n for _, _, n in parts]), cols)
    tiles = [n // tr for _, _, n in parts]
    starts = [r0 // tr for _, r0, _ in parts]
    assert all(r0 % tr == 0 for _, r0, _ in parts) and sum(tiles) == rows // tr

    def body(*refs):
        p_refs = refs[:len(parts)]
        w_ref, m_ref, v_ref, g_out, d_out, m_out, v_out = refs[len(parts):]
        i = pl.program_id(0)
        for h, p_ref in enumerate(p_refs):
            @pl.when((i >= starts[h]) & (i < starts[h] + tiles[h]))
            def _(p_ref=p_ref):
                g = p_ref[0].astype(F32)
                for d in range(1, NDEV):
                    g = g + p_ref[d].astype(F32)
                g_out[...] = g
                dl, mn, vn = _adamw_math(w_ref[...], g, m_ref[...], v_ref[...])
                d_out[...] = dl
                m_out[...] = mn
                v_out[...] = vn

    sp = pl.BlockSpec((tr, cols), lambda i: (i, 0))
    psp = [pl.BlockSpec((NDEV, tr, cols), lambda i, h=h: (0, jnp.clip(i, starts[h], starts[h] + tiles[h] - 1), 0))
           for h in range(len(parts))]
    return pl.pallas_call(body, grid=(rows // tr,), in_specs=psp + [sp, sp, sp], out_specs=[sp] * 4,
                          out_shape=[SDS((rows, cols), F32)] * 4, name=name,
                          compiler_params=_params(("parallel",)))(*[p for p, _, _ in parts], w, m, v)


def _pack(items):
    rows = []
    for a in items:
        a = a.reshape(-1).astype(F32)
        pad = (-a.shape[0]) % 128
        rows.append(jnp.pad(a, (0, pad)).reshape(-1, 128))
    out = jnp.concatenate(rows, axis=0)
    return jnp.pad(out, ((0, (-out.shape[0]) % 8), (0, 0)))


def _unpack(packed, shapes):
    out, r = [], 0
    for shp in shapes:
        n = math.prod(shp)
        nr = -(-n // 128)
        out.append(packed[r:r + nr].reshape(-1)[:n].reshape(shp))
        r += nr
    return out


def _ident(t):
    return (t,)


def _add(t, res):
    return (t + res,)


def kernel(x, norm_mix, norm_mlp, ssm_w_in, ssm_conv_w, ssm_conv_b, ssm_dt_bias, ssm_a_log, ssm_d, ssm_norm_w, ssm_w_out, attn_w_qkv, attn_w_o, mlp_w1, mlp_w2, final_norm, loss_target, m_norm_mix, m_norm_mlp, m_ssm_w_in, m_ssm_conv_w, m_ssm_conv_b, m_ssm_dt_bias, m_ssm_a_log, m_ssm_d, m_ssm_norm_w, m_ssm_w_out, m_attn_w_qkv, m_attn_w_o, m_mlp_w1, m_mlp_w2, m_final_norm, v_norm_mix, v_norm_mlp, v_ssm_w_in, v_ssm_conv_w, v_ssm_conv_b, v_ssm_dt_bias, v_ssm_a_log, v_ssm_d, v_ssm_norm_w, v_ssm_w_out, v_attn_w_qkv, v_attn_w_o, v_mlp_w1, v_mlp_w2, v_final_norm):
    t = x.shape[1]
    x0 = x.reshape(t, D_MODEL)
    tgt = loss_target.reshape(t, D_MODEL)
    me = _my_index()
    in_dim = D_INNER + CONV_DIM + SSM_HEADS
    in_shard = in_dim // NDEV
    zx_dim = D_INNER + CONV_DIM

    s_out, s_qkv, s_o = ssm_w_out[0].astype(BF16), attn_w_qkv[0].astype(BF16), attn_w_o[0].astype(BF16)
    s_w1, s_w2 = mlp_w1.astype(BF16), mlp_w2.astype(BF16)
    g_in, g_cw = _gather_two_level([ssm_w_in[0].astype(BF16), ssm_conv_w[0]], "gather_in_proj")
    w_in = jnp.transpose(g_in, (1, 0, 2)).reshape(D_MODEL, in_dim)
    w_z, w_x = w_in[:, :D_INNER], w_in[:, D_INNER:zx_dim]
    w_dt = jnp.pad(w_in[:, zx_dim:], ((0, 0), (0, 128 - SSM_HEADS)))
    conv_w = jnp.transpose(g_cw, (1, 0, 2)).reshape(CONV_WIDTH, CONV_DIM)
    g_w1, g_w2 = [None, None], [None, None]

    def lanes(p):
        return jnp.pad(p.reshape(SSM_GROUPS, 1, 8), ((0, 0), (0, 0), (0, 120)))

    def rows(p):
        return jnp.broadcast_to(p.reshape(SSM_HEADS, 1), (SSM_HEADS, 128))

    bias_g, bias_r = lanes(ssm_dt_bias[0]), rows(ssm_dt_bias[0])
    alog_g, alog_r = lanes(ssm_a_log[0]), rows(ssm_a_log[0])
    d_exp = jnp.repeat(ssm_d[0], SSM_HEAD_DIM).reshape(1, D_INNER)
    norm_w = ssm_norm_w

    def relu2(tot):
        r = jnp.maximum(tot, 0.0)
        return r, r * r

    def mlp_fwd(xin, layer, tag, down_carry):
        h = _rmsnorm_fwd(xin, norm_mlp[layer:layer + 1], f"norm_mlp{tag}")
        r, a, w2 = _mm_fwd(h, g_w1[layer], "cols", D_MODEL, D_FF, epi=relu2, outs=(BF16, BF16),
                           carry=("gather", [s_w2[layer]]), name=f"mlp_up{tag}")
        g_w2[layer] = w2.reshape(D_FF, D_MODEL)
        xout, *got = _mm_fwd(a, g_w2[layer], "plain", D_FF, D_MODEL, epi=_add, outs=(F32,), extras=(xin,),
                             carry=("gather", down_carry) if down_carry else None, name=f"mlp_down{tag}")
        return h, r, a, xout, got

    h0 = _rmsnorm_fwd(x0, norm_mix[0:1], "norm_mix0")
    z, g_out = _mm_fwd(h0, w_z, "plain", D_MODEL, D_INNER, epi=_ident, outs=(F32,), carry=("gather", [s_out]), name="ssm_in_z")
    xpre, g_w1[0] = _mm_fwd(h0, w_x, "plain", D_MODEL, CONV_DIM, epi=_ident, outs=(F32,), carry=("gather", [s_w1[0]]),
                            name="ssm_in_x")
    dtp, = _mm_fwd(h0, w_dt, "plain", D_MODEL, 128, epi=_ident, outs=(F32,), name="ssm_in_dt")
    xbc = _conv_fwd(xpre, conv_w, ssm_conv_b, "conv_fwd")
    dtp64 = dtp[:, :SSM_HEADS]
    dtp_g = jnp.pad(jnp.transpose(dtp64.reshape(t, SSM_GROUPS, 8), (1, 0, 2)), ((0, 0), (0, 0), (0, 120)))
    dtp_t = jnp.transpose(dtp64)
    y_ssd, yn, states = _ssd_fwd(xbc, z, dtp_g, dtp_t, bias_g, bias_r, alog_g, alog_r, d_exp, norm_w, "ssd_fwd")
    g_out = g_out.reshape(D_INNER, D_MODEL)
    x1, g_o = _mm_fwd(yn, g_out, "plain", D_INNER, D_MODEL, epi=_add, outs=(F32,), extras=(x0,), carry=("gather", [s_o]),
                      name="ssm_out")
    h1, r1, a1, x2, (g_qkv,) = mlp_fwd(x1, 0, "0", [s_qkv])

    h2 = _rmsnorm_fwd(x2, norm_mix[1:2], "norm_mix1")
    w_qkv = _pair_major(jnp.transpose(g_qkv, (1, 0, 2)).reshape(D_MODEL, QKV_DIM))
    qkv, g_w1[1] = _mm_fwd(h2, w_qkv, "plain", D_MODEL, QKV_DIM, epi=_ident, outs=(F32,),
                           carry=("gather", [s_w1[1]]), name="attn_qkv")
    att = [_attn_fwd(qkv, g, f"attn_fwd{g}") for g in range(3)]
    os_, ls_ = [a[0] for a in att], [a[1] for a in att]
    o_mix = _combine_fwd(os_, ls_, "attn_combine")
    x3, = _mm_fwd(o_mix, g_o, "cols", ATTN_W, D_MODEL, epi=_add, outs=(F32,), extras=(x2,), tn=D_MODEL // NDEV, name="attn_out")
    h3, r3, a3, x4, _ = mlp_fwd(x3, 1, "1", None)

    dx4, dx4b, loss_acc, d_final = _loss_head(x4, tgt, final_norm.reshape(1, D_MODEL), "loss_head")

    def mlp_bwd(xin, h, r, a, dxo, dxob, layer, tag):
        du, = _mm_dx(dxob, g_w2[layer], "plain", D_FF, D_MODEL, epi=lambda tot, rr: (tot * (2.0 * rr.astype(F32)),),
                     outs=(BF16,), extras=(r,), name=f"mlp_down_dx{tag}")
        dw2, = _mm_dw(a, dxob, "plain", name=f"mlp_down_dw{tag}")
        dw2 = dw2.reshape(NDEV, D_FF // NDEV, D_MODEL)
        dw1, p_dw2 = _mm_dw(h, du, "cols", carry=("scatter", [dw2], [W2_SPANS[0]]), name=f"mlp_up_dw{tag}")
        dh, p_dw1 = _mm_dx(du, g_w1[layer], "cols", D_MODEL, D_FF, epi=_ident, outs=(F32,),
                           carry=("scatter", [dw1], [W1_SPANS[0]]), name=f"mlp_up_dx{tag}")
        dxi, dxib, dg = _rmsnorm_bwd(xin, norm_mlp[layer:layer + 1], dh, dxo, f"norm_mlp_bwd{tag}")
        return dxi, dxib, dg, (dw1, dw2), (p_dw1, p_dw2)

    W1_SPANS = [(0, 7 * D_MODEL // 8), (7 * D_MODEL // 8, D_MODEL // 8)]
    W2_SPANS = [(0, 7 * D_FF // NDEV // 8), (7 * D_FF // NDEV // 8, D_FF // NDEV // 8)]

    def mlp_parts(first, rest):
        return [[(first[0],) + W1_SPANS[0], (rest[0],) + W1_SPANS[1]], [(first[1],) + W2_SPANS[0], (rest[1],) + W2_SPANS[1]]]

    dx3, dx3b, dg_mlp1, left1, got1 = mlp_bwd(x3, h3, r3, a3, dx4, dx4b, 1, "1")

    dw_o, *rest1 = _mm_dw(o_mix, dx3b, "cols", tn=D_MODEL // NDEV, carry=("scatter", list(left1), [W1_SPANS[1], W2_SPANS[1]]),
                          name="attn_out_dw")
    p_w1_1, p_w2_1 = mlp_parts(got1, rest1)
    do, p_o = _mm_dx(dx3b, g_o, "cols", ATTN_W, D_MODEL, epi=_ident, outs=(F32,), tk=D_MODEL // NDEV,
                     carry=("scatter", [dw_o]), name="attn_out_dx")
    cb = _combine_bwd(do, os_, ls_, "attn_combine_bwd")
    dos, dvecs = cb[:3], cb[3:]
    dqkv = None
    for g in range(3):
        dqkv = _attn_bwd(qkv, dos[g], ls_[g], dvecs[g], dqkv, g, f"attn_bwd{g}")
    dw_qkv, = _mm_dw(h2, dqkv, "plain", tk=1024, name="attn_qkv_dw")
    dw_qkv = jnp.transpose(_pair_major(dw_qkv, inverse=True).reshape(D_MODEL, NDEV, QKV_DIM // NDEV), (1, 0, 2))
    dh2, p_qkv = _mm_dx(dqkv, w_qkv, "plain", D_MODEL, QKV_DIM, epi=_ident, outs=(F32,), tk=1536,
                        carry=("scatter", [dw_qkv]), name="attn_qkv_dx")
    dx2, dx2b, dg_mix1 = _rmsnorm_bwd(x2, norm_mix[1:2], dh2, dx3, "norm_mix_bwd1")

    dx1, dx1b, dg_mlp0, left0, got0 = mlp_bwd(x1, h1, r1, a1, dx2, dx2b, 0, "0")

    dw_out, *rest0 = _mm_dw(yn, dx1b, "plain", carry=("scatter", list(left0), [W1_SPANS[1], W2_SPANS[1]]), name="ssm_out_dw")
    p_w1_0, p_w2_0 = mlp_parts(got0, rest0)
    dw_out = dw_out.reshape(NDEV, D_INNER // NDEV, D_MODEL)
    dyn, p_out = _mm_dx(dx1b, g_out, "plain", D_INNER, D_MODEL, epi=_ident, outs=(F32,), carry=("scatter", [dw_out]),
                        name="ssm_out_dx")
    dxs, d_b, d_c, dz, ddtp_g, hsums, csums = _ssd_bwd(dyn, y_ssd, xbc, z, states, dtp_g, dtp_t, bias_g, bias_r,
                                                       alog_g, alog_r, d_exp, norm_w, "ssd_bwd")
    du, sums = None, []
    for part, col0, tag in ((dxs, 0, "x"), (d_b, D_INNER, "b"), (d_c, D_INNER + SSM_GROUPS * SSM_STATE, "c")):
        du, s = _conv_bwd(xpre, conv_w, ssm_conv_b, part, du, col0, f"conv_bwd_{tag}")
        sums.append(s)
    conv_sums = jnp.concatenate(sums, axis=1)
    ddtp = jnp.transpose(ddtp_g[:, :, :8], (1, 0, 2)).reshape(t, SSM_HEADS)
    ddtp = jnp.pad(ddtp, ((0, 0), (0, 128 - SSM_HEADS))).astype(BF16)
    dw_z, = _mm_dw(h0, dz, "plain", name="ssm_in_z_dw")
    dw_x, = _mm_dw(h0, du, "plain", name="ssm_in_x_dw")
    dw_dt, = _mm_dw(h0, ddtp, "plain", name="ssm_in_dt_dw")
    half = D_MODEL // 2
    dw_in = jnp.concatenate([dw_z, dw_x, dw_dt[:, :SSM_HEADS]], axis=1)
    dw_in = jnp.transpose(dw_in.reshape(D_MODEL, NDEV, in_shard), (1, 0, 2))
    dh0, = _mm_dx(ddtp, w_dt, "plain", D_MODEL, 128, epi=_ident, outs=(F32,), name="ssm_in_dt_dx")
    dh0, p_in_a = _mm_dx(dz, w_z, "plain", D_MODEL, D_INNER, epi=_add, outs=(F32,), extras=(dh0,),
                         carry=("scatter", [dw_in], [(0, half)]), name="ssm_in_z_dx")
    dh0, p_in_b = _mm_dx(du, w_x, "plain", D_MODEL, CONV_DIM, epi=_add, outs=(F32,), extras=(dh0,),
                         carry=("scatter", [dw_in], [(half, half)]), name="ssm_in_x_dx")
    dx0, _, dg_mix0 = _rmsnorm_bwd(x0, norm_mix[0:1], dh0, dx1, "norm_mix_bwd0")

    def whole(p):
        return [(p, 0, p.shape[1])]

    parts = [[(p_in_a, 0, half), (p_in_b, half, half)], whole(p_out), whole(p_qkv), whole(p_o),
             p_w1_0, p_w1_1, p_w2_0, p_w2_1]

    def own(w, mm, vv):
        shp = w.shape
        f = lambda a: a.reshape(-1, shp[-1])
        return f(w), f(mm), f(vv), shp

    big = {}
    for key, part, (w, mm, vv) in (
            ("ssm_w_in", parts[0], (ssm_w_in, m_ssm_w_in, v_ssm_w_in)),
            ("ssm_w_out", parts[1], (ssm_w_out, m_ssm_w_out, v_ssm_w_out)),
            ("attn_w_qkv", parts[2], (attn_w_qkv, m_attn_w_qkv, v_attn_w_qkv)),
            ("attn_w_o", parts[3], (attn_w_o, m_attn_w_o, v_attn_w_o))):
        w2, m2, v2, shp = own(w, mm, vv)
        res = _reduce_adamw(part, w2, m2, v2, f"adamw_{key}")
        big[key] = [r.reshape(shp) for r in res]
    for key, pa, pb, (w, mm, vv) in (("mlp_w1", parts[4], parts[5], (mlp_w1, m_mlp_w1, v_mlp_w1)),
                                     ("mlp_w2", parts[6], parts[7], (mlp_w2, m_mlp_w2, v_mlp_w2))):
        res = [_reduce_adamw(p, w[l], mm[l], vv[l], f"adamw_{key}_{l}") for l, p in enumerate((pa, pb))]
        big[key] = [jnp.stack([res[0][i], res[1][i]], axis=0) for i in range(4)]

    d_norm_mix = jnp.concatenate([dg_mix0, dg_mix1], axis=0)
    d_norm_mlp = jnp.concatenate([dg_mlp0, dg_mlp1], axis=0)
    d_conv_b = conv_sums[4:5]
    d_conv_w = conv_sums[0:4]
    head = hsums[:, :, :8]
    d_dt_bias, d_a_log, d_d = (head[:, k, :].reshape(1, SSM_HEADS) for k in range(3))
    d_ssm_norm = csums[0:1]
    small = [d_norm_mix, d_norm_mlp, d_conv_b, d_dt_bias, d_a_log, d_d, d_ssm_norm, d_final, d_conv_w, loss_acc[0:1, 0:1]]
    shapes = [a.shape for a in small]
    summed = _unpack(_all_reduce_small(_pack(small), "reduce_small"), shapes)
    g_conv_w_full = summed[8]
    loss = summed[9].reshape(())
    g_conv_w = lax.dynamic_slice(g_conv_w_full, (0, me * (CONV_DIM // NDEV)), (CONV_WIDTH, CONV_DIM // NDEV))

    small_names = ["norm_mix", "norm_mlp", "ssm_conv_b", "ssm_dt_bias", "ssm_a_log", "ssm_d", "ssm_norm_w", "final_norm"]
    small_w = [norm_mix, norm_mlp, ssm_conv_b, ssm_dt_bias, ssm_a_log, ssm_d, ssm_norm_w, final_norm, ssm_conv_w]
    small_m = [m_norm_mix, m_norm_mlp, m_ssm_conv_b, m_ssm_dt_bias, m_ssm_a_log, m_ssm_d, m_ssm_norm_w, m_final_norm, m_ssm_conv_w]
    small_v = [v_norm_mix, v_norm_mlp, v_ssm_conv_b, v_ssm_dt_bias, v_ssm_a_log, v_ssm_d, v_ssm_norm_w, v_final_norm, v_ssm_conv_w]
    small_g = [summed[i].reshape(small_w[i].shape) for i in range(8)] + [g_conv_w.reshape(ssm_conv_w.shape)]
    wshapes = [a.shape for a in small_w]
    sd, sm, sv = _adamw(_pack(small_g), _pack(small_w), _pack(small_m), _pack(small_v), "adamw_small")
    sd, sm, sv = _unpack(sd, wshapes), _unpack(sm, wshapes), _unpack(sv, wshapes)
    res = {n: (small_g[i], sd[i], sm[i], sv[i]) for i, n in enumerate(small_names + ["ssm_conv_w"])}
    for n in big:
        res[n] = tuple(big[n])

    order = ["norm_mix", "norm_mlp", "ssm_w_in", "ssm_conv_w", "ssm_conv_b", "ssm_dt_bias", "ssm_a_log", "ssm_d",
             "ssm_norm_w", "ssm_w_out", "attn_w_qkv", "attn_w_o", "mlp_w1", "mlp_w2", "final_norm"]
    outs = [loss, dx0.reshape(x.shape)]
    for kind in range(4):
        outs += [res[n][kind] for n in order]
    return tuple(outs)
```

```python
import math

import jax
import jax.numpy as jnp
from jax import lax
from jax.experimental import pallas as pl
from jax.experimental.pallas import tpu as pltpu

F32, BF16 = jnp.float32, jnp.bfloat16
SDS = jax.ShapeDtypeStruct
MESH = pl.DeviceIdType.MESH

NDEV = 8
D_MODEL = 2048
D_INNER = 4096
SSM_HEADS = 64
SSM_HEAD_DIM = 64
SSM_GROUPS = 8
SSM_STATE = 128
CHUNK = 128
CONV_DIM = 6144
CONV_WIDTH = 4
GROUP_W = D_INNER // SSM_GROUPS
ATTN_GROUPS = ((128, 1), (512, 4), (2048, 16))
ATTN_W = 1024
QKV_DIM = 9216
D_FF = 8192
EPS = 1e-5
ADAM_LR, ADAM_B1, ADAM_B2, ADAM_EPS, ADAM_WD, ADAM_STEP = 0.001, 0.9, 0.999, 1e-08, 0.01, 10

VMEM_LIMIT = 48 * 1024 * 1024


def _params(sem):
    return pltpu.CompilerParams(dimension_semantics=sem, vmem_limit_bytes=VMEM_LIMIT)


def _sigmoid(v):
    return 1.0 / (1.0 + jnp.exp(-v))


def _softplus(v):
    return jnp.maximum(v, 0.0) + jnp.log1p(jnp.exp(-jnp.abs(v)))


def _wspec(layout, kw, nw, tr, tc, sel):
    if layout == "plain":
        return pl.BlockSpec((tr, tc), lambda *g: sel(*g))
    per = (nw // NDEV) // tc
    return pl.BlockSpec((None, tr, tc), lambda *g: (sel(*g)[1] // per, sel(*g)[0], sel(*g)[1] % per))


def _wshape(layout, kw, nw):
    return (kw, nw) if layout == "plain" else (NDEV, kw, nw // NDEV)


def _mm_call(name, grid, in_specs, out_specs, out_shape, dims, n_extra, epi, tm, tn, carry):
    nk = grid[2]
    steps = grid[0] * grid[1] * nk
    kind, moved = carry[:2] if carry else (None, ())
    spans = carry[2] if carry and len(carry) > 2 else [None] * len(moved)
    nc = len(moved)
    n_out = len(out_shape)

    def body(*refs):
        a_ref, b_ref = refs[0], refs[1]
        extra = refs[2:2 + n_extra]
        c_in = refs[2 + n_extra:2 + n_extra + nc]
        outs = refs[2 + n_extra + nc:2 + n_extra + nc + n_out]
        c_out = refs[2 + n_extra + nc + n_out:2 + n_extra + 2 * nc + n_out]
        acc = refs[2 + n_extra + 2 * nc + n_out]
        sems = refs[3 + n_extra + 2 * nc + n_out:]
        i, j, k = pl.program_id(0), pl.program_id(1), pl.program_id(2)
        step = (i * grid[1] + j) * nk + k
        if nc:
            @pl.when(step == 0)
            def _():
                if kind == "gather":
                    _two_level("start", c_in, c_out, *sems)
                else:
                    _exchange_start(spans, c_in, c_out, *sems)

            if kind == "gather":
                @pl.when(step == (2 * steps) // 3)
                def _():
                    _two_level("pass", c_in, c_out, *sems)

        d = lax.dot_general(a_ref[...].astype(BF16), b_ref[...].astype(BF16), (dims, ((), ())), preferred_element_type=F32)

        def finish(total):
            vals = epi(total, *[e[...] for e in extra])
            for o, v in zip(outs, vals):
                o[...] = v.astype(o.dtype)

        if nk == 1:
            finish(d)
        else:
            @pl.when(k == 0)
            def _():
                acc[...] = d

            @pl.when(jnp.logical_and(k > 0, k < nk - 1))
            def _():
                acc[...] += d

            @pl.when(k == nk - 1)
            def _():
                finish(acc[...] + d)

        if nc:
            @pl.when(step == steps - 1)
            def _():
                if kind == "gather":
                    _two_level("wait", c_in, c_out, *sems)
                else:
                    _exchange_wait(spans, c_in, c_out, *sems)

    hbm = pl.BlockSpec(memory_space=pl.ANY)
    scratch = [pltpu.VMEM((tm, tn), F32)] + (_exchange_sems(nc) if nc else [])
    c_shape = [SDS(((NDEV,) + t.shape) if kind == "gather" else t.shape, t.dtype) for t in moved]
    sem = ("arbitrary",) * 3 if nc else ("parallel", "parallel", "arbitrary")
    return pl.pallas_call(
        body, grid=grid, in_specs=in_specs + [hbm] * nc, out_specs=out_specs + [hbm] * nc,
        out_shape=out_shape + c_shape, scratch_shapes=scratch, name=name, compiler_params=_params(sem))


def _mm_fwd(a, w, layout, kw, nw, *, epi, outs, extras=(), tm=1024, tn=1024, tk=2048, carry=None, name):
    m = a.shape[0]
    tm, tn, tk = min(tm, m), min(tn, nw), min(tk, kw)
    grid = (m // tm, nw // tn, kw // tk)
    o_spec = pl.BlockSpec((tm, tn), lambda i, j, k: (i, j))
    in_specs = [pl.BlockSpec((tm, tk), lambda i, j, k: (i, k)), _wspec(layout, kw, nw, tk, tn, lambda i, j, k: (k, j))]
    in_specs += [o_spec] * len(extras)
    call = _mm_call(name, grid, in_specs, [o_spec] * len(outs), [SDS((m, nw), dt) for dt in outs], ((1,), (0,)),
                    len(extras), epi, tm, tn, carry)
    return call(a, w, *extras, *(carry[1] if carry else ()))


def _mm_dx(g, w, layout, kw, nw, *, epi, outs, extras=(), tm=1024, tn=1024, tk=2048, carry=None, name):
    m = g.shape[0]
    tm, tn, tk = min(tm, m), min(tn, kw), min(tk, nw if layout == "plain" else nw // NDEV)
    grid = (m // tm, kw // tn, nw // tk)
    o_spec = pl.BlockSpec((tm, tn), lambda i, j, k: (i, j))
    in_specs = [pl.BlockSpec((tm, tk), lambda i, j, k: (i, k)), _wspec(layout, kw, nw, tn, tk, lambda i, j, k: (j, k))]
    in_specs += [o_spec] * len(extras)
    call = _mm_call(name, grid, in_specs, [o_spec] * len(outs), [SDS((m, kw), dt) for dt in outs], ((1,), (1,)),
                    len(extras), epi, tm, tn, carry)
    return call(g, w, *extras, *(carry[1] if carry else ()))


def _mm_dw(a, g, layout, *, tm=1024, tn=1024, tk=2048, carry=None, name):
    m, kw = a.shape
    nw = g.shape[1]
    tm, tn, tk = min(tm, kw), min(tn, nw if layout == "plain" else nw // NDEV), min(tk, m)
    grid = (kw // tm, nw // tn, m // tk)
    in_specs = [pl.BlockSpec((tk, tm), lambda i, j, k: (k, i)), pl.BlockSpec((tk, tn), lambda i, j, k: (k, j))]
    o_spec = _wspec(layout, kw, nw, tm, tn, lambda i, j, k: (i, j))
    call = _mm_call(name, grid, in_specs, [o_spec], [SDS(_wshape(layout, kw, nw), BF16)], ((0,), (0,)),
                    0, lambda t: (t,), tm, tn, carry)
    return call(a, g, *(carry[1] if carry else ()))


ROW_TILE = 256


def _rmsnorm_fwd(x, g, name):
    t, d = x.shape

    def body(x_ref, g_ref, h_ref):
        xv = x_ref[...]
        r = lax.rsqrt(jnp.mean(xv * xv, axis=-1, keepdims=True) + EPS)
        h_ref[...] = (xv * r * g_ref[...]).astype(BF16)

    row = pl.BlockSpec((ROW_TILE, d), lambda i: (i, 0))
    vec = pl.BlockSpec((1, d), lambda i: (0, 0))
    return pl.pallas_call(body, grid=(t // ROW_TILE,), in_specs=[row, vec], out_specs=row, out_shape=SDS((t, d), BF16),
                          name=name, compiler_params=_params(("parallel",)))(x, g)


def _rmsnorm_bwd(x, g, dh, dres, name):
    t, d = x.shape

    def body(x_ref, g_ref, dh_ref, dres_ref, dx_ref, dxb_ref, dg_ref):
        xv = x_ref[...]
        r = lax.rsqrt(jnp.mean(xv * xv, axis=-1, keepdims=True) + EPS)
        xh = xv * r
        dhv = dh_ref[...]
        gd = dhv * g_ref[...]
        dx = dres_ref[...] + r * (gd - xh * jnp.mean(gd * xh, axis=-1, keepdims=True))
        dx_ref[...] = dx
        dxb_ref[...] = dx.astype(BF16)
        part = jnp.sum(dhv * xh, axis=0, keepdims=True)

        @pl.when(pl.program_id(0) == 0)
        def _():
            dg_ref[...] = part

        @pl.when(pl.program_id(0) > 0)
        def _():
            dg_ref[...] += part

    row = pl.BlockSpec((ROW_TILE, d), lambda i: (i, 0))
    vec = pl.BlockSpec((1, d), lambda i: (0, 0))
    return pl.pallas_call(body, grid=(t // ROW_TILE,), in_specs=[row, vec, row, row], out_specs=[row, row, vec],
                          out_shape=[SDS((t, d), F32), SDS((t, d), BF16), SDS((1, d), F32)], name=name,
                          compiler_params=_params(("arbitrary",)))(x, g, dh, dres)


def _loss_head(x, tgt, g, name):
    t, d = x.shape

    def body(x_ref, t_ref, g_ref, dx_ref, dxb_ref, loss_ref, dg_ref):
        xv = x_ref[...]
        r = lax.rsqrt(jnp.mean(xv * xv, axis=-1, keepdims=True) + EPS)
        xh = xv * r
        gv = g_ref[...]
        err = xh * gv - t_ref[...]
        part_loss = 0.5 * jnp.sum(jnp.mean(err * err, axis=-1, keepdims=True), axis=0, keepdims=True)
        dy = err * (1.0 / d)
        gd = dy * gv
        dx = r * (gd - xh * jnp.mean(gd * xh, axis=-1, keepdims=True))
        dx_ref[...] = dx
        dxb_ref[...] = dx.astype(BF16)
        part_g = jnp.sum(dy * xh, axis=0, keepdims=True)
        part_l = jnp.broadcast_to(part_loss, (8, 128))

        @pl.when(pl.program_id(0) == 0)
        def _():
            dg_ref[...] = part_g
            loss_ref[...] = part_l

        @pl.when(pl.program_id(0) > 0)
        def _():
            dg_ref[...] += part_g
            loss_ref[...] += part_l

    row = pl.BlockSpec((ROW_TILE, d), lambda i: (i, 0))
    vec = pl.BlockSpec((1, d), lambda i: (0, 0))
    sc = pl.BlockSpec((8, 128), lambda i: (0, 0))
    return pl.pallas_call(body, grid=(t // ROW_TILE,), in_specs=[row, row, vec], out_specs=[row, row, sc, vec],
                          out_shape=[SDS((t, d), F32), SDS((t, d), BF16), SDS((8, 128), F32), SDS((1, d), F32)], name=name,
                          compiler_params=_params(("arbitrary",)))(x, tgt, g)


CONV_ROWS = 256
CONV_COLS = 2048


def _shift_down(cur, prev8, k):
    sh = pltpu.roll(cur, k, axis=0)
    ph = pltpu.roll(prev8, k, axis=0)
    rid = lax.broadcasted_iota(jnp.int32, ph.shape, 0)
    head = jnp.where(rid < k, ph, sh[0:8])
    return jnp.concatenate([head, sh[8:]], axis=0)


def _shift_up(cur, next8, k):
    n = cur.shape[0]
    sh = pltpu.roll(cur, n - k, axis=0)
    nh = pltpu.roll(next8, 8 - k, axis=0)
    rid = lax.broadcasted_iota(jnp.int32, nh.shape, 0)
    tail = jnp.where(rid >= 8 - k, nh, sh[n - 8:])
    return jnp.concatenate([sh[:n - 8], tail], axis=0)


def _conv_pre(cur, prev8, w, b):
    acc = w[3:4, :] * cur + b
    for k in range(1, CONV_WIDTH):
        acc = acc + w[3 - k:4 - k, :] * _shift_down(cur, prev8, k)
    return acc


def _conv_fwd(u, w, b, name):
    t, c = u.shape
    per = CONV_ROWS // 8

    def body(u_ref, p_ref, w_ref, b_ref, o_ref):
        prev8 = jnp.where(pl.program_id(1) == 0, 0.0, p_ref[...])
        pre = _conv_pre(u_ref[...], prev8, w_ref[...], b_ref[...])
        o_ref[...] = pre * _sigmoid(pre)

    cur = pl.BlockSpec((CONV_ROWS, CONV_COLS), lambda j, i: (i, j))
    prev = pl.BlockSpec((8, CONV_COLS), lambda j, i: (jnp.maximum(i * per - 1, 0), j))
    wsp = pl.BlockSpec((CONV_WIDTH, CONV_COLS), lambda j, i: (0, j))
    bsp = pl.BlockSpec((1, CONV_COLS), lambda j, i: (0, j))
    return pl.pallas_call(body, grid=(c // CONV_COLS, t // CONV_ROWS), in_specs=[cur, prev, wsp, bsp], out_specs=cur,
                          out_shape=SDS((t, c), F32), name=name, compiler_params=_params(("parallel", "parallel")))(u, u, w, b)


def _conv_bwd(u, w, b, dout, du_prev, col0, name):
    t, c = u.shape
    per = CONV_ROWS // 8
    last = t // CONV_ROWS - 1

    def dsilu(pre, d):
        s = _sigmoid(pre)
        return d * (s * (1.0 + pre * (1.0 - s)))

    def body(u_ref, p_ref, n_ref, w_ref, b_ref, d_ref, dn_ref, *rest):
        du_ref, dw_ref = rest[-2:]
        i = pl.program_id(1)
        cur = u_ref[...]
        wv, bv = w_ref[...], b_ref[...]
        prev8 = jnp.where(i == 0, 0.0, p_ref[...])
        dpre = dsilu(_conv_pre(cur, prev8, wv, bv), d_ref[...])
        nxt, tail = n_ref[...], cur[CONV_ROWS - 8:]
        rid = lax.broadcasted_iota(jnp.int32, nxt.shape, 0)
        pre_n = wv[3:4, :] * nxt + bv
        for k in range(1, CONV_WIDTH):
            pre_n = pre_n + wv[3 - k:4 - k, :] * jnp.where(rid < k, pltpu.roll(tail, k, axis=0), pltpu.roll(nxt, k, axis=0))
        dpre_n = jnp.where(i == last, 0.0, dsilu(pre_n, dn_ref[...]))
        du = wv[3:4, :] * dpre
        for k in range(1, CONV_WIDTH):
            du = du + wv[3 - k:4 - k, :] * _shift_up(dpre, dpre_n, k)
        du_ref[...] = du.astype(BF16)
        rows = [jnp.sum(dpre * _shift_down(cur, prev8, 3 - k), axis=0, keepdims=True) for k in range(3)]
        rows.append(jnp.sum(dpre * cur, axis=0, keepdims=True))
        rows.append(jnp.sum(dpre, axis=0, keepdims=True))
        part = jnp.concatenate(rows + [jnp.zeros((3, cur.shape[1]), F32)], axis=0)

        @pl.when(i == 0)
        def _():
            dw_ref[...] = part

        @pl.when(i > 0)
        def _():
            dw_ref[...] += part

    width = dout.shape[1]
    tc = min(CONV_COLS, width)
    ob = col0 // tc

    def at(rows, here):
        off = ob if here else 0
        if rows == CONV_ROWS:
            return pl.BlockSpec((rows, tc), lambda j, i: (i, off + j))
        if rows == -8:
            return pl.BlockSpec((8, tc), lambda j, i: (jnp.maximum(i * per - 1, 0), off + j))
        if rows == 8:
            return pl.BlockSpec((8, tc), lambda j, i: (jnp.minimum((i + 1) * per, t // 8 - 1), off + j))
        return pl.BlockSpec((rows, tc), lambda j, i: (0, off + j))

    in_specs = [at(CONV_ROWS, True), at(-8, True), at(8, True), at(CONV_WIDTH, True), at(1, True),
                at(CONV_ROWS, False), at(8, False)]
    args = [u, u, u, w, b, dout, dout]
    aliases = {}
    if du_prev is not None:
        in_specs.append(pl.BlockSpec(memory_space=pl.ANY))
        args.append(du_prev)
        aliases = {len(args) - 1: 0}
    acc = pl.BlockSpec((8, tc), lambda j, i: (0, j))
    return pl.pallas_call(body, grid=(width // tc, t // CONV_ROWS), in_specs=in_specs,
                          out_specs=[at(CONV_ROWS, True), acc], out_shape=[SDS((t, c), BF16), SDS((8, width), F32)],
                          input_output_aliases=aliases, name=name,
                          compiler_params=_params(("parallel", "arbitrary")))(*args)


def _pieces(v, n):
    out, rest = [], v
    for _ in range(n):
        p = rest.astype(BF16)
        out.append(p)
        rest = rest - p.astype(F32)
    return out


def _head_expand(v):
    r = lax.broadcasted_iota(jnp.int32, (3 * 128, GROUP_W), 0) % 128
    c = lax.broadcasted_iota(jnp.int32, (3 * 128, GROUP_W), 1)
    return _dot(jnp.concatenate(_pieces(v, 3), axis=1), (c // SSM_HEAD_DIM == r).astype(BF16))


def _head_sum(vs):
    r = lax.broadcasted_iota(jnp.int32, (2 * GROUP_W, 128), 0) % GROUP_W
    c = lax.broadcasted_iota(jnp.int32, (2 * GROUP_W, 128), 1)
    stacked = jnp.concatenate([jnp.concatenate(_pieces(v, 2), axis=1) for v in vs], axis=0)
    out = _dot(stacked, (r // SSM_HEAD_DIM == c).astype(BF16))
    res, at = [], 0
    for v in vs:
        res.append(out[at:at + v.shape[0]])
        at += v.shape[0]
    return res


def _tri_left(tri, v):
    r = _dot(tri.astype(BF16), jnp.concatenate(_pieces(v, 3), axis=1))
    return r[:, 0:128] + r[:, 128:256] + r[:, 256:384]


def _dot(a, b):
    return jnp.dot(a, b, preferred_element_type=F32)


def _dot_nt(a, b):
    return lax.dot_general(a, b, (((1,), (1,)), ((), ())), preferred_element_type=F32)


def _dot_tn(a, b):
    return lax.dot_general(a, b, (((0,), (0,)), ((), ())), preferred_element_type=F32)


def _ssd_common(dtp, dtpt, bias, biasr, alog, alogr):
    li = lax.broadcasted_iota(jnp.int32, (CHUNK, CHUNK), 0)
    si = lax.broadcasted_iota(jnp.int32, (CHUNK, CHUNK), 1)
    lower, upper = (li >= si), (li <= si)
    dt = _softplus(dtp + bias)
    a_neg = -jnp.exp(alog)
    cs = _tri_left(lower, dt * a_neg)
    dtr = _softplus(dtpt + biasr)
    ar = jnp.concatenate([dtr * (-jnp.exp(alogr)), jnp.zeros((8, CHUNK), F32)], axis=0)
    r3 = lax.broadcasted_iota(jnp.int32, (3 * CHUNK, CHUNK), 0) % CHUNK
    c3 = lax.broadcasted_iota(jnp.int32, (3 * CHUNK, CHUNK), 1)
    csr = _dot(jnp.concatenate(_pieces(ar, 3), axis=1), (r3 <= c3).astype(BF16))[0:8]
    return lower, upper, dt, a_neg, cs, csr


def _head_masked_rows(v):
    hl = lax.broadcasted_iota(jnp.int32, v.shape, 1) // SSM_HEAD_DIM
    return jnp.concatenate([jnp.where(hl == j, v, jnp.zeros_like(v)) for j in range(8)], axis=0)


SSD_STEP_CHUNKS = 8


def _chunk_view(ref, kind, k):
    if kind == "rows":
        return ref.at[pl.ds(k * CHUNK, CHUNK), :]
    if kind == "lanes":
        return ref.at[:, pl.ds(k * CHUNK, CHUNK)]
    if kind == "lead":
        return ref.at[k]
    return ref


def _ssd_fwd(xbc, z, dtp_g, dtp_t, bias_g, bias_r, alog_g, alog_r, d_exp, norm_w, name):
    t = xbc.shape[0]
    nc = t // CHUNK

    def body(*refs):
        @pl.when(pl.program_id(1) == 0)
        def _():
            refs[-1][...] = jnp.zeros_like(refs[-1])

        for k in range(SSD_STEP_CHUNKS):
            chunk(*[_chunk_view(r, kind, k) for r, kind in zip(refs, kinds)])

    kinds = ["rows"] * 5 + ["lanes"] + [None] * 6 + ["rows", "rows", "lead", None]

    def chunk(xs_ref, b_ref, c_ref, z_ref, dtp_ref, dtpt_ref, bias_ref, biasr_ref, alog_ref, alogr_ref, dexp_ref, nw_ref,
              y_ref, yn_ref, st_ref, state):
        lower, _, dt, a_neg, cs, csr = _ssd_common(dtp_ref[...], dtpt_ref[...], bias_ref[...], biasr_ref[...],
                                                alog_ref[...], alogr_ref[...])
        cs_e = _head_expand(cs)
        dt_e = _head_expand(dt)
        xs = xs_ref[...]
        xdt = xs * dt_e
        bm = b_ref[...]
        cm = c_ref[...]
        bmb, cmb = bm.astype(BF16), cm.astype(BF16)
        cb = _dot_nt(cmb, bmb)
        ms = []
        for j in range(8):
            dlt = cs_e[:, SSM_HEAD_DIM * j:SSM_HEAD_DIM * j + 1] - csr[j:j + 1, :]
            ms.append((cb * jnp.exp(jnp.where(lower, dlt, -jnp.inf))).astype(BF16))
        y = _dot(jnp.concatenate(ms, axis=1), _head_masked_rows(xdt.astype(BF16)))
        st_in = state[...]
        st_ref[...] = st_in
        y = y + jnp.exp(cs_e) * _dot(cmb, st_in.astype(BF16))
        cs_last = cs_e[CHUNK - 1:CHUNK, :]
        xdtd = (xdt * jnp.exp(cs_last - cs_e)).astype(BF16)
        state[...] = jnp.exp(cs_last) * st_in + _dot(bm.T.astype(BF16), xdtd)
        y_ref[...] = y
        zz = z_ref[...]
        y2 = (y + dexp_ref[...] * xs) * (zz * _sigmoid(zz))
        r = lax.rsqrt(jnp.mean(y2 * y2, axis=-1, keepdims=True) + EPS)
        yn_ref[...] = (y2 * r * nw_ref[...]).astype(BF16)

    rows = SSD_STEP_CHUNKS * CHUNK
    gw = pl.BlockSpec((rows, GROUP_W), lambda g, c: (c, g))
    in_specs = [
        gw,
        pl.BlockSpec((rows, SSM_STATE), lambda g, c: (c, D_INNER // SSM_STATE + g)),
        pl.BlockSpec((rows, SSM_STATE), lambda g, c: (c, D_INNER // SSM_STATE + SSM_GROUPS + g)),
        gw,
        pl.BlockSpec((None, rows, 128), lambda g, c: (g, c, 0)),
        pl.BlockSpec((8, rows), lambda g, c: (g, c)),
        pl.BlockSpec((None, 1, 128), lambda g, c: (g, 0, 0)),
        pl.BlockSpec((8, 128), lambda g, c: (g, 0)),
        pl.BlockSpec((None, 1, 128), lambda g, c: (g, 0, 0)),
        pl.BlockSpec((8, 128), lambda g, c: (g, 0)),
        pl.BlockSpec((1, GROUP_W), lambda g, c: (0, g)),
        pl.BlockSpec((1, GROUP_W), lambda g, c: (0, g)),
    ]
    out_specs = [gw, gw, pl.BlockSpec((SSD_STEP_CHUNKS, SSM_STATE, GROUP_W), lambda g, c: (c, 0, g))]
    out_shape = [SDS((t, D_INNER), F32), SDS((t, D_INNER), BF16), SDS((nc, SSM_STATE, D_INNER), F32)]
    return pl.pallas_call(body, grid=(SSM_GROUPS, nc // SSD_STEP_CHUNKS), in_specs=in_specs, out_specs=out_specs,
                          out_shape=out_shape,
                          scratch_shapes=[pltpu.VMEM((SSM_STATE, GROUP_W), F32)], name=name,
                          compiler_params=_params(("parallel", "arbitrary")))(
        xbc, xbc, xbc, z, dtp_g, dtp_t, bias_g, bias_r, alog_g, alog_r, d_exp, norm_w)


def _ssd_bwd(dyn, y, xbc, z, states, dtp_g, dtp_t, bias_g, bias_r, alog_g, alog_r, d_exp, norm_w, name):
    t = xbc.shape[0]
    nc = t // CHUNK

    def body(*refs):
        step = pl.program_id(1)
        hsum_ref, csum_ref, dstate = refs[-3:]

        @pl.when(step == 0)
        def _():
            dstate[...] = jnp.zeros_like(dstate)

        parts = [chunk(*[_chunk_view(r, kind, k) for r, kind in zip(refs, kinds)])
                 for k in reversed(range(SSD_STEP_CHUNKS))]
        hpart, cpart = parts[0]
        for hp, cp in parts[1:]:
            hpart, cpart = hpart + hp, cpart + cp

        @pl.when(step == 0)
        def _():
            hsum_ref[...] = hpart
            csum_ref[...] = cpart

        @pl.when(step > 0)
        def _():
            hsum_ref[...] += hpart
            csum_ref[...] += cpart

    kinds = ["rows"] * 6 + ["lead", "rows", "lanes"] + [None] * 6 + ["rows"] * 5 + [None] * 3

    def chunk(dyn_ref, y_ref, xs_ref, b_ref, c_ref, z_ref, st_ref, dtp_ref, dtpt_ref, bias_ref, biasr_ref, alog_ref,
              alogr_ref, dexp_ref, nw_ref, dxs_ref, db_ref, dc_ref, dz_ref, ddt_ref, hsum_ref, csum_ref, dstate):
        dtp = dtp_ref[...]
        bias = bias_ref[...]
        lower, upper, dt, a_neg, cs, csr = _ssd_common(dtp, dtpt_ref[...], bias, biasr_ref[...], alog_ref[...],
                                                       alogr_ref[...])
        cs_e = _head_expand(cs)
        dt_e = _head_expand(dt)
        xs = xs_ref[...]
        xdt = xs * dt_e
        bm = b_ref[...]
        cm = c_ref[...]
        bmb, cmb = bm.astype(BF16), cm.astype(BF16)
        y = y_ref[...]
        dexp = dexp_ref[...]
        nw = nw_ref[...]

        zz = z_ref[...]
        sg = _sigmoid(zz)
        gate = zz * sg
        ytot = y + dexp * xs
        y2 = ytot * gate
        r = lax.rsqrt(jnp.mean(y2 * y2, axis=-1, keepdims=True) + EPS)
        dynv = dyn_ref[...]
        xh = y2 * r
        gn = dynv * nw
        dy2 = r * (gn - xh * jnp.mean(gn * xh, axis=-1, keepdims=True))
        dy = dy2 * gate
        dz_ref[...] = (dy2 * ytot * (sg * (1.0 + zz * (1.0 - sg)))).astype(BF16)
        csum_part = jnp.sum(dynv * xh, axis=0, keepdims=True)

        cb = _dot_nt(cmb, bmb)
        dyb = dy.astype(BF16)
        xdtb = xdt.astype(BF16)
        dym = _head_masked_rows(dyb)
        dm = _dot_nt(dym, xdtb)
        dmt = _dot_nt(_head_masked_rows(xdtb), dyb)
        lane = lax.broadcasted_iota(jnp.int32, (CHUNK, 128), 1)
        mts = []
        dcb = jnp.zeros((CHUNK, CHUNK), F32)
        dcs = jnp.zeros((CHUNK, 128), F32)
        for j in range(8):
            dlt = cs_e[:, SSM_HEAD_DIM * j:SSM_HEAD_DIM * j + 1] - csr[j:j + 1, :]
            lj = jnp.exp(jnp.where(lower, dlt, -jnp.inf))
            mj = cb * lj
            mjt = mj.T
            mts.append(mjt.astype(BF16))
            dmj = dm[CHUNK * j:CHUNK * (j + 1)]
            dcb = dcb + dmj * lj
            rows = jnp.sum(dmj * mj, axis=1, keepdims=True)
            cols = jnp.sum(dmt[CHUNK * j:CHUNK * (j + 1)] * mjt, axis=1, keepdims=True)
            dcs = dcs + jnp.where(lane == j, rows - cols, 0.0)
        dxdt = _dot(jnp.concatenate(mts, axis=1), dym)
        dst_out = dstate[...]
        dst_outb = dst_out.astype(BF16)
        st_in = st_ref[...]
        st_inb = st_in.astype(BF16)
        cs_last = cs_e[CHUNK - 1:CHUNK, :]
        decay = jnp.exp(cs_last - cs_e)
        e_last = jnp.exp(cs_last)
        gpart = decay * _dot(bmb, dst_outb)
        dxdt = dxdt + gpart
        dyw = (jnp.exp(cs_e) * dy).astype(BF16)
        dcbb = dcb.astype(BF16)
        dc_ref[...] = _dot_nt(dyw, st_inb) + _dot(dcbb, bmb)
        db_ref[...] = _dot_nt((xdt * decay).astype(BF16), dst_outb) + _dot(dcb.T.astype(BF16), cmb)
        dstate[...] = e_last * dst_out + _dot(cm.T.astype(BF16), dyw)
        y_off = jnp.exp(cs_e) * _dot(cmb, st_inb)
        xg = xdt * gpart
        vec = jnp.concatenate([jnp.sum(dy * xs, axis=0, keepdims=True),
                               jnp.sum(xg + dst_out * e_last * st_in, axis=0, keepdims=True),
                               jnp.zeros((14, GROUP_W), F32)], axis=0)
        s_cs, s_dt, s_vec = _head_sum([dy * y_off - xg, dxdt * xs, vec])
        d_skip = s_vec[0:1]
        ri = lax.broadcasted_iota(jnp.int32, (CHUNK, 128), 0)
        dcs = dcs + s_cs + jnp.where(ri == CHUNK - 1, s_vec[1:2], 0.0)
        da = _tri_left(upper, dcs)
        ddt = da * a_neg + s_dt
        dxs_ref[...] = dxdt * dt_e + dy * dexp
        ddtp = ddt * _sigmoid(dtp + bias)
        ddt_ref[...] = ddtp
        d_alog = jnp.sum(da * dt, axis=0, keepdims=True) * a_neg
        hpart = jnp.concatenate([jnp.sum(ddtp, axis=0, keepdims=True), d_alog, d_skip, jnp.zeros((5, 128), F32)], axis=0)
        cpart = jnp.concatenate([csum_part, jnp.zeros((7, GROUP_W), F32)], axis=0)
        return hpart, cpart

    steps = nc // SSD_STEP_CHUNKS
    rows = SSD_STEP_CHUNKS * CHUNK
    rc = lambda c: steps - 1 - c
    gw = pl.BlockSpec((rows, GROUP_W), lambda g, c: (rc(c), g))
    bsp = pl.BlockSpec((rows, SSM_STATE), lambda g, c: (rc(c), D_INNER // SSM_STATE + g))
    csp = pl.BlockSpec((rows, SSM_STATE), lambda g, c: (rc(c), D_INNER // SSM_STATE + SSM_GROUPS + g))
    in_specs = [
        gw, gw, gw, bsp, csp, gw,
        pl.BlockSpec((SSD_STEP_CHUNKS, SSM_STATE, GROUP_W), lambda g, c: (rc(c), 0, g)),
        pl.BlockSpec((None, rows, 128), lambda g, c: (g, rc(c), 0)),
        pl.BlockSpec((8, rows), lambda g, c: (g, rc(c))),
        pl.BlockSpec((None, 1, 128), lambda g, c: (g, 0, 0)),
        pl.BlockSpec((8, 128), lambda g, c: (g, 0)),
        pl.BlockSpec((None, 1, 128), lambda g, c: (g, 0, 0)),
        pl.BlockSpec((8, 128), lambda g, c: (g, 0)),
        pl.BlockSpec((1, GROUP_W), lambda g, c: (0, g)),
        pl.BlockSpec((1, GROUP_W), lambda g, c: (0, g)),
    ]
    nsp = pl.BlockSpec((rows, SSM_STATE), lambda g, c: (rc(c), g))
    out_specs = [gw, nsp, nsp, gw,
                 pl.BlockSpec((None, rows, 128), lambda g, c: (g, rc(c), 0)),
                 pl.BlockSpec((None, 8, 128), lambda g, c: (g, 0, 0)),
                 pl.BlockSpec((8, GROUP_W), lambda g, c: (0, g))]
    gn = SSM_GROUPS * SSM_STATE
    out_shape = [SDS((t, D_INNER), F32), SDS((t, gn), F32), SDS((t, gn), F32), SDS((t, D_INNER), BF16),
                 SDS((SSM_GROUPS, t, 128), F32), SDS((SSM_GROUPS, 8, 128), F32), SDS((8, D_INNER), F32)]
    return pl.pallas_call(body, grid=(SSM_GROUPS, steps), in_specs=in_specs, out_specs=out_specs, out_shape=out_shape,
                          scratch_shapes=[pltpu.VMEM((SSM_STATE, GROUP_W), F32)], name=name,
                          compiler_params=_params(("parallel", "arbitrary")))(
        dyn, y, xbc, xbc, xbc, z, states, dtp_g, dtp_t, bias_g, bias_r, alog_g, alog_r, d_exp, norm_w)


BLK = 128
HEAD_PAIRS = ATTN_W // 128
ATTN_SCALE = 0.125
SPAN_BLOCKS = {1: 8, 4: 4, 16: 1}


def _slope_table(group):
    n = len(ATTN_GROUPS) * 16
    tbl = [[2.0 ** (-8.0 * (16 * group + 2 * p + s + 1) / n) if s < 2 else 0.0 for s in range(128)] for p in range(HEAD_PAIRS)]
    return jnp.asarray(tbl, F32)


def _lane_lo(rows):
    return lax.broadcasted_iota(jnp.int32, (rows, 128), 1) < 64


def _rows(start, dil):
    return pl.ds(start, BLK, stride=dil) if dil > 1 else pl.ds(start, BLK)


def _stack_heads(x):
    lo = _lane_lo(BLK)
    return jnp.concatenate([jnp.where(lo, x, jnp.zeros_like(x)), jnp.where(lo, jnp.zeros_like(x), x)], axis=0)


def _pair_cols(x):
    return jnp.concatenate([x[:, 0:1], x[:, 64:65]], axis=0)


def _attn_bias(sl, dil, first_span, last_span):
    qi = lax.broadcasted_iota(jnp.int32, (BLK, 2 * BLK), 0)
    kj = lax.broadcasted_iota(jnp.int32, (BLK, 2 * BLK), 1)
    dist = qi + BLK - kj
    valid = (dist >= 0) & (dist <= BLK)
    distf = dist.astype(F32) * float(dil)

    def stacked(ok, d):
        return jnp.concatenate([jnp.where(ok, -sl[:, h:h + 1] * d, -jnp.inf) for h in range(2)], axis=0)

    first = stacked(valid & ((kj >= BLK) | jnp.logical_not(first_span)), distf)
    after = None
    if last_span is not None:
        after = stacked((kj[:, :BLK] >= qi[:, :BLK]) & jnp.logical_not(last_span), distf[:, :BLK])
    return first, stacked(valid, distf), after


def _attn_specs(group, t):
    _, dil = ATTN_GROUPS[group]
    nblk = SPAN_BLOCKS[dil]
    span, edge = BLK * dil * nblk, BLK * dil
    per = span // edge

    def lane_block(which):
        if which is None:
            return lambda p: p
        return lambda p: 3 * (group * HEAD_PAIRS + p) + which

    def cur(which):
        col = lane_block(which)
        return pl.BlockSpec((span, 128), lambda s, p: (s, col(p)))

    def before(which):
        col = lane_block(which)
        return pl.BlockSpec((edge, 128), lambda s, p: (jnp.maximum(s * per - 1, 0), col(p)))

    def after(which):
        col = lane_block(which)
        return pl.BlockSpec((edge, 128), lambda s, p: (jnp.minimum((s + 1) * per, t // edge - 1), col(p)))

    slopes = pl.BlockSpec((HEAD_PAIRS, 128), lambda s, p: (0, 0))
    return dil, nblk, span, cur, before, after, slopes


def _pair_major(w, inverse=False):
    k = w.shape[0]
    g = len(ATTN_GROUPS)
    if inverse:
        return jnp.transpose(w.reshape(k, g, HEAD_PAIRS, 3, 128), (0, 3, 1, 2, 4)).reshape(k, QKV_DIM)
    return jnp.transpose(w.reshape(k, 3, g, HEAD_PAIRS, 128), (0, 2, 3, 1, 4)).reshape(k, QKV_DIM)


def _attn_fwd(qkv, group, name):
    t = qkv.shape[0]
    dil, nblk, span, cur, before, after, slopes = _attn_specs(group, t)

    def body(q_ref, k_ref, v_ref, kp_ref, vp_ref, sl_ref, o_ref, l_ref):
        first_span = pl.program_id(0) == 0
        bias_first, bias_mid, _ = _attn_bias(sl_ref[pl.ds(pl.program_id(1), 1), :], dil, first_span, None)
        lo_q, lo_k = _lane_lo(BLK), _lane_lo(2 * BLK)
        for r in range(dil):
            kp, vp = kp_ref[_rows(r, dil), :].astype(BF16), vp_ref[_rows(r, dil), :].astype(BF16)
            for b in range(nblk):
                rows = _rows(b * BLK * dil + r, dil)
                kc, vc = k_ref[rows, :].astype(BF16), v_ref[rows, :].astype(BF16)
                k2 = jnp.concatenate([kp, kc], axis=0)
                v2 = jnp.concatenate([vp, vc], axis=0)
                kp, vp = kc, vc
                s = _dot_nt(_stack_heads((q_ref[rows, :] * ATTN_SCALE).astype(BF16)), k2) + (bias_first if b == 0 else bias_mid)
                mx = jnp.max(s, axis=-1, keepdims=True)
                p = jnp.exp(s - mx)
                den = jnp.sum(p, axis=-1, keepdims=True)
                pb = p.astype(BF16)
                o_ref[rows, :] = _dot(pb[:BLK], jnp.where(lo_k, v2, jnp.zeros_like(v2))) / den[:BLK] + \
                    _dot(pb[BLK:], jnp.where(lo_k, jnp.zeros_like(v2), v2)) / den[BLK:]
                lse = mx + jnp.log(den)
                l_ref[rows, :] = jnp.where(lo_q, lse[:BLK], lse[BLK:])

    in_specs = [cur(0), cur(1), cur(2), before(1), before(2), slopes]
    return pl.pallas_call(body, grid=(t // span, HEAD_PAIRS), in_specs=in_specs, out_specs=[cur(None), cur(None)],
                          out_shape=[SDS((t, ATTN_W), F32), SDS((t, ATTN_W), F32)], name=name,
                          compiler_params=_params(("parallel", "parallel")))(qkv, qkv, qkv, qkv, qkv, _slope_table(group))


def _attn_bwd(qkv, do, lse, dvec, dqkv, group, name):
    t = qkv.shape[0]
    dil, nblk, span, cur, before, after, slopes = _attn_specs(group, t)
    nspan = t // span

    def body(q_ref, k_ref, v_ref, kp_ref, vp_ref, qn_ref, do_ref, l_ref, d_ref, don_ref, ln_ref, dn_ref, sl_ref, *rest):
        out_ref, dq_s, dk_s, dv_s = rest[-4:]
        span_id, pair_id = pl.program_id(0), pl.program_id(1)
        bias_first, bias_mid, bias_next = _attn_bias(sl_ref[pl.ds(pair_id, 1), :], dil, span_id == 0, span_id == nspan - 1)
        lo_k = _lane_lo(2 * BLK)
        for r in range(dil):
            kp, vp = kp_ref[_rows(r, dil), :].astype(BF16), vp_ref[_rows(r, dil), :].astype(BF16)
            held = None
            for b in range(nblk + 1):
                last = b == nblk
                rows = _rows(r if last else b * BLK * dil + r, dil)
                qs, dos, ls, dvs = (qn_ref, don_ref, ln_ref, dn_ref) if last else (q_ref, do_ref, l_ref, d_ref)
                qst = _stack_heads((qs[rows, :] * ATTN_SCALE).astype(BF16))
                dost = _stack_heads(dos[rows, :].astype(BF16))
                if last:
                    k2, v2, bias = kp, vp, bias_next
                else:
                    kc, vc = k_ref[rows, :].astype(BF16), v_ref[rows, :].astype(BF16)
                    k2, v2 = jnp.concatenate([kp, kc], axis=0), jnp.concatenate([vp, vc], axis=0)
                    kp, vp = kc, vc
                    bias = bias_first if b == 0 else bias_mid
                p = jnp.exp(_dot_nt(qst, k2) + bias - _pair_cols(ls[rows, :]))
                ds = (p * (_dot_nt(dost, v2) - _pair_cols(dvs[rows, :]))).astype(BF16)
                dk2 = _dot_tn(ds, qst)
                dv2 = _dot_tn(p.astype(BF16), dost)
                if held is not None:
                    dk_s[held[0], :] = held[1] + dk2[:BLK]
                    dv_s[held[0], :] = held[2] + dv2[:BLK]
                if not last:
                    k_heads = jnp.concatenate([jnp.where(lo_k, k2, jnp.zeros_like(k2)),
                                               jnp.where(lo_k, jnp.zeros_like(k2), k2)], axis=0)
                    dq_s[rows, :] = _dot(jnp.concatenate([ds[:BLK], ds[BLK:]], axis=1), k_heads) * ATTN_SCALE
                    held = (rows, dk2[BLK:], dv2[BLK:])
        out_ref[:, 0:128] = dq_s[...]
        out_ref[:, 128:256] = dk_s[...]
        out_ref[:, 256:384] = dv_s[...]

    out_spec = pl.BlockSpec((span, 3 * 128), lambda s, p: (s, group * HEAD_PAIRS + p))
    in_specs = [cur(0), cur(1), cur(2), before(1), before(2), after(0), cur(None), cur(None), cur(None),
                after(None), after(None), after(None), slopes]
    args = [qkv, qkv, qkv, qkv, qkv, qkv, do, lse, dvec, do, lse, dvec, _slope_table(group)]
    aliases = {}
    if dqkv is not None:
        in_specs.append(pl.BlockSpec(memory_space=pl.ANY))
        args.append(dqkv)
        aliases = {len(args) - 1: 0}
    return pl.pallas_call(body, grid=(nspan, HEAD_PAIRS), in_specs=in_specs, out_specs=out_spec,
                          out_shape=SDS((t, QKV_DIM), F32), input_output_aliases=aliases,
                          scratch_shapes=[pltpu.VMEM((span, 128), F32)] * 3, name=name,
                          compiler_params=_params(("parallel", "parallel")))(*args)


def _combine_weights(l0, l1, l2):
    mx = jnp.maximum(jnp.maximum(l0, l1), l2)
    e0, e1, e2 = jnp.exp(l0 - mx), jnp.exp(l1 - mx), jnp.exp(l2 - mx)
    den = e0 + e1 + e2
    return e0 / den, e1 / den, e2 / den


def _combine_fwd(os_, ls_, name):
    t = os_[0].shape[0]

    def body(o0, o1, o2, l0, l1, l2, out):
        w0, w1, w2 = _combine_weights(l0[...], l1[...], l2[...])
        out[...] = (w0 * o0[...] + w1 * o1[...] + w2 * o2[...]).astype(BF16)

    row = pl.BlockSpec((ROW_TILE, ATTN_W), lambda i: (i, 0))
    return pl.pallas_call(body, grid=(t // ROW_TILE,), in_specs=[row] * 6, out_specs=row, out_shape=SDS((t, ATTN_W), BF16),
                          name=name, compiler_params=_params(("parallel",)))(*os_, *ls_)


def _combine_bwd(do, os_, ls_, name):
    t = do.shape[0]

    def body(do_ref, o0, o1, o2, l0, l1, l2, g0, g1, g2, d0, d1, d2):
        w0, w1, w2 = _combine_weights(l0[...], l1[...], l2[...])
        dov = do_ref[...]
        prod = dov * (w0 * o0[...] + w1 * o1[...] + w2 * o2[...])
        r = (lax.broadcasted_iota(jnp.int32, (3 * 128, 128), 0) % 128) // 64
        c = lax.broadcasted_iota(jnp.int32, (3 * 128, 128), 1) // 64
        same = (r == c).astype(BF16)
        tbar = jnp.concatenate([_dot(jnp.concatenate(_pieces(prod[:, 128 * k:128 * k + 128], 3), axis=1), same)
                                for k in range(HEAD_PAIRS)], axis=1)
        for w, g, d in ((w0, g0, d0), (w1, g1, d1), (w2, g2, d2)):
            g[...] = w * dov
            d[...] = w * tbar

    row = pl.BlockSpec((ROW_TILE, ATTN_W), lambda i: (i, 0))
    return pl.pallas_call(body, grid=(t // ROW_TILE,), in_specs=[row] * 7, out_specs=[row] * 6,
                          out_shape=[SDS((t, ATTN_W), F32)] * 6, name=name,
                          compiler_params=_params(("parallel",)))(do, *os_, *ls_)


def _peer(k):
    x, y, c = lax.axis_index("x"), lax.axis_index("y"), lax.axis_index("c")
    px = 1 - x if k & 4 else x
    py = 1 - y if k & 2 else y
    pc = 1 - c if k & 1 else c
    return (px, py, pc), 4 * px + 2 * py + pc


def _my_index():
    return 4 * lax.axis_index("x") + 2 * lax.axis_index("y") + lax.axis_index("c")


def _exchange_sems(n):
    return [pltpu.SemaphoreType.DMA((n * (NDEV - 1),)), pltpu.SemaphoreType.DMA((n * (NDEV - 1),)),
            pltpu.SemaphoreType.DMA((n,))]


def _scatter_copies(spans, ins, outs, send, recv, local, arrivals):
    me = _my_index()
    own, sent, arriving = [], [], []
    for i in range(len(ins)):
        def part(ref, slot, i=i):
            return ref.at[slot] if spans[i] is None else ref.at[slot, pl.ds(spans[i][0], spans[i][1])]
        own.append(pltpu.make_async_copy(part(ins[i], me), part(outs[i], me), local.at[i]))
        for k in range(1, NDEV):
            peer, pidx = _peer(k)
            s = i * (NDEV - 1) + k - 1
            for dst, into in ((part(outs[i], me), sent), (part(outs[i], pidx), arriving)):
                if into is sent or arrivals:
                    into.append(pltpu.make_async_remote_copy(src_ref=part(ins[i], pidx), dst_ref=dst, send_sem=send.at[s],
                                                             recv_sem=recv.at[s], device_id=peer, device_id_type=MESH))
    return own, sent, arriving


def _exchange_start(spans, ins, outs, send, recv, local):
    own, sent, _ = _scatter_copies(spans, ins, outs, send, recv, local, arrivals=False)
    for cp in own + sent:
        cp.start()


def _exchange_wait(spans, ins, outs, send, recv, local):
    own, sent, arriving = _scatter_copies(spans, ins, outs, send, recv, local, arrivals=True)
    for cp in sent:
        cp.wait_send()
    for cp in arriving:
        cp.wait_recv()
    for cp in own:
        cp.wait()


def _two_level(phase, ins, outs, send, recv, local):
    n = len(ins)
    x, y, c = lax.axis_index("x"), lax.axis_index("y"), lax.axis_index("c")
    here, sibling = (x, y, c), (x, y, 1 - c)
    chips = [(1 - x, y), (x, 1 - y), (1 - x, 1 - y)]

    def slot(px, py, pc):
        return 4 * px + 2 * py + pc

    def copy(i, k, block, to, src=None):
        return pltpu.make_async_remote_copy(src_ref=outs[i].at[block] if src is None else src, dst_ref=outs[i].at[block],
                                            send_sem=send.at[7 * i + k], recv_sem=recv.at[7 * i + k],
                                            device_id=to, device_id_type=MESH)

    me = slot(x, y, c)
    if phase == "pass":
        for i in range(n):
            for j, chip in enumerate(chips):
                copy(i, 1 + j, slot(*chip, c), here).wait_recv()
                copy(i, 4 + j, slot(*chip, c), sibling).start()
        return
    own = [pltpu.make_async_copy(ins[i], outs[i].at[me], local.at[i]) for i in range(n)]
    first = [copy(i, 0, me, sibling, src=ins[i]) for i in range(n)]
    first += [copy(i, 1 + j, me, (*chip, c), src=ins[i]) for i in range(n) for j, chip in enumerate(chips)]
    if phase == "start":
        for cp in own + first:
            cp.start()
        return
    for i in range(n):
        copy(i, 0, slot(x, y, 1 - c), here).wait_recv()
        for j, chip in enumerate(chips):
            copy(i, 4 + j, slot(*chip, 1 - c), here).wait_recv()
    passed = [copy(i, 4 + j, slot(*chip, c), sibling) for i in range(n) for j, chip in enumerate(chips)]
    for cp in first + passed:
        cp.wait_send()
    for cp in own:
        cp.wait()


def _gather_two_level(tensors, name):
    n = len(tensors)

    def body(*refs):
        for phase in ("start", "pass", "wait"):
            _two_level(phase, refs[:n], refs[n:2 * n], *refs[2 * n:])

    hbm = pl.BlockSpec(memory_space=pl.ANY)
    return pl.pallas_call(body, in_specs=[hbm] * n, out_specs=[hbm] * n,
                          out_shape=[SDS((NDEV,) + t.shape, t.dtype) for t in tensors],
                          scratch_shapes=_exchange_sems(n), name=name)(*tensors)


def _all_reduce_small(v, name):
    rows = v.shape[0]

    def body(v_ref, out_ref, land, send, recv):
        me = _my_index()
        land[me] = v_ref[...]
        remote = []
        for k in range(1, NDEV):
            peer, _ = _peer(k)
            cp = pltpu.make_async_remote_copy(src_ref=v_ref, dst_ref=land.at[me], send_sem=send.at[k - 1],
                                              recv_sem=recv.at[k - 1], device_id=peer, device_id_type=MESH)
            cp.start()
            remote.append(cp)
        for cp in remote:
            cp.wait_send()
        for k in range(1, NDEV):
            peer, pidx = _peer(k)
            pltpu.make_async_remote_copy(src_ref=v_ref, dst_ref=land.at[pidx], send_sem=send.at[k - 1],
                                         recv_sem=recv.at[k - 1], device_id=peer, device_id_type=MESH).wait_recv()
        total = land[0]
        for d in range(1, NDEV):
            total = total + land[d]
        out_ref[...] = total

    vm = pl.BlockSpec(memory_space=pltpu.VMEM)
    return pl.pallas_call(
        body, in_specs=[vm], out_specs=vm, out_shape=SDS((rows, 128), F32),
        scratch_shapes=[pltpu.VMEM((NDEV, rows, 128), F32), pltpu.SemaphoreType.DMA((NDEV - 1,)),
                        pltpu.SemaphoreType.DMA((NDEV - 1,))],
        name=name)(v)


def _adamw_math(w, g, m, v):
    m = ADAM_B1 * m + (1.0 - ADAM_B1) * g
    v = ADAM_B2 * v + (1.0 - ADAM_B2) * (g * g)
    m_hat = m / (1.0 - ADAM_B1 ** ADAM_STEP)
    v_hat = v / (1.0 - ADAM_B2 ** ADAM_STEP)
    delta = -ADAM_LR * (m_hat / (jnp.sqrt(v_hat) + ADAM_EPS) + ADAM_WD * w)
    return delta, m, v


def _row_tile(rows, cols):
    tr = rows
    while tr * cols * 4 > (1 << 20) and tr % 16 == 0:
        tr //= 2
    return tr


def _adamw(g, w, m, v, name):
    rows, cols = w.shape
    tr = _row_tile(rows, cols)

    def body(g_ref, w_ref, m_ref, v_ref, d_out, m_out, v_out):
        d, mn, vn = _adamw_math(w_ref[...], g_ref[...], m_ref[...], v_ref[...])
        d_out[...] = d
        m_out[...] = mn
        v_out[...] = vn

    sp = pl.BlockSpec((tr, cols), lambda i: (i, 0))
    return pl.pallas_call(body, grid=(rows // tr,), in_specs=[sp] * 4, out_specs=[sp] * 3,
                          out_shape=[SDS((rows, cols), F32)] * 3, name=name, compiler_params=_params(("parallel",)))(g, w, m, v)


def _reduce_adamw(parts, w, m, v, name):
    rows, cols = w.shape
    tr = _row_tile(math.gcd(*[n for _, _, n in parts]), cols)
    tiles = [n // tr for _, _, n in parts]
    starts = [r0 // tr for _, r0, _ in parts]
    assert all(r0 % tr == 0 for _, r0, _ in parts) and sum(tiles) == rows // tr

    def body(*refs):
        p_refs = refs[:len(parts)]
        w_ref, m_ref, v_ref, g_out, d_out, m_out, v_out = refs[len(parts):]
        i = pl.program_id(0)
        for h, p_ref in enumerate(p_refs):
            @pl.when((i >= starts[h]) & (i < starts[h] + tiles[h]))
            def _(p_ref=p_ref):
                g = p_ref[0].astype(F32)
                for d in range(1, NDEV):
                    g = g + p_ref[d].astype(F32)
                g_out[...] = g
                dl, mn, vn = _adamw_math(w_ref[...], g, m_ref[...], v_ref[...])
                d_out[...] = dl
                m_out[...] = mn
                v_out[...] = vn

    sp = pl.BlockSpec((tr, cols), lambda i: (i, 0))
    psp = [pl.BlockSpec((NDEV, tr, cols), lambda i, h=h: (0, jnp.clip(i, starts[h], starts[h] + tiles[h] - 1), 0))
           for h in range(len(parts))]
    return pl.pallas_call(body, grid=(rows // tr,), in_specs=psp + [sp, sp, sp], out_specs=[sp] * 4,
                          out_shape=[SDS((rows, cols), F32)] * 4, name=name,
                          compiler_params=_params(("parallel",)))(*[p for p, _, _ in parts], w, m, v)


def _pack(items):
    rows = []
    for a in items:
        a = a.reshape(-1).astype(F32)
        pad = (-a.shape[0]) % 128
        rows.append(jnp.pad(a, (0, pad)).reshape(-1, 128))
    out = jnp.concatenate(rows, axis=0)
    return jnp.pad(out, ((0, (-out.shape[0]) % 8), (0, 0)))


def _unpack(packed, shapes):
    out, r = [], 0
    for shp in shapes:
        n = math.prod(shp)
        nr = -(-n // 128)
        out.append(packed[r:r + nr].reshape(-1)[:n].reshape(shp))
        r += nr
    return out


def _ident(t):
    return (t,)


def _add(t, res):
    return (t + res,)


def kernel(x, norm_mix, norm_mlp, ssm_w_in, ssm_conv_w, ssm_conv_b, ssm_dt_bias, ssm_a_log, ssm_d, ssm_norm_w, ssm_w_out, attn_w_qkv, attn_w_o, mlp_w1, mlp_w2, final_norm, loss_target, m_norm_mix, m_norm_mlp, m_ssm_w_in, m_ssm_conv_w, m_ssm_conv_b, m_ssm_dt_bias, m_ssm_a_log, m_ssm_d, m_ssm_norm_w, m_ssm_w_out, m_attn_w_qkv, m_attn_w_o, m_mlp_w1, m_mlp_w2, m_final_norm, v_norm_mix, v_norm_mlp, v_ssm_w_in, v_ssm_conv_w, v_ssm_conv_b, v_ssm_dt_bias, v_ssm_a_log, v_ssm_d, v_ssm_norm_w, v_ssm_w_out, v_attn_w_qkv, v_attn_w_o, v_mlp_w1, v_mlp_w2, v_final_norm):
    t = x.shape[1]
    x0 = x.reshape(t, D_MODEL)
    tgt = loss_target.reshape(t, D_MODEL)
    me = _my_index()
    in_dim = D_INNER + CONV_DIM + SSM_HEADS
    in_shard = in_dim // NDEV
    zx_dim = D_INNER + CONV_DIM

    s_out, s_qkv, s_o = ssm_w_out[0].astype(BF16), attn_w_qkv[0].astype(BF16), attn_w_o[0].astype(BF16)
    s_w1, s_w2 = mlp_w1.astype(BF16), mlp_w2.astype(BF16)
    g_in, g_cw = _gather_two_level([ssm_w_in[0].astype(BF16), ssm_conv_w[0]], "gather_in_proj")
    w_in = jnp.transpose(g_in, (1, 0, 2)).reshape(D_MODEL, in_dim)
    w_z, w_x = w_in[:, :D_INNER], w_in[:, D_INNER:zx_dim]
    w_dt = jnp.pad(w_in[:, zx_dim:], ((0, 0), (0, 128 - SSM_HEADS)))
    conv_w = jnp.transpose(g_cw, (1, 0, 2)).reshape(CONV_WIDTH, CONV_DIM)
    g_w1, g_w2 = [None, None], [None, None]

    def lanes(p):
        return jnp.pad(p.reshape(SSM_GROUPS, 1, 8), ((0, 0), (0, 0), (0, 120)))

    def rows(p):
        return jnp.broadcast_to(p.reshape(SSM_HEADS, 1), (SSM_HEADS, 128))

    bias_g, bias_r = lanes(ssm_dt_bias[0]), rows(ssm_dt_bias[0])
    alog_g, alog_r = lanes(ssm_a_log[0]), rows(ssm_a_log[0])
    d_exp = jnp.repeat(ssm_d[0], SSM_HEAD_DIM).reshape(1, D_INNER)
    norm_w = ssm_norm_w

    def relu2(tot):
        r = jnp.maximum(tot, 0.0)
        return r, r * r

    def mlp_fwd(xin, layer, tag, down_carry):
        h = _rmsnorm_fwd(xin, norm_mlp[layer:layer + 1], f"norm_mlp{tag}")
        r, a, w2 = _mm_fwd(h, g_w1[layer], "cols", D_MODEL, D_FF, epi=relu2, outs=(BF16, BF16),
                           carry=("gather", [s_w2[layer]]), name=f"mlp_up{tag}")
        g_w2[layer] = w2.reshape(D_FF, D_MODEL)
        xout, *got = _mm_fwd(a, g_w2[layer], "plain", D_FF, D_MODEL, epi=_add, outs=(F32,), extras=(xin,),
                             carry=("gather", down_carry) if down_carry else None, name=f"mlp_down{tag}")
        return h, r, a, xout, got

    h0 = _rmsnorm_fwd(x0, norm_mix[0:1], "norm_mix0")
    z, g_out = _mm_fwd(h0, w_z, "plain", D_MODEL, D_INNER, epi=_ident, outs=(F32,), carry=("gather", [s_out]), name="ssm_in_z")
    xpre, g_w1[0] = _mm_fwd(h0, w_x, "plain", D_MODEL, CONV_DIM, epi=_ident, outs=(F32,), carry=("gather", [s_w1[0]]),
                            name="ssm_in_x")
    dtp, = _mm_fwd(h0, w_dt, "plain", D_MODEL, 128, epi=_ident, outs=(F32,), name="ssm_in_dt")
    xbc = _conv_fwd(xpre, conv_w, ssm_conv_b, "conv_fwd")
    dtp64 = dtp[:, :SSM_HEADS]
    dtp_g = jnp.pad(jnp.transpose(dtp64.reshape(t, SSM_GROUPS, 8), (1, 0, 2)), ((0, 0), (0, 0), (0, 120)))
    dtp_t = jnp.transpose(dtp64)
    y_ssd, yn, states = _ssd_fwd(xbc, z, dtp_g, dtp_t, bias_g, bias_r, alog_g, alog_r, d_exp, norm_w, "ssd_fwd")
    g_out = g_out.reshape(D_INNER, D_MODEL)
    x1, g_o = _mm_fwd(yn, g_out, "plain", D_INNER, D_MODEL, epi=_add, outs=(F32,), extras=(x0,), carry=("gather", [s_o]),
                      name="ssm_out")
    h1, r1, a1, x2, (g_qkv,) = mlp_fwd(x1, 0, "0", [s_qkv])

    h2 = _rmsnorm_fwd(x2, norm_mix[1:2], "norm_mix1")
    w_qkv = _pair_major(jnp.transpose(g_qkv, (1, 0, 2)).reshape(D_MODEL, QKV_DIM))
    qkv, g_w1[1] = _mm_fwd(h2, w_qkv, "plain", D_MODEL, QKV_DIM, epi=_ident, outs=(F32,),
                           carry=("gather", [s_w1[1]]), name="attn_qkv")
    att = [_attn_fwd(qkv, g, f"attn_fwd{g}") for g in range(3)]
    os_, ls_ = [a[0] for a in att], [a[1] for a in att]
    o_mix = _combine_fwd(os_, ls_, "attn_combine")
    x3, = _mm_fwd(o_mix, g_o, "cols", ATTN_W, D_MODEL, epi=_add, outs=(F32,), extras=(x2,), tn=D_MODEL // NDEV, name="attn_out")
    h3, r3, a3, x4, _ = mlp_fwd(x3, 1, "1", None)

    dx4, dx4b, loss_acc, d_final = _loss_head(x4, tgt, final_norm.reshape(1, D_MODEL), "loss_head")

    def mlp_bwd(xin, h, r, a, dxo, dxob, layer, tag):
        du, = _mm_dx(dxob, g_w2[layer], "plain", D_FF, D_MODEL, epi=lambda tot, rr: (tot * (2.0 * rr.astype(F32)),),
                     outs=(BF16,), extras=(r,), name=f"mlp_down_dx{tag}")
        dw2, = _mm_dw(a, dxob, "plain", name=f"mlp_down_dw{tag}")
        dw2 = dw2.reshape(NDEV, D_FF // NDEV, D_MODEL)
        dw1, p_dw2 = _mm_dw(h, du, "cols", carry=("scatter", [dw2], [W2_SPANS[0]]), name=f"mlp_up_dw{tag}")
        dh, p_dw1 = _mm_dx(du, g_w1[layer], "cols", D_MODEL, D_FF, epi=_ident, outs=(F32,),
                           carry=("scatter", [dw1], [W1_SPANS[0]]), name=f"mlp_up_dx{tag}")
        dxi, dxib, dg = _rmsnorm_bwd(xin, norm_mlp[layer:layer + 1], dh, dxo, f"norm_mlp_bwd{tag}")
        return dxi, dxib, dg, (dw1, dw2), (p_dw1, p_dw2)

    W1_SPANS = [(0, 7 * D_MODEL // 8), (7 * D_MODEL // 8, D_MODEL // 8)]
    W2_SPANS = [(0, 7 * D_FF // NDEV // 8), (7 * D_FF // NDEV // 8, D_FF // NDEV // 8)]

    def mlp_parts(first, rest):
        return [[(first[0],) + W1_SPANS[0], (rest[0],) + W1_SPANS[1]], [(first[1],) + W2_SPANS[0], (rest[1],) + W2_SPANS[1]]]

    dx3, dx3b, dg_mlp1, left1, got1 = mlp_bwd(x3, h3, r3, a3, dx4, dx4b, 1, "1")

    dw_o, *rest1 = _mm_dw(o_mix, dx3b, "cols", tn=D_MODEL // NDEV, carry=("scatter", list(left1), [W1_SPANS[1], W2_SPANS[1]]),
                          name="attn_out_dw")
    p_w1_1, p_w2_1 = mlp_parts(got1, rest1)
    do, p_o = _mm_dx(dx3b, g_o, "cols", ATTN_W, D_MODEL, epi=_ident, outs=(F32,), tk=D_MODEL // NDEV,
                     carry=("scatter", [dw_o]), name="attn_out_dx")
    cb = _combine_bwd(do, os_, ls_, "attn_combine_bwd")
    dos, dvecs = cb[:3], cb[3:]
    dqkv = None
    for g in range(3):
        dqkv = _attn_bwd(qkv, dos[g], ls_[g], dvecs[g], dqkv, g, f"attn_bwd{g}")
    dw_qkv, = _mm_dw(h2, dqkv, "plain", tk=1024, name="attn_qkv_dw")
    dw_qkv = jnp.transpose(_pair_major(dw_qkv, inverse=True).reshape(D_MODEL, NDEV, QKV_DIM // NDEV), (1, 0, 2))
    dh2, p_qkv = _mm_dx(dqkv, w_qkv, "plain", D_MODEL, QKV_DIM, epi=_ident, outs=(F32,), tk=1536,
                        carry=("scatter", [dw_qkv]), name="attn_qkv_dx")
    dx2, dx2b, dg_mix1 = _rmsnorm_bwd(x2, norm_mix[1:2], dh2, dx3, "norm_mix_bwd1")

    dx1, dx1b, dg_mlp0, left0, got0 = mlp_bwd(x1, h1, r1, a1, dx2, dx2b, 0, "0")

    dw_out, *rest0 = _mm_dw(yn, dx1b, "plain", carry=("scatter", list(left0), [W1_SPANS[1], W2_SPANS[1]]), name="ssm_out_dw")
    p_w1_0, p_w2_0 = mlp_parts(got0, rest0)
    dw_out = dw_out.reshape(NDEV, D_INNER // NDEV, D_MODEL)
    dyn, p_out = _mm_dx(dx1b, g_out, "plain", D_INNER, D_MODEL, epi=_ident, outs=(F32,), carry=("scatter", [dw_out]),
                        name="ssm_out_dx")
    dxs, d_b, d_c, dz, ddtp_g, hsums, csums = _ssd_bwd(dyn, y_ssd, xbc, z, states, dtp_g, dtp_t, bias_g, bias_r,
                                                       alog_g, alog_r, d_exp, norm_w, "ssd_bwd")
    du, sums = None, []
    for part, col0, tag in ((dxs, 0, "x"), (d_b, D_INNER, "b"), (d_c, D_INNER + SSM_GROUPS * SSM_STATE, "c")):
        du, s = _conv_bwd(xpre, conv_w, ssm_conv_b, part, du, col0, f"conv_bwd_{tag}")
        sums.append(s)
    conv_sums = jnp.concatenate(sums, axis=1)
    ddtp = jnp.transpose(ddtp_g[:, :, :8], (1, 0, 2)).reshape(t, SSM_HEADS)
    ddtp = jnp.pad(ddtp, ((0, 0), (0, 128 - SSM_HEADS))).astype(BF16)
    dw_z, = _mm_dw(h0, dz, "plain", name="ssm_in_z_dw")
    dw_x, = _mm_dw(h0, du, "plain", name="ssm_in_x_dw")
    dw_dt, = _mm_dw(h0, ddtp, "plain", name="ssm_in_dt_dw")
    half = D_MODEL // 2
    dw_in = jnp.concatenate([dw_z, dw_x, dw_dt[:, :SSM_HEADS]], axis=1)
    dw_in = jnp.transpose(dw_in.reshape(D_MODEL, NDEV, in_shard), (1, 0, 2))
    dh0, = _mm_dx(ddtp, w_dt, "plain", D_MODEL, 128, epi=_ident, outs=(F32,), name="ssm_in_dt_dx")
    dh0, p_in_a = _mm_dx(dz, w_z, "plain", D_MODEL, D_INNER, epi=_add, outs=(F32,), extras=(dh0,),
                         carry=("scatter", [dw_in], [(0, half)]), name="ssm_in_z_dx")
    dh0, p_in_b = _mm_dx(du, w_x, "plain", D_MODEL, CONV_DIM, epi=_add, outs=(F32,), extras=(dh0,),
                         carry=("scatter", [dw_in], [(half, half)]), name="ssm_in_x_dx")
    dx0, _, dg_mix0 = _rmsnorm_bwd(x0, norm_mix[0:1], dh0, dx1, "norm_mix_bwd0")

    def whole(p):
        return [(p, 0, p.shape[1])]

    parts = [[(p_in_a, 0, half), (p_in_b, half, half)], whole(p_out), whole(p_qkv), whole(p_o),
             p_w1_0, p_w1_1, p_w2_0, p_w2_1]

    def own(w, mm, vv):
        shp = w.shape
        f = lambda a: a.reshape(-1, shp[-1])
        return f(w), f(mm), f(vv), shp

    big = {}
    for key, part, (w, mm, vv) in (
            ("ssm_w_in", parts[0], (ssm_w_in, m_ssm_w_in, v_ssm_w_in)),
            ("ssm_w_out", parts[1], (ssm_w_out, m_ssm_w_out, v_ssm_w_out)),
            ("attn_w_qkv", parts[2], (attn_w_qkv, m_attn_w_qkv, v_attn_w_qkv)),
            ("attn_w_o", parts[3], (attn_w_o, m_attn_w_o, v_attn_w_o))):
        w2, m2, v2, shp = own(w, mm, vv)
        res = _reduce_adamw(part, w2, m2, v2, f"adamw_{key}")
        big[key] = [r.reshape(shp) for r in res]
    for key, pa, pb, (w, mm, vv) in (("mlp_w1", parts[4], parts[5], (mlp_w1, m_mlp_w1, v_mlp_w1)),
                                     ("mlp_w2", parts[6], parts[7], (mlp_w2, m_mlp_w2, v_mlp_w2))):
        res = [_reduce_adamw(p, w[l], mm[l], vv[l], f"adamw_{key}_{l}") for l, p in enumerate((pa, pb))]
        big[key] = [jnp.stack([res[0][i], res[1][i]], axis=0) for i in range(4)]

    d_norm_mix = jnp.concatenate([dg_mix0, dg_mix1], axis=0)
    d_norm_mlp = jnp.concatenate([dg_mlp0, dg_mlp1], axis=0)
    d_conv_b = conv_sums[4:5]
    d_conv_w = conv_sums[0:4]
    head = hsums[:, :, :8]
    d_dt_bias, d_a_log, d_d = (head[:, k, :].reshape(1, SSM_HEADS) for k in range(3))
    d_ssm_norm = csums[0:1]
    small = [d_norm_mix, d_norm_mlp, d_conv_b, d_dt_bias, d_a_log, d_d, d_ssm_norm, d_final, d_conv_w, loss_acc[0:1, 0:1]]
    shapes = [a.shape for a in small]
    summed = _unpack(_all_reduce_small(_pack(small), "reduce_small"), shapes)
    g_conv_w_full = summed[8]
    loss = summed[9].reshape(())
    g_conv_w = lax.dynamic_slice(g_conv_w_full, (0, me * (CONV_DIM // NDEV)), (CONV_WIDTH, CONV_DIM // NDEV))

    small_names = ["norm_mix", "norm_mlp", "ssm_conv_b", "ssm_dt_bias", "ssm_a_log", "ssm_d", "ssm_norm_w", "final_norm"]
    small_w = [norm_mix, norm_mlp, ssm_conv_b, ssm_dt_bias, ssm_a_log, ssm_d, ssm_norm_w, final_norm, ssm_conv_w]
    small_m = [m_norm_mix, m_norm_mlp, m_ssm_conv_b, m_ssm_dt_bias, m_ssm_a_log, m_ssm_d, m_ssm_norm_w, m_final_norm, m_ssm_conv_w]
    small_v = [v_norm_mix, v_norm_mlp, v_ssm_conv_b, v_ssm_dt_bias, v_ssm_a_log, v_ssm_d, v_ssm_norm_w, v_final_norm, v_ssm_conv_w]
    small_g = [summed[i].reshape(small_w[i].shape) for i in range(8)] + [g_conv_w.reshape(ssm_conv_w.shape)]
    wshapes = [a.shape for a in small_w]
    sd, sm, sv = _adamw(_pack(small_g), _pack(small_w), _pack(small_m), _pack(small_v), "adamw_small")
    sd, sm, sv = _unpack(sd, wshapes), _unpack(sm, wshapes), _unpack(sv, wshapes)
    res = {n: (small_g[i], sd[i], sm[i], sv[i]) for i, n in enumerate(small_names + ["ssm_conv_w"])}
    for n in big:
        res[n] = tuple(big[n])

    order = ["norm_mix", "norm_mlp", "ssm_w_in", "ssm_conv_w", "ssm_conv_b", "ssm_dt_bias", "ssm_a_log", "ssm_d",
             "ssm_norm_w", "ssm_w_out", "attn_w_qkv", "attn_w_o", "mlp_w1", "mlp_w2", "final_norm"]
    outs = [loss, dx0.reshape(x.shape)]
    for kind in range(4):
        outs += [res[n][kind] for n in order]
    return tuple(outs)
```

```python
import math

import jax
import jax.numpy as jnp
from jax import lax
from jax.experimental import pallas as pl
from jax.experimental.pallas import tpu as pltpu

F32, BF16 = jnp.float32, jnp.bfloat16
SDS = jax.ShapeDtypeStruct
MESH = pl.DeviceIdType.MESH

NDEV = 8
D_MODEL = 2048
D_INNER = 4096
SSM_HEADS = 64
SSM_HEAD_DIM = 64
SSM_GROUPS = 8
SSM_STATE = 128
CHUNK = 128
CONV_DIM = 6144
CONV_WIDTH = 4
GROUP_W = D_INNER // SSM_GROUPS
ATTN_GROUPS = ((128, 1), (512, 4), (2048, 16))
ATTN_W = 1024
QKV_DIM = 9216
D_FF = 8192
EPS = 1e-5
ADAM_LR, ADAM_B1, ADAM_B2, ADAM_EPS, ADAM_WD, ADAM_STEP = 0.001, 0.9, 0.999, 1e-08, 0.01, 10

VMEM_LIMIT = 48 * 1024 * 1024


def _params(sem):
    return pltpu.CompilerParams(dimension_semantics=sem, vmem_limit_bytes=VMEM_LIMIT)


def _sigmoid(v):
    return 1.0 / (1.0 + jnp.exp(-v))


def _softplus(v):
    return jnp.maximum(v, 0.0) + jnp.log1p(jnp.exp(-jnp.abs(v)))


MM_SUB = 256


def _wspec(layout, kw, nw, tr, tc, sel):
    if layout == "plain":
        return pl.BlockSpec((tr, tc), lambda *g: sel(*g))
    per = (nw // NDEV) // tc
    return pl.BlockSpec((None, tr, tc), lambda *g: (sel(*g)[1] // per, sel(*g)[0], sel(*g)[1] % per))


def _wshape(layout, kw, nw):
    return (kw, nw) if layout == "plain" else (NDEV, kw, nw // NDEV)


def _mm_call(name, grid, in_specs, out_specs, out_shape, dims, n_extra, epi, tm, tn, carry):
    nk = grid[2]
    steps = grid[0] * grid[1] * nk
    kind, moved = carry[:2] if carry else (None, ())
    spans = carry[2] if carry and len(carry) > 2 else [None] * len(moved)
    nc = len(moved)
    n_out = len(out_shape)

    def body(*refs):
        a_ref, b_ref = refs[0], refs[1]
        extra = refs[2:2 + n_extra]
        c_in = refs[2 + n_extra:2 + n_extra + nc]
        outs = refs[2 + n_extra + nc:2 + n_extra + nc + n_out]
        c_out = refs[2 + n_extra + nc + n_out:2 + n_extra + 2 * nc + n_out]
        acc = refs[2 + n_extra + 2 * nc + n_out]
        sems = refs[3 + n_extra + 2 * nc + n_out:]
        i, j, k = pl.program_id(0), pl.program_id(1), pl.program_id(2)
        step = (i * grid[1] + j) * nk + k
        if nc:
            @pl.when(step == 0)
            def _():
                if kind == "gather":
                    _two_level("start", c_in, c_out, *sems)
                else:
                    _exchange_start(spans, c_in, c_out, *sems)

            if kind == "gather":
                @pl.when(step == (2 * steps) // 3)
                def _():
                    _two_level("pass", c_in, c_out, *sems)

        def finish(total):
            vals = epi(total, *[e[...] for e in extra])
            for o, v in zip(outs, vals):
                o[...] = v.astype(o.dtype)

        if nk == 1:
            finish(lax.dot_general(a_ref[...].astype(BF16), b_ref[...].astype(BF16), (dims, ((), ())),
                                   preferred_element_type=F32))
        else:
            @pl.when(k == 0)
            def _():
                acc[...] = jnp.zeros_like(acc)

            a = a_ref[...].astype(BF16)
            sub = MM_SUB if tn % MM_SUB == 0 else tn
            for c in range(tn // sub):
                cols = slice(c * sub, (c + 1) * sub)
                b_c = b_ref[cols, :] if dims[1] == (1,) else b_ref[:, cols]
                acc[:, cols] += lax.dot_general(a, b_c.astype(BF16), (dims, ((), ())), preferred_element_type=F32)

            @pl.when(k == nk - 1)
            def _():
                finish(acc[...])

        if nc:
            @pl.when(step == steps - 1)
            def _():
                if kind == "gather":
                    _two_level("wait", c_in, c_out, *sems)
                else:
                    _exchange_wait(spans, c_in, c_out, *sems)

    hbm = pl.BlockSpec(memory_space=pl.ANY)
    scratch = [pltpu.VMEM((tm, tn), F32)] + (_exchange_sems(nc) if nc else [])
    c_shape = [SDS(((NDEV,) + t.shape) if kind == "gather" else t.shape, t.dtype) for t in moved]
    sem = ("arbitrary",) * 3 if nc else ("parallel", "parallel", "arbitrary")
    return pl.pallas_call(
        body, grid=grid, in_specs=in_specs + [hbm] * nc, out_specs=out_specs + [hbm] * nc,
        out_shape=out_shape + c_shape, scratch_shapes=scratch, name=name, compiler_params=_params(sem))


def _mm_fwd(a, w, layout, kw, nw, *, epi, outs, extras=(), tm=1024, tn=1024, tk=2048, carry=None, name):
    m = a.shape[0]
    tm, tn, tk = min(tm, m), min(tn, nw), min(tk, kw)
    grid = (m // tm, nw // tn, kw // tk)
    o_spec = pl.BlockSpec((tm, tn), lambda i, j, k: (i, j))
    in_specs = [pl.BlockSpec((tm, tk), lambda i, j, k: (i, k)), _wspec(layout, kw, nw, tk, tn, lambda i, j, k: (k, j))]
    in_specs += [o_spec] * len(extras)
    call = _mm_call(name, grid, in_specs, [o_spec] * len(outs), [SDS((m, nw), dt) for dt in outs], ((1,), (0,)),
                    len(extras), epi, tm, tn, carry)
    return call(a, w, *extras, *(carry[1] if carry else ()))


def _mm_dx(g, w, layout, kw, nw, *, epi, outs, extras=(), tm=1024, tn=1024, tk=2048, carry=None, name):
    m = g.shape[0]
    tm, tn, tk = min(tm, m), min(tn, kw), min(tk, nw if layout == "plain" else nw // NDEV)
    grid = (m // tm, kw // tn, nw // tk)
    o_spec = pl.BlockSpec((tm, tn), lambda i, j, k: (i, j))
    in_specs = [pl.BlockSpec((tm, tk), lambda i, j, k: (i, k)), _wspec(layout, kw, nw, tn, tk, lambda i, j, k: (j, k))]
    in_specs += [o_spec] * len(extras)
    call = _mm_call(name, grid, in_specs, [o_spec] * len(outs), [SDS((m, kw), dt) for dt in outs], ((1,), (1,)),
                    len(extras), epi, tm, tn, carry)
    return call(g, w, *extras, *(carry[1] if carry else ()))


def _mm_dw(a, g, layout, *, tm=1024, tn=1024, tk=2048, carry=None, name):
    m, kw = a.shape
    nw = g.shape[1]
    tm, tn, tk = min(tm, kw), min(tn, nw if layout == "plain" else nw // NDEV), min(tk, m)
    grid = (kw // tm, nw // tn, m // tk)
    in_specs = [pl.BlockSpec((tk, tm), lambda i, j, k: (k, i)), pl.BlockSpec((tk, tn), lambda i, j, k: (k, j))]
    o_spec = _wspec(layout, kw, nw, tm, tn, lambda i, j, k: (i, j))
    call = _mm_call(name, grid, in_specs, [o_spec], [SDS(_wshape(layout, kw, nw), BF16)], ((0,), (0,)),
                    0, lambda t: (t,), tm, tn, carry)
    return call(a, g, *(carry[1] if carry else ()))


ROW_TILE = 256


def _rmsnorm_fwd(x, g, name):
    t, d = x.shape

    def body(x_ref, g_ref, h_ref):
        xv = x_ref[...]
        r = lax.rsqrt(jnp.mean(xv * xv, axis=-1, keepdims=True) + EPS)
        h_ref[...] = (xv * r * g_ref[...]).astype(BF16)

    row = pl.BlockSpec((ROW_TILE, d), lambda i: (i, 0))
    vec = pl.BlockSpec((1, d), lambda i: (0, 0))
    return pl.pallas_call(body, grid=(t // ROW_TILE,), in_specs=[row, vec], out_specs=row, out_shape=SDS((t, d), BF16),
                          name=name, compiler_params=_params(("parallel",)))(x, g)


def _rmsnorm_bwd(x, g, dh, dres, name):
    t, d = x.shape

    def body(x_ref, g_ref, dh_ref, dres_ref, dx_ref, dxb_ref, dg_ref):
        xv = x_ref[...]
        r = lax.rsqrt(jnp.mean(xv * xv, axis=-1, keepdims=True) + EPS)
        xh = xv * r
        dhv = dh_ref[...]
        gd = dhv * g_ref[...]
        dx = dres_ref[...] + r * (gd - xh * jnp.mean(gd * xh, axis=-1, keepdims=True))
        dx_ref[...] = dx
        dxb_ref[...] = dx.astype(BF16)
        part = jnp.sum(dhv * xh, axis=0, keepdims=True)

        @pl.when(pl.program_id(0) == 0)
        def _():
            dg_ref[...] = part

        @pl.when(pl.program_id(0) > 0)
        def _():
            dg_ref[...] += part

    row = pl.BlockSpec((ROW_TILE, d), lambda i: (i, 0))
    vec = pl.BlockSpec((1, d), lambda i: (0, 0))
    return pl.pallas_call(body, grid=(t // ROW_TILE,), in_specs=[row, vec, row, row], out_specs=[row, row, vec],
                          out_shape=[SDS((t, d), F32), SDS((t, d), BF16), SDS((1, d), F32)], name=name,
                          compiler_params=_params(("arbitrary",)))(x, g, dh, dres)


def _loss_head(x, tgt, g, name):
    t, d = x.shape

    def body(x_ref, t_ref, g_ref, dx_ref, dxb_ref, loss_ref, dg_ref):
        xv = x_ref[...]
        r = lax.rsqrt(jnp.mean(xv * xv, axis=-1, keepdims=True) + EPS)
        xh = xv * r
        gv = g_ref[...]
        err = xh * gv - t_ref[...]
        part_loss = 0.5 * jnp.sum(jnp.mean(err * err, axis=-1, keepdims=True), axis=0, keepdims=True)
        dy = err * (1.0 / d)
        gd = dy * gv
        dx = r * (gd - xh * jnp.mean(gd * xh, axis=-1, keepdims=True))
        dx_ref[...] = dx
        dxb_ref[...] = dx.astype(BF16)
        part_g = jnp.sum(dy * xh, axis=0, keepdims=True)
        part_l = jnp.broadcast_to(part_loss, (8, 128))

        @pl.when(pl.program_id(0) == 0)
        def _():
            dg_ref[...] = part_g
            loss_ref[...] = part_l

        @pl.when(pl.program_id(0) > 0)
        def _():
            dg_ref[...] += part_g
            loss_ref[...] += part_l

    row = pl.BlockSpec((ROW_TILE, d), lambda i: (i, 0))
    vec = pl.BlockSpec((1, d), lambda i: (0, 0))
    sc = pl.BlockSpec((8, 128), lambda i: (0, 0))
    return pl.pallas_call(body, grid=(t // ROW_TILE,), in_specs=[row, row, vec], out_specs=[row, row, sc, vec],
                          out_shape=[SDS((t, d), F32), SDS((t, d), BF16), SDS((8, 128), F32), SDS((1, d), F32)], name=name,
                          compiler_params=_params(("arbitrary",)))(x, tgt, g)


CONV_ROWS = 256
CONV_COLS = 2048


def _shift_down(cur, prev8, k):
    sh = pltpu.roll(cur, k, axis=0)
    ph = pltpu.roll(prev8, k, axis=0)
    rid = lax.broadcasted_iota(jnp.int32, ph.shape, 0)
    head = jnp.where(rid < k, ph, sh[0:8])
    return jnp.concatenate([head, sh[8:]], axis=0)


def _shift_up(cur, next8, k):
    n = cur.shape[0]
    sh = pltpu.roll(cur, n - k, axis=0)
    nh = pltpu.roll(next8, 8 - k, axis=0)
    rid = lax.broadcasted_iota(jnp.int32, nh.shape, 0)
    tail = jnp.where(rid >= 8 - k, nh, sh[n - 8:])
    return jnp.concatenate([sh[:n - 8], tail], axis=0)


def _conv_pre(cur, prev8, w, b):
    acc = w[3:4, :] * cur + b
    for k in range(1, CONV_WIDTH):
        acc = acc + w[3 - k:4 - k, :] * _shift_down(cur, prev8, k)
    return acc


def _conv_fwd(u, w, b, name):
    t, c = u.shape
    per = CONV_ROWS // 8

    def body(u_ref, p_ref, w_ref, b_ref, o_ref):
        prev8 = jnp.where(pl.program_id(1) == 0, 0.0, p_ref[...])
        pre = _conv_pre(u_ref[...], prev8, w_ref[...], b_ref[...])
        o_ref[...] = pre * _sigmoid(pre)

    cur = pl.BlockSpec((CONV_ROWS, CONV_COLS), lambda j, i: (i, j))
    prev = pl.BlockSpec((8, CONV_COLS), lambda j, i: (jnp.maximum(i * per - 1, 0), j))
    wsp = pl.BlockSpec((CONV_WIDTH, CONV_COLS), lambda j, i: (0, j))
    bsp = pl.BlockSpec((1, CONV_COLS), lambda j, i: (0, j))
    return pl.pallas_call(body, grid=(c // CONV_COLS, t // CONV_ROWS), in_specs=[cur, prev, wsp, bsp], out_specs=cur,
                          out_shape=SDS((t, c), F32), name=name, compiler_params=_params(("parallel", "parallel")))(u, u, w, b)


def _conv_bwd(u, w, b, dout, du_prev, col0, name):
    t, c = u.shape
    per = CONV_ROWS // 8
    last = t // CONV_ROWS - 1

    def dsilu(pre, d):
        s = _sigmoid(pre)
        return d * (s * (1.0 + pre * (1.0 - s)))

    def body(u_ref, p_ref, n_ref, w_ref, b_ref, d_ref, dn_ref, *rest):
        du_ref, dw_ref = rest[-2:]
        i = pl.program_id(1)
        cur = u_ref[...]
        wv, bv = w_ref[...], b_ref[...]
        prev8 = jnp.where(i == 0, 0.0, p_ref[...])
        dpre = dsilu(_conv_pre(cur, prev8, wv, bv), d_ref[...])
        nxt, tail = n_ref[...], cur[CONV_ROWS - 8:]
        rid = lax.broadcasted_iota(jnp.int32, nxt.shape, 0)
        pre_n = wv[3:4, :] * nxt + bv
        for k in range(1, CONV_WIDTH):
            pre_n = pre_n + wv[3 - k:4 - k, :] * jnp.where(rid < k, pltpu.roll(tail, k, axis=0), pltpu.roll(nxt, k, axis=0))
        dpre_n = jnp.where(i == last, 0.0, dsilu(pre_n, dn_ref[...]))
        du = wv[3:4, :] * dpre
        for k in range(1, CONV_WIDTH):
            du = du + wv[3 - k:4 - k, :] * _shift_up(dpre, dpre_n, k)
        du_ref[...] = du.astype(BF16)
        rows = [jnp.sum(dpre * _shift_down(cur, prev8, 3 - k), axis=0, keepdims=True) for k in range(3)]
        rows.append(jnp.sum(dpre * cur, axis=0, keepdims=True))
        rows.append(jnp.sum(dpre, axis=0, keepdims=True))
        part = jnp.concatenate(rows + [jnp.zeros((3, cur.shape[1]), F32)], axis=0)

        @pl.when(i == 0)
        def _():
            dw_ref[...] = part

        @pl.when(i > 0)
        def _():
            dw_ref[...] += part

    width = dout.shape[1]
    tc = min(CONV_COLS, width)
    ob = col0 // tc

    def at(rows, here):
        off = ob if here else 0
        if rows == CONV_ROWS:
            return pl.BlockSpec((rows, tc), lambda j, i: (i, off + j))
        if rows == -8:
            return pl.BlockSpec((8, tc), lambda j, i: (jnp.maximum(i * per - 1, 0), off + j))
        if rows == 8:
            return pl.BlockSpec((8, tc), lambda j, i: (jnp.minimum((i + 1) * per, t // 8 - 1), off + j))
        return pl.BlockSpec((rows, tc), lambda j, i: (0, off + j))

    in_specs = [at(CONV_ROWS, True), at(-8, True), at(8, True), at(CONV_WIDTH, True), at(1, True),
                at(CONV_ROWS, False), at(8, False)]
    args = [u, u, u, w, b, dout, dout]
    aliases = {}
    if du_prev is not None:
        in_specs.append(pl.BlockSpec(memory_space=pl.ANY))
        args.append(du_prev)
        aliases = {len(args) - 1: 0}
    acc = pl.BlockSpec((8, tc), lambda j, i: (0, j))
    return pl.pallas_call(body, grid=(width // tc, t // CONV_ROWS), in_specs=in_specs,
                          out_specs=[at(CONV_ROWS, True), acc], out_shape=[SDS((t, c), BF16), SDS((8, width), F32)],
                          input_output_aliases=aliases, name=name,
                          compiler_params=_params(("parallel", "arbitrary")))(*args)


def _pieces(v, n):
    out, rest = [], v
    for _ in range(n):
        p = rest.astype(BF16)
        out.append(p)
        rest = rest - p.astype(F32)
    return out


def _head_expand(v):
    r = lax.broadcasted_iota(jnp.int32, (3 * 128, GROUP_W), 0) % 128
    c = lax.broadcasted_iota(jnp.int32, (3 * 128, GROUP_W), 1)
    return _dot(jnp.concatenate(_pieces(v, 3), axis=1), (c // SSM_HEAD_DIM == r).astype(BF16))


def _head_sum(vs):
    r = lax.broadcasted_iota(jnp.int32, (2 * GROUP_W, 128), 0) % GROUP_W
    c = lax.broadcasted_iota(jnp.int32, (2 * GROUP_W, 128), 1)
    stacked = jnp.concatenate([jnp.concatenate(_pieces(v, 2), axis=1) for v in vs], axis=0)
    out = _dot(stacked, (r // SSM_HEAD_DIM == c).astype(BF16))
    res, at = [], 0
    for v in vs:
        res.append(out[at:at + v.shape[0]])
        at += v.shape[0]
    return res


def _tri_left(tri, v):
    r = _dot(tri.astype(BF16), jnp.concatenate(_pieces(v, 3), axis=1))
    return r[:, 0:128] + r[:, 128:256] + r[:, 256:384]


def _dot(a, b):
    return jnp.dot(a, b, preferred_element_type=F32)


def _dot_nt(a, b):
    return lax.dot_general(a, b, (((1,), (1,)), ((), ())), preferred_element_type=F32)


def _dot_tn(a, b):
    return lax.dot_general(a, b, (((0,), (0,)), ((), ())), preferred_element_type=F32)


def _ssd_common(dtp, dtpt, bias, biasr, alog, alogr):
    li = lax.broadcasted_iota(jnp.int32, (CHUNK, CHUNK), 0)
    si = lax.broadcasted_iota(jnp.int32, (CHUNK, CHUNK), 1)
    lower, upper = (li >= si), (li <= si)
    dt = _softplus(dtp + bias)
    a_neg = -jnp.exp(alog)
    cs = _tri_left(lower, dt * a_neg)
    dtr = _softplus(dtpt + biasr)
    ar = jnp.concatenate([dtr * (-jnp.exp(alogr)), jnp.zeros((8, CHUNK), F32)], axis=0)
    r3 = lax.broadcasted_iota(jnp.int32, (3 * CHUNK, CHUNK), 0) % CHUNK
    c3 = lax.broadcasted_iota(jnp.int32, (3 * CHUNK, CHUNK), 1)
    csr = _dot(jnp.concatenate(_pieces(ar, 3), axis=1), (r3 <= c3).astype(BF16))[0:8]
    return lower, upper, dt, a_neg, cs, csr


def _head_masked_rows(v):
    hl = lax.broadcasted_iota(jnp.int32, v.shape, 1) // SSM_HEAD_DIM
    return jnp.concatenate([jnp.where(hl == j, v, jnp.zeros_like(v)) for j in range(8)], axis=0)


SSD_STEP_CHUNKS = 8


def _chunk_view(ref, kind, k):
    if kind == "rows":
        return ref.at[pl.ds(k * CHUNK, CHUNK), :]
    if kind == "lanes":
        return ref.at[:, pl.ds(k * CHUNK, CHUNK)]
    if kind == "lead":
        return ref.at[k]
    return ref


def _ssd_fwd(xbc, z, dtp_g, dtp_t, bias_g, bias_r, alog_g, alog_r, d_exp, norm_w, name):
    t = xbc.shape[0]
    nc = t // CHUNK

    def body(*refs):
        @pl.when(pl.program_id(1) == 0)
        def _():
            refs[-1][...] = jnp.zeros_like(refs[-1])

        for k in range(SSD_STEP_CHUNKS):
            chunk(*[_chunk_view(r, kind, k) for r, kind in zip(refs, kinds)])

    kinds = ["rows"] * 5 + ["lanes"] + [None] * 6 + ["rows", "rows", "lead", None]

    def chunk(xs_ref, b_ref, c_ref, z_ref, dtp_ref, dtpt_ref, bias_ref, biasr_ref, alog_ref, alogr_ref, dexp_ref, nw_ref,
              y_ref, yn_ref, st_ref, state):
        lower, _, dt, a_neg, cs, csr = _ssd_common(dtp_ref[...], dtpt_ref[...], bias_ref[...], biasr_ref[...],
                                                alog_ref[...], alogr_ref[...])
        cs_e = _head_expand(cs)
        dt_e = _head_expand(dt)
        xs = xs_ref[...]
        xdt = xs * dt_e
        bm = b_ref[...]
        cm = c_ref[...]
        bmb, cmb = bm.astype(BF16), cm.astype(BF16)
        cb = _dot_nt(cmb, bmb)
        ms = []
        for j in range(8):
            dlt = cs_e[:, SSM_HEAD_DIM * j:SSM_HEAD_DIM * j + 1] - csr[j:j + 1, :]
            ms.append((cb * jnp.exp(jnp.where(lower, dlt, -jnp.inf))).astype(BF16))
        y = _dot(jnp.concatenate(ms, axis=1), _head_masked_rows(xdt.astype(BF16)))
        st_in = state[...]
        st_ref[...] = st_in
        y = y + jnp.exp(cs_e) * _dot(cmb, st_in.astype(BF16))
        cs_last = cs_e[CHUNK - 1:CHUNK, :]
        xdtd = (xdt * jnp.exp(cs_last - cs_e)).astype(BF16)
        state[...] = jnp.exp(cs_last) * st_in + _dot(bm.T.astype(BF16), xdtd)
        y_ref[...] = y
        zz = z_ref[...]
        y2 = (y + dexp_ref[...] * xs) * (zz * _sigmoid(zz))
        r = lax.rsqrt(jnp.mean(y2 * y2, axis=-1, keepdims=True) + EPS)
        yn_ref[...] = (y2 * r * nw_ref[...]).astype(BF16)

    rows = SSD_STEP_CHUNKS * CHUNK
    gw = pl.BlockSpec((rows, GROUP_W), lambda g, c: (c, g))
    in_specs = [
        gw,
        pl.BlockSpec((rows, SSM_STATE), lambda g, c: (c, D_INNER // SSM_STATE + g)),
        pl.BlockSpec((rows, SSM_STATE), lambda g, c: (c, D_INNER // SSM_STATE + SSM_GROUPS + g)),
        gw,
        pl.BlockSpec((None, rows, 128), lambda g, c: (g, c, 0)),
        pl.BlockSpec((8, rows), lambda g, c: (g, c)),
        pl.BlockSpec((None, 1, 128), lambda g, c: (g, 0, 0)),
        pl.BlockSpec((8, 128), lambda g, c: (g, 0)),
        pl.BlockSpec((None, 1, 128), lambda g, c: (g, 0, 0)),
        pl.BlockSpec((8, 128), lambda g, c: (g, 0)),
        pl.BlockSpec((1, GROUP_W), lambda g, c: (0, g)),
        pl.BlockSpec((1, GROUP_W), lambda g, c: (0, g)),
    ]
    out_specs = [gw, gw, pl.BlockSpec((SSD_STEP_CHUNKS, SSM_STATE, GROUP_W), lambda g, c: (c, 0, g))]
    out_shape = [SDS((t, D_INNER), F32), SDS((t, D_INNER), BF16), SDS((nc, SSM_STATE, D_INNER), F32)]
    return pl.pallas_call(body, grid=(SSM_GROUPS, nc // SSD_STEP_CHUNKS), in_specs=in_specs, out_specs=out_specs,
                          out_shape=out_shape,
                          scratch_shapes=[pltpu.VMEM((SSM_STATE, GROUP_W), F32)], name=name,
                          compiler_params=_params(("parallel", "arbitrary")))(
        xbc, xbc, xbc, z, dtp_g, dtp_t, bias_g, bias_r, alog_g, alog_r, d_exp, norm_w)


def _ssd_bwd(dyn, y, xbc, z, states, dtp_g, dtp_t, bias_g, bias_r, alog_g, alog_r, d_exp, norm_w, name):
    t = xbc.shape[0]
    nc = t // CHUNK

    def body(*refs):
        step = pl.program_id(1)
        hsum_ref, csum_ref, dstate = refs[-3:]

        @pl.when(step == 0)
        def _():
            dstate[...] = jnp.zeros_like(dstate)

        parts = [chunk(*[_chunk_view(r, kind, k) for r, kind in zip(refs, kinds)])
                 for k in reversed(range(SSD_STEP_CHUNKS))]
        hpart, cpart = parts[0]
        for hp, cp in parts[1:]:
            hpart, cpart = hpart + hp, cpart + cp

        @pl.when(step == 0)
        def _():
            hsum_ref[...] = hpart
            csum_ref[...] = cpart

        @pl.when(step > 0)
        def _():
            hsum_ref[...] += hpart
            csum_ref[...] += cpart

    kinds = ["rows"] * 6 + ["lead", "rows", "lanes"] + [None] * 6 + ["rows"] * 5 + [None] * 3

    def chunk(dyn_ref, y_ref, xs_ref, b_ref, c_ref, z_ref, st_ref, dtp_ref, dtpt_ref, bias_ref, biasr_ref, alog_ref,
              alogr_ref, dexp_ref, nw_ref, dxs_ref, db_ref, dc_ref, dz_ref, ddt_ref, hsum_ref, csum_ref, dstate):
        dtp = dtp_ref[...]
        bias = bias_ref[...]
        lower, upper, dt, a_neg, cs, csr = _ssd_common(dtp, dtpt_ref[...], bias, biasr_ref[...], alog_ref[...],
                                                       alogr_ref[...])
        cs_e = _head_expand(cs)
        dt_e = _head_expand(dt)
        xs = xs_ref[...]
        xdt = xs * dt_e
        bm = b_ref[...]
        cm = c_ref[...]
        bmb, cmb = bm.astype(BF16), cm.astype(BF16)
        y = y_ref[...]
        dexp = dexp_ref[...]
        nw = nw_ref[...]

        zz = z_ref[...]
        sg = _sigmoid(zz)
        gate = zz * sg
        ytot = y + dexp * xs
        y2 = ytot * gate
        r = lax.rsqrt(jnp.mean(y2 * y2, axis=-1, keepdims=True) + EPS)
        dynv = dyn_ref[...]
        xh = y2 * r
        gn = dynv * nw
        dy2 = r * (gn - xh * jnp.mean(gn * xh, axis=-1, keepdims=True))
        dy = dy2 * gate
        dz_ref[...] = (dy2 * ytot * (sg * (1.0 + zz * (1.0 - sg)))).astype(BF16)
        csum_part = jnp.sum(dynv * xh, axis=0, keepdims=True)

        cb = _dot_nt(cmb, bmb)
        dyb = dy.astype(BF16)
        xdtb = xdt.astype(BF16)
        dym = _head_masked_rows(dyb)
        dm = _dot_nt(dym, xdtb)
        dmt = _dot_nt(_head_masked_rows(xdtb), dyb)
        lane = lax.broadcasted_iota(jnp.int32, (CHUNK, 128), 1)
        mts = []
        dcb = jnp.zeros((CHUNK, CHUNK), F32)
        dcs = jnp.zeros((CHUNK, 128), F32)
        for j in range(8):
            dlt = cs_e[:, SSM_HEAD_DIM * j:SSM_HEAD_DIM * j + 1] - csr[j:j + 1, :]
            lj = jnp.exp(jnp.where(lower, dlt, -jnp.inf))
            mj = cb * lj
            mjt = mj.T
            mts.append(mjt.astype(BF16))
            dmj = dm[CHUNK * j:CHUNK * (j + 1)]
            dcb = dcb + dmj * lj
            rows = jnp.sum(dmj * mj, axis=1, keepdims=True)
            cols = jnp.sum(dmt[CHUNK * j:CHUNK * (j + 1)] * mjt, axis=1, keepdims=True)
            dcs = dcs + jnp.where(lane == j, rows - cols, 0.0)
        dxdt = _dot(jnp.concatenate(mts, axis=1), dym)
        dst_out = dstate[...]
        dst_outb = dst_out.astype(BF16)
        st_in = st_ref[...]
        st_inb = st_in.astype(BF16)
        cs_last = cs_e[CHUNK - 1:CHUNK, :]
        decay = jnp.exp(cs_last - cs_e)
        e_last = jnp.exp(cs_last)
        gpart = decay * _dot(bmb, dst_outb)
        dxdt = dxdt + gpart
        dyw = (jnp.exp(cs_e) * dy).astype(BF16)
        dcbb = dcb.astype(BF16)
        dc_ref[...] = _dot_nt(dyw, st_inb) + _dot(dcbb, bmb)
        db_ref[...] = _dot_nt((xdt * decay).astype(BF16), dst_outb) + _dot(dcb.T.astype(BF16), cmb)
        dstate[...] = e_last * dst_out + _dot(cm.T.astype(BF16), dyw)
        y_off = jnp.exp(cs_e) * _dot(cmb, st_inb)
        xg = xdt * gpart
        vec = jnp.concatenate([jnp.sum(dy * xs, axis=0, keepdims=True),
                               jnp.sum(xg + dst_out * e_last * st_in, axis=0, keepdims=True),
                               jnp.zeros((14, GROUP_W), F32)], axis=0)
        s_cs, s_dt, s_vec = _head_sum([dy * y_off - xg, dxdt * xs, vec])
        d_skip = s_vec[0:1]
        ri = lax.broadcasted_iota(jnp.int32, (CHUNK, 128), 0)
        dcs = dcs + s_cs + jnp.where(ri == CHUNK - 1, s_vec[1:2], 0.0)
        da = _tri_left(upper, dcs)
        ddt = da * a_neg + s_dt
        dxs_ref[...] = dxdt * dt_e + dy * dexp
        ddtp = ddt * _sigmoid(dtp + bias)
        ddt_ref[...] = ddtp
        d_alog = jnp.sum(da * dt, axis=0, keepdims=True) * a_neg
        hpart = jnp.concatenate([jnp.sum(ddtp, axis=0, keepdims=True), d_alog, d_skip, jnp.zeros((5, 128), F32)], axis=0)
        cpart = jnp.concatenate([csum_part, jnp.zeros((7, GROUP_W), F32)], axis=0)
        return hpart, cpart

    steps = nc // SSD_STEP_CHUNKS
    rows = SSD_STEP_CHUNKS * CHUNK
    rc = lambda c: steps - 1 - c
    gw = pl.BlockSpec((rows, GROUP_W), lambda g, c: (rc(c), g))
    bsp = pl.BlockSpec((rows, SSM_STATE), lambda g, c: (rc(c), D_INNER // SSM_STATE + g))
    csp = pl.BlockSpec((rows, SSM_STATE), lambda g, c: (rc(c), D_INNER // SSM_STATE + SSM_GROUPS + g))
    in_specs = [
        gw, gw, gw, bsp, csp, gw,
        pl.BlockSpec((SSD_STEP_CHUNKS, SSM_STATE, GROUP_W), lambda g, c: (rc(c), 0, g)),
        pl.BlockSpec((None, rows, 128), lambda g, c: (g, rc(c), 0)),
        pl.BlockSpec((8, rows), lambda g, c: (g, rc(c))),
        pl.BlockSpec((None, 1, 128), lambda g, c: (g, 0, 0)),
        pl.BlockSpec((8, 128), lambda g, c: (g, 0)),
        pl.BlockSpec((None, 1, 128), lambda g, c: (g, 0, 0)),
        pl.BlockSpec((8, 128), lambda g, c: (g, 0)),
        pl.BlockSpec((1, GROUP_W), lambda g, c: (0, g)),
        pl.BlockSpec((1, GROUP_W), lambda g, c: (0, g)),
    ]
    nsp = pl.BlockSpec((rows, SSM_STATE), lambda g, c: (rc(c), g))
    out_specs = [gw, nsp, nsp, gw,
                 pl.BlockSpec((None, rows, 128), lambda g, c: (g, rc(c), 0)),
                 pl.BlockSpec((None, 8, 128), lambda g, c: (g, 0, 0)),
                 pl.BlockSpec((8, GROUP_W), lambda g, c: (0, g))]
    gn = SSM_GROUPS * SSM_STATE
    out_shape = [SDS((t, D_INNER), F32), SDS((t, gn), F32), SDS((t, gn), F32), SDS((t, D_INNER), BF16),
                 SDS((SSM_GROUPS, t, 128), F32), SDS((SSM_GROUPS, 8, 128), F32), SDS((8, D_INNER), F32)]
    return pl.pallas_call(body, grid=(SSM_GROUPS, steps), in_specs=in_specs, out_specs=out_specs, out_shape=out_shape,
                          scratch_shapes=[pltpu.VMEM((SSM_STATE, GROUP_W), F32)], name=name,
                          compiler_params=_params(("parallel", "arbitrary")))(
        dyn, y, xbc, xbc, xbc, z, states, dtp_g, dtp_t, bias_g, bias_r, alog_g, alog_r, d_exp, norm_w)


BLK = 128
HEAD_PAIRS = ATTN_W // 128
ATTN_SCALE = 0.125
SPAN_BLOCKS = {1: 8, 4: 4, 16: 1}


def _slope_table(group):
    n = len(ATTN_GROUPS) * 16
    tbl = [[2.0 ** (-8.0 * (16 * group + 2 * p + s + 1) / n) if s < 2 else 0.0 for s in range(128)] for p in range(HEAD_PAIRS)]
    return jnp.asarray(tbl, F32)


def _lane_lo(rows):
    return lax.broadcasted_iota(jnp.int32, (rows, 128), 1) < 64


def _rows(start, dil):
    return pl.ds(start, BLK, stride=dil) if dil > 1 else pl.ds(start, BLK)


def _stack_heads(x):
    lo = _lane_lo(BLK)
    return jnp.concatenate([jnp.where(lo, x, jnp.zeros_like(x)), jnp.where(lo, jnp.zeros_like(x), x)], axis=0)


def _pair_cols(x):
    return jnp.concatenate([x[:, 0:1], x[:, 64:65]], axis=0)


def _attn_bias(sl, dil, first_span, last_span):
    qi = lax.broadcasted_iota(jnp.int32, (BLK, 2 * BLK), 0)
    kj = lax.broadcasted_iota(jnp.int32, (BLK, 2 * BLK), 1)
    dist = qi + BLK - kj
    valid = (dist >= 0) & (dist <= BLK)
    distf = dist.astype(F32) * float(dil)

    def stacked(ok, d):
        return jnp.concatenate([jnp.where(ok, -sl[:, h:h + 1] * d, -jnp.inf) for h in range(2)], axis=0)

    first = stacked(valid & ((kj >= BLK) | jnp.logical_not(first_span)), distf)
    after = None
    if last_span is not None:
        after = stacked((kj[:, :BLK] >= qi[:, :BLK]) & jnp.logical_not(last_span), distf[:, :BLK])
    return first, stacked(valid, distf), after


def _attn_specs(group, t):
    _, dil = ATTN_GROUPS[group]
    nblk = SPAN_BLOCKS[dil]
    span, edge = BLK * dil * nblk, BLK * dil
    per = span // edge

    def lane_block(which):
        if which is None:
            return lambda p: p
        return lambda p: 3 * (group * HEAD_PAIRS + p) + which

    def cur(which):
        col = lane_block(which)
        return pl.BlockSpec((span, 128), lambda s, p: (s, col(p)))

    def before(which):
        col = lane_block(which)
        return pl.BlockSpec((edge, 128), lambda s, p: (jnp.maximum(s * per - 1, 0), col(p)))

    def after(which):
        col = lane_block(which)
        return pl.BlockSpec((edge, 128), lambda s, p: (jnp.minimum((s + 1) * per, t // edge - 1), col(p)))

    slopes = pl.BlockSpec((HEAD_PAIRS, 128), lambda s, p: (0, 0))
    return dil, nblk, span, cur, before, after, slopes


def _pair_major(w, inverse=False):
    k = w.shape[0]
    g = len(ATTN_GROUPS)
    if inverse:
        return jnp.transpose(w.reshape(k, g, HEAD_PAIRS, 3, 128), (0, 3, 1, 2, 4)).reshape(k, QKV_DIM)
    return jnp.transpose(w.reshape(k, 3, g, HEAD_PAIRS, 128), (0, 2, 3, 1, 4)).reshape(k, QKV_DIM)


def _attn_fwd(qkv, group, name):
    t = qkv.shape[0]
    dil, nblk, span, cur, before, after, slopes = _attn_specs(group, t)

    def body(q_ref, k_ref, v_ref, kp_ref, vp_ref, sl_ref, o_ref, l_ref):
        first_span = pl.program_id(0) == 0
        bias_first, bias_mid, _ = _attn_bias(sl_ref[pl.ds(pl.program_id(1), 1), :], dil, first_span, None)
        lo_q, lo_k = _lane_lo(BLK), _lane_lo(2 * BLK)
        for r in range(dil):
            kp, vp = kp_ref[_rows(r, dil), :].astype(BF16), vp_ref[_rows(r, dil), :].astype(BF16)
            for b in range(nblk):
                rows = _rows(b * BLK * dil + r, dil)
                kc, vc = k_ref[rows, :].astype(BF16), v_ref[rows, :].astype(BF16)
                k2 = jnp.concatenate([kp, kc], axis=0)
                v2 = jnp.concatenate([vp, vc], axis=0)
                kp, vp = kc, vc
                s = _dot_nt(_stack_heads((q_ref[rows, :] * ATTN_SCALE).astype(BF16)), k2) + (bias_first if b == 0 else bias_mid)
                mx = jnp.max(s, axis=-1, keepdims=True)
                p = jnp.exp(s - mx)
                den = jnp.sum(p, axis=-1, keepdims=True)
                pb = p.astype(BF16)
                o_ref[rows, :] = _dot(pb[:BLK], jnp.where(lo_k, v2, jnp.zeros_like(v2))) / den[:BLK] + \
                    _dot(pb[BLK:], jnp.where(lo_k, jnp.zeros_like(v2), v2)) / den[BLK:]
                lse = mx + jnp.log(den)
                l_ref[rows, :] = jnp.where(lo_q, lse[:BLK], lse[BLK:])

    in_specs = [cur(0), cur(1), cur(2), before(1), before(2), slopes]
    return pl.pallas_call(body, grid=(t // span, HEAD_PAIRS), in_specs=in_specs, out_specs=[cur(None), cur(None)],
                          out_shape=[SDS((t, ATTN_W), F32), SDS((t, ATTN_W), F32)], name=name,
                          compiler_params=_params(("parallel", "parallel")))(qkv, qkv, qkv, qkv, qkv, _slope_table(group))


def _attn_bwd(qkv, do, lse, dvec, dqkv, group, name):
    t = qkv.shape[0]
    dil, nblk, span, cur, before, after, slopes = _attn_specs(group, t)
    nspan = t // span

    def body(q_ref, k_ref, v_ref, kp_ref, vp_ref, qn_ref, do_ref, l_ref, d_ref, don_ref, ln_ref, dn_ref, sl_ref, *rest):
        out_ref, dq_s, dk_s, dv_s = rest[-4:]
        span_id, pair_id = pl.program_id(0), pl.program_id(1)
        bias_first, bias_mid, bias_next = _attn_bias(sl_ref[pl.ds(pair_id, 1), :], dil, span_id == 0, span_id == nspan - 1)
        lo_k = _lane_lo(2 * BLK)
        for r in range(dil):
            kp, vp = kp_ref[_rows(r, dil), :].astype(BF16), vp_ref[_rows(r, dil), :].astype(BF16)
            held = None
            for b in range(nblk + 1):
                last = b == nblk
                rows = _rows(r if last else b * BLK * dil + r, dil)
                qs, dos, ls, dvs = (qn_ref, don_ref, ln_ref, dn_ref) if last else (q_ref, do_ref, l_ref, d_ref)
                qst = _stack_heads((qs[rows, :] * ATTN_SCALE).astype(BF16))
                dost = _stack_heads(dos[rows, :].astype(BF16))
                if last:
                    k2, v2, bias = kp, vp, bias_next
                else:
                    kc, vc = k_ref[rows, :].astype(BF16), v_ref[rows, :].astype(BF16)
                    k2, v2 = jnp.concatenate([kp, kc], axis=0), jnp.concatenate([vp, vc], axis=0)
                    kp, vp = kc, vc
                    bias = bias_first if b == 0 else bias_mid
                p = jnp.exp(_dot_nt(qst, k2) + bias - _pair_cols(ls[rows, :]))
                ds = (p * (_dot_nt(dost, v2) - _pair_cols(dvs[rows, :]))).astype(BF16)
                dk2 = _dot_tn(ds, qst)
                dv2 = _dot_tn(p.astype(BF16), dost)
                if held is not None:
                    dk_s[held[0], :] = held[1] + dk2[:BLK]
                    dv_s[held[0], :] = held[2] + dv2[:BLK]
                if not last:
                    k_heads = jnp.concatenate([jnp.where(lo_k, k2, jnp.zeros_like(k2)),
                                               jnp.where(lo_k, jnp.zeros_like(k2), k2)], axis=0)
                    dq_s[rows, :] = _dot(jnp.concatenate([ds[:BLK], ds[BLK:]], axis=1), k_heads) * ATTN_SCALE
                    held = (rows, dk2[BLK:], dv2[BLK:])
        out_ref[:, 0:128] = dq_s[...]
        out_ref[:, 128:256] = dk_s[...]
        out_ref[:, 256:384] = dv_s[...]

    out_spec = pl.BlockSpec((span, 3 * 128), lambda s, p: (s, group * HEAD_PAIRS + p))
    in_specs = [cur(0), cur(1), cur(2), before(1), before(2), after(0), cur(None), cur(None), cur(None),
                after(None), after(None), after(None), slopes]
    args = [qkv, qkv, qkv, qkv, qkv, qkv, do, lse, dvec, do, lse, dvec, _slope_table(group)]
    aliases = {}
    if dqkv is not None:
        in_specs.append(pl.BlockSpec(memory_space=pl.ANY))
        args.append(dqkv)
        aliases = {len(args) - 1: 0}
    return pl.pallas_call(body, grid=(nspan, HEAD_PAIRS), in_specs=in_specs, out_specs=out_spec,
                          out_shape=SDS((t, QKV_DIM), F32), input_output_aliases=aliases,
                          scratch_shapes=[pltpu.VMEM((span, 128), F32)] * 3, name=name,
                          compiler_params=_params(("parallel", "parallel")))(*args)


def _combine_weights(l0, l1, l2):
    mx = jnp.maximum(jnp.maximum(l0, l1), l2)
    e0, e1, e2 = jnp.exp(l0 - mx), jnp.exp(l1 - mx), jnp.exp(l2 - mx)
    den = e0 + e1 + e2
    return e0 / den, e1 / den, e2 / den


def _combine_fwd(os_, ls_, name):
    t = os_[0].shape[0]

    def body(o0, o1, o2, l0, l1, l2, out):
        w0, w1, w2 = _combine_weights(l0[...], l1[...], l2[...])
        out[...] = (w0 * o0[...] + w1 * o1[...] + w2 * o2[...]).astype(BF16)

    row = pl.BlockSpec((ROW_TILE, ATTN_W), lambda i: (i, 0))
    return pl.pallas_call(body, grid=(t // ROW_TILE,), in_specs=[row] * 6, out_specs=row, out_shape=SDS((t, ATTN_W), BF16),
                          name=name, compiler_params=_params(("parallel",)))(*os_, *ls_)


def _combine_bwd(do, os_, ls_, name):
    t = do.shape[0]

    def body(do_ref, o0, o1, o2, l0, l1, l2, g0, g1, g2, d0, d1, d2):
        w0, w1, w2 = _combine_weights(l0[...], l1[...], l2[...])
        dov = do_ref[...]
        prod = dov * (w0 * o0[...] + w1 * o1[...] + w2 * o2[...])
        r = (lax.broadcasted_iota(jnp.int32, (3 * 128, 128), 0) % 128) // 64
        c = lax.broadcasted_iota(jnp.int32, (3 * 128, 128), 1) // 64
        same = (r == c).astype(BF16)
        tbar = jnp.concatenate([_dot(jnp.concatenate(_pieces(prod[:, 128 * k:128 * k + 128], 3), axis=1), same)
                                for k in range(HEAD_PAIRS)], axis=1)
        for w, g, d in ((w0, g0, d0), (w1, g1, d1), (w2, g2, d2)):
            g[...] = w * dov
            d[...] = w * tbar

    row = pl.BlockSpec((ROW_TILE, ATTN_W), lambda i: (i, 0))
    return pl.pallas_call(body, grid=(t // ROW_TILE,), in_specs=[row] * 7, out_specs=[row] * 6,
                          out_shape=[SDS((t, ATTN_W), F32)] * 6, name=name,
                          compiler_params=_params(("parallel",)))(do, *os_, *ls_)


def _peer(k):
    x, y, c = lax.axis_index("x"), lax.axis_index("y"), lax.axis_index("c")
    px = 1 - x if k & 4 else x
    py = 1 - y if k & 2 else y
    pc = 1 - c if k & 1 else c
    return (px, py, pc), 4 * px + 2 * py + pc


def _my_index():
    return 4 * lax.axis_index("x") + 2 * lax.axis_index("y") + lax.axis_index("c")


def _exchange_sems(n):
    return [pltpu.SemaphoreType.DMA((n * (NDEV - 1),)), pltpu.SemaphoreType.DMA((n * (NDEV - 1),)),
            pltpu.SemaphoreType.DMA((n,))]


def _scatter_copies(spans, ins, outs, send, recv, local, arrivals):
    me = _my_index()
    own, sent, arriving = [], [], []
    for i in range(len(ins)):
        def part(ref, slot, i=i):
            return ref.at[slot] if spans[i] is None else ref.at[slot, pl.ds(spans[i][0], spans[i][1])]
        own.append(pltpu.make_async_copy(part(ins[i], me), part(outs[i], me), local.at[i]))
        for k in range(1, NDEV):
            peer, pidx = _peer(k)
            s = i * (NDEV - 1) + k - 1
            for dst, into in ((part(outs[i], me), sent), (part(outs[i], pidx), arriving)):
                if into is sent or arrivals:
                    into.append(pltpu.make_async_remote_copy(src_ref=part(ins[i], pidx), dst_ref=dst, send_sem=send.at[s],
                                                             recv_sem=recv.at[s], device_id=peer, device_id_type=MESH))
    return own, sent, arriving


def _exchange_start(spans, ins, outs, send, recv, local):
    own, sent, _ = _scatter_copies(spans, ins, outs, send, recv, local, arrivals=False)
    for cp in own + sent:
        cp.start()


def _exchange_wait(spans, ins, outs, send, recv, local):
    own, sent, arriving = _scatter_copies(spans, ins, outs, send, recv, local, arrivals=True)
    for cp in sent:
        cp.wait_send()
    for cp in arriving:
        cp.wait_recv()
    for cp in own:
        cp.wait()


def _two_level(phase, ins, outs, send, recv, local):
    n = len(ins)
    x, y, c = lax.axis_index("x"), lax.axis_index("y"), lax.axis_index("c")
    here, sibling = (x, y, c), (x, y, 1 - c)
    chips = [(1 - x, y), (x, 1 - y), (1 - x, 1 - y)]

    def slot(px, py, pc):
        return 4 * px + 2 * py + pc

    def copy(i, k, block, to, src=None):
        return pltpu.make_async_remote_copy(src_ref=outs[i].at[block] if src is None else src, dst_ref=outs[i].at[block],
                                            send_sem=send.at[7 * i + k], recv_sem=recv.at[7 * i + k],
                                            device_id=to, device_id_type=MESH)

    me = slot(x, y, c)
    if phase == "pass":
        for i in range(n):
            for j, chip in enumerate(chips):
                copy(i, 1 + j, slot(*chip, c), here).wait_recv()
                copy(i, 4 + j, slot(*chip, c), sibling).start()
        return
    own = [pltpu.make_async_copy(ins[i], outs[i].at[me], local.at[i]) for i in range(n)]
    first = [copy(i, 0, me, sibling, src=ins[i]) for i in range(n)]
    first += [copy(i, 1 + j, me, (*chip, c), src=ins[i]) for i in range(n) for j, chip in enumerate(chips)]
    if phase == "start":
        for cp in own + first:
            cp.start()
        return
    for i in range(n):
        copy(i, 0, slot(x, y, 1 - c), here).wait_recv()
        for j, chip in enumerate(chips):
            copy(i, 4 + j, slot(*chip, 1 - c), here).wait_recv()
    passed = [copy(i, 4 + j, slot(*chip, c), sibling) for i in range(n) for j, chip in enumerate(chips)]
    for cp in first + passed:
        cp.wait_send()
    for cp in own:
        cp.wait()


def _gather_two_level(tensors, name):
    n = len(tensors)

    def body(*refs):
        for phase in ("start", "pass", "wait"):
            _two_level(phase, refs[:n], refs[n:2 * n], *refs[2 * n:])

    hbm = pl.BlockSpec(memory_space=pl.ANY)
    return pl.pallas_call(body, in_specs=[hbm] * n, out_specs=[hbm] * n,
                          out_shape=[SDS((NDEV,) + t.shape, t.dtype) for t in tensors],
                          scratch_shapes=_exchange_sems(n), name=name)(*tensors)


def _all_reduce_small(v, name):
    rows = v.shape[0]

    def body(v_ref, out_ref, land, send, recv):
        me = _my_index()
        land[me] = v_ref[...]
        remote = []
        for k in range(1, NDEV):
            peer, _ = _peer(k)
            cp = pltpu.make_async_remote_copy(src_ref=v_ref, dst_ref=land.at[me], send_sem=send.at[k - 1],
                                              recv_sem=recv.at[k - 1], device_id=peer, device_id_type=MESH)
            cp.start()
            remote.append(cp)
        for cp in remote:
            cp.wait_send()
        for k in range(1, NDEV):
            peer, pidx = _peer(k)
            pltpu.make_async_remote_copy(src_ref=v_ref, dst_ref=land.at[pidx], send_sem=send.at[k - 1],
                                         recv_sem=recv.at[k - 1], device_id=peer, device_id_type=MESH).wait_recv()
        total = land[0]
        for d in range(1, NDEV):
            total = total + land[d]
        out_ref[...] = total

    vm = pl.BlockSpec(memory_space=pltpu.VMEM)
    return pl.pallas_call(
        body, in_specs=[vm], out_specs=vm, out_shape=SDS((rows, 128), F32),
        scratch_shapes=[pltpu.VMEM((NDEV, rows, 128), F32), pltpu.SemaphoreType.DMA((NDEV - 1,)),
                        pltpu.SemaphoreType.DMA((NDEV - 1,))],
        name=name)(v)


def _adamw_math(w, g, m, v):
    m = ADAM_B1 * m + (1.0 - ADAM_B1) * g
    v = ADAM_B2 * v + (1.0 - ADAM_B2) * (g * g)
    m_hat = m / (1.0 - ADAM_B1 ** ADAM_STEP)
    v_hat = v / (1.0 - ADAM_B2 ** ADAM_STEP)
    delta = -ADAM_LR * (m_hat / (jnp.sqrt(v_hat) + ADAM_EPS) + ADAM_WD * w)
    return delta, m, v


def _row_tile(rows, cols):
    tr = rows
    while tr * cols * 4 > (1 << 20) and tr % 16 == 0:
        tr //= 2
    return tr


def _adamw(g, w, m, v, name):
    rows, cols = w.shape
    tr = _row_tile(rows, cols)

    def body(g_ref, w_ref, m_ref, v_ref, d_out, m_out, v_out):
        d, mn, vn = _adamw_math(w_ref[...], g_ref[...], m_ref[...], v_ref[...])
        d_out[...] = d
        m_out[...] = mn
        v_out[...] = vn

    sp = pl.BlockSpec((tr, cols), lambda i: (i, 0))
    return pl.pallas_call(body, grid=(rows // tr,), in_specs=[sp] * 4, out_specs=[sp] * 3,
                          out_shape=[SDS((rows, cols), F32)] * 3, name=name, compiler_params=_params(("parallel",)))(g, w, m, v)


def _reduce_adamw(parts, w, m, v, name):
    rows, cols = w.shape
    tr = _row_tile(math.gcd(*[n for _, _, n in parts]), cols)
    tiles = [n // tr for _, _, n in parts]
    starts = [r0 // tr for _, r0, _ in parts]
    assert all(r0 % tr == 0 for _, r0, _ in parts) and sum(tiles) == rows // tr

    def body(*refs):
        p_refs = refs[:len(parts)]
        w_ref, m_ref, v_ref, g_out, d_out, m_out, v_out = refs[len(parts):]
        i = pl.program_id(0)
        for h, p_ref in enumerate(p_refs):
            @pl.when((i >= starts[h]) & (i < starts[h] + tiles[h]))
            def _(p_ref=p_ref):
                g = p_ref[0].astype(F32)
                for d in range(1, NDEV):
                    g = g + p_ref[d].astype(F32)
                g_out[...] = g
                dl, mn, vn = _adamw_math(w_ref[...], g, m_ref[...], v_ref[...])
                d_out[...] = dl
                m_out[...] = mn
                v_out[...] = vn

    sp = pl.BlockSpec((tr, cols), lambda i: (i, 0))
    psp = [pl.BlockSpec((NDEV, tr, cols), lambda i, h=h: (0, jnp.clip(i, starts[h], starts[h] + tiles[h] - 1), 0))
           for h in range(len(parts))]
    return pl.pallas_call(body, grid=(rows // tr,), in_specs=psp + [sp, sp, sp], out_specs=[sp] * 4,
                          out_shape=[SDS((rows, cols), F32)] * 4, name=name,
                          compiler_params=_params(("parallel",)))(*[p for p, _, _ in parts], w, m, v)


def _pack(items):
    rows = []
    for a in items:
        a = a.reshape(-1).astype(F32)
        pad = (-a.shape[0]) % 128
        rows.append(jnp.pad(a, (0, pad)).reshape(-1, 128))
    out = jnp.concatenate(rows, axis=0)
    return jnp.pad(out, ((0, (-out.shape[0]) % 8), (0, 0)))


def _unpack(packed, shapes):
    out, r = [], 0
    for shp in shapes:
        n = math.prod(shp)
        nr = -(-n // 128)
        out.append(packed[r:r + nr].reshape(-1)[:n].reshape(shp))
        r += nr
    return out


def _ident(t):
    return (t,)


def _add(t, res):
    return (t + res,)


def kernel(x, norm_mix, norm_mlp, ssm_w_in, ssm_conv_w, ssm_conv_b, ssm_dt_bias, ssm_a_log, ssm_d, ssm_norm_w, ssm_w_out, attn_w_qkv, attn_w_o, mlp_w1, mlp_w2, final_norm, loss_target, m_norm_mix, m_norm_mlp, m_ssm_w_in, m_ssm_conv_w, m_ssm_conv_b, m_ssm_dt_bias, m_ssm_a_log, m_ssm_d, m_ssm_norm_w, m_ssm_w_out, m_attn_w_qkv, m_attn_w_o, m_mlp_w1, m_mlp_w2, m_final_norm, v_norm_mix, v_norm_mlp, v_ssm_w_in, v_ssm_conv_w, v_ssm_conv_b, v_ssm_dt_bias, v_ssm_a_log, v_ssm_d, v_ssm_norm_w, v_ssm_w_out, v_attn_w_qkv, v_attn_w_o, v_mlp_w1, v_mlp_w2, v_final_norm):
    t = x.shape[1]
    x0 = x.reshape(t, D_MODEL)
    tgt = loss_target.reshape(t, D_MODEL)
    me = _my_index()
    in_dim = D_INNER + CONV_DIM + SSM_HEADS
    in_shard = in_dim // NDEV
    zx_dim = D_INNER + CONV_DIM

    s_out, s_qkv, s_o = ssm_w_out[0].astype(BF16), attn_w_qkv[0].astype(BF16), attn_w_o[0].astype(BF16)
    s_w1, s_w2 = mlp_w1.astype(BF16), mlp_w2.astype(BF16)
    g_in, g_cw = _gather_two_level([ssm_w_in[0].astype(BF16), ssm_conv_w[0]], "gather_in_proj")
    w_in = jnp.transpose(g_in, (1, 0, 2)).reshape(D_MODEL, in_dim)
    w_z, w_x = w_in[:, :D_INNER], w_in[:, D_INNER:zx_dim]
    w_dt = jnp.pad(w_in[:, zx_dim:], ((0, 0), (0, 128 - SSM_HEADS)))
    conv_w = jnp.transpose(g_cw, (1, 0, 2)).reshape(CONV_WIDTH, CONV_DIM)
    g_w1, g_w2 = [None, None], [None, None]

    def lanes(p):
        return jnp.pad(p.reshape(SSM_GROUPS, 1, 8), ((0, 0), (0, 0), (0, 120)))

    def rows(p):
        return jnp.broadcast_to(p.reshape(SSM_HEADS, 1), (SSM_HEADS, 128))

    bias_g, bias_r = lanes(ssm_dt_bias[0]), rows(ssm_dt_bias[0])
    alog_g, alog_r = lanes(ssm_a_log[0]), rows(ssm_a_log[0])
    d_exp = jnp.repeat(ssm_d[0], SSM_HEAD_DIM).reshape(1, D_INNER)
    norm_w = ssm_norm_w

    def relu2(tot):
        r = jnp.maximum(tot, 0.0)
        return r, r * r

    def mlp_fwd(xin, layer, tag, down_carry):
        h = _rmsnorm_fwd(xin, norm_mlp[layer:layer + 1], f"norm_mlp{tag}")
        r, a, w2 = _mm_fwd(h, g_w1[layer], "cols", D_MODEL, D_FF, epi=relu2, outs=(BF16, BF16),
                           carry=("gather", [s_w2[layer]]), name=f"mlp_up{tag}")
        g_w2[layer] = w2.reshape(D_FF, D_MODEL)
        xout, *got = _mm_fwd(a, g_w2[layer], "plain", D_FF, D_MODEL, epi=_add, outs=(F32,), extras=(xin,),
                             carry=("gather", down_carry) if down_carry else None, name=f"mlp_down{tag}")
        return h, r, a, xout, got

    h0 = _rmsnorm_fwd(x0, norm_mix[0:1], "norm_mix0")
    z, g_out = _mm_fwd(h0, w_z, "plain", D_MODEL, D_INNER, epi=_ident, outs=(F32,), carry=("gather", [s_out]), name="ssm_in_z")
    xpre, g_w1[0] = _mm_fwd(h0, w_x, "plain", D_MODEL, CONV_DIM, epi=_ident, outs=(F32,), carry=("gather", [s_w1[0]]),
                            name="ssm_in_x")
    dtp, = _mm_fwd(h0, w_dt, "plain", D_MODEL, 128, epi=_ident, outs=(F32,), name="ssm_in_dt")
    xbc = _conv_fwd(xpre, conv_w, ssm_conv_b, "conv_fwd")
    dtp64 = dtp[:, :SSM_HEADS]
    dtp_g = jnp.pad(jnp.transpose(dtp64.reshape(t, SSM_GROUPS, 8), (1, 0, 2)), ((0, 0), (0, 0), (0, 120)))
    dtp_t = jnp.transpose(dtp64)
    y_ssd, yn, states = _ssd_fwd(xbc, z, dtp_g, dtp_t, bias_g, bias_r, alog_g, alog_r, d_exp, norm_w, "ssd_fwd")
    g_out = g_out.reshape(D_INNER, D_MODEL)
    x1, g_o = _mm_fwd(yn, g_out, "plain", D_INNER, D_MODEL, epi=_add, outs=(F32,), extras=(x0,), carry=("gather", [s_o]),
                      name="ssm_out")
    h1, r1, a1, x2, (g_qkv,) = mlp_fwd(x1, 0, "0", [s_qkv])

    h2 = _rmsnorm_fwd(x2, norm_mix[1:2], "norm_mix1")
    w_qkv = _pair_major(jnp.transpose(g_qkv, (1, 0, 2)).reshape(D_MODEL, QKV_DIM))
    qkv, g_w1[1] = _mm_fwd(h2, w_qkv, "plain", D_MODEL, QKV_DIM, epi=_ident, outs=(F32,),
                           carry=("gather", [s_w1[1]]), name="attn_qkv")
    att = [_attn_fwd(qkv, g, f"attn_fwd{g}") for g in range(3)]
    os_, ls_ = [a[0] for a in att], [a[1] for a in att]
    o_mix = _combine_fwd(os_, ls_, "attn_combine")
    x3, = _mm_fwd(o_mix, g_o, "cols", ATTN_W, D_MODEL, epi=_add, outs=(F32,), extras=(x2,), tn=D_MODEL // NDEV, name="attn_out")
    h3, r3, a3, x4, _ = mlp_fwd(x3, 1, "1", None)

    dx4, dx4b, loss_acc, d_final = _loss_head(x4, tgt, final_norm.reshape(1, D_MODEL), "loss_head")

    def mlp_bwd(xin, h, r, a, dxo, dxob, layer, tag):
        du, = _mm_dx(dxob, g_w2[layer], "plain", D_FF, D_MODEL, epi=lambda tot, rr: (tot * (2.0 * rr.astype(F32)),),
                     outs=(BF16,), extras=(r,), name=f"mlp_down_dx{tag}")
        dw2, = _mm_dw(a, dxob, "plain", name=f"mlp_down_dw{tag}")
        dw2 = dw2.reshape(NDEV, D_FF // NDEV, D_MODEL)
        dw1, p_dw2 = _mm_dw(h, du, "cols", carry=("scatter", [dw2], [W2_SPANS[0]]), name=f"mlp_up_dw{tag}")
        dh, p_dw1 = _mm_dx(du, g_w1[layer], "cols", D_MODEL, D_FF, epi=_ident, outs=(F32,),
                           carry=("scatter", [dw1], [W1_SPANS[0]]), name=f"mlp_up_dx{tag}")
        dxi, dxib, dg = _rmsnorm_bwd(xin, norm_mlp[layer:layer + 1], dh, dxo, f"norm_mlp_bwd{tag}")
        return dxi, dxib, dg, (dw1, dw2), (p_dw1, p_dw2)

    W1_SPANS = [(0, 7 * D_MODEL // 8), (7 * D_MODEL // 8, D_MODEL // 8)]
    W2_SPANS = [(0, 7 * D_FF // NDEV // 8), (7 * D_FF // NDEV // 8, D_FF // NDEV // 8)]

    def mlp_parts(first, rest):
        return [[(first[0],) + W1_SPANS[0], (rest[0],) + W1_SPANS[1]], [(first[1],) + W2_SPANS[0], (rest[1],) + W2_SPANS[1]]]

    dx3, dx3b, dg_mlp1, left1, got1 = mlp_bwd(x3, h3, r3, a3, dx4, dx4b, 1, "1")

    dw_o, *rest1 = _mm_dw(o_mix, dx3b, "cols", tn=D_MODEL // NDEV, carry=("scatter", list(left1), [W1_SPANS[1], W2_SPANS[1]]),
                          name="attn_out_dw")
    p_w1_1, p_w2_1 = mlp_parts(got1, rest1)
    do, p_o = _mm_dx(dx3b, g_o, "cols", ATTN_W, D_MODEL, epi=_ident, outs=(F32,), tk=D_MODEL // NDEV,
                     carry=("scatter", [dw_o]), name="attn_out_dx")
    cb = _combine_bwd(do, os_, ls_, "attn_combine_bwd")
    dos, dvecs = cb[:3], cb[3:]
    dqkv = None
    for g in range(3):
        dqkv = _attn_bwd(qkv, dos[g], ls_[g], dvecs[g], dqkv, g, f"attn_bwd{g}")
    dw_qkv, = _mm_dw(h2, dqkv, "plain", tk=1024, name="attn_qkv_dw")
    dw_qkv = jnp.transpose(_pair_major(dw_qkv, inverse=True).reshape(D_MODEL, NDEV, QKV_DIM // NDEV), (1, 0, 2))
    dh2, p_qkv = _mm_dx(dqkv, w_qkv, "plain", D_MODEL, QKV_DIM, epi=_ident, outs=(F32,), tk=1536,
                        carry=("scatter", [dw_qkv]), name="attn_qkv_dx")
    dx2, dx2b, dg_mix1 = _rmsnorm_bwd(x2, norm_mix[1:2], dh2, dx3, "norm_mix_bwd1")

    dx1, dx1b, dg_mlp0, left0, got0 = mlp_bwd(x1, h1, r1, a1, dx2, dx2b, 0, "0")

    dw_out, *rest0 = _mm_dw(yn, dx1b, "plain", carry=("scatter", list(left0), [W1_SPANS[1], W2_SPANS[1]]), name="ssm_out_dw")
    p_w1_0, p_w2_0 = mlp_parts(got0, rest0)
    dw_out = dw_out.reshape(NDEV, D_INNER // NDEV, D_MODEL)
    dyn, p_out = _mm_dx(dx1b, g_out, "plain", D_INNER, D_MODEL, epi=_ident, outs=(F32,), carry=("scatter", [dw_out]),
                        name="ssm_out_dx")
    dxs, d_b, d_c, dz, ddtp_g, hsums, csums = _ssd_bwd(dyn, y_ssd, xbc, z, states, dtp_g, dtp_t, bias_g, bias_r,
                                                       alog_g, alog_r, d_exp, norm_w, "ssd_bwd")
    du, sums = None, []
    for part, col0, tag in ((dxs, 0, "x"), (d_b, D_INNER, "b"), (d_c, D_INNER + SSM_GROUPS * SSM_STATE, "c")):
        du, s = _conv_bwd(xpre, conv_w, ssm_conv_b, part, du, col0, f"conv_bwd_{tag}")
        sums.append(s)
    conv_sums = jnp.concatenate(sums, axis=1)
    ddtp = jnp.transpose(ddtp_g[:, :, :8], (1, 0, 2)).reshape(t, SSM_HEADS)
    ddtp = jnp.pad(ddtp, ((0, 0), (0, 128 - SSM_HEADS))).astype(BF16)
    dw_z, = _mm_dw(h0, dz, "plain", name="ssm_in_z_dw")
    dw_x, = _mm_dw(h0, du, "plain", name="ssm_in_x_dw")
    dw_dt, = _mm_dw(h0, ddtp, "plain", name="ssm_in_dt_dw")
    half = D_MODEL // 2
    dw_in = jnp.concatenate([dw_z, dw_x, dw_dt[:, :SSM_HEADS]], axis=1)
    dw_in = jnp.transpose(dw_in.reshape(D_MODEL, NDEV, in_shard), (1, 0, 2))
    dh0, = _mm_dx(ddtp, w_dt, "plain", D_MODEL, 128, epi=_ident, outs=(F32,), name="ssm_in_dt_dx")
    dh0, p_in_a = _mm_dx(dz, w_z, "plain", D_MODEL, D_INNER, epi=_add, outs=(F32,), extras=(dh0,),
                         carry=("scatter", [dw_in], [(0, half)]), name="ssm_in_z_dx")
    dh0, p_in_b = _mm_dx(du, w_x, "plain", D_MODEL, CONV_DIM, epi=_add, outs=(F32,), extras=(dh0,),
                         carry=("scatter", [dw_in], [(half, half)]), name="ssm_in_x_dx")
    dx0, _, dg_mix0 = _rmsnorm_bwd(x0, norm_mix[0:1], dh0, dx1, "norm_mix_bwd0")

    def whole(p):
        return [(p, 0, p.shape[1])]

    parts = [[(p_in_a, 0, half), (p_in_b, half, half)], whole(p_out), whole(p_qkv), whole(p_o),
             p_w1_0, p_w1_1, p_w2_0, p_w2_1]

    def own(w, mm, vv):
        shp = w.shape
        f = lambda a: a.reshape(-1, shp[-1])
        return f(w), f(mm), f(vv), shp

    big = {}
    for key, part, (w, mm, vv) in (
            ("ssm_w_in", parts[0], (ssm_w_in, m_ssm_w_in, v_ssm_w_in)),
            ("ssm_w_out", parts[1], (ssm_w_out, m_ssm_w_out, v_ssm_w_out)),
            ("attn_w_qkv", parts[2], (attn_w_qkv, m_attn_w_qkv, v_attn_w_qkv)),
            ("attn_w_o", parts[3], (attn_w_o, m_attn_w_o, v_attn_w_o))):
        w2, m2, v2, shp = own(w, mm, vv)
        res = _reduce_adamw(part, w2, m2, v2, f"adamw_{key}")
        big[key] = [r.reshape(shp) for r in res]
    for key, pa, pb, (w, mm, vv) in (("mlp_w1", parts[4], parts[5], (mlp_w1, m_mlp_w1, v_mlp_w1)),
                                     ("mlp_w2", parts[6], parts[7], (mlp_w2, m_mlp_w2, v_mlp_w2))):
        res = [_reduce_adamw(p, w[l], mm[l], vv[l], f"adamw_{key}_{l}") for l, p in enumerate((pa, pb))]
        big[key] = [jnp.stack([res[0][i], res[1][i]], axis=0) for i in range(4)]

    d_norm_mix = jnp.concatenate([dg_mix0, dg_mix1], axis=0)
    d_norm_mlp = jnp.concatenate([dg_mlp0, dg_mlp1], axis=0)
    d_conv_b = conv_sums[4:5]
    d_conv_w = conv_sums[0:4]
    head = hsums[:, :, :8]
    d_dt_bias, d_a_log, d_d = (head[:, k, :].reshape(1, SSM_HEADS) for k in range(3))
    d_ssm_norm = csums[0:1]
    small = [d_norm_mix, d_norm_mlp, d_conv_b, d_dt_bias, d_a_log, d_d, d_ssm_norm, d_final, d_conv_w, loss_acc[0:1, 0:1]]
    shapes = [a.shape for a in small]
    summed = _unpack(_all_reduce_small(_pack(small), "reduce_small"), shapes)
    g_conv_w_full = summed[8]
    loss = summed[9].reshape(())
    g_conv_w = lax.dynamic_slice(g_conv_w_full, (0, me * (CONV_DIM // NDEV)), (CONV_WIDTH, CONV_DIM // NDEV))

    small_names = ["norm_mix", "norm_mlp", "ssm_conv_b", "ssm_dt_bias", "ssm_a_log", "ssm_d", "ssm_norm_w", "final_norm"]
    small_w = [norm_mix, norm_mlp, ssm_conv_b, ssm_dt_bias, ssm_a_log, ssm_d, ssm_norm_w, final_norm, ssm_conv_w]
    small_m = [m_norm_mix, m_norm_mlp, m_ssm_conv_b, m_ssm_dt_bias, m_ssm_a_log, m_ssm_d, m_ssm_norm_w, m_final_norm, m_ssm_conv_w]
    small_v = [v_norm_mix, v_norm_mlp, v_ssm_conv_b, v_ssm_dt_bias, v_ssm_a_log, v_ssm_d, v_ssm_norm_w, v_final_norm, v_ssm_conv_w]
    small_g = [summed[i].reshape(small_w[i].shape) for i in range(8)] + [g_conv_w.reshape(ssm_conv_w.shape)]
    wshapes = [a.shape for a in small_w]
    sd, sm, sv = _adamw(_pack(small_g), _pack(small_w), _pack(small_m), _pack(small_v), "adamw_small")
    sd, sm, sv = _unpack(sd, wshapes), _unpack(sm, wshapes), _unpack(sv, wshapes)
    res = {n: (small_g[i], sd[i], sm[i], sv[i]) for i, n in enumerate(small_names + ["ssm_conv_w"])}
    for n in big:
        res[n] = tuple(big[n])

    order = ["norm_mix", "norm_mlp", "ssm_w_in", "ssm_conv_w", "ssm_conv_b", "ssm_dt_bias", "ssm_a_log", "ssm_d",
             "ssm_norm_w", "ssm_w_out", "attn_w_qkv", "attn_w_o", "mlp_w1", "mlp_w2", "final_norm"]
    outs = [loss, dx0.reshape(x.shape)]
    for kind in range(4):
        outs += [res[n][kind] for n in order]
    return tuple(outs)
```
